```python
import math
import jax, jax.numpy as jnp
from jax import lax
import numpy as np

D_MODEL = 1024
BATCH = 8
SEQ = 2048
DEPTH = 1

ATT_HEADS = 8
ATT_KV_HEADS = 2
HEAD_DIM = 64
WINDOW = 128
BLOCK = 128
ATT_Q = ATT_HEADS * HEAD_DIM
ATT_KV = ATT_KV_HEADS * HEAD_DIM
GLA_HEADS = 4
GLA_DK = 64
GLA_DV = 128
GLA_RANK = 16
GLA_NORMALIZER = 16.0
GLA_CHUNK = 16
GLA_K = GLA_HEADS * GLA_DK
GLA_V = GLA_HEADS * GLA_DV
MIX_WIDTH = ATT_Q + GLA_V
IN_COLS = ATT_Q + 2 * ATT_KV + 2 * GLA_K + GLA_V + GLA_RANK + GLA_V
N_GROUPS = 4
EXPERTS_PER_GROUP = 4
N_EXPERTS = N_GROUPS * EXPERTS_PER_GROUP
TOP_K = 2
D_EXPERT = 256
EPS = 1e-6

kernel_name = "hymba_swa_sink_gla_hier_moe_adaln"


def rms_norm(x, g):
    xf = x.astype(jnp.float32)
    y = xf * lax.rsqrt(jnp.mean(xf * xf, axis=-1, keepdims=True) + EPS)
    return (y * g.astype(jnp.float32)).astype(x.dtype)


def alibi_slopes(n_heads):
    h = jnp.arange(1, n_heads + 1, dtype=jnp.float32)
    return jnp.exp2(-8.0 * h / n_heads)


def sliding_window_attention(q, k, v, sinks):
    B, T, Hq, hd = q.shape
    Hkv = k.shape[2]
    G = Hq // Hkv
    nb = T // BLOCK
    f32 = jnp.float32
    qb = q.reshape(B, nb, BLOCK, Hkv, G, hd)

    def banded(t):
        prev = jnp.concatenate([jnp.zeros_like(t[:, :BLOCK]), t[:, :-BLOCK]], axis=1)
        return jnp.concatenate([prev.reshape(B, nb, BLOCK, Hkv, hd),
                                t.reshape(B, nb, BLOCK, Hkv, hd)], axis=2)

    kb, vb = banded(k), banded(v)
    s = jnp.einsum('bnqhgd,bnkhd->bnhgqk', qb, kb).astype(f32) * (hd ** -0.5)
    qi = jnp.arange(BLOCK)[:, None]
    kj = jnp.arange(2 * BLOCK)[None, :]
    dist = qi - kj + BLOCK
    blk = jnp.arange(nb)[:, None, None]
    valid = (dist >= 0) & (dist < WINDOW) & (blk * BLOCK + kj - BLOCK >= 0)
    slopes = alibi_slopes(Hq).reshape(Hkv, G)[:, :, None, None]
    s = s - slopes * dist.astype(f32)
    s = jnp.where(valid[None, :, None, None], s, -jnp.inf)
    sk = sinks.astype(f32).reshape(Hkv, G)[:, :, None, None]
    m = jnp.maximum(jnp.max(s, axis=-1, keepdims=True), sk)
    p = jnp.exp(s - m)
    p = p / (jnp.sum(p, axis=-1, keepdims=True) + jnp.exp(sk - m))
    o = jnp.einsum('bnhgqk,bnkhd->bnqhgd', p.astype(v.dtype), vb)
    return o.reshape(B, T, Hq * hd)


def gla_chunked(q, k, v, log_a):
    B, T, H, dk = q.shape
    dv = v.shape[-1]
    C = GLA_CHUNK
    n = T // C
    f32 = jnp.float32
    q = q.astype(f32).reshape(B, n, C, H, dk) * (dk ** -0.5)
    k = k.astype(f32).reshape(B, n, C, H, dk)
    v = v.astype(f32).reshape(B, n, C, H, dv)
    b = jnp.cumsum(log_a.astype(f32).reshape(B, n, C, H, dk), axis=2)
    causal = jnp.tril(jnp.ones((C, C), dtype=bool))
    diff = b[:, :, :, None] - b[:, :, None, :]
    decay = jnp.where(causal[:, :, None, None], jnp.exp(jnp.minimum(diff, 0.0)), 0.0)
    attn = jnp.einsum('bnijhd,bnjhd->bnhij', q[:, :, :, None] * decay, k)
    o_intra = jnp.einsum('bnhij,bnjhe->bnihe', attn, v)
    b_last = b[:, :, -1]
    q_dec = q * jnp.exp(b)
    k_dec = k * jnp.exp(b_last[:, :, None] - b)
    u = jnp.einsum('bnjhd,bnjhe->bnhde', k_dec, v)

    def step(S, inp):
        a, du = inp
        return a[..., None] * S + du, S

    S0 = jnp.zeros((B, H, dk, dv), f32)
    _, S_prev = lax.scan(step, S0, (jnp.moveaxis(jnp.exp(b_last), 1, 0), jnp.moveaxis(u, 1, 0)))
    S_prev = jnp.moveaxis(S_prev, 0, 1)
    o_inter = jnp.einsum('bnihd,bnhde->bnihe', q_dec, S_prev)
    return (o_intra + o_inter).reshape(B, T, H, dv)


def hier_moe(h, w_group, b_group, w_router, b_router, w1, w3, w2):
    B, T, D = h.shape
    N = B * T
    hf = h.reshape(N, D)
    g_logits = (hf @ w_group + b_group).astype(jnp.float32)
    g_prob = jax.nn.softmax(g_logits, axis=-1)
    g_sel = jnp.argmax(g_logits, axis=-1)
    p_group = jnp.take_along_axis(g_prob, g_sel[:, None], axis=-1)
    e_logits = (hf @ w_router + b_router).astype(jnp.float32).reshape(N, N_GROUPS, EXPERTS_PER_GROUP)
    e_logits = jnp.take_along_axis(e_logits, g_sel[:, None, None], axis=1)[:, 0]
    e_prob = jax.nn.softmax(e_logits, axis=-1)
    top_v, top_i = lax.top_k(e_prob, TOP_K)
    top_v = top_v / jnp.sum(top_v, axis=-1, keepdims=True)
    expert_idx = g_sel[:, None] * EXPERTS_PER_GROUP + top_i
    weights = p_group * top_v
    combine = jnp.sum(jax.nn.one_hot(expert_idx, N_EXPERTS, dtype=jnp.float32) * weights[..., None], axis=1)
    a = jnp.einsum('nd,edf->nef', hf, w1)
    g = jnp.einsum('nd,edf->nef', hf, w3)
    hid = jax.nn.silu(a) * g * combine.astype(hf.dtype)[..., None]
    y = jnp.einsum('nef,efd->nd', hid, w2)
    return y.reshape(B, T, D)


def setup_inputs(seed: int = 0) -> dict:
    key = jax.random.key(seed)
    ks = jax.random.split(key, 24)
    D, L = D_MODEL, DEPTH
    nrm = lambda k, shape, s: jax.random.normal(k, shape, jnp.float32) * s
    return {
        "x": nrm(ks[0], (BATCH, SEQ, D), 1.0),
        "c": nrm(ks[1], (BATCH, D), 1.0),
        "w_ada": nrm(ks[2], (L, D, 6 * D), D ** -0.5),
        "b_ada": nrm(ks[3], (L, 6 * D), 0.02),
        "g_norm1": 1.0 + nrm(ks[4], (L, D), 0.02),
        "w_in": nrm(ks[5], (L, D, IN_COLS), D ** -0.5),
        "q_norm": 1.0 + nrm(ks[6], (L, HEAD_DIM), 0.02),
        "k_norm": 1.0 + nrm(ks[7], (L, HEAD_DIM), 0.02),
        "sinks": nrm(ks[8], (L, ATT_HEADS), 1.0),
        "w_gk2": nrm(ks[9], (L, GLA_RANK, GLA_K), GLA_RANK ** -0.5),
        "b_gk": nrm(ks[10], (L, GLA_K), 0.1),
        "g_gla_out": 1.0 + nrm(ks[11], (L, GLA_DV), 0.02),
        "g_att_out": 1.0 + nrm(ks[12], (L, ATT_Q), 0.02),
        "w_out": nrm(ks[13], (L, MIX_WIDTH, D), MIX_WIDTH ** -0.5),
        "g_norm2": 1.0 + nrm(ks[14], (L, D), 0.02),
        "w_group": nrm(ks[15], (L, D, N_GROUPS), D ** -0.5),
        "b_group": nrm(ks[16], (L, N_GROUPS), 0.01),
        "w_router": nrm(ks[17], (L, D, N_EXPERTS), D ** -0.5),
        "b_router": nrm(ks[18], (L, N_EXPERTS), 0.01),
        "w1": nrm(ks[19], (L, N_EXPERTS, D, D_EXPERT), D ** -0.5),
        "w3": nrm(ks[20], (L, N_EXPERTS, D, D_EXPERT), D ** -0.5),
        "w2": nrm(ks[21], (L, N_EXPERTS, D_EXPERT, D), D_EXPERT ** -0.5),
    }


def reference(x, c, w_ada, b_ada, g_norm1, w_in, q_norm, k_norm, sinks, w_gk2, b_gk,
              g_gla_out, g_att_out, w_out, g_norm2, w_group, b_group, w_router, b_router,
              w1, w3, w2):
    B, T, D = x.shape
    split_points = list(np.cumsum([ATT_Q, ATT_KV, ATT_KV, GLA_K, GLA_K, GLA_V, GLA_RANK]))
    for l in range(DEPTH):
        mod = (jax.nn.silu(c) @ w_ada[l] + b_ada[l]).reshape(B, 6, D)[:, :, None, :]
        shift1, scale1, gate1, shift2, scale2, gate2 = [mod[:, i] for i in range(6)]

        h = rms_norm(x, g_norm1[l]) * (1.0 + scale1) + shift1
        proj = h @ w_in[l]
        q_a, k_a, v_a, q_g, k_g, v_g, lr_g, og_g = jnp.split(proj, split_points, axis=-1)
        q_a = rms_norm(q_a.reshape(B, T, ATT_HEADS, HEAD_DIM), q_norm[l])
        k_a = rms_norm(k_a.reshape(B, T, ATT_KV_HEADS, HEAD_DIM), k_norm[l])
        v_a = v_a.reshape(B, T, ATT_KV_HEADS, HEAD_DIM)
        y_att = rms_norm(sliding_window_attention(q_a, k_a, v_a, sinks[l]), g_att_out[l])
        log_a = jax.nn.log_sigmoid((lr_g @ w_gk2[l] + b_gk[l]).astype(jnp.float32)) / GLA_NORMALIZER
        o_g = gla_chunked(q_g.reshape(B, T, GLA_HEADS, GLA_DK),
                          k_g.reshape(B, T, GLA_HEADS, GLA_DK),
                          v_g.reshape(B, T, GLA_HEADS, GLA_DV),
                          log_a.reshape(B, T, GLA_HEADS, GLA_DK)).astype(x.dtype)
        o_g = rms_norm(o_g, g_gla_out[l]).reshape(B, T, GLA_V)
        y_gla = o_g * jax.nn.silu(og_g)
        mix = jnp.concatenate([y_att, y_gla], axis=-1) @ w_out[l]
        x = x + gate1 * mix

        h = rms_norm(x, g_norm2[l]) * (1.0 + scale2) + shift2
        y = hier_moe(h, w_group[l], b_group[l], w_router[l], b_router[l], w1[l], w3[l], w2[l])
        x = x + gate2 * y
    return x
```

```python
import functools

import jax
import jax.numpy as jnp
from jax import lax
from jax.experimental import pallas as pl
from jax.experimental.pallas import tpu as pltpu

F32 = jnp.float32
BF16 = jnp.bfloat16

EPS = 1e-6
ATT_HEADS = 8
ATT_KV_HEADS = 2
HEAD_DIM = 64
WINDOW = 128
ATT_Q = ATT_HEADS * HEAD_DIM
ATT_KV = ATT_KV_HEADS * HEAD_DIM
GLA_HEADS = 4
GLA_DK = 64
GLA_DV = 128
GLA_RANK = 16
GLA_NORMALIZER = 16.0
GLA_K = GLA_HEADS * GLA_DK
GLA_V = GLA_HEADS * GLA_DV
N_GROUPS = 4
EXPERTS_PER_GROUP = 4
N_EXPERTS = N_GROUPS * EXPERTS_PER_GROUP
D_EXPERT = 256

LANES = 128
GLA_CHUNK = 128
GLA_SUB = 16
N_SUB = GLA_CHUNK // GLA_SUB
ROUTER_LANE0 = N_GROUPS
VMEM_LIMIT = 56 * 1024 * 1024

_QA0, _KA0, _VA0 = 0, ATT_Q, ATT_Q + ATT_KV
_QG0 = _VA0 + ATT_KV
_KG0 = _QG0 + GLA_K
_VG0 = _KG0 + GLA_K
_OG0 = _VG0 + GLA_V
_LR0 = _OG0 + GLA_V
IN_COLS_PAD = _LR0 + LANES


def _dot(a, b):
    return jnp.dot(a, b, preferred_element_type=F32)


def _dot_nt(a, b):
    return lax.dot_general(a, b, (((1,), (1,)), ((), ())), preferred_element_type=F32)


def _sigmoid(x):
    return 1.0 / (1.0 + jnp.exp(-x))


def _params(*sem):
    return pltpu.CompilerParams(dimension_semantics=sem, vmem_limit_bytes=VMEM_LIMIT)


def _mod_kernel(c_ref, w_ref, b_ref, o_ref):
    c = c_ref[...]
    s = (c * _sigmoid(c)).astype(BF16)
    o_ref[...] = _dot(s, w_ref[...].astype(BF16)) + b_ref[...]


def _adaln_mod(c, w_ada, b_ada):
    B, D = c.shape
    n = w_ada.shape[1]
    tn = 1536
    return pl.pallas_call(
        _mod_kernel,
        grid=(n // tn,),
        in_specs=[pl.BlockSpec((B, D), lambda j: (0, 0)),
                  pl.BlockSpec((D, tn), lambda j: (0, j)),
                  pl.BlockSpec((1, tn), lambda j: (0, j))],
        out_specs=pl.BlockSpec((B, tn), lambda j: (0, j)),
        out_shape=jax.ShapeDtypeStruct((B, n), F32),
        compiler_params=_params("arbitrary"),
        name="adaln_mod",
    )(c, w_ada, b_ada.reshape(1, n))


def _inproj_kernel(x_ref, mod_ref, g1_ref, w_ref, qn_ref, kn_ref, bdq_ref, bdk_ref, wgk_ref, bgk_ref,
                   qa_ref, ka_ref, va_ref, qg_ref, kg_ref, vg_ref, la_ref, og_ref):
    x = x_ref[...]
    ms = jnp.mean(x * x, axis=-1, keepdims=True)
    xn = x * lax.rsqrt(ms + EPS) * g1_ref[...]
    h = xn * (1.0 + mod_ref[1:2, :]) + mod_ref[0:1, :]
    proj = _dot(h.astype(BF16), w_ref[...])

    q = proj[:, _QA0:_QA0 + ATT_Q]
    q_ms = _dot((q * q).astype(BF16), bdq_ref[...])
    qa_ref[...] = (q * lax.rsqrt(q_ms + EPS) * qn_ref[...] * (HEAD_DIM ** -0.5)).astype(BF16)

    k = proj[:, _KA0:_KA0 + ATT_KV]
    k_ms = _dot((k * k).astype(BF16), bdk_ref[...])
    k = k * lax.rsqrt(k_ms + EPS) * kn_ref[...]
    v = proj[:, _VA0:_VA0 + ATT_KV]
    low = lax.broadcasted_iota(jnp.int32, k.shape, 1) < HEAD_DIM
    for src, dst in ((k, ka_ref), (v, va_ref)):
        swapped = pltpu.roll(src, HEAD_DIM, axis=1)
        dst[:, 0:LANES] = jnp.where(low, src, swapped).astype(BF16)
        dst[:, LANES:2 * LANES] = jnp.where(low, swapped, src).astype(BF16)

    qg_ref[...] = (proj[:, _QG0:_QG0 + GLA_K] * (GLA_DK ** -0.5)).astype(BF16)
    kg_ref[...] = proj[:, _KG0:_KG0 + GLA_K].astype(BF16)
    vg_ref[...] = proj[:, _VG0:_VG0 + GLA_V].astype(BF16)
    og = proj[:, _OG0:_OG0 + GLA_V]
    og_ref[...] = (og * _sigmoid(og)).astype(BF16)

    gate = _dot(proj[:, _LR0:_LR0 + LANES].astype(BF16), wgk_ref[...]) + bgk_ref[...]
    log_sig = jnp.minimum(gate, 0.0) - jnp.log(1.0 + jnp.exp(-jnp.abs(gate)))
    la_ref[...] = log_sig * (1.0 / GLA_NORMALIZER)


def _in_proj(x2, mod, g1, w_in_p, qn, kn, bdq, bdk, wgk, bgk, *, T, tm):
    N, D = x2.shape
    per_b = T // tm
    const = lambda shape: pl.BlockSpec(shape, lambda i: (0,) * len(shape))
    rows = lambda c: pl.BlockSpec((tm, c), lambda i: (i, 0))
    outs = [(ATT_Q, BF16), (2 * ATT_KV, BF16), (2 * ATT_KV, BF16), (GLA_K, BF16), (GLA_K, BF16),
            (GLA_V, BF16), (GLA_K, F32), (GLA_V, BF16)]
    return pl.pallas_call(
        _inproj_kernel,
        grid=(N // tm,),
        in_specs=[rows(D),
                  pl.BlockSpec((None, 6, D), lambda i: (i // per_b, 0, 0)),
                  const((1, D)), const(w_in_p.shape), const(qn.shape), const(kn.shape),
                  const(bdq.shape), const(bdk.shape), const(wgk.shape), const(bgk.shape)],
        out_specs=[rows(c) for c, _ in outs],
        out_shape=[jax.ShapeDtypeStruct((N, c), dt) for c, dt in outs],
        compiler_params=_params("arbitrary"),
        name="in_proj",
    )(x2, mod, g1, w_in_p, qn, kn, bdq, bdk, wgk, bgk)


def _attn_kernel(sinks_ref, q_ref, kc_ref, vc_ref, kp_ref, vp_ref, gatt_ref, o_ref):
    blk = WINDOW
    first = pl.program_id(1) == 0
    qi = lax.broadcasted_iota(jnp.int32, (blk, blk), 0)
    cj = lax.broadcasted_iota(jnp.int32, (blk, blk), 1)
    from_prev = cj > qi
    dist = (qi - cj + jnp.where(from_prev, blk, 0)).astype(F32)
    no_prev = jnp.where(jnp.logical_and(from_prev, first), -1e30, 0.0)
    low = cj < HEAD_DIM
    half = (jnp.where(low, 1.0, 0.0).astype(BF16), jnp.where(low, 0.0, 1.0).astype(BF16))
    r2 = lax.broadcasted_iota(jnp.int32, (2 * blk, blk), 0)
    c2 = lax.broadcasted_iota(jnp.int32, (2 * blk, blk), 1)
    half_sum = jnp.where((r2 < blk) == (c2 < HEAD_DIM), 1.0, 0.0).astype(BF16)

    pairs = []
    for g in range(ATT_KV_HEADS):
        lanes = slice(g * LANES, (g + 1) * LANES)
        kc, kp, vc, vp = kc_ref[:, lanes], kp_ref[:, lanes], vc_ref[:, lanes], vp_ref[:, lanes]
        v_stack = jnp.concatenate([vp * half[0], vp * half[1], vc * half[0], vc * half[1]], axis=0)
        for jj in range(ATT_HEADS // ATT_KV_HEADS // 2):
            j = g * (ATT_HEADS // ATT_KV_HEADS // 2) + jj
            qp = q_ref[:, j * LANES:(j + 1) * LANES]
            probs, sink_terms = [], []
            for p in range(2):
                h = 2 * j + p
                slope = 2.0 ** (-8.0 * (h + 1) / ATT_HEADS)
                s = jnp.where(from_prev, _dot_nt(qp, kp * half[p]), _dot_nt(qp, kc * half[p]))
                s = s - slope * dist + no_prev
                sink = sinks_ref[h]
                m = jnp.maximum(jnp.max(s, axis=-1, keepdims=True), sink)
                probs.append(jnp.exp(s - m))
                sink_terms.append(jnp.exp(sink - m))
            p_prev = [jnp.where(from_prev, e, 0.0).astype(BF16) for e in probs]
            p_cur = [jnp.where(from_prev, 0.0, e).astype(BF16) for e in probs]
            pv = _dot(jnp.concatenate(p_prev + p_cur, axis=1), v_stack)
            den = _dot(jnp.concatenate([e.astype(BF16) for e in probs], axis=1), half_sum)
            den = den + jnp.where(low, sink_terms[0], sink_terms[1])
            pairs.append(pv / den)
    o = jnp.concatenate(pairs, axis=1)
    ms = jnp.mean(o * o, axis=-1, keepdims=True)
    o_ref[...] = (o * lax.rsqrt(ms + EPS) * gatt_ref[...]).astype(BF16)


def _swa_attention(sinks, qa, ka, va, gatt, *, B, T):
    nb = T // WINDOW
    cur = lambda c: pl.BlockSpec((WINDOW, c), lambda b, i: (b * nb + i, 0))
    prev = lambda c: pl.BlockSpec((WINDOW, c), lambda b, i: (b * nb + jnp.maximum(i - 1, 0), 0))
    return pl.pallas_call(
        _attn_kernel,
        grid=(B, nb),
        in_specs=[pl.BlockSpec(memory_space=pltpu.SMEM),
                  cur(ATT_Q), cur(2 * ATT_KV), cur(2 * ATT_KV), prev(2 * ATT_KV), prev(2 * ATT_KV),
                  pl.BlockSpec((1, ATT_Q), lambda b, i: (0, 0))],
        out_specs=cur(ATT_Q),
        out_shape=jax.ShapeDtypeStruct((B * T, ATT_Q), BF16),
        compiler_params=_params("arbitrary", "arbitrary"),
        name="swa_attn",
    )(sinks, qa, ka, va, ka, va, gatt)


def _gla_kernel(q_ref, k_ref, v_ref, la_ref, og_ref, tri_ref, bdtri_ref, qmask_ref, bdms_ref, smask_ref,
                ggla_ref, o_ref, state_ref):
    L = GLA_CHUNK

    @pl.when(pl.program_id(1) == 0)
    def _():
        state_ref[...] = jnp.zeros_like(state_ref)

    la = la_ref[...]
    la_hi = la.astype(BF16)
    la_lo = (la - la_hi.astype(F32)).astype(BF16)
    tri, bdtri = tri_ref[...], bdtri_ref[...]
    b = _dot(tri, la_hi) + _dot(tri, la_lo)
    b_in = _dot(bdtri, la_hi) + _dot(bdtri, la_lo)
    ref = b - b_in
    b_last = b[L - 1:L, :]
    q = q_ref[...].astype(F32)
    k = k_ref[...].astype(F32)
    q_in = q * jnp.exp(b_in)
    q_dec = (q * jnp.exp(b)).astype(BF16)
    k_dec = k * jnp.exp(b_last - b)

    rows = lax.broadcasted_iota(jnp.int32, (L, LANES), 0)
    causal = lax.broadcasted_iota(jnp.int32, (L, L), 0) >= lax.broadcasted_iota(jnp.int32, (L, L), 1)
    for pair in range(GLA_HEADS // 2):
        kl = slice(pair * LANES, (pair + 1) * LANES)
        vl = slice(pair * 2 * GLA_DV, (pair + 1) * 2 * GLA_DV)
        k_p, b_p, ref_p = k[:, kl], b[:, kl], ref[:, kl]
        keys = []
        for g in range(N_SUB):
            last_row = (g + 1) * GLA_SUB - 1
            expo = jnp.where(rows <= last_row, ref_p[g * GLA_SUB:g * GLA_SUB + 1, :] - b_p, -1e4)
            keys.append((k_p * jnp.exp(expo)).astype(BF16))
        keys = jnp.concatenate(keys, axis=1)
        q_rep = jnp.concatenate([q_in[:, kl].astype(BF16)] * N_SUB, axis=1)
        q_both = jnp.concatenate([q_rep * qmask_ref[0], q_rep * qmask_ref[1]], axis=0)
        scores = _dot_nt(q_both, keys)
        v_p = v_ref[:, vl]
        o_parts = []
        for hh in range(2):
            a = jnp.where(causal, scores[hh * L:(hh + 1) * L, :], 0.0).astype(BF16)
            o_parts.append(_dot(a, v_p[:, hh * GLA_DV:(hh + 1) * GLA_DV]))
        state = state_ref[pair]
        o = jnp.concatenate(o_parts, axis=1) + _dot(q_dec[:, kl], state.astype(BF16))

        update = _dot(k_dec[:, kl].T.astype(BF16), v_p) * smask_ref[...]
        decay = jnp.broadcast_to(jnp.exp(b_last[:, kl]), (LANES, LANES)).T
        state_ref[pair] = state * jnp.concatenate([decay, decay], axis=1) + update

        ms = _dot((o * o).astype(BF16), bdms_ref[...])
        y = o * lax.rsqrt(ms + EPS) * ggla_ref[:, vl] * og_ref[:, vl].astype(F32)
        o_ref[:, vl] = y.astype(BF16)


def _gla(qg, kg, vg, la, og, tri, bdtri, qmask, bdms, smask, ggla, *, B, T):
    L = GLA_CHUNK
    nc = T // L
    rows = lambda c: pl.BlockSpec((L, c), lambda b, i: (b * nc + i, 0))
    const = lambda a: pl.BlockSpec(a.shape, lambda b, i: (0,) * a.ndim)
    return pl.pallas_call(
        _gla_kernel,
        grid=(B, nc),
        in_specs=[rows(GLA_K), rows(GLA_K), rows(GLA_V), rows(GLA_K), rows(GLA_V),
                  const(tri), const(bdtri), const(qmask), const(bdms), const(smask), const(ggla)],
        out_specs=rows(GLA_V),
        out_shape=jax.ShapeDtypeStruct((B * T, GLA_V), BF16),
        scratch_shapes=[pltpu.VMEM((GLA_HEADS // 2, LANES, 2 * GLA_DV), F32)],
        compiler_params=_params("arbitrary", "arbitrary"),
        name="gla",
    )(qg, kg, vg, la, og, tri, bdtri, qmask, bdms, smask, ggla)


def _outproj_kernel(ya_ref, yg_ref, x_ref, mod_ref, wo_ref, g2_ref, wr_ref, br_ref,
                    x1_ref, h2_ref, comb_ref):
    mix = _dot(ya_ref[...], wo_ref[0:ATT_Q, :]) + _dot(yg_ref[...], wo_ref[ATT_Q:ATT_Q + GLA_V, :])
    x1 = x_ref[...] + mod_ref[2:3, :] * mix
    x1_ref[...] = x1
    ms = jnp.mean(x1 * x1, axis=-1, keepdims=True)
    h2 = (x1 * lax.rsqrt(ms + EPS) * g2_ref[...]) * (1.0 + mod_ref[4:5, :]) + mod_ref[3:4, :]
    h2b = h2.astype(BF16)
    h2_ref[...] = h2b

    logits = _dot(h2b, wr_ref[...]) + br_ref[...]
    lane = lax.broadcasted_iota(jnp.int32, logits.shape, 1)
    neg_inf = -jnp.inf
    g_log = jnp.where(lane < N_GROUPS, logits, neg_inf)
    g_max = jnp.max(g_log, axis=-1, keepdims=True)
    g_sel = jnp.min(jnp.where(g_log == g_max, lane, LANES), axis=-1, keepdims=True)
    g_sum = jnp.sum(jnp.where(lane < N_GROUPS, jnp.exp(logits - g_max), 0.0), axis=-1, keepdims=True)
    p_group = 1.0 / g_sum
    e_lo = ROUTER_LANE0 + EXPERTS_PER_GROUP * g_sel
    in_group = jnp.logical_and(lane >= e_lo, lane < e_lo + EXPERTS_PER_GROUP)
    e_log = jnp.where(in_group, logits, neg_inf)
    e_max = jnp.max(e_log, axis=-1, keepdims=True)
    top1 = jnp.min(jnp.where(e_log == e_max, lane, LANES), axis=-1, keepdims=True)
    e_log2 = jnp.where(lane == top1, neg_inf, e_log)
    e_max2 = jnp.max(e_log2, axis=-1, keepdims=True)
    top2 = jnp.min(jnp.where(e_log2 == e_max2, lane, LANES), axis=-1, keepdims=True)
    ratio = jnp.exp(e_max2 - e_max)
    w_top1 = p_group / (1.0 + ratio)
    w_top2 = p_group * ratio / (1.0 + ratio)
    comb_ref[...] = jnp.where(lane == top1, w_top1, 0.0) + jnp.where(lane == top2, w_top2, 0.0)


def _out_proj(ya, yg, x2, mod, wo, g2, wr, br, *, T, tm):
    N, D = x2.shape
    per_b = T // tm
    rows = lambda c: pl.BlockSpec((tm, c), lambda i: (i, 0))
    const = lambda a: pl.BlockSpec(a.shape, lambda i: (0,) * a.ndim)
    return pl.pallas_call(
        _outproj_kernel,
        grid=(N // tm,),
        in_specs=[rows(ATT_Q), rows(GLA_V), rows(D),
                  pl.BlockSpec((None, 6, D), lambda i: (i // per_b, 0, 0)),
                  const(wo), const(g2), const(wr), const(br)],
        out_specs=[rows(D), rows(D), rows(LANES)],
        out_shape=[jax.ShapeDtypeStruct((N, D), F32), jax.ShapeDtypeStruct((N, D), BF16),
                   jax.ShapeDtypeStruct((N, LANES), F32)],
        compiler_params=_params("arbitrary"),
        name="out_proj",
    )(ya, yg, x2, mod, wo, g2, wr, br)


def _moe_kernel(h_ref, comb_ref, x1_ref, mod_ref, w1_ref, w3_ref, w2_ref, o_ref, acc_ref):
    e = pl.program_id(1)

    @pl.when(e == 0)
    def _():
        acc_ref[...] = jnp.zeros_like(acc_ref)

    h = h_ref[...]
    a = _dot(h, w1_ref[...])
    g = _dot(h, w3_ref[...])
    comb = comb_ref[...]
    lane = lax.broadcasted_iota(jnp.int32, comb.shape, 1)
    weight = jnp.sum(jnp.where(lane == ROUTER_LANE0 + e, comb, 0.0), axis=-1, keepdims=True)
    hid = a * _sigmoid(a) * g * weight
    acc_ref[...] += _dot(hid.astype(BF16), w2_ref[...])

    @pl.when(e == N_EXPERTS - 1)
    def _():
        o_ref[...] = x1_ref[...] + mod_ref[5:6, :] * acc_ref[...]


def _moe(h2, comb, x1, mod, w1, w3, w2, *, T, tm):
    N, D = x1.shape
    per_b = T // tm
    rows = lambda c: pl.BlockSpec((tm, c), lambda i, e: (i, 0))
    return pl.pallas_call(
        _moe_kernel,
        grid=(N // tm, N_EXPERTS),
        in_specs=[rows(D), rows(LANES), rows(D),
                  pl.BlockSpec((None, 6, D), lambda i, e: (i // per_b, 0, 0)),
                  pl.BlockSpec((None, D, D_EXPERT), lambda i, e: (e, 0, 0)),
                  pl.BlockSpec((None, D, D_EXPERT), lambda i, e: (e, 0, 0)),
                  pl.BlockSpec((None, D_EXPERT, D), lambda i, e: (e, 0, 0))],
        out_specs=rows(D),
        out_shape=jax.ShapeDtypeStruct((N, D), F32),
        scratch_shapes=[pltpu.VMEM((tm, D), F32)],
        compiler_params=_params("arbitrary", "arbitrary"),
        name="moe",
    )(h2, comb, x1, mod, w1, w3, w2)


def _block_diag(n, blk, value, dtype):
    r = jnp.arange(n)[:, None] // blk
    c = jnp.arange(n)[None, :] // blk
    return jnp.where(r == c, value, 0.0).astype(dtype)


def _gla_constants():
    L = GLA_CHUNK
    i = jnp.arange(L)[:, None]
    j = jnp.arange(L)[None, :]
    tri = (j <= i).astype(BF16)
    bdtri = jnp.logical_and(j <= i, i // GLA_SUB == j // GLA_SUB).astype(BF16)
    col = jnp.arange(N_SUB * LANES)[None, :]
    qmask = jnp.stack([jnp.logical_and(col // LANES == i // GLA_SUB, (col % LANES) // GLA_DK == hh)
                       for hh in range(2)]).astype(BF16)
    bdms = _block_diag(2 * GLA_DV, GLA_DV, 1.0 / GLA_DV, BF16)
    d = jnp.arange(LANES)[:, None] // GLA_DK
    e = jnp.arange(2 * GLA_DV)[None, :] // GLA_DV
    smask = (d == e).astype(F32)
    return tri, bdtri, qmask, bdms, smask


def kernel(x, c, w_ada, b_ada, g_norm1, w_in, q_norm, k_norm, sinks, w_gk2, b_gk, g_gla_out, g_att_out,
           w_out, g_norm2, w_group, b_group, w_router, b_router, w1, w3, w2):
    B, T, D = x.shape
    N = B * T
    depth = w_ada.shape[0]
    tri, bdtri, qmask, bdms, smask = _gla_constants()
    bdq = _block_diag(ATT_Q, HEAD_DIM, 1.0 / HEAD_DIM, BF16)
    bdk = _block_diag(ATT_KV, HEAD_DIM, 1.0 / HEAD_DIM, BF16)

    x2 = x.reshape(N, D)
    for l in range(depth):
        mod = _adaln_mod(c, w_ada[l], b_ada[l]).reshape(B, 6, D)

        wl = w_in[l]
        lr0 = ATT_Q + 2 * ATT_KV + 2 * GLA_K + GLA_V
        w_in_p = jnp.concatenate(
            [wl[:, :lr0], wl[:, lr0 + GLA_RANK:], wl[:, lr0:lr0 + GLA_RANK],
             jnp.zeros((D, LANES - GLA_RANK), wl.dtype)], axis=1).astype(BF16)
        wgk = jnp.concatenate([w_gk2[l], jnp.zeros((LANES - GLA_RANK, GLA_K), F32)], axis=0).astype(BF16)
        qa, ka, va, qg, kg, vg, la, og = _in_proj(
            x2, mod, g_norm1[l].reshape(1, D), w_in_p,
            jnp.tile(q_norm[l], ATT_HEADS).reshape(1, ATT_Q), jnp.tile(k_norm[l], ATT_KV_HEADS).reshape(1, ATT_KV),
            bdq, bdk, wgk, b_gk[l].reshape(1, GLA_K), T=T, tm=256)

        y_att = _swa_attention(sinks[l], qa, ka, va, g_att_out[l].reshape(1, ATT_Q), B=B, T=T)
        y_gla = _gla(qg, kg, vg, la, og, tri, bdtri, qmask, bdms, smask,
                     jnp.tile(g_gla_out[l], GLA_HEADS).reshape(1, GLA_V), B=B, T=T)

        pad = LANES - N_GROUPS - N_EXPERTS
        wr = jnp.concatenate([w_group[l], w_router[l], jnp.zeros((D, pad), F32)], axis=1).astype(BF16)
        br = jnp.concatenate([b_group[l], b_router[l], jnp.zeros((pad,), F32)]).reshape(1, LANES)
        x1, h2, comb = _out_proj(y_att, y_gla, x2, mod, w_out[l].astype(BF16), g_norm2[l].reshape(1, D),
                                 wr, br, T=T, tm=256)

        x2 = _moe(h2, comb, x1, mod, w1[l].astype(BF16), w3[l].astype(BF16), w2[l].astype(BF16), T=T, tm=512)
    return x2.reshape(B, T, D)
```

```python
import functools

import jax
import jax.numpy as jnp
from jax import lax
from jax.experimental import pallas as pl
from jax.experimental.pallas import tpu as pltpu

F32 = jnp.float32
BF16 = jnp.bfloat16

EPS = 1e-6
ATT_HEADS = 8
ATT_KV_HEADS = 2
HEAD_DIM = 64
WINDOW = 128
ATT_Q = ATT_HEADS * HEAD_DIM
ATT_KV = ATT_KV_HEADS * HEAD_DIM
GLA_HEADS = 4
GLA_DK = 64
GLA_DV = 128
GLA_RANK = 16
GLA_NORMALIZER = 16.0
GLA_K = GLA_HEADS * GLA_DK
GLA_V = GLA_HEADS * GLA_DV
N_GROUPS = 4
EXPERTS_PER_GROUP = 4
N_EXPERTS = N_GROUPS * EXPERTS_PER_GROUP
D_EXPERT = 256

LANES = 128
GLA_CHUNK = 128
GLA_SUB = 16
N_SUB = GLA_CHUNK // GLA_SUB
ROUTER_LANE0 = N_GROUPS
VMEM_LIMIT = 56 * 1024 * 1024
MOE_TILE = 256
CHUNK = 16
TILE_CHUNKS = MOE_TILE // CHUNK
LOCAL_CHUNKS = (MOE_TILE + N_GROUPS * (CHUNK - 1)) // CHUNK + 2
LOCAL_ROWS = LOCAL_CHUNKS * CHUNK

_QA0, _KA0, _VA0 = 0, ATT_Q, ATT_Q + ATT_KV
_QG0 = _VA0 + ATT_KV
_KG0 = _QG0 + GLA_K
_VG0 = _KG0 + GLA_K
_OG0 = _VG0 + GLA_V
_LR0 = _OG0 + GLA_V
IN_COLS_PAD = _LR0 + LANES


def _dot(a, b):
    return jnp.dot(a, b, preferred_element_type=F32)


def _dot_nt(a, b):
    return lax.dot_general(a, b, (((1,), (1,)), ((), ())), preferred_element_type=F32)


def _sigmoid(x):
    return 1.0 / (1.0 + jnp.exp(-x))


def _params(*sem):
    return pltpu.CompilerParams(dimension_semantics=sem, vmem_limit_bytes=VMEM_LIMIT)


def _mod_kernel(c_ref, w_ref, b_ref, o_ref):
    c = c_ref[...]
    s = (c * _sigmoid(c)).astype(BF16)
    o_ref[...] = _dot(s, w_ref[...].astype(BF16)) + b_ref[...]


def _adaln_mod(c, w_ada, b_ada):
    B, D = c.shape
    n = w_ada.shape[1]
    tn = 1536
    return pl.pallas_call(
        _mod_kernel,
        grid=(n // tn,),
        in_specs=[pl.BlockSpec((B, D), lambda j: (0, 0)),
                  pl.BlockSpec((D, tn), lambda j: (0, j)),
                  pl.BlockSpec((1, tn), lambda j: (0, j))],
        out_specs=pl.BlockSpec((B, tn), lambda j: (0, j)),
        out_shape=jax.ShapeDtypeStruct((B, n), F32),
        compiler_params=_params("arbitrary"),
        name="adaln_mod",
    )(c, w_ada, b_ada.reshape(1, n))


def _inproj_kernel(x_ref, mod_ref, g1_ref, w_ref, qn_ref, kn_ref, bdq_ref, bdk_ref, wgk_ref, bgk_ref,
                   qa_ref, ka_ref, va_ref, qg_ref, kg_ref, vg_ref, la_ref, og_ref):
    x = x_ref[...]
    ms = jnp.mean(x * x, axis=-1, keepdims=True)
    xn = x * lax.rsqrt(ms + EPS) * g1_ref[...]
    h = xn * (1.0 + mod_ref[1:2, :]) + mod_ref[0:1, :]
    proj = _dot(h.astype(BF16), w_ref[...])

    q = proj[:, _QA0:_QA0 + ATT_Q]
    q_ms = _dot((q * q).astype(BF16), bdq_ref[...])
    qa_ref[...] = (q * lax.rsqrt(q_ms + EPS) * qn_ref[...] * (HEAD_DIM ** -0.5)).astype(BF16)

    k = proj[:, _KA0:_KA0 + ATT_KV]
    k_ms = _dot((k * k).astype(BF16), bdk_ref[...])
    k = k * lax.rsqrt(k_ms + EPS) * kn_ref[...]
    v = proj[:, _VA0:_VA0 + ATT_KV]
    low = lax.broadcasted_iota(jnp.int32, k.shape, 1) < HEAD_DIM
    for src, dst in ((k, ka_ref), (v, va_ref)):
        swapped = pltpu.roll(src, HEAD_DIM, axis=1)
        dst[:, 0:LANES] = jnp.where(low, src, swapped).astype(BF16)
        dst[:, LANES:2 * LANES] = jnp.where(low, swapped, src).astype(BF16)

    qg_ref[...] = (proj[:, _QG0:_QG0 + GLA_K] * (GLA_DK ** -0.5)).astype(BF16)
    kg_ref[...] = proj[:, _KG0:_KG0 + GLA_K].astype(BF16)
    vg_ref[...] = proj[:, _VG0:_VG0 + GLA_V].astype(BF16)
    og = proj[:, _OG0:_OG0 + GLA_V]
    og_ref[...] = (og * _sigmoid(og)).astype(BF16)

    gate = _dot(proj[:, _LR0:_LR0 + LANES].astype(BF16), wgk_ref[...]) + bgk_ref[...]
    log_sig = jnp.minimum(gate, 0.0) - jnp.log(1.0 + jnp.exp(-jnp.abs(gate)))
    la_ref[...] = log_sig * (1.0 / GLA_NORMALIZER)


def _in_proj(x2, mod, g1, w_in_p, qn, kn, bdq, bdk, wgk, bgk, *, T, tm):
    N, D = x2.shape
    per_b = T // tm
    const = lambda shape: pl.BlockSpec(shape, lambda i: (0,) * len(shape))
    rows = lambda c: pl.BlockSpec((tm, c), lambda i: (i, 0))
    outs = [(ATT_Q, BF16), (2 * ATT_KV, BF16), (2 * ATT_KV, BF16), (GLA_K, BF16), (GLA_K, BF16),
            (GLA_V, BF16), (GLA_K, F32), (GLA_V, BF16)]
    return pl.pallas_call(
        _inproj_kernel,
        grid=(N // tm,),
        in_specs=[rows(D),
                  pl.BlockSpec((None, 6, D), lambda i: (i // per_b, 0, 0)),
                  const((1, D)), const(w_in_p.shape), const(qn.shape), const(kn.shape),
                  const(bdq.shape), const(bdk.shape), const(wgk.shape), const(bgk.shape)],
        out_specs=[rows(c) for c, _ in outs],
        out_shape=[jax.ShapeDtypeStruct((N, c), dt) for c, dt in outs],
        compiler_params=_params("arbitrary"),
        name="in_proj",
    )(x2, mod, g1, w_in_p, qn, kn, bdq, bdk, wgk, bgk)


def _attn_kernel(sinks_ref, q_ref, kc_ref, vc_ref, kp_ref, vp_ref, gatt_ref, o_ref):
    blk = WINDOW
    first = pl.program_id(1) == 0
    qi = lax.broadcasted_iota(jnp.int32, (blk, blk), 0)
    cj = lax.broadcasted_iota(jnp.int32, (blk, blk), 1)
    from_prev = cj > qi
    dist = (qi - cj + jnp.where(from_prev, blk, 0)).astype(F32)
    no_prev = jnp.where(jnp.logical_and(from_prev, first), -1e30, 0.0)
    low = cj < HEAD_DIM
    half = (jnp.where(low, 1.0, 0.0).astype(BF16), jnp.where(low, 0.0, 1.0).astype(BF16))
    r2 = lax.broadcasted_iota(jnp.int32, (2 * blk, blk), 0)
    c2 = lax.broadcasted_iota(jnp.int32, (2 * blk, blk), 1)
    half_sum = jnp.where((r2 < blk) == (c2 < HEAD_DIM), 1.0, 0.0).astype(BF16)

    pairs = []
    for g in range(ATT_KV_HEADS):
        lanes = slice(g * LANES, (g + 1) * LANES)
        kc, kp, vc, vp = kc_ref[:, lanes], kp_ref[:, lanes], vc_ref[:, lanes], vp_ref[:, lanes]
        v_stack = jnp.concatenate([vp * half[0], vp * half[1], vc * half[0], vc * half[1]], axis=0)
        for jj in range(ATT_HEADS // ATT_KV_HEADS // 2):
            j = g * (ATT_HEADS // ATT_KV_HEADS // 2) + jj
            qp = q_ref[:, j * LANES:(j + 1) * LANES]
            probs, sink_terms = [], []
            for p in range(2):
                h = 2 * j + p
                slope = 2.0 ** (-8.0 * (h + 1) / ATT_HEADS)
                s = jnp.where(from_prev, _dot_nt(qp, kp * half[p]), _dot_nt(qp, kc * half[p]))
                s = s - slope * dist + no_prev
                sink = sinks_ref[h]
                m = jnp.maximum(jnp.max(s, axis=-1, keepdims=True), sink)
                probs.append(jnp.exp(s - m))
                sink_terms.append(jnp.exp(sink - m))
            p_prev = [jnp.where(from_prev, e, 0.0).astype(BF16) for e in probs]
            p_cur = [jnp.where(from_prev, 0.0, e).astype(BF16) for e in probs]
            pv = _dot(jnp.concatenate(p_prev + p_cur, axis=1), v_stack)
            den = _dot(jnp.concatenate([e.astype(BF16) for e in probs], axis=1), half_sum)
            den = den + jnp.where(low, sink_terms[0], sink_terms[1])
            pairs.append(pv / den)
    o = jnp.concatenate(pairs, axis=1)
    ms = jnp.mean(o * o, axis=-1, keepdims=True)
    o_ref[...] = (o * lax.rsqrt(ms + EPS) * gatt_ref[...]).astype(BF16)


def _swa_attention(sinks, qa, ka, va, gatt, *, B, T):
    nb = T // WINDOW
    cur = lambda c: pl.BlockSpec((WINDOW, c), lambda b, i: (b * nb + i, 0))
    prev = lambda c: pl.BlockSpec((WINDOW, c), lambda b, i: (b * nb + jnp.maximum(i - 1, 0), 0))
    return pl.pallas_call(
        _attn_kernel,
        grid=(B, nb),
        in_specs=[pl.BlockSpec(memory_space=pltpu.SMEM),
                  cur(ATT_Q), cur(2 * ATT_KV), cur(2 * ATT_KV), prev(2 * ATT_KV), prev(2 * ATT_KV),
                  pl.BlockSpec((1, ATT_Q), lambda b, i: (0, 0))],
        out_specs=cur(ATT_Q),
        out_shape=jax.ShapeDtypeStruct((B * T, ATT_Q), BF16),
        compiler_params=_params("arbitrary", "arbitrary"),
        name="swa_attn",
    )(sinks, qa, ka, va, ka, va, gatt)


def _gla_kernel(q_ref, k_ref, v_ref, la_ref, og_ref, tri_ref, bdtri_ref, qmask_ref, bdms_ref, smask_ref,
                ggla_ref, o_ref, state_ref):
    L = GLA_CHUNK

    @pl.when(pl.program_id(1) == 0)
    def _():
        state_ref[...] = jnp.zeros_like(state_ref)

    la = la_ref[...]
    la_hi = la.astype(BF16)
    la_lo = (la - la_hi.astype(F32)).astype(BF16)
    tri, bdtri = tri_ref[...], bdtri_ref[...]
    b = _dot(tri, la_hi) + _dot(tri, la_lo)
    b_in = _dot(bdtri, la_hi) + _dot(bdtri, la_lo)
    ref = b - b_in
    b_last = b[L - 1:L, :]
    q = q_ref[...].astype(F32)
    k = k_ref[...].astype(F32)
    q_in = q * jnp.exp(b_in)
    q_dec = (q * jnp.exp(b)).astype(BF16)
    k_dec = k * jnp.exp(b_last - b)

    rows = lax.broadcasted_iota(jnp.int32, (L, LANES), 0)
    causal = lax.broadcasted_iota(jnp.int32, (L, L), 0) >= lax.broadcasted_iota(jnp.int32, (L, L), 1)
    for pair in range(GLA_HEADS // 2):
        kl = slice(pair * LANES, (pair + 1) * LANES)
        vl = slice(pair * 2 * GLA_DV, (pair + 1) * 2 * GLA_DV)
        k_p, b_p, ref_p = k[:, kl], b[:, kl], ref[:, kl]
        keys = []
        for g in range(N_SUB):
            last_row = (g + 1) * GLA_SUB - 1
            expo = jnp.where(rows <= last_row, ref_p[g * GLA_SUB:g * GLA_SUB + 1, :] - b_p, -1e4)
            keys.append((k_p * jnp.exp(expo)).astype(BF16))
        keys = jnp.concatenate(keys, axis=1)
        q_rep = jnp.concatenate([q_in[:, kl].astype(BF16)] * N_SUB, axis=1)
        q_both = jnp.concatenate([q_rep * qmask_ref[0], q_rep * qmask_ref[1]], axis=0)
        scores = _dot_nt(q_both, keys)
        v_p = v_ref[:, vl]
        o_parts = []
        for hh in range(2):
            a = jnp.where(causal, scores[hh * L:(hh + 1) * L, :], 0.0).astype(BF16)
            o_parts.append(_dot(a, v_p[:, hh * GLA_DV:(hh + 1) * GLA_DV]))
        state = state_ref[pair]
        o = jnp.concatenate(o_parts, axis=1) + _dot(q_dec[:, kl], state.astype(BF16))

        update = _dot(k_dec[:, kl].T.astype(BF16), v_p) * smask_ref[...]
        decay = jnp.broadcast_to(jnp.exp(b_last[:, kl]), (LANES, LANES)).T
        state_ref[pair] = state * jnp.concatenate([decay, decay], axis=1) + update

        ms = _dot((o * o).astype(BF16), bdms_ref[...])
        y = o * lax.rsqrt(ms + EPS) * ggla_ref[:, vl] * og_ref[:, vl].astype(F32)
        o_ref[:, vl] = y.astype(BF16)


def _gla(qg, kg, vg, la, og, tri, bdtri, qmask, bdms, smask, ggla, *, B, T):
    L = GLA_CHUNK
    nc = T // L
    rows = lambda c: pl.BlockSpec((L, c), lambda b, i: (b * nc + i, 0))
    const = lambda a: pl.BlockSpec(a.shape, lambda b, i: (0,) * a.ndim)
    return pl.pallas_call(
        _gla_kernel,
        grid=(B, nc),
        in_specs=[rows(GLA_K), rows(GLA_K), rows(GLA_V), rows(GLA_K), rows(GLA_V),
                  const(tri), const(bdtri), const(qmask), const(bdms), const(smask), const(ggla)],
        out_specs=rows(GLA_V),
        out_shape=jax.ShapeDtypeStruct((B * T, GLA_V), BF16),
        scratch_shapes=[pltpu.VMEM((GLA_HEADS // 2, LANES, 2 * GLA_DV), F32)],
        compiler_params=_params("arbitrary", "arbitrary"),
        name="gla",
    )(qg, kg, vg, la, og, tri, bdtri, qmask, bdms, smask, ggla)


def _outproj_kernel(ya_ref, yg_ref, x_ref, mod_ref, wo_ref, g2_ref, wr_ref, br_ref, stril_ref,
                    x1_ref, row_ref, lpos_ref, cnt_ref):
    mix = _dot(ya_ref[...], wo_ref[0:ATT_Q, :]) + _dot(yg_ref[...], wo_ref[ATT_Q:ATT_Q + GLA_V, :])
    x1 = x_ref[...] + mod_ref[2:3, :] * mix
    x1_ref[...] = x1
    ms = jnp.mean(x1 * x1, axis=-1, keepdims=True)
    h2 = (x1 * lax.rsqrt(ms + EPS) * g2_ref[...]) * (1.0 + mod_ref[4:5, :]) + mod_ref[3:4, :]
    h2b = h2.astype(BF16)

    logits = _dot(h2b, wr_ref[...]) + br_ref[...]
    lane = lax.broadcasted_iota(jnp.int32, logits.shape, 1)
    neg_inf = -jnp.inf
    g_log = jnp.where(lane < N_GROUPS, logits, neg_inf)
    g_max = jnp.max(g_log, axis=-1, keepdims=True)
    g_sel = jnp.min(jnp.where(g_log == g_max, lane, LANES), axis=-1, keepdims=True)
    g_sum = jnp.sum(jnp.where(lane < N_GROUPS, jnp.exp(logits - g_max), 0.0), axis=-1, keepdims=True)
    p_group = 1.0 / g_sum
    e_lo = ROUTER_LANE0 + EXPERTS_PER_GROUP * g_sel
    in_group = jnp.logical_and(lane >= e_lo, lane < e_lo + EXPERTS_PER_GROUP)
    e_log = jnp.where(in_group, logits, neg_inf)
    e_max = jnp.max(e_log, axis=-1, keepdims=True)
    top1 = jnp.min(jnp.where(e_log == e_max, lane, LANES), axis=-1, keepdims=True)
    e_log2 = jnp.where(lane == top1, neg_inf, e_log)
    e_max2 = jnp.max(e_log2, axis=-1, keepdims=True)
    top2 = jnp.min(jnp.where(e_log2 == e_max2, lane, LANES), axis=-1, keepdims=True)
    ratio = jnp.exp(e_max2 - e_max)
    w_top1 = p_group / (1.0 + ratio)
    w_top2 = p_group * ratio / (1.0 + ratio)
    weights = (jnp.where(lane == top1 - e_lo, w_top1, 0.0) + jnp.where(lane == top2 - e_lo, w_top2, 0.0))
    w_hi = weights.astype(BF16)
    w_lo = (weights - w_hi.astype(F32)).astype(BF16)

    onehot = jnp.where(lane == g_sel, 1.0, 0.0)
    before = _dot(stril_ref[...], onehot.astype(BF16))
    count = jnp.sum(onehot, axis=0, keepdims=True)
    padded = jnp.broadcast_to(jnp.floor((count + (CHUNK - 1.0)) * (1.0 / CHUNK)) * CHUNK, (8, LANES))
    lane8 = lax.broadcasted_iota(jnp.int32, (8, LANES), 1)
    start = jnp.zeros((8, LANES), F32)
    for shift in range(1, N_GROUPS):
        start = start + jnp.where(lane8 >= shift, pltpu.roll(padded, shift, axis=1), 0.0)
    lpos = jnp.sum(onehot * (before + start[0:1, :]), axis=-1, keepdims=True)
    lpos_b = jnp.broadcast_to(lpos, logits.shape)
    lpos_ref[...] = lpos_b
    cnt_ref[...] = count
    tm = logits.shape[0]
    local_row = lax.broadcasted_iota(jnp.int32, (LOCAL_ROWS, tm), 0).astype(F32)
    perm = jnp.where(local_row == lpos_b.T[0:1, :], 1.0, 0.0).astype(BF16)
    row_ref[...] = _dot(perm, jnp.concatenate([h2b, w_hi, w_lo], axis=1)).astype(BF16)


def _out_proj(ya, yg, x2, mod, wo, g2, wr, br, stril, *, T):
    N, D = x2.shape
    tm = MOE_TILE
    per_b = T // tm
    rows = lambda c: pl.BlockSpec((tm, c), lambda i: (i, 0))
    const = lambda a: pl.BlockSpec(a.shape, lambda i: (0,) * a.ndim)
    return pl.pallas_call(
        _outproj_kernel,
        grid=(N // tm,),
        in_specs=[rows(ATT_Q), rows(GLA_V), rows(D),
                  pl.BlockSpec((None, 6, D), lambda i: (i // per_b, 0, 0)),
                  const(wo), const(g2), const(wr), const(br), const(stril)],
        out_specs=[rows(D), pl.BlockSpec((LOCAL_ROWS, D + 2 * LANES), lambda i: (i, 0)), rows(LANES),
                   pl.BlockSpec((None, 1, LANES), lambda i: (i, 0, 0))],
        out_shape=[jax.ShapeDtypeStruct((N, D), F32),
                   jax.ShapeDtypeStruct((N // tm * LOCAL_ROWS, D + 2 * LANES), BF16),
                   jax.ShapeDtypeStruct((N, LANES), F32),
                   jax.ShapeDtypeStruct((N // tm, 1, LANES), F32)],
        compiler_params=_params("arbitrary"),
        name="out_proj",
    )(ya, yg, x2, mod, wo, g2, wr, br, stril)


def _chunk_copy(src_ref, src_chunk, dst_ref, dst_chunk, sem):
    src = src_ref.at[pl.ds(pl.multiple_of(src_chunk * CHUNK, CHUNK), CHUNK)]
    dst = dst_ref.at[pl.ds(pl.multiple_of(dst_chunk * CHUNK, CHUNK), CHUNK)]
    return pltpu.make_async_copy(src, dst, sem)


def _moe_kernel(src_ref, nvalid_ref, grp_ref, nt_ref, used_ref, rows_ref, w1_ref, w3_ref, w2_ref, y_ref,
                in_buf, out_buf, zero_buf, in_sem, out_sem, zero_sem):
    del grp_ref
    j = pl.program_id(0)
    n_tiles = nt_ref[0]
    d_model = w2_ref.shape[1]

    def gather(tile, slot):
        def body(k, carry):
            _chunk_copy(rows_ref, src_ref[tile * TILE_CHUNKS + k], in_buf.at[slot], k, in_sem.at[slot]).start()
            return carry
        lax.fori_loop(0, TILE_CHUNKS, body, 0, unroll=True)

    def wait_gather(slot):
        def body(k, carry):
            _chunk_copy(rows_ref, 0, in_buf.at[slot], k, in_sem.at[slot]).wait()
            return carry
        lax.fori_loop(0, TILE_CHUNKS, body, 0, unroll=True)

    def scatter(tile, slot):
        def body(k, carry):
            _chunk_copy(out_buf.at[slot], k, y_ref, src_ref[tile * TILE_CHUNKS + k], out_sem.at[slot]).start()
            return carry
        lax.fori_loop(0, nvalid_ref[tile], body, 0)

    def wait_scatter(tile, slot):
        def body(k, carry):
            _chunk_copy(out_buf.at[slot], k, y_ref, 0, out_sem.at[slot]).wait()
            return carry
        lax.fori_loop(0, nvalid_ref[tile], body, 0)

    def zero_fill(wait):
        def per_tile(i, carry):
            def body(c, inner):
                copy = _chunk_copy(zero_buf, 0, y_ref, i * LOCAL_CHUNKS + c, zero_sem)
                if wait:
                    copy.wait()
                else:
                    copy.start()
                return inner
            return lax.fori_loop(used_ref[i], LOCAL_CHUNKS, body, carry)
        lax.fori_loop(0, used_ref.shape[0], per_tile, 0)

    @pl.when(j == 0)
    def _():
        zero_buf[...] = jnp.zeros_like(zero_buf)
        zero_fill(wait=False)
        gather(0, 0)

    @pl.when(j + 1 < n_tiles)
    def _():
        gather(j + 1, (j + 1) % 2)

    @pl.when(j < n_tiles)
    def _():
        slot = j % 2
        wait_gather(slot)

        @pl.when(j >= 2)
        def _():
            wait_scatter(j - 2, slot)

        rows = in_buf[slot]
        h = rows[:, 0:d_model]
        weights = rows[:, d_model:d_model + LANES].astype(F32) + rows[:, d_model + LANES:].astype(F32)
        a = _dot(h, w1_ref[...])
        g = _dot(h, w3_ref[...])
        scale = jnp.concatenate(
            [jnp.broadcast_to(weights[:, k:k + 1], (rows.shape[0], D_EXPERT)) for k in range(EXPERTS_PER_GROUP)],
            axis=1)
        hid = a * _sigmoid(a) * g * scale
        out_buf[slot] = _dot(hid.astype(BF16), w2_ref[...]).astype(BF16)
        scatter(j, slot)

        @pl.when(j == n_tiles - 1)
        def _():
            @pl.when(j >= 1)
            def _():
                wait_scatter(j - 1, 1 - slot)
            wait_scatter(j, slot)
            zero_fill(wait=True)


def _combine_kernel(x1_ref, mod_ref, lpos_ref, y_ref, o_ref):
    tm = x1_ref.shape[0]
    local_row = lax.broadcasted_iota(jnp.int32, (tm, LOCAL_ROWS), 1).astype(F32)
    unsort = jnp.where(local_row == lpos_ref[:, 0:1], 1.0, 0.0).astype(BF16)
    o_ref[...] = x1_ref[...] + mod_ref[5:6, :] * _dot(unsort, y_ref[...])


def _moe_plan(cnt):
    n_local = cnt.shape[0]
    chunks = (cnt + CHUNK - 1) // CHUNK
    used = jnp.sum(chunks, axis=1)
    local_off = jnp.cumsum(chunks, axis=1) - chunks
    tiles_g = (jnp.sum(chunks, axis=0) + TILE_CHUNKS - 1) // TILE_CHUNKS
    tile_end = jnp.cumsum(tiles_g)
    n_tiles = tile_end[-1]
    group_start = (tile_end - tiles_g) * TILE_CHUNKS
    seg_len = chunks.T.reshape(-1)
    seg_start = (group_start[:, None] + (jnp.cumsum(chunks, axis=0) - chunks).T).reshape(-1)
    seg_src = (jnp.arange(n_local)[None, :] * LOCAL_CHUNKS + local_off.T).reshape(-1)
    max_chunks = n_local * MOE_TILE // CHUNK + n_local * N_GROUPS + N_GROUPS * TILE_CHUNKS
    max_tiles = (max_chunks + TILE_CHUNKS - 1) // TILE_CHUNKS
    c = jnp.arange(max_tiles * TILE_CHUNKS)
    seg = jnp.sum(seg_start[None, :] <= c[:, None], axis=1) - 1
    within = c - seg_start[seg]
    valid = within < seg_len[seg]
    src = jnp.where(valid, seg_src[seg] + within, LOCAL_CHUNKS - 1)
    nvalid = jnp.sum(valid.reshape(max_tiles, TILE_CHUNKS), axis=1)
    j = jnp.minimum(jnp.arange(max_tiles), n_tiles - 1)
    grp = jnp.sum(j[:, None] >= tile_end[None, :], axis=1)
    i32 = lambda a: a.astype(jnp.int32)
    return i32(src), i32(nvalid), i32(grp), i32(n_tiles).reshape(1), i32(used)


def _moe(plan, rows_local, w1g, w3g, w2g):
    src, nvalid, grp, n_tiles, used = plan
    D = w2g.shape[2]
    wide = w1g.shape[2]
    weights = lambda shape: pl.BlockSpec((None,) + shape, lambda j, src, nv, grp, nt, used: (grp[j], 0, 0))
    return pl.pallas_call(
        _moe_kernel,
        grid_spec=pltpu.PrefetchScalarGridSpec(
            num_scalar_prefetch=5,
            grid=(grp.shape[0],),
            in_specs=[pl.BlockSpec(memory_space=pl.ANY), weights((D, wide)), weights((D, wide)), weights((wide, D))],
            out_specs=pl.BlockSpec(memory_space=pl.ANY),
            scratch_shapes=[pltpu.VMEM((2, MOE_TILE, rows_local.shape[1]), BF16),
                            pltpu.VMEM((2, MOE_TILE, D), BF16),
                            pltpu.VMEM((CHUNK, D), BF16),
                            pltpu.SemaphoreType.DMA((2,)), pltpu.SemaphoreType.DMA((2,)),
                            pltpu.SemaphoreType.DMA(())]),
        out_shape=jax.ShapeDtypeStruct((rows_local.shape[0], D), BF16),
        compiler_params=_params("arbitrary"),
        name="moe",
    )(src, nvalid, grp, n_tiles, used, rows_local, w1g, w3g, w2g)


def _combine(x1, mod, lpos, y_local, *, T):
    N, D = x1.shape
    per_b = T // MOE_TILE
    return pl.pallas_call(
        _combine_kernel,
        grid=(N // MOE_TILE,),
        in_specs=[pl.BlockSpec((MOE_TILE, D), lambda i: (i, 0)),
                  pl.BlockSpec((None, 6, D), lambda i: (i // per_b, 0, 0)),
                  pl.BlockSpec((MOE_TILE, LANES), lambda i: (i, 0)),
                  pl.BlockSpec((LOCAL_ROWS, D), lambda i: (i, 0))],
        out_specs=pl.BlockSpec((MOE_TILE, D), lambda i: (i, 0)),
        out_shape=jax.ShapeDtypeStruct((N, D), F32),
        compiler_params=_params("arbitrary"),
        name="moe_combine",
    )(x1, mod, lpos, y_local)


def _block_diag(n, blk, value, dtype):
    r = jnp.arange(n)[:, None] // blk
    c = jnp.arange(n)[None, :] // blk
    return jnp.where(r == c, value, 0.0).astype(dtype)


def _gla_constants():
    L = GLA_CHUNK
    i = jnp.arange(L)[:, None]
    j = jnp.arange(L)[None, :]
    tri = (j <= i).astype(BF16)
    bdtri = jnp.logical_and(j <= i, i // GLA_SUB == j // GLA_SUB).astype(BF16)
    col = jnp.arange(N_SUB * LANES)[None, :]
    qmask = jnp.stack([jnp.logical_and(col // LANES == i // GLA_SUB, (col % LANES) // GLA_DK == hh)
                       for hh in range(2)]).astype(BF16)
    bdms = _block_diag(2 * GLA_DV, GLA_DV, 1.0 / GLA_DV, BF16)
    d = jnp.arange(LANES)[:, None] // GLA_DK
    e = jnp.arange(2 * GLA_DV)[None, :] // GLA_DV
    smask = (d == e).astype(F32)
    return tri, bdtri, qmask, bdms, smask


def kernel(x, c, w_ada, b_ada, g_norm1, w_in, q_norm, k_norm, sinks, w_gk2, b_gk, g_gla_out, g_att_out,
           w_out, g_norm2, w_group, b_group, w_router, b_router, w1, w3, w2):
    B, T, D = x.shape
    N = B * T
    depth = w_ada.shape[0]
    tri, bdtri, qmask, bdms, smask = _gla_constants()
    bdq = _block_diag(ATT_Q, HEAD_DIM, 1.0 / HEAD_DIM, BF16)
    bdk = _block_diag(ATT_KV, HEAD_DIM, 1.0 / HEAD_DIM, BF16)

    x2 = x.reshape(N, D)
    for l in range(depth):
        mod = _adaln_mod(c, w_ada[l], b_ada[l]).reshape(B, 6, D)

        wl = w_in[l]
        lr0 = ATT_Q + 2 * ATT_KV + 2 * GLA_K + GLA_V
        w_in_p = jnp.concatenate(
            [wl[:, :lr0], wl[:, lr0 + GLA_RANK:], wl[:, lr0:lr0 + GLA_RANK],
             jnp.zeros((D, LANES - GLA_RANK), wl.dtype)], axis=1).astype(BF16)
        wgk = jnp.concatenate([w_gk2[l], jnp.zeros((LANES - GLA_RANK, GLA_K), F32)], axis=0).astype(BF16)
        qa, ka, va, qg, kg, vg, la, og = _in_proj(
            x2, mod, g_norm1[l].reshape(1, D), w_in_p,
            jnp.tile(q_norm[l], ATT_HEADS).reshape(1, ATT_Q), jnp.tile(k_norm[l], ATT_KV_HEADS).reshape(1, ATT_KV),
            bdq, bdk, wgk, b_gk[l].reshape(1, GLA_K), T=T, tm=256)

        y_att = _swa_attention(sinks[l], qa, ka, va, g_att_out[l].reshape(1, ATT_Q), B=B, T=T)
        y_gla = _gla(qg, kg, vg, la, og, tri, bdtri, qmask, bdms, smask,
                     jnp.tile(g_gla_out[l], GLA_HEADS).reshape(1, GLA_V), B=B, T=T)

        pad = LANES - N_GROUPS - N_EXPERTS
        wr = jnp.concatenate([w_group[l], w_router[l], jnp.zeros((D, pad), F32)], axis=1).astype(BF16)
        br = jnp.concatenate([b_group[l], b_router[l], jnp.zeros((pad,), F32)]).reshape(1, LANES)
        stril = (jnp.arange(MOE_TILE)[None, :] < jnp.arange(MOE_TILE)[:, None]).astype(BF16)
        x1, rows_local, lpos, cnt = _out_proj(y_att, y_gla, x2, mod, w_out[l].astype(BF16),
                                              g_norm2[l].reshape(1, D), wr, br, stril, T=T)
        plan = _moe_plan(cnt[:, 0, :N_GROUPS].astype(jnp.int32))
        wide = EXPERTS_PER_GROUP * D_EXPERT
        by_group = lambda w: (w.reshape(N_GROUPS, EXPERTS_PER_GROUP, D, D_EXPERT).transpose(0, 2, 1, 3)
                              .reshape(N_GROUPS, D, wide).astype(BF16))
        y_local = _moe(plan, rows_local, by_group(w1[l]), by_group(w3[l]),
                       w2[l].reshape(N_GROUPS, wide, D).astype(BF16))
        x2 = _combine(x1, mod, lpos, y_local, T=T)
    return x2.reshape(B, T, D)
```

```python
import functools

import jax
import jax.numpy as jnp
from jax import lax
from jax.experimental import pallas as pl
from jax.experimental.pallas import tpu as pltpu

F32 = jnp.float32
BF16 = jnp.bfloat16

EPS = 1e-6
ATT_HEADS = 8
ATT_KV_HEADS = 2
HEAD_DIM = 64
WINDOW = 128
ATT_Q = ATT_HEADS * HEAD_DIM
ATT_KV = ATT_KV_HEADS * HEAD_DIM
GLA_HEADS = 4
GLA_DK = 64
GLA_DV = 128
GLA_RANK = 16
GLA_NORMALIZER = 16.0
GLA_K = GLA_HEADS * GLA_DK
GLA_V = GLA_HEADS * GLA_DV
N_GROUPS = 4
EXPERTS_PER_GROUP = 4
N_EXPERTS = N_GROUPS * EXPERTS_PER_GROUP
D_EXPERT = 256

LANES = 128
PROJ_SUB = 256
PROJ_TILE = 2 * PROJ_SUB
OUT_SUBS = 2
ATT_BLOCKS_PER_STEP = 4
GLA_BATCH_PER_STEP = 4
GLA_CHUNK = 128
GLA_SUB = 16
N_SUB = GLA_CHUNK // GLA_SUB
ROUTER_LANE0 = N_GROUPS
VMEM_LIMIT = 56 * 1024 * 1024
MOE_TILE = 256
CHUNK = 16
TILE_CHUNKS = MOE_TILE // CHUNK
LOCAL_CHUNKS = (MOE_TILE + N_GROUPS * (CHUNK - 1)) // CHUNK + 2
LOCAL_ROWS = LOCAL_CHUNKS * CHUNK

_QA0, _KA0, _VA0 = 0, ATT_Q, ATT_Q + ATT_KV
_QG0 = _VA0 + ATT_KV
_KG0 = _QG0 + GLA_K
_VG0 = _KG0 + GLA_K
_OG0 = _VG0 + GLA_V
_LR0 = _OG0 + GLA_V
IN_COLS_PAD = _LR0 + LANES


def _dot(a, b):
    return jnp.dot(a, b, preferred_element_type=F32)


def _dot_nt(a, b):
    return lax.dot_general(a, b, (((1,), (1,)), ((), ())), preferred_element_type=F32)


def _sigmoid(x):
    return 1.0 / (1.0 + jnp.exp(-x))


def _params(*sem):
    return pltpu.CompilerParams(dimension_semantics=sem, vmem_limit_bytes=VMEM_LIMIT)


def _mod_kernel(c_ref, w_ref, b_ref, o_ref):
    c = c_ref[...]
    s = (c * _sigmoid(c)).astype(BF16)
    o_ref[...] = _dot(s, w_ref[...].astype(BF16)) + b_ref[...]


def _adaln_mod(c, w_ada, b_ada):
    B, D = c.shape
    n = w_ada.shape[1]
    tn = 1536
    return pl.pallas_call(
        _mod_kernel,
        grid=(n // tn,),
        in_specs=[pl.BlockSpec((B, D), lambda j: (0, 0)),
                  pl.BlockSpec((D, tn), lambda j: (0, j)),
                  pl.BlockSpec((1, tn), lambda j: (0, j))],
        out_specs=pl.BlockSpec((B, tn), lambda j: (0, j)),
        out_shape=jax.ShapeDtypeStruct((B, n), F32),
        compiler_params=_params("arbitrary"),
        name="adaln_mod",
    )(c, w_ada, b_ada.reshape(1, n))


def _inproj_kernel(x_ref, mod_ref, g1_ref, w_ref, qn_ref, kn_ref, bdq_ref, bdk_ref, wgk_ref, bgk_ref,
                   qa_ref, ka_ref, va_ref, qg_ref, kg_ref, vg_ref, la_ref, og_ref):
    subs = range(x_ref.shape[0] // PROJ_SUB)
    rows = lambda t: slice(t * PROJ_SUB, (t + 1) * PROJ_SUB)

    h = []
    for t in subs:
        x = x_ref[rows(t), :]
        ms = jnp.mean(x * x, axis=-1, keepdims=True)
        xn = x * lax.rsqrt(ms + EPS) * g1_ref[...]
        h.append((xn * (1.0 + mod_ref[1:2, :]) + mod_ref[0:1, :]).astype(BF16))

    proj = lambda t, c0, width: _dot(h[t], w_ref[:, c0:c0 + width])
    gate_of = lambda lr: _dot(lr.astype(BF16), wgk_ref[...]) + bgk_ref[...]
    qa, kv, q_ms, k_ms, qk_g, vg, og, lr, gate = ({} for _ in range(9))
    for t in subs:
        qa[t] = proj(t, _QA0, ATT_Q)
        if t > 0:
            gate[t - 1] = gate_of(lr[t - 1])
        kv[t] = proj(t, _KA0, 2 * ATT_KV)
        q_ms[t] = _dot((qa[t] * qa[t]).astype(BF16), bdq_ref[...])
        qk_g[t] = proj(t, _QG0, 2 * GLA_K)
        k = kv[t][:, 0:ATT_KV]
        k_ms[t] = _dot((k * k).astype(BF16), bdk_ref[...])
        vg[t] = proj(t, _VG0, GLA_V)
        og[t] = proj(t, _OG0, GLA_V)
        lr[t] = proj(t, _LR0, LANES)
    gate[subs[-1]] = gate_of(lr[subs[-1]])

    low = lax.broadcasted_iota(jnp.int32, (PROJ_SUB, ATT_KV), 1) < HEAD_DIM
    for t in subs:
        qa_ref[rows(t), :] = (qa[t] * lax.rsqrt(q_ms[t] + EPS) * qn_ref[...] * (HEAD_DIM ** -0.5)).astype(BF16)
        k = kv[t][:, 0:ATT_KV] * lax.rsqrt(k_ms[t] + EPS) * kn_ref[...]
        v = kv[t][:, ATT_KV:2 * ATT_KV]
        for src, dst in ((k, ka_ref), (v, va_ref)):
            swapped = pltpu.roll(src, HEAD_DIM, axis=1)
            dst[rows(t), 0:LANES] = jnp.where(low, src, swapped).astype(BF16)
            dst[rows(t), LANES:2 * LANES] = jnp.where(low, swapped, src).astype(BF16)
        qg_ref[rows(t), :] = (qk_g[t][:, 0:GLA_K] * (GLA_DK ** -0.5)).astype(BF16)
        kg_ref[rows(t), :] = qk_g[t][:, GLA_K:2 * GLA_K].astype(BF16)
        vg_ref[rows(t), :] = vg[t].astype(BF16)
        og_ref[rows(t), :] = (og[t] * _sigmoid(og[t])).astype(BF16)
        log_sig = jnp.minimum(gate[t], 0.0) - jnp.log(1.0 + jnp.exp(-jnp.abs(gate[t])))
        la_ref[rows(t), :] = log_sig * (1.0 / GLA_NORMALIZER)


def _in_proj(x2, mod, g1, w_in_p, qn, kn, bdq, bdk, wgk, bgk, *, T, tm):
    N, D = x2.shape
    per_b = T // tm
    const = lambda shape: pl.BlockSpec(shape, lambda i: (0,) * len(shape))
    rows = lambda c: pl.BlockSpec((tm, c), lambda i: (i, 0))
    outs = [(ATT_Q, BF16), (2 * ATT_KV, BF16), (2 * ATT_KV, BF16), (GLA_K, BF16), (GLA_K, BF16),
            (GLA_V, BF16), (GLA_K, F32), (GLA_V, BF16)]
    return pl.pallas_call(
        _inproj_kernel,
        grid=(N // tm,),
        in_specs=[rows(D),
                  pl.BlockSpec((None, 6, D), lambda i: (i // per_b, 0, 0)),
                  const((1, D)), const(w_in_p.shape), const(qn.shape), const(kn.shape),
                  const(bdq.shape), const(bdk.shape), const(wgk.shape), const(bgk.shape)],
        out_specs=[rows(c) for c, _ in outs],
        out_shape=[jax.ShapeDtypeStruct((N, c), dt) for c, dt in outs],
        compiler_params=_params("arbitrary"),
        name="in_proj",
    )(x2, mod, g1, w_in_p, qn, kn, bdq, bdk, wgk, bgk)


def _attn_kernel(sinks_ref, q_ref, kc_ref, vc_ref, kp_ref, vp_ref, gatt_ref, o_ref):
    blk = WINDOW
    first = pl.program_id(1) == 0
    qi = lax.broadcasted_iota(jnp.int32, (blk, blk), 0)
    cj = lax.broadcasted_iota(jnp.int32, (blk, blk), 1)
    from_prev = cj > qi
    dist = (qi - cj + jnp.where(from_prev, blk, 0)).astype(F32)
    no_prev = jnp.where(jnp.logical_and(from_prev, first), -1e30, 0.0)
    low = cj < HEAD_DIM
    half = (jnp.where(low, 1.0, 0.0).astype(BF16), jnp.where(low, 0.0, 1.0).astype(BF16))
    r2 = lax.broadcasted_iota(jnp.int32, (2 * blk, blk), 0)
    c2 = lax.broadcasted_iota(jnp.int32, (2 * blk, blk), 1)
    half_sum = jnp.where((r2 < blk) == (c2 < HEAD_DIM), 1.0, 0.0).astype(BF16)

    n_pairs = ATT_HEADS // 2
    pairs_per_kv = n_pairs // ATT_KV_HEADS
    units = [(bi, j) for bi in range(ATT_BLOCKS_PER_STEP) for j in range(n_pairs)]

    def kv_blocks(bi, g):
        rows = slice(bi * blk, (bi + 1) * blk)
        prev_rows = slice((bi - 1) * blk, bi * blk)
        lanes = slice(g * LANES, (g + 1) * LANES)
        cur = (kc_ref[rows, lanes], vc_ref[rows, lanes])
        prev = (kp_ref[:, lanes], vp_ref[:, lanes]) if bi == 0 else (kc_ref[prev_rows, lanes],
                                                                      vc_ref[prev_rows, lanes])
        return prev, cur

    scores = {}
    for bi, j in units:
        (kp, _), (kc, _) = kv_blocks(bi, j // pairs_per_kv)
        qp = q_ref[bi * blk:(bi + 1) * blk, j * LANES:(j + 1) * LANES]
        for p in range(2):
            scores[bi, j, p] = (_dot_nt(qp, kp * half[p]), _dot_nt(qp, kc * half[p]))

    probs, sink_terms = {}, {}
    for bi, j in units:
        for p in range(2):
            h = 2 * j + p
            slope = 2.0 ** (-8.0 * (h + 1) / ATT_HEADS)
            s_prev, s_cur = scores[bi, j, p]
            s = jnp.where(from_prev, s_prev, s_cur) - slope * dist
            if bi == 0:
                s = s + no_prev
            sink = sinks_ref[h]
            m = jnp.maximum(jnp.max(s, axis=-1, keepdims=True), sink)
            probs[bi, j, p] = jnp.exp(s - m)
            sink_terms[bi, j, p] = jnp.exp(sink - m)

    outs = {}
    for bi, j in units:
        (_, vp), (_, vc) = kv_blocks(bi, j // pairs_per_kv)
        v_stack = jnp.concatenate([vp * half[0], vp * half[1], vc * half[0], vc * half[1]], axis=0)
        e = [probs[bi, j, p] for p in range(2)]
        p_prev = [jnp.where(from_prev, x, 0.0).astype(BF16) for x in e]
        p_cur = [jnp.where(from_prev, 0.0, x).astype(BF16) for x in e]
        pv = _dot(jnp.concatenate(p_prev + p_cur, axis=1), v_stack)
        den = _dot(jnp.concatenate([x.astype(BF16) for x in e], axis=1), half_sum)
        den = den + jnp.where(low, sink_terms[bi, j, 0], sink_terms[bi, j, 1])
        outs[bi, j] = pv / den

    for bi in range(ATT_BLOCKS_PER_STEP):
        o = jnp.concatenate([outs[bi, j] for j in range(n_pairs)], axis=1)
        ms = jnp.mean(o * o, axis=-1, keepdims=True)
        o_ref[bi * blk:(bi + 1) * blk, :] = (o * lax.rsqrt(ms + EPS) * gatt_ref[...]).astype(BF16)


def _swa_attention(sinks, qa, ka, va, gatt, *, B, T):
    step_rows = ATT_BLOCKS_PER_STEP * WINDOW
    nb = T // step_rows
    cur = lambda c: pl.BlockSpec((step_rows, c), lambda b, i: (b * nb + i, 0))
    prev = lambda c: pl.BlockSpec(
        (WINDOW, c), lambda b, i: (b * (T // WINDOW) + jnp.maximum(i * ATT_BLOCKS_PER_STEP - 1, 0), 0))
    return pl.pallas_call(
        _attn_kernel,
        grid=(B, nb),
        in_specs=[pl.BlockSpec(memory_space=pltpu.SMEM),
                  cur(ATT_Q), cur(2 * ATT_KV), cur(2 * ATT_KV), prev(2 * ATT_KV), prev(2 * ATT_KV),
                  pl.BlockSpec((1, ATT_Q), lambda b, i: (0, 0))],
        out_specs=cur(ATT_Q),
        out_shape=jax.ShapeDtypeStruct((B * T, ATT_Q), BF16),
        compiler_params=_params("arbitrary", "arbitrary"),
        name="swa_attn",
    )(sinks, qa, ka, va, ka, va, gatt)


def _gla_kernel(q_ref, k_ref, v_ref, la_ref, og_ref, tri_ref, bdtri_ref, qmask_ref, bdms_ref, smask_ref,
                ggla_ref, o_ref, state_ref):
    L = GLA_CHUNK

    @pl.when(pl.program_id(1) == 0)
    def _():
        state_ref[...] = jnp.zeros_like(state_ref)

    tri, bdtri = tri_ref[...], bdtri_ref[...]
    rows = lax.broadcasted_iota(jnp.int32, (L, LANES), 0)
    causal = lax.broadcasted_iota(jnp.int32, (L, L), 0) >= lax.broadcasted_iota(jnp.int32, (L, L), 1)
    seqs = range(GLA_BATCH_PER_STEP)
    units = [(s, pair) for s in seqs for pair in range(GLA_HEADS // 2)]
    kl = lambda pair: slice(pair * LANES, (pair + 1) * LANES)
    vl = lambda pair: slice(pair * 2 * GLA_DV, (pair + 1) * 2 * GLA_DV)

    b, b_in = {}, {}
    for s in seqs:
        la = la_ref[s]
        la_hi = la.astype(BF16)
        la_lo = (la - la_hi.astype(F32)).astype(BF16)
        b[s] = _dot(tri, la_hi) + _dot(tri, la_lo)
        b_in[s] = _dot(bdtri, la_hi) + _dot(bdtri, la_lo)

    q_both, keys, q_dec, k_dec_t, b_last = {}, {}, {}, {}, {}
    for s in seqs:
        ref = b[s] - b_in[s]
        b_last[s] = b[s][L - 1:L, :]
        q = q_ref[s].astype(F32)
        k = k_ref[s].astype(F32)
        q_in = (q * jnp.exp(b_in[s])).astype(BF16)
        q_dec[s] = (q * jnp.exp(b[s])).astype(BF16)
        k_dec = k * jnp.exp(b_last[s] - b[s])
        for pair in range(GLA_HEADS // 2):
            k_p, b_p, ref_p = k[:, kl(pair)], b[s][:, kl(pair)], ref[:, kl(pair)]
            expanded = []
            for g in range(N_SUB):
                last_row = (g + 1) * GLA_SUB - 1
                expo = jnp.where(rows <= last_row, ref_p[g * GLA_SUB:g * GLA_SUB + 1, :] - b_p, -1e4)
                expanded.append((k_p * jnp.exp(expo)).astype(BF16))
            keys[s, pair] = jnp.concatenate(expanded, axis=1)
            q_rep = jnp.concatenate([q_in[:, kl(pair)]] * N_SUB, axis=1)
            q_both[s, pair] = jnp.concatenate([q_rep * qmask_ref[0], q_rep * qmask_ref[1]], axis=0)
            k_dec_t[s, pair] = k_dec[:, kl(pair)].T.astype(BF16)

    scores = {u: _dot_nt(q_both[u], keys[u]) for u in units}

    outs, updates = {}, {}
    for s, pair in units:
        v_p = v_ref[s, :, vl(pair)]
        o_parts = []
        for hh in range(2):
            a = jnp.where(causal, scores[s, pair][hh * L:(hh + 1) * L, :], 0.0).astype(BF16)
            o_parts.append(_dot(a, v_p[:, hh * GLA_DV:(hh + 1) * GLA_DV]))
        state = state_ref[s, pair]
        outs[s, pair] = jnp.concatenate(o_parts, axis=1) + _dot(q_dec[s][:, kl(pair)], state.astype(BF16))
        updates[s, pair] = _dot(k_dec_t[s, pair], v_p)

    for s, pair in units:
        decay = jnp.broadcast_to(jnp.exp(b_last[s][:, kl(pair)]), (LANES, LANES)).T
        state_ref[s, pair] = (state_ref[s, pair] * jnp.concatenate([decay, decay], axis=1)
                              + updates[s, pair] * smask_ref[...])
        o = outs[s, pair]
        ms = _dot((o * o).astype(BF16), bdms_ref[...])
        y = o * lax.rsqrt(ms + EPS) * ggla_ref[:, vl(pair)] * og_ref[s, :, vl(pair)].astype(F32)
        o_ref[s, :, vl(pair)] = y.astype(BF16)


def _gla(qg, kg, vg, la, og, tri, bdtri, qmask, bdms, smask, ggla, *, B, T):
    L = GLA_CHUNK
    nb = GLA_BATCH_PER_STEP
    seq = lambda a: a.reshape(B, T, a.shape[-1])
    rows = lambda c: pl.BlockSpec((nb, L, c), lambda b, i: (b, i, 0))
    const = lambda a: pl.BlockSpec(a.shape, lambda b, i: (0,) * a.ndim)
    out = pl.pallas_call(
        _gla_kernel,
        grid=(B // nb, T // L),
        in_specs=[rows(GLA_K), rows(GLA_K), rows(GLA_V), rows(GLA_K), rows(GLA_V),
                  const(tri), const(bdtri), const(qmask), const(bdms), const(smask), const(ggla)],
        out_specs=rows(GLA_V),
        out_shape=jax.ShapeDtypeStruct((B, T, GLA_V), BF16),
        scratch_shapes=[pltpu.VMEM((nb, GLA_HEADS // 2, LANES, 2 * GLA_DV), F32)],
        compiler_params=_params("arbitrary", "arbitrary"),
        name="gla",
    )(seq(qg), seq(kg), seq(vg), seq(la), seq(og), tri, bdtri, qmask, bdms, smask, ggla)
    return out.reshape(B * T, GLA_V)


def _route(logits):
    lane = lax.broadcasted_iota(jnp.int32, logits.shape, 1)
    neg_inf = -jnp.inf
    g_log = jnp.where(lane < N_GROUPS, logits, neg_inf)
    g_max = jnp.max(g_log, axis=-1, keepdims=True)
    g_sel = jnp.min(jnp.where(g_log == g_max, lane, LANES), axis=-1, keepdims=True)
    g_sum = jnp.sum(jnp.where(lane < N_GROUPS, jnp.exp(logits - g_max), 0.0), axis=-1, keepdims=True)
    p_group = 1.0 / g_sum
    e_lo = ROUTER_LANE0 + EXPERTS_PER_GROUP * g_sel
    in_group = jnp.logical_and(lane >= e_lo, lane < e_lo + EXPERTS_PER_GROUP)
    e_log = jnp.where(in_group, logits, neg_inf)
    e_max = jnp.max(e_log, axis=-1, keepdims=True)
    top1 = jnp.min(jnp.where(e_log == e_max, lane, LANES), axis=-1, keepdims=True)
    e_log2 = jnp.where(lane == top1, neg_inf, e_log)
    e_max2 = jnp.max(e_log2, axis=-1, keepdims=True)
    top2 = jnp.min(jnp.where(e_log2 == e_max2, lane, LANES), axis=-1, keepdims=True)
    ratio = jnp.exp(e_max2 - e_max)
    w_top1 = p_group / (1.0 + ratio)
    w_top2 = p_group * ratio / (1.0 + ratio)
    weights = jnp.where(lane == top1 - e_lo, w_top1, 0.0) + jnp.where(lane == top2 - e_lo, w_top2, 0.0)
    return g_sel, weights


def _outproj_kernel(ya_ref, yg_ref, x_ref, mod_ref, wo_ref, g2_ref, wr_ref, br_ref, stril_ref,
                    x1_ref, row_ref, lpos_ref, cnt_ref):
    tm = MOE_TILE
    subs = range(x_ref.shape[0] // tm)
    rows = lambda t: slice(t * tm, (t + 1) * tm)

    mix = [_dot(ya_ref[rows(t), :], wo_ref[0:ATT_Q, :]) + _dot(yg_ref[rows(t), :], wo_ref[ATT_Q:ATT_Q + GLA_V, :])
           for t in subs]
    h2b = []
    for t in subs:
        x1 = x_ref[rows(t), :] + mod_ref[2:3, :] * mix[t]
        x1_ref[rows(t), :] = x1
        ms = jnp.mean(x1 * x1, axis=-1, keepdims=True)
        h2 = (x1 * lax.rsqrt(ms + EPS) * g2_ref[...]) * (1.0 + mod_ref[4:5, :]) + mod_ref[3:4, :]
        h2b.append(h2.astype(BF16))
    logits = [_dot(h2b[t], wr_ref[...]) + br_ref[...] for t in subs]

    routed = [_route(logits[t]) for t in subs]
    lane = lax.broadcasted_iota(jnp.int32, (tm, LANES), 1)
    onehot = [jnp.where(lane == routed[t][0], 1.0, 0.0) for t in subs]
    before = [_dot(stril_ref[...], onehot[t].astype(BF16)) for t in subs]

    lane8 = lax.broadcasted_iota(jnp.int32, (8, LANES), 1)
    local_row = lax.broadcasted_iota(jnp.int32, (LOCAL_ROWS, tm), 0).astype(F32)
    for t in subs:
        count = jnp.sum(onehot[t], axis=0, keepdims=True)
        padded = jnp.broadcast_to(jnp.floor((count + (CHUNK - 1.0)) * (1.0 / CHUNK)) * CHUNK, (8, LANES))
        start = jnp.zeros((8, LANES), F32)
        for shift in range(1, N_GROUPS):
            start = start + jnp.where(lane8 >= shift, pltpu.roll(padded, shift, axis=1), 0.0)
        lpos = jnp.sum(onehot[t] * (before[t] + start[0:1, :]), axis=-1, keepdims=True)
        lpos_b = jnp.broadcast_to(lpos, (tm, LANES))
        lpos_ref[rows(t), :] = lpos_b
        cnt_ref[t] = count
        weights = routed[t][1]
        w_hi = weights.astype(BF16)
        w_lo = (weights - w_hi.astype(F32)).astype(BF16)
        perm = jnp.where(local_row == lpos_b.T[0:1, :], 1.0, 0.0).astype(BF16)
        row_ref[t * LOCAL_ROWS:(t + 1) * LOCAL_ROWS, :] = _dot(
            perm, jnp.concatenate([h2b[t], w_hi, w_lo], axis=1)).astype(BF16)


def _out_proj(ya, yg, x2, mod, wo, g2, wr, br, stril, *, T):
    N, D = x2.shape
    subs = OUT_SUBS
    tm = subs * MOE_TILE
    per_b = T // tm
    rows = lambda c: pl.BlockSpec((tm, c), lambda i: (i, 0))
    const = lambda a: pl.BlockSpec(a.shape, lambda i: (0,) * a.ndim)
    return pl.pallas_call(
        _outproj_kernel,
        grid=(N // tm,),
        in_specs=[rows(ATT_Q), rows(GLA_V), rows(D),
                  pl.BlockSpec((None, 6, D), lambda i: (i // per_b, 0, 0)),
                  const(wo), const(g2), const(wr), const(br), const(stril)],
        out_specs=[rows(D), pl.BlockSpec((subs * LOCAL_ROWS, D + 2 * LANES), lambda i: (i, 0)), rows(LANES),
                   pl.BlockSpec((subs, 1, LANES), lambda i: (i, 0, 0))],
        out_shape=[jax.ShapeDtypeStruct((N, D), F32),
                   jax.ShapeDtypeStruct((N // MOE_TILE * LOCAL_ROWS, D + 2 * LANES), BF16),
                   jax.ShapeDtypeStruct((N, LANES), F32),
                   jax.ShapeDtypeStruct((N // MOE_TILE, 1, LANES), F32)],
        compiler_params=_params("arbitrary"),
        name="out_proj",
    )(ya, yg, x2, mod, wo, g2, wr, br, stril)


def _chunk_copy(src_ref, src_chunk, dst_ref, dst_chunk, sem):
    src = src_ref.at[pl.ds(pl.multiple_of(src_chunk * CHUNK, CHUNK), CHUNK)]
    dst = dst_ref.at[pl.ds(pl.multiple_of(dst_chunk * CHUNK, CHUNK), CHUNK)]
    return pltpu.make_async_copy(src, dst, sem)


def _moe_kernel(src_ref, nvalid_ref, grp_ref, nt_ref, used_ref, rows_ref, w1_ref, w3_ref, w2_ref, y_ref,
                in_buf, out_buf, zero_buf, in_sem, out_sem, zero_sem):
    del grp_ref
    j = pl.program_id(0)
    n_tiles = nt_ref[0]
    d_model = w2_ref.shape[2]

    def gather(tile, slot):
        def body(k, carry):
            _chunk_copy(rows_ref, src_ref[tile * TILE_CHUNKS + k], in_buf.at[slot], k, in_sem.at[slot]).start()
            return carry
        lax.fori_loop(0, TILE_CHUNKS, body, 0, unroll=True)

    def wait_gather(slot):
        def body(k, carry):
            _chunk_copy(rows_ref, 0, in_buf.at[slot], k, in_sem.at[slot]).wait()
            return carry
        lax.fori_loop(0, TILE_CHUNKS, body, 0, unroll=True)

    def scatter(tile, slot):
        def body(k, carry):
            _chunk_copy(out_buf.at[slot], k, y_ref, src_ref[tile * TILE_CHUNKS + k], out_sem.at[slot]).start()
            return carry
        lax.fori_loop(0, nvalid_ref[tile], body, 0)

    def wait_scatter(tile, slot):
        def body(k, carry):
            _chunk_copy(out_buf.at[slot], k, y_ref, 0, out_sem.at[slot]).wait()
            return carry
        lax.fori_loop(0, nvalid_ref[tile], body, 0)

    def zero_fill(wait):
        def per_tile(i, carry):
            def body(c, inner):
                copy = _chunk_copy(zero_buf, 0, y_ref, i * LOCAL_CHUNKS + c, zero_sem)
                if wait:
                    copy.wait()
                else:
                    copy.start()
                return inner
            return lax.fori_loop(used_ref[i], LOCAL_CHUNKS, body, carry)
        lax.fori_loop(0, used_ref.shape[0], per_tile, 0)

    @pl.when(j == 0)
    def _():
        zero_buf[...] = jnp.zeros_like(zero_buf)
        zero_fill(wait=False)
        gather(0, 0)

    @pl.when(j + 1 < n_tiles)
    def _():
        gather(j + 1, (j + 1) % 2)

    @pl.when(j < n_tiles)
    def _():
        slot = j % 2
        wait_gather(slot)

        @pl.when(j >= 2)
        def _():
            wait_scatter(j - 2, slot)

        rows = in_buf[slot]
        h = rows[:, 0:d_model]
        weights = rows[:, d_model:d_model + LANES].astype(F32) + rows[:, d_model + LANES:].astype(F32)
        experts = range(EXPERTS_PER_GROUP)
        up = [(_dot(h, w1_ref[k]), _dot(h, w3_ref[k])) for k in experts]
        hid = [(a * _sigmoid(a) * g * weights[:, k:k + 1]).astype(BF16) for k, (a, g) in zip(experts, up)]
        y = _dot(hid[0], w2_ref[0])
        for k in experts[1:]:
            y = y + _dot(hid[k], w2_ref[k])
        out_buf[slot] = y.astype(BF16)
        scatter(j, slot)

        @pl.when(j == n_tiles - 1)
        def _():
            @pl.when(j >= 1)
            def _():
                wait_scatter(j - 1, 1 - slot)
            wait_scatter(j, slot)
            zero_fill(wait=True)


def _combine_kernel(x1_ref, mod_ref, lpos_ref, y_ref, o_ref):
    tm = x1_ref.shape[0]
    local_row = lax.broadcasted_iota(jnp.int32, (tm, LOCAL_ROWS), 1).astype(F32)
    unsort = jnp.where(local_row == lpos_ref[:, 0:1], 1.0, 0.0).astype(BF16)
    o_ref[...] = x1_ref[...] + mod_ref[5:6, :] * _dot(unsort, y_ref[...])


def _moe_plan(cnt):
    n_local = cnt.shape[0]
    chunks = (cnt + CHUNK - 1) // CHUNK
    used = jnp.sum(chunks, axis=1)
    local_off = jnp.cumsum(chunks, axis=1) - chunks
    tiles_g = (jnp.sum(chunks, axis=0) + TILE_CHUNKS - 1) // TILE_CHUNKS
    tile_end = jnp.cumsum(tiles_g)
    n_tiles = tile_end[-1]
    group_start = (tile_end - tiles_g) * TILE_CHUNKS
    seg_len = chunks.T.reshape(-1)
    seg_start = (group_start[:, None] + (jnp.cumsum(chunks, axis=0) - chunks).T).reshape(-1)
    seg_src = (jnp.arange(n_local)[None, :] * LOCAL_CHUNKS + local_off.T).reshape(-1)
    max_chunks = n_local * MOE_TILE // CHUNK + n_local * N_GROUPS + N_GROUPS * TILE_CHUNKS
    max_tiles = (max_chunks + TILE_CHUNKS - 1) // TILE_CHUNKS
    c = jnp.arange(max_tiles * TILE_CHUNKS)[:, None]
    within = c - seg_start[None, :]
    hit = jnp.logical_and(within >= 0, within < seg_len[None, :])
    valid = jnp.any(hit, axis=1)
    src = jnp.sum(jnp.where(hit, seg_src[None, :] + within, 0), axis=1)
    src = jnp.where(valid, src, LOCAL_CHUNKS - 1)
    nvalid = jnp.sum(valid.reshape(max_tiles, TILE_CHUNKS), axis=1)
    j = jnp.minimum(jnp.arange(max_tiles), n_tiles - 1)
    grp = jnp.sum(j[:, None] >= tile_end[None, :], axis=1)
    i32 = lambda a: a.astype(jnp.int32)
    return i32(src), i32(nvalid), i32(grp), i32(n_tiles).reshape(1), i32(used)


def _moe(plan, rows_local, w1g, w3g, w2g):
    src, nvalid, grp, n_tiles, used = plan
    D = w2g.shape[3]
    weights = lambda a: pl.BlockSpec((None,) + a.shape[1:], lambda j, src, nv, grp, nt, used: (grp[j], 0, 0, 0))
    return pl.pallas_call(
        _moe_kernel,
        grid_spec=pltpu.PrefetchScalarGridSpec(
            num_scalar_prefetch=5,
            grid=(grp.shape[0],),
            in_specs=[pl.BlockSpec(memory_space=pl.ANY), weights(w1g), weights(w3g), weights(w2g)],
            out_specs=pl.BlockSpec(memory_space=pl.ANY),
            scratch_shapes=[pltpu.VMEM((2, MOE_TILE, rows_local.shape[1]), BF16),
                            pltpu.VMEM((2, MOE_TILE, D), BF16),
                            pltpu.VMEM((CHUNK, D), BF16),
                            pltpu.SemaphoreType.DMA((2,)), pltpu.SemaphoreType.DMA((2,)),
                            pltpu.SemaphoreType.DMA(())]),
        out_shape=jax.ShapeDtypeStruct((rows_local.shape[0], D), BF16),
        compiler_params=_params("arbitrary"),
        name="moe",
    )(src, nvalid, grp, n_tiles, used, rows_local, w1g, w3g, w2g)


def _combine(x1, mod, lpos, y_local, *, T):
    N, D = x1.shape
    per_b = T // MOE_TILE
    return pl.pallas_call(
        _combine_kernel,
        grid=(N // MOE_TILE,),
        in_specs=[pl.BlockSpec((MOE_TILE, D), lambda i: (i, 0)),
                  pl.BlockSpec((None, 6, D), lambda i: (i // per_b, 0, 0)),
                  pl.BlockSpec((MOE_TILE, LANES), lambda i: (i, 0)),
                  pl.BlockSpec((LOCAL_ROWS, D), lambda i: (i, 0))],
        out_specs=pl.BlockSpec((MOE_TILE, D), lambda i: (i, 0)),
        out_shape=jax.ShapeDtypeStruct((N, D), F32),
        compiler_params=_params("arbitrary"),
        name="moe_combine",
    )(x1, mod, lpos, y_local)


def _block_diag(n, blk, value, dtype):
    r = jnp.arange(n)[:, None] // blk
    c = jnp.arange(n)[None, :] // blk
    return jnp.where(r == c, value, 0.0).astype(dtype)


def _gla_constants():
    L = GLA_CHUNK
    i = jnp.arange(L)[:, None]
    j = jnp.arange(L)[None, :]
    tri = (j <= i).astype(BF16)
    bdtri = jnp.logical_and(j <= i, i // GLA_SUB == j // GLA_SUB).astype(BF16)
    col = jnp.arange(N_SUB * LANES)[None, :]
    qmask = jnp.stack([jnp.logical_and(col // LANES == i // GLA_SUB, (col % LANES) // GLA_DK == hh)
                       for hh in range(2)]).astype(BF16)
    bdms = _block_diag(2 * GLA_DV, GLA_DV, 1.0 / GLA_DV, BF16)
    d = jnp.arange(LANES)[:, None] // GLA_DK
    e = jnp.arange(2 * GLA_DV)[None, :] // GLA_DV
    smask = (d == e).astype(F32)
    return tri, bdtri, qmask, bdms, smask


def kernel(x, c, w_ada, b_ada, g_norm1, w_in, q_norm, k_norm, sinks, w_gk2, b_gk, g_gla_out, g_att_out,
           w_out, g_norm2, w_group, b_group, w_router, b_router, w1, w3, w2):
    B, T, D = x.shape
    N = B * T
    depth = w_ada.shape[0]
    tri, bdtri, qmask, bdms, smask = _gla_constants()
    bdq = _block_diag(ATT_Q, HEAD_DIM, 1.0 / HEAD_DIM, BF16)
    bdk = _block_diag(ATT_KV, HEAD_DIM, 1.0 / HEAD_DIM, BF16)

    x2 = x.reshape(N, D)
    for l in range(depth):
        mod = _adaln_mod(c, w_ada[l], b_ada[l]).reshape(B, 6, D)

        wl = w_in[l]
        lr0 = ATT_Q + 2 * ATT_KV + 2 * GLA_K + GLA_V
        w_in_p = jnp.concatenate(
            [wl[:, :lr0], wl[:, lr0 + GLA_RANK:], wl[:, lr0:lr0 + GLA_RANK],
             jnp.zeros((D, LANES - GLA_RANK), wl.dtype)], axis=1).astype(BF16)
        wgk = jnp.concatenate([w_gk2[l], jnp.zeros((LANES - GLA_RANK, GLA_K), F32)], axis=0).astype(BF16)
        qa, ka, va, qg, kg, vg, la, og = _in_proj(
            x2, mod, g_norm1[l].reshape(1, D), w_in_p,
            jnp.tile(q_norm[l], ATT_HEADS).reshape(1, ATT_Q), jnp.tile(k_norm[l], ATT_KV_HEADS).reshape(1, ATT_KV),
            bdq, bdk, wgk, b_gk[l].reshape(1, GLA_K), T=T, tm=PROJ_TILE)

        y_att = _swa_attention(sinks[l], qa, ka, va, g_att_out[l].reshape(1, ATT_Q), B=B, T=T)
        y_gla = _gla(qg, kg, vg, la, og, tri, bdtri, qmask, bdms, smask,
                     jnp.tile(g_gla_out[l], GLA_HEADS).reshape(1, GLA_V), B=B, T=T)

        pad = LANES - N_GROUPS - N_EXPERTS
        wr = jnp.concatenate([w_group[l], w_router[l], jnp.zeros((D, pad), F32)], axis=1).astype(BF16)
        br = jnp.concatenate([b_group[l], b_router[l], jnp.zeros((pad,), F32)]).reshape(1, LANES)
        stril = (jnp.arange(MOE_TILE)[None, :] < jnp.arange(MOE_TILE)[:, None]).astype(BF16)
        x1, rows_local, lpos, cnt = _out_proj(y_att, y_gla, x2, mod, w_out[l].astype(BF16),
                                              g_norm2[l].reshape(1, D), wr, br, stril, T=T)
        plan = _moe_plan(cnt[:, 0, :N_GROUPS].astype(jnp.int32))
        by_group = lambda w: w.astype(BF16).reshape((N_GROUPS, EXPERTS_PER_GROUP) + w.shape[1:])
        y_local = _moe(plan, rows_local, by_group(w1[l]), by_group(w3[l]), by_group(w2[l]))
        x2 = _combine(x1, mod, lpos, y_local, T=T)
    return x2.reshape(B, T, D)
```

```python
import functools

import jax
import jax.numpy as jnp
from jax import lax
from jax.experimental import pallas as pl
from jax.experimental.pallas import tpu as pltpu

F32 = jnp.float32
BF16 = jnp.bfloat16

EPS = 1e-6
ATT_HEADS = 8
ATT_KV_HEADS = 2
HEAD_DIM = 64
WINDOW = 128
ATT_Q = ATT_HEADS * HEAD_DIM
ATT_KV = ATT_KV_HEADS * HEAD_DIM
GLA_HEADS = 4
GLA_DK = 64
GLA_DV = 128
GLA_RANK = 16
GLA_NORMALIZER = 16.0
GLA_K = GLA_HEADS * GLA_DK
GLA_V = GLA_HEADS * GLA_DV
N_GROUPS = 4
EXPERTS_PER_GROUP = 4
N_EXPERTS = N_GROUPS * EXPERTS_PER_GROUP
D_EXPERT = 256

LANES = 128
PROJ_SUB = 512
PROJ_TILE = 2 * PROJ_SUB
OUT_SUBS = 2
COMBINE_SUBS = 4
ATT_BLOCKS_PER_STEP = 4
GLA_BATCH_PER_STEP = 4
GLA_CHUNK = 128
GLA_SUB = 16
N_SUB = GLA_CHUNK // GLA_SUB
ROUTER_LANE0 = N_GROUPS
VMEM_LIMIT = 56 * 1024 * 1024
MOE_TILE = 256
CHUNK = 16
TILE_CHUNKS = MOE_TILE // CHUNK
LOCAL_CHUNKS = (MOE_TILE + N_GROUPS * (CHUNK - 1)) // CHUNK + 2
LOCAL_ROWS = LOCAL_CHUNKS * CHUNK

_QA0, _KA0, _VA0 = 0, ATT_Q, ATT_Q + ATT_KV
_QG0 = _VA0 + ATT_KV
_KG0 = _QG0 + GLA_K
_VG0 = _KG0 + GLA_K
_OG0 = _VG0 + GLA_V
_LR0 = _OG0 + GLA_V
IN_COLS_PAD = _LR0 + LANES


def _dot(a, b):
    return jnp.dot(a, b, preferred_element_type=F32)


def _dot_nt(a, b):
    return lax.dot_general(a, b, (((1,), (1,)), ((), ())), preferred_element_type=F32)


def _sigmoid(x):
    return 1.0 / (1.0 + jnp.exp(-x))


def _params(*sem):
    return pltpu.CompilerParams(dimension_semantics=sem, vmem_limit_bytes=VMEM_LIMIT)


def _mod_kernel(c_ref, w_ref, b_ref, o_ref):
    c = c_ref[...]
    s = (c * _sigmoid(c)).astype(BF16)
    o_ref[...] = _dot(s, w_ref[...].astype(BF16)) + b_ref[...]


def _adaln_mod(c, w_ada, b_ada):
    B, D = c.shape
    n = w_ada.shape[1]
    tn = 1536
    return pl.pallas_call(
        _mod_kernel,
        grid=(n // tn,),
        in_specs=[pl.BlockSpec((B, D), lambda j: (0, 0)),
                  pl.BlockSpec((D, tn), lambda j: (0, j)),
                  pl.BlockSpec((1, tn), lambda j: (0, j))],
        out_specs=pl.BlockSpec((B, tn), lambda j: (0, j)),
        out_shape=jax.ShapeDtypeStruct((B, n), F32),
        compiler_params=_params("arbitrary"),
        name="adaln_mod",
    )(c, w_ada, b_ada.reshape(1, n))


def _inproj_kernel(x_ref, mod_ref, g1_ref, w_ref, qn_ref, kn_ref, bdq_ref, bdk_ref, wgk_ref, bgk_ref,
                   qa_ref, ka_ref, va_ref, qg_ref, kg_ref, vg_ref, la_ref, og_ref):
    subs = range(x_ref.shape[0] // PROJ_SUB)
    rows = lambda t: slice(t * PROJ_SUB, (t + 1) * PROJ_SUB)

    h = []
    for t in subs:
        x = x_ref[rows(t), :]
        ms = jnp.mean(x * x, axis=-1, keepdims=True)
        xn = x * lax.rsqrt(ms + EPS) * g1_ref[...]
        h.append((xn * (1.0 + mod_ref[1:2, :]) + mod_ref[0:1, :]).astype(BF16))

    proj = lambda t, c0, width: _dot(h[t], w_ref[:, c0:c0 + width])
    gate_of = lambda lr: _dot(lr.astype(BF16), wgk_ref[...]) + bgk_ref[...]
    qa, kv, q_ms, k_ms, qk_g, vg, og, lr, gate = ({} for _ in range(9))
    for t in subs:
        qa[t] = proj(t, _QA0, ATT_Q)
        if t > 0:
            gate[t - 1] = gate_of(lr[t - 1])
        kv[t] = proj(t, _KA0, 2 * ATT_KV)
        q_sq = (qa[t] * qa[t]).astype(BF16)
        q_ms[t] = jnp.concatenate([_dot(q_sq[:, c:c + 2 * LANES], bdq_ref[...])
                                   for c in range(0, ATT_Q, 2 * LANES)], axis=1)
        qk_g[t] = proj(t, _QG0, 2 * GLA_K)
        k = kv[t][:, 0:ATT_KV]
        k_ms[t] = _dot((k * k).astype(BF16), bdk_ref[...])
        vg[t] = proj(t, _VG0, GLA_V)
        og[t] = proj(t, _OG0, GLA_V)
        lr[t] = proj(t, _LR0, LANES)
    gate[subs[-1]] = gate_of(lr[subs[-1]])

    low = lax.broadcasted_iota(jnp.int32, (PROJ_SUB, ATT_KV), 1) < HEAD_DIM
    for t in subs:
        qa_ref[rows(t), :] = (qa[t] * lax.rsqrt(q_ms[t] + EPS) * qn_ref[...] * (HEAD_DIM ** -0.5)).astype(BF16)
        k = kv[t][:, 0:ATT_KV] * lax.rsqrt(k_ms[t] + EPS) * kn_ref[...]
        v = kv[t][:, ATT_KV:2 * ATT_KV]
        for src, dst in ((k, ka_ref), (v, va_ref)):
            swapped = pltpu.roll(src, HEAD_DIM, axis=1)
            dst[rows(t), 0:LANES] = jnp.where(low, src, swapped).astype(BF16)
            dst[rows(t), LANES:2 * LANES] = jnp.where(low, swapped, src).astype(BF16)
        qg_ref[rows(t), :] = (qk_g[t][:, 0:GLA_K] * (GLA_DK ** -0.5)).astype(BF16)
        kg_ref[rows(t), :] = qk_g[t][:, GLA_K:2 * GLA_K].astype(BF16)
        vg_ref[rows(t), :] = vg[t].astype(BF16)
        og_ref[rows(t), :] = (og[t] * _sigmoid(og[t])).astype(BF16)
        log_sig = jnp.minimum(gate[t], 0.0) - jnp.log(1.0 + jnp.exp(-jnp.abs(gate[t])))
        la_ref[rows(t), :] = log_sig * (1.0 / GLA_NORMALIZER)


def _in_proj(x2, mod, g1, w_in_p, qn, kn, bdq, bdk, wgk, bgk, *, T, tm):
    N, D = x2.shape
    per_b = T // tm
    const = lambda shape: pl.BlockSpec(shape, lambda i: (0,) * len(shape))
    rows = lambda c: pl.BlockSpec((tm, c), lambda i: (i, 0))
    outs = [(ATT_Q, BF16), (2 * ATT_KV, BF16), (2 * ATT_KV, BF16), (GLA_K, BF16), (GLA_K, BF16),
            (GLA_V, BF16), (GLA_K, F32), (GLA_V, BF16)]
    return pl.pallas_call(
        _inproj_kernel,
        grid=(N // tm,),
        in_specs=[rows(D),
                  pl.BlockSpec((None, 6, D), lambda i: (i // per_b, 0, 0)),
                  const((1, D)), const(w_in_p.shape), const(qn.shape), const(kn.shape),
                  const(bdq.shape), const(bdk.shape), const(wgk.shape), const(bgk.shape)],
        out_specs=[rows(c) for c, _ in outs],
        out_shape=[jax.ShapeDtypeStruct((N, c), dt) for c, dt in outs],
        compiler_params=_params("arbitrary"),
        name="in_proj",
    )(x2, mod, g1, w_in_p, qn, kn, bdq, bdk, wgk, bgk)


def _attn_kernel(sinks_ref, q_ref, kc_ref, vc_ref, kp_ref, vp_ref, gatt_ref, o_ref):
    blk = WINDOW
    first = pl.program_id(1) == 0
    qi = lax.broadcasted_iota(jnp.int32, (blk, blk), 0)
    cj = lax.broadcasted_iota(jnp.int32, (blk, blk), 1)
    from_prev = cj > qi
    dist = (qi - cj + jnp.where(from_prev, blk, 0)).astype(F32)
    no_prev = jnp.where(jnp.logical_and(from_prev, first), -1e30, 0.0)
    low = cj < HEAD_DIM
    half = (jnp.where(low, 1.0, 0.0).astype(BF16), jnp.where(low, 0.0, 1.0).astype(BF16))
    half2 = tuple(jnp.concatenate([m, m], axis=0) for m in half)
    prev_mask = jnp.where(from_prev, 1.0, 0.0).astype(BF16)
    cur_mask = jnp.where(from_prev, 0.0, 1.0).astype(BF16)

    n_pairs = ATT_HEADS // 2
    pairs_per_kv = n_pairs // ATT_KV_HEADS
    units = [(bi, j) for bi in range(ATT_BLOCKS_PER_STEP) for j in range(n_pairs)]

    def kv_blocks(bi, g):
        rows = slice(bi * blk, (bi + 1) * blk)
        prev_rows = slice((bi - 1) * blk, bi * blk)
        lanes = slice(g * LANES, (g + 1) * LANES)
        cur = (kc_ref[rows, lanes], vc_ref[rows, lanes])
        prev = (kp_ref[:, lanes], vp_ref[:, lanes]) if bi == 0 else (kc_ref[prev_rows, lanes],
                                                                      vc_ref[prev_rows, lanes])
        return prev, cur

    scores = {}
    for bi, j in units:
        (kp, _), (kc, _) = kv_blocks(bi, j // pairs_per_kv)
        k_both = jnp.concatenate([kp, kc], axis=0)
        qp = q_ref[bi * blk:(bi + 1) * blk, j * LANES:(j + 1) * LANES]
        for p in range(2):
            scores[bi, j, p] = _dot_nt(qp, k_both * half2[p])

    probs, sink_terms = {}, {}
    for bi, j in units:
        for p in range(2):
            h = 2 * j + p
            slope = 2.0 ** (-8.0 * (h + 1) / ATT_HEADS)
            s_both = scores[bi, j, p]
            s = jnp.where(from_prev, s_both[:, 0:blk], s_both[:, blk:2 * blk]) - slope * dist
            if bi == 0:
                s = s + no_prev
            sink = sinks_ref[h]
            m = jnp.maximum(jnp.max(s, axis=-1, keepdims=True), sink)
            probs[bi, j, p] = jnp.exp(s - m)
            sink_terms[bi, j, p] = jnp.exp(sink - m)

    outs = {}
    for bi, j in units:
        (_, vp), (_, vc) = kv_blocks(bi, j // pairs_per_kv)
        v_stack = jnp.concatenate([jnp.concatenate([v * half[p], half[p]], axis=1)
                                   for v in (vp, vc) for p in range(2)], axis=0)
        e = [probs[bi, j, p].astype(BF16) for p in range(2)]
        p_all = jnp.concatenate([x * m for m in (prev_mask, cur_mask) for x in e], axis=1)
        pv = _dot(p_all, v_stack)
        den = pv[:, LANES:2 * LANES] + jnp.where(low, sink_terms[bi, j, 0], sink_terms[bi, j, 1])
        outs[bi, j] = pv[:, 0:LANES] / den

    for bi in range(ATT_BLOCKS_PER_STEP):
        o = jnp.concatenate([outs[bi, j] for j in range(n_pairs)], axis=1)
        ms = jnp.mean(o * o, axis=-1, keepdims=True)
        o_ref[bi * blk:(bi + 1) * blk, :] = (o * lax.rsqrt(ms + EPS) * gatt_ref[...]).astype(BF16)


def _swa_attention(sinks, qa, ka, va, gatt, *, B, T):
    step_rows = ATT_BLOCKS_PER_STEP * WINDOW
    nb = T // step_rows
    cur = lambda c: pl.BlockSpec((step_rows, c), lambda b, i: (b * nb + i, 0))
    prev = lambda c: pl.BlockSpec(
        (WINDOW, c), lambda b, i: (b * (T // WINDOW) + jnp.maximum(i * ATT_BLOCKS_PER_STEP - 1, 0), 0))
    return pl.pallas_call(
        _attn_kernel,
        grid=(B, nb),
        in_specs=[pl.BlockSpec(memory_space=pltpu.SMEM),
                  cur(ATT_Q), cur(2 * ATT_KV), cur(2 * ATT_KV), prev(2 * ATT_KV), prev(2 * ATT_KV),
                  pl.BlockSpec((1, ATT_Q), lambda b, i: (0, 0))],
        out_specs=cur(ATT_Q),
        out_shape=jax.ShapeDtypeStruct((B * T, ATT_Q), BF16),
        compiler_params=_params("arbitrary", "arbitrary"),
        name="swa_attn",
    )(sinks, qa, ka, va, ka, va, gatt)


def _gla_kernel(q_ref, k_ref, v_ref, la_ref, og_ref, tri_ref, bdtri_ref, qmask_ref, bdms_ref, smask_ref,
                ggla_ref, o_ref, state_ref):
    L = GLA_CHUNK

    @pl.when(pl.program_id(1) == 0)
    def _():
        state_ref[...] = jnp.zeros_like(state_ref)

    tri, bdtri = tri_ref[...], bdtri_ref[...]
    rows = lax.broadcasted_iota(jnp.int32, (L, LANES), 0)
    causal = lax.broadcasted_iota(jnp.int32, (L, L), 0) >= lax.broadcasted_iota(jnp.int32, (L, L), 1)
    seqs = range(GLA_BATCH_PER_STEP)
    units = [(s, pair) for s in seqs for pair in range(GLA_HEADS // 2)]
    kl = lambda pair: slice(pair * LANES, (pair + 1) * LANES)
    vl = lambda pair: slice(pair * 2 * GLA_DV, (pair + 1) * 2 * GLA_DV)

    b, b_in = {}, {}
    for s in seqs:
        la = la_ref[s]
        la_hi = la.astype(BF16)
        la_lo = (la - la_hi.astype(F32)).astype(BF16)
        b[s] = _dot(tri, la_hi) + _dot(tri, la_lo)
        b_in[s] = _dot(bdtri, la_hi) + _dot(bdtri, la_lo)

    q_both, keys, q_dec, k_dec_t, b_last = {}, {}, {}, {}, {}
    for s in seqs:
        ref = b[s] - b_in[s]
        b_last[s] = b[s][L - 1:L, :]
        q = q_ref[s].astype(F32)
        k = k_ref[s].astype(F32)
        q_in = (q * jnp.exp(b_in[s])).astype(BF16)
        q_dec[s] = (q * jnp.exp(b[s])).astype(BF16)
        k_dec = k * jnp.exp(b_last[s] - b[s])
        for pair in range(GLA_HEADS // 2):
            k_p, b_p, ref_p = k[:, kl(pair)], b[s][:, kl(pair)], ref[:, kl(pair)]
            expanded = []
            for g in range(N_SUB):
                last_row = (g + 1) * GLA_SUB - 1
                expo = jnp.where(rows <= last_row, ref_p[g * GLA_SUB:g * GLA_SUB + 1, :] - b_p, -1e4)
                expanded.append((k_p * jnp.exp(expo)).astype(BF16))
            keys[s, pair] = jnp.concatenate(expanded, axis=1)
            q_rep = jnp.concatenate([q_in[:, kl(pair)]] * N_SUB, axis=1)
            q_both[s, pair] = jnp.concatenate([q_rep * qmask_ref[0], q_rep * qmask_ref[1]], axis=0)
            k_dec_t[s, pair] = k_dec[:, kl(pair)].T.astype(BF16)

    scores = {u: _dot_nt(q_both[u], keys[u]) for u in units}

    outs, updates = {}, {}
    for s, pair in units:
        v_p = v_ref[s, :, vl(pair)]
        o_parts = []
        for hh in range(2):
            a = jnp.where(causal, scores[s, pair][hh * L:(hh + 1) * L, :], 0.0).astype(BF16)
            o_parts.append(_dot(a, v_p[:, hh * GLA_DV:(hh + 1) * GLA_DV]))
        state = state_ref[s, pair]
        outs[s, pair] = jnp.concatenate(o_parts, axis=1) + _dot(q_dec[s][:, kl(pair)], state.astype(BF16))
        updates[s, pair] = _dot(k_dec_t[s, pair], v_p)

    for s, pair in units:
        decay = jnp.broadcast_to(jnp.exp(b_last[s][:, kl(pair)]), (LANES, LANES)).T
        state_ref[s, pair] = (state_ref[s, pair] * jnp.concatenate([decay, decay], axis=1)
                              + updates[s, pair] * smask_ref[...])
        o = outs[s, pair]
        ms = _dot((o * o).astype(BF16), bdms_ref[...])
        y = o * lax.rsqrt(ms + EPS) * ggla_ref[:, vl(pair)] * og_ref[s, :, vl(pair)].astype(F32)
        o_ref[s, :, vl(pair)] = y.astype(BF16)


def _gla(qg, kg, vg, la, og, tri, bdtri, qmask, bdms, smask, ggla, *, B, T):
    L = GLA_CHUNK
    nb = GLA_BATCH_PER_STEP
    seq = lambda a: a.reshape(B, T, a.shape[-1])
    rows = lambda c: pl.BlockSpec((nb, L, c), lambda b, i: (b, i, 0))
    const = lambda a: pl.BlockSpec(a.shape, lambda b, i: (0,) * a.ndim)
    out = pl.pallas_call(
        _gla_kernel,
        grid=(B // nb, T // L),
        in_specs=[rows(GLA_K), rows(GLA_K), rows(GLA_V), rows(GLA_K), rows(GLA_V),
                  const(tri), const(bdtri), const(qmask), const(bdms), const(smask), const(ggla)],
        out_specs=rows(GLA_V),
        out_shape=jax.ShapeDtypeStruct((B, T, GLA_V), BF16),
        scratch_shapes=[pltpu.VMEM((nb, GLA_HEADS // 2, LANES, 2 * GLA_DV), F32)],
        compiler_params=_params("arbitrary", "arbitrary"),
        name="gla",
    )(seq(qg), seq(kg), seq(vg), seq(la), seq(og), tri, bdtri, qmask, bdms, smask, ggla)
    return out.reshape(B * T, GLA_V)


def _route(logits):
    lane = lax.broadcasted_iota(jnp.int32, logits.shape, 1)
    neg_inf = -jnp.inf
    g_log = jnp.where(lane < N_GROUPS, logits, neg_inf)
    g_max = jnp.max(g_log, axis=-1, keepdims=True)
    g_sel = jnp.min(jnp.where(g_log == g_max, lane, LANES), axis=-1, keepdims=True)
    g_sum = jnp.sum(jnp.where(lane < N_GROUPS, jnp.exp(logits - g_max), 0.0), axis=-1, keepdims=True)
    p_group = 1.0 / g_sum
    e_lo = ROUTER_LANE0 + EXPERTS_PER_GROUP * g_sel
    in_group = jnp.logical_and(lane >= e_lo, lane < e_lo + EXPERTS_PER_GROUP)
    e_log = jnp.where(in_group, logits, neg_inf)
    e_max = jnp.max(e_log, axis=-1, keepdims=True)
    top1 = jnp.min(jnp.where(e_log == e_max, lane, LANES), axis=-1, keepdims=True)
    e_log2 = jnp.where(lane == top1, neg_inf, e_log)
    e_max2 = jnp.max(e_log2, axis=-1, keepdims=True)
    top2 = jnp.min(jnp.where(e_log2 == e_max2, lane, LANES), axis=-1, keepdims=True)
    ratio = jnp.exp(e_max2 - e_max)
    w_top1 = p_group / (1.0 + ratio)
    w_top2 = p_group * ratio / (1.0 + ratio)
    weights = jnp.where(lane == top1 - e_lo, w_top1, 0.0) + jnp.where(lane == top2 - e_lo, w_top2, 0.0)
    return g_sel, weights


def _outproj_kernel(ya_ref, yg_ref, x_ref, mod_ref, wo_ref, g2_ref, wr_ref, br_ref, stril_ref,
                    x1_ref, row_ref, lpos_ref, cnt_ref):
    tm = MOE_TILE
    subs = range(x_ref.shape[0] // tm)
    rows = lambda t: slice(t * tm, (t + 1) * tm)

    mix = [_dot(ya_ref[rows(t), :], wo_ref[0:ATT_Q, :]) + _dot(yg_ref[rows(t), :], wo_ref[ATT_Q:ATT_Q + GLA_V, :])
           for t in subs]
    h2b = []
    for t in subs:
        x1 = x_ref[rows(t), :] + mod_ref[2:3, :] * mix[t]
        x1_ref[rows(t), :] = x1
        ms = jnp.mean(x1 * x1, axis=-1, keepdims=True)
        h2 = (x1 * lax.rsqrt(ms + EPS) * g2_ref[...]) * (1.0 + mod_ref[4:5, :]) + mod_ref[3:4, :]
        h2b.append(h2.astype(BF16))
    logits = [_dot(h2b[t], wr_ref[...]) + br_ref[...] for t in subs]

    routed = [_route(logits[t]) for t in subs]
    lane = lax.broadcasted_iota(jnp.int32, (tm, LANES), 1)
    onehot = [jnp.where(lane == routed[t][0], 1.0, 0.0) for t in subs]
    before = [_dot(stril_ref[...], onehot[t].astype(BF16)) for t in subs]

    lane8 = lax.broadcasted_iota(jnp.int32, (8, LANES), 1)
    local_row = lax.broadcasted_iota(jnp.int32, (LOCAL_ROWS, tm), 0).astype(F32)
    for t in subs:
        count = jnp.sum(onehot[t], axis=0, keepdims=True)
        padded = jnp.broadcast_to(jnp.floor((count + (CHUNK - 1.0)) * (1.0 / CHUNK)) * CHUNK, (8, LANES))
        start = jnp.zeros((8, LANES), F32)
        for shift in range(1, N_GROUPS):
            start = start + jnp.where(lane8 >= shift, pltpu.roll(padded, shift, axis=1), 0.0)
        lpos = jnp.sum(onehot[t] * (before[t] + start[0:1, :]), axis=-1, keepdims=True)
        lpos_b = jnp.broadcast_to(lpos, (tm, LANES))
        lpos_ref[rows(t), :] = lpos_b
        cnt_ref[t] = count
        weights = routed[t][1]
        w_hi = weights.astype(BF16)
        w_lo = (weights - w_hi.astype(F32)).astype(BF16)
        perm = jnp.where(local_row == lpos_b.T[0:1, :], 1.0, 0.0).astype(BF16)
        row_ref[t * LOCAL_ROWS:(t + 1) * LOCAL_ROWS, :] = _dot(
            perm, jnp.concatenate([h2b[t], w_hi, w_lo], axis=1)).astype(BF16)


def _out_proj(ya, yg, x2, mod, wo, g2, wr, br, stril, *, T):
    N, D = x2.shape
    subs = OUT_SUBS
    tm = subs * MOE_TILE
    per_b = T // tm
    rows = lambda c: pl.BlockSpec((tm, c), lambda i: (i, 0))
    const = lambda a: pl.BlockSpec(a.shape, lambda i: (0,) * a.ndim)
    return pl.pallas_call(
        _outproj_kernel,
        grid=(N // tm,),
        in_specs=[rows(ATT_Q), rows(GLA_V), rows(D),
                  pl.BlockSpec((None, 6, D), lambda i: (i // per_b, 0, 0)),
                  const(wo), const(g2), const(wr), const(br), const(stril)],
        out_specs=[rows(D), pl.BlockSpec((subs * LOCAL_ROWS, D + 2 * LANES), lambda i: (i, 0)), rows(LANES),
                   pl.BlockSpec((subs, 1, LANES), lambda i: (i, 0, 0))],
        out_shape=[jax.ShapeDtypeStruct((N, D), F32),
                   jax.ShapeDtypeStruct((N // MOE_TILE * LOCAL_ROWS, D + 2 * LANES), BF16),
                   jax.ShapeDtypeStruct((N, LANES), F32),
                   jax.ShapeDtypeStruct((N // MOE_TILE, 1, LANES), F32)],
        compiler_params=_params("arbitrary"),
        name="out_proj",
    )(ya, yg, x2, mod, wo, g2, wr, br, stril)


def _chunk_copy(src_ref, src_chunk, dst_ref, dst_chunk, sem):
    src = src_ref.at[pl.ds(pl.multiple_of(src_chunk * CHUNK, CHUNK), CHUNK)]
    dst = dst_ref.at[pl.ds(pl.multiple_of(dst_chunk * CHUNK, CHUNK), CHUNK)]
    return pltpu.make_async_copy(src, dst, sem)


def _moe_kernel(src_ref, nvalid_ref, grp_ref, nt_ref, used_ref, rows_ref, w1_ref, w3_ref, w2_ref, y_ref,
                in_buf, out_buf, zero_buf, in_sem, out_sem, zero_sem):
    del grp_ref
    j = pl.program_id(0)
    n_tiles = nt_ref[0]
    d_model = w2_ref.shape[2]

    def gather(tile, slot):
        def body(k, carry):
            _chunk_copy(rows_ref, src_ref[tile * TILE_CHUNKS + k], in_buf.at[slot], k, in_sem.at[slot]).start()
            return carry
        lax.fori_loop(0, TILE_CHUNKS, body, 0, unroll=True)

    def wait_gather(slot):
        def body(k, carry):
            _chunk_copy(rows_ref, 0, in_buf.at[slot], k, in_sem.at[slot]).wait()
            return carry
        lax.fori_loop(0, TILE_CHUNKS, body, 0, unroll=True)

    def scatter(tile, slot):
        def body(k, carry):
            _chunk_copy(out_buf.at[slot], k, y_ref, src_ref[tile * TILE_CHUNKS + k], out_sem.at[slot]).start()
            return carry
        lax.fori_loop(0, nvalid_ref[tile], body, 0)

    def wait_scatter(tile, slot):
        def body(k, carry):
            _chunk_copy(out_buf.at[slot], k, y_ref, 0, out_sem.at[slot]).wait()
            return carry
        lax.fori_loop(0, nvalid_ref[tile], body, 0)

    def zero_fill(wait):
        def per_tile(i, carry):
            def body(c, inner):
                copy = _chunk_copy(zero_buf, 0, y_ref, i * LOCAL_CHUNKS + c, zero_sem)
                if wait:
                    copy.wait()
                else:
                    copy.start()
                return inner
            return lax.fori_loop(used_ref[i], LOCAL_CHUNKS, body, carry)
        lax.fori_loop(0, used_ref.shape[0], per_tile, 0)

    @pl.when(j == 0)
    def _():
        zero_buf[...] = jnp.zeros_like(zero_buf)
        zero_fill(wait=False)
        gather(0, 0)

    @pl.when(j < n_tiles)
    def _():
        slot = j % 2
        last = n_tiles - 1
        wait_gather(slot)
        rows = in_buf[slot]
        h = rows[:, 0:d_model]
        weights = rows[:, d_model:d_model + LANES].astype(F32) + rows[:, d_model + LANES:].astype(F32)
        experts = range(EXPERTS_PER_GROUP)
        up = [(_dot(h, w1_ref[k]), _dot(h, w3_ref[k])) for k in experts]
        gather(jnp.minimum(j + 1, last), 1 - slot)
        hid = [(a * _sigmoid(a) * g * weights[:, k:k + 1]).astype(BF16) for k, (a, g) in zip(experts, up)]
        y = _dot(hid[0], w2_ref[0])
        for k in experts[1:]:
            y = y + _dot(hid[k], w2_ref[k])

        @pl.when(j >= 2)
        def _():
            wait_scatter(j - 2, slot)

        out_buf[slot] = y.astype(BF16)
        scatter(j, slot)

        @pl.when(j == last)
        def _():
            wait_gather(1 - slot)

            @pl.when(j >= 1)
            def _():
                wait_scatter(j - 1, 1 - slot)
            wait_scatter(j, slot)
            zero_fill(wait=True)


def _combine_kernel(x1_ref, mod_ref, lpos_ref, y_ref, o_ref):
    tm = MOE_TILE
    local_row = lax.broadcasted_iota(jnp.int32, (tm, LOCAL_ROWS), 1).astype(F32)
    for t in range(x1_ref.shape[0] // tm):
        rows = slice(t * tm, (t + 1) * tm)
        unsort = jnp.where(local_row == lpos_ref[rows, 0:1], 1.0, 0.0).astype(BF16)
        y = _dot(unsort, y_ref[t * LOCAL_ROWS:(t + 1) * LOCAL_ROWS, :])
        o_ref[rows, :] = x1_ref[rows, :] + mod_ref[5:6, :] * y


def _moe_plan(cnt):
    n_local = cnt.shape[0]
    chunks = (cnt + CHUNK - 1) // CHUNK
    used = jnp.sum(chunks, axis=1)
    local_off = jnp.cumsum(chunks, axis=1) - chunks
    tiles_g = (jnp.sum(chunks, axis=0) + TILE_CHUNKS - 1) // TILE_CHUNKS
    tile_end = jnp.cumsum(tiles_g)
    n_tiles = tile_end[-1]
    group_start = (tile_end - tiles_g) * TILE_CHUNKS
    seg_len = chunks.T.reshape(-1)
    seg_start = (group_start[:, None] + (jnp.cumsum(chunks, axis=0) - chunks).T).reshape(-1)
    seg_src = (jnp.arange(n_local)[None, :] * LOCAL_CHUNKS + local_off.T).reshape(-1)
    max_chunks = n_local * MOE_TILE // CHUNK + n_local * N_GROUPS + N_GROUPS * TILE_CHUNKS
    max_tiles = (max_chunks + TILE_CHUNKS - 1) // TILE_CHUNKS
    c = jnp.arange(max_tiles * TILE_CHUNKS)[:, None]
    within = c - seg_start[None, :]
    hit = jnp.logical_and(within >= 0, within < seg_len[None, :])
    valid = jnp.any(hit, axis=1)
    src = jnp.sum(jnp.where(hit, seg_src[None, :] + within, 0), axis=1)
    src = jnp.where(valid, src, LOCAL_CHUNKS - 1)
    nvalid = jnp.sum(valid.reshape(max_tiles, TILE_CHUNKS), axis=1)
    j = jnp.minimum(jnp.arange(max_tiles), n_tiles - 1)
    grp = jnp.sum(j[:, None] >= tile_end[None, :], axis=1)
    i32 = lambda a: a.astype(jnp.int32)
    return i32(src), i32(nvalid), i32(grp), i32(n_tiles).reshape(1), i32(used)


def _moe(plan, rows_local, w1g, w3g, w2g):
    src, nvalid, grp, n_tiles, used = plan
    D = w2g.shape[3]
    weights = lambda a: pl.BlockSpec((None,) + a.shape[1:], lambda j, src, nv, grp, nt, used: (grp[j], 0, 0, 0))
    return pl.pallas_call(
        _moe_kernel,
        grid_spec=pltpu.PrefetchScalarGridSpec(
            num_scalar_prefetch=5,
            grid=(grp.shape[0],),
            in_specs=[pl.BlockSpec(memory_space=pl.ANY), weights(w1g), weights(w3g), weights(w2g)],
            out_specs=pl.BlockSpec(memory_space=pl.ANY),
            scratch_shapes=[pltpu.VMEM((2, MOE_TILE, rows_local.shape[1]), BF16),
                            pltpu.VMEM((2, MOE_TILE, D), BF16),
                            pltpu.VMEM((CHUNK, D), BF16),
                            pltpu.SemaphoreType.DMA((2,)), pltpu.SemaphoreType.DMA((2,)),
                            pltpu.SemaphoreType.DMA(())]),
        out_shape=jax.ShapeDtypeStruct((rows_local.shape[0], D), BF16),
        compiler_params=_params("arbitrary"),
        name="moe",
    )(src, nvalid, grp, n_tiles, used, rows_local, w1g, w3g, w2g)


def _combine(x1, mod, lpos, y_local, *, T):
    N, D = x1.shape
    subs = COMBINE_SUBS
    tm = subs * MOE_TILE
    per_b = T // tm
    return pl.pallas_call(
        _combine_kernel,
        grid=(N // tm,),
        in_specs=[pl.BlockSpec((tm, D), lambda i: (i, 0)),
                  pl.BlockSpec((None, 6, D), lambda i: (i // per_b, 0, 0)),
                  pl.BlockSpec((tm, LANES), lambda i: (i, 0)),
                  pl.BlockSpec((subs * LOCAL_ROWS, D), lambda i: (i, 0))],
        out_specs=pl.BlockSpec((tm, D), lambda i: (i, 0)),
        out_shape=jax.ShapeDtypeStruct((N, D), F32),
        compiler_params=_params("arbitrary"),
        name="moe_combine",
    )(x1, mod, lpos, y_local)


def _block_diag(n, blk, value, dtype):
    r = jnp.arange(n)[:, None] // blk
    c = jnp.arange(n)[None, :] // blk
    return jnp.where(r == c, value, 0.0).astype(dtype)


def _gla_constants():
    L = GLA_CHUNK
    i = jnp.arange(L)[:, None]
    j = jnp.arange(L)[None, :]
    tri = (j <= i).astype(BF16)
    bdtri = jnp.logical_and(j <= i, i // GLA_SUB == j // GLA_SUB).astype(BF16)
    col = jnp.arange(N_SUB * LANES)[None, :]
    qmask = jnp.stack([jnp.logical_and(col // LANES == i // GLA_SUB, (col % LANES) // GLA_DK == hh)
                       for hh in range(2)]).astype(BF16)
    bdms = _block_diag(2 * GLA_DV, GLA_DV, 1.0 / GLA_DV, BF16)
    d = jnp.arange(LANES)[:, None] // GLA_DK
    e = jnp.arange(2 * GLA_DV)[None, :] // GLA_DV
    smask = (d == e).astype(F32)
    return tri, bdtri, qmask, bdms, smask


def kernel(x, c, w_ada, b_ada, g_norm1, w_in, q_norm, k_norm, sinks, w_gk2, b_gk, g_gla_out, g_att_out,
           w_out, g_norm2, w_group, b_group, w_router, b_router, w1, w3, w2):
    B, T, D = x.shape
    N = B * T
    depth = w_ada.shape[0]
    tri, bdtri, qmask, bdms, smask = _gla_constants()
    bdq = _block_diag(2 * LANES, HEAD_DIM, 1.0 / HEAD_DIM, BF16)
    bdk = _block_diag(ATT_KV, HEAD_DIM, 1.0 / HEAD_DIM, BF16)

    x2 = x.reshape(N, D)
    for l in range(depth):
        mod = _adaln_mod(c, w_ada[l], b_ada[l]).reshape(B, 6, D)

        wl = w_in[l].astype(BF16)
        lr0 = ATT_Q + 2 * ATT_KV + 2 * GLA_K + GLA_V
        w_in_p = jnp.concatenate(
            [wl[:, :lr0], wl[:, lr0 + GLA_RANK:], wl[:, lr0:lr0 + GLA_RANK],
             jnp.zeros((D, LANES - GLA_RANK), wl.dtype)], axis=1)
        wgk = jnp.concatenate([w_gk2[l], jnp.zeros((LANES - GLA_RANK, GLA_K), F32)], axis=0).astype(BF16)
        qa, ka, va, qg, kg, vg, la, og = _in_proj(
            x2, mod, g_norm1[l].reshape(1, D), w_in_p,
            jnp.tile(q_norm[l], ATT_HEADS).reshape(1, ATT_Q), jnp.tile(k_norm[l], ATT_KV_HEADS).reshape(1, ATT_KV),
            bdq, bdk, wgk, b_gk[l].reshape(1, GLA_K), T=T, tm=PROJ_TILE)

        y_att = _swa_attention(sinks[l], qa, ka, va, g_att_out[l].reshape(1, ATT_Q), B=B, T=T)
        y_gla = _gla(qg, kg, vg, la, og, tri, bdtri, qmask, bdms, smask,
                     jnp.tile(g_gla_out[l], GLA_HEADS).reshape(1, GLA_V), B=B, T=T)

        pad = LANES - N_GROUPS - N_EXPERTS
        wr = jnp.concatenate([w_group[l], w_router[l], jnp.zeros((D, pad), F32)], axis=1).astype(BF16)
        br = jnp.concatenate([b_group[l], b_router[l], jnp.zeros((pad,), F32)]).reshape(1, LANES)
        stril = (jnp.arange(MOE_TILE)[None, :] < jnp.arange(MOE_TILE)[:, None]).astype(BF16)
        x1, rows_local, lpos, cnt = _out_proj(y_att, y_gla, x2, mod, w_out[l].astype(BF16),
                                              g_norm2[l].reshape(1, D), wr, br, stril, T=T)
        plan = _moe_plan(cnt[:, 0, :N_GROUPS].astype(jnp.int32))
        by_group = lambda w: w.astype(BF16).reshape((N_GROUPS, EXPERTS_PER_GROUP) + w.shape[1:])
        y_local = _moe(plan, rows_local, by_group(w1[l]), by_group(w3[l]), by_group(w2[l]))
        x2 = _combine(x1, mod, lpos, y_local, T=T)
    return x2.reshape(B, T, D)
```

```python
import functools

import jax
import jax.numpy as jnp
from jax import lax
from jax.experimental import pallas as pl
from jax.experimental.pallas import tpu as pltpu

F32 = jnp.float32
BF16 = jnp.bfloat16

EPS = 1e-6
ATT_HEADS = 8
ATT_KV_HEADS = 2
HEAD_DIM = 64
WINDOW = 128
ATT_Q = ATT_HEADS * HEAD_DIM
ATT_KV = ATT_KV_HEADS * HEAD_DIM
GLA_HEADS = 4
GLA_DK = 64
GLA_DV = 128
GLA_RANK = 16
GLA_NORMALIZER = 16.0
GLA_K = GLA_HEADS * GLA_DK
GLA_V = GLA_HEADS * GLA_DV
N_GROUPS = 4
EXPERTS_PER_GROUP = 4
N_EXPERTS = N_GROUPS * EXPERTS_PER_GROUP
D_EXPERT = 256

LANES = 128
PROJ_SUB = 512
PROJ_TILE = 2 * PROJ_SUB
OUT_SUBS = 4
COMBINE_SUBS = 4
ROUTER_ROWS = 24
ATT_BLOCKS_PER_STEP = 4
GLA_BATCH_PER_STEP = 4
GLA_CHUNK = 128
GLA_SUB = 16
N_SUB = GLA_CHUNK // GLA_SUB
ROUTER_LANE0 = N_GROUPS
VMEM_LIMIT = 56 * 1024 * 1024
MOE_TILE = 256
CHUNK = 16
TILE_CHUNKS = MOE_TILE // CHUNK
LOCAL_CHUNKS = (MOE_TILE + N_GROUPS * (CHUNK - 1)) // CHUNK + 2
LOCAL_ROWS = LOCAL_CHUNKS * CHUNK

_QA0, _KA0, _VA0 = 0, ATT_Q, ATT_Q + ATT_KV
_QG0 = _VA0 + ATT_KV
_KG0 = _QG0 + GLA_K
_VG0 = _KG0 + GLA_K
_OG0 = _VG0 + GLA_V
_LR0 = _OG0 + GLA_V
IN_COLS_PAD = _LR0 + LANES


def _dot(a, b):
    return jnp.dot(a, b, preferred_element_type=F32)


def _dot_nt(a, b):
    return lax.dot_general(a, b, (((1,), (1,)), ((), ())), preferred_element_type=F32)


def _sigmoid(x):
    return 1.0 / (1.0 + jnp.exp(-x))


def _params(*sem):
    return pltpu.CompilerParams(dimension_semantics=sem, vmem_limit_bytes=VMEM_LIMIT)


def _mod_kernel(c_ref, w_ref, b_ref, o_ref):
    c = c_ref[...]
    s = (c * _sigmoid(c)).astype(BF16)
    o_ref[...] = _dot(s, w_ref[...].astype(BF16)) + b_ref[...]


def _adaln_mod(c, w_ada, b_ada):
    B, D = c.shape
    n = w_ada.shape[1]
    tn = 1536
    return pl.pallas_call(
        _mod_kernel,
        grid=(n // tn,),
        in_specs=[pl.BlockSpec((B, D), lambda j: (0, 0)),
                  pl.BlockSpec((D, tn), lambda j: (0, j)),
                  pl.BlockSpec((1, tn), lambda j: (0, j))],
        out_specs=pl.BlockSpec((B, tn), lambda j: (0, j)),
        out_shape=jax.ShapeDtypeStruct((B, n), F32),
        compiler_params=_params("arbitrary"),
        name="adaln_mod",
    )(c, w_ada, b_ada.reshape(1, n))


def _inproj_kernel(x_ref, mod_ref, g1_ref, w_ref, qn_ref, kn_ref, bdq_ref, bdk_ref, wgk_ref, bgk_ref,
                   qa_ref, ka_ref, va_ref, qg_ref, kg_ref, vg_ref, la_ref, og_ref):
    subs = range(x_ref.shape[0] // PROJ_SUB)
    rows = lambda t: slice(t * PROJ_SUB, (t + 1) * PROJ_SUB)

    h = []
    for t in subs:
        x = x_ref[rows(t), :]
        ms = jnp.mean(x * x, axis=-1, keepdims=True)
        xn = x * lax.rsqrt(ms + EPS) * g1_ref[...]
        h.append((xn * (1.0 + mod_ref[1:2, :]) + mod_ref[0:1, :]).astype(BF16))

    proj = lambda t, c0, width: _dot(h[t], w_ref[:, c0:c0 + width])
    gate_of = lambda lr: _dot(lr.astype(BF16), wgk_ref[...]) + bgk_ref[...]
    qa, kv, q_ms, k_ms, qk_g, vg, og, lr, gate = ({} for _ in range(9))
    for t in subs:
        qa[t] = proj(t, _QA0, ATT_Q)
        if t > 0:
            gate[t - 1] = gate_of(lr[t - 1])
        kv[t] = proj(t, _KA0, 2 * ATT_KV)
        q_sq = (qa[t] * qa[t]).astype(BF16)
        q_ms[t] = jnp.concatenate([_dot(q_sq[:, c:c + 2 * LANES], bdq_ref[...])
                                   for c in range(0, ATT_Q, 2 * LANES)], axis=1)
        qk_g[t] = proj(t, _QG0, 2 * GLA_K)
        k = kv[t][:, 0:ATT_KV]
        k_ms[t] = _dot((k * k).astype(BF16), bdk_ref[...])
        vg[t] = proj(t, _VG0, GLA_V)
        og[t] = proj(t, _OG0, GLA_V)
        lr[t] = proj(t, _LR0, LANES)
    gate[subs[-1]] = gate_of(lr[subs[-1]])

    low = lax.broadcasted_iota(jnp.int32, (PROJ_SUB, ATT_KV), 1) < HEAD_DIM
    for t in subs:
        qa_ref[rows(t), :] = (qa[t] * lax.rsqrt(q_ms[t] + EPS) * qn_ref[...] * (HEAD_DIM ** -0.5)).astype(BF16)
        k = kv[t][:, 0:ATT_KV] * lax.rsqrt(k_ms[t] + EPS) * kn_ref[...]
        v = kv[t][:, ATT_KV:2 * ATT_KV]
        for src, dst in ((k, ka_ref), (v, va_ref)):
            swapped = pltpu.roll(src, HEAD_DIM, axis=1)
            dst[rows(t), 0:LANES] = jnp.where(low, src, swapped).astype(BF16)
            dst[rows(t), LANES:2 * LANES] = jnp.where(low, swapped, src).astype(BF16)
        qg_ref[rows(t), :] = (qk_g[t][:, 0:GLA_K] * (GLA_DK ** -0.5)).astype(BF16)
        kg_ref[rows(t), :] = qk_g[t][:, GLA_K:2 * GLA_K].astype(BF16)
        vg_ref[rows(t), :] = vg[t].astype(BF16)
        og_ref[rows(t), :] = (og[t] * _sigmoid(og[t])).astype(BF16)
        log_sig = jnp.minimum(gate[t], 0.0) - jnp.log(1.0 + jnp.exp(-jnp.abs(gate[t])))
        la_ref[rows(t), :] = log_sig * (1.0 / GLA_NORMALIZER)


def _in_proj(x2, mod, g1, w_in_p, qn, kn, bdq, bdk, wgk, bgk, *, T, tm):
    N, D = x2.shape
    per_b = T // tm
    const = lambda shape: pl.BlockSpec(shape, lambda i: (0,) * len(shape))
    rows = lambda c: pl.BlockSpec((tm, c), lambda i: (i, 0))
    outs = [(ATT_Q, BF16), (2 * ATT_KV, BF16), (2 * ATT_KV, BF16), (GLA_K, BF16), (GLA_K, BF16),
            (GLA_V, BF16), (GLA_K, F32), (GLA_V, BF16)]
    return pl.pallas_call(
        _inproj_kernel,
        grid=(N // tm,),
        in_specs=[rows(D),
                  pl.BlockSpec((None, 6, D), lambda i: (i // per_b, 0, 0)),
                  const((1, D)), const(w_in_p.shape), const(qn.shape), const(kn.shape),
                  const(bdq.shape), const(bdk.shape), const(wgk.shape), const(bgk.shape)],
        out_specs=[rows(c) for c, _ in outs],
        out_shape=[jax.ShapeDtypeStruct((N, c), dt) for c, dt in outs],
        compiler_params=_params("arbitrary"),
        name="in_proj",
    )(x2, mod, g1, w_in_p, qn, kn, bdq, bdk, wgk, bgk)


def _attn_kernel(sinks_ref, q_ref, kc_ref, vc_ref, kp_ref, vp_ref, gatt_ref, o_ref):
    blk = WINDOW
    first = pl.program_id(1) == 0
    qi = lax.broadcasted_iota(jnp.int32, (blk, blk), 0)
    cj = lax.broadcasted_iota(jnp.int32, (blk, blk), 1)
    from_prev = cj > qi
    dist = (qi - cj + jnp.where(from_prev, blk, 0)).astype(F32)
    no_prev = jnp.where(jnp.logical_and(from_prev, first), -1e30, 0.0)
    low = cj < HEAD_DIM
    half = (jnp.where(low, 1.0, 0.0).astype(BF16), jnp.where(low, 0.0, 1.0).astype(BF16))
    half2 = tuple(jnp.concatenate([m, m], axis=0) for m in half)
    prev_mask = jnp.where(from_prev, 1.0, 0.0).astype(BF16)
    cur_mask = jnp.where(from_prev, 0.0, 1.0).astype(BF16)

    n_pairs = ATT_HEADS // 2
    pairs_per_kv = n_pairs // ATT_KV_HEADS
    units = [(bi, j) for bi in range(ATT_BLOCKS_PER_STEP) for j in range(n_pairs)]

    def kv_blocks(bi, g):
        rows = slice(bi * blk, (bi + 1) * blk)
        prev_rows = slice((bi - 1) * blk, bi * blk)
        lanes = slice(g * LANES, (g + 1) * LANES)
        cur = (kc_ref[rows, lanes], vc_ref[rows, lanes])
        prev = (kp_ref[:, lanes], vp_ref[:, lanes]) if bi == 0 else (kc_ref[prev_rows, lanes],
                                                                      vc_ref[prev_rows, lanes])
        return prev, cur

    scores = {}
    for bi, j in units:
        (kp, _), (kc, _) = kv_blocks(bi, j // pairs_per_kv)
        k_both = jnp.concatenate([kp, kc], axis=0)
        qp = q_ref[bi * blk:(bi + 1) * blk, j * LANES:(j + 1) * LANES]
        for p in range(2):
            scores[bi, j, p] = _dot_nt(qp, k_both * half2[p])

    probs, sink_terms = {}, {}
    for bi, j in units:
        for p in range(2):
            h = 2 * j + p
            slope = 2.0 ** (-8.0 * (h + 1) / ATT_HEADS)
            s_both = scores[bi, j, p]
            s = jnp.where(from_prev, s_both[:, 0:blk], s_both[:, blk:2 * blk]) - slope * dist
            if bi == 0:
                s = s + no_prev
            sink = sinks_ref[h]
            m = jnp.maximum(jnp.max(s, axis=-1, keepdims=True), sink)
            probs[bi, j, p] = jnp.exp(s - m)
            sink_terms[bi, j, p] = jnp.exp(sink - m)

    outs = {}
    for bi, j in units:
        (_, vp), (_, vc) = kv_blocks(bi, j // pairs_per_kv)
        v_stack = jnp.concatenate([jnp.concatenate([v * half[p], half[p]], axis=1)
                                   for v in (vp, vc) for p in range(2)], axis=0)
        e = [probs[bi, j, p].astype(BF16) for p in range(2)]
        p_all = jnp.concatenate([x * m for m in (prev_mask, cur_mask) for x in e], axis=1)
        pv = _dot(p_all, v_stack)
        den = pv[:, LANES:2 * LANES] + jnp.where(low, sink_terms[bi, j, 0], sink_terms[bi, j, 1])
        outs[bi, j] = pv[:, 0:LANES] / den

    for bi in range(ATT_BLOCKS_PER_STEP):
        o = jnp.concatenate([outs[bi, j] for j in range(n_pairs)], axis=1)
        ms = jnp.mean(o * o, axis=-1, keepdims=True)
        o_ref[bi * blk:(bi + 1) * blk, :] = (o * lax.rsqrt(ms + EPS) * gatt_ref[...]).astype(BF16)


def _swa_attention(sinks, qa, ka, va, gatt, *, B, T):
    step_rows = ATT_BLOCKS_PER_STEP * WINDOW
    nb = T // step_rows
    cur = lambda c: pl.BlockSpec((step_rows, c), lambda b, i: (b * nb + i, 0))
    prev = lambda c: pl.BlockSpec(
        (WINDOW, c), lambda b, i: (b * (T // WINDOW) + jnp.maximum(i * ATT_BLOCKS_PER_STEP - 1, 0), 0))
    return pl.pallas_call(
        _attn_kernel,
        grid=(B, nb),
        in_specs=[pl.BlockSpec(memory_space=pltpu.SMEM),
                  cur(ATT_Q), cur(2 * ATT_KV), cur(2 * ATT_KV), prev(2 * ATT_KV), prev(2 * ATT_KV),
                  pl.BlockSpec((1, ATT_Q), lambda b, i: (0, 0))],
        out_specs=cur(ATT_Q),
        out_shape=jax.ShapeDtypeStruct((B * T, ATT_Q), BF16),
        compiler_params=_params("arbitrary", "arbitrary"),
        name="swa_attn",
    )(sinks, qa, ka, va, ka, va, gatt)


def _gla_kernel(q_ref, k_ref, v_ref, la_ref, og_ref, tri_ref, bdtri_ref, qmask_ref, bdms_ref, smask_ref,
                ggla_ref, o_ref, state_ref):
    L = GLA_CHUNK

    @pl.when(pl.program_id(1) == 0)
    def _():
        state_ref[...] = jnp.zeros_like(state_ref)

    tri, bdtri = tri_ref[...], bdtri_ref[...]
    rows = lax.broadcasted_iota(jnp.int32, (L, LANES), 0)
    causal = lax.broadcasted_iota(jnp.int32, (L, L), 0) >= lax.broadcasted_iota(jnp.int32, (L, L), 1)
    seqs = range(GLA_BATCH_PER_STEP)
    units = [(s, pair) for s in seqs for pair in range(GLA_HEADS // 2)]
    kl = lambda pair: slice(pair * LANES, (pair + 1) * LANES)
    vl = lambda pair: slice(pair * 2 * GLA_DV, (pair + 1) * 2 * GLA_DV)

    b, b_in = {}, {}
    for s in seqs:
        la = la_ref[s]
        la_hi = la.astype(BF16)
        la_lo = (la - la_hi.astype(F32)).astype(BF16)
        b[s] = _dot(tri, la_hi) + _dot(tri, la_lo)
        b_in[s] = _dot(bdtri, la_hi) + _dot(bdtri, la_lo)

    q_both, keys, q_dec, k_dec_t, b_last = {}, {}, {}, {}, {}
    for s in seqs:
        ref = b[s] - b_in[s]
        b_last[s] = b[s][L - 1:L, :]
        q = q_ref[s].astype(F32)
        k = k_ref[s].astype(F32)
        q_in = (q * jnp.exp(b_in[s])).astype(BF16)
        q_dec[s] = (q * jnp.exp(b[s])).astype(BF16)
        k_dec = k * jnp.exp(b_last[s] - b[s])
        for pair in range(GLA_HEADS // 2):
            k_p, b_p, ref_p = k[:, kl(pair)], b[s][:, kl(pair)], ref[:, kl(pair)]
            expanded = []
            for g in range(N_SUB):
                last_row = (g + 1) * GLA_SUB - 1
                expo = jnp.where(rows <= last_row, ref_p[g * GLA_SUB:g * GLA_SUB + 1, :] - b_p, -1e4)
                expanded.append((k_p * jnp.exp(expo)).astype(BF16))
            keys[s, pair] = jnp.concatenate(expanded, axis=1)
            q_rep = jnp.concatenate([q_in[:, kl(pair)]] * N_SUB, axis=1)
            q_both[s, pair] = jnp.concatenate([q_rep * qmask_ref[0], q_rep * qmask_ref[1]], axis=0)
            k_dec_t[s, pair] = k_dec[:, kl(pair)].T.astype(BF16)

    scores = {u: _dot_nt(q_both[u], keys[u]) for u in units}

    outs, updates = {}, {}
    for s, pair in units:
        v_p = v_ref[s, :, vl(pair)]
        o_parts = []
        for hh in range(2):
            a = jnp.where(causal, scores[s, pair][hh * L:(hh + 1) * L, :], 0.0).astype(BF16)
            o_parts.append(_dot(a, v_p[:, hh * GLA_DV:(hh + 1) * GLA_DV]))
        state = state_ref[s, pair]
        outs[s, pair] = jnp.concatenate(o_parts, axis=1) + _dot(q_dec[s][:, kl(pair)], state.astype(BF16))
        updates[s, pair] = _dot(k_dec_t[s, pair], v_p)

    for s, pair in units:
        decay = jnp.broadcast_to(jnp.exp(b_last[s][:, kl(pair)]), (LANES, LANES)).T
        state_ref[s, pair] = (state_ref[s, pair] * jnp.concatenate([decay, decay], axis=1)
                              + updates[s, pair] * smask_ref[...])
        o = outs[s, pair]
        ms = _dot((o * o).astype(BF16), bdms_ref[...])
        y = o * lax.rsqrt(ms + EPS) * ggla_ref[:, vl(pair)] * og_ref[s, :, vl(pair)].astype(F32)
        o_ref[s, :, vl(pair)] = y.astype(BF16)


def _gla(qg, kg, vg, la, og, tri, bdtri, qmask, bdms, smask, ggla, *, B, T):
    L = GLA_CHUNK
    nb = GLA_BATCH_PER_STEP
    seq = lambda a: a.reshape(B, T, a.shape[-1])
    rows = lambda c: pl.BlockSpec((nb, L, c), lambda b, i: (b, i, 0))
    const = lambda a: pl.BlockSpec(a.shape, lambda b, i: (0,) * a.ndim)
    out = pl.pallas_call(
        _gla_kernel,
        grid=(B // nb, T // L),
        in_specs=[rows(GLA_K), rows(GLA_K), rows(GLA_V), rows(GLA_K), rows(GLA_V),
                  const(tri), const(bdtri), const(qmask), const(bdms), const(smask), const(ggla)],
        out_specs=rows(GLA_V),
        out_shape=jax.ShapeDtypeStruct((B, T, GLA_V), BF16),
        scratch_shapes=[pltpu.VMEM((nb, GLA_HEADS // 2, LANES, 2 * GLA_DV), F32)],
        compiler_params=_params("arbitrary", "arbitrary"),
        name="gla",
    )(seq(qg), seq(kg), seq(vg), seq(la), seq(og), tri, bdtri, qmask, bdms, smask, ggla)
    return out.reshape(B * T, GLA_V)


def _route(logits_t):
    lt = logits_t[0:ROUTER_ROWS, :]
    row = lax.broadcasted_iota(jnp.int32, lt.shape, 0)
    neg_inf = -jnp.inf
    g_log = jnp.where(row < N_GROUPS, lt, neg_inf)
    g_max = jnp.max(g_log, axis=0, keepdims=True)
    g_sel = jnp.min(jnp.where(g_log == g_max, row, LANES), axis=0, keepdims=True)
    g_sum = jnp.sum(jnp.where(row < N_GROUPS, jnp.exp(lt - g_max), 0.0), axis=0, keepdims=True)
    p_group = 1.0 / g_sum
    e_lo = ROUTER_LANE0 + EXPERTS_PER_GROUP * g_sel
    in_group = jnp.logical_and(row >= e_lo, row < e_lo + EXPERTS_PER_GROUP)
    e_log = jnp.where(in_group, lt, neg_inf)
    e_max = jnp.max(e_log, axis=0, keepdims=True)
    top1 = jnp.min(jnp.where(e_log == e_max, row, LANES), axis=0, keepdims=True)
    e_log2 = jnp.where(row == top1, neg_inf, e_log)
    e_max2 = jnp.max(e_log2, axis=0, keepdims=True)
    top2 = jnp.min(jnp.where(e_log2 == e_max2, row, LANES), axis=0, keepdims=True)
    ratio = jnp.exp(e_max2 - e_max)
    w_top1 = p_group / (1.0 + ratio)
    w_top2 = p_group * ratio / (1.0 + ratio)
    row8 = lax.broadcasted_iota(jnp.int32, (8, lt.shape[1]), 0)
    weights = jnp.where(row8 == top1 - e_lo, w_top1, 0.0) + jnp.where(row8 == top2 - e_lo, w_top2, 0.0)
    return g_sel, weights


def _outproj_kernel(ya_ref, yg_ref, x_ref, mod_ref, wo_ref, g2_ref, wrt_ref, brt_ref, striu_ref,
                    x1_ref, row_ref, lpos_ref, cnt_ref):
    tm = MOE_TILE
    subs = range(x_ref.shape[0] // tm)
    rows = lambda t: slice(t * tm, (t + 1) * tm)

    mix = [_dot(ya_ref[rows(t), :], wo_ref[0:ATT_Q, :]) + _dot(yg_ref[rows(t), :], wo_ref[ATT_Q:ATT_Q + GLA_V, :])
           for t in subs]
    h2b = []
    for t in subs:
        x1 = x_ref[rows(t), :] + mod_ref[2:3, :] * mix[t]
        x1_ref[rows(t), :] = x1
        ms = jnp.mean(x1 * x1, axis=-1, keepdims=True)
        h2 = (x1 * lax.rsqrt(ms + EPS) * g2_ref[...]) * (1.0 + mod_ref[4:5, :]) + mod_ref[3:4, :]
        h2b.append(h2.astype(BF16))
    logits_t = [_dot_nt(wrt_ref[...], h2b[t]) + brt_ref[...] for t in subs]

    routed = [_route(logits_t[t]) for t in subs]
    row8 = lax.broadcasted_iota(jnp.int32, (8, tm), 0)
    onehot = [jnp.where(row8 == routed[t][0], 1.0, 0.0) for t in subs]
    before = [_dot(onehot[t].astype(BF16), striu_ref[...]) for t in subs]

    local_row = lax.broadcasted_iota(jnp.int32, (LOCAL_ROWS, tm), 0).astype(F32)
    pad_rows = jnp.zeros((LANES - 8, tm), F32)
    for t in subs:
        count = jnp.sum(onehot[t], axis=1, keepdims=True)
        cnt_ref[t] = jnp.broadcast_to(count, (8, LANES))
        padded = jnp.broadcast_to(jnp.floor((count + (CHUNK - 1.0)) * (1.0 / CHUNK)) * CHUNK, (8, tm))
        start = jnp.zeros((8, tm), F32)
        for shift in range(1, N_GROUPS):
            start = start + jnp.where(row8 >= shift, pltpu.roll(padded, shift, axis=0), 0.0)
        lpos = jnp.sum(onehot[t] * (before[t] + start), axis=0, keepdims=True)
        lpos_ref[rows(t), :] = jnp.broadcast_to(lpos, (LANES, tm)).T
        weights = jnp.concatenate([routed[t][1], pad_rows], axis=0).T
        w_hi = weights.astype(BF16)
        w_lo = (weights - w_hi.astype(F32)).astype(BF16)
        perm = jnp.where(local_row == lpos, 1.0, 0.0).astype(BF16)
        row_ref[t * LOCAL_ROWS:(t + 1) * LOCAL_ROWS, :] = _dot(
            perm, jnp.concatenate([h2b[t], w_hi, w_lo], axis=1)).astype(BF16)


def _out_proj(ya, yg, x2, mod, wo, g2, wr, br, stril, *, T):
    N, D = x2.shape
    subs = OUT_SUBS
    tm = subs * MOE_TILE
    per_b = T // tm
    rows = lambda c: pl.BlockSpec((tm, c), lambda i: (i, 0))
    const = lambda a: pl.BlockSpec(a.shape, lambda i: (0,) * a.ndim)
    return pl.pallas_call(
        _outproj_kernel,
        grid=(N // tm,),
        in_specs=[rows(ATT_Q), rows(GLA_V), rows(D),
                  pl.BlockSpec((None, 6, D), lambda i: (i // per_b, 0, 0)),
                  const(wo), const(g2), const(wr), const(br), const(stril)],
        out_specs=[rows(D), pl.BlockSpec((subs * LOCAL_ROWS, D + 2 * LANES), lambda i: (i, 0)), rows(LANES),
                   pl.BlockSpec((subs, 8, LANES), lambda i: (i, 0, 0))],
        out_shape=[jax.ShapeDtypeStruct((N, D), F32),
                   jax.ShapeDtypeStruct((N // MOE_TILE * LOCAL_ROWS, D + 2 * LANES), BF16),
                   jax.ShapeDtypeStruct((N, LANES), F32),
                   jax.ShapeDtypeStruct((N // MOE_TILE, 8, LANES), F32)],
        compiler_params=_params("arbitrary"),
        name="out_proj",
    )(ya, yg, x2, mod, wo, g2, wr, br, stril)


def _chunk_copy(src_ref, src_chunk, dst_ref, dst_chunk, sem):
    return pltpu.make_async_copy(src_ref.at[src_chunk], dst_ref.at[dst_chunk], sem)


def _moe_kernel(src_ref, nvalid_ref, grp_ref, nt_ref, used_ref, rows_ref, w1_ref, w3_ref, w2_ref, y_ref,
                in_buf, out_buf, zero_buf, in_sem, out_sem, zero_sem):
    del grp_ref
    j = pl.program_id(0)
    n_tiles = nt_ref[0]
    d_model = w2_ref.shape[2]

    def gather(tile, slot):
        def body(k, carry):
            _chunk_copy(rows_ref, src_ref[tile * TILE_CHUNKS + k], in_buf.at[slot], k, in_sem.at[slot]).start()
            return carry
        lax.fori_loop(0, TILE_CHUNKS, body, 0, unroll=True)

    def wait_gather(slot):
        def body(k, carry):
            _chunk_copy(rows_ref, 0, in_buf.at[slot], k, in_sem.at[slot]).wait()
            return carry
        lax.fori_loop(0, TILE_CHUNKS, body, 0, unroll=True)

    def scatter(tile, slot):
        def body(k, carry):
            _chunk_copy(out_buf.at[slot], k, y_ref, src_ref[tile * TILE_CHUNKS + k], out_sem.at[slot]).start()
            return carry
        lax.fori_loop(0, nvalid_ref[tile], body, 0)

    def wait_scatter(tile, slot):
        def body(k, carry):
            _chunk_copy(out_buf.at[slot], k, y_ref, 0, out_sem.at[slot]).wait()
            return carry
        lax.fori_loop(0, nvalid_ref[tile], body, 0)

    def zero_fill(wait):
        def per_tile(i, carry):
            def body(c, inner):
                copy = _chunk_copy(zero_buf, 0, y_ref, i * LOCAL_CHUNKS + c, zero_sem)
                if wait:
                    copy.wait()
                else:
                    copy.start()
                return inner
            return lax.fori_loop(used_ref[i], LOCAL_CHUNKS, body, carry)
        lax.fori_loop(0, used_ref.shape[0], per_tile, 0)

    @pl.when(j == 0)
    def _():
        zero_buf[...] = jnp.zeros_like(zero_buf)
        zero_fill(wait=False)
        gather(0, 0)

    @pl.when(j + 1 < n_tiles)
    def _():
        gather(j + 1, (j + 1) % 2)

    @pl.when(j < n_tiles)
    def _():
        slot = j % 2
        wait_gather(slot)
        rows = in_buf[slot].reshape(MOE_TILE, in_buf.shape[-1])
        h = rows[:, 0:d_model]
        weights = rows[:, d_model:d_model + LANES].astype(F32) + rows[:, d_model + LANES:].astype(F32)
        experts = range(EXPERTS_PER_GROUP)
        up = [(_dot(h, w1_ref[k]), _dot(h, w3_ref[k])) for k in experts]
        hid = [(a * _sigmoid(a) * g * weights[:, k:k + 1]).astype(BF16) for k, (a, g) in zip(experts, up)]
        y = _dot(hid[0], w2_ref[0])
        for k in experts[1:]:
            y = y + _dot(hid[k], w2_ref[k])

        @pl.when(j >= 2)
        def _():
            wait_scatter(j - 2, slot)

        out_buf[slot] = y.astype(BF16).reshape(TILE_CHUNKS, CHUNK, d_model)
        scatter(j, slot)

        @pl.when(j == n_tiles - 1)
        def _():
            @pl.when(j >= 1)
            def _():
                wait_scatter(j - 1, 1 - slot)
            wait_scatter(j, slot)
            zero_fill(wait=True)


def _combine_kernel(x1_ref, mod_ref, lpos_ref, y_ref, o_ref):
    tm = MOE_TILE
    local_row = lax.broadcasted_iota(jnp.int32, (tm, LOCAL_ROWS), 1).astype(F32)
    for t in range(x1_ref.shape[0] // tm):
        rows = slice(t * tm, (t + 1) * tm)
        unsort = jnp.where(local_row == lpos_ref[rows, 0:1], 1.0, 0.0).astype(BF16)
        y = _dot(unsort, y_ref[t * LOCAL_ROWS:(t + 1) * LOCAL_ROWS, :])
        o_ref[rows, :] = x1_ref[rows, :] + mod_ref[5:6, :] * y


def _moe_plan(cnt):
    n_local = cnt.shape[0]
    chunks = (cnt + CHUNK - 1) // CHUNK
    used = jnp.sum(chunks, axis=1)
    local_off = jnp.cumsum(chunks, axis=1) - chunks
    tiles_g = (jnp.sum(chunks, axis=0) + TILE_CHUNKS - 1) // TILE_CHUNKS
    tile_end = jnp.cumsum(tiles_g)
    n_tiles = tile_end[-1]
    group_start = (tile_end - tiles_g) * TILE_CHUNKS
    seg_len = chunks.T.reshape(-1)
    seg_start = (group_start[:, None] + (jnp.cumsum(chunks, axis=0) - chunks).T).reshape(-1)
    seg_src = (jnp.arange(n_local)[None, :] * LOCAL_CHUNKS + local_off.T).reshape(-1)
    max_chunks = n_local * MOE_TILE // CHUNK + n_local * N_GROUPS + N_GROUPS * TILE_CHUNKS
    max_tiles = (max_chunks + TILE_CHUNKS - 1) // TILE_CHUNKS
    c = jnp.arange(max_tiles * TILE_CHUNKS)[:, None]
    within = c - seg_start[None, :]
    hit = jnp.logical_and(within >= 0, within < seg_len[None, :])
    valid = jnp.any(hit, axis=1)
    src = jnp.sum(jnp.where(hit, seg_src[None, :] + within, 0), axis=1)
    src = jnp.where(valid, src, LOCAL_CHUNKS - 1)
    nvalid = jnp.sum(valid.reshape(max_tiles, TILE_CHUNKS), axis=1)
    j = jnp.minimum(jnp.arange(max_tiles), n_tiles - 1)
    grp = jnp.sum(j[:, None] >= tile_end[None, :], axis=1)
    i32 = lambda a: a.astype(jnp.int32)
    return i32(src), i32(nvalid), i32(grp), i32(n_tiles).reshape(1), i32(used)


def _moe(plan, rows_local, w1g, w3g, w2g):
    src, nvalid, grp, n_tiles, used = plan
    D = w2g.shape[3]
    n_rows, cols = rows_local.shape
    weights = lambda a: pl.BlockSpec((None,) + a.shape[1:], lambda j, src, nv, grp, nt, used: (grp[j], 0, 0, 0))
    y = pl.pallas_call(
        _moe_kernel,
        grid_spec=pltpu.PrefetchScalarGridSpec(
            num_scalar_prefetch=5,
            grid=(grp.shape[0],),
            in_specs=[pl.BlockSpec(memory_space=pl.ANY), weights(w1g), weights(w3g), weights(w2g)],
            out_specs=pl.BlockSpec(memory_space=pl.ANY),
            scratch_shapes=[pltpu.VMEM((2, TILE_CHUNKS, CHUNK, cols), BF16),
                            pltpu.VMEM((2, TILE_CHUNKS, CHUNK, D), BF16),
                            pltpu.VMEM((1, CHUNK, D), BF16),
                            pltpu.SemaphoreType.DMA((2,)), pltpu.SemaphoreType.DMA((2,)),
                            pltpu.SemaphoreType.DMA(())]),
        out_shape=jax.ShapeDtypeStruct((n_rows // CHUNK, CHUNK, D), BF16),
        compiler_params=_params("arbitrary"),
        name="moe",
    )(src, nvalid, grp, n_tiles, used, rows_local.reshape(n_rows // CHUNK, CHUNK, cols), w1g, w3g, w2g)
    return y.reshape(n_rows, D)


def _combine(x1, mod, lpos, y_local, *, T):
    N, D = x1.shape
    subs = COMBINE_SUBS
    tm = subs * MOE_TILE
    per_b = T // tm
    return pl.pallas_call(
        _combine_kernel,
        grid=(N // tm,),
        in_specs=[pl.BlockSpec((tm, D), lambda i: (i, 0)),
                  pl.BlockSpec((None, 6, D), lambda i: (i // per_b, 0, 0)),
                  pl.BlockSpec((tm, LANES), lambda i: (i, 0)),
                  pl.BlockSpec((subs * LOCAL_ROWS, D), lambda i: (i, 0))],
        out_specs=pl.BlockSpec((tm, D), lambda i: (i, 0)),
        out_shape=jax.ShapeDtypeStruct((N, D), F32),
        compiler_params=_params("arbitrary"),
        name="moe_combine",
    )(x1, mod, lpos, y_local)


def _block_diag(n, blk, value, dtype):
    r = jnp.arange(n)[:, None] // blk
    c = jnp.arange(n)[None, :] // blk
    return jnp.where(r == c, value, 0.0).astype(dtype)


def _gla_constants():
    L = GLA_CHUNK
    i = jnp.arange(L)[:, None]
    j = jnp.arange(L)[None, :]
    tri = (j <= i).astype(BF16)
    bdtri = jnp.logical_and(j <= i, i // GLA_SUB == j // GLA_SUB).astype(BF16)
    col = jnp.arange(N_SUB * LANES)[None, :]
    qmask = jnp.stack([jnp.logical_and(col // LANES == i // GLA_SUB, (col % LANES) // GLA_DK == hh)
                       for hh in range(2)]).astype(BF16)
    bdms = _block_diag(2 * GLA_DV, GLA_DV, 1.0 / GLA_DV, BF16)
    d = jnp.arange(LANES)[:, None] // GLA_DK
    e = jnp.arange(2 * GLA_DV)[None, :] // GLA_DV
    smask = (d == e).astype(F32)
    return tri, bdtri, qmask, bdms, smask


def kernel(x, c, w_ada, b_ada, g_norm1, w_in, q_norm, k_norm, sinks, w_gk2, b_gk, g_gla_out, g_att_out,
           w_out, g_norm2, w_group, b_group, w_router, b_router, w1, w3, w2):
    B, T, D = x.shape
    N = B * T
    depth = w_ada.shape[0]
    tri, bdtri, qmask, bdms, smask = _gla_constants()
    bdq = _block_diag(2 * LANES, HEAD_DIM, 1.0 / HEAD_DIM, BF16)
    bdk = _block_diag(ATT_KV, HEAD_DIM, 1.0 / HEAD_DIM, BF16)

    x2 = x.reshape(N, D)
    for l in range(depth):
        mod = _adaln_mod(c, w_ada[l], b_ada[l]).reshape(B, 6, D)

        wl = w_in[l].astype(BF16)
        lr0 = ATT_Q + 2 * ATT_KV + 2 * GLA_K + GLA_V
        w_in_p = jnp.concatenate(
            [wl[:, :lr0], wl[:, lr0 + GLA_RANK:], wl[:, lr0:lr0 + GLA_RANK],
             jnp.zeros((D, LANES - GLA_RANK), wl.dtype)], axis=1)
        wgk = jnp.concatenate([w_gk2[l], jnp.zeros((LANES - GLA_RANK, GLA_K), F32)], axis=0).astype(BF16)
        qa, ka, va, qg, kg, vg, la, og = _in_proj(
            x2, mod, g_norm1[l].reshape(1, D), w_in_p,
            jnp.tile(q_norm[l], ATT_HEADS).reshape(1, ATT_Q), jnp.tile(k_norm[l], ATT_KV_HEADS).reshape(1, ATT_KV),
            bdq, bdk, wgk, b_gk[l].reshape(1, GLA_K), T=T, tm=PROJ_TILE)

        y_att = _swa_attention(sinks[l], qa, ka, va, g_att_out[l].reshape(1, ATT_Q), B=B, T=T)
        y_gla = _gla(qg, kg, vg, la, og, tri, bdtri, qmask, bdms, smask,
                     jnp.tile(g_gla_out[l], GLA_HEADS).reshape(1, GLA_V), B=B, T=T)

        pad = LANES - N_GROUPS - N_EXPERTS
        wr_t = jnp.concatenate([w_group[l], w_router[l], jnp.zeros((D, pad), F32)], axis=1).T.astype(BF16)
        br_t = jnp.concatenate([b_group[l], b_router[l], jnp.zeros((pad,), F32)]).reshape(LANES, 1)
        striu = (jnp.arange(MOE_TILE)[:, None] < jnp.arange(MOE_TILE)[None, :]).astype(BF16)
        x1, rows_local, lpos, cnt = _out_proj(y_att, y_gla, x2, mod, w_out[l].astype(BF16),
                                              g_norm2[l].reshape(1, D), wr_t, br_t, striu, T=T)
        plan = _moe_plan(cnt[:, :N_GROUPS, 0].astype(jnp.int32))
        by_group = lambda w: w.astype(BF16).reshape((N_GROUPS, EXPERTS_PER_GROUP) + w.shape[1:])
        y_local = _moe(plan, rows_local, by_group(w1[l]), by_group(w3[l]), by_group(w2[l]))
        x2 = _combine(x1, mod, lpos, y_local, T=T)
    return x2.reshape(B, T, D)
```

```python
import functools

import jax
import jax.numpy as jnp
import numpy as np
from jax import lax
from jax.experimental import pallas as pl
from jax.experimental.pallas import tpu as pltpu

F32 = jnp.float32
BF16 = jnp.bfloat16

EPS = 1e-6
ATT_HEADS = 8
ATT_KV_HEADS = 2
HEAD_DIM = 64
WINDOW = 128
ATT_Q = ATT_HEADS * HEAD_DIM
ATT_KV = ATT_KV_HEADS * HEAD_DIM
GLA_HEADS = 4
GLA_DK = 64
GLA_DV = 128
GLA_RANK = 16
GLA_NORMALIZER = 16.0
GLA_K = GLA_HEADS * GLA_DK
GLA_V = GLA_HEADS * GLA_DV
N_GROUPS = 4
EXPERTS_PER_GROUP = 4
N_EXPERTS = N_GROUPS * EXPERTS_PER_GROUP
D_EXPERT = 256

LANES = 128
PROJ_SUB = 512
PROJ_TILE = 2 * PROJ_SUB
OUT_SUBS = 4
COMBINE_SUBS = 4
ROUTER_ROWS = 24
ATT_BLOCKS_PER_STEP = 4
GLA_BATCH_PER_STEP = 4
GLA_CHUNK = 128
GLA_SUB = 16
N_SUB = GLA_CHUNK // GLA_SUB
ROUTER_LANE0 = N_GROUPS
VMEM_LIMIT = 56 * 1024 * 1024
MOE_TILE = 256
CHUNK = 16
TILE_CHUNKS = MOE_TILE // CHUNK
LOCAL_CHUNKS = (MOE_TILE + N_GROUPS * (CHUNK - 1)) // CHUNK + 2
LOCAL_ROWS = LOCAL_CHUNKS * CHUNK

_QA0, _KA0, _VA0 = 0, ATT_Q, ATT_Q + ATT_KV
_QG0 = _VA0 + ATT_KV
_KG0 = _QG0 + GLA_K
_VG0 = _KG0 + GLA_K
_OG0 = _VG0 + GLA_V
_LR0 = _OG0 + GLA_V
IN_COLS_PAD = _LR0 + LANES


def _dot(a, b):
    return jnp.dot(a, b, preferred_element_type=F32)


def _dot_nt(a, b):
    return lax.dot_general(a, b, (((1,), (1,)), ((), ())), preferred_element_type=F32)


def _sigmoid(x):
    return 1.0 / (1.0 + jnp.exp(-x))


def _params(*sem):
    return pltpu.CompilerParams(dimension_semantics=sem, vmem_limit_bytes=VMEM_LIMIT)


def _mod_kernel(c_ref, w_ref, b_ref, o_ref):
    c = c_ref[...]
    s = (c * _sigmoid(c)).astype(BF16)
    o_ref[...] = _dot(s, w_ref[...].astype(BF16)) + b_ref[...]


def _adaln_mod(c, w_ada, b_ada):
    B, D = c.shape
    n = w_ada.shape[1]
    tn = 1536
    return pl.pallas_call(
        _mod_kernel,
        grid=(n // tn,),
        in_specs=[pl.BlockSpec((B, D), lambda j: (0, 0)),
                  pl.BlockSpec((D, tn), lambda j: (0, j)),
                  pl.BlockSpec((1, tn), lambda j: (0, j))],
        out_specs=pl.BlockSpec((B, tn), lambda j: (0, j)),
        out_shape=jax.ShapeDtypeStruct((B, n), F32),
        compiler_params=_params("arbitrary"),
        name="adaln_mod",
    )(c, w_ada, b_ada.reshape(1, n))


def _inproj_kernel(x_ref, mod_ref, g1_ref, w_ref, qn_ref, kn_ref, bdq_ref, bdk_ref, wgk_ref, bgk_ref,
                   qa_ref, ka_ref, va_ref, qg_ref, kg_ref, vg_ref, la_ref, og_ref):
    subs = range(x_ref.shape[0] // PROJ_SUB)
    rows = lambda t: slice(t * PROJ_SUB, (t + 1) * PROJ_SUB)

    h = []
    for t in subs:
        x = x_ref[rows(t), :]
        ms = jnp.mean(x * x, axis=-1, keepdims=True)
        xn = x * lax.rsqrt(ms + EPS) * g1_ref[...]
        h.append((xn * (1.0 + mod_ref[1:2, :]) + mod_ref[0:1, :]).astype(BF16))

    proj = lambda t, c0, width: _dot(h[t], w_ref[:, c0:c0 + width])
    gate_of = lambda lr: _dot(lr.astype(BF16), wgk_ref[...]) + bgk_ref[...]
    qa, kv, q_ms, k_ms, qk_g, vg, og, lr, gate = ({} for _ in range(9))
    for t in subs:
        qa[t] = proj(t, _QA0, ATT_Q)
        if t > 0:
            gate[t - 1] = gate_of(lr[t - 1])
        kv[t] = proj(t, _KA0, 2 * ATT_KV)
        q_sq = (qa[t] * qa[t]).astype(BF16)
        q_ms[t] = jnp.concatenate([_dot(q_sq[:, c:c + 2 * LANES], bdq_ref[...])
                                   for c in range(0, ATT_Q, 2 * LANES)], axis=1)
        qk_g[t] = proj(t, _QG0, 2 * GLA_K)
        k = kv[t][:, 0:ATT_KV]
        k_ms[t] = _dot((k * k).astype(BF16), bdk_ref[...])
        vg[t] = proj(t, _VG0, GLA_V)
        og[t] = proj(t, _OG0, GLA_V)
        lr[t] = proj(t, _LR0, LANES)
    gate[subs[-1]] = gate_of(lr[subs[-1]])

    low = lax.broadcasted_iota(jnp.int32, (PROJ_SUB, ATT_KV), 1) < HEAD_DIM
    for t in subs:
        qa_ref[rows(t), :] = (qa[t] * lax.rsqrt(q_ms[t] + EPS) * qn_ref[...] * (HEAD_DIM ** -0.5)).astype(BF16)
        k = kv[t][:, 0:ATT_KV] * lax.rsqrt(k_ms[t] + EPS) * kn_ref[...]
        v = kv[t][:, ATT_KV:2 * ATT_KV]
        for src, dst in ((k, ka_ref), (v, va_ref)):
            swapped = pltpu.roll(src, HEAD_DIM, axis=1)
            dst[rows(t), 0:LANES] = jnp.where(low, src, swapped).astype(BF16)
            dst[rows(t), LANES:2 * LANES] = jnp.where(low, swapped, src).astype(BF16)
        qg_ref[rows(t), :] = (qk_g[t][:, 0:GLA_K] * (GLA_DK ** -0.5)).astype(BF16)
        kg_ref[rows(t), :] = qk_g[t][:, GLA_K:2 * GLA_K].astype(BF16)
        vg_ref[rows(t), :] = vg[t].astype(BF16)
        og_ref[rows(t), :] = (og[t] * _sigmoid(og[t])).astype(BF16)
        log_sig = jnp.minimum(gate[t], 0.0) - jnp.log(1.0 + jnp.exp(-jnp.abs(gate[t])))
        la_ref[rows(t), :] = log_sig * (1.0 / GLA_NORMALIZER)


def _in_proj(x2, mod, g1, w_in_p, qn, kn, bdq, bdk, wgk, bgk, *, T, tm):
    N, D = x2.shape
    per_b = T // tm
    const = lambda shape: pl.BlockSpec(shape, lambda i: (0,) * len(shape))
    rows = lambda c: pl.BlockSpec((tm, c), lambda i: (i, 0))
    outs = [(ATT_Q, BF16), (2 * ATT_KV, BF16), (2 * ATT_KV, BF16), (GLA_K, BF16), (GLA_K, BF16),
            (GLA_V, BF16), (GLA_K, F32), (GLA_V, BF16)]
    return pl.pallas_call(
        _inproj_kernel,
        grid=(N // tm,),
        in_specs=[rows(D),
                  pl.BlockSpec((None, 6, D), lambda i: (i // per_b, 0, 0)),
                  const((1, D)), const(w_in_p.shape), const(qn.shape), const(kn.shape),
                  const(bdq.shape), const(bdk.shape), const(wgk.shape), const(bgk.shape)],
        out_specs=[rows(c) for c, _ in outs],
        out_shape=[jax.ShapeDtypeStruct((N, c), dt) for c, dt in outs],
        compiler_params=_params("arbitrary"),
        name="in_proj",
    )(x2, mod, g1, w_in_p, qn, kn, bdq, bdk, wgk, bgk)


def _attn_kernel(sinks_ref, q_ref, kc_ref, vc_ref, kp_ref, vp_ref, gatt_ref, o_ref):
    blk = WINDOW
    first = pl.program_id(1) == 0
    qi = lax.broadcasted_iota(jnp.int32, (blk, blk), 0)
    cj = lax.broadcasted_iota(jnp.int32, (blk, blk), 1)
    from_prev = cj > qi
    dist = (qi - cj + jnp.where(from_prev, blk, 0)).astype(F32)
    no_prev = jnp.where(jnp.logical_and(from_prev, first), -1e30, 0.0)
    low = cj < HEAD_DIM
    half = (jnp.where(low, 1.0, 0.0).astype(BF16), jnp.where(low, 0.0, 1.0).astype(BF16))
    half2 = tuple(jnp.concatenate([m, m], axis=0) for m in half)
    prev_mask = jnp.where(from_prev, 1.0, 0.0).astype(BF16)
    cur_mask = jnp.where(from_prev, 0.0, 1.0).astype(BF16)

    n_pairs = ATT_HEADS // 2
    pairs_per_kv = n_pairs // ATT_KV_HEADS
    units = [(bi, j) for bi in range(ATT_BLOCKS_PER_STEP) for j in range(n_pairs)]

    def kv_blocks(bi, g):
        rows = slice(bi * blk, (bi + 1) * blk)
        prev_rows = slice((bi - 1) * blk, bi * blk)
        lanes = slice(g * LANES, (g + 1) * LANES)
        cur = (kc_ref[rows, lanes], vc_ref[rows, lanes])
        prev = (kp_ref[:, lanes], vp_ref[:, lanes]) if bi == 0 else (kc_ref[prev_rows, lanes],
                                                                      vc_ref[prev_rows, lanes])
        return prev, cur

    scores = {}
    for bi, j in units:
        (kp, _), (kc, _) = kv_blocks(bi, j // pairs_per_kv)
        k_both = jnp.concatenate([kp, kc], axis=0)
        qp = q_ref[bi * blk:(bi + 1) * blk, j * LANES:(j + 1) * LANES]
        for p in range(2):
            scores[bi, j, p] = _dot_nt(qp, k_both * half2[p])

    probs, sink_terms = {}, {}
    for bi, j in units:
        for p in range(2):
            h = 2 * j + p
            slope = 2.0 ** (-8.0 * (h + 1) / ATT_HEADS)
            s_both = scores[bi, j, p]
            s = jnp.where(from_prev, s_both[:, 0:blk], s_both[:, blk:2 * blk]) - slope * dist
            if bi == 0:
                s = s + no_prev
            sink = sinks_ref[h]
            m = jnp.maximum(jnp.max(s, axis=-1, keepdims=True), sink)
            probs[bi, j, p] = jnp.exp(s - m)
            sink_terms[bi, j, p] = jnp.exp(sink - m)

    outs = {}
    for bi, j in units:
        (_, vp), (_, vc) = kv_blocks(bi, j // pairs_per_kv)
        v_stack = jnp.concatenate([jnp.concatenate([v * half[p], half[p]], axis=1)
                                   for v in (vp, vc) for p in range(2)], axis=0)
        e = [probs[bi, j, p].astype(BF16) for p in range(2)]
        p_all = jnp.concatenate([x * m for m in (prev_mask, cur_mask) for x in e], axis=1)
        pv = _dot(p_all, v_stack)
        den = pv[:, LANES:2 * LANES] + jnp.where(low, sink_terms[bi, j, 0], sink_terms[bi, j, 1])
        outs[bi, j] = pv[:, 0:LANES] / den

    for bi in range(ATT_BLOCKS_PER_STEP):
        o = jnp.concatenate([outs[bi, j] for j in range(n_pairs)], axis=1)
        ms = jnp.mean(o * o, axis=-1, keepdims=True)
        o_ref[bi * blk:(bi + 1) * blk, :] = (o * lax.rsqrt(ms + EPS) * gatt_ref[...]).astype(BF16)


def _swa_attention(sinks, qa, ka, va, gatt, *, B, T):
    step_rows = ATT_BLOCKS_PER_STEP * WINDOW
    nb = T // step_rows
    cur = lambda c: pl.BlockSpec((step_rows, c), lambda b, i: (b * nb + i, 0))
    prev = lambda c: pl.BlockSpec(
        (WINDOW, c), lambda b, i: (b * (T // WINDOW) + jnp.maximum(i * ATT_BLOCKS_PER_STEP - 1, 0), 0))
    return pl.pallas_call(
        _attn_kernel,
        grid=(B, nb),
        in_specs=[pl.BlockSpec(memory_space=pltpu.SMEM),
                  cur(ATT_Q), cur(2 * ATT_KV), cur(2 * ATT_KV), prev(2 * ATT_KV), prev(2 * ATT_KV),
                  pl.BlockSpec((1, ATT_Q), lambda b, i: (0, 0))],
        out_specs=cur(ATT_Q),
        out_shape=jax.ShapeDtypeStruct((B * T, ATT_Q), BF16),
        compiler_params=_params("arbitrary", "arbitrary"),
        name="swa_attn",
    )(sinks, qa, ka, va, ka, va, gatt)


def _gla_kernel(q_ref, k_ref, v_ref, la_ref, og_ref, cum_ref, qmask_ref, bdms_ref, smask_ref,
                ggla_ref, o_ref, state_ref):
    L = GLA_CHUNK

    @pl.when(pl.program_id(1) == 0)
    def _():
        state_ref[...] = jnp.zeros_like(state_ref)

    rows = lax.broadcasted_iota(jnp.int32, (L, LANES), 0)
    causal_t = lax.broadcasted_iota(jnp.int32, (L, L), 0) <= lax.broadcasted_iota(jnp.int32, (L, L), 1)
    seqs = range(GLA_BATCH_PER_STEP)
    units = [(s, pair) for s in seqs for pair in range(GLA_HEADS // 2)]
    kl = lambda pair: slice(pair * LANES, (pair + 1) * LANES)
    vl = lambda pair: slice(pair * 2 * GLA_DV, (pair + 1) * 2 * GLA_DV)

    b, b_in = {}, {}
    for s in seqs:
        la = la_ref[s]
        la_hi = la.astype(BF16)
        la_lo = (la - la_hi.astype(F32)).astype(BF16)
        sums = _dot(cum_ref[...], jnp.concatenate([la_hi, la_lo], axis=0))
        b[s] = sums[0:L, :]
        b_in[s] = sums[L:2 * L, :]

    q_both, keys, q_dec, k_dec_t, b_last = {}, {}, {}, {}, {}
    for s in seqs:
        ref = b[s] - b_in[s]
        b_last[s] = b[s][L - 1:L, :]
        q = q_ref[s].astype(F32)
        k = k_ref[s].astype(F32)
        q_in = (q * jnp.exp(b_in[s])).astype(BF16)
        q_dec[s] = (q * jnp.exp(b[s])).astype(BF16)
        k_dec = k * jnp.exp(b_last[s] - b[s])
        ref_next = jnp.concatenate([ref[GLA_SUB:, :], ref[L - GLA_SUB:, :]], axis=0)
        k_earlier = k * jnp.exp(ref_next - b[s])
        k_same = k * jnp.exp(-b_in[s])
        group_rows = lambda a, g: a[g * GLA_SUB:(g + 1) * GLA_SUB, :]
        zero_rows = jnp.zeros((GLA_SUB, LANES), F32)
        for pair in range(GLA_HEADS // 2):
            ke, ks, ref_p = k_earlier[:, kl(pair)], k_same[:, kl(pair)], ref[:, kl(pair)]
            expanded = []
            for g in range(N_SUB):
                ref_g = ref_p[g * GLA_SUB:g * GLA_SUB + 1, :]
                pieces = [group_rows(ke, e) * jnp.exp(ref_g - ref_p[(e + 1) * GLA_SUB:(e + 1) * GLA_SUB + 1, :])
                          for e in range(g)]
                pieces += [group_rows(ks, g)] + [zero_rows] * (N_SUB - 1 - g)
                expanded.append(jnp.concatenate(pieces, axis=0).astype(BF16))
            keys[s, pair] = jnp.concatenate(expanded, axis=1)
            q_rep = jnp.concatenate([q_in[:, kl(pair)]] * N_SUB, axis=1)
            q_both[s, pair] = jnp.concatenate([q_rep * qmask_ref[0], q_rep * qmask_ref[1]], axis=0)
            k_dec_t[s, pair] = k_dec[:, kl(pair)].T.astype(BF16)

    scores_t = {u: _dot_nt(keys[u], q_both[u]) for u in units}

    outs, updates = {}, {}
    for s, pair in units:
        v_p = v_ref[s, :, vl(pair)]
        o_parts = []
        for hh in range(2):
            a = jnp.where(causal_t, scores_t[s, pair][:, hh * L:(hh + 1) * L], 0.0).T.astype(BF16)
            o_parts.append(_dot(a, v_p[:, hh * GLA_DV:(hh + 1) * GLA_DV]))
        state = state_ref[s, pair]
        outs[s, pair] = jnp.concatenate(o_parts, axis=1) + _dot(q_dec[s][:, kl(pair)], state.astype(BF16))
        updates[s, pair] = _dot(k_dec_t[s, pair], v_p)

    for s, pair in units:
        decay = jnp.broadcast_to(jnp.exp(b_last[s][:, kl(pair)]), (LANES, LANES)).T
        state_ref[s, pair] = (state_ref[s, pair] * jnp.concatenate([decay, decay], axis=1)
                              + updates[s, pair] * smask_ref[...])
        o = outs[s, pair]
        ms = _dot((o * o).astype(BF16), bdms_ref[...])
        y = o * lax.rsqrt(ms + EPS) * ggla_ref[:, vl(pair)] * og_ref[s, :, vl(pair)].astype(F32)
        o_ref[s, :, vl(pair)] = y.astype(BF16)


def _gla(qg, kg, vg, la, og, cum, qmask, bdms, smask, ggla, *, B, T):
    L = GLA_CHUNK
    nb = GLA_BATCH_PER_STEP
    seq = lambda a: a.reshape(B, T, a.shape[-1])
    rows = lambda c: pl.BlockSpec((nb, L, c), lambda b, i: (b, i, 0))
    const = lambda a: pl.BlockSpec(a.shape, lambda b, i: (0,) * a.ndim)
    out = pl.pallas_call(
        _gla_kernel,
        grid=(B // nb, T // L),
        in_specs=[rows(GLA_K), rows(GLA_K), rows(GLA_V), rows(GLA_K), rows(GLA_V),
                  const(cum), const(qmask), const(bdms), const(smask), const(ggla)],
        out_specs=rows(GLA_V),
        out_shape=jax.ShapeDtypeStruct((B, T, GLA_V), BF16),
        scratch_shapes=[pltpu.VMEM((nb, GLA_HEADS // 2, LANES, 2 * GLA_DV), F32)],
        compiler_params=_params("arbitrary", "arbitrary"),
        name="gla",
    )(seq(qg), seq(kg), seq(vg), seq(la), seq(og), cum, qmask, bdms, smask, ggla)
    return out.reshape(B * T, GLA_V)


def _route(logits_t):
    lt = logits_t[0:ROUTER_ROWS, :]
    row = lax.broadcasted_iota(jnp.int32, lt.shape, 0)
    neg_inf = -jnp.inf
    g_log = jnp.where(row < N_GROUPS, lt, neg_inf)
    g_max = jnp.max(g_log, axis=0, keepdims=True)
    g_sel = jnp.min(jnp.where(g_log == g_max, row, LANES), axis=0, keepdims=True)
    g_sum = jnp.sum(jnp.where(row < N_GROUPS, jnp.exp(lt - g_max), 0.0), axis=0, keepdims=True)
    p_group = 1.0 / g_sum
    e_lo = ROUTER_LANE0 + EXPERTS_PER_GROUP * g_sel
    in_group = jnp.logical_and(row >= e_lo, row < e_lo + EXPERTS_PER_GROUP)
    e_log = jnp.where(in_group, lt, neg_inf)
    e_max = jnp.max(e_log, axis=0, keepdims=True)
    top1 = jnp.min(jnp.where(e_log == e_max, row, LANES), axis=0, keepdims=True)
    e_log2 = jnp.where(row == top1, neg_inf, e_log)
    e_max2 = jnp.max(e_log2, axis=0, keepdims=True)
    top2 = jnp.min(jnp.where(e_log2 == e_max2, row, LANES), axis=0, keepdims=True)
    ratio = jnp.exp(e_max2 - e_max)
    w_top1 = p_group / (1.0 + ratio)
    w_top2 = p_group * ratio / (1.0 + ratio)
    row8 = lax.broadcasted_iota(jnp.int32, (8, lt.shape[1]), 0)
    weights = jnp.where(row8 == top1 - e_lo, w_top1, 0.0) + jnp.where(row8 == top2 - e_lo, w_top2, 0.0)
    return g_sel, weights


def _outproj_kernel(ya_ref, yg_ref, x_ref, mod_ref, wo_ref, g2_ref, wrt_ref, brt_ref, striu_ref,
                    x1_ref, row_ref, lpos_ref, cnt_ref):
    tm = MOE_TILE
    subs = range(x_ref.shape[0] // tm)
    rows = lambda t: slice(t * tm, (t + 1) * tm)

    mix = [_dot(ya_ref[rows(t), :], wo_ref[0:ATT_Q, :]) + _dot(yg_ref[rows(t), :], wo_ref[ATT_Q:ATT_Q + GLA_V, :])
           for t in subs]
    h2b = []
    for t in subs:
        x1 = x_ref[rows(t), :] + mod_ref[2:3, :] * mix[t]
        x1_ref[rows(t), :] = x1
        ms = jnp.mean(x1 * x1, axis=-1, keepdims=True)
        h2 = (x1 * lax.rsqrt(ms + EPS) * g2_ref[...]) * (1.0 + mod_ref[4:5, :]) + mod_ref[3:4, :]
        h2b.append(h2.astype(BF16))
    logits_t = [_dot_nt(wrt_ref[...], h2b[t]) + brt_ref[...] for t in subs]

    routed = [_route(logits_t[t]) for t in subs]
    row8 = lax.broadcasted_iota(jnp.int32, (8, tm), 0)
    onehot = [jnp.where(row8 == routed[t][0], 1.0, 0.0) for t in subs]
    before = [_dot(onehot[t].astype(BF16), striu_ref[...]) for t in subs]

    local_row = lax.broadcasted_iota(jnp.int32, (LOCAL_ROWS, tm), 0).astype(F32)
    pad_rows = jnp.zeros((LANES - 8, tm), F32)
    for t in subs:
        count = jnp.sum(onehot[t], axis=1, keepdims=True)
        cnt_ref[t] = jnp.broadcast_to(count, (8, LANES))
        padded = jnp.broadcast_to(jnp.floor((count + (CHUNK - 1.0)) * (1.0 / CHUNK)) * CHUNK, (8, tm))
        start = jnp.zeros((8, tm), F32)
        for shift in range(1, N_GROUPS):
            start = start + jnp.where(row8 >= shift, pltpu.roll(padded, shift, axis=0), 0.0)
        lpos = jnp.sum(onehot[t] * (before[t] + start), axis=0, keepdims=True)
        lpos_ref[rows(t), :] = jnp.broadcast_to(lpos, (LANES, tm)).T
        weights = jnp.concatenate([routed[t][1], pad_rows], axis=0).T
        w_hi = weights.astype(BF16)
        w_lo = (weights - w_hi.astype(F32)).astype(BF16)
        perm = jnp.where(local_row == lpos, 1.0, 0.0).astype(BF16)
        row_ref[t * LOCAL_ROWS:(t + 1) * LOCAL_ROWS, :] = _dot(
            perm, jnp.concatenate([h2b[t], w_hi, w_lo], axis=1)).astype(BF16)


def _out_proj(ya, yg, x2, mod, wo, g2, wr, br, stril, *, T):
    N, D = x2.shape
    subs = OUT_SUBS
    tm = subs * MOE_TILE
    per_b = T // tm
    rows = lambda c: pl.BlockSpec((tm, c), lambda i: (i, 0))
    const = lambda a: pl.BlockSpec(a.shape, lambda i: (0,) * a.ndim)
    return pl.pallas_call(
        _outproj_kernel,
        grid=(N // tm,),
        in_specs=[rows(ATT_Q), rows(GLA_V), rows(D),
                  pl.BlockSpec((None, 6, D), lambda i: (i // per_b, 0, 0)),
                  const(wo), const(g2), const(wr), const(br), const(stril)],
        out_specs=[rows(D), pl.BlockSpec((subs * LOCAL_ROWS, D + 2 * LANES), lambda i: (i, 0)), rows(LANES),
                   pl.BlockSpec((subs, 8, LANES), lambda i: (i, 0, 0))],
        out_shape=[jax.ShapeDtypeStruct((N, D), F32),
                   jax.ShapeDtypeStruct((N // MOE_TILE * LOCAL_ROWS, D + 2 * LANES), BF16),
                   jax.ShapeDtypeStruct((N, LANES), F32),
                   jax.ShapeDtypeStruct((N // MOE_TILE, 8, LANES), F32)],
        compiler_params=_params("arbitrary"),
        name="out_proj",
    )(ya, yg, x2, mod, wo, g2, wr, br, stril)


def _chunk_copy(src_ref, src_chunk, dst_ref, dst_chunk, sem):
    return pltpu.make_async_copy(src_ref.at[src_chunk], dst_ref.at[dst_chunk], sem)


def _moe_kernel(src_ref, nvalid_ref, grp_ref, nt_ref, used_ref, rows_ref, w1f_ref, w3f_ref, w2f_ref, y_ref,
                in_buf, out_buf, zero_buf, w1_ref, w3_ref, w2_ref, in_sem, out_sem, zero_sem):
    j = pl.program_id(0)
    n_tiles = nt_ref[0]
    d_model = w2_ref.shape[2]

    @pl.when(jnp.logical_or(j == 0, grp_ref[j] != grp_ref[jnp.maximum(j - 1, 0)]))
    def _():
        for src, dst in ((w1f_ref, w1_ref), (w3f_ref, w3_ref), (w2f_ref, w2_ref)):
            for k in range(EXPERTS_PER_GROUP):
                dst[k] = src[k].astype(BF16)

    def gather(tile, slot):
        def body(k, carry):
            _chunk_copy(rows_ref, src_ref[tile * TILE_CHUNKS + k], in_buf.at[slot], k, in_sem.at[slot]).start()
            return carry
        lax.fori_loop(0, TILE_CHUNKS, body, 0, unroll=True)

    def wait_gather(slot):
        def body(k, carry):
            _chunk_copy(rows_ref, 0, in_buf.at[slot], k, in_sem.at[slot]).wait()
            return carry
        lax.fori_loop(0, TILE_CHUNKS, body, 0, unroll=True)

    def scatter(tile, slot):
        def body(k, carry):
            _chunk_copy(out_buf.at[slot], k, y_ref, src_ref[tile * TILE_CHUNKS + k], out_sem.at[slot]).start()
            return carry
        lax.fori_loop(0, nvalid_ref[tile], body, 0)

    def wait_scatter(tile, slot):
        def body(k, carry):
            _chunk_copy(out_buf.at[slot], k, y_ref, 0, out_sem.at[slot]).wait()
            return carry
        lax.fori_loop(0, nvalid_ref[tile], body, 0)

    def zero_fill(wait):
        def per_tile(i, carry):
            def body(c, inner):
                copy = _chunk_copy(zero_buf, 0, y_ref, i * LOCAL_CHUNKS + c, zero_sem)
                if wait:
                    copy.wait()
                else:
                    copy.start()
                return inner
            return lax.fori_loop(used_ref[i], LOCAL_CHUNKS, body, carry)
        lax.fori_loop(0, used_ref.shape[0], per_tile, 0)

    @pl.when(j == 0)
    def _():
        zero_buf[...] = jnp.zeros_like(zero_buf)
        zero_fill(wait=False)
        gather(0, 0)

    @pl.when(j + 1 < n_tiles)
    def _():
        gather(j + 1, (j + 1) % 2)

    @pl.when(j < n_tiles)
    def _():
        slot = j % 2
        wait_gather(slot)
        rows = in_buf[slot].reshape(MOE_TILE, in_buf.shape[-1])
        h = rows[:, 0:d_model]
        weights = rows[:, d_model:d_model + LANES].astype(F32) + rows[:, d_model + LANES:].astype(F32)
        experts = range(EXPERTS_PER_GROUP)
        up = [(_dot(h, w1_ref[k]), _dot(h, w3_ref[k])) for k in experts]
        hid = [(a * _sigmoid(a) * g * weights[:, k:k + 1]).astype(BF16) for k, (a, g) in zip(experts, up)]
        y = _dot(hid[0], w2_ref[0])
        for k in experts[1:]:
            y = y + _dot(hid[k], w2_ref[k])

        @pl.when(j >= 2)
        def _():
            wait_scatter(j - 2, slot)

        out_buf[slot] = y.astype(BF16).reshape(TILE_CHUNKS, CHUNK, d_model)
        scatter(j, slot)

        @pl.when(j == n_tiles - 1)
        def _():
            @pl.when(j >= 1)
            def _():
                wait_scatter(j - 1, 1 - slot)
            wait_scatter(j, slot)
            zero_fill(wait=True)


def _combine_kernel(x1_ref, mod_ref, lpos_ref, y_ref, o_ref):
    tm = MOE_TILE
    local_row = lax.broadcasted_iota(jnp.int32, (tm, LOCAL_ROWS), 1).astype(F32)
    for t in range(x1_ref.shape[0] // tm):
        rows = slice(t * tm, (t + 1) * tm)
        unsort = jnp.where(local_row == lpos_ref[rows, 0:1], 1.0, 0.0).astype(BF16)
        y = _dot(unsort, y_ref[t * LOCAL_ROWS:(t + 1) * LOCAL_ROWS, :])
        o_ref[rows, :] = x1_ref[rows, :] + mod_ref[5:6, :] * y


def _moe_plan(cnt):
    n_local = cnt.shape[0]
    chunks = (cnt + CHUNK - 1) // CHUNK
    used = jnp.sum(chunks, axis=1)
    local_off = jnp.cumsum(chunks, axis=1) - chunks
    tiles_g = (jnp.sum(chunks, axis=0) + TILE_CHUNKS - 1) // TILE_CHUNKS
    tile_end = jnp.cumsum(tiles_g)
    n_tiles = tile_end[-1]
    group_start = (tile_end - tiles_g) * TILE_CHUNKS
    seg_len = chunks.T.reshape(-1)
    seg_start = (group_start[:, None] + (jnp.cumsum(chunks, axis=0) - chunks).T).reshape(-1)
    seg_src = (jnp.arange(n_local)[None, :] * LOCAL_CHUNKS + local_off.T).reshape(-1)
    max_chunks = n_local * MOE_TILE // CHUNK + n_local * N_GROUPS + N_GROUPS * TILE_CHUNKS
    max_tiles = (max_chunks + TILE_CHUNKS - 1) // TILE_CHUNKS
    c = jnp.arange(max_tiles * TILE_CHUNKS)[:, None]
    within = c - seg_start[None, :]
    hit = jnp.logical_and(within >= 0, within < seg_len[None, :])
    valid = jnp.any(hit, axis=1)
    src = jnp.sum(jnp.where(hit, seg_src[None, :] + within, 0), axis=1)
    src = jnp.where(valid, src, LOCAL_CHUNKS - 1)
    nvalid = jnp.sum(valid.reshape(max_tiles, TILE_CHUNKS), axis=1)
    j = jnp.minimum(jnp.arange(max_tiles), n_tiles - 1)
    grp = jnp.sum(j[:, None] >= tile_end[None, :], axis=1)
    i32 = lambda a: a.astype(jnp.int32)
    return i32(src), i32(nvalid), i32(grp), i32(n_tiles).reshape(1), i32(used)


def _moe(plan, rows_local, w1g, w3g, w2g):
    src, nvalid, grp, n_tiles, used = plan
    D = w2g.shape[3]
    n_rows, cols = rows_local.shape
    weights = lambda a: pl.BlockSpec((None,) + a.shape[1:], lambda j, src, nv, grp, nt, used: (grp[j], 0, 0, 0))
    y = pl.pallas_call(
        _moe_kernel,
        grid_spec=pltpu.PrefetchScalarGridSpec(
            num_scalar_prefetch=5,
            grid=(grp.shape[0],),
            in_specs=[pl.BlockSpec(memory_space=pl.ANY), weights(w1g), weights(w3g), weights(w2g)],
            out_specs=pl.BlockSpec(memory_space=pl.ANY),
            scratch_shapes=[pltpu.VMEM((2, TILE_CHUNKS, CHUNK, cols), BF16),
                            pltpu.VMEM((2, TILE_CHUNKS, CHUNK, D), BF16),
                            pltpu.VMEM((1, CHUNK, D), BF16),
                            pltpu.VMEM(w1g.shape[1:], BF16), pltpu.VMEM(w3g.shape[1:], BF16),
                            pltpu.VMEM(w2g.shape[1:], BF16),
                            pltpu.SemaphoreType.DMA((2,)), pltpu.SemaphoreType.DMA((2,)),
                            pltpu.SemaphoreType.DMA(())]),
        out_shape=jax.ShapeDtypeStruct((n_rows // CHUNK, CHUNK, D), BF16),
        compiler_params=_params("arbitrary"),
        name="moe",
    )(src, nvalid, grp, n_tiles, used, rows_local.reshape(n_rows // CHUNK, CHUNK, cols), w1g, w3g, w2g)
    return y.reshape(n_rows, D)


def _combine(x1, mod, lpos, y_local, *, T):
    N, D = x1.shape
    subs = COMBINE_SUBS
    tm = subs * MOE_TILE
    per_b = T // tm
    return pl.pallas_call(
        _combine_kernel,
        grid=(N // tm,),
        in_specs=[pl.BlockSpec((tm, D), lambda i: (i, 0)),
                  pl.BlockSpec((None, 6, D), lambda i: (i // per_b, 0, 0)),
                  pl.BlockSpec((tm, LANES), lambda i: (i, 0)),
                  pl.BlockSpec((subs * LOCAL_ROWS, D), lambda i: (i, 0))],
        out_specs=pl.BlockSpec((tm, D), lambda i: (i, 0)),
        out_shape=jax.ShapeDtypeStruct((N, D), F32),
        compiler_params=_params("arbitrary"),
        name="moe_combine",
    )(x1, mod, lpos, y_local)


def _block_diag(n, blk, value, dtype):
    r = np.arange(n)[:, None] // blk
    c = np.arange(n)[None, :] // blk
    return jnp.asarray(np.where(r == c, value, 0.0), dtype)


def _gla_constants():
    L = GLA_CHUNK
    i = np.arange(L)[:, None]
    j = np.arange(L)[None, :]
    tri = j <= i
    bdtri = np.logical_and(j <= i, i // GLA_SUB == j // GLA_SUB)
    cum = np.block([[tri, tri], [bdtri, bdtri]])
    col = np.arange(N_SUB * LANES)[None, :]
    qmask = np.stack([np.logical_and(col // LANES == i // GLA_SUB, (col % LANES) // GLA_DK == hh)
                      for hh in range(2)])
    bdms = _block_diag(2 * GLA_DV, GLA_DV, 1.0 / GLA_DV, BF16)
    d = np.arange(LANES)[:, None] // GLA_DK
    e = np.arange(2 * GLA_DV)[None, :] // GLA_DV
    smask = d == e
    return (jnp.asarray(cum, BF16), jnp.asarray(qmask, BF16), bdms, jnp.asarray(smask, F32))


def kernel(x, c, w_ada, b_ada, g_norm1, w_in, q_norm, k_norm, sinks, w_gk2, b_gk, g_gla_out, g_att_out,
           w_out, g_norm2, w_group, b_group, w_router, b_router, w1, w3, w2):
    B, T, D = x.shape
    N = B * T
    depth = w_ada.shape[0]
    cum, qmask, bdms, smask = _gla_constants()
    bdq = _block_diag(2 * LANES, HEAD_DIM, 1.0 / HEAD_DIM, BF16)
    bdk = _block_diag(ATT_KV, HEAD_DIM, 1.0 / HEAD_DIM, BF16)

    x2 = x.reshape(N, D)
    for l in range(depth):
        mod = _adaln_mod(c, w_ada[l], b_ada[l]).reshape(B, 6, D)

        wl = w_in[l].astype(BF16)
        lr0 = ATT_Q + 2 * ATT_KV + 2 * GLA_K + GLA_V
        w_in_p = jnp.concatenate(
            [wl[:, :lr0], wl[:, lr0 + GLA_RANK:], wl[:, lr0:lr0 + GLA_RANK],
             jnp.zeros((D, LANES - GLA_RANK), wl.dtype)], axis=1)
        wgk = jnp.concatenate([w_gk2[l], jnp.zeros((LANES - GLA_RANK, GLA_K), F32)], axis=0).astype(BF16)
        qa, ka, va, qg, kg, vg, la, og = _in_proj(
            x2, mod, g_norm1[l].reshape(1, D), w_in_p,
            jnp.tile(q_norm[l], ATT_HEADS).reshape(1, ATT_Q), jnp.tile(k_norm[l], ATT_KV_HEADS).reshape(1, ATT_KV),
            bdq, bdk, wgk, b_gk[l].reshape(1, GLA_K), T=T, tm=PROJ_TILE)

        y_att = _swa_attention(sinks[l], qa, ka, va, g_att_out[l].reshape(1, ATT_Q), B=B, T=T)
        y_gla = _gla(qg, kg, vg, la, og, cum, qmask, bdms, smask,
                     jnp.tile(g_gla_out[l], GLA_HEADS).reshape(1, GLA_V), B=B, T=T)

        pad = LANES - N_GROUPS - N_EXPERTS
        wr_t = jnp.concatenate([w_group[l], w_router[l], jnp.zeros((D, pad), F32)], axis=1).T.astype(BF16)
        br_t = jnp.concatenate([b_group[l], b_router[l], jnp.zeros((pad,), F32)]).reshape(LANES, 1)
        striu = jnp.asarray(np.arange(MOE_TILE)[:, None] < np.arange(MOE_TILE)[None, :], BF16)
        x1, rows_local, lpos, cnt = _out_proj(y_att, y_gla, x2, mod, w_out[l].astype(BF16),
                                              g_norm2[l].reshape(1, D), wr_t, br_t, striu, T=T)
        plan = _moe_plan(cnt[:, :N_GROUPS, 0].astype(jnp.int32))
        by_group = lambda w: w.reshape((N_GROUPS, EXPERTS_PER_GROUP) + w.shape[1:])
        y_local = _moe(plan, rows_local, by_group(w1[l]), by_group(w3[l]), by_group(w2[l]))
        x2 = _combine(x1, mod, lpos, y_local, T=T)
    return x2.reshape(B, T, D)
```

```python
import functools

import jax
import jax.numpy as jnp
import numpy as np
from jax import lax
from jax.experimental import pallas as pl
from jax.experimental.pallas import tpu as pltpu

F32 = jnp.float32
BF16 = jnp.bfloat16

EPS = 1e-6
ATT_HEADS = 8
ATT_KV_HEADS = 2
HEAD_DIM = 64
WINDOW = 128
ATT_Q = ATT_HEADS * HEAD_DIM
ATT_KV = ATT_KV_HEADS * HEAD_DIM
GLA_HEADS = 4
GLA_DK = 64
GLA_DV = 128
GLA_RANK = 16
GLA_NORMALIZER = 16.0
GLA_K = GLA_HEADS * GLA_DK
GLA_V = GLA_HEADS * GLA_DV
N_GROUPS = 4
EXPERTS_PER_GROUP = 4
N_EXPERTS = N_GROUPS * EXPERTS_PER_GROUP
D_EXPERT = 256

LANES = 128
PROJ_SUB = 512
PROJ_TILE = 2 * PROJ_SUB
OUT_SUBS = 4
COMBINE_SUBS = 4
ROUTER_ROWS = 24
ATT_BLOCKS_PER_STEP = 4
GLA_BATCH_PER_STEP = 4
GLA_CHUNK = 128
GLA_SUB = 16
N_SUB = GLA_CHUNK // GLA_SUB
ROUTER_LANE0 = N_GROUPS
VMEM_LIMIT = 56 * 1024 * 1024
MOE_TILE = 256
CHUNK = 16
TILE_CHUNKS = MOE_TILE // CHUNK
LOCAL_CHUNKS = (MOE_TILE + N_GROUPS * (CHUNK - 1)) // CHUNK + 2
LOCAL_ROWS = LOCAL_CHUNKS * CHUNK

_QA0, _KA0, _VA0 = 0, ATT_Q, ATT_Q + ATT_KV
_QG0 = _VA0 + ATT_KV
_KG0 = _QG0 + GLA_K
_VG0 = _KG0 + GLA_K
_OG0 = _VG0 + GLA_V
_LR0 = _OG0 + GLA_V
IN_COLS_PAD = _LR0 + LANES


def _dot(a, b):
    return jnp.dot(a, b, preferred_element_type=F32)


def _dot_nt(a, b):
    return lax.dot_general(a, b, (((1,), (1,)), ((), ())), preferred_element_type=F32)


def _sigmoid(x):
    return 1.0 / (1.0 + jnp.exp(-x))


def _params(*sem):
    return pltpu.CompilerParams(dimension_semantics=sem, vmem_limit_bytes=VMEM_LIMIT)


def _mod_kernel(c_ref, w_ref, b_ref, o_ref):
    c = c_ref[...]
    s = (c * _sigmoid(c)).astype(BF16)
    o_ref[...] = _dot(s, w_ref[...].astype(BF16)) + b_ref[...]


def _adaln_mod(c, w_ada, b_ada):
    B, D = c.shape
    n = w_ada.shape[1]
    tn = 1536
    return pl.pallas_call(
        _mod_kernel,
        grid=(n // tn,),
        in_specs=[pl.BlockSpec((B, D), lambda j: (0, 0)),
                  pl.BlockSpec((D, tn), lambda j: (0, j)),
                  pl.BlockSpec((1, tn), lambda j: (0, j))],
        out_specs=pl.BlockSpec((B, tn), lambda j: (0, j)),
        out_shape=jax.ShapeDtypeStruct((B, n), F32),
        compiler_params=_params("arbitrary"),
        name="adaln_mod",
    )(c, w_ada, b_ada.reshape(1, n))


def _inproj_kernel(x_ref, mod_ref, g1_ref, wf_ref, qn_ref, kn_ref, bdq_ref, bdk_ref, wgk_ref, bgk_ref,
                   qa_ref, ka_ref, va_ref, qg_ref, kg_ref, vg_ref, la_ref, og_ref, w_ref):
    subs = range(x_ref.shape[0] // PROJ_SUB)
    rows = lambda t: slice(t * PROJ_SUB, (t + 1) * PROJ_SUB)

    @pl.when(pl.program_id(0) == 0)
    def _():
        lr_src = _OG0
        w_ref[:, 0:_OG0] = wf_ref[:, 0:_OG0].astype(BF16)
        w_ref[:, _OG0:_LR0] = wf_ref[:, lr_src + GLA_RANK:lr_src + GLA_RANK + GLA_V].astype(BF16)
        w_ref[:, _LR0:IN_COLS_PAD] = jnp.concatenate(
            [wf_ref[:, lr_src:lr_src + GLA_RANK], jnp.zeros((wf_ref.shape[0], LANES - GLA_RANK), F32)],
            axis=1).astype(BF16)

    h = []
    for t in subs:
        x = x_ref[rows(t), :]
        ms = jnp.mean(x * x, axis=-1, keepdims=True)
        xn = x * lax.rsqrt(ms + EPS) * g1_ref[...]
        h.append((xn * (1.0 + mod_ref[1:2, :]) + mod_ref[0:1, :]).astype(BF16))

    proj = lambda t, c0, width: _dot(h[t], w_ref[:, c0:c0 + width])
    gate_of = lambda lr: _dot(lr.astype(BF16), wgk_ref[...]) + bgk_ref[...]
    qa, kv, q_ms, k_ms, qk_g, vg, og, lr, gate = ({} for _ in range(9))
    for t in subs:
        qa[t] = proj(t, _QA0, ATT_Q)
        if t > 0:
            gate[t - 1] = gate_of(lr[t - 1])
        kv[t] = proj(t, _KA0, 2 * ATT_KV)
        q_sq = (qa[t] * qa[t]).astype(BF16)
        q_ms[t] = jnp.concatenate([_dot(q_sq[:, c:c + 2 * LANES], bdq_ref[...])
                                   for c in range(0, ATT_Q, 2 * LANES)], axis=1)
        qk_g[t] = proj(t, _QG0, 2 * GLA_K)
        k = kv[t][:, 0:ATT_KV]
        k_ms[t] = _dot((k * k).astype(BF16), bdk_ref[...])
        vg[t] = proj(t, _VG0, GLA_V)
        og[t] = proj(t, _OG0, GLA_V)
        lr[t] = proj(t, _LR0, LANES)
    gate[subs[-1]] = gate_of(lr[subs[-1]])

    low = lax.broadcasted_iota(jnp.int32, (PROJ_SUB, ATT_KV), 1) < HEAD_DIM
    for t in subs:
        qa_ref[rows(t), :] = (qa[t] * lax.rsqrt(q_ms[t] + EPS) * qn_ref[...] * (HEAD_DIM ** -0.5)).astype(BF16)
        k = kv[t][:, 0:ATT_KV] * lax.rsqrt(k_ms[t] + EPS) * kn_ref[...]
        v = kv[t][:, ATT_KV:2 * ATT_KV]
        for src, dst in ((k, ka_ref), (v, va_ref)):
            swapped = pltpu.roll(src, HEAD_DIM, axis=1)
            dst[rows(t), 0:LANES] = jnp.where(low, src, swapped).astype(BF16)
            dst[rows(t), LANES:2 * LANES] = jnp.where(low, swapped, src).astype(BF16)
        qg_ref[rows(t), :] = (qk_g[t][:, 0:GLA_K] * (GLA_DK ** -0.5)).astype(BF16)
        kg_ref[rows(t), :] = qk_g[t][:, GLA_K:2 * GLA_K].astype(BF16)
        vg_ref[rows(t), :] = vg[t].astype(BF16)
        og_ref[rows(t), :] = (og[t] * _sigmoid(og[t])).astype(BF16)
        log_sig = jnp.minimum(gate[t], 0.0) - jnp.log(1.0 + jnp.exp(-jnp.abs(gate[t])))
        la_ref[rows(t), :] = log_sig * (1.0 / GLA_NORMALIZER)


def _in_proj(x2, mod, g1, w_in, qn, kn, bdq, bdk, wgk, bgk, *, T, tm):
    N, D = x2.shape
    per_b = T // tm
    const = lambda shape: pl.BlockSpec(shape, lambda i: (0,) * len(shape))
    rows = lambda c: pl.BlockSpec((tm, c), lambda i: (i, 0))
    outs = [(ATT_Q, BF16), (2 * ATT_KV, BF16), (2 * ATT_KV, BF16), (GLA_K, BF16), (GLA_K, BF16),
            (GLA_V, BF16), (GLA_K, F32), (GLA_V, BF16)]
    return pl.pallas_call(
        _inproj_kernel,
        grid=(N // tm,),
        in_specs=[rows(D),
                  pl.BlockSpec((None, 6, D), lambda i: (i // per_b, 0, 0)),
                  const((1, D)),
                  pl.BlockSpec(w_in.shape, lambda i: (0, 0), pipeline_mode=pl.Buffered(1)),
                  const(qn.shape), const(kn.shape),
                  const(bdq.shape), const(bdk.shape), const(wgk.shape), const(bgk.shape)],
        out_specs=[rows(c) for c, _ in outs],
        out_shape=[jax.ShapeDtypeStruct((N, c), dt) for c, dt in outs],
        scratch_shapes=[pltpu.VMEM((D, IN_COLS_PAD), BF16)],
        compiler_params=_params("arbitrary"),
        name="in_proj",
    )(x2, mod, g1, w_in, qn, kn, bdq, bdk, wgk, bgk)


def _attn_kernel(sinks_ref, q_ref, kc_ref, vc_ref, kp_ref, vp_ref, gatt_ref, o_ref):
    blk = WINDOW
    first = pl.program_id(1) == 0
    qi = lax.broadcasted_iota(jnp.int32, (blk, blk), 0)
    cj = lax.broadcasted_iota(jnp.int32, (blk, blk), 1)
    from_prev = cj > qi
    dist = (qi - cj + jnp.where(from_prev, blk, 0)).astype(F32)
    no_prev = jnp.where(jnp.logical_and(from_prev, first), -1e30, 0.0)
    low = cj < HEAD_DIM
    half = (jnp.where(low, 1.0, 0.0).astype(BF16), jnp.where(low, 0.0, 1.0).astype(BF16))
    half2 = tuple(jnp.concatenate([m, m], axis=0) for m in half)
    prev_mask = jnp.where(from_prev, 1.0, 0.0).astype(BF16)
    cur_mask = jnp.where(from_prev, 0.0, 1.0).astype(BF16)

    n_pairs = ATT_HEADS // 2
    pairs_per_kv = n_pairs // ATT_KV_HEADS
    units = [(bi, j) for bi in range(ATT_BLOCKS_PER_STEP) for j in range(n_pairs)]

    def kv_blocks(bi, g):
        rows = slice(bi * blk, (bi + 1) * blk)
        prev_rows = slice((bi - 1) * blk, bi * blk)
        lanes = slice(g * LANES, (g + 1) * LANES)
        cur = (kc_ref[rows, lanes], vc_ref[rows, lanes])
        prev = (kp_ref[:, lanes], vp_ref[:, lanes]) if bi == 0 else (kc_ref[prev_rows, lanes],
                                                                      vc_ref[prev_rows, lanes])
        return prev, cur

    scores = {}
    for bi, j in units:
        (kp, _), (kc, _) = kv_blocks(bi, j // pairs_per_kv)
        k_both = jnp.concatenate([kp, kc], axis=0)
        qp = q_ref[bi * blk:(bi + 1) * blk, j * LANES:(j + 1) * LANES]
        for p in range(2):
            scores[bi, j, p] = _dot_nt(qp, k_both * half2[p])

    probs, sink_terms = {}, {}
    for bi, j in units:
        for p in range(2):
            h = 2 * j + p
            slope = 2.0 ** (-8.0 * (h + 1) / ATT_HEADS)
            s_both = scores[bi, j, p]
            s = jnp.where(from_prev, s_both[:, 0:blk], s_both[:, blk:2 * blk]) - slope * dist
            if bi == 0:
                s = s + no_prev
            sink = sinks_ref[h]
            m = jnp.maximum(jnp.max(s, axis=-1, keepdims=True), sink)
            probs[bi, j, p] = jnp.exp(s - m)
            sink_terms[bi, j, p] = jnp.exp(sink - m)

    outs = {}
    for bi, j in units:
        (_, vp), (_, vc) = kv_blocks(bi, j // pairs_per_kv)
        v_stack = jnp.concatenate([jnp.concatenate([v * half[p], half[p]], axis=1)
                                   for v in (vp, vc) for p in range(2)], axis=0)
        e = [probs[bi, j, p].astype(BF16) for p in range(2)]
        p_all = jnp.concatenate([x * m for m in (prev_mask, cur_mask) for x in e], axis=1)
        pv = _dot(p_all, v_stack)
        den = pv[:, LANES:2 * LANES] + jnp.where(low, sink_terms[bi, j, 0], sink_terms[bi, j, 1])
        outs[bi, j] = pv[:, 0:LANES] / den

    for bi in range(ATT_BLOCKS_PER_STEP):
        o = jnp.concatenate([outs[bi, j] for j in range(n_pairs)], axis=1)
        ms = jnp.mean(o * o, axis=-1, keepdims=True)
        o_ref[bi * blk:(bi + 1) * blk, :] = (o * lax.rsqrt(ms + EPS) * gatt_ref[...]).astype(BF16)


def _swa_attention(sinks, qa, ka, va, gatt, *, B, T):
    step_rows = ATT_BLOCKS_PER_STEP * WINDOW
    nb = T // step_rows
    cur = lambda c: pl.BlockSpec((step_rows, c), lambda b, i: (b * nb + i, 0))
    prev = lambda c: pl.BlockSpec(
        (WINDOW, c), lambda b, i: (b * (T // WINDOW) + jnp.maximum(i * ATT_BLOCKS_PER_STEP - 1, 0), 0))
    return pl.pallas_call(
        _attn_kernel,
        grid=(B, nb),
        in_specs=[pl.BlockSpec(memory_space=pltpu.SMEM),
                  cur(ATT_Q), cur(2 * ATT_KV), cur(2 * ATT_KV), prev(2 * ATT_KV), prev(2 * ATT_KV),
                  pl.BlockSpec((1, ATT_Q), lambda b, i: (0, 0))],
        out_specs=cur(ATT_Q),
        out_shape=jax.ShapeDtypeStruct((B * T, ATT_Q), BF16),
        compiler_params=_params("arbitrary", "arbitrary"),
        name="swa_attn",
    )(sinks, qa, ka, va, ka, va, gatt)


def _gla_kernel(q_ref, k_ref, v_ref, la_ref, og_ref, cum_ref, qmask_ref, bdms_ref, smask_ref,
                ggla_ref, o_ref, state_ref):
    L = GLA_CHUNK

    @pl.when(pl.program_id(1) == 0)
    def _():
        state_ref[...] = jnp.zeros_like(state_ref)

    rows = lax.broadcasted_iota(jnp.int32, (L, LANES), 0)
    causal = lax.broadcasted_iota(jnp.int32, (L, L), 0) >= lax.broadcasted_iota(jnp.int32, (L, L), 1)
    seqs = range(GLA_BATCH_PER_STEP)
    units = [(s, pair) for s in seqs for pair in range(GLA_HEADS // 2)]
    kl = lambda pair: slice(pair * LANES, (pair + 1) * LANES)
    vl = lambda pair: slice(pair * 2 * GLA_DV, (pair + 1) * 2 * GLA_DV)

    b, b_in = {}, {}
    for s in seqs:
        la = la_ref[s]
        la_hi = la.astype(BF16)
        la_lo = (la - la_hi.astype(F32)).astype(BF16)
        sums = _dot(cum_ref[...], jnp.concatenate([la_hi, la_lo], axis=0))
        b[s] = sums[0:L, :]
        b_in[s] = sums[L:2 * L, :]

    q_both, keys, q_dec, k_dec_t, b_last = {}, {}, {}, {}, {}
    for s in seqs:
        ref = b[s] - b_in[s]
        b_last[s] = b[s][L - 1:L, :]
        q = q_ref[s].astype(F32)
        k = k_ref[s].astype(F32)
        q_in = (q * jnp.exp(b_in[s])).astype(BF16)
        q_dec[s] = (q * jnp.exp(b[s])).astype(BF16)
        k_dec = k * jnp.exp(b_last[s] - b[s])
        for pair in range(GLA_HEADS // 2):
            k_p, b_p, ref_p = k[:, kl(pair)], b[s][:, kl(pair)], ref[:, kl(pair)]
            expanded = []
            for g in range(N_SUB):
                last_row = (g + 1) * GLA_SUB - 1
                expo = jnp.where(rows <= last_row, ref_p[g * GLA_SUB:g * GLA_SUB + 1, :] - b_p, -1e4)
                expanded.append((k_p * jnp.exp(expo)).astype(BF16))
            keys[s, pair] = jnp.concatenate(expanded, axis=1)
            q_rep = jnp.concatenate([q_in[:, kl(pair)]] * N_SUB, axis=1)
            q_both[s, pair] = jnp.concatenate([q_rep * qmask_ref[0], q_rep * qmask_ref[1]], axis=0)
            k_dec_t[s, pair] = k_dec[:, kl(pair)].T.astype(BF16)

    scores = {u: _dot_nt(q_both[u], keys[u]) for u in units}

    outs, updates = {}, {}
    for s, pair in units:
        v_p = v_ref[s, :, vl(pair)]
        o_parts = []
        for hh in range(2):
            a = jnp.where(causal, scores[s, pair][hh * L:(hh + 1) * L, :], 0.0).astype(BF16)
            o_parts.append(_dot(a, v_p[:, hh * GLA_DV:(hh + 1) * GLA_DV]))
        state = state_ref[s, pair]
        outs[s, pair] = jnp.concatenate(o_parts, axis=1) + _dot(q_dec[s][:, kl(pair)], state.astype(BF16))
        updates[s, pair] = _dot(k_dec_t[s, pair], v_p)

    for s, pair in units:
        decay = jnp.broadcast_to(jnp.exp(b_last[s][:, kl(pair)]), (LANES, LANES)).T
        state_ref[s, pair] = (state_ref[s, pair] * jnp.concatenate([decay, decay], axis=1)
                              + updates[s, pair] * smask_ref[...])
        o = outs[s, pair]
        ms = _dot((o * o).astype(BF16), bdms_ref[...])
        y = o * lax.rsqrt(ms + EPS) * ggla_ref[:, vl(pair)] * og_ref[s, :, vl(pair)].astype(F32)
        o_ref[s, :, vl(pair)] = y.astype(BF16)


def _gla(qg, kg, vg, la, og, cum, qmask, bdms, smask, ggla, *, B, T):
    L = GLA_CHUNK
    nb = GLA_BATCH_PER_STEP
    seq = lambda a: a.reshape(B, T, a.shape[-1])
    rows = lambda c: pl.BlockSpec((nb, L, c), lambda b, i: (b, i, 0))
    const = lambda a: pl.BlockSpec(a.shape, lambda b, i: (0,) * a.ndim)
    out = pl.pallas_call(
        _gla_kernel,
        grid=(B // nb, T // L),
        in_specs=[rows(GLA_K), rows(GLA_K), rows(GLA_V), rows(GLA_K), rows(GLA_V),
                  const(cum), const(qmask), const(bdms), const(smask), const(ggla)],
        out_specs=rows(GLA_V),
        out_shape=jax.ShapeDtypeStruct((B, T, GLA_V), BF16),
        scratch_shapes=[pltpu.VMEM((nb, GLA_HEADS // 2, LANES, 2 * GLA_DV), F32)],
        compiler_params=_params("arbitrary", "arbitrary"),
        name="gla",
    )(seq(qg), seq(kg), seq(vg), seq(la), seq(og), cum, qmask, bdms, smask, ggla)
    return out.reshape(B * T, GLA_V)


def _route(logits_t):
    lt = logits_t[0:ROUTER_ROWS, :]
    row = lax.broadcasted_iota(jnp.int32, lt.shape, 0)
    neg_inf = -jnp.inf
    g_log = jnp.where(row < N_GROUPS, lt, neg_inf)
    g_max = jnp.max(g_log, axis=0, keepdims=True)
    g_sel = jnp.min(jnp.where(g_log == g_max, row, LANES), axis=0, keepdims=True)
    g_sum = jnp.sum(jnp.where(row < N_GROUPS, jnp.exp(lt - g_max), 0.0), axis=0, keepdims=True)
    p_group = 1.0 / g_sum
    e_lo = ROUTER_LANE0 + EXPERTS_PER_GROUP * g_sel
    in_group = jnp.logical_and(row >= e_lo, row < e_lo + EXPERTS_PER_GROUP)
    e_log = jnp.where(in_group, lt, neg_inf)
    e_max = jnp.max(e_log, axis=0, keepdims=True)
    top1 = jnp.min(jnp.where(e_log == e_max, row, LANES), axis=0, keepdims=True)
    e_log2 = jnp.where(row == top1, neg_inf, e_log)
    e_max2 = jnp.max(e_log2, axis=0, keepdims=True)
    top2 = jnp.min(jnp.where(e_log2 == e_max2, row, LANES), axis=0, keepdims=True)
    ratio = jnp.exp(e_max2 - e_max)
    w_top1 = p_group / (1.0 + ratio)
    w_top2 = p_group * ratio / (1.0 + ratio)
    row8 = lax.broadcasted_iota(jnp.int32, (8, lt.shape[1]), 0)
    weights = jnp.where(row8 == top1 - e_lo, w_top1, 0.0) + jnp.where(row8 == top2 - e_lo, w_top2, 0.0)
    return g_sel, weights


def _outproj_kernel(ya_ref, yg_ref, x_ref, mod_ref, wo_ref, g2_ref, wrt_ref, brt_ref, striu_ref,
                    x1_ref, row_ref, lpos_ref, cnt_ref):
    tm = MOE_TILE
    subs = range(x_ref.shape[0] // tm)
    rows = lambda t: slice(t * tm, (t + 1) * tm)

    mix = [_dot(ya_ref[rows(t), :], wo_ref[0:ATT_Q, :]) + _dot(yg_ref[rows(t), :], wo_ref[ATT_Q:ATT_Q + GLA_V, :])
           for t in subs]
    h2b = []
    for t in subs:
        x1 = x_ref[rows(t), :] + mod_ref[2:3, :] * mix[t]
        x1_ref[rows(t), :] = x1
        ms = jnp.mean(x1 * x1, axis=-1, keepdims=True)
        h2 = (x1 * lax.rsqrt(ms + EPS) * g2_ref[...]) * (1.0 + mod_ref[4:5, :]) + mod_ref[3:4, :]
        h2b.append(h2.astype(BF16))
    logits_t = [_dot_nt(wrt_ref[...], h2b[t]) + brt_ref[...] for t in subs]

    routed = [_route(logits_t[t]) for t in subs]
    row8 = lax.broadcasted_iota(jnp.int32, (8, tm), 0)
    onehot = [jnp.where(row8 == routed[t][0], 1.0, 0.0) for t in subs]
    before = [_dot(onehot[t].astype(BF16), striu_ref[...]) for t in subs]

    local_row = lax.broadcasted_iota(jnp.int32, (LOCAL_ROWS, tm), 0).astype(F32)
    pad_rows = jnp.zeros((LANES - 8, tm), F32)
    for t in subs:
        count = jnp.sum(onehot[t], axis=1, keepdims=True)
        cnt_ref[t] = jnp.broadcast_to(count, (8, LANES))
        padded = jnp.broadcast_to(jnp.floor((count + (CHUNK - 1.0)) * (1.0 / CHUNK)) * CHUNK, (8, tm))
        start = jnp.zeros((8, tm), F32)
        for shift in range(1, N_GROUPS):
            start = start + jnp.where(row8 >= shift, pltpu.roll(padded, shift, axis=0), 0.0)
        lpos = jnp.sum(onehot[t] * (before[t] + start), axis=0, keepdims=True)
        lpos_ref[rows(t), :] = jnp.broadcast_to(lpos, (LANES, tm)).T
        weights = jnp.concatenate([routed[t][1], pad_rows], axis=0).T
        w_hi = weights.astype(BF16)
        w_lo = (weights - w_hi.astype(F32)).astype(BF16)
        perm = jnp.where(local_row == lpos, 1.0, 0.0).astype(BF16)
        row_ref[t * LOCAL_ROWS:(t + 1) * LOCAL_ROWS, :] = _dot(
            perm, jnp.concatenate([h2b[t], w_hi, w_lo], axis=1)).astype(BF16)


def _out_proj(ya, yg, x2, mod, wo, g2, wr, br, stril, *, T):
    N, D = x2.shape
    subs = OUT_SUBS
    tm = subs * MOE_TILE
    per_b = T // tm
    rows = lambda c: pl.BlockSpec((tm, c), lambda i: (i, 0))
    const = lambda a: pl.BlockSpec(a.shape, lambda i: (0,) * a.ndim)
    return pl.pallas_call(
        _outproj_kernel,
        grid=(N // tm,),
        in_specs=[rows(ATT_Q), rows(GLA_V), rows(D),
                  pl.BlockSpec((None, 6, D), lambda i: (i // per_b, 0, 0)),
                  const(wo), const(g2), const(wr), const(br), const(stril)],
        out_specs=[rows(D), pl.BlockSpec((subs * LOCAL_ROWS, D + 2 * LANES), lambda i: (i, 0)), rows(LANES),
                   pl.BlockSpec((subs, 8, LANES), lambda i: (i, 0, 0))],
        out_shape=[jax.ShapeDtypeStruct((N, D), F32),
                   jax.ShapeDtypeStruct((N // MOE_TILE * LOCAL_ROWS, D + 2 * LANES), BF16),
                   jax.ShapeDtypeStruct((N, LANES), F32),
                   jax.ShapeDtypeStruct((N // MOE_TILE, 8, LANES), F32)],
        compiler_params=_params("arbitrary"),
        name="out_proj",
    )(ya, yg, x2, mod, wo, g2, wr, br, stril)


def _chunk_copy(src_ref, src_chunk, dst_ref, dst_chunk, sem):
    return pltpu.make_async_copy(src_ref.at[src_chunk], dst_ref.at[dst_chunk], sem)


def _moe_kernel(src_ref, nvalid_ref, grp_ref, nt_ref, used_ref, rows_ref, w1f_ref, w3f_ref, w2f_ref, y_ref,
                in_buf, out_buf, zero_buf, w1_ref, w3_ref, w2_ref, in_sem, out_sem, zero_sem):
    j = pl.program_id(0)
    n_tiles = nt_ref[0]
    d_model = w2_ref.shape[2]

    @pl.when(jnp.logical_or(j == 0, grp_ref[j] != grp_ref[jnp.maximum(j - 1, 0)]))
    def _():
        for src, dst in ((w1f_ref, w1_ref), (w3f_ref, w3_ref), (w2f_ref, w2_ref)):
            for k in range(EXPERTS_PER_GROUP):
                dst[k] = src[k].astype(BF16)

    def gather(tile, slot):
        def body(k, carry):
            _chunk_copy(rows_ref, src_ref[tile * TILE_CHUNKS + k], in_buf.at[slot], k, in_sem.at[slot]).start()
            return carry
        lax.fori_loop(0, TILE_CHUNKS, body, 0, unroll=True)

    def wait_gather(slot):
        def body(k, carry):
            _chunk_copy(rows_ref, 0, in_buf.at[slot], k, in_sem.at[slot]).wait()
            return carry
        lax.fori_loop(0, TILE_CHUNKS, body, 0, unroll=True)

    def scatter(tile, slot):
        def body(k, carry):
            _chunk_copy(out_buf.at[slot], k, y_ref, src_ref[tile * TILE_CHUNKS + k], out_sem.at[slot]).start()
            return carry
        lax.fori_loop(0, nvalid_ref[tile], body, 0)

    def wait_scatter(tile, slot):
        def body(k, carry):
            _chunk_copy(out_buf.at[slot], k, y_ref, 0, out_sem.at[slot]).wait()
            return carry
        lax.fori_loop(0, nvalid_ref[tile], body, 0)

    def zero_fill(wait):
        def per_tile(i, carry):
            def body(c, inner):
                copy = _chunk_copy(zero_buf, 0, y_ref, i * LOCAL_CHUNKS + c, zero_sem)
                if wait:
                    copy.wait()
                else:
                    copy.start()
                return inner
            return lax.fori_loop(used_ref[i], LOCAL_CHUNKS, body, carry)
        lax.fori_loop(0, used_ref.shape[0], per_tile, 0)

    @pl.when(j == 0)
    def _():
        zero_buf[...] = jnp.zeros_like(zero_buf)
        zero_fill(wait=False)
        gather(0, 0)

    @pl.when(j + 1 < n_tiles)
    def _():
        gather(j + 1, (j + 1) % 2)

    @pl.when(j < n_tiles)
    def _():
        slot = j % 2
        wait_gather(slot)
        rows = in_buf[slot].reshape(MOE_TILE, in_buf.shape[-1])
        h = rows[:, 0:d_model]
        weights = rows[:, d_model:d_model + LANES].astype(F32) + rows[:, d_model + LANES:].astype(F32)
        experts = range(EXPERTS_PER_GROUP)
        up = [(_dot(h, w1_ref[k]), _dot(h, w3_ref[k])) for k in experts]
        hid = [(a * _sigmoid(a) * g * weights[:, k:k + 1]).astype(BF16) for k, (a, g) in zip(experts, up)]
        y = _dot(hid[0], w2_ref[0])
        for k in experts[1:]:
            y = y + _dot(hid[k], w2_ref[k])

        @pl.when(j >= 2)
        def _():
            wait_scatter(j - 2, slot)

        out_buf[slot] = y.astype(BF16).reshape(TILE_CHUNKS, CHUNK, d_model)
        scatter(j, slot)

        @pl.when(j == n_tiles - 1)
        def _():
            @pl.when(j >= 1)
            def _():
                wait_scatter(j - 1, 1 - slot)
            wait_scatter(j, slot)
            zero_fill(wait=True)


def _combine_kernel(x1_ref, mod_ref, lpos_ref, y_ref, o_ref):
    tm = MOE_TILE
    local_row = lax.broadcasted_iota(jnp.int32, (tm, LOCAL_ROWS), 1).astype(F32)
    for t in range(x1_ref.shape[0] // tm):
        rows = slice(t * tm, (t + 1) * tm)
        unsort = jnp.where(local_row == lpos_ref[rows, 0:1], 1.0, 0.0).astype(BF16)
        y = _dot(unsort, y_ref[t * LOCAL_ROWS:(t + 1) * LOCAL_ROWS, :])
        o_ref[rows, :] = x1_ref[rows, :] + mod_ref[5:6, :] * y


def _moe_plan(cnt):
    n_local = cnt.shape[0]
    chunks = (cnt + CHUNK - 1) // CHUNK
    used = jnp.sum(chunks, axis=1)
    local_off = jnp.cumsum(chunks, axis=1) - chunks
    tiles_g = (jnp.sum(chunks, axis=0) + TILE_CHUNKS - 1) // TILE_CHUNKS
    tile_end = jnp.cumsum(tiles_g)
    n_tiles = tile_end[-1]
    group_start = (tile_end - tiles_g) * TILE_CHUNKS
    seg_len = chunks.T.reshape(-1)
    seg_start = (group_start[:, None] + (jnp.cumsum(chunks, axis=0) - chunks).T).reshape(-1)
    seg_src = (jnp.arange(n_local)[None, :] * LOCAL_CHUNKS + local_off.T).reshape(-1)
    max_chunks = n_local * MOE_TILE // CHUNK + n_local * N_GROUPS + N_GROUPS * TILE_CHUNKS
    max_tiles = (max_chunks + TILE_CHUNKS - 1) // TILE_CHUNKS
    c = jnp.arange(max_tiles * TILE_CHUNKS)[:, None]
    within = c - seg_start[None, :]
    hit = jnp.logical_and(within >= 0, within < seg_len[None, :])
    valid = jnp.any(hit, axis=1)
    src = jnp.sum(jnp.where(hit, seg_src[None, :] + within, 0), axis=1)
    src = jnp.where(valid, src, LOCAL_CHUNKS - 1)
    nvalid = jnp.sum(valid.reshape(max_tiles, TILE_CHUNKS), axis=1)
    j = jnp.minimum(jnp.arange(max_tiles), n_tiles - 1)
    grp = jnp.sum(j[:, None] >= tile_end[None, :], axis=1)
    i32 = lambda a: a.astype(jnp.int32)
    return i32(src), i32(nvalid), i32(grp), i32(n_tiles).reshape(1), i32(used)


def _moe(plan, rows_local, w1g, w3g, w2g):
    src, nvalid, grp, n_tiles, used = plan
    D = w2g.shape[3]
    n_rows, cols = rows_local.shape
    weights = lambda a: pl.BlockSpec((None,) + a.shape[1:], lambda j, src, nv, grp, nt, used: (grp[j], 0, 0, 0))
    y = pl.pallas_call(
        _moe_kernel,
        grid_spec=pltpu.PrefetchScalarGridSpec(
            num_scalar_prefetch=5,
            grid=(grp.shape[0],),
            in_specs=[pl.BlockSpec(memory_space=pl.ANY), weights(w1g), weights(w3g), weights(w2g)],
            out_specs=pl.BlockSpec(memory_space=pl.ANY),
            scratch_shapes=[pltpu.VMEM((2, TILE_CHUNKS, CHUNK, cols), BF16),
                            pltpu.VMEM((2, TILE_CHUNKS, CHUNK, D), BF16),
                            pltpu.VMEM((1, CHUNK, D), BF16),
                            pltpu.VMEM(w1g.shape[1:], BF16), pltpu.VMEM(w3g.shape[1:], BF16),
                            pltpu.VMEM(w2g.shape[1:], BF16),
                            pltpu.SemaphoreType.DMA((2,)), pltpu.SemaphoreType.DMA((2,)),
                            pltpu.SemaphoreType.DMA(())]),
        out_shape=jax.ShapeDtypeStruct((n_rows // CHUNK, CHUNK, D), BF16),
        compiler_params=_params("arbitrary"),
        name="moe",
    )(src, nvalid, grp, n_tiles, used, rows_local.reshape(n_rows // CHUNK, CHUNK, cols), w1g, w3g, w2g)
    return y.reshape(n_rows, D)


def _combine(x1, mod, lpos, y_local, *, T):
    N, D = x1.shape
    subs = COMBINE_SUBS
    tm = subs * MOE_TILE
    per_b = T // tm
    return pl.pallas_call(
        _combine_kernel,
        grid=(N // tm,),
        in_specs=[pl.BlockSpec((tm, D), lambda i: (i, 0)),
                  pl.BlockSpec((None, 6, D), lambda i: (i // per_b, 0, 0)),
                  pl.BlockSpec((tm, LANES), lambda i: (i, 0)),
                  pl.BlockSpec((subs * LOCAL_ROWS, D), lambda i: (i, 0))],
        out_specs=pl.BlockSpec((tm, D), lambda i: (i, 0)),
        out_shape=jax.ShapeDtypeStruct((N, D), F32),
        compiler_params=_params("arbitrary"),
        name="moe_combine",
    )(x1, mod, lpos, y_local)


def _block_diag(n, blk, value, dtype):
    r = np.arange(n)[:, None] // blk
    c = np.arange(n)[None, :] // blk
    return jnp.asarray(np.where(r == c, value, 0.0), dtype)


def _gla_constants():
    L = GLA_CHUNK
    i = np.arange(L)[:, None]
    j = np.arange(L)[None, :]
    tri = j <= i
    bdtri = np.logical_and(j <= i, i // GLA_SUB == j // GLA_SUB)
    cum = np.block([[tri, tri], [bdtri, bdtri]])
    col = np.arange(N_SUB * LANES)[None, :]
    qmask = np.stack([np.logical_and(col // LANES == i // GLA_SUB, (col % LANES) // GLA_DK == hh)
                      for hh in range(2)])
    bdms = _block_diag(2 * GLA_DV, GLA_DV, 1.0 / GLA_DV, BF16)
    d = np.arange(LANES)[:, None] // GLA_DK
    e = np.arange(2 * GLA_DV)[None, :] // GLA_DV
    smask = d == e
    return (jnp.asarray(cum, BF16), jnp.asarray(qmask, BF16), bdms, jnp.asarray(smask, F32))


def kernel(x, c, w_ada, b_ada, g_norm1, w_in, q_norm, k_norm, sinks, w_gk2, b_gk, g_gla_out, g_att_out,
           w_out, g_norm2, w_group, b_group, w_router, b_router, w1, w3, w2):
    B, T, D = x.shape
    N = B * T
    depth = w_ada.shape[0]
    cum, qmask, bdms, smask = _gla_constants()
    bdq = _block_diag(2 * LANES, HEAD_DIM, 1.0 / HEAD_DIM, BF16)
    bdk = _block_diag(ATT_KV, HEAD_DIM, 1.0 / HEAD_DIM, BF16)

    x2 = x.reshape(N, D)
    for l in range(depth):
        mod = _adaln_mod(c, w_ada[l], b_ada[l]).reshape(B, 6, D)

        wgk =jnp.concatenate([w_gk2[l], jnp.zeros((LANES - GLA_RANK, GLA_K), F32)], axis=0).astype(BF16)
        qa, ka, va, qg, kg, vg, la, og = _in_proj(
            x2, mod, g_norm1[l].reshape(1, D), w_in[l],
            jnp.tile(q_norm[l], ATT_HEADS).reshape(1, ATT_Q), jnp.tile(k_norm[l], ATT_KV_HEADS).reshape(1, ATT_KV),
            bdq, bdk, wgk, b_gk[l].reshape(1, GLA_K), T=T, tm=PROJ_TILE)

        y_att = _swa_attention(sinks[l], qa, ka, va, g_att_out[l].reshape(1, ATT_Q), B=B, T=T)
        y_gla = _gla(qg, kg, vg, la, og, cum, qmask, bdms, smask,
                     jnp.tile(g_gla_out[l], GLA_HEADS).reshape(1, GLA_V), B=B, T=T)

        pad = LANES - N_GROUPS - N_EXPERTS
        wr_t = jnp.concatenate([w_group[l], w_router[l], jnp.zeros((D, pad), F32)], axis=1).T.astype(BF16)
        br_t = jnp.concatenate([b_group[l], b_router[l], jnp.zeros((pad,), F32)]).reshape(LANES, 1)
        striu = jnp.asarray(np.arange(MOE_TILE)[:, None] < np.arange(MOE_TILE)[None, :], BF16)
        x1, rows_local, lpos, cnt = _out_proj(y_att, y_gla, x2, mod, w_out[l].astype(BF16),
                                              g_norm2[l].reshape(1, D), wr_t, br_t, striu, T=T)
        plan = _moe_plan(cnt[:, :N_GROUPS, 0].astype(jnp.int32))
        by_group = lambda w: w.reshape((N_GROUPS, EXPERTS_PER_GROUP) + w.shape[1:])
        y_local = _moe(plan, rows_local, by_group(w1[l]), by_group(w3[l]), by_group(w2[l]))
        x2 = _combine(x1, mod, lpos, y_local, T=T)
    return x2.reshape(B, T, D)
```

```python
import functools

import jax
import jax.numpy as jnp
import numpy as np
from jax import lax
from jax.experimental import pallas as pl
from jax.experimental.pallas import tpu as pltpu

F32 = jnp.float32
BF16 = jnp.bfloat16

EPS = 1e-6
ATT_HEADS = 8
ATT_KV_HEADS = 2
HEAD_DIM = 64
WINDOW = 128
ATT_Q = ATT_HEADS * HEAD_DIM
ATT_KV = ATT_KV_HEADS * HEAD_DIM
GLA_HEADS = 4
GLA_DK = 64
GLA_DV = 128
GLA_RANK = 16
GLA_NORMALIZER = 16.0
GLA_K = GLA_HEADS * GLA_DK
GLA_V = GLA_HEADS * GLA_DV
N_GROUPS = 4
EXPERTS_PER_GROUP = 4
N_EXPERTS = N_GROUPS * EXPERTS_PER_GROUP
D_EXPERT = 256

LANES = 128
PROJ_SUB = 512
PROJ_TILE = 2 * PROJ_SUB
OUT_SUBS = 4
COMBINE_SUBS = 4
ROUTER_ROWS = 24
ATT_BLOCKS_PER_STEP = 4
GLA_BATCH_PER_STEP = 4
GLA_CHUNK = 128
GLA_SUB = 16
N_SUB = GLA_CHUNK // GLA_SUB
ROUTER_LANE0 = N_GROUPS
VMEM_LIMIT = 56 * 1024 * 1024
MOE_TILE = 256
CHUNK = 16
TILE_CHUNKS = MOE_TILE // CHUNK
LOCAL_CHUNKS = (MOE_TILE + N_GROUPS * (CHUNK - 1)) // CHUNK + 2
LOCAL_ROWS = LOCAL_CHUNKS * CHUNK

_QA0, _KA0, _VA0 = 0, ATT_Q, ATT_Q + ATT_KV
_QG0 = _VA0 + ATT_KV
_KG0 = _QG0 + GLA_K
_VG0 = _KG0 + GLA_K
_OG0 = _VG0 + GLA_V
_LR0 = _OG0 + GLA_V
IN_COLS_PAD = _LR0 + LANES


def _dot(a, b):
    return jnp.dot(a, b, preferred_element_type=F32)


def _dot_nt(a, b):
    return lax.dot_general(a, b, (((1,), (1,)), ((), ())), preferred_element_type=F32)


def _sigmoid(x):
    return 1.0 / (1.0 + jnp.exp(-x))


def _params(*sem):
    return pltpu.CompilerParams(dimension_semantics=sem, vmem_limit_bytes=VMEM_LIMIT)


def _mod_kernel(c_ref, w_ref, b_ref, o_ref):
    c = c_ref[...]
    s = (c * _sigmoid(c)).astype(BF16)
    o_ref[...] = _dot(s, w_ref[...].astype(BF16)) + b_ref[...]


def _adaln_mod(c, w_ada, b_ada):
    B, D = c.shape
    n = w_ada.shape[1]
    tn = 1536
    return pl.pallas_call(
        _mod_kernel,
        grid=(n // tn,),
        in_specs=[pl.BlockSpec((B, D), lambda j: (0, 0)),
                  pl.BlockSpec((D, tn), lambda j: (0, j)),
                  pl.BlockSpec((1, tn), lambda j: (0, j))],
        out_specs=pl.BlockSpec((B, tn), lambda j: (0, j)),
        out_shape=jax.ShapeDtypeStruct((B, n), F32),
        compiler_params=_params("arbitrary"),
        name="adaln_mod",
    )(c, w_ada, b_ada.reshape(1, n))


def _inproj_kernel(x_ref, mod_ref, g1_ref, wf_ref, qn_ref, kn_ref, bdq_ref, bdk_ref, wgk_ref, bgk_ref,
                   qa_ref, ka_ref, va_ref, qg_ref, kg_ref, vg_ref, la_ref, og_ref, w_ref):
    subs = range(x_ref.shape[0] // PROJ_SUB)
    rows = lambda t: slice(t * PROJ_SUB, (t + 1) * PROJ_SUB)

    @pl.when(pl.program_id(0) == 0)
    def _():
        lr_src = _OG0
        w_ref[0:_OG0, :] = wf_ref[0:_OG0, :].astype(BF16)
        w_ref[_OG0:_LR0, :] = wf_ref[lr_src + GLA_RANK:lr_src + GLA_RANK + GLA_V, :].astype(BF16)
        w_ref[_LR0:_LR0 + GLA_RANK, :] = wf_ref[lr_src:lr_src + GLA_RANK, :].astype(BF16)
        w_ref[_LR0 + GLA_RANK:IN_COLS_PAD, :] = jnp.zeros((LANES - GLA_RANK, wf_ref.shape[1]), BF16)

    h = []
    for t in subs:
        x = x_ref[rows(t), :]
        ms = jnp.mean(x * x, axis=-1, keepdims=True)
        xn = x * lax.rsqrt(ms + EPS) * g1_ref[...]
        h.append((xn * (1.0 + mod_ref[1:2, :]) + mod_ref[0:1, :]).astype(BF16))

    proj = lambda t, c0, width: _dot_nt(h[t], w_ref[c0:c0 + width, :])
    gate_of = lambda lr: _dot(lr.astype(BF16), wgk_ref[...]) + bgk_ref[...]
    qa, kv, q_ms, k_ms, qk_g, vg, og, lr, gate = ({} for _ in range(9))
    for t in subs:
        qa[t] = proj(t, _QA0, ATT_Q)
        if t > 0:
            gate[t - 1] = gate_of(lr[t - 1])
        kv[t] = proj(t, _KA0, 2 * ATT_KV)
        q_sq = (qa[t] * qa[t]).astype(BF16)
        q_ms[t] = jnp.concatenate([_dot(q_sq[:, c:c + 2 * LANES], bdq_ref[...])
                                   for c in range(0, ATT_Q, 2 * LANES)], axis=1)
        qk_g[t] = proj(t, _QG0, 2 * GLA_K)
        k = kv[t][:, 0:ATT_KV]
        k_ms[t] = _dot((k * k).astype(BF16), bdk_ref[...])
        vg[t] = proj(t, _VG0, GLA_V)
        og[t] = proj(t, _OG0, GLA_V)
        lr[t] = proj(t, _LR0, LANES)
    gate[subs[-1]] = gate_of(lr[subs[-1]])

    low = lax.broadcasted_iota(jnp.int32, (PROJ_SUB, ATT_KV), 1) < HEAD_DIM
    for t in subs:
        qa_ref[rows(t), :] = (qa[t] * lax.rsqrt(q_ms[t] + EPS) * qn_ref[...] * (HEAD_DIM ** -0.5)).astype(BF16)
        k = kv[t][:, 0:ATT_KV] * lax.rsqrt(k_ms[t] + EPS) * kn_ref[...]
        v = kv[t][:, ATT_KV:2 * ATT_KV]
        for src, dst in ((k, ka_ref), (v, va_ref)):
            swapped = pltpu.roll(src, HEAD_DIM, axis=1)
            dst[rows(t), 0:LANES] = jnp.where(low, src, swapped).astype(BF16)
            dst[rows(t), LANES:2 * LANES] = jnp.where(low, swapped, src).astype(BF16)
        qg_ref[rows(t), :] = (qk_g[t][:, 0:GLA_K] * (GLA_DK ** -0.5)).astype(BF16)
        kg_ref[rows(t), :] = qk_g[t][:, GLA_K:2 * GLA_K].astype(BF16)
        vg_ref[rows(t), :] = vg[t].astype(BF16)
        og_ref[rows(t), :] = (og[t] * _sigmoid(og[t])).astype(BF16)
        log_sig = jnp.minimum(gate[t], 0.0) - jnp.log(1.0 + jnp.exp(-jnp.abs(gate[t])))
        la_ref[rows(t), :] = log_sig * (1.0 / GLA_NORMALIZER)


def _in_proj(x2, mod, g1, w_in_t, qn, kn, bdq, bdk, wgk, bgk, *, T, tm):
    N, D = x2.shape
    per_b = T // tm
    const = lambda shape: pl.BlockSpec(shape, lambda i: (0,) * len(shape))
    rows = lambda c: pl.BlockSpec((tm, c), lambda i: (i, 0))
    outs = [(ATT_Q, BF16), (2 * ATT_KV, BF16), (2 * ATT_KV, BF16), (GLA_K, BF16), (GLA_K, BF16),
            (GLA_V, BF16), (GLA_K, F32), (GLA_V, BF16)]
    return pl.pallas_call(
        _inproj_kernel,
        grid=(N // tm,),
        in_specs=[rows(D),
                  pl.BlockSpec((None, 6, D), lambda i: (i // per_b, 0, 0)),
                  const((1, D)),
                  pl.BlockSpec(w_in_t.shape, lambda i: (0, 0), pipeline_mode=pl.Buffered(1)),
                  const(qn.shape), const(kn.shape),
                  const(bdq.shape), const(bdk.shape), const(wgk.shape), const(bgk.shape)],
        out_specs=[rows(c) for c, _ in outs],
        out_shape=[jax.ShapeDtypeStruct((N, c), dt) for c, dt in outs],
        scratch_shapes=[pltpu.VMEM((IN_COLS_PAD, D), BF16)],
        compiler_params=_params("arbitrary"),
        name="in_proj",
    )(x2, mod, g1, w_in_t, qn, kn, bdq, bdk, wgk, bgk)


def _attn_kernel(sinks_ref, q_ref, kc_ref, vc_ref, kp_ref, vp_ref, gatt_ref, o_ref):
    blk = WINDOW
    first = pl.program_id(1) == 0
    qi = lax.broadcasted_iota(jnp.int32, (blk, blk), 0)
    cj = lax.broadcasted_iota(jnp.int32, (blk, blk), 1)
    from_prev = cj > qi
    dist = (qi - cj + jnp.where(from_prev, blk, 0)).astype(F32)
    no_prev = jnp.where(jnp.logical_and(from_prev, first), -1e30, 0.0)
    low = cj < HEAD_DIM
    half = (jnp.where(low, 1.0, 0.0).astype(BF16), jnp.where(low, 0.0, 1.0).astype(BF16))
    half2 = tuple(jnp.concatenate([m, m], axis=0) for m in half)
    prev_mask = jnp.where(from_prev, 1.0, 0.0).astype(BF16)
    cur_mask = jnp.where(from_prev, 0.0, 1.0).astype(BF16)

    n_pairs = ATT_HEADS // 2
    pairs_per_kv = n_pairs // ATT_KV_HEADS
    units = [(bi, j) for bi in range(ATT_BLOCKS_PER_STEP) for j in range(n_pairs)]

    def kv_blocks(bi, g):
        rows = slice(bi * blk, (bi + 1) * blk)
        prev_rows = slice((bi - 1) * blk, bi * blk)
        lanes = slice(g * LANES, (g + 1) * LANES)
        cur = (kc_ref[rows, lanes], vc_ref[rows, lanes])
        prev = (kp_ref[:, lanes], vp_ref[:, lanes]) if bi == 0 else (kc_ref[prev_rows, lanes],
                                                                      vc_ref[prev_rows, lanes])
        return prev, cur

    scores = {}
    for bi, j in units:
        (kp, _), (kc, _) = kv_blocks(bi, j // pairs_per_kv)
        k_both = jnp.concatenate([kp, kc], axis=0)
        qp = q_ref[bi * blk:(bi + 1) * blk, j * LANES:(j + 1) * LANES]
        for p in range(2):
            scores[bi, j, p] = _dot_nt(qp, k_both * half2[p])

    probs, sink_terms = {}, {}
    for bi, j in units:
        for p in range(2):
            h = 2 * j + p
            slope = 2.0 ** (-8.0 * (h + 1) / ATT_HEADS)
            s_both = scores[bi, j, p]
            s = jnp.where(from_prev, s_both[:, 0:blk], s_both[:, blk:2 * blk]) - slope * dist
            if bi == 0:
                s = s + no_prev
            sink = sinks_ref[h]
            m = jnp.maximum(jnp.max(s, axis=-1, keepdims=True), sink)
            probs[bi, j, p] = jnp.exp(s - m)
            sink_terms[bi, j, p] = jnp.exp(sink - m)

    outs = {}
    for bi, j in units:
        (_, vp), (_, vc) = kv_blocks(bi, j // pairs_per_kv)
        v_stack = jnp.concatenate([jnp.concatenate([v * half[p], half[p]], axis=1)
                                   for v in (vp, vc) for p in range(2)], axis=0)
        e = [probs[bi, j, p].astype(BF16) for p in range(2)]
        p_all = jnp.concatenate([x * m for m in (prev_mask, cur_mask) for x in e], axis=1)
        pv = _dot(p_all, v_stack)
        den = pv[:, LANES:2 * LANES] + jnp.where(low, sink_terms[bi, j, 0], sink_terms[bi, j, 1])
        outs[bi, j] = pv[:, 0:LANES] / den

    for bi in range(ATT_BLOCKS_PER_STEP):
        o = jnp.concatenate([outs[bi, j] for j in range(n_pairs)], axis=1)
        ms = jnp.mean(o * o, axis=-1, keepdims=True)
        o_ref[bi * blk:(bi + 1) * blk, :] = (o * lax.rsqrt(ms + EPS) * gatt_ref[...]).astype(BF16)


def _swa_attention(sinks, qa, ka, va, gatt, *, B, T):
    step_rows = ATT_BLOCKS_PER_STEP * WINDOW
    nb = T // step_rows
    cur = lambda c: pl.BlockSpec((step_rows, c), lambda b, i: (b * nb + i, 0))
    prev = lambda c: pl.BlockSpec(
        (WINDOW, c), lambda b, i: (b * (T // WINDOW) + jnp.maximum(i * ATT_BLOCKS_PER_STEP - 1, 0), 0))
    return pl.pallas_call(
        _attn_kernel,
        grid=(B, nb),
        in_specs=[pl.BlockSpec(memory_space=pltpu.SMEM),
                  cur(ATT_Q), cur(2 * ATT_KV), cur(2 * ATT_KV), prev(2 * ATT_KV), prev(2 * ATT_KV),
                  pl.BlockSpec((1, ATT_Q), lambda b, i: (0, 0))],
        out_specs=cur(ATT_Q),
        out_shape=jax.ShapeDtypeStruct((B * T, ATT_Q), BF16),
        compiler_params=_params("arbitrary", "arbitrary"),
        name="swa_attn",
    )(sinks, qa, ka, va, ka, va, gatt)


def _gla_kernel(q_ref, k_ref, v_ref, la_ref, og_ref, cum_ref, qmask_ref, bdms_ref, smask_ref,
                ggla_ref, o_ref, state_ref):
    L = GLA_CHUNK

    @pl.when(pl.program_id(1) == 0)
    def _():
        state_ref[...] = jnp.zeros_like(state_ref)

    rows = lax.broadcasted_iota(jnp.int32, (L, LANES), 0)
    causal = lax.broadcasted_iota(jnp.int32, (L, L), 0) >= lax.broadcasted_iota(jnp.int32, (L, L), 1)
    seqs = range(GLA_BATCH_PER_STEP)
    units = [(s, pair) for s in seqs for pair in range(GLA_HEADS // 2)]
    kl = lambda pair: slice(pair * LANES, (pair + 1) * LANES)
    vl = lambda pair: slice(pair * 2 * GLA_DV, (pair + 1) * 2 * GLA_DV)

    b, b_in = {}, {}
    for s in seqs:
        la = la_ref[s]
        la_hi = la.astype(BF16)
        la_lo = (la - la_hi.astype(F32)).astype(BF16)
        sums = _dot(cum_ref[...], jnp.concatenate([la_hi, la_lo], axis=0))
        b[s] = sums[0:L, :]
        b_in[s] = sums[L:2 * L, :]

    q_both, keys, q_dec, k_dec_t, b_last = {}, {}, {}, {}, {}
    for s in seqs:
        ref = b[s] - b_in[s]
        b_last[s] = b[s][L - 1:L, :]
        q = q_ref[s].astype(F32)
        k = k_ref[s].astype(F32)
        q_in = (q * jnp.exp(b_in[s])).astype(BF16)
        q_dec[s] = (q * jnp.exp(b[s])).astype(BF16)
        k_dec = k * jnp.exp(b_last[s] - b[s])
        for pair in range(GLA_HEADS // 2):
            k_p, b_p, ref_p = k[:, kl(pair)], b[s][:, kl(pair)], ref[:, kl(pair)]
            expanded = []
            for g in range(N_SUB):
                last_row = (g + 1) * GLA_SUB - 1
                expo = jnp.where(rows <= last_row, ref_p[g * GLA_SUB:g * GLA_SUB + 1, :] - b_p, -1e4)
                expanded.append((k_p * jnp.exp(expo)).astype(BF16))
            keys[s, pair] = jnp.concatenate(expanded, axis=1)
            q_rep = jnp.concatenate([q_in[:, kl(pair)]] * N_SUB, axis=1)
            q_both[s, pair] = jnp.concatenate([q_rep * qmask_ref[0], q_rep * qmask_ref[1]], axis=0)
            k_dec_t[s, pair] = k_dec[:, kl(pair)].T.astype(BF16)

    scores = {u: _dot_nt(q_both[u], keys[u]) for u in units}

    outs, updates = {}, {}
    for s, pair in units:
        v_p = v_ref[s, :, vl(pair)]
        o_parts = []
        for hh in range(2):
            a = jnp.where(causal, scores[s, pair][hh * L:(hh + 1) * L, :], 0.0).astype(BF16)
            o_parts.append(_dot(a, v_p[:, hh * GLA_DV:(hh + 1) * GLA_DV]))
        state = state_ref[s, pair]
        outs[s, pair] = jnp.concatenate(o_parts, axis=1) + _dot(q_dec[s][:, kl(pair)], state.astype(BF16))
        updates[s, pair] = _dot(k_dec_t[s, pair], v_p)

    for s, pair in units:
        decay = jnp.broadcast_to(jnp.exp(b_last[s][:, kl(pair)]), (LANES, LANES)).T
        state_ref[s, pair] = (state_ref[s, pair] * jnp.concatenate([decay, decay], axis=1)
                              + updates[s, pair] * smask_ref[...])
        o = outs[s, pair]
        ms = _dot((o * o).astype(BF16), bdms_ref[...])
        y = o * lax.rsqrt(ms + EPS) * ggla_ref[:, vl(pair)] * og_ref[s, :, vl(pair)].astype(F32)
        o_ref[s, :, vl(pair)] = y.astype(BF16)


def _gla(qg, kg, vg, la, og, cum, qmask, bdms, smask, ggla, *, B, T):
    L = GLA_CHUNK
    nb = GLA_BATCH_PER_STEP
    seq = lambda a: a.reshape(B, T, a.shape[-1])
    rows = lambda c: pl.BlockSpec((nb, L, c), lambda b, i: (b, i, 0))
    const = lambda a: pl.BlockSpec(a.shape, lambda b, i: (0,) * a.ndim)
    out = pl.pallas_call(
        _gla_kernel,
        grid=(B // nb, T // L),
        in_specs=[rows(GLA_K), rows(GLA_K), rows(GLA_V), rows(GLA_K), rows(GLA_V),
                  const(cum), const(qmask), const(bdms), const(smask), const(ggla)],
        out_specs=rows(GLA_V),
        out_shape=jax.ShapeDtypeStruct((B, T, GLA_V), BF16),
        scratch_shapes=[pltpu.VMEM((nb, GLA_HEADS // 2, LANES, 2 * GLA_DV), F32)],
        compiler_params=_params("arbitrary", "arbitrary"),
        name="gla",
    )(seq(qg), seq(kg), seq(vg), seq(la), seq(og), cum, qmask, bdms, smask, ggla)
    return out.reshape(B * T, GLA_V)


def _route(logits_t):
    lt = logits_t[0:ROUTER_ROWS, :]
    row = lax.broadcasted_iota(jnp.int32, lt.shape, 0)
    neg_inf = -jnp.inf
    g_log = jnp.where(row < N_GROUPS, lt, neg_inf)
    g_max = jnp.max(g_log, axis=0, keepdims=True)
    g_sel = jnp.min(jnp.where(g_log == g_max, row, LANES), axis=0, keepdims=True)
    g_sum = jnp.sum(jnp.where(row < N_GROUPS, jnp.exp(lt - g_max), 0.0), axis=0, keepdims=True)
    p_group = 1.0 / g_sum
    e_lo = ROUTER_LANE0 + EXPERTS_PER_GROUP * g_sel
    in_group = jnp.logical_and(row >= e_lo, row < e_lo + EXPERTS_PER_GROUP)
    e_log = jnp.where(in_group, lt, neg_inf)
    e_max = jnp.max(e_log, axis=0, keepdims=True)
    top1 = jnp.min(jnp.where(e_log == e_max, row, LANES), axis=0, keepdims=True)
    e_log2 = jnp.where(row == top1, neg_inf, e_log)
    e_max2 = jnp.max(e_log2, axis=0, keepdims=True)
    top2 = jnp.min(jnp.where(e_log2 == e_max2, row, LANES), axis=0, keepdims=True)
    ratio = jnp.exp(e_max2 - e_max)
    w_top1 = p_group / (1.0 + ratio)
    w_top2 = p_group * ratio / (1.0 + ratio)
    row8 = lax.broadcasted_iota(jnp.int32, (8, lt.shape[1]), 0)
    weights = jnp.where(row8 == top1 - e_lo, w_top1, 0.0) + jnp.where(row8 == top2 - e_lo, w_top2, 0.0)
    return g_sel, weights


def _outproj_kernel(ya_ref, yg_ref, x_ref, mod_ref, wo_ref, g2_ref, wrt_ref, brt_ref, striu_ref,
                    x1_ref, row_ref, lpos_ref, cnt_ref):
    tm = MOE_TILE
    subs = range(x_ref.shape[0] // tm)
    rows = lambda t: slice(t * tm, (t + 1) * tm)

    mix = [_dot(ya_ref[rows(t), :], wo_ref[0:ATT_Q, :]) + _dot(yg_ref[rows(t), :], wo_ref[ATT_Q:ATT_Q + GLA_V, :])
           for t in subs]
    h2b = []
    for t in subs:
        x1 = x_ref[rows(t), :] + mod_ref[2:3, :] * mix[t]
        x1_ref[rows(t), :] = x1
        ms = jnp.mean(x1 * x1, axis=-1, keepdims=True)
        h2 = (x1 * lax.rsqrt(ms + EPS) * g2_ref[...]) * (1.0 + mod_ref[4:5, :]) + mod_ref[3:4, :]
        h2b.append(h2.astype(BF16))
    logits_t = [_dot_nt(wrt_ref[...], h2b[t]) + brt_ref[...] for t in subs]

    routed = [_route(logits_t[t]) for t in subs]
    row8 = lax.broadcasted_iota(jnp.int32, (8, tm), 0)
    onehot = [jnp.where(row8 == routed[t][0], 1.0, 0.0) for t in subs]
    before = [_dot(onehot[t].astype(BF16), striu_ref[...]) for t in subs]

    local_row = lax.broadcasted_iota(jnp.int32, (LOCAL_ROWS, tm), 0).astype(F32)
    pad_rows = jnp.zeros((LANES - 8, tm), F32)
    for t in subs:
        count = jnp.sum(onehot[t], axis=1, keepdims=True)
        cnt_ref[t] = jnp.broadcast_to(count, (8, LANES))
        padded = jnp.broadcast_to(jnp.floor((count + (CHUNK - 1.0)) * (1.0 / CHUNK)) * CHUNK, (8, tm))
        start = jnp.zeros((8, tm), F32)
        for shift in range(1, N_GROUPS):
            start = start + jnp.where(row8 >= shift, pltpu.roll(padded, shift, axis=0), 0.0)
        lpos = jnp.sum(onehot[t] * (before[t] + start), axis=0, keepdims=True)
        lpos_ref[rows(t), :] = jnp.broadcast_to(lpos, (LANES, tm)).T
        weights = jnp.concatenate([routed[t][1], pad_rows], axis=0).T
        w_hi = weights.astype(BF16)
        w_lo = (weights - w_hi.astype(F32)).astype(BF16)
        perm = jnp.where(local_row == lpos, 1.0, 0.0).astype(BF16)
        row_ref[t * LOCAL_ROWS:(t + 1) * LOCAL_ROWS, :] = _dot(
            perm, jnp.concatenate([h2b[t], w_hi, w_lo], axis=1)).astype(BF16)


def _out_proj(ya, yg, x2, mod, wo, g2, wr, br, stril, *, T):
    N, D = x2.shape
    subs = OUT_SUBS
    tm = subs * MOE_TILE
    per_b = T // tm
    rows = lambda c: pl.BlockSpec((tm, c), lambda i: (i, 0))
    const = lambda a: pl.BlockSpec(a.shape, lambda i: (0,) * a.ndim)
    return pl.pallas_call(
        _outproj_kernel,
        grid=(N // tm,),
        in_specs=[rows(ATT_Q), rows(GLA_V), rows(D),
                  pl.BlockSpec((None, 6, D), lambda i: (i // per_b, 0, 0)),
                  const(wo), const(g2), const(wr), const(br), const(stril)],
        out_specs=[rows(D), pl.BlockSpec((subs * LOCAL_ROWS, D + 2 * LANES), lambda i: (i, 0)), rows(LANES),
                   pl.BlockSpec((subs, 8, LANES), lambda i: (i, 0, 0))],
        out_shape=[jax.ShapeDtypeStruct((N, D), F32),
                   jax.ShapeDtypeStruct((N // MOE_TILE * LOCAL_ROWS, D + 2 * LANES), BF16),
                   jax.ShapeDtypeStruct((N, LANES), F32),
                   jax.ShapeDtypeStruct((N // MOE_TILE, 8, LANES), F32)],
        compiler_params=_params("arbitrary"),
        name="out_proj",
    )(ya, yg, x2, mod, wo, g2, wr, br, stril)


def _chunk_copy(src_ref, src_chunk, dst_ref, dst_chunk, sem):
    return pltpu.make_async_copy(src_ref.at[src_chunk], dst_ref.at[dst_chunk], sem)


def _moe_kernel(src_ref, dst_ref, grp_ref, nt_ref, used_ref, rows_ref, w1f_ref, w3f_ref, w2f_ref, y_ref,
                in_buf, out_buf, zero_buf, w1_ref, w3_ref, w2_ref, in_sem, out_sem, zero_sem):
    j = pl.program_id(0)
    n_tiles = nt_ref[0]
    d_model = w2_ref.shape[2]

    @pl.when(jnp.logical_or(j == 0, grp_ref[j] != grp_ref[jnp.maximum(j - 1, 0)]))
    def _():
        for src, dst in ((w1f_ref, w1_ref), (w3f_ref, w3_ref), (w2f_ref, w2_ref)):
            for k in range(EXPERTS_PER_GROUP):
                dst[k] = src[k].astype(BF16)

    def gather(tile, slot):
        def body(k, carry):
            _chunk_copy(rows_ref, src_ref[tile * TILE_CHUNKS + k], in_buf.at[slot], k, in_sem.at[slot]).start()
            return carry
        lax.fori_loop(0, TILE_CHUNKS, body, 0, unroll=True)

    def wait_gather(slot):
        def body(k, carry):
            _chunk_copy(rows_ref, 0, in_buf.at[slot], k, in_sem.at[slot]).wait()
            return carry
        lax.fori_loop(0, TILE_CHUNKS, body, 0, unroll=True)

    def scatter(tile, slot):
        def body(k, carry):
            _chunk_copy(out_buf.at[slot], k, y_ref, dst_ref[tile * TILE_CHUNKS + k], out_sem.at[slot]).start()
            return carry
        lax.fori_loop(0, TILE_CHUNKS, body, 0, unroll=True)

    def wait_scatter(slot):
        def body(k, carry):
            _chunk_copy(out_buf.at[slot], k, y_ref, 0, out_sem.at[slot]).wait()
            return carry
        lax.fori_loop(0, TILE_CHUNKS, body, 0, unroll=True)

    def zero_fill(wait):
        def per_tile(i, carry):
            def body(c, inner):
                copy = _chunk_copy(zero_buf, 0, y_ref, i * LOCAL_CHUNKS + c, zero_sem)
                if wait:
                    copy.wait()
                else:
                    copy.start()
                return inner
            return lax.fori_loop(used_ref[i], LOCAL_CHUNKS, body, carry)
        lax.fori_loop(0, used_ref.shape[0], per_tile, 0)

    @pl.when(j == 0)
    def _():
        zero_buf[...] = jnp.zeros_like(zero_buf)
        scratch0 = used_ref.shape[0] * LOCAL_CHUNKS
        for wait in (False, True):
            for k in range(2 * TILE_CHUNKS):
                copy = _chunk_copy(zero_buf, 0, y_ref, scratch0 + k, zero_sem)
                copy.wait() if wait else copy.start()
        zero_fill(wait=False)
        gather(0, 0)

    @pl.when(j + 1 < n_tiles)
    def _():
        gather(j + 1, (j + 1) % 2)

    @pl.when(j < n_tiles)
    def _():
        slot = j % 2
        wait_gather(slot)
        rows = in_buf[slot].reshape(MOE_TILE, in_buf.shape[-1])
        h = rows[:, 0:d_model]
        weights = rows[:, d_model:d_model + LANES].astype(F32) + rows[:, d_model + LANES:].astype(F32)
        experts = range(EXPERTS_PER_GROUP)
        up = [(_dot(h, w1_ref[k]), _dot(h, w3_ref[k])) for k in experts]
        hid = [(a * _sigmoid(a) * g * weights[:, k:k + 1]).astype(BF16) for k, (a, g) in zip(experts, up)]
        y = _dot(hid[0], w2_ref[0])
        for k in experts[1:]:
            y = y + _dot(hid[k], w2_ref[k])

        @pl.when(j >= 2)
        def _():
            wait_scatter(slot)

        out_buf[slot] = y.astype(BF16).reshape(TILE_CHUNKS, CHUNK, d_model)
        scatter(j, slot)

        @pl.when(j == n_tiles - 1)
        def _():
            @pl.when(j >= 1)
            def _():
                wait_scatter(1 - slot)
            wait_scatter(slot)
            zero_fill(wait=True)


def _combine_kernel(x1_ref, mod_ref, lpos_ref, y_ref, o_ref):
    tm = MOE_TILE
    local_row = lax.broadcasted_iota(jnp.int32, (tm, LOCAL_ROWS), 1).astype(F32)
    for t in range(x1_ref.shape[0] // tm):
        rows = slice(t * tm, (t + 1) * tm)
        unsort = jnp.where(local_row == lpos_ref[rows, 0:1], 1.0, 0.0).astype(BF16)
        y = _dot(unsort, y_ref[t * LOCAL_ROWS:(t + 1) * LOCAL_ROWS, :])
        o_ref[rows, :] = x1_ref[rows, :] + mod_ref[5:6, :] * y


def _moe_plan(cnt):
    n_local = cnt.shape[0]
    chunks = (cnt + CHUNK - 1) // CHUNK
    used = jnp.sum(chunks, axis=1)
    local_off = jnp.cumsum(chunks, axis=1) - chunks
    tiles_g = (jnp.sum(chunks, axis=0) + TILE_CHUNKS - 1) // TILE_CHUNKS
    tile_end = jnp.cumsum(tiles_g)
    n_tiles = tile_end[-1]
    group_start = (tile_end - tiles_g) * TILE_CHUNKS
    seg_len = chunks.T.reshape(-1)
    seg_start = (group_start[:, None] + (jnp.cumsum(chunks, axis=0) - chunks).T).reshape(-1)
    seg_src = (jnp.arange(n_local)[None, :] * LOCAL_CHUNKS + local_off.T).reshape(-1)
    max_chunks = n_local * MOE_TILE // CHUNK + n_local * N_GROUPS + N_GROUPS * TILE_CHUNKS
    max_tiles = (max_chunks + TILE_CHUNKS - 1) // TILE_CHUNKS
    c = jnp.arange(max_tiles * TILE_CHUNKS)[:, None]
    within = c - seg_start[None, :]
    hit = jnp.logical_and(within >= 0, within < seg_len[None, :])
    valid = jnp.any(hit, axis=1)
    src = jnp.sum(jnp.where(hit, seg_src[None, :] + within, 0), axis=1)
    src = jnp.where(valid, src, LOCAL_CHUNKS - 1)
    slot_k = c[:, 0] % (2 * TILE_CHUNKS)
    dst = jnp.where(valid, src, n_local * LOCAL_CHUNKS + slot_k)
    j = jnp.minimum(jnp.arange(max_tiles), n_tiles - 1)
    grp = jnp.sum(j[:, None] >= tile_end[None, :], axis=1)
    i32 = lambda a: a.astype(jnp.int32)
    return i32(src), i32(dst), i32(grp), i32(n_tiles).reshape(1), i32(used)


def _moe(plan, rows_local, w1g, w3g, w2g):
    src, dst, grp, n_tiles, used = plan
    D = w2g.shape[3]
    n_rows, cols = rows_local.shape
    n_chunks = n_rows // CHUNK
    weights = lambda a: pl.BlockSpec((None,) + a.shape[1:], lambda j, src, nv, grp, nt, used: (grp[j], 0, 0, 0))
    y = pl.pallas_call(
        _moe_kernel,
        grid_spec=pltpu.PrefetchScalarGridSpec(
            num_scalar_prefetch=5,
            grid=(grp.shape[0],),
            in_specs=[pl.BlockSpec(memory_space=pl.ANY), weights(w1g), weights(w3g), weights(w2g)],
            out_specs=pl.BlockSpec(memory_space=pl.ANY),
            scratch_shapes=[pltpu.VMEM((2, TILE_CHUNKS, CHUNK, cols), BF16),
                            pltpu.VMEM((2, TILE_CHUNKS, CHUNK, D), BF16),
                            pltpu.VMEM((1, CHUNK, D), BF16),
                            pltpu.VMEM(w1g.shape[1:], BF16), pltpu.VMEM(w3g.shape[1:], BF16),
                            pltpu.VMEM(w2g.shape[1:], BF16),
                            pltpu.SemaphoreType.DMA((2,)), pltpu.SemaphoreType.DMA((2,)),
                            pltpu.SemaphoreType.DMA(())]),
        out_shape=jax.ShapeDtypeStruct((n_chunks + 2 * TILE_CHUNKS, CHUNK, D), BF16),
        compiler_params=_params("arbitrary"),
        name="moe",
    )(src, dst, grp, n_tiles, used, rows_local.reshape(n_chunks, CHUNK, cols), w1g, w3g, w2g)
    return y.reshape((n_chunks + 2 * TILE_CHUNKS) * CHUNK, D)


def _combine(x1, mod, lpos, y_local, *, T):
    N, D = x1.shape
    subs = COMBINE_SUBS
    tm = subs * MOE_TILE
    per_b = T // tm
    return pl.pallas_call(
        _combine_kernel,
        grid=(N // tm,),
        in_specs=[pl.BlockSpec((tm, D), lambda i: (i, 0)),
                  pl.BlockSpec((None, 6, D), lambda i: (i // per_b, 0, 0)),
                  pl.BlockSpec((tm, LANES), lambda i: (i, 0)),
                  pl.BlockSpec((subs * LOCAL_ROWS, D), lambda i: (i, 0))],
        out_specs=pl.BlockSpec((tm, D), lambda i: (i, 0)),
        out_shape=jax.ShapeDtypeStruct((N, D), F32),
        compiler_params=_params("arbitrary"),
        name="moe_combine",
    )(x1, mod, lpos, y_local)


def _block_diag(n, blk, value, dtype):
    r = np.arange(n)[:, None] // blk
    c = np.arange(n)[None, :] // blk
    return jnp.asarray(np.where(r == c, value, 0.0), dtype)


def _gla_constants():
    L = GLA_CHUNK
    i = np.arange(L)[:, None]
    j = np.arange(L)[None, :]
    tri = j <= i
    bdtri = np.logical_and(j <= i, i // GLA_SUB == j // GLA_SUB)
    cum = np.block([[tri, tri], [bdtri, bdtri]])
    col = np.arange(N_SUB * LANES)[None, :]
    qmask = np.stack([np.logical_and(col // LANES == i // GLA_SUB, (col % LANES) // GLA_DK == hh)
                      for hh in range(2)])
    bdms = _block_diag(2 * GLA_DV, GLA_DV, 1.0 / GLA_DV, BF16)
    d = np.arange(LANES)[:, None] // GLA_DK
    e = np.arange(2 * GLA_DV)[None, :] // GLA_DV
    smask = d == e
    return (jnp.asarray(cum, BF16), jnp.asarray(qmask, BF16), bdms, jnp.asarray(smask, F32))


def kernel(x, c, w_ada, b_ada, g_norm1, w_in, q_norm, k_norm, sinks, w_gk2, b_gk, g_gla_out, g_att_out,
           w_out, g_norm2, w_group, b_group, w_router, b_router, w1, w3, w2):
    B, T, D = x.shape
    N = B * T
    depth = w_ada.shape[0]
    cum, qmask, bdms, smask = _gla_constants()
    bdq = _block_diag(2 * LANES, HEAD_DIM, 1.0 / HEAD_DIM, BF16)
    bdk = _block_diag(ATT_KV, HEAD_DIM, 1.0 / HEAD_DIM, BF16)

    x2 = x.reshape(N, D)
    for l in range(depth):
        mod = _adaln_mod(c, w_ada[l], b_ada[l]).reshape(B, 6, D)

        wgk =jnp.concatenate([w_gk2[l], jnp.zeros((LANES - GLA_RANK, GLA_K), F32)], axis=0).astype(BF16)
        qa, ka, va, qg, kg, vg, la, og = _in_proj(
            x2, mod, g_norm1[l].reshape(1, D), w_in[l].T,
            jnp.tile(q_norm[l], ATT_HEADS).reshape(1, ATT_Q), jnp.tile(k_norm[l], ATT_KV_HEADS).reshape(1, ATT_KV),
            bdq, bdk, wgk, b_gk[l].reshape(1, GLA_K), T=T, tm=PROJ_TILE)

        y_att = _swa_attention(sinks[l], qa, ka, va, g_att_out[l].reshape(1, ATT_Q), B=B, T=T)
        y_gla = _gla(qg, kg, vg, la, og, cum, qmask, bdms, smask,
                     jnp.tile(g_gla_out[l], GLA_HEADS).reshape(1, GLA_V), B=B, T=T)

        pad = LANES - N_GROUPS - N_EXPERTS
        wr_t = jnp.concatenate([w_group[l], w_router[l], jnp.zeros((D, pad), F32)], axis=1).T.astype(BF16)
        br_t = jnp.concatenate([b_group[l], b_router[l], jnp.zeros((pad,), F32)]).reshape(LANES, 1)
        striu = jnp.asarray(np.arange(MOE_TILE)[:, None] < np.arange(MOE_TILE)[None, :], BF16)
        x1, rows_local, lpos, cnt = _out_proj(y_att, y_gla, x2, mod, w_out[l].astype(BF16),
                                              g_norm2[l].reshape(1, D), wr_t, br_t, striu, T=T)
        plan = _moe_plan(cnt[:, :N_GROUPS, 0].astype(jnp.int32))
        by_group = lambda w: w.reshape((N_GROUPS, EXPERTS_PER_GROUP) + w.shape[1:])
        y_local = _moe(plan, rows_local, by_group(w1[l]), by_group(w3[l]), by_group(w2[l]))
        x2 = _combine(x1, mod, lpos, y_local, T=T)
    return x2.reshape(B, T, D)
```

```python
import jax
import jax.numpy as jnp
import numpy as np
from jax import lax
from jax.experimental import pallas as pl
from jax.experimental.pallas import tpu as pltpu

F32 = jnp.float32
BF16 = jnp.bfloat16

EPS = 1e-6
ATT_HEADS = 8
ATT_KV_HEADS = 2
HEAD_DIM = 64
WINDOW = 128
ATT_Q = ATT_HEADS * HEAD_DIM
ATT_KV = ATT_KV_HEADS * HEAD_DIM
GLA_HEADS = 4
GLA_DK = 64
GLA_DV = 128
GLA_RANK = 16
GLA_NORMALIZER = 16.0
GLA_K = GLA_HEADS * GLA_DK
GLA_V = GLA_HEADS * GLA_DV
N_GROUPS = 4
EXPERTS_PER_GROUP = 4
N_EXPERTS = N_GROUPS * EXPERTS_PER_GROUP

LANES = 128
PROJ_SPLITS = (512, 512)
PROJ_TILE = sum(PROJ_SPLITS)
OUT_SUBS = 4
COMBINE_SUBS = 4
ROUTER_ROWS = 24
ATT_BLOCKS_PER_STEP = 4
GLA_BATCH_PER_STEP = 4
GLA_CHUNK = 128
GLA_SUB = 16
N_SUB = GLA_CHUNK // GLA_SUB
ROUTER_LANE0 = N_GROUPS
VMEM_LIMIT = 56 * 1024 * 1024
MOE_TILE = 256
CHUNK = 16
TILE_CHUNKS = MOE_TILE // CHUNK
LOCAL_CHUNKS = (MOE_TILE + N_GROUPS * (CHUNK - 1)) // CHUNK + 2
LOCAL_ROWS = LOCAL_CHUNKS * CHUNK

_QA0, _KA0, _VA0 = 0, ATT_Q, ATT_Q + ATT_KV
_QG0 = _VA0 + ATT_KV
_KG0 = _QG0 + GLA_K
_VG0 = _KG0 + GLA_K
_OG0 = _VG0 + GLA_V
_LR0 = _OG0 + GLA_V
IN_COLS_PAD = _LR0 + LANES


def _dot(a, b):
    return jnp.dot(a, b, preferred_element_type=F32)


def _dot_nt(a, b):
    return lax.dot_general(a, b, (((1,), (1,)), ((), ())), preferred_element_type=F32)


def _sigmoid(x):
    return 1.0 / (1.0 + jnp.exp(-x))


def _params(*sem):
    return pltpu.CompilerParams(dimension_semantics=sem, vmem_limit_bytes=VMEM_LIMIT)


def _mod_kernel(c_ref, w_ref, b_ref, o_ref):
    c = c_ref[...]
    s = (c * _sigmoid(c)).astype(BF16)
    o_ref[...] = _dot(s, w_ref[...].astype(BF16)) + b_ref[...]


def _adaln_mod(c, w_ada, b_ada):
    B, D = c.shape
    n = w_ada.shape[1]
    tn = 1536
    return pl.pallas_call(
        _mod_kernel,
        grid=(n // tn,),
        in_specs=[pl.BlockSpec((B, D), lambda j: (0, 0)),
                  pl.BlockSpec((D, tn), lambda j: (0, j)),
                  pl.BlockSpec((1, tn), lambda j: (0, j))],
        out_specs=pl.BlockSpec((B, tn), lambda j: (0, j)),
        out_shape=jax.ShapeDtypeStruct((B, n), F32),
        compiler_params=_params("arbitrary"),
        name="adaln_mod",
    )(c, w_ada, b_ada.reshape(1, n))


def _inproj_kernel(x_ref, mod_ref, g1_ref, wf_ref, qn_ref, kn_ref, bdq_ref, bdk_ref, wgk_ref, bgk_ref,
                   qa_ref, ka_ref, va_ref, qg_ref, kg_ref, vg_ref, la_ref, og_ref, w_ref):
    subs = range(len(PROJ_SPLITS))
    starts = [sum(PROJ_SPLITS[:t]) for t in subs]
    rows = lambda t: slice(starts[t], starts[t] + PROJ_SPLITS[t])

    @pl.when(pl.program_id(0) == 0)
    def _():
        lr_src = _OG0
        w_ref[0:_OG0, :] = wf_ref[0:_OG0, :].astype(BF16)
        w_ref[_OG0:_LR0, :] = wf_ref[lr_src + GLA_RANK:lr_src + GLA_RANK + GLA_V, :].astype(BF16)
        w_ref[_LR0:_LR0 + GLA_RANK, :] = wf_ref[lr_src:lr_src + GLA_RANK, :].astype(BF16)
        w_ref[_LR0 + GLA_RANK:IN_COLS_PAD, :] = jnp.zeros((LANES - GLA_RANK, wf_ref.shape[1]), BF16)

    h = []
    for t in subs:
        x = x_ref[rows(t), :]
        ms = jnp.mean(x * x, axis=-1, keepdims=True)
        xn = x * lax.rsqrt(ms + EPS) * g1_ref[...]
        h.append((xn * (1.0 + mod_ref[1:2, :]) + mod_ref[0:1, :]).astype(BF16))

    proj = lambda t, c0, width: _dot_nt(h[t], w_ref[c0:c0 + width, :])
    gate_of = lambda lr: _dot(lr.astype(BF16), wgk_ref[...]) + bgk_ref[...]
    qa, kv, q_ms, k_ms, qk_g, vg, og, lr, gate = ({} for _ in range(9))
    for t in subs:
        qa[t] = proj(t, _QA0, ATT_Q)
        if t > 0:
            gate[t - 1] = gate_of(lr[t - 1])
        kv[t] = proj(t, _KA0, 2 * ATT_KV)
        q_sq = (qa[t] * qa[t]).astype(BF16)
        q_ms[t] = jnp.concatenate([_dot(q_sq[:, c:c + 2 * LANES], bdq_ref[...])
                                   for c in range(0, ATT_Q, 2 * LANES)], axis=1)
        qk_g[t] = proj(t, _QG0, 2 * GLA_K)
        k = kv[t][:, 0:ATT_KV]
        k_ms[t] = _dot((k * k).astype(BF16), bdk_ref[...])
        vg[t] = proj(t, _VG0, GLA_V)
        og[t] = proj(t, _OG0, GLA_V)
        lr[t] = proj(t, _LR0, LANES)
    gate[subs[-1]] = gate_of(lr[subs[-1]])

    for t in subs:
        low = lax.broadcasted_iota(jnp.int32, (PROJ_SPLITS[t], ATT_KV), 1) < HEAD_DIM
        qa_ref[rows(t), :] = (qa[t] * lax.rsqrt(q_ms[t] + EPS) * qn_ref[...] * (HEAD_DIM ** -0.5)).astype(BF16)
        k = kv[t][:, 0:ATT_KV] * lax.rsqrt(k_ms[t] + EPS) * kn_ref[...]
        v = kv[t][:, ATT_KV:2 * ATT_KV]
        for src, dst in ((k, ka_ref), (v, va_ref)):
            swapped = pltpu.roll(src, HEAD_DIM, axis=1)
            dst[rows(t), 0:LANES] = jnp.where(low, src, swapped).astype(BF16)
            dst[rows(t), LANES:2 * LANES] = jnp.where(low, swapped, src).astype(BF16)
        qg_ref[rows(t), :] = (qk_g[t][:, 0:GLA_K] * (GLA_DK ** -0.5)).astype(BF16)
        kg_ref[rows(t), :] = qk_g[t][:, GLA_K:2 * GLA_K].astype(BF16)
        vg_ref[rows(t), :] = vg[t].astype(BF16)
        og_ref[rows(t), :] = (og[t] * _sigmoid(og[t])).astype(BF16)
        log_sig = jnp.minimum(gate[t], 0.0) - jnp.log(1.0 + jnp.exp(-jnp.abs(gate[t])))
        la_ref[rows(t), :] = log_sig * (1.0 / GLA_NORMALIZER)


def _in_proj(x2, mod, g1, w_in_t, qn, kn, bdq, bdk, wgk, bgk, *, T, tm):
    N, D = x2.shape
    per_b = T // tm
    const = lambda shape: pl.BlockSpec(shape, lambda i: (0,) * len(shape))
    rows = lambda c: pl.BlockSpec((tm, c), lambda i: (i, 0))
    outs = [(ATT_Q, BF16), (2 * ATT_KV, BF16), (2 * ATT_KV, BF16), (GLA_K, BF16), (GLA_K, BF16),
            (GLA_V, BF16), (GLA_K, F32), (GLA_V, BF16)]
    return pl.pallas_call(
        _inproj_kernel,
        grid=(N // tm,),
        in_specs=[rows(D),
                  pl.BlockSpec((None, 6, D), lambda i: (i // per_b, 0, 0)),
                  const((1, D)),
                  pl.BlockSpec(w_in_t.shape, lambda i: (0, 0), pipeline_mode=pl.Buffered(1)),
                  const(qn.shape), const(kn.shape),
                  const(bdq.shape), const(bdk.shape), const(wgk.shape), const(bgk.shape)],
        out_specs=[rows(c) for c, _ in outs],
        out_shape=[jax.ShapeDtypeStruct((N, c), dt) for c, dt in outs],
        scratch_shapes=[pltpu.VMEM((IN_COLS_PAD, D), BF16)],
        compiler_params=_params("arbitrary"),
        name="in_proj",
    )(x2, mod, g1, w_in_t, qn, kn, bdq, bdk, wgk, bgk)


def _attn_kernel(sinks_ref, q_ref, kc_ref, vc_ref, kp_ref, vp_ref, gatt_ref, o_ref):
    blk = WINDOW
    first = pl.program_id(1) == 0
    qi = lax.broadcasted_iota(jnp.int32, (blk, blk), 0)
    cj = lax.broadcasted_iota(jnp.int32, (blk, blk), 1)
    from_prev = cj > qi
    dist = (qi - cj + jnp.where(from_prev, blk, 0)).astype(F32)
    no_prev = jnp.where(jnp.logical_and(from_prev, first), -1e30, 0.0)
    low = cj < HEAD_DIM
    half = (jnp.where(low, 1.0, 0.0).astype(BF16), jnp.where(low, 0.0, 1.0).astype(BF16))
    half2 = tuple(jnp.concatenate([m, m], axis=0) for m in half)
    prev_mask = jnp.where(from_prev, 1.0, 0.0).astype(BF16)
    cur_mask = jnp.where(from_prev, 0.0, 1.0).astype(BF16)

    n_pairs = ATT_HEADS // 2
    pairs_per_kv = n_pairs // ATT_KV_HEADS
    units = [(bi, j) for bi in range(ATT_BLOCKS_PER_STEP) for j in range(n_pairs)]

    def kv_blocks(bi, g):
        rows = slice(bi * blk, (bi + 1) * blk)
        prev_rows = slice((bi - 1) * blk, bi * blk)
        lanes = slice(g * LANES, (g + 1) * LANES)
        cur = (kc_ref[rows, lanes], vc_ref[rows, lanes])
        prev = (kp_ref[:, lanes], vp_ref[:, lanes]) if bi == 0 else (kc_ref[prev_rows, lanes],
                                                                      vc_ref[prev_rows, lanes])
        return prev, cur

    scores = {}
    for bi, j in units:
        (kp, _), (kc, _) = kv_blocks(bi, j // pairs_per_kv)
        k_both = jnp.concatenate([kp, kc], axis=0)
        qp = q_ref[bi * blk:(bi + 1) * blk, j * LANES:(j + 1) * LANES]
        for p in range(2):
            scores[bi, j, p] = _dot_nt(qp, k_both * half2[p])

    probs, sink_terms = {}, {}
    for bi, j in units:
        for p in range(2):
            h = 2 * j + p
            slope = 2.0 ** (-8.0 * (h + 1) / ATT_HEADS)
            s_both = scores[bi, j, p]
            s = jnp.where(from_prev, s_both[:, 0:blk], s_both[:, blk:2 * blk]) - slope * dist
            if bi == 0:
                s = s + no_prev
            sink = sinks_ref[h]
            m = jnp.maximum(jnp.max(s, axis=-1, keepdims=True), sink)
            probs[bi, j, p] = jnp.exp(s - m)
            sink_terms[bi, j, p] = jnp.exp(sink - m)

    outs = {}
    for bi, j in units:
        (_, vp), (_, vc) = kv_blocks(bi, j // pairs_per_kv)
        v_stack = jnp.concatenate([jnp.concatenate([v * half[p], half[p]], axis=1)
                                   for v in (vp, vc) for p in range(2)], axis=0)
        e = [probs[bi, j, p].astype(BF16) for p in range(2)]
        p_all = jnp.concatenate([x * m for m in (prev_mask, cur_mask) for x in e], axis=1)
        pv = _dot(p_all, v_stack)
        den = pv[:, LANES:2 * LANES] + jnp.where(low, sink_terms[bi, j, 0], sink_terms[bi, j, 1])
        outs[bi, j] = pv[:, 0:LANES] / den

    for bi in range(ATT_BLOCKS_PER_STEP):
        o = jnp.concatenate([outs[bi, j] for j in range(n_pairs)], axis=1)
        ms = jnp.mean(o * o, axis=-1, keepdims=True)
        o_ref[bi * blk:(bi + 1) * blk, :] = (o * lax.rsqrt(ms + EPS) * gatt_ref[...]).astype(BF16)


def _swa_attention(sinks, qa, ka, va, gatt, *, B, T):
    step_rows = ATT_BLOCKS_PER_STEP * WINDOW
    nb = T // step_rows
    cur = lambda c: pl.BlockSpec((step_rows, c), lambda b, i: (b * nb + i, 0))
    prev = lambda c: pl.BlockSpec(
        (WINDOW, c), lambda b, i: (b * (T // WINDOW) + jnp.maximum(i * ATT_BLOCKS_PER_STEP - 1, 0), 0))
    return pl.pallas_call(
        _attn_kernel,
        grid=(B, nb),
        in_specs=[pl.BlockSpec(memory_space=pltpu.SMEM),
                  cur(ATT_Q), cur(2 * ATT_KV), cur(2 * ATT_KV), prev(2 * ATT_KV), prev(2 * ATT_KV),
                  pl.BlockSpec((1, ATT_Q), lambda b, i: (0, 0))],
        out_specs=cur(ATT_Q),
        out_shape=jax.ShapeDtypeStruct((B * T, ATT_Q), BF16),
        compiler_params=_params("arbitrary", "arbitrary"),
        name="swa_attn",
    )(sinks, qa, ka, va, ka, va, gatt)


def _gla_kernel(q_ref, k_ref, v_ref, la_ref, og_ref, cum_ref, qmask_ref, bdms_ref, smask_ref,
                ggla_ref, o_ref, state_ref):
    L = GLA_CHUNK

    @pl.when(pl.program_id(1) == 0)
    def _():
        state_ref[...] = jnp.zeros_like(state_ref)

    rows = lax.broadcasted_iota(jnp.int32, (L, LANES), 0)
    causal = lax.broadcasted_iota(jnp.int32, (L, L), 0) >= lax.broadcasted_iota(jnp.int32, (L, L), 1)
    seqs = range(GLA_BATCH_PER_STEP)
    units = [(s, pair) for s in seqs for pair in range(GLA_HEADS // 2)]
    kl = lambda pair: slice(pair * LANES, (pair + 1) * LANES)
    vl = lambda pair: slice(pair * 2 * GLA_DV, (pair + 1) * 2 * GLA_DV)

    b, b_in = {}, {}
    for s in seqs:
        la = la_ref[s]
        la_hi = la.astype(BF16)
        la_lo = (la - la_hi.astype(F32)).astype(BF16)
        sums = _dot(cum_ref[...], jnp.concatenate([la_hi, la_lo], axis=0))
        b[s] = sums[0:L, :]
        b_in[s] = sums[L:2 * L, :]

    q_both, keys, q_dec, k_dec_t, b_last = {}, {}, {}, {}, {}
    for s in seqs:
        ref = b[s] - b_in[s]
        b_last[s] = b[s][L - 1:L, :]
        q = q_ref[s].astype(F32)
        k = k_ref[s].astype(F32)
        q_in = (q * jnp.exp(b_in[s])).astype(BF16)
        q_dec[s] = (q * jnp.exp(b[s])).astype(BF16)
        k_dec = k * jnp.exp(b_last[s] - b[s])
        for pair in range(GLA_HEADS // 2):
            k_p, b_p, ref_p = k[:, kl(pair)], b[s][:, kl(pair)], ref[:, kl(pair)]
            expanded = []
            for g in range(N_SUB):
                last_row = (g + 1) * GLA_SUB - 1
                expo = jnp.where(rows <= last_row, ref_p[g * GLA_SUB:g * GLA_SUB + 1, :] - b_p, -1e4)
                expanded.append((k_p * jnp.exp(expo)).astype(BF16))
            keys[s, pair] = jnp.concatenate(expanded, axis=1)
            q_rep = jnp.concatenate([q_in[:, kl(pair)]] * N_SUB, axis=1)
            q_both[s, pair] = jnp.concatenate([q_rep * qmask_ref[0], q_rep * qmask_ref[1]], axis=0)
            k_dec_t[s, pair] = k_dec[:, kl(pair)].T.astype(BF16)

    scores = {u: _dot_nt(q_both[u], keys[u]) for u in units}

    outs, updates = {}, {}
    for s, pair in units:
        v_p = v_ref[s, :, vl(pair)]
        o_parts = []
        for hh in range(2):
            a = jnp.where(causal, scores[s, pair][hh * L:(hh + 1) * L, :], 0.0).astype(BF16)
            o_parts.append(_dot(a, v_p[:, hh * GLA_DV:(hh + 1) * GLA_DV]))
        state = state_ref[s, pair]
        outs[s, pair] = jnp.concatenate(o_parts, axis=1) + _dot(q_dec[s][:, kl(pair)], state.astype(BF16))
        updates[s, pair] = _dot(k_dec_t[s, pair], v_p)

    for s, pair in units:
        decay = jnp.broadcast_to(jnp.exp(b_last[s][:, kl(pair)]), (LANES, LANES)).T
        state_ref[s, pair] = (state_ref[s, pair] * jnp.concatenate([decay, decay], axis=1)
                              + updates[s, pair] * smask_ref[...])
        o = outs[s, pair]
        ms = _dot((o * o).astype(BF16), bdms_ref[...])
        y = o * lax.rsqrt(ms + EPS) * ggla_ref[:, vl(pair)] * og_ref[s, :, vl(pair)].astype(F32)
        o_ref[s, :, vl(pair)] = y.astype(BF16)


def _gla(qg, kg, vg, la, og, cum, qmask, bdms, smask, ggla, *, B, T):
    L = GLA_CHUNK
    nb = GLA_BATCH_PER_STEP
    seq = lambda a: a.reshape(B, T, a.shape[-1])
    rows = lambda c: pl.BlockSpec((nb, L, c), lambda b, i: (b, i, 0))
    const = lambda a: pl.BlockSpec(a.shape, lambda b, i: (0,) * a.ndim)
    out = pl.pallas_call(
        _gla_kernel,
        grid=(B // nb, T // L),
        in_specs=[rows(GLA_K), rows(GLA_K), rows(GLA_V), rows(GLA_K), rows(GLA_V),
                  const(cum), const(qmask), const(bdms), const(smask), const(ggla)],
        out_specs=rows(GLA_V),
        out_shape=jax.ShapeDtypeStruct((B, T, GLA_V), BF16),
        scratch_shapes=[pltpu.VMEM((nb, GLA_HEADS // 2, LANES, 2 * GLA_DV), F32)],
        compiler_params=_params("arbitrary", "arbitrary"),
        name="gla",
    )(seq(qg), seq(kg), seq(vg), seq(la), seq(og), cum, qmask, bdms, smask, ggla)
    return out.reshape(B * T, GLA_V)


def _route(logits_t):
    lt = logits_t[0:ROUTER_ROWS, :]
    row = lax.broadcasted_iota(jnp.int32, lt.shape, 0)
    neg_inf = -jnp.inf
    g_log = jnp.where(row < N_GROUPS, lt, neg_inf)
    g_max = jnp.max(g_log, axis=0, keepdims=True)
    g_sel = jnp.min(jnp.where(g_log == g_max, row, LANES), axis=0, keepdims=True)
    g_sum = jnp.sum(jnp.where(row < N_GROUPS, jnp.exp(lt - g_max), 0.0), axis=0, keepdims=True)
    p_group = 1.0 / g_sum
    e_lo = ROUTER_LANE0 + EXPERTS_PER_GROUP * g_sel
    in_group = jnp.logical_and(row >= e_lo, row < e_lo + EXPERTS_PER_GROUP)
    e_log = jnp.where(in_group, lt, neg_inf)
    e_max = jnp.max(e_log, axis=0, keepdims=True)
    top1 = jnp.min(jnp.where(e_log == e_max, row, LANES), axis=0, keepdims=True)
    e_log2 = jnp.where(row == top1, neg_inf, e_log)
    e_max2 = jnp.max(e_log2, axis=0, keepdims=True)
    top2 = jnp.min(jnp.where(e_log2 == e_max2, row, LANES), axis=0, keepdims=True)
    ratio = jnp.exp(e_max2 - e_max)
    w_top1 = p_group / (1.0 + ratio)
    w_top2 = p_group * ratio / (1.0 + ratio)
    row8 = lax.broadcasted_iota(jnp.int32, (8, lt.shape[1]), 0)
    weights = jnp.where(row8 == top1 - e_lo, w_top1, 0.0) + jnp.where(row8 == top2 - e_lo, w_top2, 0.0)
    return g_sel, weights


def _outproj_kernel(ya_ref, yg_ref, x_ref, mod_ref, wo_ref, g2_ref, wrt_ref, brt_ref, striu_ref,
                    x1_ref, row_ref, lpos_ref, cnt_ref):
    tm = MOE_TILE
    subs = range(x_ref.shape[0] // tm)
    rows = lambda t: slice(t * tm, (t + 1) * tm)

    mix = [_dot(ya_ref[rows(t), :], wo_ref[0:ATT_Q, :]) + _dot(yg_ref[rows(t), :], wo_ref[ATT_Q:ATT_Q + GLA_V, :])
           for t in subs]
    h2b = []
    for t in subs:
        x1 = x_ref[rows(t), :] + mod_ref[2:3, :] * mix[t]
        x1_ref[rows(t), :] = x1
        ms = jnp.mean(x1 * x1, axis=-1, keepdims=True)
        h2 = (x1 * lax.rsqrt(ms + EPS) * g2_ref[...]) * (1.0 + mod_ref[4:5, :]) + mod_ref[3:4, :]
        h2b.append(h2.astype(BF16))
    logits_t = [_dot_nt(wrt_ref[...], h2b[t]) + brt_ref[...] for t in subs]

    routed = [_route(logits_t[t]) for t in subs]
    row8 = lax.broadcasted_iota(jnp.int32, (8, tm), 0)
    onehot = [jnp.where(row8 == routed[t][0], 1.0, 0.0) for t in subs]
    before = [_dot(onehot[t].astype(BF16), striu_ref[...]) for t in subs]

    local_row = lax.broadcasted_iota(jnp.int32, (LOCAL_ROWS, tm), 0).astype(F32)
    pad_rows = jnp.zeros((LANES - 8, tm), F32)
    for t in subs:
        count = jnp.sum(onehot[t], axis=1, keepdims=True)
        cnt_ref[t] = jnp.broadcast_to(count, (8, LANES))
        padded = jnp.broadcast_to(jnp.floor((count + (CHUNK - 1.0)) * (1.0 / CHUNK)) * CHUNK, (8, tm))
        start = jnp.zeros((8, tm), F32)
        for shift in range(1, N_GROUPS):
            start = start + jnp.where(row8 >= shift, pltpu.roll(padded, shift, axis=0), 0.0)
        lpos = jnp.sum(onehot[t] * (before[t] + start), axis=0, keepdims=True)
        lpos_ref[rows(t), :] = jnp.broadcast_to(lpos, (LANES, tm)).T
        weights = jnp.concatenate([routed[t][1], pad_rows], axis=0).T
        w_hi = weights.astype(BF16)
        w_lo = (weights - w_hi.astype(F32)).astype(BF16)
        perm = jnp.where(local_row == lpos, 1.0, 0.0).astype(BF16)
        row_ref[t * LOCAL_ROWS:(t + 1) * LOCAL_ROWS, :] = _dot(
            perm, jnp.concatenate([h2b[t], w_hi, w_lo], axis=1)).astype(BF16)


def _out_proj(ya, yg, x2, mod, wo, g2, wr, br, stril, *, T):
    N, D = x2.shape
    subs = OUT_SUBS
    tm = subs * MOE_TILE
    per_b = T // tm
    rows = lambda c: pl.BlockSpec((tm, c), lambda i: (i, 0))
    const = lambda a: pl.BlockSpec(a.shape, lambda i: (0,) * a.ndim)
    return pl.pallas_call(
        _outproj_kernel,
        grid=(N // tm,),
        in_specs=[rows(ATT_Q), rows(GLA_V), rows(D),
                  pl.BlockSpec((None, 6, D), lambda i: (i // per_b, 0, 0)),
                  const(wo), const(g2), const(wr), const(br), const(stril)],
        out_specs=[rows(D), pl.BlockSpec((subs * LOCAL_ROWS, D + 2 * LANES), lambda i: (i, 0)), rows(LANES),
                   pl.BlockSpec((subs, 8, LANES), lambda i: (i, 0, 0))],
        out_shape=[jax.ShapeDtypeStruct((N, D), F32),
                   jax.ShapeDtypeStruct((N // MOE_TILE * LOCAL_ROWS, D + 2 * LANES), BF16),
                   jax.ShapeDtypeStruct((N, LANES), F32),
                   jax.ShapeDtypeStruct((N // MOE_TILE, 8, LANES), F32)],
        compiler_params=_params("arbitrary"),
        name="out_proj",
    )(ya, yg, x2, mod, wo, g2, wr, br, stril)


def _chunk_copy(src_ref, src_chunk, dst_ref, dst_chunk, sem):
    return pltpu.make_async_copy(src_ref.at[src_chunk], dst_ref.at[dst_chunk], sem)


def _moe_kernel(src_ref, dst_ref, grp_ref, next_ref, nt_ref, used_ref, rows_ref, w1f_ref, w3f_ref, w2f_ref, y_ref,
                in_buf, out_buf, zero_buf, st1_ref, st3_ref, st2_ref, w1_ref, w3_ref, w2_ref,
                in_sem, out_sem, zero_sem, w_sem):
    j = pl.program_id(0)
    n_tiles = nt_ref[0]
    d_model = w2_ref.shape[2]

    stages = ((w1f_ref, st1_ref, w1_ref), (w3f_ref, st3_ref, w3_ref), (w2f_ref, st2_ref, w2_ref))

    def fetch_weights(group):
        for hbm, stage, _ in stages:
            pltpu.make_async_copy(hbm.at[group], stage, w_sem).start()

    @pl.when(j == 0)
    def _():
        fetch_weights(grp_ref[0])

    @pl.when(jnp.logical_or(j == 0, grp_ref[j] != grp_ref[jnp.maximum(j - 1, 0)]))
    def _():
        for hbm, stage, dst in stages:
            pltpu.make_async_copy(hbm.at[0], stage, w_sem).wait()
        for hbm, stage, dst in stages:
            for k in range(EXPERTS_PER_GROUP):
                dst[k] = stage[k].astype(BF16)

        @pl.when(next_ref[j] != grp_ref[j])
        def _():
            fetch_weights(next_ref[j])

    def gather(tile, slot):
        def body(k, carry):
            _chunk_copy(rows_ref, src_ref[tile * TILE_CHUNKS + k], in_buf.at[slot], k, in_sem.at[slot]).start()
            return carry
        lax.fori_loop(0, TILE_CHUNKS, body, 0, unroll=True)

    def wait_gather(slot):
        def body(k, carry):
            _chunk_copy(rows_ref, 0, in_buf.at[slot], k, in_sem.at[slot]).wait()
            return carry
        lax.fori_loop(0, TILE_CHUNKS, body, 0, unroll=True)

    def scatter(tile, slot):
        def body(k, carry):
            _chunk_copy(out_buf.at[slot], k, y_ref, dst_ref[tile * TILE_CHUNKS + k], out_sem.at[slot]).start()
            return carry
        lax.fori_loop(0, TILE_CHUNKS, body, 0, unroll=True)

    def wait_scatter(slot):
        def body(k, carry):
            _chunk_copy(out_buf.at[slot], k, y_ref, 0, out_sem.at[slot]).wait()
            return carry
        lax.fori_loop(0, TILE_CHUNKS, body, 0, unroll=True)

    def zero_fill(wait):
        def per_tile(i, carry):
            def body(c, inner):
                copy = _chunk_copy(zero_buf, 0, y_ref, i * LOCAL_CHUNKS + c, zero_sem)
                if wait:
                    copy.wait()
                else:
                    copy.start()
                return inner
            return lax.fori_loop(used_ref[i], LOCAL_CHUNKS, body, carry)
        lax.fori_loop(0, used_ref.shape[0], per_tile, 0)

    @pl.when(j == 0)
    def _():
        zero_buf[...] = jnp.zeros_like(zero_buf)
        scratch0 = used_ref.shape[0] * LOCAL_CHUNKS
        for wait in (False, True):
            for k in range(2 * TILE_CHUNKS):
                copy = _chunk_copy(zero_buf, 0, y_ref, scratch0 + k, zero_sem)
                copy.wait() if wait else copy.start()
        zero_fill(wait=False)
        gather(0, 0)

    @pl.when(j + 1 < n_tiles)
    def _():
        gather(j + 1, (j + 1) % 2)

    @pl.when(j < n_tiles)
    def _():
        slot = j % 2
        wait_gather(slot)
        rows = in_buf[slot].reshape(MOE_TILE, in_buf.shape[-1])
        h = rows[:, 0:d_model]
        weights = rows[:, d_model:d_model + LANES].astype(F32) + rows[:, d_model + LANES:].astype(F32)
        experts = range(EXPERTS_PER_GROUP)
        up = [(_dot(h, w1_ref[k]), _dot(h, w3_ref[k])) for k in experts]
        hid = [(a * _sigmoid(a) * g * weights[:, k:k + 1]).astype(BF16) for k, (a, g) in zip(experts, up)]
        y = _dot(hid[0], w2_ref[0])
        for k in experts[1:]:
            y = y + _dot(hid[k], w2_ref[k])

        @pl.when(j >= 2)
        def _():
            wait_scatter(slot)

        out_buf[slot] = y.astype(BF16).reshape(TILE_CHUNKS, CHUNK, d_model)
        scatter(j, slot)

        @pl.when(j == n_tiles - 1)
        def _():
            @pl.when(j >= 1)
            def _():
                wait_scatter(1 - slot)
            wait_scatter(slot)
            zero_fill(wait=True)


def _combine_kernel(x1_ref, mod_ref, lpos_ref, y_ref, o_ref):
    tm = MOE_TILE
    local_row = lax.broadcasted_iota(jnp.int32, (tm, LOCAL_ROWS), 1).astype(F32)
    for t in range(x1_ref.shape[0] // tm):
        rows = slice(t * tm, (t + 1) * tm)
        unsort = jnp.where(local_row == lpos_ref[rows, 0:1], 1.0, 0.0).astype(BF16)
        y = _dot(unsort, y_ref[t * LOCAL_ROWS:(t + 1) * LOCAL_ROWS, :])
        o_ref[rows, :] = x1_ref[rows, :] + mod_ref[5:6, :] * y


def _moe_plan(cnt):
    n_local = cnt.shape[0]
    chunks = (cnt + CHUNK - 1) // CHUNK
    used = jnp.sum(chunks, axis=1)
    local_off = jnp.cumsum(chunks, axis=1) - chunks
    tiles_g = (jnp.sum(chunks, axis=0) + TILE_CHUNKS - 1) // TILE_CHUNKS
    tile_end = jnp.cumsum(tiles_g)
    n_tiles = tile_end[-1]
    group_start = (tile_end - tiles_g) * TILE_CHUNKS
    seg_len = chunks.T.reshape(-1)
    seg_start = (group_start[:, None] + (jnp.cumsum(chunks, axis=0) - chunks).T).reshape(-1)
    seg_src = (jnp.arange(n_local)[None, :] * LOCAL_CHUNKS + local_off.T).reshape(-1)
    max_chunks = n_local * MOE_TILE // CHUNK + n_local * N_GROUPS + N_GROUPS * TILE_CHUNKS
    max_tiles = (max_chunks + TILE_CHUNKS - 1) // TILE_CHUNKS
    c = jnp.arange(max_tiles * TILE_CHUNKS)[:, None]
    within = c - seg_start[None, :]
    hit = jnp.logical_and(within >= 0, within < seg_len[None, :])
    valid = jnp.any(hit, axis=1)
    src = jnp.sum(jnp.where(hit, seg_src[None, :] + within, 0), axis=1)
    src = jnp.where(valid, src, LOCAL_CHUNKS - 1)
    slot_k = c[:, 0] % (2 * TILE_CHUNKS)
    dst = jnp.where(valid, src, n_local * LOCAL_CHUNKS + slot_k)
    j = jnp.minimum(jnp.arange(max_tiles), n_tiles - 1)
    grp = jnp.sum(j[:, None] >= tile_end[None, :], axis=1)
    gid = jnp.arange(N_GROUPS)
    later = jnp.where(jnp.logical_and(gid[None, :] > gid[:, None], tiles_g[None, :] > 0), gid[None, :], N_GROUPS)
    next_of = jnp.min(later, axis=1)
    next_of = jnp.where(next_of == N_GROUPS, gid, next_of)
    next_grp = jnp.sum(jnp.where(grp[:, None] == gid[None, :], next_of[None, :], 0), axis=1)
    i32 = lambda a: a.astype(jnp.int32)
    return i32(src), i32(dst), i32(grp), i32(next_grp), i32(n_tiles).reshape(1), i32(used)


def _moe(plan, rows_local, w1g, w3g, w2g):
    src, dst, grp, next_grp, n_tiles, used = plan
    D = w2g.shape[3]
    n_rows, cols = rows_local.shape
    n_chunks = n_rows // CHUNK
    hbm = pl.BlockSpec(memory_space=pl.ANY)
    group_weights = [w1g, w3g, w2g]
    y = pl.pallas_call(
        _moe_kernel,
        grid_spec=pltpu.PrefetchScalarGridSpec(
            num_scalar_prefetch=6,
            grid=(grp.shape[0],),
            in_specs=[hbm, hbm, hbm, hbm],
            out_specs=hbm,
            scratch_shapes=([pltpu.VMEM((2, TILE_CHUNKS, CHUNK, cols), BF16),
                             pltpu.VMEM((2, TILE_CHUNKS, CHUNK, D), BF16),
                             pltpu.VMEM((1, CHUNK, D), BF16)]
                            + [pltpu.VMEM(w.shape[1:], F32) for w in group_weights]
                            + [pltpu.VMEM(w.shape[1:], BF16) for w in group_weights]
                            + [pltpu.SemaphoreType.DMA((2,)), pltpu.SemaphoreType.DMA((2,)),
                               pltpu.SemaphoreType.DMA(()), pltpu.SemaphoreType.DMA(())])),
        out_shape=jax.ShapeDtypeStruct((n_chunks + 2 * TILE_CHUNKS, CHUNK, D), BF16),
        compiler_params=_params("arbitrary"),
        name="moe",
    )(src, dst, grp, next_grp, n_tiles, used, rows_local.reshape(n_chunks, CHUNK, cols), w1g, w3g, w2g)
    return y.reshape((n_chunks + 2 * TILE_CHUNKS) * CHUNK, D)


def _combine(x1, mod, lpos, y_local, *, T):
    N, D = x1.shape
    subs = COMBINE_SUBS
    tm = subs * MOE_TILE
    per_b = T // tm
    return pl.pallas_call(
        _combine_kernel,
        grid=(N // tm,),
        in_specs=[pl.BlockSpec((tm, D), lambda i: (i, 0)),
                  pl.BlockSpec((None, 6, D), lambda i: (i // per_b, 0, 0)),
                  pl.BlockSpec((tm, LANES), lambda i: (i, 0)),
                  pl.BlockSpec((subs * LOCAL_ROWS, D), lambda i: (i, 0))],
        out_specs=pl.BlockSpec((tm, D), lambda i: (i, 0)),
        out_shape=jax.ShapeDtypeStruct((N, D), F32),
        compiler_params=_params("arbitrary"),
        name="moe_combine",
    )(x1, mod, lpos, y_local)


def _block_diag(n, blk, value, dtype):
    r = np.arange(n)[:, None] // blk
    c = np.arange(n)[None, :] // blk
    return jnp.asarray(np.where(r == c, value, 0.0), dtype)


def _gla_constants():
    L = GLA_CHUNK
    i = np.arange(L)[:, None]
    j = np.arange(L)[None, :]
    tri = j <= i
    bdtri = np.logical_and(j <= i, i // GLA_SUB == j // GLA_SUB)
    cum = np.block([[tri, tri], [bdtri, bdtri]])
    col = np.arange(N_SUB * LANES)[None, :]
    qmask = np.stack([np.logical_and(col // LANES == i // GLA_SUB, (col % LANES) // GLA_DK == hh)
                      for hh in range(2)])
    bdms = _block_diag(2 * GLA_DV, GLA_DV, 1.0 / GLA_DV, BF16)
    d = np.arange(LANES)[:, None] // GLA_DK
    e = np.arange(2 * GLA_DV)[None, :] // GLA_DV
    smask = d == e
    return (jnp.asarray(cum, BF16), jnp.asarray(qmask, BF16), bdms, jnp.asarray(smask, F32))


def kernel(x, c, w_ada, b_ada, g_norm1, w_in, q_norm, k_norm, sinks, w_gk2, b_gk, g_gla_out, g_att_out,
           w_out, g_norm2, w_group, b_group, w_router, b_router, w1, w3, w2):
    B, T, D = x.shape
    N = B * T
    depth = w_ada.shape[0]
    cum, qmask, bdms, smask = _gla_constants()
    bdq = _block_diag(2 * LANES, HEAD_DIM, 1.0 / HEAD_DIM, BF16)
    bdk = _block_diag(ATT_KV, HEAD_DIM, 1.0 / HEAD_DIM, BF16)

    x2 = x.reshape(N, D)
    for l in range(depth):
        mod = _adaln_mod(c, w_ada[l], b_ada[l]).reshape(B, 6, D)

        wgk =jnp.concatenate([w_gk2[l], jnp.zeros((LANES - GLA_RANK, GLA_K), F32)], axis=0).astype(BF16)
        qa, ka, va, qg, kg, vg, la, og = _in_proj(
            x2, mod, g_norm1[l].reshape(1, D), w_in[l].T,
            jnp.tile(q_norm[l], ATT_HEADS).reshape(1, ATT_Q), jnp.tile(k_norm[l], ATT_KV_HEADS).reshape(1, ATT_KV),
            bdq, bdk, wgk, b_gk[l].reshape(1, GLA_K), T=T, tm=PROJ_TILE)

        y_att = _swa_attention(sinks[l], qa, ka, va, g_att_out[l].reshape(1, ATT_Q), B=B, T=T)
        y_gla = _gla(qg, kg, vg, la, og, cum, qmask, bdms, smask,
                     jnp.tile(g_gla_out[l], GLA_HEADS).reshape(1, GLA_V), B=B, T=T)

        pad = LANES - N_GROUPS - N_EXPERTS
        wr_t = jnp.concatenate([w_group[l], w_router[l], jnp.zeros((D, pad), F32)], axis=1).T.astype(BF16)
        br_t = jnp.concatenate([b_group[l], b_router[l], jnp.zeros((pad,), F32)]).reshape(LANES, 1)
        striu = jnp.asarray(np.arange(MOE_TILE)[:, None] < np.arange(MOE_TILE)[None, :], BF16)
        x1, rows_local, lpos, cnt = _out_proj(y_att, y_gla, x2, mod, w_out[l].astype(BF16),
                                              g_norm2[l].reshape(1, D), wr_t, br_t, striu, T=T)
        plan = _moe_plan(cnt[:, :N_GROUPS, 0].astype(jnp.int32))
        by_group = lambda w: w.reshape((N_GROUPS, EXPERTS_PER_GROUP) + w.shape[1:])
        y_local = _moe(plan, rows_local, by_group(w1[l]), by_group(w3[l]), by_group(w2[l]))
        x2 = _combine(x1, mod, lpos, y_local, T=T)
    return x2.reshape(B, T, D)
```

```python
import jax
import jax.numpy as jnp
import numpy as np
from jax import lax
from jax.experimental import pallas as pl
from jax.experimental.pallas import tpu as pltpu

F32 = jnp.float32
BF16 = jnp.bfloat16

EPS = 1e-6
ATT_HEADS = 8
ATT_KV_HEADS = 2
HEAD_DIM = 64
WINDOW = 128
ATT_Q = ATT_HEADS * HEAD_DIM
ATT_KV = ATT_KV_HEADS * HEAD_DIM
GLA_HEADS = 4
GLA_DK = 64
GLA_DV = 128
GLA_RANK = 16
GLA_NORMALIZER = 16.0
GLA_K = GLA_HEADS * GLA_DK
GLA_V = GLA_HEADS * GLA_DV
N_GROUPS = 4
EXPERTS_PER_GROUP = 4
N_EXPERTS = N_GROUPS * EXPERTS_PER_GROUP

LANES = 128
PROJ_SPLITS = (512, 512)
PROJ_TILE = sum(PROJ_SPLITS)
OUT_SUBS = 4
COMBINE_SUBS = 4
ROUTER_ROWS = 24
SEQS_PER_STEP = 4
GLA_CHUNK = 128
GLA_SUB = 16
N_SUB = GLA_CHUNK // GLA_SUB
ROUTER_LANE0 = N_GROUPS
VMEM_LIMIT = 56 * 1024 * 1024
MOE_TILE = 256
EXPERT_TILE = 256
CHUNK = 16
TILE_CHUNKS = EXPERT_TILE // CHUNK
LOCAL_CHUNKS = (MOE_TILE + N_GROUPS * (CHUNK - 1)) // CHUNK + 2
LOCAL_ROWS = LOCAL_CHUNKS * CHUNK

_QA0, _KA0, _VA0 = 0, ATT_Q, ATT_Q + ATT_KV
_QG0 = _VA0 + ATT_KV
_KG0 = _QG0 + GLA_K
_VG0 = _KG0 + GLA_K
_OG0 = _VG0 + GLA_V
_LR0 = _OG0 + GLA_V
IN_COLS_PAD = _LR0 + LANES


def _dot(a, b):
    return jnp.dot(a, b, preferred_element_type=F32)


def _dot_nt(a, b):
    return lax.dot_general(a, b, (((1,), (1,)), ((), ())), preferred_element_type=F32)


def _sigmoid(x):
    return 1.0 / (1.0 + jnp.exp(-x))


def _params(*sem):
    return pltpu.CompilerParams(dimension_semantics=sem, vmem_limit_bytes=VMEM_LIMIT)


def _mod_kernel(c_ref, w_ref, b_ref, o_ref):
    c = c_ref[...]
    s = (c * _sigmoid(c)).astype(BF16)
    o_ref[...] = _dot(s, w_ref[...].astype(BF16)) + b_ref[...]


def _adaln_mod(c, w_ada, b_ada):
    B, D = c.shape
    n = w_ada.shape[1]
    tn = 1536
    return pl.pallas_call(
        _mod_kernel,
        grid=(n // tn,),
        in_specs=[pl.BlockSpec((B, D), lambda j: (0, 0)),
                  pl.BlockSpec((D, tn), lambda j: (0, j)),
                  pl.BlockSpec((1, tn), lambda j: (0, j))],
        out_specs=pl.BlockSpec((B, tn), lambda j: (0, j)),
        out_shape=jax.ShapeDtypeStruct((B, n), F32),
        compiler_params=_params("arbitrary"),
        name="adaln_mod",
    )(c, w_ada, b_ada.reshape(1, n))


def _inproj_kernel(x_ref, mod_ref, g1_ref, wf_ref, qn_ref, kn_ref, bdq_ref, bdk_ref, wgk_ref, bgk_ref,
                   qa_ref, ka_ref, va_ref, qg_ref, kg_ref, vg_ref, la_ref, og_ref, w_ref):
    subs = range(len(PROJ_SPLITS))
    starts = [sum(PROJ_SPLITS[:t]) for t in subs]
    rows = lambda t: slice(starts[t], starts[t] + PROJ_SPLITS[t])

    @pl.when(pl.program_id(0) == 0)
    def _():
        lr_src = _OG0
        w_ref[0:_OG0, :] = wf_ref[0:_OG0, :].astype(BF16)
        w_ref[_OG0:_LR0, :] = wf_ref[lr_src + GLA_RANK:lr_src + GLA_RANK + GLA_V, :].astype(BF16)
        w_ref[_LR0:_LR0 + GLA_RANK, :] = wf_ref[lr_src:lr_src + GLA_RANK, :].astype(BF16)
        w_ref[_LR0 + GLA_RANK:IN_COLS_PAD, :] = jnp.zeros((LANES - GLA_RANK, wf_ref.shape[1]), BF16)

    h = []
    for t in subs:
        x = x_ref[rows(t), :]
        ms = jnp.mean(x * x, axis=-1, keepdims=True)
        xn = x * lax.rsqrt(ms + EPS) * g1_ref[...]
        h.append((xn * (1.0 + mod_ref[1:2, :]) + mod_ref[0:1, :]).astype(BF16))

    proj = lambda t, c0, width: _dot_nt(h[t], w_ref[c0:c0 + width, :])
    gate_of = lambda lr: _dot(lr.astype(BF16), wgk_ref[...]) + bgk_ref[...]
    qa, kv, q_ms, k_ms, qk_g, vg, og, lr, gate = ({} for _ in range(9))
    for t in subs:
        qa[t] = proj(t, _QA0, ATT_Q)
        if t > 0:
            gate[t - 1] = gate_of(lr[t - 1])
        kv[t] = proj(t, _KA0, 2 * ATT_KV)
        q_sq = (qa[t] * qa[t]).astype(BF16)
        q_ms[t] = jnp.concatenate([_dot(q_sq[:, c:c + 2 * LANES], bdq_ref[...])
                                   for c in range(0, ATT_Q, 2 * LANES)], axis=1)
        qk_g[t] = proj(t, _QG0, 2 * GLA_K)
        k = kv[t][:, 0:ATT_KV]
        k_ms[t] = _dot((k * k).astype(BF16), bdk_ref[...])
        vg[t] = proj(t, _VG0, GLA_V)
        og[t] = proj(t, _OG0, GLA_V)
        lr[t] = proj(t, _LR0, LANES)
    gate[subs[-1]] = gate_of(lr[subs[-1]])

    for t in subs:
        low = lax.broadcasted_iota(jnp.int32, (PROJ_SPLITS[t], ATT_KV), 1) < HEAD_DIM
        qa_ref[rows(t), :] = (qa[t] * lax.rsqrt(q_ms[t] + EPS) * qn_ref[...] * (HEAD_DIM ** -0.5)).astype(BF16)
        k = kv[t][:, 0:ATT_KV] * lax.rsqrt(k_ms[t] + EPS) * kn_ref[...]
        v = kv[t][:, ATT_KV:2 * ATT_KV]
        for src, dst in ((k, ka_ref), (v, va_ref)):
            swapped = pltpu.roll(src, HEAD_DIM, axis=1)
            dst[rows(t), 0:LANES] = jnp.where(low, src, swapped).astype(BF16)
            dst[rows(t), LANES:2 * LANES] = jnp.where(low, swapped, src).astype(BF16)
        qg_ref[rows(t), :] = (qk_g[t][:, 0:GLA_K] * (GLA_DK ** -0.5)).astype(BF16)
        kg_ref[rows(t), :] = qk_g[t][:, GLA_K:2 * GLA_K].astype(BF16)
        vg_ref[rows(t), :] = vg[t].astype(BF16)
        og_ref[rows(t), :] = (og[t] * _sigmoid(og[t])).astype(BF16)
        log_sig = jnp.minimum(gate[t], 0.0) - jnp.log(1.0 + jnp.exp(-jnp.abs(gate[t])))
        la_ref[rows(t), :] = log_sig * (1.0 / GLA_NORMALIZER)


def _in_proj(x2, mod, g1, w_in_t, qn, kn, bdq, bdk, wgk, bgk, *, T, tm):
    N, D = x2.shape
    per_b = T // tm
    const = lambda shape: pl.BlockSpec(shape, lambda i: (0,) * len(shape))
    rows = lambda c: pl.BlockSpec((tm, c), lambda i: (i, 0))
    outs = [(ATT_Q, BF16), (2 * ATT_KV, BF16), (2 * ATT_KV, BF16), (GLA_K, BF16), (GLA_K, BF16),
            (GLA_V, BF16), (GLA_K, F32), (GLA_V, BF16)]
    return pl.pallas_call(
        _inproj_kernel,
        grid=(N // tm,),
        in_specs=[rows(D),
                  pl.BlockSpec((None, 6, D), lambda i: (i // per_b, 0, 0)),
                  const((1, D)),
                  pl.BlockSpec(w_in_t.shape, lambda i: (0, 0), pipeline_mode=pl.Buffered(1)),
                  const(qn.shape), const(kn.shape),
                  const(bdq.shape), const(bdk.shape), const(wgk.shape), const(bgk.shape)],
        out_specs=[rows(c) for c, _ in outs],
        out_shape=[jax.ShapeDtypeStruct((N, c), dt) for c, dt in outs],
        scratch_shapes=[pltpu.VMEM((IN_COLS_PAD, D), BF16)],
        compiler_params=_params("arbitrary"),
        name="in_proj",
    )(x2, mod, g1, w_in_t, qn, kn, bdq, bdk, wgk, bgk)


def _attn_stages(seqs, sinks_ref, q_ref, kc_ref, vc_ref, gatt_ref, o_ref, kp_ref, vp_ref):
    blk = WINDOW
    first = pl.program_id(1) == 0
    qi = lax.broadcasted_iota(jnp.int32, (blk, blk), 0)
    cj = lax.broadcasted_iota(jnp.int32, (blk, blk), 1)
    from_prev = cj > qi
    dist = (qi - cj + jnp.where(from_prev, blk, 0)).astype(F32)
    no_prev = jnp.where(jnp.logical_and(from_prev, first), -1e30, 0.0)
    low = cj < HEAD_DIM
    half = (jnp.where(low, 1.0, 0.0).astype(BF16), jnp.where(low, 0.0, 1.0).astype(BF16))
    half2 = tuple(jnp.concatenate([m, m], axis=0) for m in half)
    prev_mask = jnp.where(from_prev, 1.0, 0.0).astype(BF16)
    cur_mask = jnp.where(from_prev, 0.0, 1.0).astype(BF16)

    n_pairs = ATT_HEADS // 2
    pairs_per_kv = n_pairs // ATT_KV_HEADS
    units = [(bi, j) for bi in seqs for j in range(n_pairs)]

    def kv_blocks(bi, g):
        lanes = slice(g * LANES, (g + 1) * LANES)
        return (kp_ref[bi, :, lanes], vp_ref[bi, :, lanes]), (kc_ref[bi, :, lanes], vc_ref[bi, :, lanes])

    scores = {}
    for bi, j in units:
        (kp, _), (kc, _) = kv_blocks(bi, j // pairs_per_kv)
        k_both = jnp.concatenate([kp, kc], axis=0)
        qp = q_ref[bi, :, j * LANES:(j + 1) * LANES]
        for p in range(2):
            scores[bi, j, p] = _dot_nt(qp, k_both * half2[p])
    yield

    probs, sink_terms = {}, {}
    for bi, j in units:
        for p in range(2):
            h = 2 * j + p
            slope = 2.0 ** (-8.0 * (h + 1) / ATT_HEADS)
            s_both = scores[bi, j, p]
            s = jnp.where(from_prev, s_both[:, 0:blk], s_both[:, blk:2 * blk]) - slope * dist + no_prev
            sink = sinks_ref[h]
            m = jnp.maximum(jnp.max(s, axis=-1, keepdims=True), sink)
            probs[bi, j, p] = jnp.exp(s - m)
            sink_terms[bi, j, p] = jnp.exp(sink - m)
    yield

    outs = {}
    for bi, j in units:
        (_, vp), (_, vc) = kv_blocks(bi, j // pairs_per_kv)
        v_stack = jnp.concatenate([jnp.concatenate([v * half[p], half[p]], axis=1)
                                   for v in (vp, vc) for p in range(2)], axis=0)
        e = [probs[bi, j, p].astype(BF16) for p in range(2)]
        p_all = jnp.concatenate([x * m for m in (prev_mask, cur_mask) for x in e], axis=1)
        pv = _dot(p_all, v_stack)
        den = pv[:, LANES:2 * LANES] + jnp.where(low, sink_terms[bi, j, 0], sink_terms[bi, j, 1])
        outs[bi, j] = pv[:, 0:LANES] / den
    yield

    for bi in seqs:
        o = jnp.concatenate([outs[bi, j] for j in range(n_pairs)], axis=1)
        ms = jnp.mean(o * o, axis=-1, keepdims=True)
        o_ref[bi] = (o * lax.rsqrt(ms + EPS) * gatt_ref[...]).astype(BF16)
        kp_ref[bi] = kc_ref[bi]
        vp_ref[bi] = vc_ref[bi]


def _gla_stages(seqs, q_ref, k_ref, v_ref, la_ref, og_ref, cum_ref, qmask_ref, bdms_ref, smask_ref,
                ggla_ref, o_ref, state_ref):
    L = GLA_CHUNK
    rows = lax.broadcasted_iota(jnp.int32, (L, LANES), 0)
    causal = lax.broadcasted_iota(jnp.int32, (L, L), 0) >= lax.broadcasted_iota(jnp.int32, (L, L), 1)
    units = [(s, pair) for s in seqs for pair in range(GLA_HEADS // 2)]
    kl = lambda pair: slice(pair * LANES, (pair + 1) * LANES)
    vl = lambda pair: slice(pair * 2 * GLA_DV, (pair + 1) * 2 * GLA_DV)

    b, b_in = {}, {}
    for s in seqs:
        la = la_ref[s]
        la_hi = la.astype(BF16)
        la_lo = (la - la_hi.astype(F32)).astype(BF16)
        sums = _dot(cum_ref[...], jnp.concatenate([la_hi, la_lo], axis=0))
        b[s] = sums[0:L, :]
        b_in[s] = sums[L:2 * L, :]
    yield

    q_both, keys, q_dec, k_dec_t, b_last = {}, {}, {}, {}, {}
    for s in seqs:
        ref = b[s] - b_in[s]
        b_last[s] = b[s][L - 1:L, :]
        q = q_ref[s].astype(F32)
        k = k_ref[s].astype(F32)
        q_in = (q * jnp.exp(b_in[s])).astype(BF16)
        q_dec[s] = (q * jnp.exp(b[s])).astype(BF16)
        k_dec = k * jnp.exp(b_last[s] - b[s])
        for pair in range(GLA_HEADS // 2):
            k_p, b_p, ref_p = k[:, kl(pair)], b[s][:, kl(pair)], ref[:, kl(pair)]
            expanded = []
            for g in range(N_SUB):
                last_row = (g + 1) * GLA_SUB - 1
                expo = jnp.where(rows <= last_row, ref_p[g * GLA_SUB:g * GLA_SUB + 1, :] - b_p, -1e4)
                expanded.append((k_p * jnp.exp(expo)).astype(BF16))
            keys[s, pair] = jnp.concatenate(expanded, axis=1)
            q_rep = jnp.concatenate([q_in[:, kl(pair)]] * N_SUB, axis=1)
            q_both[s, pair] = jnp.concatenate([q_rep * qmask_ref[0], q_rep * qmask_ref[1]], axis=0)
            k_dec_t[s, pair] = k_dec[:, kl(pair)].T.astype(BF16)
    yield

    scores = {u: _dot_nt(q_both[u], keys[u]) for u in units}
    yield

    outs, updates = {}, {}
    for s, pair in units:
        v_p = v_ref[s, :, vl(pair)]
        o_parts = []
        for hh in range(2):
            a = jnp.where(causal, scores[s, pair][hh * L:(hh + 1) * L, :], 0.0).astype(BF16)
            o_parts.append(_dot(a, v_p[:, hh * GLA_DV:(hh + 1) * GLA_DV]))
        state = state_ref[s, pair]
        outs[s, pair] = jnp.concatenate(o_parts, axis=1) + _dot(q_dec[s][:, kl(pair)], state.astype(BF16))
        updates[s, pair] = _dot(k_dec_t[s, pair], v_p)
    yield

    for s, pair in units:
        decay =jnp.broadcast_to(jnp.exp(b_last[s][:, kl(pair)]), (LANES, LANES)).T
        state_ref[s, pair] = (state_ref[s, pair] * jnp.concatenate([decay, decay], axis=1)
                              + updates[s, pair] * smask_ref[...])
        o = outs[s, pair]
        ms = _dot((o * o).astype(BF16), bdms_ref[...])
        y = o * lax.rsqrt(ms + EPS) * ggla_ref[:, vl(pair)] * og_ref[s, :, vl(pair)].astype(F32)
        o_ref[s, :, vl(pair)] = y.astype(BF16)


def _mixers_kernel(sinks_ref, qa_ref, ka_ref, va_ref, gatt_ref,
                   qg_ref, kg_ref, vg_ref, la_ref, og_ref, cum_ref, qmask_ref, bdms_ref, smask_ref, ggla_ref,
                   ya_ref, yg_ref, kp_ref, vp_ref, state_ref):
    @pl.when(pl.program_id(1) == 0)
    def _():
        state_ref[...] = jnp.zeros_like(state_ref)
        kp_ref[...] = jnp.zeros_like(kp_ref)
        vp_ref[...] = jnp.zeros_like(vp_ref)

    n_seq = qa_ref.shape[0]
    waiting = []
    for group in (tuple(range(0, n_seq // 2)), tuple(range(n_seq // 2, n_seq))):
        waiting.append([
            _attn_stages(group, sinks_ref, qa_ref, ka_ref, va_ref, gatt_ref, ya_ref, kp_ref, vp_ref),
            _gla_stages(group, qg_ref, kg_ref, vg_ref, la_ref, og_ref, cum_ref, qmask_ref, bdms_ref, smask_ref,
                        ggla_ref, yg_ref, state_ref)])
    pending = []
    while pending or waiting:
        if waiting:
            pending.extend(waiting.pop(0))
        for stages in list(pending):
            if next(stages, "done") == "done":
                pending.remove(stages)


def _mixers(sinks, qa, ka, va, gatt, qg, kg, vg, la, og, cum, qmask, bdms, smask, ggla, *, B, T):
    assert GLA_CHUNK == WINDOW
    L = WINDOW
    nb = SEQS_PER_STEP
    seq = lambda a: a.reshape(B, T, a.shape[-1])
    rows = lambda c: pl.BlockSpec((nb, L, c), lambda b, i: (b, i, 0))
    const = lambda a: pl.BlockSpec(a.shape, lambda b, i: (0,) * a.ndim)
    y_att, y_gla = pl.pallas_call(
        _mixers_kernel,
        grid=(B // nb, T // L),
        in_specs=[pl.BlockSpec(memory_space=pltpu.SMEM),
                  rows(ATT_Q), rows(2 * ATT_KV), rows(2 * ATT_KV), const(gatt),
                  rows(GLA_K), rows(GLA_K), rows(GLA_V), rows(GLA_K), rows(GLA_V),
                  const(cum), const(qmask), const(bdms), const(smask), const(ggla)],
        out_specs=[rows(ATT_Q), rows(GLA_V)],
        out_shape=[jax.ShapeDtypeStruct((B, T, ATT_Q), BF16), jax.ShapeDtypeStruct((B, T, GLA_V), BF16)],
        scratch_shapes=[pltpu.VMEM((nb, L, 2 * ATT_KV), BF16), pltpu.VMEM((nb, L, 2 * ATT_KV), BF16),
                        pltpu.VMEM((nb, GLA_HEADS // 2, LANES, 2 * GLA_DV), F32)],
        compiler_params=_params("arbitrary", "arbitrary"),
        name="mixers",
    )(sinks, seq(qa), seq(ka), seq(va), gatt, seq(qg), seq(kg), seq(vg), seq(la), seq(og),
      cum, qmask, bdms, smask, ggla)
    return y_att.reshape(B * T, ATT_Q), y_gla.reshape(B * T, GLA_V)


def _route(logits_t):
    lt = logits_t[0:ROUTER_ROWS, :]
    row = lax.broadcasted_iota(jnp.int32, lt.shape, 0)
    neg_inf = -jnp.inf
    g_log = jnp.where(row < N_GROUPS, lt, neg_inf)
    g_max = jnp.max(g_log, axis=0, keepdims=True)
    g_sel = jnp.min(jnp.where(g_log == g_max, row, LANES), axis=0, keepdims=True)
    g_sum = jnp.sum(jnp.where(row < N_GROUPS, jnp.exp(lt - g_max), 0.0), axis=0, keepdims=True)
    p_group = 1.0 / g_sum
    e_lo = ROUTER_LANE0 + EXPERTS_PER_GROUP * g_sel
    in_group = jnp.logical_and(row >= e_lo, row < e_lo + EXPERTS_PER_GROUP)
    e_log = jnp.where(in_group, lt, neg_inf)
    e_max = jnp.max(e_log, axis=0, keepdims=True)
    top1 = jnp.min(jnp.where(e_log == e_max, row, LANES), axis=0, keepdims=True)
    e_log2 = jnp.where(row == top1, neg_inf, e_log)
    e_max2 = jnp.max(e_log2, axis=0, keepdims=True)
    top2 = jnp.min(jnp.where(e_log2 == e_max2, row, LANES), axis=0, keepdims=True)
    ratio = jnp.exp(e_max2 - e_max)
    w_top1 = p_group / (1.0 + ratio)
    w_top2 = p_group * ratio / (1.0 + ratio)
    row8 = lax.broadcasted_iota(jnp.int32, (8, lt.shape[1]), 0)
    weights = jnp.where(row8 == top1 - e_lo, w_top1, 0.0) + jnp.where(row8 == top2 - e_lo, w_top2, 0.0)
    return g_sel, weights


def _outproj_kernel(ya_ref, yg_ref, x_ref, mod_ref, wo_ref, g2_ref, wrt_ref, brt_ref, striu_ref,
                    x1_ref, row_ref, lpos_ref, cnt_ref):
    tm = MOE_TILE
    subs = range(x_ref.shape[0] // tm)
    rows = lambda t: slice(t * tm, (t + 1) * tm)

    mix = [_dot(ya_ref[rows(t), :], wo_ref[0:ATT_Q, :]) + _dot(yg_ref[rows(t), :], wo_ref[ATT_Q:ATT_Q + GLA_V, :])
           for t in subs]
    h2b = []
    for t in subs:
        x1 = x_ref[rows(t), :] + mod_ref[2:3, :] * mix[t]
        x1_ref[rows(t), :] = x1
        ms = jnp.mean(x1 * x1, axis=-1, keepdims=True)
        h2 = (x1 * lax.rsqrt(ms + EPS) * g2_ref[...]) * (1.0 + mod_ref[4:5, :]) + mod_ref[3:4, :]
        h2b.append(h2.astype(BF16))
    logits_t = [_dot_nt(wrt_ref[...], h2b[t]) + brt_ref[...] for t in subs]

    routed = [_route(logits_t[t]) for t in subs]
    row8 = lax.broadcasted_iota(jnp.int32, (8, tm), 0)
    onehot = [jnp.where(row8 == routed[t][0], 1.0, 0.0) for t in subs]
    before = [_dot(onehot[t].astype(BF16), striu_ref[...]) for t in subs]

    local_row = lax.broadcasted_iota(jnp.int32, (LOCAL_ROWS, tm), 0).astype(F32)
    pad_rows = jnp.zeros((LANES - 8, tm), F32)
    for t in subs:
        count = jnp.sum(onehot[t], axis=1, keepdims=True)
        cnt_ref[t] = jnp.broadcast_to(count, (8, LANES))
        padded = jnp.broadcast_to(jnp.floor((count + (CHUNK - 1.0)) * (1.0 / CHUNK)) * CHUNK, (8, tm))
        start = jnp.zeros((8, tm), F32)
        for shift in range(1, N_GROUPS):
            start = start + jnp.where(row8 >= shift, pltpu.roll(padded, shift, axis=0), 0.0)
        lpos = jnp.sum(onehot[t] * (before[t] + start), axis=0, keepdims=True)
        lpos_ref[rows(t), :] = jnp.broadcast_to(lpos, (LANES, tm)).T
        weights = jnp.concatenate([routed[t][1], pad_rows], axis=0).T
        w_hi = weights.astype(BF16)
        w_lo = (weights - w_hi.astype(F32)).astype(BF16)
        perm = jnp.where(local_row == lpos, 1.0, 0.0).astype(BF16)
        row_ref[t * LOCAL_ROWS:(t + 1) * LOCAL_ROWS, :] = _dot(
            perm, jnp.concatenate([h2b[t], w_hi, w_lo], axis=1)).astype(BF16)


def _out_proj(ya, yg, x2, mod, wo, g2, wr, br, stril, *, T):
    N, D = x2.shape
    subs = OUT_SUBS
    tm = subs * MOE_TILE
    per_b = T // tm
    rows = lambda c: pl.BlockSpec((tm, c), lambda i: (i, 0))
    const = lambda a: pl.BlockSpec(a.shape, lambda i: (0,) * a.ndim)
    return pl.pallas_call(
        _outproj_kernel,
        grid=(N // tm,),
        in_specs=[rows(ATT_Q), rows(GLA_V), rows(D),
                  pl.BlockSpec((None, 6, D), lambda i: (i // per_b, 0, 0)),
                  const(wo), const(g2), const(wr), const(br), const(stril)],
        out_specs=[rows(D), pl.BlockSpec((subs * LOCAL_ROWS, D + 2 * LANES), lambda i: (i, 0)), rows(LANES),
                   pl.BlockSpec((subs, 8, LANES), lambda i: (i, 0, 0))],
        out_shape=[jax.ShapeDtypeStruct((N, D), F32),
                   jax.ShapeDtypeStruct((N // MOE_TILE * LOCAL_ROWS, D + 2 * LANES), BF16),
                   jax.ShapeDtypeStruct((N, LANES), F32),
                   jax.ShapeDtypeStruct((N // MOE_TILE, 8, LANES), F32)],
        compiler_params=_params("arbitrary"),
        name="out_proj",
    )(ya, yg, x2, mod, wo, g2, wr, br, stril)


def _chunk_copy(src_ref, src_chunk, dst_ref, dst_chunk, sem):
    return pltpu.make_async_copy(src_ref.at[src_chunk], dst_ref.at[dst_chunk], sem)


def _moe_kernel(src_ref, dst_ref, grp_ref, next_ref, nt_ref, used_ref, rows_ref, w1f_ref, w3f_ref, w2f_ref, y_ref,
                in_buf, out_buf, zero_buf, st1_ref, st3_ref, st2_ref, w1_ref, w3_ref, w2_ref,
                in_sem, out_sem, zero_sem, w_sem):
    j = pl.program_id(0)
    n_tiles = nt_ref[0]
    d_model = w2_ref.shape[2]

    stages = ((w1f_ref, st1_ref, w1_ref), (w3f_ref, st3_ref, w3_ref), (w2f_ref, st2_ref, w2_ref))

    def fetch_weights(group):
        for hbm, stage, _ in stages:
            pltpu.make_async_copy(hbm.at[group], stage, w_sem).start()

    @pl.when(j == 0)
    def _():
        fetch_weights(grp_ref[0])

    @pl.when(jnp.logical_or(j == 0, grp_ref[j] != grp_ref[jnp.maximum(j - 1, 0)]))
    def _():
        for hbm, stage, dst in stages:
            pltpu.make_async_copy(hbm.at[0], stage, w_sem).wait()
        for hbm, stage, dst in stages:
            for k in range(EXPERTS_PER_GROUP):
                dst[k] = stage[k].astype(BF16)

        @pl.when(next_ref[j] != grp_ref[j])
        def _():
            fetch_weights(next_ref[j])

    def gather(tile, slot):
        def body(k, carry):
            _chunk_copy(rows_ref, src_ref[tile * TILE_CHUNKS + k], in_buf.at[slot], k, in_sem.at[slot]).start()
            return carry
        lax.fori_loop(0, TILE_CHUNKS, body, 0, unroll=True)

    def wait_gather(slot):
        def body(k, carry):
            _chunk_copy(rows_ref, 0, in_buf.at[slot], k, in_sem.at[slot]).wait()
            return carry
        lax.fori_loop(0, TILE_CHUNKS, body, 0, unroll=True)

    def scatter(tile, slot):
        def body(k, carry):
            _chunk_copy(out_buf.at[slot], k, y_ref, dst_ref[tile * TILE_CHUNKS + k], out_sem.at[slot]).start()
            return carry
        lax.fori_loop(0, TILE_CHUNKS, body, 0, unroll=True)

    def wait_scatter(slot):
        def body(k, carry):
            _chunk_copy(out_buf.at[slot], k, y_ref, 0, out_sem.at[slot]).wait()
            return carry
        lax.fori_loop(0, TILE_CHUNKS, body, 0, unroll=True)

    def zero_fill(wait):
        def per_tile(i, carry):
            def body(c, inner):
                copy = _chunk_copy(zero_buf, 0, y_ref, i * LOCAL_CHUNKS + c, zero_sem)
                if wait:
                    copy.wait()
                else:
                    copy.start()
                return inner
            return lax.fori_loop(used_ref[i], LOCAL_CHUNKS, body, carry)
        lax.fori_loop(0, used_ref.shape[0], per_tile, 0)

    @pl.when(j == 0)
    def _():
        zero_buf[...] = jnp.zeros_like(zero_buf)
        scratch0 = used_ref.shape[0] * LOCAL_CHUNKS
        for wait in (False, True):
            for k in range(2 * TILE_CHUNKS):
                copy = _chunk_copy(zero_buf, 0, y_ref, scratch0 + k, zero_sem)
                copy.wait() if wait else copy.start()
        zero_fill(wait=False)
        gather(0, 0)

    @pl.when(j + 1 < n_tiles)
    def _():
        gather(j + 1, (j + 1) % 2)

    @pl.when(j < n_tiles)
    def _():
        slot = j % 2
        wait_gather(slot)
        rows = in_buf[slot].reshape(EXPERT_TILE, in_buf.shape[-1])
        h = rows[:, 0:d_model]
        weights = rows[:, d_model:d_model + LANES].astype(F32) + rows[:, d_model + LANES:].astype(F32)
        experts = range(EXPERTS_PER_GROUP)
        up = [(_dot(h, w1_ref[k]), _dot(h, w3_ref[k])) for k in experts]
        hid = [(a * _sigmoid(a) * g * weights[:, k:k + 1]).astype(BF16) for k, (a, g) in zip(experts, up)]
        y = _dot(hid[0], w2_ref[0])
        for k in experts[1:]:
            y = y + _dot(hid[k], w2_ref[k])

        @pl.when(j >= 2)
        def _():
            wait_scatter(slot)

        out_buf[slot] = y.astype(BF16).reshape(TILE_CHUNKS, CHUNK, d_model)
        scatter(j, slot)

        @pl.when(j == n_tiles - 1)
        def _():
            @pl.when(j >= 1)
            def _():
                wait_scatter(1 - slot)
            wait_scatter(slot)
            zero_fill(wait=True)


def _combine_kernel(x1_ref, mod_ref, lpos_ref, y_ref, o_ref):
    tm = MOE_TILE
    local_row = lax.broadcasted_iota(jnp.int32, (tm, LOCAL_ROWS), 1).astype(F32)
    for t in range(x1_ref.shape[0] // tm):
        rows = slice(t * tm, (t + 1) * tm)
        unsort = jnp.where(local_row == lpos_ref[rows, 0:1], 1.0, 0.0).astype(BF16)
        y = _dot(unsort, y_ref[t * LOCAL_ROWS:(t + 1) * LOCAL_ROWS, :])
        o_ref[rows, :] = x1_ref[rows, :] + mod_ref[5:6, :] * y


def _moe_plan(cnt):
    n_local = cnt.shape[0]
    chunks = (cnt + CHUNK - 1) // CHUNK
    used = jnp.sum(chunks, axis=1)
    local_off = jnp.cumsum(chunks, axis=1) - chunks
    tiles_g = (jnp.sum(chunks, axis=0) + TILE_CHUNKS - 1) // TILE_CHUNKS
    tile_end = jnp.cumsum(tiles_g)
    n_tiles = tile_end[-1]
    group_start = (tile_end - tiles_g) * TILE_CHUNKS
    seg_len = chunks.T.reshape(-1)
    seg_start = (group_start[:, None] + (jnp.cumsum(chunks, axis=0) - chunks).T).reshape(-1)
    seg_src = (jnp.arange(n_local)[None, :] * LOCAL_CHUNKS + local_off.T).reshape(-1)
    max_chunks = n_local * MOE_TILE // CHUNK + n_local * N_GROUPS + N_GROUPS * TILE_CHUNKS
    max_tiles = (max_chunks + TILE_CHUNKS - 1) // TILE_CHUNKS
    c = jnp.arange(max_tiles * TILE_CHUNKS)[:, None]
    within = c - seg_start[None, :]
    hit = jnp.logical_and(within >= 0, within < seg_len[None, :])
    valid = jnp.any(hit, axis=1)
    src = jnp.sum(jnp.where(hit, seg_src[None, :] + within, 0), axis=1)
    src = jnp.where(valid, src, LOCAL_CHUNKS - 1)
    slot_k = c[:, 0] % (2 * TILE_CHUNKS)
    dst = jnp.where(valid, src, n_local * LOCAL_CHUNKS + slot_k)
    j = jnp.minimum(jnp.arange(max_tiles), n_tiles - 1)
    grp = jnp.sum(j[:, None] >= tile_end[None, :], axis=1)
    gid = jnp.arange(N_GROUPS)
    later = jnp.where(jnp.logical_and(gid[None, :] > gid[:, None], tiles_g[None, :] > 0), gid[None, :], N_GROUPS)
    next_of = jnp.min(later, axis=1)
    next_of = jnp.where(next_of == N_GROUPS, gid, next_of)
    next_grp = jnp.sum(jnp.where(grp[:, None] == gid[None, :], next_of[None, :], 0), axis=1)
    i32 = lambda a: a.astype(jnp.int32)
    return i32(src), i32(dst), i32(grp), i32(next_grp), i32(n_tiles).reshape(1), i32(used)


def _moe(plan, rows_local, w1g, w3g, w2g):
    src, dst, grp, next_grp, n_tiles, used = plan
    D = w2g.shape[3]
    n_rows, cols = rows_local.shape
    n_chunks = n_rows // CHUNK
    hbm = pl.BlockSpec(memory_space=pl.ANY)
    group_weights = [w1g, w3g, w2g]
    y = pl.pallas_call(
        _moe_kernel,
        grid_spec=pltpu.PrefetchScalarGridSpec(
            num_scalar_prefetch=6,
            grid=(grp.shape[0],),
            in_specs=[hbm, hbm, hbm, hbm],
            out_specs=hbm,
            scratch_shapes=([pltpu.VMEM((2, TILE_CHUNKS, CHUNK, cols), BF16),
                             pltpu.VMEM((2, TILE_CHUNKS, CHUNK, D), BF16),
                             pltpu.VMEM((1, CHUNK, D), BF16)]
                            + [pltpu.VMEM(w.shape[1:], F32) for w in group_weights]
                            + [pltpu.VMEM(w.shape[1:], BF16) for w in group_weights]
                            + [pltpu.SemaphoreType.DMA((2,)), pltpu.SemaphoreType.DMA((2,)),
                               pltpu.SemaphoreType.DMA(()), pltpu.SemaphoreType.DMA(())])),
        out_shape=jax.ShapeDtypeStruct((n_chunks + 2 * TILE_CHUNKS, CHUNK, D), BF16),
        compiler_params=_params("arbitrary"),
        name="moe",
    )(src, dst, grp, next_grp, n_tiles, used, rows_local.reshape(n_chunks, CHUNK, cols), w1g, w3g, w2g)
    return y.reshape((n_chunks + 2 * TILE_CHUNKS) * CHUNK, D)


def _combine(x1, mod, lpos, y_local, *, T):
    N, D = x1.shape
    subs = COMBINE_SUBS
    tm = subs * MOE_TILE
    per_b = T // tm
    return pl.pallas_call(
        _combine_kernel,
        grid=(N // tm,),
        in_specs=[pl.BlockSpec((tm, D), lambda i: (i, 0)),
                  pl.BlockSpec((None, 6, D), lambda i: (i // per_b, 0, 0)),
                  pl.BlockSpec((tm, LANES), lambda i: (i, 0)),
                  pl.BlockSpec((subs * LOCAL_ROWS, D), lambda i: (i, 0))],
        out_specs=pl.BlockSpec((tm, D), lambda i: (i, 0)),
        out_shape=jax.ShapeDtypeStruct((N, D), F32),
        compiler_params=_params("arbitrary"),
        name="moe_combine",
    )(x1, mod, lpos, y_local)


def _block_diag(n, blk, value, dtype):
    r = np.arange(n)[:, None] // blk
    c = np.arange(n)[None, :] // blk
    return jnp.asarray(np.where(r == c, value, 0.0), dtype)


def _gla_constants():
    L = GLA_CHUNK
    i = np.arange(L)[:, None]
    j = np.arange(L)[None, :]
    tri = j <= i
    bdtri = np.logical_and(j <= i, i // GLA_SUB == j // GLA_SUB)
    cum = np.block([[tri, tri], [bdtri, bdtri]])
    col = np.arange(N_SUB * LANES)[None, :]
    qmask = np.stack([np.logical_and(col // LANES == i // GLA_SUB, (col % LANES) // GLA_DK == hh)
                      for hh in range(2)])
    bdms = _block_diag(2 * GLA_DV, GLA_DV, 1.0 / GLA_DV, BF16)
    d = np.arange(LANES)[:, None] // GLA_DK
    e = np.arange(2 * GLA_DV)[None, :] // GLA_DV
    smask = d == e
    return (jnp.asarray(cum, BF16), jnp.asarray(qmask, BF16), bdms, jnp.asarray(smask, F32))


def kernel(x, c, w_ada, b_ada, g_norm1, w_in, q_norm, k_norm, sinks, w_gk2, b_gk, g_gla_out, g_att_out,
           w_out, g_norm2, w_group, b_group, w_router, b_router, w1, w3, w2):
    B, T, D = x.shape
    N = B * T
    depth = w_ada.shape[0]
    cum, qmask, bdms, smask = _gla_constants()
    bdq = _block_diag(2 * LANES, HEAD_DIM, 1.0 / HEAD_DIM, BF16)
    bdk = _block_diag(ATT_KV, HEAD_DIM, 1.0 / HEAD_DIM, BF16)

    x2 = x.reshape(N, D)
    for l in range(depth):
        mod = _adaln_mod(c, w_ada[l], b_ada[l]).reshape(B, 6, D)

        wgk =jnp.concatenate([w_gk2[l], jnp.zeros((LANES - GLA_RANK, GLA_K), F32)], axis=0).astype(BF16)
        qa, ka, va, qg, kg, vg, la, og = _in_proj(
            x2, mod, g_norm1[l].reshape(1, D), w_in[l].T,
            jnp.tile(q_norm[l], ATT_HEADS).reshape(1, ATT_Q), jnp.tile(k_norm[l], ATT_KV_HEADS).reshape(1, ATT_KV),
            bdq, bdk, wgk, b_gk[l].reshape(1, GLA_K), T=T, tm=PROJ_TILE)

        y_att, y_gla = _mixers(sinks[l], qa, ka, va, g_att_out[l].reshape(1, ATT_Q),
                               qg, kg, vg, la, og, cum, qmask, bdms, smask,
                               jnp.tile(g_gla_out[l], GLA_HEADS).reshape(1, GLA_V), B=B, T=T)

        pad = LANES - N_GROUPS - N_EXPERTS
        wr_t = jnp.concatenate([w_group[l], w_router[l], jnp.zeros((D, pad), F32)], axis=1).T.astype(BF16)
        br_t = jnp.concatenate([b_group[l], b_router[l], jnp.zeros((pad,), F32)]).reshape(LANES, 1)
        striu = jnp.asarray(np.arange(MOE_TILE)[:, None] < np.arange(MOE_TILE)[None, :], BF16)
        x1, rows_local, lpos, cnt = _out_proj(y_att, y_gla, x2, mod, w_out[l].astype(BF16),
                                              g_norm2[l].reshape(1, D), wr_t, br_t, striu, T=T)
        plan = _moe_plan(cnt[:, :N_GROUPS, 0].astype(jnp.int32))
        by_group = lambda w: w.reshape((N_GROUPS, EXPERTS_PER_GROUP) + w.shape[1:])
        y_local = _moe(plan, rows_local, by_group(w1[l]), by_group(w3[l]), by_group(w2[l]))
        x2 = _combine(x1, mod, lpos, y_local, T=T)
    return x2.reshape(B, T, D)
```

```python
import jax
import jax.numpy as jnp
import numpy as np
from jax import lax
from jax.experimental import pallas as pl
from jax.experimental.pallas import tpu as pltpu

F32 = jnp.float32
BF16 = jnp.bfloat16

EPS = 1e-6
ATT_HEADS = 8
ATT_KV_HEADS = 2
HEAD_DIM = 64
WINDOW = 128
ATT_Q = ATT_HEADS * HEAD_DIM
ATT_KV = ATT_KV_HEADS * HEAD_DIM
GLA_HEADS = 4
GLA_DK = 64
GLA_DV = 128
GLA_RANK = 16
GLA_NORMALIZER = 16.0
GLA_K = GLA_HEADS * GLA_DK
GLA_V = GLA_HEADS * GLA_DV
N_GROUPS = 4
EXPERTS_PER_GROUP = 4
N_EXPERTS = N_GROUPS * EXPERTS_PER_GROUP

LANES = 128
PROJ_SPLITS = (512, 512)
PROJ_TILE = sum(PROJ_SPLITS)
OUT_SUBS = 4
COMBINE_SUBS = 4
ROUTER_ROWS = 24
SEQS_PER_STEP = 8
GLA_CHUNK = 128
GLA_SUB = 16
N_SUB = GLA_CHUNK // GLA_SUB
ROUTER_LANE0 = N_GROUPS
VMEM_LIMIT = 56 * 1024 * 1024
MOE_TILE = 256
EXPERT_TILE = 256
CHUNK = 16
TILE_CHUNKS = EXPERT_TILE // CHUNK
LOCAL_CHUNKS = (MOE_TILE + N_GROUPS * (CHUNK - 1)) // CHUNK + 2
LOCAL_ROWS = LOCAL_CHUNKS * CHUNK

_QA0, _KA0, _VA0 = 0, ATT_Q, ATT_Q + ATT_KV
_QG0 = _VA0 + ATT_KV
_KG0 = _QG0 + GLA_K
_VG0 = _KG0 + GLA_K
_OG0 = _VG0 + GLA_V
_LR0 = _OG0 + GLA_V
IN_COLS_PAD = _LR0 + LANES


def _dot(a, b):
    return jnp.dot(a, b, preferred_element_type=F32)


def _dot_nt(a, b):
    return lax.dot_general(a, b, (((1,), (1,)), ((), ())), preferred_element_type=F32)


def _sigmoid(x):
    return 1.0 / (1.0 + jnp.exp(-x))


def _params(*sem):
    return pltpu.CompilerParams(dimension_semantics=sem, vmem_limit_bytes=VMEM_LIMIT)


def _mod_kernel(c_ref, w_ref, b_ref, o_ref):
    c = c_ref[...]
    s = (c * _sigmoid(c)).astype(BF16)
    o_ref[...] = _dot(s, w_ref[...].astype(BF16)) + b_ref[...]


def _adaln_mod(c, w_ada, b_ada):
    B, D = c.shape
    n = w_ada.shape[1]
    tn = 1536
    return pl.pallas_call(
        _mod_kernel,
        grid=(n // tn,),
        in_specs=[pl.BlockSpec((B, D), lambda j: (0, 0)),
                  pl.BlockSpec((D, tn), lambda j: (0, j)),
                  pl.BlockSpec((1, tn), lambda j: (0, j))],
        out_specs=pl.BlockSpec((B, tn), lambda j: (0, j)),
        out_shape=jax.ShapeDtypeStruct((B, n), F32),
        compiler_params=_params("arbitrary"),
        name="adaln_mod",
    )(c, w_ada, b_ada.reshape(1, n))


def _inproj_kernel(x_ref, mod_ref, g1_ref, wf_ref, qn_ref, kn_ref, bdq_ref, bdk_ref, wgk_ref, bgk_ref,
                   qa_ref, ka_ref, va_ref, qg_ref, kg_ref, vg_ref, la_ref, og_ref, w_ref):
    subs = range(len(PROJ_SPLITS))
    starts = [sum(PROJ_SPLITS[:t]) for t in subs]
    rows = lambda t: slice(starts[t], starts[t] + PROJ_SPLITS[t])

    @pl.when(pl.program_id(0) == 0)
    def _():
        lr_src = _OG0
        w_ref[0:_OG0, :] = wf_ref[0:_OG0, :].astype(BF16)
        w_ref[_OG0:_LR0, :] = wf_ref[lr_src + GLA_RANK:lr_src + GLA_RANK + GLA_V, :].astype(BF16)
        w_ref[_LR0:_LR0 + GLA_RANK, :] = wf_ref[lr_src:lr_src + GLA_RANK, :].astype(BF16)
        w_ref[_LR0 + GLA_RANK:IN_COLS_PAD, :] = jnp.zeros((LANES - GLA_RANK, wf_ref.shape[1]), BF16)

    h = []
    for t in subs:
        x = x_ref[rows(t), :]
        ms = jnp.mean(x * x, axis=-1, keepdims=True)
        xn = x * lax.rsqrt(ms + EPS) * g1_ref[...]
        h.append((xn * (1.0 + mod_ref[1:2, :]) + mod_ref[0:1, :]).astype(BF16))

    proj = lambda t, c0, width: _dot_nt(h[t], w_ref[c0:c0 + width, :])
    gate_of = lambda lr: _dot(lr.astype(BF16), wgk_ref[...]) + bgk_ref[...]
    qa, kv, q_ms, k_ms, qk_g, vg, og, lr, gate = ({} for _ in range(9))
    for t in subs:
        qa[t] = proj(t, _QA0, ATT_Q)
        if t > 0:
            gate[t - 1] = gate_of(lr[t - 1])
        kv[t] = proj(t, _KA0, 2 * ATT_KV)
        q_sq = (qa[t] * qa[t]).astype(BF16)
        q_ms[t] = jnp.concatenate([_dot(q_sq[:, c:c + 2 * LANES], bdq_ref[...])
                                   for c in range(0, ATT_Q, 2 * LANES)], axis=1)
        qk_g[t] = proj(t, _QG0, 2 * GLA_K)
        k = kv[t][:, 0:ATT_KV]
        k_ms[t] = _dot((k * k).astype(BF16), bdk_ref[...])
        vg[t] = proj(t, _VG0, GLA_V)
        og[t] = proj(t, _OG0, GLA_V)
        lr[t] = proj(t, _LR0, LANES)
    gate[subs[-1]] = gate_of(lr[subs[-1]])

    for t in subs:
        low = lax.broadcasted_iota(jnp.int32, (PROJ_SPLITS[t], ATT_KV), 1) < HEAD_DIM
        qa_ref[rows(t), :] = (qa[t] * lax.rsqrt(q_ms[t] + EPS) * qn_ref[...] * (HEAD_DIM ** -0.5)).astype(BF16)
        k = kv[t][:, 0:ATT_KV] * lax.rsqrt(k_ms[t] + EPS) * kn_ref[...]
        v = kv[t][:, ATT_KV:2 * ATT_KV]
        for src, dst in ((k, ka_ref), (v, va_ref)):
            swapped = pltpu.roll(src, HEAD_DIM, axis=1)
            dst[rows(t), 0:LANES] = jnp.where(low, src, swapped).astype(BF16)
            dst[rows(t), LANES:2 * LANES] = jnp.where(low, swapped, src).astype(BF16)
        qg_ref[rows(t), :] = (qk_g[t][:, 0:GLA_K] * (GLA_DK ** -0.5)).astype(BF16)
        kg_ref[rows(t), :] = qk_g[t][:, GLA_K:2 * GLA_K].astype(BF16)
        vg_ref[rows(t), :] = vg[t].astype(BF16)
        og_ref[rows(t), :] = (og[t] * _sigmoid(og[t])).astype(BF16)
        log_sig = jnp.minimum(gate[t], 0.0) - jnp.log(1.0 + jnp.exp(-jnp.abs(gate[t])))
        la_ref[rows(t), :] = log_sig * (1.0 / GLA_NORMALIZER)


def _in_proj(x2, mod, g1, w_in_t, qn, kn, bdq, bdk, wgk, bgk, *, T, tm):
    N, D = x2.shape
    per_b = T // tm
    const = lambda shape: pl.BlockSpec(shape, lambda i: (0,) * len(shape))
    rows = lambda c: pl.BlockSpec((tm, c), lambda i: (i, 0))
    outs = [(ATT_Q, BF16), (2 * ATT_KV, BF16), (2 * ATT_KV, BF16), (GLA_K, BF16), (GLA_K, BF16),
            (GLA_V, BF16), (GLA_K, F32), (GLA_V, BF16)]
    return pl.pallas_call(
        _inproj_kernel,
        grid=(N // tm,),
        in_specs=[rows(D),
                  pl.BlockSpec((None, 6, D), lambda i: (i // per_b, 0, 0)),
                  const((1, D)),
                  pl.BlockSpec(w_in_t.shape, lambda i: (0, 0), pipeline_mode=pl.Buffered(1)),
                  const(qn.shape), const(kn.shape),
                  const(bdq.shape), const(bdk.shape), const(wgk.shape), const(bgk.shape)],
        out_specs=[rows(c) for c, _ in outs],
        out_shape=[jax.ShapeDtypeStruct((N, c), dt) for c, dt in outs],
        scratch_shapes=[pltpu.VMEM((IN_COLS_PAD, D), BF16)],
        compiler_params=_params("arbitrary"),
        name="in_proj",
    )(x2, mod, g1, w_in_t, qn, kn, bdq, bdk, wgk, bgk)


def _attn_stages(seqs, sinks_ref, q_ref, kc_ref, vc_ref, gatt_ref, o_ref, kp_ref, vp_ref):
    blk = WINDOW
    first = pl.program_id(1) == 0
    qi = lax.broadcasted_iota(jnp.int32, (blk, blk), 0)
    cj = lax.broadcasted_iota(jnp.int32, (blk, blk), 1)
    from_prev = cj > qi
    dist = (qi - cj + jnp.where(from_prev, blk, 0)).astype(F32)
    no_prev = jnp.where(jnp.logical_and(from_prev, first), -1e30, 0.0)
    low = cj < HEAD_DIM
    half = (jnp.where(low, 1.0, 0.0).astype(BF16), jnp.where(low, 0.0, 1.0).astype(BF16))
    half2 = tuple(jnp.concatenate([m, m], axis=0) for m in half)
    prev_mask = jnp.where(from_prev, 1.0, 0.0).astype(BF16)
    cur_mask = jnp.where(from_prev, 0.0, 1.0).astype(BF16)

    n_pairs = ATT_HEADS // 2
    pairs_per_kv = n_pairs // ATT_KV_HEADS
    units = [(bi, j) for bi in seqs for j in range(n_pairs)]

    def kv_blocks(bi, g):
        lanes = slice(g * LANES, (g + 1) * LANES)
        return (kp_ref[bi, :, lanes], vp_ref[bi, :, lanes]), (kc_ref[bi, :, lanes], vc_ref[bi, :, lanes])

    scores = {}
    for bi, j in units:
        (kp, _), (kc, _) = kv_blocks(bi, j // pairs_per_kv)
        k_both = jnp.concatenate([kp, kc], axis=0)
        qp = q_ref[bi, :, j * LANES:(j + 1) * LANES]
        for p in range(2):
            scores[bi, j, p] = _dot_nt(qp, k_both * half2[p])
    yield

    probs, sink_terms = {}, {}
    for bi, j in units:
        for p in range(2):
            h = 2 * j + p
            slope = 2.0 ** (-8.0 * (h + 1) / ATT_HEADS)
            s_both = scores[bi, j, p]
            s = jnp.where(from_prev, s_both[:, 0:blk], s_both[:, blk:2 * blk]) - slope * dist + no_prev
            sink = sinks_ref[h]
            m = jnp.maximum(jnp.max(s, axis=-1, keepdims=True), sink)
            probs[bi, j, p] = jnp.exp(s - m)
            sink_terms[bi, j, p] = jnp.exp(sink - m)
    yield

    outs = {}
    for bi, j in units:
        (_, vp), (_, vc) = kv_blocks(bi, j // pairs_per_kv)
        v_stack = jnp.concatenate([jnp.concatenate([v * half[p], half[p]], axis=1)
                                   for v in (vp, vc) for p in range(2)], axis=0)
        e = [probs[bi, j, p].astype(BF16) for p in range(2)]
        p_all = jnp.concatenate([x * m for m in (prev_mask, cur_mask) for x in e], axis=1)
        pv = _dot(p_all, v_stack)
        den = pv[:, LANES:2 * LANES] + jnp.where(low, sink_terms[bi, j, 0], sink_terms[bi, j, 1])
        outs[bi, j] = pv[:, 0:LANES] / den
    yield

    for bi in seqs:
        o = jnp.concatenate([outs[bi, j] for j in range(n_pairs)], axis=1)
        ms = jnp.mean(o * o, axis=-1, keepdims=True)
        o_ref[bi] = (o * lax.rsqrt(ms + EPS) * gatt_ref[...]).astype(BF16)
        kp_ref[bi] = kc_ref[bi]
        vp_ref[bi] = vc_ref[bi]


def _gla_stages(seqs, q_ref, k_ref, v_ref, la_ref, og_ref, cum_ref, qmask_ref, bdms_ref, smask_ref,
                ggla_ref, o_ref, state_ref):
    L = GLA_CHUNK
    rows = lax.broadcasted_iota(jnp.int32, (L, LANES), 0)
    causal = lax.broadcasted_iota(jnp.int32, (L, L), 0) >= lax.broadcasted_iota(jnp.int32, (L, L), 1)
    units = [(s, pair) for s in seqs for pair in range(GLA_HEADS // 2)]
    kl = lambda pair: slice(pair * LANES, (pair + 1) * LANES)
    vl = lambda pair: slice(pair * 2 * GLA_DV, (pair + 1) * 2 * GLA_DV)

    b, b_in = {}, {}
    for s in seqs:
        la = la_ref[s]
        la_hi = la.astype(BF16)
        la_lo = (la - la_hi.astype(F32)).astype(BF16)
        sums = _dot(cum_ref[...], jnp.concatenate([la_hi, la_lo], axis=0))
        b[s] = sums[0:L, :]
        b_in[s] = sums[L:2 * L, :]
    yield

    q_both, keys, q_dec, k_dec_t, b_last = {}, {}, {}, {}, {}
    for s in seqs:
        ref = b[s] - b_in[s]
        b_last[s] = b[s][L - 1:L, :]
        q = q_ref[s].astype(F32)
        k = k_ref[s].astype(F32)
        q_in = (q * jnp.exp(b_in[s])).astype(BF16)
        q_dec[s] = (q * jnp.exp(b[s])).astype(BF16)
        k_dec = k * jnp.exp(b_last[s] - b[s])
        for pair in range(GLA_HEADS // 2):
            k_p, b_p, ref_p = k[:, kl(pair)], b[s][:, kl(pair)], ref[:, kl(pair)]
            expanded = []
            for g in range(N_SUB):
                last_row = (g + 1) * GLA_SUB - 1
                expo = jnp.where(rows <= last_row, ref_p[g * GLA_SUB:g * GLA_SUB + 1, :] - b_p, -1e4)
                expanded.append((k_p * jnp.exp(expo)).astype(BF16))
            keys[s, pair] = jnp.concatenate(expanded, axis=1)
            q_rep = jnp.concatenate([q_in[:, kl(pair)]] * N_SUB, axis=1)
            q_both[s, pair] = jnp.concatenate([q_rep * qmask_ref[0], q_rep * qmask_ref[1]], axis=0)
            k_dec_t[s, pair] = k_dec[:, kl(pair)].T.astype(BF16)
    yield

    scores = {u: _dot_nt(q_both[u], keys[u]) for u in units}
    yield

    outs, updates = {}, {}
    for s, pair in units:
        v_p = v_ref[s, :, vl(pair)]
        o_parts = []
        for hh in range(2):
            a = jnp.where(causal, scores[s, pair][hh * L:(hh + 1) * L, :], 0.0).astype(BF16)
            o_parts.append(_dot(a, v_p[:, hh * GLA_DV:(hh + 1) * GLA_DV]))
        state = state_ref[s, pair]
        outs[s, pair] = jnp.concatenate(o_parts, axis=1) + _dot(q_dec[s][:, kl(pair)], state.astype(BF16))
        updates[s, pair] = _dot(k_dec_t[s, pair], v_p)
    yield

    for s, pair in units:
        decay =jnp.broadcast_to(jnp.exp(b_last[s][:, kl(pair)]), (LANES, LANES)).T
        state_ref[s, pair] = (state_ref[s, pair] * jnp.concatenate([decay, decay], axis=1)
                              + updates[s, pair] * smask_ref[...])
        o = outs[s, pair]
        ms = _dot((o * o).astype(BF16), bdms_ref[...])
        y = o * lax.rsqrt(ms + EPS) * ggla_ref[:, vl(pair)] * og_ref[s, :, vl(pair)].astype(F32)
        o_ref[s, :, vl(pair)] = y.astype(BF16)


def _mixers_kernel(sinks_ref, qa_ref, ka_ref, va_ref, gatt_ref,
                   qg_ref, kg_ref, vg_ref, la_ref, og_ref, cum_ref, qmask_ref, bdms_ref, smask_ref, ggla_ref,
                   ya_ref, yg_ref, kp_ref, vp_ref, state_ref):
    @pl.when(pl.program_id(1) == 0)
    def _():
        state_ref[...] = jnp.zeros_like(state_ref)
        kp_ref[...] = jnp.zeros_like(kp_ref)
        vp_ref[...] = jnp.zeros_like(vp_ref)

    n_seq = qa_ref.shape[0]
    waiting = []
    for group in (tuple(range(0, n_seq // 2)), tuple(range(n_seq // 2, n_seq))):
        waiting.append([
            _attn_stages(group, sinks_ref, qa_ref, ka_ref, va_ref, gatt_ref, ya_ref, kp_ref, vp_ref),
            _gla_stages(group, qg_ref, kg_ref, vg_ref, la_ref, og_ref, cum_ref, qmask_ref, bdms_ref, smask_ref,
                        ggla_ref, yg_ref, state_ref)])
    pending = []
    while pending or waiting:
        if waiting:
            pending.extend(waiting.pop(0))
        for stages in list(pending):
            if next(stages, "done") == "done":
                pending.remove(stages)


def _mixers(sinks, qa, ka, va, gatt, qg, kg, vg, la, og, cum, qmask, bdms, smask, ggla, *, B, T):
    assert GLA_CHUNK == WINDOW
    L = WINDOW
    nb = SEQS_PER_STEP
    seq = lambda a: a.reshape(B, T, a.shape[-1])
    rows = lambda c: pl.BlockSpec((nb, L, c), lambda b, i: (b, i, 0))
    const = lambda a: pl.BlockSpec(a.shape, lambda b, i: (0,) * a.ndim)
    y_att, y_gla = pl.pallas_call(
        _mixers_kernel,
        grid=(B // nb, T // L),
        in_specs=[pl.BlockSpec(memory_space=pltpu.SMEM),
                  rows(ATT_Q), rows(2 * ATT_KV), rows(2 * ATT_KV), const(gatt),
                  rows(GLA_K), rows(GLA_K), rows(GLA_V), rows(GLA_K), rows(GLA_V),
                  const(cum), const(qmask), const(bdms), const(smask), const(ggla)],
        out_specs=[rows(ATT_Q), rows(GLA_V)],
        out_shape=[jax.ShapeDtypeStruct((B, T, ATT_Q), BF16), jax.ShapeDtypeStruct((B, T, GLA_V), BF16)],
        scratch_shapes=[pltpu.VMEM((nb, L, 2 * ATT_KV), BF16), pltpu.VMEM((nb, L, 2 * ATT_KV), BF16),
                        pltpu.VMEM((nb, GLA_HEADS // 2, LANES, 2 * GLA_DV), F32)],
        compiler_params=_params("arbitrary", "arbitrary"),
        name="mixers",
    )(sinks, seq(qa), seq(ka), seq(va), gatt, seq(qg), seq(kg), seq(vg), seq(la), seq(og),
      cum, qmask, bdms, smask, ggla)
    return y_att.reshape(B * T, ATT_Q), y_gla.reshape(B * T, GLA_V)


def _route(logits_t):
    lt = logits_t[0:ROUTER_ROWS, :]
    row = lax.broadcasted_iota(jnp.int32, lt.shape, 0)
    neg_inf = -jnp.inf
    g_log = jnp.where(row < N_GROUPS, lt, neg_inf)
    g_max = jnp.max(g_log, axis=0, keepdims=True)
    g_sel = jnp.min(jnp.where(g_log == g_max, row, LANES), axis=0, keepdims=True)
    g_sum = jnp.sum(jnp.where(row < N_GROUPS, jnp.exp(lt - g_max), 0.0), axis=0, keepdims=True)
    p_group = 1.0 / g_sum
    e_lo = ROUTER_LANE0 + EXPERTS_PER_GROUP * g_sel
    in_group = jnp.logical_and(row >= e_lo, row < e_lo + EXPERTS_PER_GROUP)
    e_log = jnp.where(in_group, lt, neg_inf)
    e_max = jnp.max(e_log, axis=0, keepdims=True)
    top1 = jnp.min(jnp.where(e_log == e_max, row, LANES), axis=0, keepdims=True)
    e_log2 = jnp.where(row == top1, neg_inf, e_log)
    e_max2 = jnp.max(e_log2, axis=0, keepdims=True)
    top2 = jnp.min(jnp.where(e_log2 == e_max2, row, LANES), axis=0, keepdims=True)
    ratio = jnp.exp(e_max2 - e_max)
    w_top1 = p_group / (1.0 + ratio)
    w_top2 = p_group * ratio / (1.0 + ratio)
    row8 = lax.broadcasted_iota(jnp.int32, (8, lt.shape[1]), 0)
    weights = jnp.where(row8 == top1 - e_lo, w_top1, 0.0) + jnp.where(row8 == top2 - e_lo, w_top2, 0.0)
    return g_sel, weights


def _outproj_kernel(ya_ref, yg_ref, x_ref, mod_ref, wo_ref, g2_ref, wrt_ref, brt_ref, striu_ref,
                    x1_ref, row_ref, lpos_ref, cnt_ref):
    tm = MOE_TILE
    subs = range(x_ref.shape[0] // tm)
    rows = lambda t: slice(t * tm, (t + 1) * tm)

    mix = [_dot(ya_ref[rows(t), :], wo_ref[0:ATT_Q, :]) + _dot(yg_ref[rows(t), :], wo_ref[ATT_Q:ATT_Q + GLA_V, :])
           for t in subs]
    h2b = []
    for t in subs:
        x1 = x_ref[rows(t), :] + mod_ref[2:3, :] * mix[t]
        x1_ref[rows(t), :] = x1
        ms = jnp.mean(x1 * x1, axis=-1, keepdims=True)
        h2 = (x1 * lax.rsqrt(ms + EPS) * g2_ref[...]) * (1.0 + mod_ref[4:5, :]) + mod_ref[3:4, :]
        h2b.append(h2.astype(BF16))
    logits_t = [_dot_nt(wrt_ref[...], h2b[t]) + brt_ref[...] for t in subs]

    routed = [_route(logits_t[t]) for t in subs]
    row8 = lax.broadcasted_iota(jnp.int32, (8, tm), 0)
    onehot = [jnp.where(row8 == routed[t][0], 1.0, 0.0) for t in subs]
    before = [_dot(onehot[t].astype(BF16), striu_ref[...]) for t in subs]

    local_row = lax.broadcasted_iota(jnp.int32, (LOCAL_ROWS, tm), 0).astype(F32)
    pad_rows = jnp.zeros((LANES - 8, tm), F32)
    for t in subs:
        count = jnp.sum(onehot[t], axis=1, keepdims=True)
        cnt_ref[t] = jnp.broadcast_to(count, (8, LANES))
        padded = jnp.broadcast_to(jnp.floor((count + (CHUNK - 1.0)) * (1.0 / CHUNK)) * CHUNK, (8, tm))
        start = jnp.zeros((8, tm), F32)
        for shift in range(1, N_GROUPS):
            start = start + jnp.where(row8 >= shift, pltpu.roll(padded, shift, axis=0), 0.0)
        lpos = jnp.sum(onehot[t] * (before[t] + start), axis=0, keepdims=True)
        lpos_ref[t] = jnp.broadcast_to(lpos, (8, tm))
        weights = jnp.concatenate([routed[t][1], pad_rows], axis=0).T
        w_hi = weights.astype(BF16)
        w_lo = (weights - w_hi.astype(F32)).astype(BF16)
        perm = jnp.where(local_row == lpos, 1.0, 0.0).astype(BF16)
        row_ref[t * LOCAL_ROWS:(t + 1) * LOCAL_ROWS, :] = _dot(
            perm, jnp.concatenate([h2b[t], w_hi, w_lo], axis=1)).astype(BF16)


def _out_proj(ya, yg, x2, mod, wo, g2, wr, br, stril, *, T):
    N, D = x2.shape
    subs = OUT_SUBS
    tm = subs * MOE_TILE
    per_b = T // tm
    rows = lambda c: pl.BlockSpec((tm, c), lambda i: (i, 0))
    const = lambda a: pl.BlockSpec(a.shape, lambda i: (0,) * a.ndim)
    return pl.pallas_call(
        _outproj_kernel,
        grid=(N // tm,),
        in_specs=[rows(ATT_Q), rows(GLA_V), rows(D),
                  pl.BlockSpec((None, 6, D), lambda i: (i // per_b, 0, 0)),
                  const(wo), const(g2), const(wr), const(br), const(stril)],
        out_specs=[rows(D), pl.BlockSpec((subs * LOCAL_ROWS, D + 2 * LANES), lambda i: (i, 0)),
                   pl.BlockSpec((subs, 8, MOE_TILE), lambda i: (i, 0, 0)),
                   pl.BlockSpec((subs, 8, LANES), lambda i: (i, 0, 0))],
        out_shape=[jax.ShapeDtypeStruct((N, D), F32),
                   jax.ShapeDtypeStruct((N // MOE_TILE * LOCAL_ROWS, D + 2 * LANES), BF16),
                   jax.ShapeDtypeStruct((N // MOE_TILE, 8, MOE_TILE), F32),
                   jax.ShapeDtypeStruct((N // MOE_TILE, 8, LANES), F32)],
        compiler_params=_params("arbitrary"),
        name="out_proj",
    )(ya, yg, x2, mod, wo, g2, wr, br, stril)


def _chunk_copy(src_ref, src_chunk, dst_ref, dst_chunk, sem):
    return pltpu.make_async_copy(src_ref.at[src_chunk], dst_ref.at[dst_chunk], sem)


def _moe_kernel(src_ref, dst_ref, grp_ref, next_ref, nt_ref, used_ref, rows_ref, w1f_ref, w3f_ref, w2f_ref, y_ref,
                in_buf, out_buf, zero_buf, st1_ref, st3_ref, st2_ref, w1_ref, w3_ref, w2_ref,
                in_sem, out_sem, zero_sem, w_sem):
    j = pl.program_id(0)
    n_tiles = nt_ref[0]
    d_model = w2_ref.shape[2]

    stages = ((w1f_ref, st1_ref, w1_ref), (w3f_ref, st3_ref, w3_ref), (w2f_ref, st2_ref, w2_ref))

    def fetch_weights(group):
        for hbm, stage, _ in stages:
            pltpu.make_async_copy(hbm.at[group], stage, w_sem).start()

    @pl.when(j == 0)
    def _():
        fetch_weights(grp_ref[0])

    @pl.when(jnp.logical_or(j == 0, grp_ref[j] != grp_ref[jnp.maximum(j - 1, 0)]))
    def _():
        for hbm, stage, dst in stages:
            pltpu.make_async_copy(hbm.at[0], stage, w_sem).wait()
        for hbm, stage, dst in stages:
            for k in range(EXPERTS_PER_GROUP):
                dst[k] = stage[k].astype(BF16)

        @pl.when(next_ref[j] != grp_ref[j])
        def _():
            fetch_weights(next_ref[j])

    def gather(tile, slot):
        def body(k, carry):
            _chunk_copy(rows_ref, src_ref[tile * TILE_CHUNKS + k], in_buf.at[slot], k, in_sem.at[slot]).start()
            return carry
        lax.fori_loop(0, TILE_CHUNKS, body, 0, unroll=True)

    def wait_gather(slot):
        def body(k, carry):
            _chunk_copy(rows_ref, 0, in_buf.at[slot], k, in_sem.at[slot]).wait()
            return carry
        lax.fori_loop(0, TILE_CHUNKS, body, 0, unroll=True)

    def scatter(tile, slot):
        def body(k, carry):
            _chunk_copy(out_buf.at[slot], k, y_ref, dst_ref[tile * TILE_CHUNKS + k], out_sem.at[slot]).start()
            return carry
        lax.fori_loop(0, TILE_CHUNKS, body, 0, unroll=True)

    def wait_scatter(slot):
        def body(k, carry):
            _chunk_copy(out_buf.at[slot], k, y_ref, 0, out_sem.at[slot]).wait()
            return carry
        lax.fori_loop(0, TILE_CHUNKS, body, 0, unroll=True)

    def zero_fill(wait):
        def per_tile(i, carry):
            def body(c, inner):
                copy = _chunk_copy(zero_buf, 0, y_ref, i * LOCAL_CHUNKS + c, zero_sem)
                if wait:
                    copy.wait()
                else:
                    copy.start()
                return inner
            return lax.fori_loop(used_ref[i], LOCAL_CHUNKS, body, carry)
        lax.fori_loop(0, used_ref.shape[0], per_tile, 0)

    @pl.when(j == 0)
    def _():
        zero_buf[...] = jnp.zeros_like(zero_buf)
        scratch0 = used_ref.shape[0] * LOCAL_CHUNKS
        for wait in (False, True):
            for k in range(2 * TILE_CHUNKS):
                copy = _chunk_copy(zero_buf, 0, y_ref, scratch0 + k, zero_sem)
                copy.wait() if wait else copy.start()
        zero_fill(wait=False)
        gather(0, 0)

    @pl.when(j + 1 < n_tiles)
    def _():
        gather(j + 1, (j + 1) % 2)

    @pl.when(j < n_tiles)
    def _():
        slot = j % 2
        wait_gather(slot)
        rows = in_buf[slot].reshape(EXPERT_TILE, in_buf.shape[-1])
        h = rows[:, 0:d_model]
        weights = rows[:, d_model:d_model + LANES].astype(F32) + rows[:, d_model + LANES:].astype(F32)
        experts = range(EXPERTS_PER_GROUP)
        up = [(_dot(h, w1_ref[k]), _dot(h, w3_ref[k])) for k in experts]
        hid = [(a * _sigmoid(a) * g * weights[:, k:k + 1]).astype(BF16) for k, (a, g) in zip(experts, up)]
        y = _dot(hid[0], w2_ref[0])
        for k in experts[1:]:
            y = y + _dot(hid[k], w2_ref[k])

        @pl.when(j >= 2)
        def _():
            wait_scatter(slot)

        out_buf[slot] = y.astype(BF16).reshape(TILE_CHUNKS, CHUNK, d_model)
        scatter(j, slot)

        @pl.when(j == n_tiles - 1)
        def _():
            @pl.when(j >= 1)
            def _():
                wait_scatter(1 - slot)
            wait_scatter(slot)
            zero_fill(wait=True)


def _combine_kernel(x1_ref, mod_ref, lpos_ref, y_ref, o_ref):
    tm = MOE_TILE
    local_row = lax.broadcasted_iota(jnp.int32, (tm, LOCAL_ROWS), 1).astype(F32)
    for t in range(x1_ref.shape[0] // tm):
        rows = slice(t * tm, (t + 1) * tm)
        lpos = jnp.broadcast_to(lpos_ref[t][0:1, :], (LANES, tm)).T[:, 0:1]
        unsort = jnp.where(local_row == lpos, 1.0, 0.0).astype(BF16)
        y = _dot(unsort, y_ref[t * LOCAL_ROWS:(t + 1) * LOCAL_ROWS, :])
        o_ref[rows, :] = x1_ref[rows, :] + mod_ref[5:6, :] * y


def _moe_plan(cnt):
    n_local = cnt.shape[0]
    chunks = (cnt + CHUNK - 1) // CHUNK
    used = jnp.sum(chunks, axis=1)
    local_off = jnp.cumsum(chunks, axis=1) - chunks
    tiles_g = (jnp.sum(chunks, axis=0) + TILE_CHUNKS - 1) // TILE_CHUNKS
    tile_end = jnp.cumsum(tiles_g)
    n_tiles = tile_end[-1]
    group_start = (tile_end - tiles_g) * TILE_CHUNKS
    seg_len = chunks.T.reshape(-1)
    seg_start = (group_start[:, None] + (jnp.cumsum(chunks, axis=0) - chunks).T).reshape(-1)
    seg_src = (jnp.arange(n_local)[None, :] * LOCAL_CHUNKS + local_off.T).reshape(-1)
    max_chunks = n_local * MOE_TILE // CHUNK + n_local * N_GROUPS + N_GROUPS * TILE_CHUNKS
    max_tiles = (max_chunks + TILE_CHUNKS - 1) // TILE_CHUNKS
    c = jnp.arange(max_tiles * TILE_CHUNKS)[:, None]
    within = c - seg_start[None, :]
    hit = jnp.logical_and(within >= 0, within < seg_len[None, :])
    valid = jnp.any(hit, axis=1)
    src = jnp.sum(jnp.where(hit, seg_src[None, :] + within, 0), axis=1)
    src = jnp.where(valid, src, LOCAL_CHUNKS - 1)
    slot_k = c[:, 0] % (2 * TILE_CHUNKS)
    dst = jnp.where(valid, src, n_local * LOCAL_CHUNKS + slot_k)
    j = jnp.minimum(jnp.arange(max_tiles), n_tiles - 1)
    grp = jnp.sum(j[:, None] >= tile_end[None, :], axis=1)
    gid = jnp.arange(N_GROUPS)
    later = jnp.where(jnp.logical_and(gid[None, :] > gid[:, None], tiles_g[None, :] > 0), gid[None, :], N_GROUPS)
    next_of = jnp.min(later, axis=1)
    next_of = jnp.where(next_of == N_GROUPS, gid, next_of)
    next_grp = jnp.sum(jnp.where(grp[:, None] == gid[None, :], next_of[None, :], 0), axis=1)
    i32 = lambda a: a.astype(jnp.int32)
    return i32(src), i32(dst), i32(grp), i32(next_grp), i32(n_tiles).reshape(1), i32(used)


def _moe(plan, rows_local, w1g, w3g, w2g):
    src, dst, grp, next_grp, n_tiles, used = plan
    D = w2g.shape[3]
    n_rows, cols = rows_local.shape
    n_chunks = n_rows // CHUNK
    hbm = pl.BlockSpec(memory_space=pl.ANY)
    group_weights = [w1g, w3g, w2g]
    y = pl.pallas_call(
        _moe_kernel,
        grid_spec=pltpu.PrefetchScalarGridSpec(
            num_scalar_prefetch=6,
            grid=(grp.shape[0],),
            in_specs=[hbm, hbm, hbm, hbm],
            out_specs=hbm,
            scratch_shapes=([pltpu.VMEM((2, TILE_CHUNKS, CHUNK, cols), BF16),
                             pltpu.VMEM((2, TILE_CHUNKS, CHUNK, D), BF16),
                             pltpu.VMEM((1, CHUNK, D), BF16)]
                            + [pltpu.VMEM(w.shape[1:], F32) for w in group_weights]
                            + [pltpu.VMEM(w.shape[1:], BF16) for w in group_weights]
                            + [pltpu.SemaphoreType.DMA((2,)), pltpu.SemaphoreType.DMA((2,)),
                               pltpu.SemaphoreType.DMA(()), pltpu.SemaphoreType.DMA(())])),
        out_shape=jax.ShapeDtypeStruct((n_chunks + 2 * TILE_CHUNKS, CHUNK, D), BF16),
        compiler_params=_params("arbitrary"),
        name="moe",
    )(src, dst, grp, next_grp, n_tiles, used, rows_local.reshape(n_chunks, CHUNK, cols), w1g, w3g, w2g)
    return y.reshape((n_chunks + 2 * TILE_CHUNKS) * CHUNK, D)


def _combine(x1, mod, lpos, y_local, *, T):
    N, D = x1.shape
    subs = COMBINE_SUBS
    tm = subs * MOE_TILE
    per_b = T // tm
    return pl.pallas_call(
        _combine_kernel,
        grid=(N // tm,),
        in_specs=[pl.BlockSpec((tm, D), lambda i: (i, 0)),
                  pl.BlockSpec((None, 6, D), lambda i: (i // per_b, 0, 0)),
                  pl.BlockSpec((subs, 8, MOE_TILE), lambda i: (i, 0, 0)),
                  pl.BlockSpec((subs * LOCAL_ROWS, D), lambda i: (i, 0))],
        out_specs=pl.BlockSpec((tm, D), lambda i: (i, 0)),
        out_shape=jax.ShapeDtypeStruct((N, D), F32),
        compiler_params=_params("arbitrary"),
        name="moe_combine",
    )(x1, mod, lpos, y_local)


def _block_diag(n, blk, value, dtype):
    r = np.arange(n)[:, None] // blk
    c = np.arange(n)[None, :] // blk
    return jnp.asarray(np.where(r == c, value, 0.0), dtype)


def _gla_constants():
    L = GLA_CHUNK
    i = np.arange(L)[:, None]
    j = np.arange(L)[None, :]
    tri = j <= i
    bdtri = np.logical_and(j <= i, i // GLA_SUB == j // GLA_SUB)
    cum = np.block([[tri, tri], [bdtri, bdtri]])
    col = np.arange(N_SUB * LANES)[None, :]
    qmask = np.stack([np.logical_and(col // LANES == i // GLA_SUB, (col % LANES) // GLA_DK == hh)
                      for hh in range(2)])
    bdms = _block_diag(2 * GLA_DV, GLA_DV, 1.0 / GLA_DV, BF16)
    d = np.arange(LANES)[:, None] // GLA_DK
    e = np.arange(2 * GLA_DV)[None, :] // GLA_DV
    smask = d == e
    return (jnp.asarray(cum, BF16), jnp.asarray(qmask, BF16), bdms, jnp.asarray(smask, F32))


def kernel(x, c, w_ada, b_ada, g_norm1, w_in, q_norm, k_norm, sinks, w_gk2, b_gk, g_gla_out, g_att_out,
           w_out, g_norm2, w_group, b_group, w_router, b_router, w1, w3, w2):
    B, T, D = x.shape
    N = B * T
    depth = w_ada.shape[0]
    cum, qmask, bdms, smask = _gla_constants()
    bdq = _block_diag(2 * LANES, HEAD_DIM, 1.0 / HEAD_DIM, BF16)
    bdk = _block_diag(ATT_KV, HEAD_DIM, 1.0 / HEAD_DIM, BF16)

    x2 = x.reshape(N, D)
    for l in range(depth):
        mod = _adaln_mod(c, w_ada[l], b_ada[l]).reshape(B, 6, D)

        wgk =jnp.concatenate([w_gk2[l], jnp.zeros((LANES - GLA_RANK, GLA_K), F32)], axis=0).astype(BF16)
        qa, ka, va, qg, kg, vg, la, og = _in_proj(
            x2, mod, g_norm1[l].reshape(1, D), w_in[l].T,
            jnp.tile(q_norm[l], ATT_HEADS).reshape(1, ATT_Q), jnp.tile(k_norm[l], ATT_KV_HEADS).reshape(1, ATT_KV),
            bdq, bdk, wgk, b_gk[l].reshape(1, GLA_K), T=T, tm=PROJ_TILE)

        y_att, y_gla = _mixers(sinks[l], qa, ka, va, g_att_out[l].reshape(1, ATT_Q),
                               qg, kg, vg, la, og, cum, qmask, bdms, smask,
                               jnp.tile(g_gla_out[l], GLA_HEADS).reshape(1, GLA_V), B=B, T=T)

        pad = LANES - N_GROUPS - N_EXPERTS
        wr_t = jnp.concatenate([w_group[l], w_router[l], jnp.zeros((D, pad), F32)], axis=1).T.astype(BF16)
        br_t = jnp.concatenate([b_group[l], b_router[l], jnp.zeros((pad,), F32)]).reshape(LANES, 1)
        striu = jnp.asarray(np.arange(MOE_TILE)[:, None] < np.arange(MOE_TILE)[None, :], BF16)
        x1, rows_local, lpos, cnt = _out_proj(y_att, y_gla, x2, mod, w_out[l].astype(BF16),
                                              g_norm2[l].reshape(1, D), wr_t, br_t, striu, T=T)
        plan = _moe_plan(cnt[:, :N_GROUPS, 0].astype(jnp.int32))
        by_group = lambda w: w.reshape((N_GROUPS, EXPERTS_PER_GROUP) + w.shape[1:])
        y_local = _moe(plan, rows_local, by_group(w1[l]), by_group(w3[l]), by_group(w2[l]))
        x2 = _combine(x1, mod, lpos, y_local, T=T)
    return x2.reshape(B, T, D)
```

```python
import jax
import jax.numpy as jnp
import numpy as np
from jax import lax
from jax.experimental import pallas as pl
from jax.experimental.pallas import tpu as pltpu

F32 = jnp.float32
BF16 = jnp.bfloat16

EPS = 1e-6
LOG2_E = 1.4426950408889634
ATT_HEADS = 8
ATT_KV_HEADS = 2
HEAD_DIM = 64
WINDOW = 128
ATT_Q = ATT_HEADS * HEAD_DIM
ATT_KV = ATT_KV_HEADS * HEAD_DIM
GLA_HEADS = 4
GLA_DK = 64
GLA_DV = 128
GLA_RANK = 16
GLA_NORMALIZER = 16.0
GLA_K = GLA_HEADS * GLA_DK
GLA_V = GLA_HEADS * GLA_DV
N_GROUPS = 4
EXPERTS_PER_GROUP = 4
N_EXPERTS = N_GROUPS * EXPERTS_PER_GROUP

LANES = 128
PROJ_SPLITS = (512, 512)
PROJ_TILE = sum(PROJ_SPLITS)
OUT_SUBS = 4
COMBINE_SUBS = 4
ROUTER_ROWS = 24
SEQS_PER_STEP = 8
GLA_CHUNK = 128
GLA_SUB = 16
N_SUB = GLA_CHUNK // GLA_SUB
ROUTER_LANE0 = N_GROUPS
VMEM_LIMIT = 56 * 1024 * 1024
MOE_TILE = 256
EXPERT_TILE = 256
CHUNK = 16
TILE_CHUNKS = EXPERT_TILE // CHUNK
LOCAL_CHUNKS = (MOE_TILE + N_GROUPS * (CHUNK - 1)) // CHUNK + 2
LOCAL_ROWS = LOCAL_CHUNKS * CHUNK

_QA0, _KA0, _VA0 = 0, ATT_Q, ATT_Q + ATT_KV
_QG0 = _VA0 + ATT_KV
_KG0 = _QG0 + GLA_K
_VG0 = _KG0 + GLA_K
_OG0 = _VG0 + GLA_V
_LR0 = _OG0 + GLA_V
IN_COLS_PAD = _LR0 + LANES


def _dot(a, b):
    return jnp.dot(a, b, preferred_element_type=F32)


def _dot_nt(a, b):
    return lax.dot_general(a, b, (((1,), (1,)), ((), ())), preferred_element_type=F32)


def _sigmoid(x):
    return 1.0 / (1.0 + jnp.exp(-x))


def _params(*sem):
    return pltpu.CompilerParams(dimension_semantics=sem, vmem_limit_bytes=VMEM_LIMIT)


def _mod_kernel(c_ref, w_ref, b_ref, o_ref):
    c = c_ref[...]
    s = (c * _sigmoid(c)).astype(BF16)
    o_ref[...] = _dot(s, w_ref[...].astype(BF16)) + b_ref[...]


def _adaln_mod(c, w_ada, b_ada):
    B, D = c.shape
    n = w_ada.shape[1]
    tn = 1536
    return pl.pallas_call(
        _mod_kernel,
        grid=(n // tn,),
        in_specs=[pl.BlockSpec((B, D), lambda j: (0, 0)),
                  pl.BlockSpec((D, tn), lambda j: (0, j)),
                  pl.BlockSpec((1, tn), lambda j: (0, j))],
        out_specs=pl.BlockSpec((B, tn), lambda j: (0, j)),
        out_shape=jax.ShapeDtypeStruct((B, n), F32),
        compiler_params=_params("arbitrary"),
        name="adaln_mod",
    )(c, w_ada, b_ada.reshape(1, n))


def _inproj_kernel(x_ref, mod_ref, g1_ref, wf_ref, qn_ref, kn_ref, bdq_ref, bdk_ref, wgk_ref, bgk_ref,
                   qa_ref, ka_ref, va_ref, qg_ref, kg_ref, vg_ref, la_ref, og_ref, w_ref):
    subs = range(len(PROJ_SPLITS))
    starts = [sum(PROJ_SPLITS[:t]) for t in subs]
    rows = lambda t: slice(starts[t], starts[t] + PROJ_SPLITS[t])

    @pl.when(pl.program_id(0) == 0)
    def _():
        lr_src = _OG0
        w_ref[0:_OG0, :] = wf_ref[0:_OG0, :].astype(BF16)
        w_ref[_OG0:_LR0, :] = wf_ref[lr_src + GLA_RANK:lr_src + GLA_RANK + GLA_V, :].astype(BF16)
        w_ref[_LR0:_LR0 + GLA_RANK, :] = wf_ref[lr_src:lr_src + GLA_RANK, :].astype(BF16)
        w_ref[_LR0 + GLA_RANK:IN_COLS_PAD, :] = jnp.zeros((LANES - GLA_RANK, wf_ref.shape[1]), BF16)

    h = []
    for t in subs:
        x = x_ref[rows(t), :]
        ms = jnp.mean(x * x, axis=-1, keepdims=True)
        xn = x * lax.rsqrt(ms + EPS) * g1_ref[...]
        h.append((xn * (1.0 + mod_ref[1:2, :]) + mod_ref[0:1, :]).astype(BF16))

    proj = lambda t, c0, width: _dot_nt(h[t], w_ref[c0:c0 + width, :])
    gate_of = lambda lr: _dot(lr.astype(BF16), wgk_ref[...]) + bgk_ref[...]
    qa, kv, q_ms, k_ms, qk_g, vg, og, lr, gate = ({} for _ in range(9))
    for t in subs:
        qa[t] = proj(t, _QA0, ATT_Q)
        if t > 0:
            gate[t - 1] = gate_of(lr[t - 1])
        kv[t] = proj(t, _KA0, 2 * ATT_KV)
        q_sq = (qa[t] * qa[t]).astype(BF16)
        q_ms[t] = jnp.concatenate([_dot(q_sq[:, c:c + 2 * LANES], bdq_ref[...])
                                   for c in range(0, ATT_Q, 2 * LANES)], axis=1)
        qk_g[t] = proj(t, _QG0, 2 * GLA_K)
        k = kv[t][:, 0:ATT_KV]
        k_ms[t] = _dot((k * k).astype(BF16), bdk_ref[...])
        vg[t] = proj(t, _VG0, GLA_V)
        og[t] = proj(t, _OG0, GLA_V)
        lr[t] = proj(t, _LR0, LANES)
    gate[subs[-1]] = gate_of(lr[subs[-1]])

    for t in subs:
        low = lax.broadcasted_iota(jnp.int32, (PROJ_SPLITS[t], ATT_KV), 1) < HEAD_DIM
        qa_ref[rows(t), :] = (qa[t] * lax.rsqrt(q_ms[t] + EPS) * qn_ref[...]
                              * (HEAD_DIM ** -0.5 * LOG2_E)).astype(BF16)
        k = kv[t][:, 0:ATT_KV] * lax.rsqrt(k_ms[t] + EPS) * kn_ref[...]
        v = kv[t][:, ATT_KV:2 * ATT_KV]
        for src, dst in ((k, ka_ref), (v, va_ref)):
            swapped = pltpu.roll(src, HEAD_DIM, axis=1)
            dst[rows(t), 0:LANES] = jnp.where(low, src, swapped).astype(BF16)
            dst[rows(t), LANES:2 * LANES] = jnp.where(low, swapped, src).astype(BF16)
        qg_ref[rows(t), :] = (qk_g[t][:, 0:GLA_K] * (GLA_DK ** -0.5)).astype(BF16)
        kg_ref[rows(t), :] = qk_g[t][:, GLA_K:2 * GLA_K].astype(BF16)
        vg_ref[rows(t), :] = vg[t].astype(BF16)
        og_ref[rows(t), :] = (og[t] * _sigmoid(og[t])).astype(BF16)
        log_sig = jnp.minimum(gate[t], 0.0) - jnp.log(1.0 + jnp.exp(-jnp.abs(gate[t])))
        la_ref[rows(t), :] = log_sig * (1.0 / GLA_NORMALIZER)


def _in_proj(x2, mod, g1, w_in_t, qn, kn, bdq, bdk, wgk, bgk, *, T, tm):
    N, D = x2.shape
    per_b = T // tm
    const = lambda shape: pl.BlockSpec(shape, lambda i: (0,) * len(shape))
    rows = lambda c: pl.BlockSpec((tm, c), lambda i: (i, 0))
    outs = [(ATT_Q, BF16), (2 * ATT_KV, BF16), (2 * ATT_KV, BF16), (GLA_K, BF16), (GLA_K, BF16),
            (GLA_V, BF16), (GLA_K, F32), (GLA_V, BF16)]
    return pl.pallas_call(
        _inproj_kernel,
        grid=(N // tm,),
        in_specs=[rows(D),
                  pl.BlockSpec((None, 6, D), lambda i: (i // per_b, 0, 0)),
                  const((1, D)),
                  pl.BlockSpec(w_in_t.shape, lambda i: (0, 0), pipeline_mode=pl.Buffered(1)),
                  const(qn.shape), const(kn.shape),
                  const(bdq.shape), const(bdk.shape), const(wgk.shape), const(bgk.shape)],
        out_specs=[rows(c) for c, _ in outs],
        out_shape=[jax.ShapeDtypeStruct((N, c), dt) for c, dt in outs],
        scratch_shapes=[pltpu.VMEM((IN_COLS_PAD, D), BF16)],
        compiler_params=_params("arbitrary"),
        name="in_proj",
    )(x2, mod, g1, w_in_t, qn, kn, bdq, bdk, wgk, bgk)


def _attn_stages(seqs, sinks_ref, q_ref, kc_ref, vc_ref, gatt_ref, o_ref, kp_ref, vp_ref):
    blk = WINDOW
    first = pl.program_id(1) == 0
    qi = lax.broadcasted_iota(jnp.int32, (blk, blk), 0)
    cj = lax.broadcasted_iota(jnp.int32, (blk, blk), 1)
    from_prev = cj > qi
    dist = (qi - cj + jnp.where(from_prev, blk, 0)).astype(F32)
    no_prev = jnp.where(jnp.logical_and(from_prev, first), -1e30, 0.0)
    low = cj < HEAD_DIM
    half = (jnp.where(low, 1.0, 0.0).astype(BF16), jnp.where(low, 0.0, 1.0).astype(BF16))
    half2 = tuple(jnp.concatenate([m, m], axis=0) for m in half)
    prev_mask = jnp.where(from_prev, 1.0, 0.0).astype(BF16)
    cur_mask = jnp.where(from_prev, 0.0, 1.0).astype(BF16)

    n_pairs = ATT_HEADS // 2
    pairs_per_kv = n_pairs // ATT_KV_HEADS
    units = [(bi, j) for bi in seqs for j in range(n_pairs)]

    def kv_blocks(bi, g):
        lanes = slice(g * LANES, (g + 1) * LANES)
        return (kp_ref[bi, :, lanes], vp_ref[bi, :, lanes]), (kc_ref[bi, :, lanes], vc_ref[bi, :, lanes])

    scores = {}
    for bi, j in units:
        (kp, _), (kc, _) = kv_blocks(bi, j // pairs_per_kv)
        k_both = jnp.concatenate([kp, kc], axis=0)
        qp = q_ref[bi, :, j * LANES:(j + 1) * LANES]
        for p in range(2):
            scores[bi, j, p] = _dot_nt(qp, k_both * half2[p])
    yield

    probs, sink_terms = {}, {}
    for bi, j in units:
        for p in range(2):
            h = 2 * j + p
            slope = 2.0 ** (-8.0 * (h + 1) / ATT_HEADS) * LOG2_E
            s_both = scores[bi, j, p]
            s = jnp.where(from_prev, s_both[:, 0:blk], s_both[:, blk:2 * blk]) - slope * dist + no_prev
            sink = sinks_ref[h] * LOG2_E
            m = jnp.maximum(jnp.max(s, axis=-1, keepdims=True), sink)
            probs[bi, j, p] = jnp.exp2(s - m)
            sink_terms[bi, j, p] = jnp.exp2(sink - m)
    yield

    outs = {}
    for bi, j in units:
        (_, vp), (_, vc) = kv_blocks(bi, j // pairs_per_kv)
        v_stack = jnp.concatenate([jnp.concatenate([v * half[p], half[p]], axis=1)
                                   for v in (vp, vc) for p in range(2)], axis=0)
        e = [probs[bi, j, p].astype(BF16) for p in range(2)]
        p_all = jnp.concatenate([x * m for m in (prev_mask, cur_mask) for x in e], axis=1)
        pv = _dot(p_all, v_stack)
        den = pv[:, LANES:2 * LANES] + jnp.where(low, sink_terms[bi, j, 0], sink_terms[bi, j, 1])
        outs[bi, j] = pv[:, 0:LANES] / den
    yield

    for bi in seqs:
        o = jnp.concatenate([outs[bi, j] for j in range(n_pairs)], axis=1)
        ms = jnp.mean(o * o, axis=-1, keepdims=True)
        o_ref[bi] = (o * lax.rsqrt(ms + EPS) * gatt_ref[...]).astype(BF16)
        kp_ref[bi] = kc_ref[bi]
        vp_ref[bi] = vc_ref[bi]


def _gla_stages(seqs, q_ref, k_ref, v_ref, la_ref, og_ref, cum_ref, qmask_ref, bdms_ref, smask_ref,
                ggla_ref, o_ref, state_ref):
    L = GLA_CHUNK
    rows = lax.broadcasted_iota(jnp.int32, (L, LANES), 0)
    causal = lax.broadcasted_iota(jnp.int32, (L, L), 0) >= lax.broadcasted_iota(jnp.int32, (L, L), 1)
    units = [(s, pair) for s in seqs for pair in range(GLA_HEADS // 2)]
    kl = lambda pair: slice(pair * LANES, (pair + 1) * LANES)
    vl = lambda pair: slice(pair * 2 * GLA_DV, (pair + 1) * 2 * GLA_DV)

    b, b_in = {}, {}
    for s in seqs:
        la = la_ref[s]
        la_hi = la.astype(BF16)
        la_lo = (la - la_hi.astype(F32)).astype(BF16)
        sums = _dot(cum_ref[...], jnp.concatenate([la_hi, la_lo], axis=0)) * LOG2_E
        b[s] = sums[0:L, :]
        b_in[s] = sums[L:2 * L, :]
    yield

    q_both, keys, q_dec, k_dec_t, b_last = {}, {}, {}, {}, {}
    for s in seqs:
        ref = b[s] - b_in[s]
        b_last[s] = b[s][L - 1:L, :]
        q = q_ref[s].astype(F32)
        k = k_ref[s].astype(F32)
        q_in = (q * jnp.exp2(b_in[s])).astype(BF16)
        q_dec[s] = (q * jnp.exp2(b[s])).astype(BF16)
        k_dec = k * jnp.exp2(b_last[s] - b[s])
        for pair in range(GLA_HEADS // 2):
            k_p, b_p, ref_p = k[:, kl(pair)], b[s][:, kl(pair)], ref[:, kl(pair)]
            expanded = []
            for g in range(N_SUB):
                last_row = (g + 1) * GLA_SUB - 1
                expo = jnp.where(rows <= last_row, ref_p[g * GLA_SUB:g * GLA_SUB + 1, :] - b_p, -1e4)
                expanded.append((k_p * jnp.exp2(expo)).astype(BF16))
            keys[s, pair] = jnp.concatenate(expanded, axis=1)
            q_rep = jnp.concatenate([q_in[:, kl(pair)]] * N_SUB, axis=1)
            q_both[s, pair] = jnp.concatenate([q_rep * qmask_ref[0], q_rep * qmask_ref[1]], axis=0)
            k_dec_t[s, pair] = k_dec[:, kl(pair)].T.astype(BF16)
    yield

    scores = {u: _dot_nt(q_both[u], keys[u]) for u in units}
    yield

    outs, updates = {}, {}
    for s, pair in units:
        v_p = v_ref[s, :, vl(pair)]
        o_parts = []
        for hh in range(2):
            a = jnp.where(causal, scores[s, pair][hh * L:(hh + 1) * L, :], 0.0).astype(BF16)
            o_parts.append(_dot(a, v_p[:, hh * GLA_DV:(hh + 1) * GLA_DV]))
        state = state_ref[s, pair]
        outs[s, pair] = jnp.concatenate(o_parts, axis=1) + _dot(q_dec[s][:, kl(pair)], state.astype(BF16))
        updates[s, pair] = _dot(k_dec_t[s, pair], v_p)
    yield

    for s, pair in units:
        decay = jnp.broadcast_to(jnp.exp2(b_last[s][:, kl(pair)]), (LANES, LANES)).T
        state_ref[s, pair] = (state_ref[s, pair] * jnp.concatenate([decay, decay], axis=1)
                              + updates[s, pair] * smask_ref[...])
        o = outs[s, pair]
        ms = _dot((o * o).astype(BF16), bdms_ref[...])
        y = o * lax.rsqrt(ms + EPS) * ggla_ref[:, vl(pair)] * og_ref[s, :, vl(pair)].astype(F32)
        o_ref[s, :, vl(pair)] = y.astype(BF16)


def _mixers_kernel(sinks_ref, qa_ref, ka_ref, va_ref, gatt_ref,
                   qg_ref, kg_ref, vg_ref, la_ref, og_ref, cum_ref, qmask_ref, bdms_ref, smask_ref, ggla_ref,
                   ya_ref, yg_ref, kp_ref, vp_ref, state_ref):
    @pl.when(pl.program_id(1) == 0)
    def _():
        state_ref[...] = jnp.zeros_like(state_ref)
        kp_ref[...] = jnp.zeros_like(kp_ref)
        vp_ref[...] = jnp.zeros_like(vp_ref)

    n_seq = qa_ref.shape[0]
    waiting = []
    for group in (tuple(range(0, n_seq // 2)), tuple(range(n_seq // 2, n_seq))):
        waiting.append([
            _attn_stages(group, sinks_ref, qa_ref, ka_ref, va_ref, gatt_ref, ya_ref, kp_ref, vp_ref),
            _gla_stages(group, qg_ref, kg_ref, vg_ref, la_ref, og_ref, cum_ref, qmask_ref, bdms_ref, smask_ref,
                        ggla_ref, yg_ref, state_ref)])
    pending = []
    while pending or waiting:
        if waiting:
            pending.extend(waiting.pop(0))
        for stages in list(pending):
            if next(stages, "done") == "done":
                pending.remove(stages)


def _mixers(sinks, qa, ka, va, gatt, qg, kg, vg, la, og, cum, qmask, bdms, smask, ggla, *, B, T):
    assert GLA_CHUNK == WINDOW
    L = WINDOW
    nb = SEQS_PER_STEP
    seq = lambda a: a.reshape(B, T, a.shape[-1])
    rows = lambda c: pl.BlockSpec((nb, L, c), lambda b, i: (b, i, 0))
    const = lambda a: pl.BlockSpec(a.shape, lambda b, i: (0,) * a.ndim)
    y_att, y_gla = pl.pallas_call(
        _mixers_kernel,
        grid=(B // nb, T // L),
        in_specs=[pl.BlockSpec(memory_space=pltpu.SMEM),
                  rows(ATT_Q), rows(2 * ATT_KV), rows(2 * ATT_KV), const(gatt),
                  rows(GLA_K), rows(GLA_K), rows(GLA_V), rows(GLA_K), rows(GLA_V),
                  const(cum), const(qmask), const(bdms), const(smask), const(ggla)],
        out_specs=[rows(ATT_Q), rows(GLA_V)],
        out_shape=[jax.ShapeDtypeStruct((B, T, ATT_Q), BF16), jax.ShapeDtypeStruct((B, T, GLA_V), BF16)],
        scratch_shapes=[pltpu.VMEM((nb, L, 2 * ATT_KV), BF16), pltpu.VMEM((nb, L, 2 * ATT_KV), BF16),
                        pltpu.VMEM((nb, GLA_HEADS // 2, LANES, 2 * GLA_DV), F32)],
        compiler_params=_params("arbitrary", "arbitrary"),
        name="mixers",
    )(sinks, seq(qa), seq(ka), seq(va), gatt, seq(qg), seq(kg), seq(vg), seq(la), seq(og),
      cum, qmask, bdms, smask, ggla)
    return y_att.reshape(B * T, ATT_Q), y_gla.reshape(B * T, GLA_V)


def _route(logits_t):
    lt = logits_t[0:ROUTER_ROWS, :]
    row = lax.broadcasted_iota(jnp.int32, lt.shape, 0)
    neg_inf = -jnp.inf
    g_log = jnp.where(row < N_GROUPS, lt, neg_inf)
    g_max = jnp.max(g_log, axis=0, keepdims=True)
    g_sel = jnp.min(jnp.where(g_log == g_max, row, LANES), axis=0, keepdims=True)
    g_sum = jnp.sum(jnp.where(row < N_GROUPS, jnp.exp(lt - g_max), 0.0), axis=0, keepdims=True)
    p_group = 1.0 / g_sum
    e_lo = ROUTER_LANE0 + EXPERTS_PER_GROUP * g_sel
    in_group = jnp.logical_and(row >= e_lo, row < e_lo + EXPERTS_PER_GROUP)
    e_log = jnp.where(in_group, lt, neg_inf)
    e_max = jnp.max(e_log, axis=0, keepdims=True)
    top1 = jnp.min(jnp.where(e_log == e_max, row, LANES), axis=0, keepdims=True)
    e_log2 = jnp.where(row == top1, neg_inf, e_log)
    e_max2 = jnp.max(e_log2, axis=0, keepdims=True)
    top2 = jnp.min(jnp.where(e_log2 == e_max2, row, LANES), axis=0, keepdims=True)
    ratio = jnp.exp(e_max2 - e_max)
    w_top1 = p_group / (1.0 + ratio)
    w_top2 = p_group * ratio / (1.0 + ratio)
    row8 = lax.broadcasted_iota(jnp.int32, (8, lt.shape[1]), 0)
    weights = jnp.where(row8 == top1 - e_lo, w_top1, 0.0) + jnp.where(row8 == top2 - e_lo, w_top2, 0.0)
    return g_sel, weights


def _outproj_kernel(ya_ref, yg_ref, x_ref, mod_ref, wo_ref, g2_ref, wrt_ref, brt_ref, striu_ref,
                    x1_ref, row_ref, lpos_ref, cnt_ref):
    tm = MOE_TILE
    subs = range(x_ref.shape[0] // tm)
    rows = lambda t: slice(t * tm, (t + 1) * tm)

    mix = [_dot(ya_ref[rows(t), :], wo_ref[0:ATT_Q, :]) + _dot(yg_ref[rows(t), :], wo_ref[ATT_Q:ATT_Q + GLA_V, :])
           for t in subs]
    h2b = []
    for t in subs:
        x1 = x_ref[rows(t), :] + mod_ref[2:3, :] * mix[t]
        x1_ref[rows(t), :] = x1
        ms = jnp.mean(x1 * x1, axis=-1, keepdims=True)
        h2 = (x1 * lax.rsqrt(ms + EPS) * g2_ref[...]) * (1.0 + mod_ref[4:5, :]) + mod_ref[3:4, :]
        h2b.append(h2.astype(BF16))
    logits_t = [_dot_nt(wrt_ref[...], h2b[t]) + brt_ref[...] for t in subs]

    routed = [_route(logits_t[t]) for t in subs]
    row8 = lax.broadcasted_iota(jnp.int32, (8, tm), 0)
    onehot = [jnp.where(row8 == routed[t][0], 1.0, 0.0) for t in subs]
    before = [_dot(onehot[t].astype(BF16), striu_ref[...]) for t in subs]

    local_row = lax.broadcasted_iota(jnp.int32, (LOCAL_ROWS, tm), 0).astype(F32)
    pad_rows = jnp.zeros((LANES - 8, tm), F32)
    for t in subs:
        count = jnp.sum(onehot[t], axis=1, keepdims=True)
        cnt_ref[t] = jnp.broadcast_to(count, (8, LANES))
        padded = jnp.broadcast_to(jnp.floor((count + (CHUNK - 1.0)) * (1.0 / CHUNK)) * CHUNK, (8, tm))
        start = jnp.zeros((8, tm), F32)
        for shift in range(1, N_GROUPS):
            start = start + jnp.where(row8 >= shift, pltpu.roll(padded, shift, axis=0), 0.0)
        lpos = jnp.sum(onehot[t] * (before[t] + start), axis=0, keepdims=True)
        lpos_ref[t] = jnp.broadcast_to(lpos, (8, tm))
        weights = jnp.concatenate([routed[t][1], pad_rows], axis=0).T
        w_hi = weights.astype(BF16)
        w_lo = (weights - w_hi.astype(F32)).astype(BF16)
        perm = jnp.where(local_row == lpos, 1.0, 0.0).astype(BF16)
        row_ref[t * LOCAL_ROWS:(t + 1) * LOCAL_ROWS, :] = _dot(
            perm, jnp.concatenate([h2b[t], w_hi, w_lo], axis=1)).astype(BF16)


def _out_proj(ya, yg, x2, mod, wo, g2, wr, br, stril, *, T):
    N, D = x2.shape
    subs = OUT_SUBS
    tm = subs * MOE_TILE
    per_b = T // tm
    rows = lambda c: pl.BlockSpec((tm, c), lambda i: (i, 0))
    const = lambda a: pl.BlockSpec(a.shape, lambda i: (0,) * a.ndim)
    return pl.pallas_call(
        _outproj_kernel,
        grid=(N // tm,),
        in_specs=[rows(ATT_Q), rows(GLA_V), rows(D),
                  pl.BlockSpec((None, 6, D), lambda i: (i // per_b, 0, 0)),
                  const(wo), const(g2), const(wr), const(br), const(stril)],
        out_specs=[rows(D), pl.BlockSpec((subs * LOCAL_ROWS, D + 2 * LANES), lambda i: (i, 0)),
                   pl.BlockSpec((subs, 8, MOE_TILE), lambda i: (i, 0, 0)),
                   pl.BlockSpec((subs, 8, LANES), lambda i: (i, 0, 0))],
        out_shape=[jax.ShapeDtypeStruct((N, D), F32),
                   jax.ShapeDtypeStruct((N // MOE_TILE * LOCAL_ROWS, D + 2 * LANES), BF16),
                   jax.ShapeDtypeStruct((N // MOE_TILE, 8, MOE_TILE), F32),
                   jax.ShapeDtypeStruct((N // MOE_TILE, 8, LANES), F32)],
        compiler_params=_params("arbitrary"),
        name="out_proj",
    )(ya, yg, x2, mod, wo, g2, wr, br, stril)


def _chunk_copy(src_ref, src_chunk, dst_ref, dst_chunk, sem):
    return pltpu.make_async_copy(src_ref.at[src_chunk], dst_ref.at[dst_chunk], sem)


def _moe_kernel(src_ref, dst_ref, grp_ref, next_ref, nt_ref, used_ref, rows_ref, w1f_ref, w3f_ref, w2f_ref, y_ref,
                in_buf, out_buf, zero_buf, st1_ref, st3_ref, st2_ref, w1_ref, w3_ref, w2_ref,
                in_sem, out_sem, zero_sem, w_sem):
    j = pl.program_id(0)
    n_tiles = nt_ref[0]
    d_model = w2_ref.shape[2]

    stages = ((w1f_ref, st1_ref, w1_ref), (w3f_ref, st3_ref, w3_ref), (w2f_ref, st2_ref, w2_ref))

    def fetch_weights(group):
        for hbm, stage, _ in stages:
            pltpu.make_async_copy(hbm.at[group], stage, w_sem).start()

    @pl.when(j == 0)
    def _():
        fetch_weights(grp_ref[0])

    @pl.when(jnp.logical_or(j == 0, grp_ref[j] != grp_ref[jnp.maximum(j - 1, 0)]))
    def _():
        for hbm, stage, dst in stages:
            pltpu.make_async_copy(hbm.at[0], stage, w_sem).wait()
        for hbm, stage, dst in stages:
            for k in range(EXPERTS_PER_GROUP):
                dst[k] = stage[k].astype(BF16)

        @pl.when(next_ref[j] != grp_ref[j])
        def _():
            fetch_weights(next_ref[j])

    def gather(tile, slot):
        def body(k, carry):
            _chunk_copy(rows_ref, src_ref[tile * TILE_CHUNKS + k], in_buf.at[slot], k, in_sem.at[slot]).start()
            return carry
        lax.fori_loop(0, TILE_CHUNKS, body, 0, unroll=True)

    def wait_gather(slot):
        def body(k, carry):
            _chunk_copy(rows_ref, 0, in_buf.at[slot], k, in_sem.at[slot]).wait()
            return carry
        lax.fori_loop(0, TILE_CHUNKS, body, 0, unroll=True)

    def scatter(tile, slot):
        def body(k, carry):
            _chunk_copy(out_buf.at[slot], k, y_ref, dst_ref[tile * TILE_CHUNKS + k], out_sem.at[slot]).start()
            return carry
        lax.fori_loop(0, TILE_CHUNKS, body, 0, unroll=True)

    def wait_scatter(slot):
        def body(k, carry):
            _chunk_copy(out_buf.at[slot], k, y_ref, 0, out_sem.at[slot]).wait()
            return carry
        lax.fori_loop(0, TILE_CHUNKS, body, 0, unroll=True)

    def zero_fill(wait):
        def per_tile(i, carry):
            def body(c, inner):
                copy = _chunk_copy(zero_buf, 0, y_ref, i * LOCAL_CHUNKS + c, zero_sem)
                if wait:
                    copy.wait()
                else:
                    copy.start()
                return inner
            return lax.fori_loop(used_ref[i], LOCAL_CHUNKS, body, carry)
        lax.fori_loop(0, used_ref.shape[0], per_tile, 0)

    @pl.when(j == 0)
    def _():
        zero_buf[...] = jnp.zeros_like(zero_buf)
        scratch0 = used_ref.shape[0] * LOCAL_CHUNKS
        for wait in (False, True):
            for k in range(2 * TILE_CHUNKS):
                copy = _chunk_copy(zero_buf, 0, y_ref, scratch0 + k, zero_sem)
                copy.wait() if wait else copy.start()
        zero_fill(wait=False)
        gather(0, 0)

    @pl.when(j + 1 < n_tiles)
    def _():
        gather(j + 1, (j + 1) % 2)

    @pl.when(j < n_tiles)
    def _():
        slot = j % 2
        wait_gather(slot)
        rows = in_buf[slot].reshape(EXPERT_TILE, in_buf.shape[-1])
        h = rows[:, 0:d_model]
        weights = rows[:, d_model:d_model + LANES].astype(F32) + rows[:, d_model + LANES:].astype(F32)
        experts = range(EXPERTS_PER_GROUP)
        up = [(_dot(h, w1_ref[k]), _dot(h, w3_ref[k])) for k in experts]
        hid = [(a * _sigmoid(a) * g * weights[:, k:k + 1]).astype(BF16) for k, (a, g) in zip(experts, up)]
        y = _dot(hid[0], w2_ref[0])
        for k in experts[1:]:
            y = y + _dot(hid[k], w2_ref[k])

        @pl.when(j >= 2)
        def _():
            wait_scatter(slot)

        out_buf[slot] = y.astype(BF16).reshape(TILE_CHUNKS, CHUNK, d_model)
        scatter(j, slot)

        @pl.when(j == n_tiles - 1)
        def _():
            @pl.when(j >= 1)
            def _():
                wait_scatter(1 - slot)
            wait_scatter(slot)
            zero_fill(wait=True)


def _combine_kernel(x1_ref, mod_ref, lpos_ref, y_ref, o_ref):
    tm = MOE_TILE
    local_row = lax.broadcasted_iota(jnp.int32, (tm, LOCAL_ROWS), 1).astype(F32)
    for t in range(x1_ref.shape[0] // tm):
        rows = slice(t * tm, (t + 1) * tm)
        lpos = jnp.broadcast_to(lpos_ref[t][0:1, :], (LANES, tm)).T[:, 0:1]
        unsort = jnp.where(local_row == lpos, 1.0, 0.0).astype(BF16)
        y = _dot(unsort, y_ref[t * LOCAL_ROWS:(t + 1) * LOCAL_ROWS, :])
        o_ref[rows, :] = x1_ref[rows, :] + mod_ref[5:6, :] * y


def _moe_plan(cnt):
    n_local = cnt.shape[0]
    chunks = (cnt + CHUNK - 1) // CHUNK
    used = jnp.sum(chunks, axis=1)
    local_off = jnp.cumsum(chunks, axis=1) - chunks
    tiles_g = (jnp.sum(chunks, axis=0) + TILE_CHUNKS - 1) // TILE_CHUNKS
    tile_end = jnp.cumsum(tiles_g)
    n_tiles = tile_end[-1]
    group_start = (tile_end - tiles_g) * TILE_CHUNKS
    seg_len = chunks.T.reshape(-1)
    seg_start = (group_start[:, None] + (jnp.cumsum(chunks, axis=0) - chunks).T).reshape(-1)
    seg_src = (jnp.arange(n_local)[None, :] * LOCAL_CHUNKS + local_off.T).reshape(-1)
    max_chunks = n_local * MOE_TILE // CHUNK + n_local * N_GROUPS + N_GROUPS * TILE_CHUNKS
    max_tiles = (max_chunks + TILE_CHUNKS - 1) // TILE_CHUNKS
    c = jnp.arange(max_tiles * TILE_CHUNKS)[:, None]
    within = c - seg_start[None, :]
    hit = jnp.logical_and(within >= 0, within < seg_len[None, :])
    valid = jnp.any(hit, axis=1)
    src = jnp.sum(jnp.where(hit, seg_src[None, :] + within, 0), axis=1)
    src = jnp.where(valid, src, LOCAL_CHUNKS - 1)
    slot_k = c[:, 0] % (2 * TILE_CHUNKS)
    dst = jnp.where(valid, src, n_local * LOCAL_CHUNKS + slot_k)
    j = jnp.minimum(jnp.arange(max_tiles), n_tiles - 1)
    grp = jnp.sum(j[:, None] >= tile_end[None, :], axis=1)
    gid = jnp.arange(N_GROUPS)
    later = jnp.where(jnp.logical_and(gid[None, :] > gid[:, None], tiles_g[None, :] > 0), gid[None, :], N_GROUPS)
    next_of = jnp.min(later, axis=1)
    next_of = jnp.where(next_of == N_GROUPS, gid, next_of)
    next_grp = jnp.sum(jnp.where(grp[:, None] == gid[None, :], next_of[None, :], 0), axis=1)
    i32 = lambda a: a.astype(jnp.int32)
    return i32(src), i32(dst), i32(grp), i32(next_grp), i32(n_tiles).reshape(1), i32(used)


def _moe(plan, rows_local, w1g, w3g, w2g):
    src, dst, grp, next_grp, n_tiles, used = plan
    D = w2g.shape[3]
    n_rows, cols = rows_local.shape
    n_chunks = n_rows // CHUNK
    hbm = pl.BlockSpec(memory_space=pl.ANY)
    group_weights = [w1g, w3g, w2g]
    y = pl.pallas_call(
        _moe_kernel,
        grid_spec=pltpu.PrefetchScalarGridSpec(
            num_scalar_prefetch=6,
            grid=(grp.shape[0],),
            in_specs=[hbm, hbm, hbm, hbm],
            out_specs=hbm,
            scratch_shapes=([pltpu.VMEM((2, TILE_CHUNKS, CHUNK, cols), BF16),
                             pltpu.VMEM((2, TILE_CHUNKS, CHUNK, D), BF16),
                             pltpu.VMEM((1, CHUNK, D), BF16)]
                            + [pltpu.VMEM(w.shape[1:], F32) for w in group_weights]
                            + [pltpu.VMEM(w.shape[1:], BF16) for w in group_weights]
                            + [pltpu.SemaphoreType.DMA((2,)), pltpu.SemaphoreType.DMA((2,)),
                               pltpu.SemaphoreType.DMA(()), pltpu.SemaphoreType.DMA(())])),
        out_shape=jax.ShapeDtypeStruct((n_chunks + 2 * TILE_CHUNKS, CHUNK, D), BF16),
        compiler_params=_params("arbitrary"),
        name="moe",
    )(src, dst, grp, next_grp, n_tiles, used, rows_local.reshape(n_chunks, CHUNK, cols), w1g, w3g, w2g)
    return y.reshape((n_chunks + 2 * TILE_CHUNKS) * CHUNK, D)


def _combine(x1, mod, lpos, y_local, *, T):
    N, D = x1.shape
    subs = COMBINE_SUBS
    tm = subs * MOE_TILE
    per_b = T // tm
    return pl.pallas_call(
        _combine_kernel,
        grid=(N // tm,),
        in_specs=[pl.BlockSpec((tm, D), lambda i: (i, 0)),
                  pl.BlockSpec((None, 6, D), lambda i: (i // per_b, 0, 0)),
                  pl.BlockSpec((subs, 8, MOE_TILE), lambda i: (i, 0, 0)),
                  pl.BlockSpec((subs * LOCAL_ROWS, D), lambda i: (i, 0))],
        out_specs=pl.BlockSpec((tm, D), lambda i: (i, 0)),
        out_shape=jax.ShapeDtypeStruct((N, D), F32),
        compiler_params=_params("arbitrary"),
        name="moe_combine",
    )(x1, mod, lpos, y_local)


def _block_diag(n, blk, value, dtype):
    r = np.arange(n)[:, None] // blk
    c = np.arange(n)[None, :] // blk
    return jnp.asarray(np.where(r == c, value, 0.0), dtype)


def _gla_constants():
    L = GLA_CHUNK
    i = np.arange(L)[:, None]
    j = np.arange(L)[None, :]
    tri = j <= i
    bdtri = np.logical_and(j <= i, i // GLA_SUB == j // GLA_SUB)
    cum = np.block([[tri, tri], [bdtri, bdtri]])
    col = np.arange(N_SUB * LANES)[None, :]
    qmask = np.stack([np.logical_and(col // LANES == i // GLA_SUB, (col % LANES) // GLA_DK == hh)
                      for hh in range(2)])
    bdms = _block_diag(2 * GLA_DV, GLA_DV, 1.0 / GLA_DV, BF16)
    d = np.arange(LANES)[:, None] // GLA_DK
    e = np.arange(2 * GLA_DV)[None, :] // GLA_DV
    smask = d == e
    return (jnp.asarray(cum, BF16), jnp.asarray(qmask, BF16), bdms, jnp.asarray(smask, F32))


def kernel(x, c, w_ada, b_ada, g_norm1, w_in, q_norm, k_norm, sinks, w_gk2, b_gk, g_gla_out, g_att_out,
           w_out, g_norm2, w_group, b_group, w_router, b_router, w1, w3, w2):
    B, T, D = x.shape
    N = B * T
    depth = w_ada.shape[0]
    cum, qmask, bdms, smask = _gla_constants()
    bdq = _block_diag(2 * LANES, HEAD_DIM, 1.0 / HEAD_DIM, BF16)
    bdk = _block_diag(ATT_KV, HEAD_DIM, 1.0 / HEAD_DIM, BF16)

    x2 = x.reshape(N, D)
    for l in range(depth):
        mod = _adaln_mod(c, w_ada[l], b_ada[l]).reshape(B, 6, D)

        wgk =jnp.concatenate([w_gk2[l], jnp.zeros((LANES - GLA_RANK, GLA_K), F32)], axis=0).astype(BF16)
        qa, ka, va, qg, kg, vg, la, og = _in_proj(
            x2, mod, g_norm1[l].reshape(1, D), w_in[l].T,
            jnp.tile(q_norm[l], ATT_HEADS).reshape(1, ATT_Q), jnp.tile(k_norm[l], ATT_KV_HEADS).reshape(1, ATT_KV),
            bdq, bdk, wgk, b_gk[l].reshape(1, GLA_K), T=T, tm=PROJ_TILE)

        y_att, y_gla = _mixers(sinks[l], qa, ka, va, g_att_out[l].reshape(1, ATT_Q),
                               qg, kg, vg, la, og, cum, qmask, bdms, smask,
                               jnp.tile(g_gla_out[l], GLA_HEADS).reshape(1, GLA_V), B=B, T=T)

        pad = LANES - N_GROUPS - N_EXPERTS
        wr_t = jnp.concatenate([w_group[l], w_router[l], jnp.zeros((D, pad), F32)], axis=1).T.astype(BF16)
        br_t = jnp.concatenate([b_group[l], b_router[l], jnp.zeros((pad,), F32)]).reshape(LANES, 1)
        striu = jnp.asarray(np.arange(MOE_TILE)[:, None] < np.arange(MOE_TILE)[None, :], BF16)
        x1, rows_local, lpos, cnt = _out_proj(y_att, y_gla, x2, mod, w_out[l].astype(BF16),
                                              g_norm2[l].reshape(1, D), wr_t, br_t, striu, T=T)
        plan = _moe_plan(cnt[:, :N_GROUPS, 0].astype(jnp.int32))
        by_group = lambda w: w.reshape((N_GROUPS, EXPERTS_PER_GROUP) + w.shape[1:])
        y_local = _moe(plan, rows_local, by_group(w1[l]), by_group(w3[l]), by_group(w2[l]))
        x2 = _combine(x1, mod, lpos, y_local, T=T)
    return x2.reshape(B, T, D)
```

```python
import jax
import jax.numpy as jnp
import numpy as np
from jax import lax
from jax.experimental import pallas as pl
from jax.experimental.pallas import tpu as pltpu

F32 = jnp.float32
BF16 = jnp.bfloat16

EPS = 1e-6
LOG2_E = 1.4426950408889634
ATT_HEADS = 8
ATT_KV_HEADS = 2
HEAD_DIM = 64
WINDOW = 128
ATT_Q = ATT_HEADS * HEAD_DIM
ATT_KV = ATT_KV_HEADS * HEAD_DIM
GLA_HEADS = 4
GLA_DK = 64
GLA_DV = 128
GLA_RANK = 16
GLA_NORMALIZER = 16.0
GLA_K = GLA_HEADS * GLA_DK
GLA_V = GLA_HEADS * GLA_DV
N_GROUPS = 4
EXPERTS_PER_GROUP = 4
N_EXPERTS = N_GROUPS * EXPERTS_PER_GROUP

LANES = 128
MOD_COLS = 768
PROJ_SPLITS = (512, 512)
PROJ_TILE = sum(PROJ_SPLITS)
OUT_SUBS = 4
COMBINE_SUBS = 4
ROUTER_ROWS = 24
SEQS_PER_STEP = 8
GLA_CHUNK = 128
GLA_SUB = 16
N_SUB = GLA_CHUNK // GLA_SUB
ROUTER_LANE0 = N_GROUPS
VMEM_LIMIT = 56 * 1024 * 1024
MOE_TILE = 256
EXPERT_TILE = 256
CHUNK = 16
TILE_CHUNKS = EXPERT_TILE // CHUNK
LOCAL_CHUNKS = (MOE_TILE + N_GROUPS * (CHUNK - 1)) // CHUNK + 2
LOCAL_ROWS = LOCAL_CHUNKS * CHUNK

_QA0, _KA0, _VA0 = 0, ATT_Q, ATT_Q + ATT_KV
_QG0 = _VA0 + ATT_KV
_KG0 = _QG0 + GLA_K
_VG0 = _KG0 + GLA_K
_OG0 = _VG0 + GLA_V
_LR0 = _OG0 + GLA_V
IN_COLS_PAD = _LR0 + LANES


def _dot(a, b):
    return jnp.dot(a, b, preferred_element_type=F32)


def _dot_nt(a, b):
    return lax.dot_general(a, b, (((1,), (1,)), ((), ())), preferred_element_type=F32)


def _sigmoid(x):
    return 1.0 / (1.0 + jnp.exp(-x))


def _params(*sem):
    return pltpu.CompilerParams(dimension_semantics=sem, vmem_limit_bytes=VMEM_LIMIT)


def _mod_kernel(c_ref, w_ref, b_ref, o_ref):
    c = c_ref[...]
    s = (c * _sigmoid(c)).astype(BF16)
    o_ref[...] = _dot(s, w_ref[...].astype(BF16)) + b_ref[...]


def _adaln_mod(c, w_ada, b_ada):
    B, D = c.shape
    n = w_ada.shape[1]
    tn = MOD_COLS
    return pl.pallas_call(
        _mod_kernel,
        grid=(n // tn,),
        in_specs=[pl.BlockSpec((B, D), lambda j: (0, 0)),
                  pl.BlockSpec((D, tn), lambda j: (0, j)),
                  pl.BlockSpec((1, tn), lambda j: (0, j))],
        out_specs=pl.BlockSpec((B, tn), lambda j: (0, j)),
        out_shape=jax.ShapeDtypeStruct((B, n), F32),
        compiler_params=_params("arbitrary"),
        name="adaln_mod",
    )(c, w_ada, b_ada.reshape(1, n))


def _inproj_kernel(x_ref, mod_ref, g1_ref, wf_ref, qn_ref, kn_ref, bdq_ref, bdk_ref, wgk_ref, bgk_ref,
                   qa_ref, ka_ref, va_ref, qg_ref, kg_ref, vg_ref, la_ref, og_ref, w_ref):
    subs = range(len(PROJ_SPLITS))
    starts = [sum(PROJ_SPLITS[:t]) for t in subs]
    rows = lambda t: slice(starts[t], starts[t] + PROJ_SPLITS[t])

    @pl.when(pl.program_id(0) == 0)
    def _():
        lr_src = _OG0
        w_ref[0:_OG0, :] = wf_ref[0:_OG0, :].astype(BF16)
        w_ref[_OG0:_LR0, :] = wf_ref[lr_src + GLA_RANK:lr_src + GLA_RANK + GLA_V, :].astype(BF16)
        w_ref[_LR0:_LR0 + GLA_RANK, :] = wf_ref[lr_src:lr_src + GLA_RANK, :].astype(BF16)
        w_ref[_LR0 + GLA_RANK:IN_COLS_PAD, :] = jnp.zeros((LANES - GLA_RANK, wf_ref.shape[1]), BF16)

    h = []
    for t in subs:
        x = x_ref[rows(t), :]
        ms = jnp.mean(x * x, axis=-1, keepdims=True)
        xn = x * lax.rsqrt(ms + EPS) * g1_ref[...]
        h.append((xn * (1.0 + mod_ref[1:2, :]) + mod_ref[0:1, :]).astype(BF16))

    proj = lambda t, c0, width: _dot_nt(h[t], w_ref[c0:c0 + width, :])
    gate_of = lambda lr: _dot(lr.astype(BF16), wgk_ref[...]) + bgk_ref[...]
    qa, kv, q_ms, k_ms, qk_g, vg, og, lr, gate = ({} for _ in range(9))
    for t in subs:
        qa[t] = proj(t, _QA0, ATT_Q)
        if t > 0:
            gate[t - 1] = gate_of(lr[t - 1])
        kv[t] = proj(t, _KA0, 2 * ATT_KV)
        q_sq = (qa[t] * qa[t]).astype(BF16)
        q_ms[t] = jnp.concatenate([_dot(q_sq[:, c:c + 2 * LANES], bdq_ref[...])
                                   for c in range(0, ATT_Q, 2 * LANES)], axis=1)
        qk_g[t] = proj(t, _QG0, 2 * GLA_K)
        k = kv[t][:, 0:ATT_KV]
        k_ms[t] = _dot((k * k).astype(BF16), bdk_ref[...])
        vg[t] = proj(t, _VG0, GLA_V)
        og[t] = proj(t, _OG0, GLA_V)
        lr[t] = proj(t, _LR0, LANES)
    gate[subs[-1]] = gate_of(lr[subs[-1]])

    for t in subs:
        low = lax.broadcasted_iota(jnp.int32, (PROJ_SPLITS[t], ATT_KV), 1) < HEAD_DIM
        qa_ref[rows(t), :] = (qa[t] * lax.rsqrt(q_ms[t] + EPS) * qn_ref[...]
                              * (HEAD_DIM ** -0.5 * LOG2_E)).astype(BF16)
        k = kv[t][:, 0:ATT_KV] * lax.rsqrt(k_ms[t] + EPS) * kn_ref[...]
        v = kv[t][:, ATT_KV:2 * ATT_KV]
        for src, dst in ((k, ka_ref), (v, va_ref)):
            swapped = pltpu.roll(src, HEAD_DIM, axis=1)
            dst[rows(t), 0:LANES] = jnp.where(low, src, swapped).astype(BF16)
            dst[rows(t), LANES:2 * LANES] = jnp.where(low, swapped, src).astype(BF16)
        qg_ref[rows(t), :] = (qk_g[t][:, 0:GLA_K] * (GLA_DK ** -0.5)).astype(BF16)
        kg_ref[rows(t), :] = qk_g[t][:, GLA_K:2 * GLA_K].astype(BF16)
        vg_ref[rows(t), :] = vg[t].astype(BF16)
        og_ref[rows(t), :] = (og[t] * _sigmoid(og[t])).astype(BF16)
        log_sig = jnp.minimum(gate[t], 0.0) - jnp.log(1.0 + jnp.exp(-jnp.abs(gate[t])))
        la_ref[rows(t), :] = log_sig * (1.0 / GLA_NORMALIZER)


def _in_proj(x2, mod, g1, w_in_t, qn, kn, bdq, bdk, wgk, bgk, *, T, tm):
    N, D = x2.shape
    per_b = T // tm
    const = lambda shape: pl.BlockSpec(shape, lambda i: (0,) * len(shape))
    rows = lambda c: pl.BlockSpec((tm, c), lambda i: (i, 0))
    outs = [(ATT_Q, BF16), (2 * ATT_KV, BF16), (2 * ATT_KV, BF16), (GLA_K, BF16), (GLA_K, BF16),
            (GLA_V, BF16), (GLA_K, F32), (GLA_V, BF16)]
    return pl.pallas_call(
        _inproj_kernel,
        grid=(N // tm,),
        in_specs=[rows(D),
                  pl.BlockSpec((None, 6, D), lambda i: (i // per_b, 0, 0)),
                  const((1, D)),
                  pl.BlockSpec(w_in_t.shape, lambda i: (0, 0), pipeline_mode=pl.Buffered(1)),
                  const(qn.shape), const(kn.shape),
                  const(bdq.shape), const(bdk.shape), const(wgk.shape), const(bgk.shape)],
        out_specs=[rows(c) for c, _ in outs],
        out_shape=[jax.ShapeDtypeStruct((N, c), dt) for c, dt in outs],
        scratch_shapes=[pltpu.VMEM((IN_COLS_PAD, D), BF16)],
        compiler_params=_params("arbitrary"),
        name="in_proj",
    )(x2, mod, g1, w_in_t, qn, kn, bdq, bdk, wgk, bgk)


def _attn_stages(seqs, sinks_ref, q_ref, kc_ref, vc_ref, gatt_ref, o_ref, kp_ref, vp_ref):
    blk = WINDOW
    first = pl.program_id(1) == 0
    qi = lax.broadcasted_iota(jnp.int32, (blk, blk), 0)
    cj = lax.broadcasted_iota(jnp.int32, (blk, blk), 1)
    from_prev = cj > qi
    dist = (qi - cj + jnp.where(from_prev, blk, 0)).astype(F32)
    no_prev = jnp.where(jnp.logical_and(from_prev, first), -1e30, 0.0)
    low = cj < HEAD_DIM
    half = (jnp.where(low, 1.0, 0.0).astype(BF16), jnp.where(low, 0.0, 1.0).astype(BF16))
    half2 = tuple(jnp.concatenate([m, m], axis=0) for m in half)
    prev_mask = jnp.where(from_prev, 1.0, 0.0).astype(BF16)
    cur_mask = jnp.where(from_prev, 0.0, 1.0).astype(BF16)

    n_pairs = ATT_HEADS // 2
    pairs_per_kv = n_pairs // ATT_KV_HEADS
    units = [(bi, j) for bi in seqs for j in range(n_pairs)]

    def kv_blocks(bi, g):
        lanes = slice(g * LANES, (g + 1) * LANES)
        return (kp_ref[bi, :, lanes], vp_ref[bi, :, lanes]), (kc_ref[bi, :, lanes], vc_ref[bi, :, lanes])

    scores = {}
    for bi, j in units:
        (kp, _), (kc, _) = kv_blocks(bi, j // pairs_per_kv)
        k_both = jnp.concatenate([kp, kc], axis=0)
        qp = q_ref[bi, :, j * LANES:(j + 1) * LANES]
        for p in range(2):
            scores[bi, j, p] = _dot_nt(qp, k_both * half2[p])
    yield

    probs, sink_terms = {}, {}
    for bi, j in units:
        for p in range(2):
            h = 2 * j + p
            slope = 2.0 ** (-8.0 * (h + 1) / ATT_HEADS) * LOG2_E
            s_both = scores[bi, j, p]
            s = jnp.where(from_prev, s_both[:, 0:blk], s_both[:, blk:2 * blk]) - slope * dist + no_prev
            sink = sinks_ref[h] * LOG2_E
            m = jnp.maximum(jnp.max(s, axis=-1, keepdims=True), sink)
            probs[bi, j, p] = jnp.exp2(s - m)
            sink_terms[bi, j, p] = jnp.exp2(sink - m)
    yield

    outs = {}
    for bi, j in units:
        (_, vp), (_, vc) = kv_blocks(bi, j // pairs_per_kv)
        v_stack = jnp.concatenate([jnp.concatenate([v * half[p], half[p]], axis=1)
                                   for v in (vp, vc) for p in range(2)], axis=0)
        e = [probs[bi, j, p].astype(BF16) for p in range(2)]
        p_all = jnp.concatenate([x * m for m in (prev_mask, cur_mask) for x in e], axis=1)
        pv = _dot(p_all, v_stack)
        den = pv[:, LANES:2 * LANES] + jnp.where(low, sink_terms[bi, j, 0], sink_terms[bi, j, 1])
        outs[bi, j] = pv[:, 0:LANES] / den
    yield

    for bi in seqs:
        o = jnp.concatenate([outs[bi, j] for j in range(n_pairs)], axis=1)
        ms = jnp.mean(o * o, axis=-1, keepdims=True)
        o_ref[bi] = (o * lax.rsqrt(ms + EPS) * gatt_ref[...]).astype(BF16)
        kp_ref[bi] = kc_ref[bi]
        vp_ref[bi] = vc_ref[bi]


def _gla_stages(seqs, q_ref, k_ref, v_ref, la_ref, og_ref, cum_ref, bdms_ref, smask_ref,
                ggla_ref, o_ref, state_ref):
    L = GLA_CHUNK
    lane = lax.broadcasted_iota(jnp.int32, (GLA_SUB, LANES), 1)
    head_mask = [jnp.where(lane // GLA_DK == hh, 1.0, 0.0).astype(BF16) for hh in range(2)]
    causal = lax.broadcasted_iota(jnp.int32, (L, L), 0) >= lax.broadcasted_iota(jnp.int32, (L, L), 1)
    units = [(s, pair) for s in seqs for pair in range(GLA_HEADS // 2)]
    kl = lambda pair: slice(pair * LANES, (pair + 1) * LANES)
    vl = lambda pair: slice(pair * 2 * GLA_DV, (pair + 1) * 2 * GLA_DV)

    b, b_in = {}, {}
    for s in seqs:
        la = la_ref[s]
        la_hi = la.astype(BF16)
        la_lo = (la - la_hi.astype(F32)).astype(BF16)
        sums = _dot(cum_ref[...], jnp.concatenate([la_hi, la_lo], axis=0)) * LOG2_E
        b[s] = sums[0:L, :]
        b_in[s] = sums[L:2 * L, :]
    yield

    q_both, keys, q_dec, k_dec_t, b_last = {}, {}, {}, {}, {}
    for s in seqs:
        ref = b[s] - b_in[s]
        b_last[s] = b[s][L - 1:L, :]
        q = q_ref[s].astype(F32)
        k = k_ref[s].astype(F32)
        q_in = (q * jnp.exp2(b_in[s])).astype(BF16)
        q_dec[s] = (q * jnp.exp2(b[s])).astype(BF16)
        k_dec = k * jnp.exp2(b_last[s] - b[s])
        for pair in range(GLA_HEADS // 2):
            k_p, b_p, ref_p = k[:, kl(pair)], b[s][:, kl(pair)], ref[:, kl(pair)]
            expanded = []
            for g in range(N_SUB):
                top = (g + 1) * GLA_SUB
                live = (k_p[0:top, :] * jnp.exp2(ref_p[g * GLA_SUB:g * GLA_SUB + 1, :] - b_p[0:top, :])).astype(BF16)
                expanded.append(live if top == L else
                                jnp.concatenate([live, jnp.zeros((L - top, LANES), BF16)], axis=0))
            keys[s, pair] = jnp.concatenate(expanded, axis=1)
            q_p = q_in[:, kl(pair)]
            zero_group = jnp.zeros((GLA_SUB, LANES), BF16)
            heads = []
            for hh in range(2):
                blocks = []
                for g in range(N_SUB):
                    piece = q_p[g * GLA_SUB:(g + 1) * GLA_SUB, :] * head_mask[hh]
                    blocks.append(jnp.concatenate([zero_group] * g + [piece] + [zero_group] * (N_SUB - 1 - g), axis=0))
                heads.append(jnp.concatenate(blocks, axis=1))
            q_both[s, pair] = jnp.concatenate(heads, axis=0)
            k_dec_t[s, pair] = k_dec[:, kl(pair)].T.astype(BF16)
    yield

    scores = {u: _dot_nt(q_both[u], keys[u]) for u in units}
    yield

    outs, updates = {}, {}
    for s, pair in units:
        v_p = v_ref[s, :, vl(pair)]
        o_parts = []
        for hh in range(2):
            a = jnp.where(causal, scores[s, pair][hh * L:(hh + 1) * L, :], 0.0).astype(BF16)
            o_parts.append(_dot(a, v_p[:, hh * GLA_DV:(hh + 1) * GLA_DV]))
        state = state_ref[s, pair]
        outs[s, pair] = jnp.concatenate(o_parts, axis=1) + _dot(q_dec[s][:, kl(pair)], state.astype(BF16))
        updates[s, pair] = _dot(k_dec_t[s, pair], v_p)
    yield

    for s, pair in units:
        decay = jnp.broadcast_to(jnp.exp2(b_last[s][:, kl(pair)]), (LANES, LANES)).T
        state_ref[s, pair] = (state_ref[s, pair] * jnp.concatenate([decay, decay], axis=1)
                              + updates[s, pair] * smask_ref[...])
        o = outs[s, pair]
        ms = _dot((o * o).astype(BF16), bdms_ref[...])
        y = o * lax.rsqrt(ms + EPS) * ggla_ref[:, vl(pair)] * og_ref[s, :, vl(pair)].astype(F32)
        o_ref[s, :, vl(pair)] = y.astype(BF16)


def _mixers_kernel(sinks_ref, qa_ref, ka_ref, va_ref, gatt_ref,
                   qg_ref, kg_ref, vg_ref, la_ref, og_ref, cum_ref, bdms_ref, smask_ref, ggla_ref,
                   ya_ref, yg_ref, kp_ref, vp_ref, state_ref):
    @pl.when(pl.program_id(1) == 0)
    def _():
        state_ref[...] = jnp.zeros_like(state_ref)
        kp_ref[...] = jnp.zeros_like(kp_ref)
        vp_ref[...] = jnp.zeros_like(vp_ref)

    n_seq = qa_ref.shape[0]
    waiting = []
    for group in (tuple(range(0, n_seq // 2)), tuple(range(n_seq // 2, n_seq))):
        waiting.append([
            _attn_stages(group, sinks_ref, qa_ref, ka_ref, va_ref, gatt_ref, ya_ref, kp_ref, vp_ref),
            _gla_stages(group, qg_ref, kg_ref, vg_ref, la_ref, og_ref, cum_ref, bdms_ref, smask_ref,
                        ggla_ref, yg_ref, state_ref)])
    pending = []
    while pending or waiting:
        if waiting:
            pending.extend(waiting.pop(0))
        for stages in list(pending):
            if next(stages, "done") == "done":
                pending.remove(stages)


def _mixers(sinks, qa, ka, va, gatt, qg, kg, vg, la, og, cum, bdms, smask, ggla, *, B, T):
    assert GLA_CHUNK == WINDOW
    L = WINDOW
    nb = SEQS_PER_STEP
    seq = lambda a: a.reshape(B, T, a.shape[-1])
    rows = lambda c: pl.BlockSpec((nb, L, c), lambda b, i: (b, i, 0))
    const = lambda a: pl.BlockSpec(a.shape, lambda b, i: (0,) * a.ndim)
    y_att, y_gla = pl.pallas_call(
        _mixers_kernel,
        grid=(B // nb, T // L),
        in_specs=[pl.BlockSpec(memory_space=pltpu.SMEM),
                  rows(ATT_Q), rows(2 * ATT_KV), rows(2 * ATT_KV), const(gatt),
                  rows(GLA_K), rows(GLA_K), rows(GLA_V), rows(GLA_K), rows(GLA_V),
                  const(cum), const(bdms), const(smask), const(ggla)],
        out_specs=[rows(ATT_Q), rows(GLA_V)],
        out_shape=[jax.ShapeDtypeStruct((B, T, ATT_Q), BF16), jax.ShapeDtypeStruct((B, T, GLA_V), BF16)],
        scratch_shapes=[pltpu.VMEM((nb, L, 2 * ATT_KV), BF16), pltpu.VMEM((nb, L, 2 * ATT_KV), BF16),
                        pltpu.VMEM((nb, GLA_HEADS // 2, LANES, 2 * GLA_DV), F32)],
        compiler_params=_params("arbitrary", "arbitrary"),
        name="mixers",
    )(sinks, seq(qa), seq(ka), seq(va), gatt, seq(qg), seq(kg), seq(vg), seq(la), seq(og),
      cum, bdms, smask, ggla)
    return y_att.reshape(B * T, ATT_Q), y_gla.reshape(B * T, GLA_V)


def _route(logits_t):
    lt = logits_t[0:ROUTER_ROWS, :]
    row = lax.broadcasted_iota(jnp.int32, lt.shape, 0)
    neg_inf = -jnp.inf
    g_log = jnp.where(row < N_GROUPS, lt, neg_inf)
    g_max = jnp.max(g_log, axis=0, keepdims=True)
    g_sel = jnp.min(jnp.where(g_log == g_max, row, LANES), axis=0, keepdims=True)
    g_sum = jnp.sum(jnp.where(row < N_GROUPS, jnp.exp(lt - g_max), 0.0), axis=0, keepdims=True)
    p_group = 1.0 / g_sum
    e_lo = ROUTER_LANE0 + EXPERTS_PER_GROUP * g_sel
    in_group = jnp.logical_and(row >= e_lo, row < e_lo + EXPERTS_PER_GROUP)
    e_log = jnp.where(in_group, lt, neg_inf)
    e_max = jnp.max(e_log, axis=0, keepdims=True)
    top1 = jnp.min(jnp.where(e_log == e_max, row, LANES), axis=0, keepdims=True)
    e_log2 = jnp.where(row == top1, neg_inf, e_log)
    e_max2 = jnp.max(e_log2, axis=0, keepdims=True)
    top2 = jnp.min(jnp.where(e_log2 == e_max2, row, LANES), axis=0, keepdims=True)
    ratio = jnp.exp(e_max2 - e_max)
    w_top1 = p_group / (1.0 + ratio)
    w_top2 = p_group * ratio / (1.0 + ratio)
    row8 = lax.broadcasted_iota(jnp.int32, (8, lt.shape[1]), 0)
    weights = jnp.where(row8 == top1 - e_lo, w_top1, 0.0) + jnp.where(row8 == top2 - e_lo, w_top2, 0.0)
    return g_sel, weights


def _outproj_kernel(ya_ref, yg_ref, x_ref, mod_ref, wo_ref, g2_ref, wrt_ref, brt_ref, striu_ref,
                    x1_ref, row_ref, lpos_ref, cnt_ref):
    tm = MOE_TILE
    subs = range(x_ref.shape[0] // tm)
    rows = lambda t: slice(t * tm, (t + 1) * tm)

    mix = [_dot(ya_ref[rows(t), :], wo_ref[0:ATT_Q, :]) + _dot(yg_ref[rows(t), :], wo_ref[ATT_Q:ATT_Q + GLA_V, :])
           for t in subs]
    h2b = []
    for t in subs:
        x1 = x_ref[rows(t), :] + mod_ref[2:3, :] * mix[t]
        x1_ref[rows(t), :] = x1
        ms = jnp.mean(x1 * x1, axis=-1, keepdims=True)
        h2 = (x1 * lax.rsqrt(ms + EPS) * g2_ref[...]) * (1.0 + mod_ref[4:5, :]) + mod_ref[3:4, :]
        h2b.append(h2.astype(BF16))
    logits_t = [_dot_nt(wrt_ref[...], h2b[t]) + brt_ref[...] for t in subs]

    routed = [_route(logits_t[t]) for t in subs]
    row8 = lax.broadcasted_iota(jnp.int32, (8, tm), 0)
    onehot = [jnp.where(row8 == routed[t][0], 1.0, 0.0) for t in subs]
    before = [_dot(onehot[t].astype(BF16), striu_ref[...]) for t in subs]

    local_row = lax.broadcasted_iota(jnp.int32, (LOCAL_ROWS, tm), 0).astype(F32)
    pad_rows = jnp.zeros((LANES - 8, tm), F32)
    for t in subs:
        count = jnp.sum(onehot[t], axis=1, keepdims=True)
        cnt_ref[t] = jnp.broadcast_to(count, (8, LANES))
        padded = jnp.broadcast_to(jnp.floor((count + (CHUNK - 1.0)) * (1.0 / CHUNK)) * CHUNK, (8, tm))
        start = jnp.zeros((8, tm), F32)
        for shift in range(1, N_GROUPS):
            start = start + jnp.where(row8 >= shift, pltpu.roll(padded, shift, axis=0), 0.0)
        lpos = jnp.sum(onehot[t] * (before[t] + start), axis=0, keepdims=True)
        lpos_ref[t] = jnp.broadcast_to(lpos, (8, tm))
        weights = jnp.concatenate([routed[t][1], pad_rows], axis=0).T
        w_hi = weights.astype(BF16)
        w_lo = (weights - w_hi.astype(F32)).astype(BF16)
        perm = jnp.where(local_row == lpos, 1.0, 0.0).astype(BF16)
        row_ref[t * LOCAL_ROWS:(t + 1) * LOCAL_ROWS, :] = _dot(
            perm, jnp.concatenate([h2b[t], w_hi, w_lo], axis=1)).astype(BF16)


def _out_proj(ya, yg, x2, mod, wo, g2, wr, br, stril, *, T):
    N, D = x2.shape
    subs = OUT_SUBS
    tm = subs * MOE_TILE
    per_b = T // tm
    rows = lambda c: pl.BlockSpec((tm, c), lambda i: (i, 0))
    const = lambda a: pl.BlockSpec(a.shape, lambda i: (0,) * a.ndim)
    return pl.pallas_call(
        _outproj_kernel,
        grid=(N // tm,),
        in_specs=[rows(ATT_Q), rows(GLA_V), rows(D),
                  pl.BlockSpec((None, 6, D), lambda i: (i // per_b, 0, 0)),
                  const(wo), const(g2), const(wr), const(br), const(stril)],
        out_specs=[rows(D), pl.BlockSpec((subs * LOCAL_ROWS, D + 2 * LANES), lambda i: (i, 0)),
                   pl.BlockSpec((subs, 8, MOE_TILE), lambda i: (i, 0, 0)),
                   pl.BlockSpec((subs, 8, LANES), lambda i: (i, 0, 0))],
        out_shape=[jax.ShapeDtypeStruct((N, D), F32),
                   jax.ShapeDtypeStruct((N // MOE_TILE * LOCAL_ROWS, D + 2 * LANES), BF16),
                   jax.ShapeDtypeStruct((N // MOE_TILE, 8, MOE_TILE), F32),
                   jax.ShapeDtypeStruct((N // MOE_TILE, 8, LANES), F32)],
        compiler_params=_params("arbitrary"),
        name="out_proj",
    )(ya, yg, x2, mod, wo, g2, wr, br, stril)


def _chunk_copy(src_ref, src_chunk, dst_ref, dst_chunk, sem):
    return pltpu.make_async_copy(src_ref.at[src_chunk], dst_ref.at[dst_chunk], sem)


def _moe_kernel(src_ref, dst_ref, grp_ref, next_ref, nt_ref, used_ref, rows_ref, w1f_ref, w3f_ref, w2f_ref, y_ref,
                in_buf, out_buf, zero_buf, st1_ref, st3_ref, st2_ref, w1_ref, w3_ref, w2_ref,
                in_sem, out_sem, zero_sem, w_sem):
    j = pl.program_id(0)
    n_tiles = nt_ref[0]
    d_model = w2_ref.shape[2]

    stages = ((w1f_ref, st1_ref, w1_ref), (w3f_ref, st3_ref, w3_ref), (w2f_ref, st2_ref, w2_ref))

    def fetch_weights(group):
        for hbm, stage, _ in stages:
            pltpu.make_async_copy(hbm.at[group], stage, w_sem).start()

    def enter_group():
        for hbm, stage, dst in stages:
            pltpu.make_async_copy(hbm.at[0], stage, w_sem).wait()
        for hbm, stage, dst in stages:
            for k in range(EXPERTS_PER_GROUP):
                dst[k] = stage[k].astype(BF16)

        @pl.when(next_ref[j] != grp_ref[j])
        def _():
            fetch_weights(next_ref[j])

    def gather(tile, slot):
        def body(k, carry):
            _chunk_copy(rows_ref, src_ref[tile * TILE_CHUNKS + k], in_buf.at[slot], k, in_sem.at[slot]).start()
            return carry
        lax.fori_loop(0, TILE_CHUNKS, body, 0, unroll=True)

    def wait_gather(slot):
        def body(k, carry):
            _chunk_copy(rows_ref, 0, in_buf.at[slot], k, in_sem.at[slot]).wait()
            return carry
        lax.fori_loop(0, TILE_CHUNKS, body, 0, unroll=True)

    def scatter(tile, slot):
        def body(k, carry):
            _chunk_copy(out_buf.at[slot], k, y_ref, dst_ref[tile * TILE_CHUNKS + k], out_sem.at[slot]).start()
            return carry
        lax.fori_loop(0, TILE_CHUNKS, body, 0, unroll=True)

    def wait_scatter(slot):
        def body(k, carry):
            _chunk_copy(out_buf.at[slot], k, y_ref, 0, out_sem.at[slot]).wait()
            return carry
        lax.fori_loop(0, TILE_CHUNKS, body, 0, unroll=True)

    def zero_fill(wait):
        def per_tile(i, carry):
            def body(c, inner):
                copy = _chunk_copy(zero_buf, 0, y_ref, i * LOCAL_CHUNKS + c, zero_sem)
                if wait:
                    copy.wait()
                else:
                    copy.start()
                return inner
            return lax.fori_loop(used_ref[i], LOCAL_CHUNKS, body, carry)
        lax.fori_loop(0, used_ref.shape[0], per_tile, 0)

    @pl.when(j == 0)
    def _():
        fetch_weights(grp_ref[0])
        zero_buf[...] = jnp.zeros_like(zero_buf)
        scratch0 = used_ref.shape[0] * LOCAL_CHUNKS
        for wait in (False, True):
            for k in range(2 * TILE_CHUNKS):
                copy = _chunk_copy(zero_buf, 0, y_ref, scratch0 + k, zero_sem)
                copy.wait() if wait else copy.start()
        zero_fill(wait=False)
        gather(0, 0)

    @pl.when(jnp.logical_or(j == 0, grp_ref[j] != grp_ref[jnp.maximum(j - 1, 0)]))
    def _():
        enter_group()

    @pl.when(j + 1 < n_tiles)
    def _():
        gather(j + 1, (j + 1) % 2)

    @pl.when(j < n_tiles)
    def _():
        slot = j % 2
        wait_gather(slot)
        rows = in_buf[slot].reshape(EXPERT_TILE, in_buf.shape[-1])
        h = rows[:, 0:d_model]
        weights = rows[:, d_model:d_model + LANES].astype(F32) + rows[:, d_model + LANES:].astype(F32)
        experts = range(EXPERTS_PER_GROUP)
        up = [(_dot(h, w1_ref[k]), _dot(h, w3_ref[k])) for k in experts]
        hid = [(a * _sigmoid(a) * g * weights[:, k:k + 1]).astype(BF16) for k, (a, g) in zip(experts, up)]
        y = _dot(hid[0], w2_ref[0])
        for k in experts[1:]:
            y = y + _dot(hid[k], w2_ref[k])

        @pl.when(j >= 2)
        def _():
            wait_scatter(slot)

        out_buf[slot] = y.astype(BF16).reshape(TILE_CHUNKS, CHUNK, d_model)
        scatter(j, slot)

        @pl.when(j == n_tiles - 1)
        def _():
            @pl.when(j >= 1)
            def _():
                wait_scatter(1 - slot)
            wait_scatter(slot)
            zero_fill(wait=True)


def _combine_kernel(x1_ref, mod_ref, lpos_ref, y_ref, o_ref):
    tm = MOE_TILE
    local_row = lax.broadcasted_iota(jnp.int32, (tm, LOCAL_ROWS), 1).astype(F32)
    for t in range(x1_ref.shape[0] // tm):
        rows = slice(t * tm, (t + 1) * tm)
        lpos = jnp.broadcast_to(lpos_ref[t][0:1, :], (LANES, tm)).T[:, 0:1]
        unsort = jnp.where(local_row == lpos, 1.0, 0.0).astype(BF16)
        y = _dot(unsort, y_ref[t * LOCAL_ROWS:(t + 1) * LOCAL_ROWS, :])
        o_ref[rows, :] = x1_ref[rows, :] + mod_ref[5:6, :] * y


def _moe_plan(cnt):
    n_local = cnt.shape[0]
    chunks = (cnt + CHUNK - 1) // CHUNK
    used = jnp.sum(chunks, axis=1)
    local_off = jnp.cumsum(chunks, axis=1) - chunks
    tiles_g = (jnp.sum(chunks, axis=0) + TILE_CHUNKS - 1) // TILE_CHUNKS
    tile_end = jnp.cumsum(tiles_g)
    n_tiles = tile_end[-1]
    group_start = (tile_end - tiles_g) * TILE_CHUNKS
    seg_len = chunks.T.reshape(-1)
    seg_start = (group_start[:, None] + (jnp.cumsum(chunks, axis=0) - chunks).T).reshape(-1)
    seg_src = (jnp.arange(n_local)[None, :] * LOCAL_CHUNKS + local_off.T).reshape(-1)
    max_chunks = n_local * MOE_TILE // CHUNK + n_local * N_GROUPS + N_GROUPS * TILE_CHUNKS
    max_tiles = (max_chunks + TILE_CHUNKS - 1) // TILE_CHUNKS
    c = jnp.arange(max_tiles * TILE_CHUNKS)[:, None]
    within = c - seg_start[None, :]
    hit = jnp.logical_and(within >= 0, within < seg_len[None, :])
    valid = jnp.any(hit, axis=1)
    src = jnp.sum(jnp.where(hit, seg_src[None, :] + within, 0), axis=1)
    src = jnp.where(valid, src, LOCAL_CHUNKS - 1)
    slot_k = c[:, 0] % (2 * TILE_CHUNKS)
    dst = jnp.where(valid, src, n_local * LOCAL_CHUNKS + slot_k)
    j = jnp.minimum(jnp.arange(max_tiles), n_tiles - 1)
    grp = jnp.sum(j[:, None] >= tile_end[None, :], axis=1)
    gid = jnp.arange(N_GROUPS)
    later = jnp.where(jnp.logical_and(gid[None, :] > gid[:, None], tiles_g[None, :] > 0), gid[None, :], N_GROUPS)
    next_of = jnp.min(later, axis=1)
    next_of = jnp.where(next_of == N_GROUPS, gid, next_of)
    next_grp = jnp.sum(jnp.where(grp[:, None] == gid[None, :], next_of[None, :], 0), axis=1)
    i32 = lambda a: a.astype(jnp.int32)
    return i32(src), i32(dst), i32(grp), i32(next_grp), i32(n_tiles).reshape(1), i32(used)


def _moe(plan, rows_local, w1g, w3g, w2g):
    src, dst, grp, next_grp, n_tiles, used = plan
    D = w2g.shape[3]
    n_rows, cols = rows_local.shape
    n_chunks = n_rows // CHUNK
    hbm = pl.BlockSpec(memory_space=pl.ANY)
    group_weights = [w1g, w3g, w2g]
    y = pl.pallas_call(
        _moe_kernel,
        grid_spec=pltpu.PrefetchScalarGridSpec(
            num_scalar_prefetch=6,
            grid=(grp.shape[0],),
            in_specs=[hbm, hbm, hbm, hbm],
            out_specs=hbm,
            scratch_shapes=([pltpu.VMEM((2, TILE_CHUNKS, CHUNK, cols), BF16),
                             pltpu.VMEM((2, TILE_CHUNKS, CHUNK, D), BF16),
                             pltpu.VMEM((1, CHUNK, D), BF16)]
                            + [pltpu.VMEM(w.shape[1:], F32) for w in group_weights]
                            + [pltpu.VMEM(w.shape[1:], BF16) for w in group_weights]
                            + [pltpu.SemaphoreType.DMA((2,)), pltpu.SemaphoreType.DMA((2,)),
                               pltpu.SemaphoreType.DMA(()), pltpu.SemaphoreType.DMA(())])),
        out_shape=jax.ShapeDtypeStruct((n_chunks + 2 * TILE_CHUNKS, CHUNK, D), BF16),
        compiler_params=_params("arbitrary"),
        name="moe",
    )(src, dst, grp, next_grp, n_tiles, used, rows_local.reshape(n_chunks, CHUNK, cols), w1g, w3g, w2g)
    return y.reshape((n_chunks + 2 * TILE_CHUNKS) * CHUNK, D)


def _combine(x1, mod, lpos, y_local, *, T):
    N, D = x1.shape
    subs = COMBINE_SUBS
    tm = subs * MOE_TILE
    per_b = T // tm
    return pl.pallas_call(
        _combine_kernel,
        grid=(N // tm,),
        in_specs=[pl.BlockSpec((tm, D), lambda i: (i, 0)),
                  pl.BlockSpec((None, 6, D), lambda i: (i // per_b, 0, 0)),
                  pl.BlockSpec((subs, 8, MOE_TILE), lambda i: (i, 0, 0)),
                  pl.BlockSpec((subs * LOCAL_ROWS, D), lambda i: (i, 0))],
        out_specs=pl.BlockSpec((tm, D), lambda i: (i, 0)),
        out_shape=jax.ShapeDtypeStruct((N, D), F32),
        compiler_params=_params("arbitrary"),
        name="moe_combine",
    )(x1, mod, lpos, y_local)


def _block_diag(n, blk, value, dtype):
    r = np.arange(n)[:, None] // blk
    c = np.arange(n)[None, :] // blk
    return jnp.asarray(np.where(r == c, value, 0.0), dtype)


def _gla_constants():
    L = GLA_CHUNK
    i = np.arange(L)[:, None]
    j = np.arange(L)[None, :]
    tri = j <= i
    bdtri = np.logical_and(j <= i, i // GLA_SUB == j // GLA_SUB)
    cum = np.block([[tri, tri], [bdtri, bdtri]])
    bdms = _block_diag(2 * GLA_DV, GLA_DV, 1.0 / GLA_DV, BF16)
    d = np.arange(LANES)[:, None] // GLA_DK
    e = np.arange(2 * GLA_DV)[None, :] // GLA_DV
    smask = d == e
    return jnp.asarray(cum, BF16), bdms, jnp.asarray(smask, F32)


def kernel(x, c, w_ada, b_ada, g_norm1, w_in, q_norm, k_norm, sinks, w_gk2, b_gk, g_gla_out, g_att_out,
           w_out, g_norm2, w_group, b_group, w_router, b_router, w1, w3, w2):
    B, T, D = x.shape
    N = B * T
    depth = w_ada.shape[0]
    cum, bdms, smask = _gla_constants()
    bdq = _block_diag(2 * LANES, HEAD_DIM, 1.0 / HEAD_DIM, BF16)
    bdk = _block_diag(ATT_KV, HEAD_DIM, 1.0 / HEAD_DIM, BF16)

    x2 = x.reshape(N, D)
    for l in range(depth):
        mod = _adaln_mod(c, w_ada[l], b_ada[l]).reshape(B, 6, D)

        wgk =jnp.concatenate([w_gk2[l], jnp.zeros((LANES - GLA_RANK, GLA_K), F32)], axis=0).astype(BF16)
        qa, ka, va, qg, kg, vg, la, og = _in_proj(
            x2, mod, g_norm1[l].reshape(1, D), w_in[l].T,
            jnp.tile(q_norm[l], ATT_HEADS).reshape(1, ATT_Q), jnp.tile(k_norm[l], ATT_KV_HEADS).reshape(1, ATT_KV),
            bdq, bdk, wgk, b_gk[l].reshape(1, GLA_K), T=T, tm=PROJ_TILE)

        y_att, y_gla = _mixers(sinks[l], qa, ka, va, g_att_out[l].reshape(1, ATT_Q),
                               qg, kg, vg, la, og, cum, bdms, smask,
                               jnp.tile(g_gla_out[l], GLA_HEADS).reshape(1, GLA_V), B=B, T=T)

        pad = LANES - N_GROUPS - N_EXPERTS
        wr_t = jnp.concatenate([w_group[l], w_router[l], jnp.zeros((D, pad), F32)], axis=1).T.astype(BF16)
        br_t = jnp.concatenate([b_group[l], b_router[l], jnp.zeros((pad,), F32)]).reshape(LANES, 1)
        striu = jnp.asarray(np.arange(MOE_TILE)[:, None] < np.arange(MOE_TILE)[None, :], BF16)
        x1, rows_local, lpos, cnt = _out_proj(y_att, y_gla, x2, mod, w_out[l].astype(BF16),
                                              g_norm2[l].reshape(1, D), wr_t, br_t, striu, T=T)
        plan = _moe_plan(cnt[:, :N_GROUPS, 0].astype(jnp.int32))
        by_group = lambda w: w.reshape((N_GROUPS, EXPERTS_PER_GROUP) + w.shape[1:])
        y_local = _moe(plan, rows_local, by_group(w1[l]), by_group(w3[l]), by_group(w2[l]))
        x2 = _combine(x1, mod, lpos, y_local, T=T)
    return x2.reshape(B, T, D)
```

```python
import jax
import jax.numpy as jnp
import numpy as np
from jax import lax
from jax.experimental import pallas as pl
from jax.experimental.pallas import tpu as pltpu

F32 = jnp.float32
BF16 = jnp.bfloat16

EPS = 1e-6
LOG2_E = 1.4426950408889634
ATT_HEADS = 8
ATT_KV_HEADS = 2
HEAD_DIM = 64
WINDOW = 128
ATT_Q = ATT_HEADS * HEAD_DIM
ATT_KV = ATT_KV_HEADS * HEAD_DIM
GLA_HEADS = 4
GLA_DK = 64
GLA_DV = 128
GLA_RANK = 16
GLA_NORMALIZER = 16.0
GLA_K = GLA_HEADS * GLA_DK
GLA_V = GLA_HEADS * GLA_DV
N_GROUPS = 4
EXPERTS_PER_GROUP = 4
N_EXPERTS = N_GROUPS * EXPERTS_PER_GROUP

LANES = 128
MOD_COLS = 768
PROJ_SPLITS = (512, 512)
PROJ_TILE = sum(PROJ_SPLITS)
OUT_SUBS = 4
COMBINE_SUBS = 4
ROUTER_ROWS = 24
SEQS_PER_STEP = 8
GLA_CHUNK = 128
GLA_SUB = 16
N_SUB = GLA_CHUNK // GLA_SUB
ROUTER_LANE0 = N_GROUPS
VMEM_LIMIT = 56 * 1024 * 1024
MOE_TILE = 256
EXPERT_TILE = 256
CHUNK = 16
TILE_CHUNKS = EXPERT_TILE // CHUNK
LOCAL_CHUNKS = (MOE_TILE + N_GROUPS * (CHUNK - 1)) // CHUNK + 2
LOCAL_ROWS = LOCAL_CHUNKS * CHUNK

_QA0, _KA0, _VA0 = 0, ATT_Q, ATT_Q + ATT_KV
_QG0 = _VA0 + ATT_KV
_KG0 = _QG0 + GLA_K
_VG0 = _KG0 + GLA_K
_OG0 = _VG0 + GLA_V
_LR0 = _OG0 + GLA_V
IN_COLS_PAD = _LR0 + LANES


def _dot(a, b):
    return jnp.dot(a, b, preferred_element_type=F32)


def _dot_nt(a, b):
    return lax.dot_general(a, b, (((1,), (1,)), ((), ())), preferred_element_type=F32)


def _sigmoid(x):
    return 1.0 / (1.0 + jnp.exp(-x))


def _params(*sem):
    return pltpu.CompilerParams(dimension_semantics=sem, vmem_limit_bytes=VMEM_LIMIT)


def _mod_kernel(c_ref, w_ref, b_ref, o_ref):
    c = c_ref[...]
    s = (c * _sigmoid(c)).astype(BF16)
    o_ref[...] = _dot(s, w_ref[...].astype(BF16)) + b_ref[...]


def _adaln_mod(c, w_ada, b_ada):
    B, D = c.shape
    n = w_ada.shape[1]
    tn = MOD_COLS
    return pl.pallas_call(
        _mod_kernel,
        grid=(n // tn,),
        in_specs=[pl.BlockSpec((B, D), lambda j: (0, 0)),
                  pl.BlockSpec((D, tn), lambda j: (0, j)),
                  pl.BlockSpec((1, tn), lambda j: (0, j))],
        out_specs=pl.BlockSpec((B, tn), lambda j: (0, j)),
        out_shape=jax.ShapeDtypeStruct((B, n), F32),
        compiler_params=_params("arbitrary"),
        name="adaln_mod",
    )(c, w_ada, b_ada.reshape(1, n))


def _inproj_kernel(x_ref, mod_ref, g1_ref, wf_ref, qn_ref, kn_ref, bdq_ref, bdk_ref, wgk_ref, bgk_ref,
                   qa_ref, ka_ref, va_ref, qg_ref, kg_ref, vg_ref, la_ref, og_ref, w_ref):
    subs = range(len(PROJ_SPLITS))
    starts = [sum(PROJ_SPLITS[:t]) for t in subs]
    rows = lambda t: slice(starts[t], starts[t] + PROJ_SPLITS[t])

    @pl.when(pl.program_id(0) == 0)
    def _():
        lr_src = _OG0
        w_ref[0:_OG0, :] = wf_ref[0:_OG0, :].astype(BF16)
        w_ref[_OG0:_LR0, :] = wf_ref[lr_src + GLA_RANK:lr_src + GLA_RANK + GLA_V, :].astype(BF16)
        w_ref[_LR0:_LR0 + GLA_RANK, :] = wf_ref[lr_src:lr_src + GLA_RANK, :].astype(BF16)
        w_ref[_LR0 + GLA_RANK:IN_COLS_PAD, :] = jnp.zeros((LANES - GLA_RANK, wf_ref.shape[1]), BF16)

    h = []
    for t in subs:
        x = x_ref[rows(t), :]
        ms = jnp.mean(x * x, axis=-1, keepdims=True)
        xn = x * lax.rsqrt(ms + EPS) * g1_ref[...]
        h.append((xn * (1.0 + mod_ref[1:2, :]) + mod_ref[0:1, :]).astype(BF16))

    proj = lambda t, c0, width: _dot_nt(h[t], w_ref[c0:c0 + width, :])
    gate_of = lambda lr: _dot(lr.astype(BF16), wgk_ref[...]) + bgk_ref[...]
    qa, kv, q_ms, k_ms, qk_g, vg, og, lr, gate = ({} for _ in range(9))
    for t in subs:
        qa[t] = proj(t, _QA0, ATT_Q)
        if t > 0:
            gate[t - 1] = gate_of(lr[t - 1])
        kv[t] = proj(t, _KA0, 2 * ATT_KV)
        q_sq = (qa[t] * qa[t]).astype(BF16)
        q_ms[t] = jnp.concatenate([_dot(q_sq[:, c:c + 2 * LANES], bdq_ref[...])
                                   for c in range(0, ATT_Q, 2 * LANES)], axis=1)
        qk_g[t] = proj(t, _QG0, 2 * GLA_K)
        k = kv[t][:, 0:ATT_KV]
        k_ms[t] = _dot((k * k).astype(BF16), bdk_ref[...])
        vg[t] = proj(t, _VG0, GLA_V)
        og[t] = proj(t, _OG0, GLA_V)
        lr[t] = proj(t, _LR0, LANES)
    gate[subs[-1]] = gate_of(lr[subs[-1]])

    for t in subs:
        low = lax.broadcasted_iota(jnp.int32, (PROJ_SPLITS[t], ATT_KV), 1) < HEAD_DIM
        qa_ref[rows(t), :] = (qa[t] * lax.rsqrt(q_ms[t] + EPS) * qn_ref[...]
                              * (HEAD_DIM ** -0.5 * LOG2_E)).astype(BF16)
        k = kv[t][:, 0:ATT_KV] * lax.rsqrt(k_ms[t] + EPS) * kn_ref[...]
        v = kv[t][:, ATT_KV:2 * ATT_KV]
        for src, dst in ((k, ka_ref), (v, va_ref)):
            swapped = pltpu.roll(src, HEAD_DIM, axis=1)
            dst[rows(t), 0:LANES] = jnp.where(low, src, swapped).astype(BF16)
            dst[rows(t), LANES:2 * LANES] = jnp.where(low, swapped, src).astype(BF16)
        qg_ref[rows(t), :] = (qk_g[t][:, 0:GLA_K] * (GLA_DK ** -0.5)).astype(BF16)
        kg_ref[rows(t), :] = qk_g[t][:, GLA_K:2 * GLA_K].astype(BF16)
        vg_ref[rows(t), :] = vg[t].astype(BF16)
        og_ref[rows(t), :] = (og[t] * _sigmoid(og[t])).astype(BF16)
        log_sig = jnp.minimum(gate[t], 0.0) - jnp.log(1.0 + jnp.exp(-jnp.abs(gate[t])))
        la_ref[rows(t), :] = log_sig * (1.0 / GLA_NORMALIZER)


def _in_proj(x2, mod, g1, w_in_t, qn, kn, bdq, bdk, wgk, bgk, *, T, tm):
    N, D = x2.shape
    per_b = T // tm
    const = lambda shape: pl.BlockSpec(shape, lambda i: (0,) * len(shape))
    rows = lambda c: pl.BlockSpec((tm, c), lambda i: (i, 0))
    outs = [(ATT_Q, BF16), (2 * ATT_KV, BF16), (2 * ATT_KV, BF16), (GLA_K, BF16), (GLA_K, BF16),
            (GLA_V, BF16), (GLA_K, F32), (GLA_V, BF16)]
    return pl.pallas_call(
        _inproj_kernel,
        grid=(N // tm,),
        in_specs=[rows(D),
                  pl.BlockSpec((None, 6, D), lambda i: (i // per_b, 0, 0)),
                  const((1, D)),
                  pl.BlockSpec(w_in_t.shape, lambda i: (0, 0), pipeline_mode=pl.Buffered(1)),
                  const(qn.shape), const(kn.shape),
                  const(bdq.shape), const(bdk.shape), const(wgk.shape), const(bgk.shape)],
        out_specs=[rows(c) for c, _ in outs],
        out_shape=[jax.ShapeDtypeStruct((N, c), dt) for c, dt in outs],
        scratch_shapes=[pltpu.VMEM((IN_COLS_PAD, D), BF16)],
        compiler_params=_params("arbitrary"),
        name="in_proj",
    )(x2, mod, g1, w_in_t, qn, kn, bdq, bdk, wgk, bgk)


def _attn_stages(seqs, sinks_ref, q_ref, kc_ref, vc_ref, gatt_ref, o_ref, kp_ref, vp_ref):
    blk = WINDOW
    first = pl.program_id(1) == 0
    qi = lax.broadcasted_iota(jnp.int32, (blk, blk), 0)
    cj = lax.broadcasted_iota(jnp.int32, (blk, blk), 1)
    from_prev = cj > qi
    dist = (qi - cj + jnp.where(from_prev, blk, 0)).astype(F32)
    no_prev = jnp.where(jnp.logical_and(from_prev, first), -1e30, 0.0)
    low = cj < HEAD_DIM
    half = (jnp.where(low, 1.0, 0.0).astype(BF16), jnp.where(low, 0.0, 1.0).astype(BF16))
    half2 = tuple(jnp.concatenate([m, m], axis=0) for m in half)
    prev_mask = jnp.where(from_prev, 1.0, 0.0).astype(BF16)
    cur_mask = jnp.where(from_prev, 0.0, 1.0).astype(BF16)

    n_pairs = ATT_HEADS // 2
    pairs_per_kv = n_pairs // ATT_KV_HEADS
    units = [(bi, j) for bi in seqs for j in range(n_pairs)]

    def kv_blocks(bi, g):
        lanes = slice(g * LANES, (g + 1) * LANES)
        return (kp_ref[bi, :, lanes], vp_ref[bi, :, lanes]), (kc_ref[bi, :, lanes], vc_ref[bi, :, lanes])

    scores = {}
    for bi in seqs:
        for g in range(ATT_KV_HEADS):
            (kp, _), (kc, _) = kv_blocks(bi, g)
            k_both = jnp.concatenate([kp, kc], axis=0)
            group = range(g * pairs_per_kv, (g + 1) * pairs_per_kv)
            q_stack = jnp.concatenate([q_ref[bi, :, j * LANES:(j + 1) * LANES] for j in group], axis=0)
            for p in range(2):
                s_stack = _dot_nt(q_stack, k_both * half2[p])
                for jj, j in enumerate(group):
                    scores[bi, j, p] = s_stack[jj * blk:(jj + 1) * blk, :]
    yield

    probs, sink_terms = {}, {}
    for bi, j in units:
        for p in range(2):
            h = 2 * j + p
            slope = 2.0 ** (-8.0 * (h + 1) / ATT_HEADS) * LOG2_E
            s_both = scores[bi, j, p]
            s = jnp.where(from_prev, s_both[:, 0:blk], s_both[:, blk:2 * blk]) - slope * dist + no_prev
            sink = sinks_ref[h] * LOG2_E
            m = jnp.maximum(jnp.max(s, axis=-1, keepdims=True), sink)
            probs[bi, j, p] = jnp.exp2(s - m)
            sink_terms[bi, j, p] = jnp.exp2(sink - m)
    yield

    outs = {}
    for bi in seqs:
        for g in range(ATT_KV_HEADS):
            (_, vp), (_, vc) = kv_blocks(bi, g)
            v_stack = jnp.concatenate([jnp.concatenate([v * half[p], half[p]], axis=1)
                                       for v in (vp, vc) for p in range(2)], axis=0)
            group = range(g * pairs_per_kv, (g + 1) * pairs_per_kv)
            p_stack = []
            for j in group:
                e = [probs[bi, j, p].astype(BF16) for p in range(2)]
                p_stack.append(jnp.concatenate([x * m for m in (prev_mask, cur_mask) for x in e], axis=1))
            pv_stack = _dot(jnp.concatenate(p_stack, axis=0), v_stack)
            for jj, j in enumerate(group):
                pv = pv_stack[jj * blk:(jj + 1) * blk, :]
                den = pv[:, LANES:2 * LANES] + jnp.where(low, sink_terms[bi, j, 0], sink_terms[bi, j, 1])
                outs[bi, j] = pv[:, 0:LANES] / den
    yield

    for bi in seqs:
        o = jnp.concatenate([outs[bi, j] for j in range(n_pairs)], axis=1)
        ms = jnp.mean(o * o, axis=-1, keepdims=True)
        o_ref[bi] = (o * lax.rsqrt(ms + EPS) * gatt_ref[...]).astype(BF16)
        kp_ref[bi] = kc_ref[bi]
        vp_ref[bi] = vc_ref[bi]


def _gla_stages(seqs, q_ref, k_ref, v_ref, la_ref, og_ref, cum_ref, bdms_ref, smask_ref,
                ggla_ref, o_ref, state_ref):
    L = GLA_CHUNK
    lane = lax.broadcasted_iota(jnp.int32, (GLA_SUB, LANES), 1)
    head_mask = [jnp.where(lane // GLA_DK == hh, 1.0, 0.0).astype(BF16) for hh in range(2)]
    causal = lax.broadcasted_iota(jnp.int32, (L, L), 0) >= lax.broadcasted_iota(jnp.int32, (L, L), 1)
    units = [(s, pair) for s in seqs for pair in range(GLA_HEADS // 2)]
    kl = lambda pair: slice(pair * LANES, (pair + 1) * LANES)
    vl = lambda pair: slice(pair * 2 * GLA_DV, (pair + 1) * 2 * GLA_DV)

    b, b_in = {}, {}
    for s in seqs:
        la = la_ref[s]
        la_hi = la.astype(BF16)
        la_lo = (la - la_hi.astype(F32)).astype(BF16)
        sums = _dot(cum_ref[...], jnp.concatenate([la_hi, la_lo], axis=0)) * LOG2_E
        b[s] = sums[0:L, :]
        b_in[s] = sums[L:2 * L, :]
    yield

    q_both, keys, q_dec, k_dec_t, b_last = {}, {}, {}, {}, {}
    for s in seqs:
        ref = b[s] - b_in[s]
        b_last[s] = b[s][L - 1:L, :]
        q = q_ref[s].astype(F32)
        k = k_ref[s].astype(F32)
        q_in = (q * jnp.exp2(b_in[s])).astype(BF16)
        q_dec[s] = (q * jnp.exp2(b[s])).astype(BF16)
        k_dec = k * jnp.exp2(b_last[s] - b[s])
        for pair in range(GLA_HEADS // 2):
            k_p, b_p, ref_p = k[:, kl(pair)], b[s][:, kl(pair)], ref[:, kl(pair)]
            expanded = []
            for g in range(N_SUB):
                top = (g + 1) * GLA_SUB
                live = (k_p[0:top, :] * jnp.exp2(ref_p[g * GLA_SUB:g * GLA_SUB + 1, :] - b_p[0:top, :])).astype(BF16)
                expanded.append(live if top == L else
                                jnp.concatenate([live, jnp.zeros((L - top, LANES), BF16)], axis=0))
            keys[s, pair] = jnp.concatenate(expanded, axis=1)
            q_p = q_in[:, kl(pair)]
            zero_group = jnp.zeros((GLA_SUB, LANES), BF16)
            q_both[s, pair] = []
            for t in range(N_SUB // 2):
                lhs = []
                for hh in range(2):
                    for side in range(2):
                        g = 2 * t + side
                        piece = q_p[g * GLA_SUB:(g + 1) * GLA_SUB, :] * head_mask[hh]
                        lhs.append(jnp.concatenate([piece, zero_group] if side == 0 else [zero_group, piece], axis=1))
                q_both[s, pair].append(jnp.concatenate(lhs, axis=0))
            k_dec_t[s, pair] = k_dec[:, kl(pair)].T.astype(BF16)
    yield

    scores = {(u, t): _dot_nt(q_both[u][t], keys[u][:, 2 * t * LANES:2 * (t + 1) * LANES])
              for u in units for t in range(N_SUB // 2)}
    yield

    outs, updates = {}, {}
    for s, pair in units:
        v_p = v_ref[s, :, vl(pair)]
        o_parts = []
        for hh in range(2):
            a = jnp.concatenate([scores[(s, pair), t][2 * hh * GLA_SUB:2 * (hh + 1) * GLA_SUB, :]
                                 for t in range(N_SUB // 2)], axis=0)
            a = jnp.where(causal, a, 0.0).astype(BF16)
            o_parts.append(_dot(a, v_p[:, hh * GLA_DV:(hh + 1) * GLA_DV]))
        state = state_ref[s, pair]
        outs[s, pair] = jnp.concatenate(o_parts, axis=1) + _dot(q_dec[s][:, kl(pair)], state.astype(BF16))
        updates[s, pair] = _dot(k_dec_t[s, pair], v_p)
    yield

    for s, pair in units:
        decay = jnp.broadcast_to(jnp.exp2(b_last[s][:, kl(pair)]), (LANES, LANES)).T
        state_ref[s, pair] = (state_ref[s, pair] * jnp.concatenate([decay, decay], axis=1)
                              + updates[s, pair] * smask_ref[...])
        o = outs[s, pair]
        ms = _dot((o * o).astype(BF16), bdms_ref[...])
        y = o * lax.rsqrt(ms + EPS) * ggla_ref[:, vl(pair)] * og_ref[s, :, vl(pair)].astype(F32)
        o_ref[s, :, vl(pair)] = y.astype(BF16)


def _mixers_kernel(sinks_ref, qa_ref, ka_ref, va_ref, gatt_ref,
                   qg_ref, kg_ref, vg_ref, la_ref, og_ref, cum_ref, bdms_ref, smask_ref, ggla_ref,
                   ya_ref, yg_ref, kp_ref, vp_ref, state_ref):
    @pl.when(pl.program_id(1) == 0)
    def _():
        state_ref[...] = jnp.zeros_like(state_ref)
        kp_ref[...] = jnp.zeros_like(kp_ref)
        vp_ref[...] = jnp.zeros_like(vp_ref)

    n_seq = qa_ref.shape[0]
    waiting = []
    for group in (tuple(range(0, n_seq // 2)), tuple(range(n_seq // 2, n_seq))):
        waiting.append([
            _attn_stages(group, sinks_ref, qa_ref, ka_ref, va_ref, gatt_ref, ya_ref, kp_ref, vp_ref),
            _gla_stages(group, qg_ref, kg_ref, vg_ref, la_ref, og_ref, cum_ref, bdms_ref, smask_ref,
                        ggla_ref, yg_ref, state_ref)])
    pending = []
    while pending or waiting:
        if waiting:
            pending.extend(waiting.pop(0))
        for stages in list(pending):
            if next(stages, "done") == "done":
                pending.remove(stages)


def _mixers(sinks, qa, ka, va, gatt, qg, kg, vg, la, og, cum, bdms, smask, ggla, *, B, T):
    assert GLA_CHUNK == WINDOW
    L = WINDOW
    nb = SEQS_PER_STEP
    seq = lambda a: a.reshape(B, T, a.shape[-1])
    rows = lambda c: pl.BlockSpec((nb, L, c), lambda b, i: (b, i, 0))
    const = lambda a: pl.BlockSpec(a.shape, lambda b, i: (0,) * a.ndim)
    y_att, y_gla = pl.pallas_call(
        _mixers_kernel,
        grid=(B // nb, T // L),
        in_specs=[pl.BlockSpec(memory_space=pltpu.SMEM),
                  rows(ATT_Q), rows(2 * ATT_KV), rows(2 * ATT_KV), const(gatt),
                  rows(GLA_K), rows(GLA_K), rows(GLA_V), rows(GLA_K), rows(GLA_V),
                  const(cum), const(bdms), const(smask), const(ggla)],
        out_specs=[rows(ATT_Q), rows(GLA_V)],
        out_shape=[jax.ShapeDtypeStruct((B, T, ATT_Q), BF16), jax.ShapeDtypeStruct((B, T, GLA_V), BF16)],
        scratch_shapes=[pltpu.VMEM((nb, L, 2 * ATT_KV), BF16), pltpu.VMEM((nb, L, 2 * ATT_KV), BF16),
                        pltpu.VMEM((nb, GLA_HEADS // 2, LANES, 2 * GLA_DV), F32)],
        compiler_params=_params("arbitrary", "arbitrary"),
        name="mixers",
    )(sinks, seq(qa), seq(ka), seq(va), gatt, seq(qg), seq(kg), seq(vg), seq(la), seq(og),
      cum, bdms, smask, ggla)
    return y_att.reshape(B * T, ATT_Q), y_gla.reshape(B * T, GLA_V)


def _route(logits_t):
    lt = logits_t[0:ROUTER_ROWS, :]
    row = lax.broadcasted_iota(jnp.int32, lt.shape, 0)
    neg_inf = -jnp.inf
    g_log = jnp.where(row < N_GROUPS, lt, neg_inf)
    g_max = jnp.max(g_log, axis=0, keepdims=True)
    g_sel = jnp.min(jnp.where(g_log == g_max, row, LANES), axis=0, keepdims=True)
    g_sum = jnp.sum(jnp.where(row < N_GROUPS, jnp.exp(lt - g_max), 0.0), axis=0, keepdims=True)
    p_group = 1.0 / g_sum
    e_lo = ROUTER_LANE0 + EXPERTS_PER_GROUP * g_sel
    in_group = jnp.logical_and(row >= e_lo, row < e_lo + EXPERTS_PER_GROUP)
    e_log = jnp.where(in_group, lt, neg_inf)
    e_max = jnp.max(e_log, axis=0, keepdims=True)
    top1 = jnp.min(jnp.where(e_log == e_max, row, LANES), axis=0, keepdims=True)
    e_log2 = jnp.where(row == top1, neg_inf, e_log)
    e_max2 = jnp.max(e_log2, axis=0, keepdims=True)
    top2 = jnp.min(jnp.where(e_log2 == e_max2, row, LANES), axis=0, keepdims=True)
    ratio = jnp.exp(e_max2 - e_max)
    w_top1 = p_group / (1.0 + ratio)
    w_top2 = p_group * ratio / (1.0 + ratio)
    row8 = lax.broadcasted_iota(jnp.int32, (8, lt.shape[1]), 0)
    weights = jnp.where(row8 == top1 - e_lo, w_top1, 0.0) + jnp.where(row8 == top2 - e_lo, w_top2, 0.0)
    return g_sel, weights


def _outproj_kernel(ya_ref, yg_ref, x_ref, mod_ref, wo_ref, g2_ref, wrt_ref, brt_ref, striu_ref,
                    x1_ref, row_ref, lpos_ref, cnt_ref):
    tm = MOE_TILE
    subs = range(x_ref.shape[0] // tm)
    rows = lambda t: slice(t * tm, (t + 1) * tm)

    mix = [_dot(ya_ref[rows(t), :], wo_ref[0:ATT_Q, :]) + _dot(yg_ref[rows(t), :], wo_ref[ATT_Q:ATT_Q + GLA_V, :])
           for t in subs]
    h2b = []
    for t in subs:
        x1 = x_ref[rows(t), :] + mod_ref[2:3, :] * mix[t]
        x1_ref[rows(t), :] = x1
        ms = jnp.mean(x1 * x1, axis=-1, keepdims=True)
        h2 = (x1 * lax.rsqrt(ms + EPS) * g2_ref[...]) * (1.0 + mod_ref[4:5, :]) + mod_ref[3:4, :]
        h2b.append(h2.astype(BF16))
    logits_t = [_dot_nt(wrt_ref[...], h2b[t]) + brt_ref[...] for t in subs]

    routed = [_route(logits_t[t]) for t in subs]
    row8 = lax.broadcasted_iota(jnp.int32, (8, tm), 0)
    onehot = [jnp.where(row8 == routed[t][0], 1.0, 0.0) for t in subs]
    before = [_dot(onehot[t].astype(BF16), striu_ref[...]) for t in subs]

    local_row = lax.broadcasted_iota(jnp.int32, (LOCAL_ROWS, tm), 0).astype(F32)
    pad_rows = jnp.zeros((LANES - 8, tm), F32)
    for t in subs:
        count = jnp.sum(onehot[t], axis=1, keepdims=True)
        cnt_ref[t] = jnp.broadcast_to(count, (8, LANES))
        padded = jnp.broadcast_to(jnp.floor((count + (CHUNK - 1.0)) * (1.0 / CHUNK)) * CHUNK, (8, tm))
        start = jnp.zeros((8, tm), F32)
        for shift in range(1, N_GROUPS):
            start = start + jnp.where(row8 >= shift, pltpu.roll(padded, shift, axis=0), 0.0)
        lpos = jnp.sum(onehot[t] * (before[t] + start), axis=0, keepdims=True)
        lpos_ref[t] = jnp.broadcast_to(lpos, (8, tm))
        weights = jnp.concatenate([routed[t][1], pad_rows], axis=0).T
        w_hi = weights.astype(BF16)
        w_lo = (weights - w_hi.astype(F32)).astype(BF16)
        perm = jnp.where(local_row == lpos, 1.0, 0.0).astype(BF16)
        row_ref[t * LOCAL_ROWS:(t + 1) * LOCAL_ROWS, :] = _dot(
            perm, jnp.concatenate([h2b[t], w_hi, w_lo], axis=1)).astype(BF16)


def _out_proj(ya, yg, x2, mod, wo, g2, wr, br, stril, *, T):
    N, D = x2.shape
    subs = OUT_SUBS
    tm = subs * MOE_TILE
    per_b = T // tm
    rows = lambda c: pl.BlockSpec((tm, c), lambda i: (i, 0))
    const = lambda a: pl.BlockSpec(a.shape, lambda i: (0,) * a.ndim)
    return pl.pallas_call(
        _outproj_kernel,
        grid=(N // tm,),
        in_specs=[rows(ATT_Q), rows(GLA_V), rows(D),
                  pl.BlockSpec((None, 6, D), lambda i: (i // per_b, 0, 0)),
                  const(wo), const(g2), const(wr), const(br), const(stril)],
        out_specs=[rows(D), pl.BlockSpec((subs * LOCAL_ROWS, D + 2 * LANES), lambda i: (i, 0)),
                   pl.BlockSpec((subs, 8, MOE_TILE), lambda i: (i, 0, 0)),
                   pl.BlockSpec((subs, 8, LANES), lambda i: (i, 0, 0))],
        out_shape=[jax.ShapeDtypeStruct((N, D), F32),
                   jax.ShapeDtypeStruct((N // MOE_TILE * LOCAL_ROWS, D + 2 * LANES), BF16),
                   jax.ShapeDtypeStruct((N // MOE_TILE, 8, MOE_TILE), F32),
                   jax.ShapeDtypeStruct((N // MOE_TILE, 8, LANES), F32)],
        compiler_params=_params("arbitrary"),
        name="out_proj",
    )(ya, yg, x2, mod, wo, g2, wr, br, stril)


def _chunk_copy(src_ref, src_chunk, dst_ref, dst_chunk, sem):
    return pltpu.make_async_copy(src_ref.at[src_chunk], dst_ref.at[dst_chunk], sem)


def _moe_kernel(src_ref, dst_ref, grp_ref, next_ref, nt_ref, used_ref, rows_ref, w1f_ref, w3f_ref, w2f_ref, y_ref,
                in_buf, out_buf, zero_buf, st1_ref, st3_ref, st2_ref, w1_ref, w3_ref, w2_ref,
                in_sem, out_sem, zero_sem, w_sem):
    j = pl.program_id(0)
    n_tiles = nt_ref[0]
    d_model = w2_ref.shape[2]

    stages = ((w1f_ref, st1_ref, w1_ref), (w3f_ref, st3_ref, w3_ref), (w2f_ref, st2_ref, w2_ref))

    def fetch_weights(group):
        for hbm, stage, _ in stages:
            pltpu.make_async_copy(hbm.at[group], stage, w_sem).start()

    def enter_group():
        for hbm, stage, dst in stages:
            pltpu.make_async_copy(hbm.at[0], stage, w_sem).wait()
        for hbm, stage, dst in stages:
            for k in range(EXPERTS_PER_GROUP):
                dst[k] = stage[k].astype(BF16)

        @pl.when(next_ref[j] != grp_ref[j])
        def _():
            fetch_weights(next_ref[j])

    def gather(tile, slot):
        def body(k, carry):
            _chunk_copy(rows_ref, src_ref[tile * TILE_CHUNKS + k], in_buf.at[slot], k, in_sem.at[slot]).start()
            return carry
        lax.fori_loop(0, TILE_CHUNKS, body, 0, unroll=True)

    def wait_gather(slot):
        def body(k, carry):
            _chunk_copy(rows_ref, 0, in_buf.at[slot], k, in_sem.at[slot]).wait()
            return carry
        lax.fori_loop(0, TILE_CHUNKS, body, 0, unroll=True)

    def scatter(tile, slot):
        def body(k, carry):
            _chunk_copy(out_buf.at[slot], k, y_ref, dst_ref[tile * TILE_CHUNKS + k], out_sem.at[slot]).start()
            return carry
        lax.fori_loop(0, TILE_CHUNKS, body, 0, unroll=True)

    def wait_scatter(slot):
        def body(k, carry):
            _chunk_copy(out_buf.at[slot], k, y_ref, 0, out_sem.at[slot]).wait()
            return carry
        lax.fori_loop(0, TILE_CHUNKS, body, 0, unroll=True)

    def zero_fill(wait):
        def per_tile(i, carry):
            def body(c, inner):
                copy = _chunk_copy(zero_buf, 0, y_ref, i * LOCAL_CHUNKS + c, zero_sem)
                if wait:
                    copy.wait()
                else:
                    copy.start()
                return inner
            return lax.fori_loop(used_ref[i], LOCAL_CHUNKS, body, carry)
        lax.fori_loop(0, used_ref.shape[0], per_tile, 0)

    @pl.when(j == 0)
    def _():
        fetch_weights(grp_ref[0])
        zero_buf[...] = jnp.zeros_like(zero_buf)
        scratch0 = used_ref.shape[0] * LOCAL_CHUNKS
        for wait in (False, True):
            for k in range(2 * TILE_CHUNKS):
                copy = _chunk_copy(zero_buf, 0, y_ref, scratch0 + k, zero_sem)
                copy.wait() if wait else copy.start()
        zero_fill(wait=False)
        gather(0, 0)

    @pl.when(jnp.logical_or(j == 0, grp_ref[j] != grp_ref[jnp.maximum(j - 1, 0)]))
    def _():
        enter_group()

    @pl.when(j + 1 < n_tiles)
    def _():
        gather(j + 1, (j + 1) % 2)

    @pl.when(j < n_tiles)
    def _():
        slot = j % 2
        wait_gather(slot)
        rows = in_buf[slot].reshape(EXPERT_TILE, in_buf.shape[-1])
        h = rows[:, 0:d_model]
        weights = rows[:, d_model:d_model + LANES].astype(F32) + rows[:, d_model + LANES:].astype(F32)
        experts = range(EXPERTS_PER_GROUP)
        up = [(_dot(h, w1_ref[k]), _dot(h, w3_ref[k])) for k in experts]
        hid = [(a * _sigmoid(a) * g * weights[:, k:k + 1]).astype(BF16) for k, (a, g) in zip(experts, up)]
        y = _dot(hid[0], w2_ref[0])
        for k in experts[1:]:
            y = y + _dot(hid[k], w2_ref[k])

        @pl.when(j >= 2)
        def _():
            wait_scatter(slot)

        out_buf[slot] = y.astype(BF16).reshape(TILE_CHUNKS, CHUNK, d_model)
        scatter(j, slot)

        @pl.when(j == n_tiles - 1)
        def _():
            @pl.when(j >= 1)
            def _():
                wait_scatter(1 - slot)
            wait_scatter(slot)
            zero_fill(wait=True)


def _combine_kernel(x1_ref, mod_ref, lpos_ref, y_ref, o_ref):
    tm = MOE_TILE
    local_row = lax.broadcasted_iota(jnp.int32, (tm, LOCAL_ROWS), 1).astype(F32)
    for t in range(x1_ref.shape[0] // tm):
        rows = slice(t * tm, (t + 1) * tm)
        lpos = jnp.broadcast_to(lpos_ref[t][0:1, :], (LANES, tm)).T[:, 0:1]
        unsort = jnp.where(local_row == lpos, 1.0, 0.0).astype(BF16)
        y = _dot(unsort, y_ref[t * LOCAL_ROWS:(t + 1) * LOCAL_ROWS, :])
        o_ref[rows, :] = x1_ref[rows, :] + mod_ref[5:6, :] * y


def _moe_plan(cnt):
    n_local = cnt.shape[0]
    chunks = (cnt + CHUNK - 1) // CHUNK
    used = jnp.sum(chunks, axis=1)
    local_off = jnp.cumsum(chunks, axis=1) - chunks
    tiles_g = (jnp.sum(chunks, axis=0) + TILE_CHUNKS - 1) // TILE_CHUNKS
    tile_end = jnp.cumsum(tiles_g)
    n_tiles = tile_end[-1]
    group_start = (tile_end - tiles_g) * TILE_CHUNKS
    seg_len = chunks.T.reshape(-1)
    seg_start = (group_start[:, None] + (jnp.cumsum(chunks, axis=0) - chunks).T).reshape(-1)
    seg_src = (jnp.arange(n_local)[None, :] * LOCAL_CHUNKS + local_off.T).reshape(-1)
    max_chunks = n_local * MOE_TILE // CHUNK + n_local * N_GROUPS + N_GROUPS * TILE_CHUNKS
    max_tiles = (max_chunks + TILE_CHUNKS - 1) // TILE_CHUNKS
    c = jnp.arange(max_tiles * TILE_CHUNKS)[:, None]
    within = c - seg_start[None, :]
    hit = jnp.logical_and(within >= 0, within < seg_len[None, :])
    valid = jnp.any(hit, axis=1)
    src = jnp.sum(jnp.where(hit, seg_src[None, :] + within, 0), axis=1)
    src = jnp.where(valid, src, LOCAL_CHUNKS - 1)
    slot_k = c[:, 0] % (2 * TILE_CHUNKS)
    dst = jnp.where(valid, src, n_local * LOCAL_CHUNKS + slot_k)
    j = jnp.minimum(jnp.arange(max_tiles), n_tiles - 1)
    grp = jnp.sum(j[:, None] >= tile_end[None, :], axis=1)
    gid = jnp.arange(N_GROUPS)
    later = jnp.where(jnp.logical_and(gid[None, :] > gid[:, None], tiles_g[None, :] > 0), gid[None, :], N_GROUPS)
    next_of = jnp.min(later, axis=1)
    next_of = jnp.where(next_of == N_GROUPS, gid, next_of)
    next_grp = jnp.sum(jnp.where(grp[:, None] == gid[None, :], next_of[None, :], 0), axis=1)
    i32 = lambda a: a.astype(jnp.int32)
    return i32(src), i32(dst), i32(grp), i32(next_grp), i32(n_tiles).reshape(1), i32(used)


def _moe(plan, rows_local, w1g, w3g, w2g):
    src, dst, grp, next_grp, n_tiles, used = plan
    D = w2g.shape[3]
    n_rows, cols = rows_local.shape
    n_chunks = n_rows // CHUNK
    hbm = pl.BlockSpec(memory_space=pl.ANY)
    group_weights = [w1g, w3g, w2g]
    y = pl.pallas_call(
        _moe_kernel,
        grid_spec=pltpu.PrefetchScalarGridSpec(
            num_scalar_prefetch=6,
            grid=(grp.shape[0],),
            in_specs=[hbm, hbm, hbm, hbm],
            out_specs=hbm,
            scratch_shapes=([pltpu.VMEM((2, TILE_CHUNKS, CHUNK, cols), BF16),
                             pltpu.VMEM((2, TILE_CHUNKS, CHUNK, D), BF16),
                             pltpu.VMEM((1, CHUNK, D), BF16)]
                            + [pltpu.VMEM(w.shape[1:], F32) for w in group_weights]
                            + [pltpu.VMEM(w.shape[1:], BF16) for w in group_weights]
                            + [pltpu.SemaphoreType.DMA((2,)), pltpu.SemaphoreType.DMA((2,)),
                               pltpu.SemaphoreType.DMA(()), pltpu.SemaphoreType.DMA(())])),
        out_shape=jax.ShapeDtypeStruct((n_chunks + 2 * TILE_CHUNKS, CHUNK, D), BF16),
        compiler_params=_params("arbitrary"),
        name="moe",
    )(src, dst, grp, next_grp, n_tiles, used, rows_local.reshape(n_chunks, CHUNK, cols), w1g, w3g, w2g)
    return y.reshape((n_chunks + 2 * TILE_CHUNKS) * CHUNK, D)


def _combine(x1, mod, lpos, y_local, *, T):
    N, D = x1.shape
    subs = COMBINE_SUBS
    tm = subs * MOE_TILE
    per_b = T // tm
    return pl.pallas_call(
        _combine_kernel,
        grid=(N // tm,),
        in_specs=[pl.BlockSpec((tm, D), lambda i: (i, 0)),
                  pl.BlockSpec((None, 6, D), lambda i: (i // per_b, 0, 0)),
                  pl.BlockSpec((subs, 8, MOE_TILE), lambda i: (i, 0, 0)),
                  pl.BlockSpec((subs * LOCAL_ROWS, D), lambda i: (i, 0))],
        out_specs=pl.BlockSpec((tm, D), lambda i: (i, 0)),
        out_shape=jax.ShapeDtypeStruct((N, D), F32),
        compiler_params=_params("arbitrary"),
        name="moe_combine",
    )(x1, mod, lpos, y_local)


def _block_diag(n, blk, value, dtype):
    r = np.arange(n)[:, None] // blk
    c = np.arange(n)[None, :] // blk
    return jnp.asarray(np.where(r == c, value, 0.0), dtype)


def _gla_constants():
    L = GLA_CHUNK
    i = np.arange(L)[:, None]
    j = np.arange(L)[None, :]
    tri = j <= i
    bdtri = np.logical_and(j <= i, i // GLA_SUB == j // GLA_SUB)
    cum = np.block([[tri, tri], [bdtri, bdtri]])
    bdms = _block_diag(2 * GLA_DV, GLA_DV, 1.0 / GLA_DV, BF16)
    d = np.arange(LANES)[:, None] // GLA_DK
    e = np.arange(2 * GLA_DV)[None, :] // GLA_DV
    smask = d == e
    return jnp.asarray(cum, BF16), bdms, jnp.asarray(smask, F32)


def kernel(x, c, w_ada, b_ada, g_norm1, w_in, q_norm, k_norm, sinks, w_gk2, b_gk, g_gla_out, g_att_out,
           w_out, g_norm2, w_group, b_group, w_router, b_router, w1, w3, w2):
    B, T, D = x.shape
    N = B * T
    depth = w_ada.shape[0]
    cum, bdms, smask = _gla_constants()
    bdq = _block_diag(2 * LANES, HEAD_DIM, 1.0 / HEAD_DIM, BF16)
    bdk = _block_diag(ATT_KV, HEAD_DIM, 1.0 / HEAD_DIM, BF16)

    x2 = x.reshape(N, D)
    for l in range(depth):
        mod = _adaln_mod(c, w_ada[l], b_ada[l]).reshape(B, 6, D)

        wgk =jnp.concatenate([w_gk2[l], jnp.zeros((LANES - GLA_RANK, GLA_K), F32)], axis=0).astype(BF16)
        qa, ka, va, qg, kg, vg, la, og = _in_proj(
            x2, mod, g_norm1[l].reshape(1, D), w_in[l].T,
            jnp.tile(q_norm[l], ATT_HEADS).reshape(1, ATT_Q), jnp.tile(k_norm[l], ATT_KV_HEADS).reshape(1, ATT_KV),
            bdq, bdk, wgk, b_gk[l].reshape(1, GLA_K), T=T, tm=PROJ_TILE)

        y_att, y_gla = _mixers(sinks[l], qa, ka, va, g_att_out[l].reshape(1, ATT_Q),
                               qg, kg, vg, la, og, cum, bdms, smask,
                               jnp.tile(g_gla_out[l], GLA_HEADS).reshape(1, GLA_V), B=B, T=T)

        pad = LANES - N_GROUPS - N_EXPERTS
        wr_t = jnp.concatenate([w_group[l], w_router[l], jnp.zeros((D, pad), F32)], axis=1).T.astype(BF16)
        br_t = jnp.concatenate([b_group[l], b_router[l], jnp.zeros((pad,), F32)]).reshape(LANES, 1)
        striu = jnp.asarray(np.arange(MOE_TILE)[:, None] < np.arange(MOE_TILE)[None, :], BF16)
        x1, rows_local, lpos, cnt = _out_proj(y_att, y_gla, x2, mod, w_out[l].astype(BF16),
                                              g_norm2[l].reshape(1, D), wr_t, br_t, striu, T=T)
        plan = _moe_plan(cnt[:, :N_GROUPS, 0].astype(jnp.int32))
        by_group = lambda w: w.reshape((N_GROUPS, EXPERTS_PER_GROUP) + w.shape[1:])
        y_local = _moe(plan, rows_local, by_group(w1[l]), by_group(w3[l]), by_group(w2[l]))
        x2 = _combine(x1, mod, lpos, y_local, T=T)
    return x2.reshape(B, T, D)
```

```python
import jax
import jax.numpy as jnp
import numpy as np
from jax import lax
from jax.experimental import pallas as pl
from jax.experimental.pallas import tpu as pltpu

F32 = jnp.float32
BF16 = jnp.bfloat16

EPS = 1e-6
LOG2_E = 1.4426950408889634
ATT_HEADS = 8
ATT_KV_HEADS = 2
HEAD_DIM = 64
WINDOW = 128
ATT_Q = ATT_HEADS * HEAD_DIM
ATT_KV = ATT_KV_HEADS * HEAD_DIM
GLA_HEADS = 4
GLA_DK = 64
GLA_DV = 128
GLA_RANK = 16
GLA_NORMALIZER = 16.0
GLA_K = GLA_HEADS * GLA_DK
GLA_V = GLA_HEADS * GLA_DV
N_GROUPS = 4
EXPERTS_PER_GROUP = 4
N_EXPERTS = N_GROUPS * EXPERTS_PER_GROUP

LANES = 128
MOD_COLS = 768
PROJ_SPLITS = (512, 512)
PROJ_TILE = sum(PROJ_SPLITS)
ROUTER_ROWS = 24
GLA_CHUNK = 128
GLA_SUB = 16
N_SUB = GLA_CHUNK // GLA_SUB
ROUTER_LANE0 = N_GROUPS
VMEM_LIMIT = 56 * 1024 * 1024
MOE_TILE = 256
EXPERT_TILE = 256
CHUNK = 16
TILE_CHUNKS = EXPERT_TILE // CHUNK
LOCAL_CHUNKS = (MOE_TILE + N_GROUPS * (CHUNK - 1)) // CHUNK + 2
LOCAL_ROWS = LOCAL_CHUNKS * CHUNK

_QA0, _KA0, _VA0 = 0, ATT_Q, ATT_Q + ATT_KV
_QG0 = _VA0 + ATT_KV
_KG0 = _QG0 + GLA_K
_VG0 = _KG0 + GLA_K
_OG0 = _VG0 + GLA_V
_LR0 = _OG0 + GLA_V
IN_COLS_PAD = _LR0 + LANES


def _dot(a, b):
    return jnp.dot(a, b, preferred_element_type=F32)


def _dot_nt(a, b):
    return lax.dot_general(a, b, (((1,), (1,)), ((), ())), preferred_element_type=F32)


def _sigmoid(x):
    return 1.0 / (1.0 + jnp.exp(-x))


def _params(*sem):
    return pltpu.CompilerParams(dimension_semantics=sem, vmem_limit_bytes=VMEM_LIMIT)


def _mod_kernel(c_ref, w_ref, b_ref, o_ref):
    c = c_ref[...]
    s = (c * _sigmoid(c)).astype(BF16)
    o_ref[...] = _dot(s, w_ref[...].astype(BF16)) + b_ref[...]


def _adaln_mod(c, w_ada, b_ada):
    B, D = c.shape
    n = w_ada.shape[1]
    tn = MOD_COLS
    return pl.pallas_call(
        _mod_kernel,
        grid=(n // tn,),
        in_specs=[pl.BlockSpec((B, D), lambda j: (0, 0)),
                  pl.BlockSpec((D, tn), lambda j: (0, j)),
                  pl.BlockSpec((1, tn), lambda j: (0, j))],
        out_specs=pl.BlockSpec((B, tn), lambda j: (0, j)),
        out_shape=jax.ShapeDtypeStruct((B, n), F32),
        compiler_params=_params("arbitrary"),
        name="adaln_mod",
    )(c, w_ada, b_ada.reshape(1, n))


def _inproj_kernel(x_ref, mod_ref, g1_ref, wf_ref, qn_ref, kn_ref, bdq_ref, bdk_ref, wgk_ref, bgk_ref,
                   qa_ref, ka_ref, va_ref, qg_ref, kg_ref, vg_ref, la_ref, og_ref, w_ref):
    subs = range(len(PROJ_SPLITS))
    starts = [sum(PROJ_SPLITS[:t]) for t in subs]
    rows = lambda t: slice(starts[t], starts[t] + PROJ_SPLITS[t])

    @pl.when(pl.program_id(0) == 0)
    def _():
        lr_src = _OG0
        w_ref[0:_OG0, :] = wf_ref[0:_OG0, :].astype(BF16)
        w_ref[_OG0:_LR0, :] = wf_ref[lr_src + GLA_RANK:lr_src + GLA_RANK + GLA_V, :].astype(BF16)
        w_ref[_LR0:_LR0 + GLA_RANK, :] = wf_ref[lr_src:lr_src + GLA_RANK, :].astype(BF16)
        w_ref[_LR0 + GLA_RANK:IN_COLS_PAD, :] = jnp.zeros((LANES - GLA_RANK, wf_ref.shape[1]), BF16)

    h = []
    for t in subs:
        x = x_ref[rows(t), :]
        ms = jnp.mean(x * x, axis=-1, keepdims=True)
        xn = x * lax.rsqrt(ms + EPS) * g1_ref[...]
        h.append((xn * (1.0 + mod_ref[1:2, :]) + mod_ref[0:1, :]).astype(BF16))

    proj = lambda t, c0, width: _dot_nt(h[t], w_ref[c0:c0 + width, :])
    gate_of = lambda lr: _dot(lr.astype(BF16), wgk_ref[...]) + bgk_ref[...]
    qa, kv, q_ms, k_ms, qk_g, vg, og, lr, gate = ({} for _ in range(9))
    for t in subs:
        qa[t] = proj(t, _QA0, ATT_Q)
        if t > 0:
            gate[t - 1] = gate_of(lr[t - 1])
        kv[t] = proj(t, _KA0, 2 * ATT_KV)
        q_sq = (qa[t] * qa[t]).astype(BF16)
        q_ms[t] = jnp.concatenate([_dot(q_sq[:, c:c + 2 * LANES], bdq_ref[...])
                                   for c in range(0, ATT_Q, 2 * LANES)], axis=1)
        qk_g[t] = proj(t, _QG0, 2 * GLA_K)
        k = kv[t][:, 0:ATT_KV]
        k_ms[t] = _dot((k * k).astype(BF16), bdk_ref[...])
        vg[t] = proj(t, _VG0, GLA_V)
        og[t] = proj(t, _OG0, GLA_V)
        lr[t] = proj(t, _LR0, LANES)
    gate[subs[-1]] = gate_of(lr[subs[-1]])

    for t in subs:
        low = lax.broadcasted_iota(jnp.int32, (PROJ_SPLITS[t], ATT_KV), 1) < HEAD_DIM
        qa_ref[rows(t), :] = (qa[t] * lax.rsqrt(q_ms[t] + EPS) * qn_ref[...]
                              * (HEAD_DIM ** -0.5 * LOG2_E)).astype(BF16)
        k = kv[t][:, 0:ATT_KV] * lax.rsqrt(k_ms[t] + EPS) * kn_ref[...]
        v = kv[t][:, ATT_KV:2 * ATT_KV]
        for src, dst in ((k, ka_ref), (v, va_ref)):
            swapped = pltpu.roll(src, HEAD_DIM, axis=1)
            dst[rows(t), 0:LANES] = jnp.where(low, src, swapped).astype(BF16)
            dst[rows(t), LANES:2 * LANES] = jnp.where(low, swapped, src).astype(BF16)
        qg_ref[rows(t), :] = (qk_g[t][:, 0:GLA_K] * (GLA_DK ** -0.5)).astype(BF16)
        kg_ref[rows(t), :] = qk_g[t][:, GLA_K:2 * GLA_K].astype(BF16)
        vg_ref[rows(t), :] = vg[t].astype(BF16)
        og_ref[rows(t), :] = (og[t] * _sigmoid(og[t])).astype(BF16)
        log_sig = jnp.minimum(gate[t], 0.0) - jnp.log(1.0 + jnp.exp(-jnp.abs(gate[t])))
        la_ref[rows(t), :] = log_sig * (1.0 / GLA_NORMALIZER)


def _in_proj(x2, mod, g1, w_in_t, qn, kn, bdq, bdk, wgk, bgk, *, T, tm):
    N, D = x2.shape
    per_b = T // tm
    const = lambda shape: pl.BlockSpec(shape, lambda i: (0,) * len(shape))
    rows = lambda c: pl.BlockSpec((tm, c), lambda i: (i, 0))
    outs = [(ATT_Q, BF16), (2 * ATT_KV, BF16), (2 * ATT_KV, BF16), (GLA_K, BF16), (GLA_K, BF16),
            (GLA_V, BF16), (GLA_K, F32), (GLA_V, BF16)]
    return pl.pallas_call(
        _inproj_kernel,
        grid=(N // tm,),
        in_specs=[rows(D),
                  pl.BlockSpec((None, 6, D), lambda i: (i // per_b, 0, 0)),
                  const((1, D)),
                  pl.BlockSpec(w_in_t.shape, lambda i: (0, 0), pipeline_mode=pl.Buffered(1)),
                  const(qn.shape), const(kn.shape),
                  const(bdq.shape), const(bdk.shape), const(wgk.shape), const(bgk.shape)],
        out_specs=[rows(c) for c, _ in outs],
        out_shape=[jax.ShapeDtypeStruct((N, c), dt) for c, dt in outs],
        scratch_shapes=[pltpu.VMEM((IN_COLS_PAD, D), BF16)],
        compiler_params=_params("arbitrary"),
        name="in_proj",
    )(x2, mod, g1, w_in_t, qn, kn, bdq, bdk, wgk, bgk)


def _attn_stages(seqs, sinks_ref, q_ref, kc_ref, vc_ref, gatt_ref, o_ref, kp_ref, vp_ref):
    blk = WINDOW
    first = pl.program_id(1) == 0
    qi = lax.broadcasted_iota(jnp.int32, (blk, blk), 0)
    cj = lax.broadcasted_iota(jnp.int32, (blk, blk), 1)
    from_prev = cj > qi
    dist = (qi - cj + jnp.where(from_prev, blk, 0)).astype(F32)
    no_prev = jnp.where(jnp.logical_and(from_prev, first), -1e30, 0.0)
    low = cj < HEAD_DIM
    half = (jnp.where(low, 1.0, 0.0).astype(BF16), jnp.where(low, 0.0, 1.0).astype(BF16))
    half2 = tuple(jnp.concatenate([m, m], axis=0) for m in half)
    prev_mask = jnp.where(from_prev, 1.0, 0.0).astype(BF16)
    cur_mask = jnp.where(from_prev, 0.0, 1.0).astype(BF16)

    n_pairs = ATT_HEADS // 2
    pairs_per_kv = n_pairs // ATT_KV_HEADS
    units = [(bi, j) for bi in seqs for j in range(n_pairs)]

    def kv_blocks(bi, g):
        lanes = slice(g * LANES, (g + 1) * LANES)
        return (kp_ref[bi, :, lanes], vp_ref[bi, :, lanes]), (kc_ref[bi, :, lanes], vc_ref[bi, :, lanes])

    scores = {}
    for bi in seqs:
        for g in range(ATT_KV_HEADS):
            (kp, _), (kc, _) = kv_blocks(bi, g)
            k_both = jnp.concatenate([kp, kc], axis=0)
            group = range(g * pairs_per_kv, (g + 1) * pairs_per_kv)
            q_stack = jnp.concatenate([q_ref[bi, :, j * LANES:(j + 1) * LANES] for j in group], axis=0)
            for p in range(2):
                s_stack = _dot_nt(q_stack, k_both * half2[p])
                for jj, j in enumerate(group):
                    scores[bi, j, p] = s_stack[jj * blk:(jj + 1) * blk, :]
    yield

    bias = [2.0 ** (-8.0 * (h + 1) / ATT_HEADS) * LOG2_E * dist - no_prev for h in range(ATT_HEADS)]
    probs, sink_terms = {}, {}
    for bi, j in units:
        for p in range(2):
            h = 2 * j + p
            s_both = scores[bi, j, p]
            s = jnp.where(from_prev, s_both[:, 0:blk], s_both[:, blk:2 * blk]) - bias[h]
            sink = sinks_ref[h] * LOG2_E
            m = jnp.maximum(jnp.max(s, axis=-1, keepdims=True), sink)
            probs[bi, j, p] = jnp.exp2(s - m)
            sink_terms[bi, j, p] = jnp.exp2(sink - m)
    yield

    outs = {}
    for bi in seqs:
        for g in range(ATT_KV_HEADS):
            (_, vp), (_, vc) = kv_blocks(bi, g)
            v_stack = jnp.concatenate([jnp.concatenate([v * half[p], half[p]], axis=1)
                                       for v in (vp, vc) for p in range(2)], axis=0)
            group = range(g * pairs_per_kv, (g + 1) * pairs_per_kv)
            p_stack = []
            for j in group:
                e = [probs[bi, j, p].astype(BF16) for p in range(2)]
                p_stack.append(jnp.concatenate([x * m for m in (prev_mask, cur_mask) for x in e], axis=1))
            pv_stack = _dot(jnp.concatenate(p_stack, axis=0), v_stack)
            for jj, j in enumerate(group):
                pv = pv_stack[jj * blk:(jj + 1) * blk, :]
                den = pv[:, LANES:2 * LANES] + jnp.where(low, sink_terms[bi, j, 0], sink_terms[bi, j, 1])
                outs[bi, j] = pv[:, 0:LANES] / den
    yield

    for bi in seqs:
        o = jnp.concatenate([outs[bi, j] for j in range(n_pairs)], axis=1)
        ms = jnp.mean(o * o, axis=-1, keepdims=True)
        o_ref[bi] = (o * lax.rsqrt(ms + EPS) * gatt_ref[...]).astype(BF16)
        kp_ref[bi] = kc_ref[bi]
        vp_ref[bi] = vc_ref[bi]


def _gla_stages(seqs, q_ref, k_ref, v_ref, la_ref, og_ref, cum_ref, bdms_ref, smask_ref,
                ggla_ref, o_ref, state_ref):
    L = GLA_CHUNK
    lane = lax.broadcasted_iota(jnp.int32, (GLA_SUB, LANES), 1)
    head_mask = [jnp.where(lane // GLA_DK == hh, 1.0, 0.0).astype(BF16) for hh in range(2)]
    causal = lax.broadcasted_iota(jnp.int32, (L, L), 0) >= lax.broadcasted_iota(jnp.int32, (L, L), 1)
    units = [(s, pair) for s in seqs for pair in range(GLA_HEADS // 2)]
    kl = lambda pair: slice(pair * LANES, (pair + 1) * LANES)
    vl = lambda pair: slice(pair * 2 * GLA_DV, (pair + 1) * 2 * GLA_DV)

    b, b_in = {}, {}
    for s in seqs:
        la = la_ref[s]
        la_hi = la.astype(BF16)
        la_lo = (la - la_hi.astype(F32)).astype(BF16)
        sums = _dot(cum_ref[...], jnp.concatenate([la_hi, la_lo], axis=0)) * LOG2_E
        b[s] = sums[0:L, :]
        b_in[s] = sums[L:2 * L, :]
    yield

    q_both, keys, q_dec, k_dec_t, b_last = {}, {}, {}, {}, {}
    for s in seqs:
        ref = b[s] - b_in[s]
        b_last[s] = b[s][L - 1:L, :]
        q = q_ref[s].astype(F32)
        k = k_ref[s].astype(F32)
        q_in = (q * jnp.exp2(b_in[s])).astype(BF16)
        q_dec[s] = (q * jnp.exp2(b[s])).astype(BF16)
        k_dec = k * jnp.exp2(b_last[s] - b[s])
        for pair in range(GLA_HEADS // 2):
            k_p, b_p, ref_p = k[:, kl(pair)], b[s][:, kl(pair)], ref[:, kl(pair)]
            expanded = []
            for g in range(N_SUB):
                top = (g + 1) * GLA_SUB
                live = (k_p[0:top, :] * jnp.exp2(ref_p[g * GLA_SUB:g * GLA_SUB + 1, :] - b_p[0:top, :])).astype(BF16)
                expanded.append(live if top == L else
                                jnp.concatenate([live, jnp.zeros((L - top, LANES), BF16)], axis=0))
            keys[s, pair] = jnp.concatenate(expanded, axis=1)
            q_p = q_in[:, kl(pair)]
            zero_group = jnp.zeros((GLA_SUB, LANES), BF16)
            q_both[s, pair] = []
            for t in range(N_SUB // 2):
                lhs = []
                for hh in range(2):
                    for side in range(2):
                        g = 2 * t + side
                        piece = q_p[g * GLA_SUB:(g + 1) * GLA_SUB, :] * head_mask[hh]
                        lhs.append(jnp.concatenate([piece, zero_group] if side == 0 else [zero_group, piece], axis=1))
                q_both[s, pair].append(jnp.concatenate(lhs, axis=0))
            k_dec_t[s, pair] = k_dec[:, kl(pair)].T.astype(BF16)
    yield

    scores = {(u, t): _dot_nt(q_both[u][t], keys[u][:, 2 * t * LANES:2 * (t + 1) * LANES])
              for u in units for t in range(N_SUB // 2)}
    yield

    outs, updates = {}, {}
    for s, pair in units:
        v_p = v_ref[s, :, vl(pair)]
        o_parts = []
        for hh in range(2):
            a = jnp.concatenate([scores[(s, pair), t][2 * hh * GLA_SUB:2 * (hh + 1) * GLA_SUB, :]
                                 for t in range(N_SUB // 2)], axis=0)
            a = jnp.where(causal, a, 0.0).astype(BF16)
            o_parts.append(_dot(a, v_p[:, hh * GLA_DV:(hh + 1) * GLA_DV]))
        state = state_ref[s, pair]
        outs[s, pair] = jnp.concatenate(o_parts, axis=1) + _dot(q_dec[s][:, kl(pair)], state.astype(BF16))
        updates[s, pair] = _dot(k_dec_t[s, pair], v_p)
    yield

    for s, pair in units:
        decay = jnp.broadcast_to(jnp.exp2(b_last[s][:, kl(pair)]), (LANES, LANES)).T
        state_ref[s, pair] = (state_ref[s, pair] * jnp.concatenate([decay, decay], axis=1)
                              + updates[s, pair] * smask_ref[...])
        o = outs[s, pair]
        ms = _dot((o * o).astype(BF16), bdms_ref[...])
        y = o * lax.rsqrt(ms + EPS) * ggla_ref[:, vl(pair)] * og_ref[s, :, vl(pair)].astype(F32)
        o_ref[s, :, vl(pair)] = y.astype(BF16)


def _mixers_kernel(sinks_ref, qa_ref, ka_ref, va_ref, gatt_ref,
                   qg_ref, kg_ref, vg_ref, la_ref, og_ref, cum_ref, bdms_ref, smask_ref, ggla_ref,
                   ya_ref, yg_ref, kp_ref, vp_ref, state_ref):
    @pl.when(pl.program_id(1) == 0)
    def _():
        state_ref[...] = jnp.zeros_like(state_ref)
        kp_ref[...] = jnp.zeros_like(kp_ref)
        vp_ref[...] = jnp.zeros_like(vp_ref)

    n_seq = qa_ref.shape[0]
    waiting = []
    for group in (tuple(range(0, n_seq // 2)), tuple(range(n_seq // 2, n_seq))):
        waiting.append([
            _attn_stages(group, sinks_ref, qa_ref, ka_ref, va_ref, gatt_ref, ya_ref, kp_ref, vp_ref),
            _gla_stages(group, qg_ref, kg_ref, vg_ref, la_ref, og_ref, cum_ref, bdms_ref, smask_ref,
                        ggla_ref, yg_ref, state_ref)])
    pending = []
    while pending or waiting:
        if waiting:
            pending.extend(waiting.pop(0))
        for stages in list(pending):
            if next(stages, "done") == "done":
                pending.remove(stages)


def _route(logits_t):
    lt = logits_t[0:ROUTER_ROWS, :]
    row = lax.broadcasted_iota(jnp.int32, lt.shape, 0)
    neg_inf = -jnp.inf
    g_log = jnp.where(row < N_GROUPS, lt, neg_inf)
    g_max = jnp.max(g_log, axis=0, keepdims=True)
    g_sel = jnp.min(jnp.where(g_log == g_max, row, LANES), axis=0, keepdims=True)
    g_sum = jnp.sum(jnp.where(row < N_GROUPS, jnp.exp(lt - g_max), 0.0), axis=0, keepdims=True)
    p_group = 1.0 / g_sum
    e_lo = ROUTER_LANE0 + EXPERTS_PER_GROUP * g_sel
    in_group = jnp.logical_and(row >= e_lo, row < e_lo + EXPERTS_PER_GROUP)
    e_log = jnp.where(in_group, lt, neg_inf)
    e_max = jnp.max(e_log, axis=0, keepdims=True)
    top1 = jnp.min(jnp.where(e_log == e_max, row, LANES), axis=0, keepdims=True)
    e_log2 = jnp.where(row == top1, neg_inf, e_log)
    e_max2 = jnp.max(e_log2, axis=0, keepdims=True)
    top2 = jnp.min(jnp.where(e_log2 == e_max2, row, LANES), axis=0, keepdims=True)
    ratio = jnp.exp(e_max2 - e_max)
    w_top1 = p_group / (1.0 + ratio)
    w_top2 = p_group * ratio / (1.0 + ratio)
    row8 = lax.broadcasted_iota(jnp.int32, (8, lt.shape[1]), 0)
    weights = jnp.where(row8 == top1 - e_lo, w_top1, 0.0) + jnp.where(row8 == top2 - e_lo, w_top2, 0.0)
    return g_sel, weights


def _outproj_body(ya_ref, yg_ref, x_ref, mod_ref, wo_ref, g2_ref, wrt_ref, brt_ref, striu_ref,
                  x1_ref, row_ref, lpos_ref, cnt_ref):
    tm = MOE_TILE
    half = x_ref.shape[1]
    per_tile = tm // half
    subs = range(x_ref.shape[0] // per_tile)
    rows = lambda t: slice(t * tm, (t + 1) * tm)
    tile_of = lambda ref, t: jnp.concatenate([ref[t * per_tile + i] for i in range(per_tile)], axis=0)

    mix = [_dot(tile_of(ya_ref, t), wo_ref[0:ATT_Q, :]) + _dot(tile_of(yg_ref, t), wo_ref[ATT_Q:ATT_Q + GLA_V, :])
           for t in subs]
    h2b = []
    for t in subs:
        parts = []
        for i in range(per_tile):
            s = t * per_tile + i
            x1 = x_ref[s] + mod_ref[s, 2:3, :] * mix[t][i * half:(i + 1) * half, :]
            x1_ref[t * tm + i * half:t * tm + (i + 1) * half, :] = x1
            ms = jnp.mean(x1 * x1, axis=-1, keepdims=True)
            h2 = (x1 * lax.rsqrt(ms + EPS) * g2_ref[...]) * (1.0 + mod_ref[s, 4:5, :]) + mod_ref[s, 3:4, :]
            parts.append(h2.astype(BF16))
        h2b.append(jnp.concatenate(parts, axis=0))
    logits_t = [_dot_nt(wrt_ref[...], h2b[t]) + brt_ref[...] for t in subs]

    routed = [_route(logits_t[t]) for t in subs]
    row8 = lax.broadcasted_iota(jnp.int32, (8, tm), 0)
    onehot = [jnp.where(row8 == routed[t][0], 1.0, 0.0) for t in subs]
    before = [_dot(onehot[t].astype(BF16), striu_ref[...]) for t in subs]

    local_row = lax.broadcasted_iota(jnp.int32, (LOCAL_ROWS, tm), 0).astype(F32)
    pad_rows = jnp.zeros((LANES - 8, tm), F32)
    for t in subs:
        count = jnp.sum(onehot[t], axis=1, keepdims=True)
        cnt_ref[t] = jnp.broadcast_to(count, (8, LANES))
        padded = jnp.broadcast_to(jnp.floor((count + (CHUNK - 1.0)) * (1.0 / CHUNK)) * CHUNK, (8, tm))
        start = jnp.zeros((8, tm), F32)
        for shift in range(1, N_GROUPS):
            start = start + jnp.where(row8 >= shift, pltpu.roll(padded, shift, axis=0), 0.0)
        lpos = jnp.sum(onehot[t] * (before[t] + start), axis=0, keepdims=True)
        lpos_ref[t] = jnp.broadcast_to(lpos, (8, tm))
        weights = jnp.concatenate([routed[t][1], pad_rows], axis=0).T
        w_hi = weights.astype(BF16)
        w_lo = (weights - w_hi.astype(F32)).astype(BF16)
        perm = jnp.where(local_row == lpos, 1.0, 0.0).astype(BF16)
        row_ref[t * LOCAL_ROWS:(t + 1) * LOCAL_ROWS, :] = _dot(
            perm, jnp.concatenate([h2b[t], w_hi, w_lo], axis=1)).astype(BF16)


def _mix_out_kernel(sinks_ref, qa_ref, ka_ref, va_ref, gatt_ref,
                    qg_ref, kg_ref, vg_ref, la_ref, og_ref, cum_ref, bdms_ref, smask_ref, ggla_ref,
                    x_ref, mod_ref, wo_ref, g2_ref, wrt_ref, brt_ref, striu_ref,
                    x1_ref, row_ref, lpos_ref, cnt_ref,
                    ya_ref, yg_ref, kp_ref, vp_ref, state_ref):
    _mixers_kernel(sinks_ref, qa_ref, ka_ref, va_ref, gatt_ref,
                   qg_ref, kg_ref, vg_ref, la_ref, og_ref, cum_ref, bdms_ref, smask_ref, ggla_ref,
                   ya_ref, yg_ref, kp_ref, vp_ref, state_ref)
    _outproj_body(ya_ref, yg_ref, x_ref, mod_ref, wo_ref, g2_ref, wrt_ref, brt_ref, striu_ref,
                  x1_ref, row_ref, lpos_ref, cnt_ref)


def _mix_out(sinks, qa, ka, va, gatt, qg, kg, vg, la, og, cum, bdms, smask, ggla,
             x, mod, wo, g2, wr, br, striu):
    assert GLA_CHUNK == WINDOW and MOE_TILE % WINDOW == 0
    B, T, D = x.shape
    L = WINDOW
    N = B * T
    tiles_per_step = B * L // MOE_TILE
    seq = lambda a: a.reshape(B, T, a.shape[-1])
    rows = lambda c: pl.BlockSpec((B, L, c), lambda b, i: (0, i, 0))
    const = lambda a: pl.BlockSpec(a.shape, lambda b, i: (0,) * a.ndim)
    tiles = lambda *shape: pl.BlockSpec((tiles_per_step,) + shape, lambda b, i: (i,) + (0,) * len(shape))
    return pl.pallas_call(
        _mix_out_kernel,
        grid=(1, T // L),
        in_specs=[pl.BlockSpec(memory_space=pltpu.SMEM),
                  rows(ATT_Q), rows(2 * ATT_KV), rows(2 * ATT_KV), const(gatt),
                  rows(GLA_K), rows(GLA_K), rows(GLA_V), rows(GLA_K), rows(GLA_V),
                  const(cum), const(bdms), const(smask), const(ggla),
                  rows(D), const(mod), const(wo), const(g2), const(wr), const(br), const(striu)],
        out_specs=[pl.BlockSpec((B * L, D), lambda b, i: (i, 0)),
                   pl.BlockSpec((tiles_per_step * LOCAL_ROWS, D + 2 * LANES), lambda b, i: (i, 0)),
                   tiles(8, MOE_TILE), tiles(8, LANES)],
        out_shape=[jax.ShapeDtypeStruct((N, D), F32),
                   jax.ShapeDtypeStruct((N // MOE_TILE * LOCAL_ROWS, D + 2 * LANES), BF16),
                   jax.ShapeDtypeStruct((N // MOE_TILE, 8, MOE_TILE), F32),
                   jax.ShapeDtypeStruct((N // MOE_TILE, 8, LANES), F32)],
        scratch_shapes=[pltpu.VMEM((B, L, ATT_Q), BF16), pltpu.VMEM((B, L, GLA_V), BF16),
                        pltpu.VMEM((B, L, 2 * ATT_KV), BF16), pltpu.VMEM((B, L, 2 * ATT_KV), BF16),
                        pltpu.VMEM((B, GLA_HEADS // 2, LANES, 2 * GLA_DV), F32)],
        compiler_params=_params("arbitrary", "arbitrary"),
        name="mix_out",
    )(sinks, seq(qa), seq(ka), seq(va), gatt, seq(qg), seq(kg), seq(vg), seq(la), seq(og),
      cum, bdms, smask, ggla, x, mod, wo, g2, wr, br, striu)


def _chunk_copy(src_ref, src_chunk, dst_ref, dst_chunk, sem):
    return pltpu.make_async_copy(src_ref.at[src_chunk], dst_ref.at[dst_chunk], sem)


def _moe_kernel(src_ref, dst_ref, grp_ref, next_ref, nt_ref, used_ref, rows_ref, w1f_ref, w3f_ref, w2f_ref, y_ref,
                in_buf, out_buf, zero_buf, st1_ref, st3_ref, st2_ref, w1_ref, w3_ref, w2_ref,
                in_sem, out_sem, zero_sem, w_sem):
    j = pl.program_id(0)
    n_tiles = nt_ref[0]
    d_model = w2_ref.shape[2]

    stages = ((w1f_ref, st1_ref, w1_ref), (w3f_ref, st3_ref, w3_ref), (w2f_ref, st2_ref, w2_ref))

    def fetch_weights(group):
        for hbm, stage, _ in stages:
            pltpu.make_async_copy(hbm.at[group], stage, w_sem).start()

    def enter_group():
        for hbm, stage, dst in stages:
            pltpu.make_async_copy(hbm.at[0], stage, w_sem).wait()
        for hbm, stage, dst in stages:
            for k in range(EXPERTS_PER_GROUP):
                dst[k] = stage[k].astype(BF16)

        @pl.when(next_ref[j] != grp_ref[j])
        def _():
            fetch_weights(next_ref[j])

    def gather(tile, slot):
        def body(k, carry):
            _chunk_copy(rows_ref, src_ref[tile * TILE_CHUNKS + k], in_buf.at[slot], k, in_sem.at[slot]).start()
            return carry
        lax.fori_loop(0, TILE_CHUNKS, body, 0, unroll=True)

    def wait_gather(slot):
        def body(k, carry):
            _chunk_copy(rows_ref, 0, in_buf.at[slot], k, in_sem.at[slot]).wait()
            return carry
        lax.fori_loop(0, TILE_CHUNKS, body, 0, unroll=True)

    def scatter(tile, slot):
        def body(k, carry):
            _chunk_copy(out_buf.at[slot], k, y_ref, dst_ref[tile * TILE_CHUNKS + k], out_sem.at[slot]).start()
            return carry
        lax.fori_loop(0, TILE_CHUNKS, body, 0, unroll=True)

    def wait_scatter(slot):
        def body(k, carry):
            _chunk_copy(out_buf.at[slot], k, y_ref, 0, out_sem.at[slot]).wait()
            return carry
        lax.fori_loop(0, TILE_CHUNKS, body, 0, unroll=True)

    def zero_fill(wait):
        def per_tile(i, carry):
            def body(c, inner):
                copy = _chunk_copy(zero_buf, 0, y_ref, i * LOCAL_CHUNKS + c, zero_sem)
                if wait:
                    copy.wait()
                else:
                    copy.start()
                return inner
            return lax.fori_loop(used_ref[i], LOCAL_CHUNKS, body, carry)
        lax.fori_loop(0, used_ref.shape[0], per_tile, 0)

    @pl.when(j == 0)
    def _():
        fetch_weights(grp_ref[0])
        zero_buf[...] = jnp.zeros_like(zero_buf)
        scratch0 = used_ref.shape[0] * LOCAL_CHUNKS
        for wait in (False, True):
            for k in range(2 * TILE_CHUNKS):
                copy = _chunk_copy(zero_buf, 0, y_ref, scratch0 + k, zero_sem)
                copy.wait() if wait else copy.start()
        zero_fill(wait=False)
        gather(0, 0)

    @pl.when(jnp.logical_or(j == 0, grp_ref[j] != grp_ref[jnp.maximum(j - 1, 0)]))
    def _():
        enter_group()

    @pl.when(j + 1 < n_tiles)
    def _():
        gather(j + 1, (j + 1) % 2)

    @pl.when(j < n_tiles)
    def _():
        slot = j % 2
        wait_gather(slot)
        rows = in_buf[slot].reshape(EXPERT_TILE, in_buf.shape[-1])
        h = rows[:, 0:d_model]
        weights = rows[:, d_model:d_model + LANES].astype(F32) + rows[:, d_model + LANES:].astype(F32)
        experts = range(EXPERTS_PER_GROUP)
        up = [(_dot(h, w1_ref[k]), _dot(h, w3_ref[k])) for k in experts]
        hid = [(a * _sigmoid(a) * g * weights[:, k:k + 1]).astype(BF16) for k, (a, g) in zip(experts, up)]
        y = _dot(hid[0], w2_ref[0])
        for k in experts[1:]:
            y = y + _dot(hid[k], w2_ref[k])

        @pl.when(j >= 2)
        def _():
            wait_scatter(slot)

        out_buf[slot] = y.astype(BF16).reshape(TILE_CHUNKS, CHUNK, d_model)
        scatter(j, slot)

        @pl.when(j == n_tiles - 1)
        def _():
            @pl.when(j >= 1)
            def _():
                wait_scatter(1 - slot)
            wait_scatter(slot)
            zero_fill(wait=True)


def _combine_kernel(x1_ref, mod_ref, lpos_ref, y_ref, o_ref):
    tm = MOE_TILE
    half = o_ref.shape[1]
    per_tile = tm // half
    local_row = lax.broadcasted_iota(jnp.int32, (tm, LOCAL_ROWS), 1).astype(F32)
    for t in range(x1_ref.shape[0] // tm):
        lpos = jnp.broadcast_to(lpos_ref[t][0:1, :], (LANES, tm)).T[:, 0:1]
        unsort = jnp.where(local_row == lpos, 1.0, 0.0).astype(BF16)
        y = _dot(unsort, y_ref[t * LOCAL_ROWS:(t + 1) * LOCAL_ROWS, :])
        for i in range(per_tile):
            s = t * per_tile + i
            rows = slice(t * tm + i * half, t * tm + (i + 1) * half)
            o_ref[s] = x1_ref[rows, :] + mod_ref[s, 5:6, :] * y[i * half:(i + 1) * half, :]


def _moe_plan(cnt):
    n_local = cnt.shape[0]
    chunks = (cnt + CHUNK - 1) // CHUNK
    used = jnp.sum(chunks, axis=1)
    local_off = jnp.cumsum(chunks, axis=1) - chunks
    tiles_g = (jnp.sum(chunks, axis=0) + TILE_CHUNKS - 1) // TILE_CHUNKS
    tile_end = jnp.cumsum(tiles_g)
    n_tiles = tile_end[-1]
    group_start = (tile_end - tiles_g) * TILE_CHUNKS
    seg_len = chunks.T.reshape(-1)
    seg_start = (group_start[:, None] + (jnp.cumsum(chunks, axis=0) - chunks).T).reshape(-1)
    seg_src = (jnp.arange(n_local)[None, :] * LOCAL_CHUNKS + local_off.T).reshape(-1)
    max_chunks = n_local * MOE_TILE // CHUNK + n_local * N_GROUPS + N_GROUPS * TILE_CHUNKS
    max_tiles = (max_chunks + TILE_CHUNKS - 1) // TILE_CHUNKS
    c = jnp.arange(max_tiles * TILE_CHUNKS)[:, None]
    within = c - seg_start[None, :]
    hit = jnp.logical_and(within >= 0, within < seg_len[None, :])
    valid = jnp.any(hit, axis=1)
    src = jnp.sum(jnp.where(hit, seg_src[None, :] + within, 0), axis=1)
    src = jnp.where(valid, src, LOCAL_CHUNKS - 1)
    slot_k = c[:, 0] % (2 * TILE_CHUNKS)
    dst = jnp.where(valid, src, n_local * LOCAL_CHUNKS + slot_k)
    j = jnp.minimum(jnp.arange(max_tiles), n_tiles - 1)
    grp = jnp.sum(j[:, None] >= tile_end[None, :], axis=1)
    gid = jnp.arange(N_GROUPS)
    later = jnp.where(jnp.logical_and(gid[None, :] > gid[:, None], tiles_g[None, :] > 0), gid[None, :], N_GROUPS)
    next_of = jnp.min(later, axis=1)
    next_of = jnp.where(next_of == N_GROUPS, gid, next_of)
    next_grp = jnp.sum(jnp.where(grp[:, None] == gid[None, :], next_of[None, :], 0), axis=1)
    i32 = lambda a: a.astype(jnp.int32)
    return i32(src), i32(dst), i32(grp), i32(next_grp), i32(n_tiles).reshape(1), i32(used)


def _moe(plan, rows_local, w1g, w3g, w2g):
    src, dst, grp, next_grp, n_tiles, used = plan
    D = w2g.shape[3]
    n_rows, cols = rows_local.shape
    n_chunks = n_rows // CHUNK
    hbm = pl.BlockSpec(memory_space=pl.ANY)
    group_weights = [w1g, w3g, w2g]
    y = pl.pallas_call(
        _moe_kernel,
        grid_spec=pltpu.PrefetchScalarGridSpec(
            num_scalar_prefetch=6,
            grid=(grp.shape[0],),
            in_specs=[hbm, hbm, hbm, hbm],
            out_specs=hbm,
            scratch_shapes=([pltpu.VMEM((2, TILE_CHUNKS, CHUNK, cols), BF16),
                             pltpu.VMEM((2, TILE_CHUNKS, CHUNK, D), BF16),
                             pltpu.VMEM((1, CHUNK, D), BF16)]
                            + [pltpu.VMEM(w.shape[1:], F32) for w in group_weights]
                            + [pltpu.VMEM(w.shape[1:], BF16) for w in group_weights]
                            + [pltpu.SemaphoreType.DMA((2,)), pltpu.SemaphoreType.DMA((2,)),
                               pltpu.SemaphoreType.DMA(()), pltpu.SemaphoreType.DMA(())])),
        out_shape=jax.ShapeDtypeStruct((n_chunks + 2 * TILE_CHUNKS, CHUNK, D), BF16),
        compiler_params=_params("arbitrary"),
        name="moe",
    )(src, dst, grp, next_grp, n_tiles, used, rows_local.reshape(n_chunks, CHUNK, cols), w1g, w3g, w2g)
    return y.reshape((n_chunks + 2 * TILE_CHUNKS) * CHUNK, D)


def _combine(x1, mod, lpos, y_local, *, B, T):
    N, D = x1.shape
    L = WINDOW
    subs = B * L // MOE_TILE
    return pl.pallas_call(
        _combine_kernel,
        grid=(T // L,),
        in_specs=[pl.BlockSpec((B * L, D), lambda i: (i, 0)),
                  pl.BlockSpec(mod.shape, lambda i: (0, 0, 0)),
                  pl.BlockSpec((subs, 8, MOE_TILE), lambda i: (i, 0, 0)),
                  pl.BlockSpec((subs * LOCAL_ROWS, D), lambda i: (i, 0))],
        out_specs=pl.BlockSpec((B, L, D), lambda i: (0, i, 0)),
        out_shape=jax.ShapeDtypeStruct((B, T, D), F32),
        compiler_params=_params("arbitrary"),
        name="moe_combine",
    )(x1, mod, lpos, y_local)


def _block_diag(n, blk, value, dtype):
    r = np.arange(n)[:, None] // blk
    c = np.arange(n)[None, :] // blk
    return jnp.asarray(np.where(r == c, value, 0.0), dtype)


def _gla_constants():
    L = GLA_CHUNK
    i = np.arange(L)[:, None]
    j = np.arange(L)[None, :]
    tri = j <= i
    bdtri = np.logical_and(j <= i, i // GLA_SUB == j // GLA_SUB)
    cum = np.block([[tri, tri], [bdtri, bdtri]])
    bdms = _block_diag(2 * GLA_DV, GLA_DV, 1.0 / GLA_DV, BF16)
    d = np.arange(LANES)[:, None] // GLA_DK
    e = np.arange(2 * GLA_DV)[None, :] // GLA_DV
    smask = d == e
    return jnp.asarray(cum, BF16), bdms, jnp.asarray(smask, F32)


def kernel(x, c, w_ada, b_ada, g_norm1, w_in, q_norm, k_norm, sinks, w_gk2, b_gk, g_gla_out, g_att_out,
           w_out, g_norm2, w_group, b_group, w_router, b_router, w1, w3, w2):
    B, T, D = x.shape
    N = B * T
    depth = w_ada.shape[0]
    cum, bdms, smask = _gla_constants()
    bdq = _block_diag(2 * LANES, HEAD_DIM, 1.0 / HEAD_DIM, BF16)
    bdk = _block_diag(ATT_KV, HEAD_DIM, 1.0 / HEAD_DIM, BF16)

    for l in range(depth):
        mod = _adaln_mod(c, w_ada[l], b_ada[l]).reshape(B, 6, D)

        wgk = jnp.concatenate([w_gk2[l], jnp.zeros((LANES - GLA_RANK, GLA_K), F32)], axis=0).astype(BF16)
        qa, ka, va, qg, kg, vg, la, og = _in_proj(
            x.reshape(N, D), mod, g_norm1[l].reshape(1, D), w_in[l].T,
            jnp.tile(q_norm[l], ATT_HEADS).reshape(1, ATT_Q), jnp.tile(k_norm[l], ATT_KV_HEADS).reshape(1, ATT_KV),
            bdq, bdk, wgk, b_gk[l].reshape(1, GLA_K), T=T, tm=PROJ_TILE)

        pad = LANES - N_GROUPS - N_EXPERTS
        wr_t = jnp.concatenate([w_group[l], w_router[l], jnp.zeros((D, pad), F32)], axis=1).T.astype(BF16)
        br_t = jnp.concatenate([b_group[l], b_router[l], jnp.zeros((pad,), F32)]).reshape(LANES, 1)
        striu = jnp.asarray(np.arange(MOE_TILE)[:, None] < np.arange(MOE_TILE)[None, :], BF16)
        x1, rows_local, lpos, cnt = _mix_out(
            sinks[l], qa, ka, va, g_att_out[l].reshape(1, ATT_Q),
            qg, kg, vg, la, og, cum, bdms, smask, jnp.tile(g_gla_out[l], GLA_HEADS).reshape(1, GLA_V),
            x, mod, w_out[l].astype(BF16), g_norm2[l].reshape(1, D), wr_t, br_t, striu)
        plan = _moe_plan(cnt[:, :N_GROUPS, 0].astype(jnp.int32))
        by_group = lambda w: w.reshape((N_GROUPS, EXPERTS_PER_GROUP) + w.shape[1:])
        y_local = _moe(plan, rows_local, by_group(w1[l]), by_group(w3[l]), by_group(w2[l]))
        x = _combine(x1, mod, lpos, y_local, B=B, T=T)
    return x
```

```python
import jax
import jax.numpy as jnp
import numpy as np
from jax import lax
from jax.experimental import pallas as pl
from jax.experimental.pallas import tpu as pltpu

F32 = jnp.float32
BF16 = jnp.bfloat16

EPS = 1e-6
LOG2_E = 1.4426950408889634
ATT_HEADS = 8
ATT_KV_HEADS = 2
HEAD_DIM = 64
WINDOW = 128
ATT_Q = ATT_HEADS * HEAD_DIM
ATT_KV = ATT_KV_HEADS * HEAD_DIM
GLA_HEADS = 4
GLA_DK = 64
GLA_DV = 128
GLA_RANK = 16
GLA_NORMALIZER = 16.0
GLA_K = GLA_HEADS * GLA_DK
GLA_V = GLA_HEADS * GLA_DV
N_GROUPS = 4
EXPERTS_PER_GROUP = 4
N_EXPERTS = N_GROUPS * EXPERTS_PER_GROUP

LANES = 128
MOD_COLS = 768
PROJ_SPLITS = (512, 512)
PROJ_TILE = sum(PROJ_SPLITS)
ROUTER_ROWS = 24
GLA_CHUNK = 128
GLA_SUB = 16
N_SUB = GLA_CHUNK // GLA_SUB
ROUTER_LANE0 = N_GROUPS
VMEM_LIMIT = 62 * 1024 * 1024
MOE_TILE = 256
EXPERT_TILE = 256
CHUNK = 16
TILE_CHUNKS = EXPERT_TILE // CHUNK
LOCAL_CHUNKS = (MOE_TILE + N_GROUPS * (CHUNK - 1)) // CHUNK + 2
LOCAL_ROWS = LOCAL_CHUNKS * CHUNK

_QA0, _KA0, _VA0 = 0, ATT_Q, ATT_Q + ATT_KV
_QG0 = _VA0 + ATT_KV
_KG0 = _QG0 + GLA_K
_VG0 = _KG0 + GLA_K
_OG0 = _VG0 + GLA_V
_LR0 = _OG0 + GLA_V
IN_COLS_PAD = _LR0 + LANES


def _dot(a, b):
    return jnp.dot(a, b, preferred_element_type=F32)


def _dot_nt(a, b):
    return lax.dot_general(a, b, (((1,), (1,)), ((), ())), preferred_element_type=F32)


def _sigmoid(x):
    return 1.0 / (1.0 + jnp.exp(-x))


def _params(*sem):
    return pltpu.CompilerParams(dimension_semantics=sem, vmem_limit_bytes=VMEM_LIMIT)


def _mod_kernel(c_ref, w_ref, b_ref, o_ref):
    c = c_ref[...]
    s = (c * _sigmoid(c)).astype(BF16)
    o_ref[...] = _dot(s, w_ref[...].astype(BF16)) + b_ref[...]


def _adaln_mod(c, w_ada, b_ada):
    B, D = c.shape
    n = w_ada.shape[1]
    tn = MOD_COLS
    return pl.pallas_call(
        _mod_kernel,
        grid=(n // tn,),
        in_specs=[pl.BlockSpec((B, D), lambda j: (0, 0)),
                  pl.BlockSpec((D, tn), lambda j: (0, j)),
                  pl.BlockSpec((1, tn), lambda j: (0, j))],
        out_specs=pl.BlockSpec((B, tn), lambda j: (0, j)),
        out_shape=jax.ShapeDtypeStruct((B, n), F32),
        compiler_params=_params("arbitrary"),
        name="adaln_mod",
    )(c, w_ada, b_ada.reshape(1, n))


def _inproj_body(x_ref, mod_ref, g1_ref, wf_ref, qn_ref, kn_ref, bdq_ref, bdk_ref, wgk_ref, bgk_ref,
                 qa_ref, ka_ref, va_ref, qg_ref, kg_ref, vg_ref, la_ref, og_ref, w_ref):
    blk = x_ref.shape[1]
    subs = range(len(PROJ_SPLITS))
    seqs_of = [range(sum(PROJ_SPLITS[:t]) // blk, sum(PROJ_SPLITS[:t + 1]) // blk) for t in subs]

    def put(ref, t, value, lanes=slice(None)):
        for i, s in enumerate(seqs_of[t]):
            ref[s, :, lanes] = value[i * blk:(i + 1) * blk, :]

    @pl.when(pl.program_id(1) == 0)
    def _():
        lr_src = _OG0
        w_ref[0:_OG0, :] = wf_ref[0:_OG0, :].astype(BF16)
        w_ref[_OG0:_LR0, :] = wf_ref[lr_src + GLA_RANK:lr_src + GLA_RANK + GLA_V, :].astype(BF16)
        w_ref[_LR0:_LR0 + GLA_RANK, :] = wf_ref[lr_src:lr_src + GLA_RANK, :].astype(BF16)
        w_ref[_LR0 + GLA_RANK:IN_COLS_PAD, :] = jnp.zeros((LANES - GLA_RANK, wf_ref.shape[1]), BF16)

    h = []
    for t in subs:
        parts = []
        for s in seqs_of[t]:
            x = x_ref[s]
            ms = jnp.mean(x * x, axis=-1, keepdims=True)
            xn = x * lax.rsqrt(ms + EPS) * g1_ref[...]
            parts.append((xn * (1.0 + mod_ref[s, 1:2, :]) + mod_ref[s, 0:1, :]).astype(BF16))
        h.append(jnp.concatenate(parts, axis=0))

    proj = lambda t, c0, width: _dot_nt(h[t], w_ref[c0:c0 + width, :])
    gate_of = lambda lr: _dot(lr.astype(BF16), wgk_ref[...]) + bgk_ref[...]
    qa, kv, q_ms, k_ms, qk_g, vg, og, lr, gate = ({} for _ in range(9))
    for t in subs:
        qa[t] = proj(t, _QA0, ATT_Q)
        if t > 0:
            gate[t - 1] = gate_of(lr[t - 1])
        kv[t] = proj(t, _KA0, 2 * ATT_KV)
        q_sq = (qa[t] * qa[t]).astype(BF16)
        q_ms[t] = jnp.concatenate([_dot(q_sq[:, c:c + 2 * LANES], bdq_ref[...])
                                   for c in range(0, ATT_Q, 2 * LANES)], axis=1)
        qk_g[t] = proj(t, _QG0, 2 * GLA_K)
        k = kv[t][:, 0:ATT_KV]
        k_ms[t] = _dot((k * k).astype(BF16), bdk_ref[...])
        vg[t] = proj(t, _VG0, GLA_V)
        og[t] = proj(t, _OG0, GLA_V)
        lr[t] = proj(t, _LR0, LANES)
    gate[subs[-1]] = gate_of(lr[subs[-1]])

    for t in subs:
        low = lax.broadcasted_iota(jnp.int32, (PROJ_SPLITS[t], ATT_KV), 1) < HEAD_DIM
        put(qa_ref, t, (qa[t] * lax.rsqrt(q_ms[t] + EPS) * qn_ref[...] * (HEAD_DIM ** -0.5 * LOG2_E)).astype(BF16))
        k = kv[t][:, 0:ATT_KV] * lax.rsqrt(k_ms[t] + EPS) * kn_ref[...]
        v = kv[t][:, ATT_KV:2 * ATT_KV]
        for src, dst in ((k, ka_ref), (v, va_ref)):
            swapped = pltpu.roll(src, HEAD_DIM, axis=1)
            put(dst, t, jnp.where(low, src, swapped).astype(BF16), slice(0, LANES))
            put(dst, t, jnp.where(low, swapped, src).astype(BF16), slice(LANES, 2 * LANES))
        put(qg_ref, t, (qk_g[t][:, 0:GLA_K] * (GLA_DK ** -0.5)).astype(BF16))
        put(kg_ref, t, qk_g[t][:, GLA_K:2 * GLA_K].astype(BF16))
        put(vg_ref, t, vg[t].astype(BF16))
        put(og_ref, t, (og[t] * _sigmoid(og[t])).astype(BF16))
        log_sig = jnp.minimum(gate[t], 0.0) - jnp.log(1.0 + jnp.exp(-jnp.abs(gate[t])))
        put(la_ref, t, log_sig * (1.0 / GLA_NORMALIZER))


def _attn_stages(seqs, sinks_ref, q_ref, kc_ref, vc_ref, gatt_ref, o_ref, kp_ref, vp_ref):
    blk = WINDOW
    first = pl.program_id(1) == 0
    qi = lax.broadcasted_iota(jnp.int32, (blk, blk), 0)
    cj = lax.broadcasted_iota(jnp.int32, (blk, blk), 1)
    from_prev = cj > qi
    dist = (qi - cj + jnp.where(from_prev, blk, 0)).astype(F32)
    no_prev = jnp.where(jnp.logical_and(from_prev, first), -1e30, 0.0)
    low = cj < HEAD_DIM
    half = (jnp.where(low, 1.0, 0.0).astype(BF16), jnp.where(low, 0.0, 1.0).astype(BF16))
    half2 = tuple(jnp.concatenate([m, m], axis=0) for m in half)
    prev_mask = jnp.where(from_prev, 1.0, 0.0).astype(BF16)
    cur_mask = jnp.where(from_prev, 0.0, 1.0).astype(BF16)

    n_pairs = ATT_HEADS // 2
    pairs_per_kv = n_pairs // ATT_KV_HEADS
    units = [(bi, j) for bi in seqs for j in range(n_pairs)]

    def kv_blocks(bi, g):
        lanes = slice(g * LANES, (g + 1) * LANES)
        return (kp_ref[bi, :, lanes], vp_ref[bi, :, lanes]), (kc_ref[bi, :, lanes], vc_ref[bi, :, lanes])

    scores = {}
    for bi in seqs:
        for g in range(ATT_KV_HEADS):
            (kp, _), (kc, _) = kv_blocks(bi, g)
            k_both = jnp.concatenate([kp, kc], axis=0)
            group = range(g * pairs_per_kv, (g + 1) * pairs_per_kv)
            q_stack = jnp.concatenate([q_ref[bi, :, j * LANES:(j + 1) * LANES] for j in group], axis=0)
            for p in range(2):
                s_stack = _dot_nt(q_stack, k_both * half2[p])
                for jj, j in enumerate(group):
                    scores[bi, j, p] = s_stack[jj * blk:(jj + 1) * blk, :]
    yield

    bias = [2.0 ** (-8.0 * (h + 1) / ATT_HEADS) * LOG2_E * dist - no_prev for h in range(ATT_HEADS)]
    probs, sink_terms = {}, {}
    for bi, j in units:
        for p in range(2):
            h = 2 * j + p
            s_both = scores[bi, j, p]
            s = jnp.where(from_prev, s_both[:, 0:blk], s_both[:, blk:2 * blk]) - bias[h]
            sink = sinks_ref[h] * LOG2_E
            m = jnp.maximum(jnp.max(s, axis=-1, keepdims=True), sink)
            probs[bi, j, p] = jnp.exp2(s - m)
            sink_terms[bi, j, p] = jnp.exp2(sink - m)
    yield

    outs = {}
    for bi in seqs:
        for g in range(ATT_KV_HEADS):
            (_, vp), (_, vc) = kv_blocks(bi, g)
            v_stack = jnp.concatenate([jnp.concatenate([v * half[p], half[p]], axis=1)
                                       for v in (vp, vc) for p in range(2)], axis=0)
            group = range(g * pairs_per_kv, (g + 1) * pairs_per_kv)
            p_stack = []
            for j in group:
                e = [probs[bi, j, p].astype(BF16) for p in range(2)]
                p_stack.append(jnp.concatenate([x * m for m in (prev_mask, cur_mask) for x in e], axis=1))
            pv_stack = _dot(jnp.concatenate(p_stack, axis=0), v_stack)
            for jj, j in enumerate(group):
                pv = pv_stack[jj * blk:(jj + 1) * blk, :]
                den = pv[:, LANES:2 * LANES] + jnp.where(low, sink_terms[bi, j, 0], sink_terms[bi, j, 1])
                outs[bi, j] = pv[:, 0:LANES] / den
    yield

    for bi in seqs:
        o = jnp.concatenate([outs[bi, j] for j in range(n_pairs)], axis=1)
        ms = jnp.mean(o * o, axis=-1, keepdims=True)
        o_ref[bi] = (o * lax.rsqrt(ms + EPS) * gatt_ref[...]).astype(BF16)
        kp_ref[bi] = kc_ref[bi]
        vp_ref[bi] = vc_ref[bi]


def _gla_stages(seqs, q_ref, k_ref, v_ref, la_ref, og_ref, cum_ref, bdms_ref, smask_ref,
                ggla_ref, o_ref, state_ref):
    L = GLA_CHUNK
    lane = lax.broadcasted_iota(jnp.int32, (GLA_SUB, LANES), 1)
    head_mask = [jnp.where(lane // GLA_DK == hh, 1.0, 0.0).astype(BF16) for hh in range(2)]
    causal = lax.broadcasted_iota(jnp.int32, (L, L), 0) >= lax.broadcasted_iota(jnp.int32, (L, L), 1)
    units = [(s, pair) for s in seqs for pair in range(GLA_HEADS // 2)]
    kl = lambda pair: slice(pair * LANES, (pair + 1) * LANES)
    vl = lambda pair: slice(pair * 2 * GLA_DV, (pair + 1) * 2 * GLA_DV)

    b, b_in = {}, {}
    for s in seqs:
        la = la_ref[s]
        la_hi = la.astype(BF16)
        la_lo = (la - la_hi.astype(F32)).astype(BF16)
        sums = _dot(cum_ref[...], jnp.concatenate([la_hi, la_lo], axis=0)) * LOG2_E
        b[s] = sums[0:L, :]
        b_in[s] = sums[L:2 * L, :]
    yield

    q_both, keys, q_dec, k_dec_t, b_last = {}, {}, {}, {}, {}
    for s in seqs:
        ref = b[s] - b_in[s]
        b_last[s] = b[s][L - 1:L, :]
        q = q_ref[s].astype(F32)
        k = k_ref[s].astype(F32)
        q_in = (q * jnp.exp2(b_in[s])).astype(BF16)
        q_dec[s] = (q * jnp.exp2(b[s])).astype(BF16)
        k_dec = k * jnp.exp2(b_last[s] - b[s])
        for pair in range(GLA_HEADS // 2):
            k_p, b_p, ref_p = k[:, kl(pair)], b[s][:, kl(pair)], ref[:, kl(pair)]
            expanded = []
            for g in range(N_SUB):
                top = (g + 1) * GLA_SUB
                live = (k_p[0:top, :] * jnp.exp2(ref_p[g * GLA_SUB:g * GLA_SUB + 1, :] - b_p[0:top, :])).astype(BF16)
                expanded.append(live if top == L else
                                jnp.concatenate([live, jnp.zeros((L - top, LANES), BF16)], axis=0))
            keys[s, pair] = jnp.concatenate(expanded, axis=1)
            q_p = q_in[:, kl(pair)]
            zero_group = jnp.zeros((GLA_SUB, LANES), BF16)
            q_both[s, pair] = []
            for t in range(N_SUB // 2):
                lhs = []
                for hh in range(2):
                    for side in range(2):
                        g = 2 * t + side
                        piece = q_p[g * GLA_SUB:(g + 1) * GLA_SUB, :] * head_mask[hh]
                        lhs.append(jnp.concatenate([piece, zero_group] if side == 0 else [zero_group, piece], axis=1))
                q_both[s, pair].append(jnp.concatenate(lhs, axis=0))
            k_dec_t[s, pair] = k_dec[:, kl(pair)].T.astype(BF16)
    yield

    scores = {(u, t): _dot_nt(q_both[u][t], keys[u][:, 2 * t * LANES:2 * (t + 1) * LANES])
              for u in units for t in range(N_SUB // 2)}
    yield

    outs, updates = {}, {}
    for s, pair in units:
        v_p = v_ref[s, :, vl(pair)]
        o_parts = []
        for hh in range(2):
            a = jnp.concatenate([scores[(s, pair), t][2 * hh * GLA_SUB:2 * (hh + 1) * GLA_SUB, :]
                                 for t in range(N_SUB // 2)], axis=0)
            a = jnp.where(causal, a, 0.0).astype(BF16)
            o_parts.append(_dot(a, v_p[:, hh * GLA_DV:(hh + 1) * GLA_DV]))
        state = state_ref[s, pair]
        outs[s, pair] = jnp.concatenate(o_parts, axis=1) + _dot(q_dec[s][:, kl(pair)], state.astype(BF16))
        updates[s, pair] = _dot(k_dec_t[s, pair], v_p)
    yield

    for s, pair in units:
        decay = jnp.broadcast_to(jnp.exp2(b_last[s][:, kl(pair)]), (LANES, LANES)).T
        state_ref[s, pair] = (state_ref[s, pair] * jnp.concatenate([decay, decay], axis=1)
                              + updates[s, pair] * smask_ref[...])
        o = outs[s, pair]
        ms = _dot((o * o).astype(BF16), bdms_ref[...])
        y = o * lax.rsqrt(ms + EPS) * ggla_ref[:, vl(pair)] * og_ref[s, :, vl(pair)].astype(F32)
        o_ref[s, :, vl(pair)] = y.astype(BF16)


def _mixers_kernel(sinks_ref, qa_ref, ka_ref, va_ref, gatt_ref,
                   qg_ref, kg_ref, vg_ref, la_ref, og_ref, cum_ref, bdms_ref, smask_ref, ggla_ref,
                   ya_ref, yg_ref, kp_ref, vp_ref, state_ref):
    @pl.when(pl.program_id(1) == 0)
    def _():
        state_ref[...] = jnp.zeros_like(state_ref)
        kp_ref[...] = jnp.zeros_like(kp_ref)
        vp_ref[...] = jnp.zeros_like(vp_ref)

    n_seq = qa_ref.shape[0]
    waiting = []
    for group in (tuple(range(0, n_seq // 2)), tuple(range(n_seq // 2, n_seq))):
        waiting.append([
            _attn_stages(group, sinks_ref, qa_ref, ka_ref, va_ref, gatt_ref, ya_ref, kp_ref, vp_ref),
            _gla_stages(group, qg_ref, kg_ref, vg_ref, la_ref, og_ref, cum_ref, bdms_ref, smask_ref,
                        ggla_ref, yg_ref, state_ref)])
    pending = []
    while pending or waiting:
        if waiting:
            pending.extend(waiting.pop(0))
        for stages in list(pending):
            if next(stages, "done") == "done":
                pending.remove(stages)


def _route(logits_t):
    lt = logits_t[0:ROUTER_ROWS, :]
    row = lax.broadcasted_iota(jnp.int32, lt.shape, 0)
    neg_inf = -jnp.inf
    g_log = jnp.where(row < N_GROUPS, lt, neg_inf)
    g_max = jnp.max(g_log, axis=0, keepdims=True)
    g_sel = jnp.min(jnp.where(g_log == g_max, row, LANES), axis=0, keepdims=True)
    g_sum = jnp.sum(jnp.where(row < N_GROUPS, jnp.exp(lt - g_max), 0.0), axis=0, keepdims=True)
    p_group = 1.0 / g_sum
    e_lo = ROUTER_LANE0 + EXPERTS_PER_GROUP * g_sel
    in_group = jnp.logical_and(row >= e_lo, row < e_lo + EXPERTS_PER_GROUP)
    e_log = jnp.where(in_group, lt, neg_inf)
    e_max = jnp.max(e_log, axis=0, keepdims=True)
    top1 = jnp.min(jnp.where(e_log == e_max, row, LANES), axis=0, keepdims=True)
    e_log2 = jnp.where(row == top1, neg_inf, e_log)
    e_max2 = jnp.max(e_log2, axis=0, keepdims=True)
    top2 = jnp.min(jnp.where(e_log2 == e_max2, row, LANES), axis=0, keepdims=True)
    ratio = jnp.exp(e_max2 - e_max)
    w_top1 = p_group / (1.0 + ratio)
    w_top2 = p_group * ratio / (1.0 + ratio)
    row8 = lax.broadcasted_iota(jnp.int32, (8, lt.shape[1]), 0)
    weights = jnp.where(row8 == top1 - e_lo, w_top1, 0.0) + jnp.where(row8 == top2 - e_lo, w_top2, 0.0)
    return g_sel, weights


def _outproj_body(ya_ref, yg_ref, x_ref, mod_ref, wo_ref, g2_ref, wrt_ref, brt_ref, striu_ref,
                  x1_ref, row_ref, lpos_ref, cnt_ref):
    tm = MOE_TILE
    half = x_ref.shape[1]
    per_tile = tm // half
    subs = range(x_ref.shape[0] // per_tile)
    rows = lambda t: slice(t * tm, (t + 1) * tm)
    tile_of = lambda ref, t: jnp.concatenate([ref[t * per_tile + i] for i in range(per_tile)], axis=0)

    mix = [_dot(tile_of(ya_ref, t), wo_ref[0:ATT_Q, :]) + _dot(tile_of(yg_ref, t), wo_ref[ATT_Q:ATT_Q + GLA_V, :])
           for t in subs]
    h2b = []
    for t in subs:
        parts = []
        for i in range(per_tile):
            s = t * per_tile + i
            x1 = x_ref[s] + mod_ref[s, 2:3, :] * mix[t][i * half:(i + 1) * half, :]
            x1_ref[t * tm + i * half:t * tm + (i + 1) * half, :] = x1
            ms = jnp.mean(x1 * x1, axis=-1, keepdims=True)
            h2 = (x1 * lax.rsqrt(ms + EPS) * g2_ref[...]) * (1.0 + mod_ref[s, 4:5, :]) + mod_ref[s, 3:4, :]
            parts.append(h2.astype(BF16))
        h2b.append(jnp.concatenate(parts, axis=0))
    logits_t = [_dot_nt(wrt_ref[...], h2b[t]) + brt_ref[...] for t in subs]

    routed = [_route(logits_t[t]) for t in subs]
    row8 = lax.broadcasted_iota(jnp.int32, (8, tm), 0)
    onehot = [jnp.where(row8 == routed[t][0], 1.0, 0.0) for t in subs]
    before = [_dot(onehot[t].astype(BF16), striu_ref[...]) for t in subs]

    local_row = lax.broadcasted_iota(jnp.int32, (LOCAL_ROWS, tm), 0).astype(F32)
    pad_rows = jnp.zeros((LANES - 8, tm), F32)
    for t in subs:
        count = jnp.sum(onehot[t], axis=1, keepdims=True)
        cnt_ref[t] = jnp.broadcast_to(count, (8, LANES))
        padded = jnp.broadcast_to(jnp.floor((count + (CHUNK - 1.0)) * (1.0 / CHUNK)) * CHUNK, (8, tm))
        start = jnp.zeros((8, tm), F32)
        for shift in range(1, N_GROUPS):
            start = start + jnp.where(row8 >= shift, pltpu.roll(padded, shift, axis=0), 0.0)
        lpos = jnp.sum(onehot[t] * (before[t] + start), axis=0, keepdims=True)
        lpos_ref[t] = jnp.broadcast_to(lpos, (8, tm))
        weights = jnp.concatenate([routed[t][1], pad_rows], axis=0).T
        w_hi = weights.astype(BF16)
        w_lo = (weights - w_hi.astype(F32)).astype(BF16)
        perm = jnp.where(local_row == lpos, 1.0, 0.0).astype(BF16)
        row_ref[t * LOCAL_ROWS:(t + 1) * LOCAL_ROWS, :] = _dot(
            perm, jnp.concatenate([h2b[t], w_hi, w_lo], axis=1)).astype(BF16)


def _mix_out_kernel(sinks_ref, x_ref, mod_ref, g1_ref, wf_ref, qn_ref, kn_ref, bdq_ref, bdk_ref, wgk_ref, bgk_ref,
                    gatt_ref, cum_ref, bdms_ref, smask_ref, ggla_ref,
                    wo_ref, g2_ref, wrt_ref, brt_ref, striu_ref,
                    x1_ref, row_ref, lpos_ref, cnt_ref,
                    w_ref, qa_ref, ka_ref, va_ref, qg_ref, kg_ref, vg_ref, la_ref, og_ref,
                    ya_ref, yg_ref, kp_ref, vp_ref, state_ref):
    _inproj_body(x_ref, mod_ref, g1_ref, wf_ref, qn_ref, kn_ref, bdq_ref, bdk_ref, wgk_ref, bgk_ref,
                 qa_ref, ka_ref, va_ref, qg_ref, kg_ref, vg_ref, la_ref, og_ref, w_ref)
    _mixers_kernel(sinks_ref, qa_ref, ka_ref, va_ref, gatt_ref,
                   qg_ref, kg_ref, vg_ref, la_ref, og_ref, cum_ref, bdms_ref, smask_ref, ggla_ref,
                   ya_ref, yg_ref, kp_ref, vp_ref, state_ref)
    _outproj_body(ya_ref, yg_ref, x_ref, mod_ref, wo_ref, g2_ref, wrt_ref, brt_ref, striu_ref,
                  x1_ref, row_ref, lpos_ref, cnt_ref)


def _mix_out(sinks, x, mod, g1, w_in_t, qn, kn, bdq, bdk, wgk, bgk, gatt, cum, bdms, smask, ggla,
             wo, g2, wr, br, striu):
    assert GLA_CHUNK == WINDOW and MOE_TILE % WINDOW == 0
    B, T, D = x.shape
    L = WINDOW
    N = B * T
    assert B * L == PROJ_TILE
    tiles_per_step = B * L // MOE_TILE
    const = lambda a: pl.BlockSpec(a.shape, lambda b, i: (0,) * a.ndim)
    tiles = lambda *shape: pl.BlockSpec((tiles_per_step,) + shape, lambda b, i: (i,) + (0,) * len(shape))
    act = lambda c, dt=BF16: pltpu.VMEM((B, L, c), dt)
    return pl.pallas_call(
        _mix_out_kernel,
        grid=(1, T // L),
        in_specs=[pl.BlockSpec(memory_space=pltpu.SMEM),
                  pl.BlockSpec((B, L, D), lambda b, i: (0, i, 0)), const(mod), const(g1),
                  pl.BlockSpec(w_in_t.shape, lambda b, i: (0, 0), pipeline_mode=pl.Buffered(1)),
                  const(qn), const(kn), const(bdq), const(bdk), const(wgk), const(bgk),
                  const(gatt), const(cum), const(bdms), const(smask), const(ggla),
                  const(wo), const(g2), const(wr), const(br), const(striu)],
        out_specs=[pl.BlockSpec((B * L, D), lambda b, i: (i, 0)),
                   pl.BlockSpec((tiles_per_step * LOCAL_ROWS, D + 2 * LANES), lambda b, i: (i, 0)),
                   tiles(8, MOE_TILE), tiles(8, LANES)],
        out_shape=[jax.ShapeDtypeStruct((N, D), F32),
                   jax.ShapeDtypeStruct((N // MOE_TILE * LOCAL_ROWS, D + 2 * LANES), BF16),
                   jax.ShapeDtypeStruct((N // MOE_TILE, 8, MOE_TILE), F32),
                   jax.ShapeDtypeStruct((N // MOE_TILE, 8, LANES), F32)],
        scratch_shapes=[pltpu.VMEM((IN_COLS_PAD, D), BF16),
                        act(ATT_Q), act(2 * ATT_KV), act(2 * ATT_KV), act(GLA_K), act(GLA_K), act(GLA_V),
                        act(GLA_K, F32), act(GLA_V),
                        act(ATT_Q), act(GLA_V), act(2 * ATT_KV), act(2 * ATT_KV),
                        pltpu.VMEM((B, GLA_HEADS // 2, LANES, 2 * GLA_DV), F32)],
        compiler_params=_params("arbitrary", "arbitrary"),
        name="mix_out",
    )(sinks, x, mod, g1, w_in_t, qn, kn, bdq, bdk, wgk, bgk, gatt, cum, bdms, smask, ggla,
      wo, g2, wr, br, striu)


def _chunk_copy(src_ref, src_chunk, dst_ref, dst_chunk, sem):
    return pltpu.make_async_copy(src_ref.at[src_chunk], dst_ref.at[dst_chunk], sem)


def _moe_kernel(src_ref, dst_ref, grp_ref, next_ref, nt_ref, used_ref, rows_ref, w1f_ref, w3f_ref, w2f_ref, y_ref,
                in_buf, out_buf, zero_buf, st1_ref, st3_ref, st2_ref, w1_ref, w3_ref, w2_ref,
                in_sem, out_sem, zero_sem, w_sem):
    j = pl.program_id(0)
    n_tiles = nt_ref[0]
    d_model = w2_ref.shape[2]

    stages = ((w1f_ref, st1_ref, w1_ref), (w3f_ref, st3_ref, w3_ref), (w2f_ref, st2_ref, w2_ref))

    def fetch_weights(group):
        for hbm, stage, _ in stages:
            pltpu.make_async_copy(hbm.at[group], stage, w_sem).start()

    def enter_group():
        for hbm, stage, dst in stages:
            pltpu.make_async_copy(hbm.at[0], stage, w_sem).wait()
        for hbm, stage, dst in stages:
            for k in range(EXPERTS_PER_GROUP):
                dst[k] = stage[k].astype(BF16)

        @pl.when(next_ref[j] != grp_ref[j])
        def _():
            fetch_weights(next_ref[j])

    def gather(tile, slot):
        def body(k, carry):
            _chunk_copy(rows_ref, src_ref[tile * TILE_CHUNKS + k], in_buf.at[slot], k, in_sem.at[slot]).start()
            return carry
        lax.fori_loop(0, TILE_CHUNKS, body, 0, unroll=True)

    def wait_gather(slot):
        def body(k, carry):
            _chunk_copy(rows_ref, 0, in_buf.at[slot], k, in_sem.at[slot]).wait()
            return carry
        lax.fori_loop(0, TILE_CHUNKS, body, 0, unroll=True)

    def scatter(tile, slot):
        def body(k, carry):
            _chunk_copy(out_buf.at[slot], k, y_ref, dst_ref[tile * TILE_CHUNKS + k], out_sem.at[slot]).start()
            return carry
        lax.fori_loop(0, TILE_CHUNKS, body, 0, unroll=True)

    def wait_scatter(slot):
        def body(k, carry):
            _chunk_copy(out_buf.at[slot], k, y_ref, 0, out_sem.at[slot]).wait()
            return carry
        lax.fori_loop(0, TILE_CHUNKS, body, 0, unroll=True)

    def zero_fill(wait):
        def per_tile(i, carry):
            def body(c, inner):
                copy = _chunk_copy(zero_buf, 0, y_ref, i * LOCAL_CHUNKS + c, zero_sem)
                if wait:
                    copy.wait()
                else:
                    copy.start()
                return inner
            return lax.fori_loop(used_ref[i], LOCAL_CHUNKS, body, carry)
        lax.fori_loop(0, used_ref.shape[0], per_tile, 0)

    @pl.when(j == 0)
    def _():
        fetch_weights(grp_ref[0])
        zero_buf[...] = jnp.zeros_like(zero_buf)
        scratch0 = used_ref.shape[0] * LOCAL_CHUNKS
        for wait in (False, True):
            for k in range(2 * TILE_CHUNKS):
                copy = _chunk_copy(zero_buf, 0, y_ref, scratch0 + k, zero_sem)
                copy.wait() if wait else copy.start()
        zero_fill(wait=False)
        gather(0, 0)

    @pl.when(jnp.logical_or(j == 0, grp_ref[j] != grp_ref[jnp.maximum(j - 1, 0)]))
    def _():
        enter_group()

    @pl.when(j + 1 < n_tiles)
    def _():
        gather(j + 1, (j + 1) % 2)

    @pl.when(j < n_tiles)
    def _():
        slot = j % 2
        wait_gather(slot)
        rows = in_buf[slot].reshape(EXPERT_TILE, in_buf.shape[-1])
        h = rows[:, 0:d_model]
        weights = rows[:, d_model:d_model + LANES].astype(F32) + rows[:, d_model + LANES:].astype(F32)
        experts = range(EXPERTS_PER_GROUP)
        up = [(_dot(h, w1_ref[k]), _dot(h, w3_ref[k])) for k in experts]
        hid = [(a * _sigmoid(a) * g * weights[:, k:k + 1]).astype(BF16) for k, (a, g) in zip(experts, up)]
        y = _dot(hid[0], w2_ref[0])
        for k in experts[1:]:
            y = y + _dot(hid[k], w2_ref[k])

        @pl.when(j >= 2)
        def _():
            wait_scatter(slot)

        out_buf[slot] = y.astype(BF16).reshape(TILE_CHUNKS, CHUNK, d_model)
        scatter(j, slot)

        @pl.when(j == n_tiles - 1)
        def _():
            @pl.when(j >= 1)
            def _():
                wait_scatter(1 - slot)
            wait_scatter(slot)
            zero_fill(wait=True)


def _combine_kernel(x1_ref, mod_ref, lpos_ref, y_ref, o_ref):
    tm = MOE_TILE
    half = o_ref.shape[1]
    per_tile = tm // half
    local_row = lax.broadcasted_iota(jnp.int32, (tm, LOCAL_ROWS), 1).astype(F32)
    for t in range(x1_ref.shape[0] // tm):
        lpos = jnp.broadcast_to(lpos_ref[t][0:1, :], (LANES, tm)).T[:, 0:1]
        unsort = jnp.where(local_row == lpos, 1.0, 0.0).astype(BF16)
        y = _dot(unsort, y_ref[t * LOCAL_ROWS:(t + 1) * LOCAL_ROWS, :])
        for i in range(per_tile):
            s = t * per_tile + i
            rows = slice(t * tm + i * half, t * tm + (i + 1) * half)
            o_ref[s] = x1_ref[rows, :] + mod_ref[s, 5:6, :] * y[i * half:(i + 1) * half, :]


def _moe_plan(cnt):
    n_local = cnt.shape[0]
    chunks = (cnt + CHUNK - 1) // CHUNK
    used = jnp.sum(chunks, axis=1)
    local_off = jnp.cumsum(chunks, axis=1) - chunks
    tiles_g = (jnp.sum(chunks, axis=0) + TILE_CHUNKS - 1) // TILE_CHUNKS
    tile_end = jnp.cumsum(tiles_g)
    n_tiles = tile_end[-1]
    group_start = (tile_end - tiles_g) * TILE_CHUNKS
    seg_len = chunks.T.reshape(-1)
    seg_start = (group_start[:, None] + (jnp.cumsum(chunks, axis=0) - chunks).T).reshape(-1)
    seg_src = (jnp.arange(n_local)[None, :] * LOCAL_CHUNKS + local_off.T).reshape(-1)
    max_chunks = n_local * MOE_TILE // CHUNK + n_local * N_GROUPS + N_GROUPS * TILE_CHUNKS
    max_tiles = (max_chunks + TILE_CHUNKS - 1) // TILE_CHUNKS
    c = jnp.arange(max_tiles * TILE_CHUNKS)[:, None]
    within = c - seg_start[None, :]
    hit = jnp.logical_and(within >= 0, within < seg_len[None, :])
    valid = jnp.any(hit, axis=1)
    src = jnp.sum(jnp.where(hit, seg_src[None, :] + within, 0), axis=1)
    src = jnp.where(valid, src, LOCAL_CHUNKS - 1)
    slot_k = c[:, 0] % (2 * TILE_CHUNKS)
    dst = jnp.where(valid, src, n_local * LOCAL_CHUNKS + slot_k)
    j = jnp.minimum(jnp.arange(max_tiles), n_tiles - 1)
    grp = jnp.sum(j[:, None] >= tile_end[None, :], axis=1)
    gid = jnp.arange(N_GROUPS)
    later = jnp.where(jnp.logical_and(gid[None, :] > gid[:, None], tiles_g[None, :] > 0), gid[None, :], N_GROUPS)
    next_of = jnp.min(later, axis=1)
    next_of = jnp.where(next_of == N_GROUPS, gid, next_of)
    next_grp = jnp.sum(jnp.where(grp[:, None] == gid[None, :], next_of[None, :], 0), axis=1)
    i32 = lambda a: a.astype(jnp.int32)
    return i32(src), i32(dst), i32(grp), i32(next_grp), i32(n_tiles).reshape(1), i32(used)


def _moe(plan, rows_local, w1g, w3g, w2g):
    src, dst, grp, next_grp, n_tiles, used = plan
    D = w2g.shape[3]
    n_rows, cols = rows_local.shape
    n_chunks = n_rows // CHUNK
    hbm = pl.BlockSpec(memory_space=pl.ANY)
    group_weights = [w1g, w3g, w2g]
    y = pl.pallas_call(
        _moe_kernel,
        grid_spec=pltpu.PrefetchScalarGridSpec(
            num_scalar_prefetch=6,
            grid=(grp.shape[0],),
            in_specs=[hbm, hbm, hbm, hbm],
            out_specs=hbm,
            scratch_shapes=([pltpu.VMEM((2, TILE_CHUNKS, CHUNK, cols), BF16),
                             pltpu.VMEM((2, TILE_CHUNKS, CHUNK, D), BF16),
                             pltpu.VMEM((1, CHUNK, D), BF16)]
                            + [pltpu.VMEM(w.shape[1:], F32) for w in group_weights]
                            + [pltpu.VMEM(w.shape[1:], BF16) for w in group_weights]
                            + [pltpu.SemaphoreType.DMA((2,)), pltpu.SemaphoreType.DMA((2,)),
                               pltpu.SemaphoreType.DMA(()), pltpu.SemaphoreType.DMA(())])),
        out_shape=jax.ShapeDtypeStruct((n_chunks + 2 * TILE_CHUNKS, CHUNK, D), BF16),
        compiler_params=_params("arbitrary"),
        name="moe",
    )(src, dst, grp, next_grp, n_tiles, used, rows_local.reshape(n_chunks, CHUNK, cols), w1g, w3g, w2g)
    return y.reshape((n_chunks + 2 * TILE_CHUNKS) * CHUNK, D)


def _combine(x1, mod, lpos, y_local, *, B, T):
    N, D = x1.shape
    L = WINDOW
    subs = B * L // MOE_TILE
    return pl.pallas_call(
        _combine_kernel,
        grid=(T // L,),
        in_specs=[pl.BlockSpec((B * L, D), lambda i: (i, 0)),
                  pl.BlockSpec(mod.shape, lambda i: (0, 0, 0)),
                  pl.BlockSpec((subs, 8, MOE_TILE), lambda i: (i, 0, 0)),
                  pl.BlockSpec((subs * LOCAL_ROWS, D), lambda i: (i, 0))],
        out_specs=pl.BlockSpec((B, L, D), lambda i: (0, i, 0)),
        out_shape=jax.ShapeDtypeStruct((B, T, D), F32),
        compiler_params=_params("arbitrary"),
        name="moe_combine",
    )(x1, mod, lpos, y_local)


def _block_diag(n, blk, value, dtype):
    r = np.arange(n)[:, None] // blk
    c = np.arange(n)[None, :] // blk
    return jnp.asarray(np.where(r == c, value, 0.0), dtype)


def _gla_constants():
    L = GLA_CHUNK
    i = np.arange(L)[:, None]
    j = np.arange(L)[None, :]
    tri = j <= i
    bdtri = np.logical_and(j <= i, i // GLA_SUB == j // GLA_SUB)
    cum = np.block([[tri, tri], [bdtri, bdtri]])
    bdms = _block_diag(2 * GLA_DV, GLA_DV, 1.0 / GLA_DV, BF16)
    d = np.arange(LANES)[:, None] // GLA_DK
    e = np.arange(2 * GLA_DV)[None, :] // GLA_DV
    smask = d == e
    return jnp.asarray(cum, BF16), bdms, jnp.asarray(smask, F32)


def kernel(x, c, w_ada, b_ada, g_norm1, w_in, q_norm, k_norm, sinks, w_gk2, b_gk, g_gla_out, g_att_out,
           w_out, g_norm2, w_group, b_group, w_router, b_router, w1, w3, w2):
    B, T, D = x.shape
    N = B * T
    depth = w_ada.shape[0]
    cum, bdms, smask = _gla_constants()
    bdq = _block_diag(2 * LANES, HEAD_DIM, 1.0 / HEAD_DIM, BF16)
    bdk = _block_diag(ATT_KV, HEAD_DIM, 1.0 / HEAD_DIM, BF16)

    for l in range(depth):
        mod = _adaln_mod(c, w_ada[l], b_ada[l]).reshape(B, 6, D)

        wgk = jnp.concatenate([w_gk2[l], jnp.zeros((LANES - GLA_RANK, GLA_K), F32)], axis=0).astype(BF16)
        pad = LANES - N_GROUPS - N_EXPERTS
        wr_t = jnp.concatenate([w_group[l], w_router[l], jnp.zeros((D, pad), F32)], axis=1).T.astype(BF16)
        br_t = jnp.concatenate([b_group[l], b_router[l], jnp.zeros((pad,), F32)]).reshape(LANES, 1)
        striu = jnp.asarray(np.arange(MOE_TILE)[:, None] < np.arange(MOE_TILE)[None, :], BF16)
        x1, rows_local, lpos, cnt = _mix_out(
            sinks[l], x, mod, g_norm1[l].reshape(1, D), w_in[l].T,
            jnp.tile(q_norm[l], ATT_HEADS).reshape(1, ATT_Q), jnp.tile(k_norm[l], ATT_KV_HEADS).reshape(1, ATT_KV),
            bdq, bdk, wgk, b_gk[l].reshape(1, GLA_K),
            g_att_out[l].reshape(1, ATT_Q), cum, bdms, smask, jnp.tile(g_gla_out[l], GLA_HEADS).reshape(1, GLA_V),
            w_out[l].astype(BF16), g_norm2[l].reshape(1, D), wr_t, br_t, striu)
        plan = _moe_plan(cnt[:, :N_GROUPS, 0].astype(jnp.int32))
        by_group = lambda w: w.reshape((N_GROUPS, EXPERTS_PER_GROUP) + w.shape[1:])
        y_local = _moe(plan, rows_local, by_group(w1[l]), by_group(w3[l]), by_group(w2[l]))
        x = _combine(x1, mod, lpos, y_local, B=B, T=T)
    return x
```

```python
import jax
import jax.numpy as jnp
import numpy as np
from jax import lax
from jax.experimental import pallas as pl
from jax.experimental.pallas import tpu as pltpu

F32 = jnp.float32
BF16 = jnp.bfloat16

EPS = 1e-6
LOG2_E = 1.4426950408889634
ATT_HEADS = 8
ATT_KV_HEADS = 2
HEAD_DIM = 64
WINDOW = 128
ATT_Q = ATT_HEADS * HEAD_DIM
ATT_KV = ATT_KV_HEADS * HEAD_DIM
GLA_HEADS = 4
GLA_DK = 64
GLA_DV = 128
GLA_RANK = 16
GLA_NORMALIZER = 16.0
GLA_K = GLA_HEADS * GLA_DK
GLA_V = GLA_HEADS * GLA_DV
N_GROUPS = 4
EXPERTS_PER_GROUP = 4
N_EXPERTS = N_GROUPS * EXPERTS_PER_GROUP

LANES = 128
MOD_COLS = 768
PROJ_SPLITS = (512, 512)
PROJ_TILE = sum(PROJ_SPLITS)
ROUTER_ROWS = 24
GLA_CHUNK = 128
GLA_SUB = 16
N_SUB = GLA_CHUNK // GLA_SUB
ROUTER_LANE0 = N_GROUPS
VMEM_LIMIT = 62 * 1024 * 1024
MOE_TILE = 256
EXPERT_TILE = 512
CHUNK = 16
TILE_CHUNKS = EXPERT_TILE // CHUNK
LOCAL_CHUNKS = (MOE_TILE + N_GROUPS * (CHUNK - 1)) // CHUNK + 2
LOCAL_ROWS = LOCAL_CHUNKS * CHUNK

_QA0, _KA0, _VA0 = 0, ATT_Q, ATT_Q + ATT_KV
_QG0 = _VA0 + ATT_KV
_KG0 = _QG0 + GLA_K
_VG0 = _KG0 + GLA_K
_OG0 = _VG0 + GLA_V
_LR0 = _OG0 + GLA_V
IN_COLS_PAD = _LR0 + LANES


def _dot(a, b):
    return jnp.dot(a, b, preferred_element_type=F32)


def _dot_nt(a, b):
    return lax.dot_general(a, b, (((1,), (1,)), ((), ())), preferred_element_type=F32)


def _sigmoid(x):
    return 1.0 / (1.0 + jnp.exp(-x))


def _params(*sem):
    return pltpu.CompilerParams(dimension_semantics=sem, vmem_limit_bytes=VMEM_LIMIT)


def _mod_kernel(c_ref, w_ref, b_ref, o_ref):
    c = c_ref[...]
    s = (c * _sigmoid(c)).astype(BF16)
    o_ref[...] = _dot(s, w_ref[...].astype(BF16)) + b_ref[...]


def _adaln_mod(c, w_ada, b_ada):
    B, D = c.shape
    n = w_ada.shape[1]
    tn = MOD_COLS
    return pl.pallas_call(
        _mod_kernel,
        grid=(n // tn,),
        in_specs=[pl.BlockSpec((B, D), lambda j: (0, 0)),
                  pl.BlockSpec((D, tn), lambda j: (0, j)),
                  pl.BlockSpec((1, tn), lambda j: (0, j))],
        out_specs=pl.BlockSpec((B, tn), lambda j: (0, j)),
        out_shape=jax.ShapeDtypeStruct((B, n), F32),
        compiler_params=_params("arbitrary"),
        name="adaln_mod",
    )(c, w_ada, b_ada.reshape(1, n))


def _inproj_body(x_ref, mod_ref, g1_ref, wf_ref, qn_ref, kn_ref, bdq_ref, bdk_ref, wgk_ref, bgk_ref,
                 qa_ref, ka_ref, va_ref, qg_ref, kg_ref, vg_ref, la_ref, og_ref, w_ref):
    blk = x_ref.shape[1]
    subs = range(len(PROJ_SPLITS))
    seqs_of = [range(sum(PROJ_SPLITS[:t]) // blk, sum(PROJ_SPLITS[:t + 1]) // blk) for t in subs]

    def put(ref, t, value, lanes=slice(None)):
        for i, s in enumerate(seqs_of[t]):
            ref[s, :, lanes] = value[i * blk:(i + 1) * blk, :]

    @pl.when(pl.program_id(1) == 0)
    def _():
        lr_src = _OG0
        w_ref[0:_OG0, :] = wf_ref[0:_OG0, :].astype(BF16)
        w_ref[_OG0:_LR0, :] = wf_ref[lr_src + GLA_RANK:lr_src + GLA_RANK + GLA_V, :].astype(BF16)
        w_ref[_LR0:_LR0 + GLA_RANK, :] = wf_ref[lr_src:lr_src + GLA_RANK, :].astype(BF16)
        w_ref[_LR0 + GLA_RANK:IN_COLS_PAD, :] = jnp.zeros((LANES - GLA_RANK, wf_ref.shape[1]), BF16)

    h = []
    for t in subs:
        parts = []
        for s in seqs_of[t]:
            x = x_ref[s]
            ms = jnp.mean(x * x, axis=-1, keepdims=True)
            xn = x * lax.rsqrt(ms + EPS) * g1_ref[...]
            parts.append((xn * (1.0 + mod_ref[s, 1:2, :]) + mod_ref[s, 0:1, :]).astype(BF16))
        h.append(jnp.concatenate(parts, axis=0))

    proj = lambda t, c0, width: _dot_nt(h[t], w_ref[c0:c0 + width, :])
    gate_of = lambda lr: _dot(lr.astype(BF16), wgk_ref[...]) + bgk_ref[...]
    qa, kv, q_ms, k_ms, qk_g, vg, og, lr, gate = ({} for _ in range(9))
    for t in subs:
        qa[t] = proj(t, _QA0, ATT_Q)
        if t > 0:
            gate[t - 1] = gate_of(lr[t - 1])
        kv[t] = proj(t, _KA0, 2 * ATT_KV)
        q_sq = (qa[t] * qa[t]).astype(BF16)
        q_ms[t] = jnp.concatenate([_dot(q_sq[:, c:c + 2 * LANES], bdq_ref[...])
                                   for c in range(0, ATT_Q, 2 * LANES)], axis=1)
        qk_g[t] = proj(t, _QG0, 2 * GLA_K)
        k = kv[t][:, 0:ATT_KV]
        k_ms[t] = _dot((k * k).astype(BF16), bdk_ref[...])
        vg[t] = proj(t, _VG0, GLA_V)
        og[t] = proj(t, _OG0, GLA_V)
        lr[t] = proj(t, _LR0, LANES)
    gate[subs[-1]] = gate_of(lr[subs[-1]])

    for t in subs:
        low = lax.broadcasted_iota(jnp.int32, (PROJ_SPLITS[t], ATT_KV), 1) < HEAD_DIM
        put(qa_ref, t, (qa[t] * lax.rsqrt(q_ms[t] + EPS) * qn_ref[...] * (HEAD_DIM ** -0.5 * LOG2_E)).astype(BF16))
        k = kv[t][:, 0:ATT_KV] * lax.rsqrt(k_ms[t] + EPS) * kn_ref[...]
        v = kv[t][:, ATT_KV:2 * ATT_KV]
        for src, dst in ((k, ka_ref), (v, va_ref)):
            swapped = pltpu.roll(src, HEAD_DIM, axis=1)
            put(dst, t, jnp.where(low, src, swapped).astype(BF16), slice(0, LANES))
            put(dst, t, jnp.where(low, swapped, src).astype(BF16), slice(LANES, 2 * LANES))
        put(qg_ref, t, (qk_g[t][:, 0:GLA_K] * (GLA_DK ** -0.5)).astype(BF16))
        put(kg_ref, t, qk_g[t][:, GLA_K:2 * GLA_K].astype(BF16))
        put(vg_ref, t, vg[t].astype(BF16))
        put(og_ref, t, (og[t] * _sigmoid(og[t])).astype(BF16))
        log_sig = jnp.minimum(gate[t], 0.0) - jnp.log(1.0 + jnp.exp(-jnp.abs(gate[t])))
        put(la_ref, t, log_sig * (1.0 / GLA_NORMALIZER))


def _attn_stages(seqs, sinks_ref, q_ref, kc_ref, vc_ref, gatt_ref, o_ref, kp_ref, vp_ref):
    blk = WINDOW
    first = pl.program_id(1) == 0
    qi = lax.broadcasted_iota(jnp.int32, (blk, blk), 0)
    cj = lax.broadcasted_iota(jnp.int32, (blk, blk), 1)
    from_prev = cj > qi
    dist = (qi - cj + jnp.where(from_prev, blk, 0)).astype(F32)
    no_prev = jnp.where(jnp.logical_and(from_prev, first), -1e30, 0.0)
    low = cj < HEAD_DIM
    half = (jnp.where(low, 1.0, 0.0).astype(BF16), jnp.where(low, 0.0, 1.0).astype(BF16))
    half2 = tuple(jnp.concatenate([m, m], axis=0) for m in half)
    prev_mask = jnp.where(from_prev, 1.0, 0.0).astype(BF16)
    cur_mask = jnp.where(from_prev, 0.0, 1.0).astype(BF16)

    n_pairs = ATT_HEADS // 2
    pairs_per_kv = n_pairs // ATT_KV_HEADS
    units = [(bi, j) for bi in seqs for j in range(n_pairs)]

    def kv_blocks(bi, g):
        lanes = slice(g * LANES, (g + 1) * LANES)
        return (kp_ref[bi, :, lanes], vp_ref[bi, :, lanes]), (kc_ref[bi, :, lanes], vc_ref[bi, :, lanes])

    scores = {}
    for bi in seqs:
        for g in range(ATT_KV_HEADS):
            (kp, _), (kc, _) = kv_blocks(bi, g)
            k_both = jnp.concatenate([kp, kc], axis=0)
            group = range(g * pairs_per_kv, (g + 1) * pairs_per_kv)
            q_stack = jnp.concatenate([q_ref[bi, :, j * LANES:(j + 1) * LANES] for j in group], axis=0)
            for p in range(2):
                s_stack = _dot_nt(q_stack, k_both * half2[p])
                for jj, j in enumerate(group):
                    scores[bi, j, p] = s_stack[jj * blk:(jj + 1) * blk, :]
    yield

    bias = [2.0 ** (-8.0 * (h + 1) / ATT_HEADS) * LOG2_E * dist - no_prev for h in range(ATT_HEADS)]
    probs, sink_terms = {}, {}
    for bi, j in units:
        for p in range(2):
            h = 2 * j + p
            s_both = scores[bi, j, p]
            s = jnp.where(from_prev, s_both[:, 0:blk], s_both[:, blk:2 * blk]) - bias[h]
            sink = sinks_ref[h] * LOG2_E
            m = jnp.maximum(jnp.max(s, axis=-1, keepdims=True), sink)
            probs[bi, j, p] = jnp.exp2(s - m)
            sink_terms[bi, j, p] = jnp.exp2(sink - m)
    yield

    outs = {}
    for bi in seqs:
        for g in range(ATT_KV_HEADS):
            (_, vp), (_, vc) = kv_blocks(bi, g)
            v_stack = jnp.concatenate([jnp.concatenate([v * half[p], half[p]], axis=1)
                                       for v in (vp, vc) for p in range(2)], axis=0)
            group = range(g * pairs_per_kv, (g + 1) * pairs_per_kv)
            p_stack = []
            for j in group:
                e = [probs[bi, j, p].astype(BF16) for p in range(2)]
                p_stack.append(jnp.concatenate([x * m for m in (prev_mask, cur_mask) for x in e], axis=1))
            pv_stack = _dot(jnp.concatenate(p_stack, axis=0), v_stack)
            for jj, j in enumerate(group):
                pv = pv_stack[jj * blk:(jj + 1) * blk, :]
                den = pv[:, LANES:2 * LANES] + jnp.where(low, sink_terms[bi, j, 0], sink_terms[bi, j, 1])
                outs[bi, j] = pv[:, 0:LANES] / den
    yield

    for bi in seqs:
        o = jnp.concatenate([outs[bi, j] for j in range(n_pairs)], axis=1)
        ms = jnp.mean(o * o, axis=-1, keepdims=True)
        o_ref[bi] = (o * lax.rsqrt(ms + EPS) * gatt_ref[...]).astype(BF16)
        kp_ref[bi] = kc_ref[bi]
        vp_ref[bi] = vc_ref[bi]


def _gla_stages(seqs, q_ref, k_ref, v_ref, la_ref, og_ref, cum_ref, bdms_ref, smask_ref,
                ggla_ref, o_ref, state_ref):
    L = GLA_CHUNK
    lane = lax.broadcasted_iota(jnp.int32, (GLA_SUB, LANES), 1)
    head_mask = [jnp.where(lane // GLA_DK == hh, 1.0, 0.0).astype(BF16) for hh in range(2)]
    causal = lax.broadcasted_iota(jnp.int32, (L, L), 0) >= lax.broadcasted_iota(jnp.int32, (L, L), 1)
    units = [(s, pair) for s in seqs for pair in range(GLA_HEADS // 2)]
    kl = lambda pair: slice(pair * LANES, (pair + 1) * LANES)
    vl = lambda pair: slice(pair * 2 * GLA_DV, (pair + 1) * 2 * GLA_DV)

    b, b_in = {}, {}
    for s in seqs:
        la = la_ref[s]
        la_hi = la.astype(BF16)
        la_lo = (la - la_hi.astype(F32)).astype(BF16)
        sums = _dot(cum_ref[...], jnp.concatenate([la_hi, la_lo], axis=0)) * LOG2_E
        b[s] = sums[0:L, :]
        b_in[s] = sums[L:2 * L, :]
    yield

    q_both, keys, q_dec, k_dec_t, b_last = {}, {}, {}, {}, {}
    for s in seqs:
        ref = b[s] - b_in[s]
        b_last[s] = b[s][L - 1:L, :]
        q = q_ref[s].astype(F32)
        k = k_ref[s].astype(F32)
        q_in = (q * jnp.exp2(b_in[s])).astype(BF16)
        q_dec[s] = (q * jnp.exp2(b[s])).astype(BF16)
        k_dec = k * jnp.exp2(b_last[s] - b[s])
        for pair in range(GLA_HEADS // 2):
            k_p, b_p, ref_p = k[:, kl(pair)], b[s][:, kl(pair)], ref[:, kl(pair)]
            expanded = []
            for g in range(N_SUB):
                top = (g + 1) * GLA_SUB
                live = (k_p[0:top, :] * jnp.exp2(ref_p[g * GLA_SUB:g * GLA_SUB + 1, :] - b_p[0:top, :])).astype(BF16)
                expanded.append(live if top == L else
                                jnp.concatenate([live, jnp.zeros((L - top, LANES), BF16)], axis=0))
            keys[s, pair] = jnp.concatenate(expanded, axis=1)
            q_p = q_in[:, kl(pair)]
            zero_group = jnp.zeros((GLA_SUB, LANES), BF16)
            q_both[s, pair] = []
            for t in range(N_SUB // 2):
                lhs = []
                for hh in range(2):
                    for side in range(2):
                        g = 2 * t + side
                        piece = q_p[g * GLA_SUB:(g + 1) * GLA_SUB, :] * head_mask[hh]
                        lhs.append(jnp.concatenate([piece, zero_group] if side == 0 else [zero_group, piece], axis=1))
                q_both[s, pair].append(jnp.concatenate(lhs, axis=0))
            k_dec_t[s, pair] = k_dec[:, kl(pair)].T.astype(BF16)
    yield

    scores = {(u, t): _dot_nt(q_both[u][t], keys[u][:, 2 * t * LANES:2 * (t + 1) * LANES])
              for u in units for t in range(N_SUB // 2)}
    yield

    outs, updates = {}, {}
    for s, pair in units:
        v_p = v_ref[s, :, vl(pair)]
        o_parts = []
        for hh in range(2):
            a = jnp.concatenate([scores[(s, pair), t][2 * hh * GLA_SUB:2 * (hh + 1) * GLA_SUB, :]
                                 for t in range(N_SUB // 2)], axis=0)
            a = jnp.where(causal, a, 0.0).astype(BF16)
            o_parts.append(_dot(a, v_p[:, hh * GLA_DV:(hh + 1) * GLA_DV]))
        state = state_ref[s, pair]
        outs[s, pair] = jnp.concatenate(o_parts, axis=1) + _dot(q_dec[s][:, kl(pair)], state.astype(BF16))
        updates[s, pair] = _dot(k_dec_t[s, pair], v_p)
    yield

    for s, pair in units:
        decay = jnp.broadcast_to(jnp.exp2(b_last[s][:, kl(pair)]), (LANES, LANES)).T
        state_ref[s, pair] = (state_ref[s, pair] * jnp.concatenate([decay, decay], axis=1)
                              + updates[s, pair] * smask_ref[...])
        o = outs[s, pair]
        ms = _dot((o * o).astype(BF16), bdms_ref[...])
        y = o * lax.rsqrt(ms + EPS) * ggla_ref[:, vl(pair)] * og_ref[s, :, vl(pair)].astype(F32)
        o_ref[s, :, vl(pair)] = y.astype(BF16)


def _mixers_kernel(sinks_ref, qa_ref, ka_ref, va_ref, gatt_ref,
                   qg_ref, kg_ref, vg_ref, la_ref, og_ref, cum_ref, bdms_ref, smask_ref, ggla_ref,
                   ya_ref, yg_ref, kp_ref, vp_ref, state_ref):
    @pl.when(pl.program_id(1) == 0)
    def _():
        state_ref[...] = jnp.zeros_like(state_ref)
        kp_ref[...] = jnp.zeros_like(kp_ref)
        vp_ref[...] = jnp.zeros_like(vp_ref)

    n_seq = qa_ref.shape[0]
    waiting = []
    for group in (tuple(range(0, n_seq // 2)), tuple(range(n_seq // 2, n_seq))):
        waiting.append([
            _attn_stages(group, sinks_ref, qa_ref, ka_ref, va_ref, gatt_ref, ya_ref, kp_ref, vp_ref),
            _gla_stages(group, qg_ref, kg_ref, vg_ref, la_ref, og_ref, cum_ref, bdms_ref, smask_ref,
                        ggla_ref, yg_ref, state_ref)])
    pending = []
    while pending or waiting:
        if waiting:
            pending.extend(waiting.pop(0))
        for stages in list(pending):
            if next(stages, "done") == "done":
                pending.remove(stages)


def _route(logits_t):
    lt = logits_t[0:ROUTER_ROWS, :]
    row = lax.broadcasted_iota(jnp.int32, lt.shape, 0)
    neg_inf = -jnp.inf
    g_log = jnp.where(row < N_GROUPS, lt, neg_inf)
    g_max = jnp.max(g_log, axis=0, keepdims=True)
    g_sel = jnp.min(jnp.where(g_log == g_max, row, LANES), axis=0, keepdims=True)
    g_sum = jnp.sum(jnp.where(row < N_GROUPS, jnp.exp(lt - g_max), 0.0), axis=0, keepdims=True)
    p_group = 1.0 / g_sum
    e_lo = ROUTER_LANE0 + EXPERTS_PER_GROUP * g_sel
    in_group = jnp.logical_and(row >= e_lo, row < e_lo + EXPERTS_PER_GROUP)
    e_log = jnp.where(in_group, lt, neg_inf)
    e_max = jnp.max(e_log, axis=0, keepdims=True)
    top1 = jnp.min(jnp.where(e_log == e_max, row, LANES), axis=0, keepdims=True)
    e_log2 = jnp.where(row == top1, neg_inf, e_log)
    e_max2 = jnp.max(e_log2, axis=0, keepdims=True)
    top2 = jnp.min(jnp.where(e_log2 == e_max2, row, LANES), axis=0, keepdims=True)
    ratio = jnp.exp(e_max2 - e_max)
    w_top1 = p_group / (1.0 + ratio)
    w_top2 = p_group * ratio / (1.0 + ratio)
    row8 = lax.broadcasted_iota(jnp.int32, (8, lt.shape[1]), 0)
    weights = jnp.where(row8 == top1 - e_lo, w_top1, 0.0) + jnp.where(row8 == top2 - e_lo, w_top2, 0.0)
    return g_sel, weights


def _outproj_body(ya_ref, yg_ref, x_ref, mod_ref, wo_ref, g2_ref, wrt_ref, brt_ref, striu_ref,
                  x1_ref, row_ref, lpos_ref, cnt_ref):
    tm = MOE_TILE
    half = x_ref.shape[1]
    per_tile = tm // half
    subs = range(x_ref.shape[0] // per_tile)
    rows = lambda t: slice(t * tm, (t + 1) * tm)
    tile_of = lambda ref, t: jnp.concatenate([ref[t * per_tile + i] for i in range(per_tile)], axis=0)

    mix = [_dot(tile_of(ya_ref, t), wo_ref[0:ATT_Q, :]) + _dot(tile_of(yg_ref, t), wo_ref[ATT_Q:ATT_Q + GLA_V, :])
           for t in subs]
    h2b = []
    for t in subs:
        parts = []
        for i in range(per_tile):
            s = t * per_tile + i
            x1 = x_ref[s] + mod_ref[s, 2:3, :] * mix[t][i * half:(i + 1) * half, :]
            x1_ref[t * tm + i * half:t * tm + (i + 1) * half, :] = x1
            ms = jnp.mean(x1 * x1, axis=-1, keepdims=True)
            h2 = (x1 * lax.rsqrt(ms + EPS) * g2_ref[...]) * (1.0 + mod_ref[s, 4:5, :]) + mod_ref[s, 3:4, :]
            parts.append(h2.astype(BF16))
        h2b.append(jnp.concatenate(parts, axis=0))
    logits_t = [_dot_nt(wrt_ref[...], h2b[t]) + brt_ref[...] for t in subs]

    routed = [_route(logits_t[t]) for t in subs]
    row8 = lax.broadcasted_iota(jnp.int32, (8, tm), 0)
    onehot = [jnp.where(row8 == routed[t][0], 1.0, 0.0) for t in subs]
    before = [_dot(onehot[t].astype(BF16), striu_ref[...]) for t in subs]

    local_row = lax.broadcasted_iota(jnp.int32, (LOCAL_ROWS, tm), 0).astype(F32)
    pad_rows = jnp.zeros((LANES - 8, tm), F32)
    for t in subs:
        count = jnp.sum(onehot[t], axis=1, keepdims=True)
        cnt_ref[t] = jnp.broadcast_to(count, (8, LANES))
        padded = jnp.broadcast_to(jnp.floor((count + (CHUNK - 1.0)) * (1.0 / CHUNK)) * CHUNK, (8, tm))
        start = jnp.zeros((8, tm), F32)
        for shift in range(1, N_GROUPS):
            start = start + jnp.where(row8 >= shift, pltpu.roll(padded, shift, axis=0), 0.0)
        lpos = jnp.sum(onehot[t] * (before[t] + start), axis=0, keepdims=True)
        lpos_ref[t] = jnp.broadcast_to(lpos, (8, tm))
        weights = jnp.concatenate([routed[t][1], pad_rows], axis=0).T
        w_hi = weights.astype(BF16)
        w_lo = (weights - w_hi.astype(F32)).astype(BF16)
        perm = jnp.where(local_row == lpos, 1.0, 0.0).astype(BF16)
        row_ref[t * LOCAL_ROWS:(t + 1) * LOCAL_ROWS, :] = _dot(
            perm, jnp.concatenate([h2b[t], w_hi, w_lo], axis=1)).astype(BF16)


def _mix_out_kernel(sinks_ref, x_ref, mod_ref, g1_ref, wf_ref, qn_ref, kn_ref, bdq_ref, bdk_ref, wgk_ref, bgk_ref,
                    gatt_ref, cum_ref, bdms_ref, smask_ref, ggla_ref,
                    wo_ref, g2_ref, wrt_ref, brt_ref, striu_ref,
                    x1_ref, row_ref, lpos_ref, cnt_ref,
                    w_ref, qa_ref, ka_ref, va_ref, qg_ref, kg_ref, vg_ref, la_ref, og_ref,
                    ya_ref, yg_ref, kp_ref, vp_ref, state_ref):
    _inproj_body(x_ref, mod_ref, g1_ref, wf_ref, qn_ref, kn_ref, bdq_ref, bdk_ref, wgk_ref, bgk_ref,
                 qa_ref, ka_ref, va_ref, qg_ref, kg_ref, vg_ref, la_ref, og_ref, w_ref)
    _mixers_kernel(sinks_ref, qa_ref, ka_ref, va_ref, gatt_ref,
                   qg_ref, kg_ref, vg_ref, la_ref, og_ref, cum_ref, bdms_ref, smask_ref, ggla_ref,
                   ya_ref, yg_ref, kp_ref, vp_ref, state_ref)
    _outproj_body(ya_ref, yg_ref, x_ref, mod_ref, wo_ref, g2_ref, wrt_ref, brt_ref, striu_ref,
                  x1_ref, row_ref, lpos_ref, cnt_ref)


def _mix_out(sinks, x, mod, g1, w_in_t, qn, kn, bdq, bdk, wgk, bgk, gatt, cum, bdms, smask, ggla,
             wo, g2, wr, br, striu):
    assert GLA_CHUNK == WINDOW and MOE_TILE % WINDOW == 0
    B, T, D = x.shape
    L = WINDOW
    N = B * T
    assert B * L == PROJ_TILE
    tiles_per_step = B * L // MOE_TILE
    const = lambda a: pl.BlockSpec(a.shape, lambda b, i: (0,) * a.ndim)
    tiles = lambda *shape: pl.BlockSpec((tiles_per_step,) + shape, lambda b, i: (i,) + (0,) * len(shape))
    act = lambda c, dt=BF16: pltpu.VMEM((B, L, c), dt)
    return pl.pallas_call(
        _mix_out_kernel,
        grid=(1, T // L),
        in_specs=[pl.BlockSpec(memory_space=pltpu.SMEM),
                  pl.BlockSpec((B, L, D), lambda b, i: (0, i, 0)), const(mod), const(g1),
                  pl.BlockSpec(w_in_t.shape, lambda b, i: (0, 0), pipeline_mode=pl.Buffered(1)),
                  const(qn), const(kn), const(bdq), const(bdk), const(wgk), const(bgk),
                  const(gatt), const(cum), const(bdms), const(smask), const(ggla),
                  const(wo), const(g2), const(wr), const(br), const(striu)],
        out_specs=[pl.BlockSpec((B * L, D), lambda b, i: (i, 0)),
                   pl.BlockSpec((tiles_per_step * LOCAL_ROWS, D + 2 * LANES), lambda b, i: (i, 0)),
                   tiles(8, MOE_TILE), tiles(8, LANES)],
        out_shape=[jax.ShapeDtypeStruct((N, D), F32),
                   jax.ShapeDtypeStruct((N // MOE_TILE * LOCAL_ROWS, D + 2 * LANES), BF16),
                   jax.ShapeDtypeStruct((N // MOE_TILE, 8, MOE_TILE), F32),
                   jax.ShapeDtypeStruct((N // MOE_TILE, 8, LANES), F32)],
        scratch_shapes=[pltpu.VMEM((IN_COLS_PAD, D), BF16),
                        act(ATT_Q), act(2 * ATT_KV), act(2 * ATT_KV), act(GLA_K), act(GLA_K), act(GLA_V),
                        act(GLA_K, F32), act(GLA_V),
                        act(ATT_Q), act(GLA_V), act(2 * ATT_KV), act(2 * ATT_KV),
                        pltpu.VMEM((B, GLA_HEADS // 2, LANES, 2 * GLA_DV), F32)],
        compiler_params=_params("arbitrary", "arbitrary"),
        name="mix_out",
    )(sinks, x, mod, g1, w_in_t, qn, kn, bdq, bdk, wgk, bgk, gatt, cum, bdms, smask, ggla,
      wo, g2, wr, br, striu)


def _chunk_copy(src_ref, src_chunk, dst_ref, dst_chunk, sem):
    return pltpu.make_async_copy(src_ref.at[src_chunk], dst_ref.at[dst_chunk], sem)


def _moe_kernel(src_ref, dst_ref, grp_ref, next_ref, nt_ref, used_ref, rows_ref, w1f_ref, w3f_ref, w2f_ref, y_ref,
                in_buf, out_buf, zero_buf, st1_ref, st3_ref, st2_ref, w1_ref, w3_ref, w2_ref,
                in_sem, out_sem, zero_sem, w_sem):
    j = pl.program_id(0)
    n_tiles = nt_ref[0]
    d_model = w2_ref.shape[2]

    stages = ((w1f_ref, st1_ref, w1_ref), (w3f_ref, st3_ref, w3_ref), (w2f_ref, st2_ref, w2_ref))

    def fetch_weights(group):
        for hbm, stage, _ in stages:
            pltpu.make_async_copy(hbm.at[group], stage, w_sem).start()

    def enter_group():
        for hbm, stage, dst in stages:
            pltpu.make_async_copy(hbm.at[0], stage, w_sem).wait()
        for hbm, stage, dst in stages:
            for k in range(EXPERTS_PER_GROUP):
                dst[k] = stage[k].astype(BF16)

        @pl.when(next_ref[j] != grp_ref[j])
        def _():
            fetch_weights(next_ref[j])

    def gather(tile, slot):
        def body(k, carry):
            _chunk_copy(rows_ref, src_ref[tile * TILE_CHUNKS + k], in_buf.at[slot], k, in_sem.at[slot]).start()
            return carry
        lax.fori_loop(0, TILE_CHUNKS, body, 0, unroll=True)

    def wait_gather(slot):
        def body(k, carry):
            _chunk_copy(rows_ref, 0, in_buf.at[slot], k, in_sem.at[slot]).wait()
            return carry
        lax.fori_loop(0, TILE_CHUNKS, body, 0, unroll=True)

    def scatter(tile, slot):
        def body(k, carry):
            _chunk_copy(out_buf.at[slot], k, y_ref, dst_ref[tile * TILE_CHUNKS + k], out_sem.at[slot]).start()
            return carry
        lax.fori_loop(0, TILE_CHUNKS, body, 0, unroll=True)

    def wait_scatter(slot):
        def body(k, carry):
            _chunk_copy(out_buf.at[slot], k, y_ref, 0, out_sem.at[slot]).wait()
            return carry
        lax.fori_loop(0, TILE_CHUNKS, body, 0, unroll=True)

    def zero_fill(wait):
        def per_tile(i, carry):
            def body(c, inner):
                copy = _chunk_copy(zero_buf, 0, y_ref, i * LOCAL_CHUNKS + c, zero_sem)
                if wait:
                    copy.wait()
                else:
                    copy.start()
                return inner
            return lax.fori_loop(used_ref[i], LOCAL_CHUNKS, body, carry)
        lax.fori_loop(0, used_ref.shape[0], per_tile, 0)

    @pl.when(j == 0)
    def _():
        fetch_weights(grp_ref[0])
        zero_buf[...] = jnp.zeros_like(zero_buf)
        scratch0 = used_ref.shape[0] * LOCAL_CHUNKS
        for wait in (False, True):
            for k in range(2 * TILE_CHUNKS):
                copy = _chunk_copy(zero_buf, 0, y_ref, scratch0 + k, zero_sem)
                copy.wait() if wait else copy.start()
        zero_fill(wait=False)
        gather(0, 0)

    @pl.when(jnp.logical_or(j == 0, grp_ref[j] != grp_ref[jnp.maximum(j - 1, 0)]))
    def _():
        enter_group()

    @pl.when(j + 1 < n_tiles)
    def _():
        gather(j + 1, (j + 1) % 2)

    @pl.when(j < n_tiles)
    def _():
        slot = j % 2
        wait_gather(slot)
        rows = in_buf[slot].reshape(EXPERT_TILE, in_buf.shape[-1])
        h = rows[:, 0:d_model]
        weights = rows[:, d_model:d_model + LANES].astype(F32) + rows[:, d_model + LANES:].astype(F32)
        experts = range(EXPERTS_PER_GROUP)
        up = [(_dot(h, w1_ref[k]), _dot(h, w3_ref[k])) for k in experts]
        hid = [(a * _sigmoid(a) * g * weights[:, k:k + 1]).astype(BF16) for k, (a, g) in zip(experts, up)]
        y = _dot(hid[0], w2_ref[0])
        for k in experts[1:]:
            y = y + _dot(hid[k], w2_ref[k])

        @pl.when(j >= 2)
        def _():
            wait_scatter(slot)

        out_buf[slot] = y.astype(BF16).reshape(TILE_CHUNKS, CHUNK, d_model)
        scatter(j, slot)

        @pl.when(j == n_tiles - 1)
        def _():
            @pl.when(j >= 1)
            def _():
                wait_scatter(1 - slot)
            wait_scatter(slot)
            zero_fill(wait=True)


def _combine_kernel(x1_ref, mod_ref, lpos_ref, y_ref, o_ref):
    tm = MOE_TILE
    half = o_ref.shape[1]
    per_tile = tm // half
    local_row = lax.broadcasted_iota(jnp.int32, (tm, LOCAL_ROWS), 1).astype(F32)
    for t in range(x1_ref.shape[0] // tm):
        lpos = jnp.broadcast_to(lpos_ref[t][0:1, :], (LANES, tm)).T[:, 0:1]
        unsort = jnp.where(local_row == lpos, 1.0, 0.0).astype(BF16)
        y = _dot(unsort, y_ref[t * LOCAL_ROWS:(t + 1) * LOCAL_ROWS, :])
        for i in range(per_tile):
            s = t * per_tile + i
            rows = slice(t * tm + i * half, t * tm + (i + 1) * half)
            o_ref[s] = x1_ref[rows, :] + mod_ref[s, 5:6, :] * y[i * half:(i + 1) * half, :]


def _moe_plan(cnt):
    n_local = cnt.shape[0]
    chunks = (cnt + CHUNK - 1) // CHUNK
    used = jnp.sum(chunks, axis=1)
    local_off = jnp.cumsum(chunks, axis=1) - chunks
    tiles_g = (jnp.sum(chunks, axis=0) + TILE_CHUNKS - 1) // TILE_CHUNKS
    tile_end = jnp.cumsum(tiles_g)
    n_tiles = tile_end[-1]
    group_start = (tile_end - tiles_g) * TILE_CHUNKS
    seg_len = chunks.T.reshape(-1)
    seg_start = (group_start[:, None] + (jnp.cumsum(chunks, axis=0) - chunks).T).reshape(-1)
    seg_src = (jnp.arange(n_local)[None, :] * LOCAL_CHUNKS + local_off.T).reshape(-1)
    max_chunks = n_local * MOE_TILE // CHUNK + n_local * N_GROUPS + N_GROUPS * TILE_CHUNKS
    max_tiles = (max_chunks + TILE_CHUNKS - 1) // TILE_CHUNKS
    c = jnp.arange(max_tiles * TILE_CHUNKS)[:, None]
    within = c - seg_start[None, :]
    hit = jnp.logical_and(within >= 0, within < seg_len[None, :])
    valid = jnp.any(hit, axis=1)
    src = jnp.sum(jnp.where(hit, seg_src[None, :] + within, 0), axis=1)
    src = jnp.where(valid, src, LOCAL_CHUNKS - 1)
    slot_k = c[:, 0] % (2 * TILE_CHUNKS)
    dst = jnp.where(valid, src, n_local * LOCAL_CHUNKS + slot_k)
    j = jnp.minimum(jnp.arange(max_tiles), n_tiles - 1)
    grp = jnp.sum(j[:, None] >= tile_end[None, :], axis=1)
    gid = jnp.arange(N_GROUPS)
    later = jnp.where(jnp.logical_and(gid[None, :] > gid[:, None], tiles_g[None, :] > 0), gid[None, :], N_GROUPS)
    next_of = jnp.min(later, axis=1)
    next_of = jnp.where(next_of == N_GROUPS, gid, next_of)
    next_grp = jnp.sum(jnp.where(grp[:, None] == gid[None, :], next_of[None, :], 0), axis=1)
    i32 = lambda a: a.astype(jnp.int32)
    return i32(src), i32(dst), i32(grp), i32(next_grp), i32(n_tiles).reshape(1), i32(used)


def _moe(plan, rows_local, w1g, w3g, w2g):
    src, dst, grp, next_grp, n_tiles, used = plan
    D = w2g.shape[3]
    n_rows, cols = rows_local.shape
    n_chunks = n_rows // CHUNK
    hbm = pl.BlockSpec(memory_space=pl.ANY)
    group_weights = [w1g, w3g, w2g]
    y = pl.pallas_call(
        _moe_kernel,
        grid_spec=pltpu.PrefetchScalarGridSpec(
            num_scalar_prefetch=6,
            grid=(grp.shape[0],),
            in_specs=[hbm, hbm, hbm, hbm],
            out_specs=hbm,
            scratch_shapes=([pltpu.VMEM((2, TILE_CHUNKS, CHUNK, cols), BF16),
                             pltpu.VMEM((2, TILE_CHUNKS, CHUNK, D), BF16),
                             pltpu.VMEM((1, CHUNK, D), BF16)]
                            + [pltpu.VMEM(w.shape[1:], F32) for w in group_weights]
                            + [pltpu.VMEM(w.shape[1:], BF16) for w in group_weights]
                            + [pltpu.SemaphoreType.DMA((2,)), pltpu.SemaphoreType.DMA((2,)),
                               pltpu.SemaphoreType.DMA(()), pltpu.SemaphoreType.DMA(())])),
        out_shape=jax.ShapeDtypeStruct((n_chunks + 2 * TILE_CHUNKS, CHUNK, D), BF16),
        compiler_params=_params("arbitrary"),
        name="moe",
    )(src, dst, grp, next_grp, n_tiles, used, rows_local.reshape(n_chunks, CHUNK, cols), w1g, w3g, w2g)
    return y.reshape((n_chunks + 2 * TILE_CHUNKS) * CHUNK, D)


def _combine(x1, mod, lpos, y_local, *, B, T):
    N, D = x1.shape
    L = WINDOW
    subs = B * L // MOE_TILE
    return pl.pallas_call(
        _combine_kernel,
        grid=(T // L,),
        in_specs=[pl.BlockSpec((B * L, D), lambda i: (i, 0)),
                  pl.BlockSpec(mod.shape, lambda i: (0, 0, 0)),
                  pl.BlockSpec((subs, 8, MOE_TILE), lambda i: (i, 0, 0)),
                  pl.BlockSpec((subs * LOCAL_ROWS, D), lambda i: (i, 0))],
        out_specs=pl.BlockSpec((B, L, D), lambda i: (0, i, 0)),
        out_shape=jax.ShapeDtypeStruct((B, T, D), F32),
        compiler_params=_params("arbitrary"),
        name="moe_combine",
    )(x1, mod, lpos, y_local)


def _block_diag(n, blk, value, dtype):
    r = np.arange(n)[:, None] // blk
    c = np.arange(n)[None, :] // blk
    return jnp.asarray(np.where(r == c, value, 0.0), dtype)


def _gla_constants():
    L = GLA_CHUNK
    i = np.arange(L)[:, None]
    j = np.arange(L)[None, :]
    tri = j <= i
    bdtri = np.logical_and(j <= i, i // GLA_SUB == j // GLA_SUB)
    cum = np.block([[tri, tri], [bdtri, bdtri]])
    bdms = _block_diag(2 * GLA_DV, GLA_DV, 1.0 / GLA_DV, BF16)
    d = np.arange(LANES)[:, None] // GLA_DK
    e = np.arange(2 * GLA_DV)[None, :] // GLA_DV
    smask = d == e
    return jnp.asarray(cum, BF16), bdms, jnp.asarray(smask, F32)


def kernel(x, c, w_ada, b_ada, g_norm1, w_in, q_norm, k_norm, sinks, w_gk2, b_gk, g_gla_out, g_att_out,
           w_out, g_norm2, w_group, b_group, w_router, b_router, w1, w3, w2):
    B, T, D = x.shape
    N = B * T
    depth = w_ada.shape[0]
    cum, bdms, smask = _gla_constants()
    bdq = _block_diag(2 * LANES, HEAD_DIM, 1.0 / HEAD_DIM, BF16)
    bdk = _block_diag(ATT_KV, HEAD_DIM, 1.0 / HEAD_DIM, BF16)

    for l in range(depth):
        mod = _adaln_mod(c, w_ada[l], b_ada[l]).reshape(B, 6, D)

        wgk = jnp.concatenate([w_gk2[l], jnp.zeros((LANES - GLA_RANK, GLA_K), F32)], axis=0).astype(BF16)
        pad = LANES - N_GROUPS - N_EXPERTS
        wr_t = jnp.concatenate([w_group[l], w_router[l], jnp.zeros((D, pad), F32)], axis=1).T.astype(BF16)
        br_t = jnp.concatenate([b_group[l], b_router[l], jnp.zeros((pad,), F32)]).reshape(LANES, 1)
        striu = jnp.asarray(np.arange(MOE_TILE)[:, None] < np.arange(MOE_TILE)[None, :], BF16)
        x1, rows_local, lpos, cnt = _mix_out(
            sinks[l], x, mod, g_norm1[l].reshape(1, D), w_in[l].T,
            jnp.tile(q_norm[l], ATT_HEADS).reshape(1, ATT_Q), jnp.tile(k_norm[l], ATT_KV_HEADS).reshape(1, ATT_KV),
            bdq, bdk, wgk, b_gk[l].reshape(1, GLA_K),
            g_att_out[l].reshape(1, ATT_Q), cum, bdms, smask, jnp.tile(g_gla_out[l], GLA_HEADS).reshape(1, GLA_V),
            w_out[l].astype(BF16), g_norm2[l].reshape(1, D), wr_t, br_t, striu)
        plan = _moe_plan(cnt[:, :N_GROUPS, 0].astype(jnp.int32))
        by_group = lambda w: w.reshape((N_GROUPS, EXPERTS_PER_GROUP) + w.shape[1:])
        y_local = _moe(plan, rows_local, by_group(w1[l]), by_group(w3[l]), by_group(w2[l]))
        x = _combine(x1, mod, lpos, y_local, B=B, T=T)
    return x
```

```python
import jax
import jax.numpy as jnp
import numpy as np
from jax import lax
from jax.experimental import pallas as pl
from jax.experimental.pallas import tpu as pltpu

F32 = jnp.float32
BF16 = jnp.bfloat16

EPS = 1e-6
LOG2_E = 1.4426950408889634
ATT_HEADS = 8
ATT_KV_HEADS = 2
HEAD_DIM = 64
WINDOW = 128
ATT_Q = ATT_HEADS * HEAD_DIM
ATT_KV = ATT_KV_HEADS * HEAD_DIM
GLA_HEADS = 4
GLA_DK = 64
GLA_DV = 128
GLA_RANK = 16
GLA_NORMALIZER = 16.0
GLA_K = GLA_HEADS * GLA_DK
GLA_V = GLA_HEADS * GLA_DV
N_GROUPS = 4
EXPERTS_PER_GROUP = 4
N_EXPERTS = N_GROUPS * EXPERTS_PER_GROUP

LANES = 128
MOD_COLS = 768
PROJ_SPLITS = (512, 512)
PROJ_TILE = sum(PROJ_SPLITS)
COMBINE_BLOCKS = 2
ROUTER_ROWS = 24
GLA_CHUNK = 128
GLA_SUB = 16
N_SUB = GLA_CHUNK // GLA_SUB
ROUTER_LANE0 = N_GROUPS
VMEM_LIMIT = 62 * 1024 * 1024
MOE_TILE = 256
EXPERT_TILE = 512
CHUNK = 16
TILE_CHUNKS = EXPERT_TILE // CHUNK
LOCAL_CHUNKS = (MOE_TILE + N_GROUPS * (CHUNK - 1)) // CHUNK + 2
LOCAL_ROWS = LOCAL_CHUNKS * CHUNK

_QA0, _KA0, _VA0 = 0, ATT_Q, ATT_Q + ATT_KV
_QG0 = _VA0 + ATT_KV
_KG0 = _QG0 + GLA_K
_VG0 = _KG0 + GLA_K
_OG0 = _VG0 + GLA_V
_LR0 = _OG0 + GLA_V
IN_COLS_PAD = _LR0 + LANES


def _dot(a, b):
    return jnp.dot(a, b, preferred_element_type=F32)


def _dot_nt(a, b):
    return lax.dot_general(a, b, (((1,), (1,)), ((), ())), preferred_element_type=F32)


def _sigmoid(x):
    return 1.0 / (1.0 + jnp.exp(-x))


def _params(*sem):
    return pltpu.CompilerParams(dimension_semantics=sem, vmem_limit_bytes=VMEM_LIMIT)


def _mod_kernel(c_ref, w_ref, b_ref, o_ref):
    c = c_ref[...]
    s = (c * _sigmoid(c)).astype(BF16)
    o_ref[...] = _dot(s, w_ref[...].astype(BF16)) + b_ref[...]


def _adaln_mod(c, w_ada, b_ada):
    B, D = c.shape
    n = w_ada.shape[1]
    tn = MOD_COLS
    return pl.pallas_call(
        _mod_kernel,
        grid=(n // tn,),
        in_specs=[pl.BlockSpec((B, D), lambda j: (0, 0)),
                  pl.BlockSpec((D, tn), lambda j: (0, j)),
                  pl.BlockSpec((1, tn), lambda j: (0, j))],
        out_specs=pl.BlockSpec((B, tn), lambda j: (0, j)),
        out_shape=jax.ShapeDtypeStruct((B, n), F32),
        compiler_params=_params("arbitrary"),
        name="adaln_mod",
    )(c, w_ada, b_ada.reshape(1, n))


def _round_w_in(wf_ref, w_ref, wof_ref, wo_ref, stage_ref, sem):
    @pl.when(pl.program_id(1) == 0)
    def _():
        fetch = pltpu.make_async_copy(wof_ref, stage_ref, sem)
        fetch.start()
        lr_src = _OG0
        w_ref[0:_OG0, :] = wf_ref[0:_OG0, :].astype(BF16)
        w_ref[_OG0:_LR0, :] = wf_ref[lr_src + GLA_RANK:lr_src + GLA_RANK + GLA_V, :].astype(BF16)
        w_ref[_LR0:_LR0 + GLA_RANK, :] = wf_ref[lr_src:lr_src + GLA_RANK, :].astype(BF16)
        w_ref[_LR0 + GLA_RANK:IN_COLS_PAD, :] = jnp.zeros((LANES - GLA_RANK, wf_ref.shape[1]), BF16)
        fetch.wait()
        wo_ref[...] = stage_ref[...].astype(BF16)


def _inproj_stages(subs, x_ref, mod_ref, g1_ref, qn_ref, kn_ref, bdq_ref, bdk_ref, wgk_ref, bgk_ref,
                   qa_ref, ka_ref, va_ref, qg_ref, kg_ref, vg_ref, la_ref, og_ref, w_ref):
    blk = x_ref.shape[1]
    seqs_of = [range(sum(PROJ_SPLITS[:t]) // blk, sum(PROJ_SPLITS[:t + 1]) // blk)
               for t in range(len(PROJ_SPLITS))]

    def put(ref, t, value, lanes=slice(None)):
        for i, s in enumerate(seqs_of[t]):
            ref[s, :, lanes] = value[i * blk:(i + 1) * blk, :]

    h = {}
    for t in subs:
        parts = []
        for s in seqs_of[t]:
            x = x_ref[s]
            ms = jnp.mean(x * x, axis=-1, keepdims=True)
            xn = x * lax.rsqrt(ms + EPS) * g1_ref[...]
            parts.append((xn * (1.0 + mod_ref[s, 1:2, :]) + mod_ref[s, 0:1, :]).astype(BF16))
        h[t] = jnp.concatenate(parts, axis=0)
    yield

    proj = lambda t, c0, width: _dot_nt(h[t], w_ref[c0:c0 + width, :])
    gate_of = lambda lr: _dot(lr.astype(BF16), wgk_ref[...]) + bgk_ref[...]
    qa, kv, q_ms, k_ms, qk_g, vg, og, lr, gate = ({} for _ in range(9))
    prev = None
    for t in subs:
        qa[t] = proj(t, _QA0, ATT_Q)
        if prev is not None:
            gate[prev] = gate_of(lr[prev])
        prev = t
        yield
        kv[t] = proj(t, _KA0, 2 * ATT_KV)
        q_sq = (qa[t] * qa[t]).astype(BF16)
        q_ms[t] = jnp.concatenate([_dot(q_sq[:, c:c + 2 * LANES], bdq_ref[...])
                                   for c in range(0, ATT_Q, 2 * LANES)], axis=1)
        yield
        qk_g[t] = proj(t, _QG0, 2 * GLA_K)
        k = kv[t][:, 0:ATT_KV]
        k_ms[t] = _dot((k * k).astype(BF16), bdk_ref[...])
        yield
        vg[t] = proj(t, _VG0, GLA_V)
        yield
        og[t] = proj(t, _OG0, GLA_V)
        lr[t] = proj(t, _LR0, LANES)
        yield
    gate[prev] = gate_of(lr[prev])
    yield

    for t in subs:
        low = lax.broadcasted_iota(jnp.int32, (PROJ_SPLITS[t], ATT_KV), 1) < HEAD_DIM
        put(qa_ref, t, (qa[t] * lax.rsqrt(q_ms[t] + EPS) * qn_ref[...] * (HEAD_DIM ** -0.5 * LOG2_E)).astype(BF16))
        k = kv[t][:, 0:ATT_KV] * lax.rsqrt(k_ms[t] + EPS) * kn_ref[...]
        v = kv[t][:, ATT_KV:2 * ATT_KV]
        for src, dst in ((k, ka_ref), (v, va_ref)):
            swapped = pltpu.roll(src, HEAD_DIM, axis=1)
            put(dst, t, jnp.where(low, src, swapped).astype(BF16), slice(0, LANES))
            put(dst, t, jnp.where(low, swapped, src).astype(BF16), slice(LANES, 2 * LANES))
        put(qg_ref, t, (qk_g[t][:, 0:GLA_K] * (GLA_DK ** -0.5)).astype(BF16))
        put(kg_ref, t, qk_g[t][:, GLA_K:2 * GLA_K].astype(BF16))
        put(vg_ref, t, vg[t].astype(BF16))
        put(og_ref, t, (og[t] * _sigmoid(og[t])).astype(BF16))
        log_sig = jnp.minimum(gate[t], 0.0) - jnp.log(1.0 + jnp.exp(-jnp.abs(gate[t])))
        put(la_ref, t, log_sig * (1.0 / GLA_NORMALIZER))


def _attn_stages(seqs, sinks_ref, q_ref, kc_ref, vc_ref, gatt_ref, o_ref, kp_ref, vp_ref):
    blk = WINDOW
    first = pl.program_id(1) == 0
    qi = lax.broadcasted_iota(jnp.int32, (blk, blk), 0)
    cj = lax.broadcasted_iota(jnp.int32, (blk, blk), 1)
    from_prev = cj > qi
    dist = (qi - cj + jnp.where(from_prev, blk, 0)).astype(F32)
    no_prev = jnp.where(jnp.logical_and(from_prev, first), -1e30, 0.0)
    low = cj < HEAD_DIM
    half = (jnp.where(low, 1.0, 0.0).astype(BF16), jnp.where(low, 0.0, 1.0).astype(BF16))
    half2 = tuple(jnp.concatenate([m, m], axis=0) for m in half)
    prev_mask = jnp.where(from_prev, 1.0, 0.0).astype(BF16)
    cur_mask = jnp.where(from_prev, 0.0, 1.0).astype(BF16)

    n_pairs = ATT_HEADS // 2
    pairs_per_kv = n_pairs // ATT_KV_HEADS
    units = [(bi, j) for bi in seqs for j in range(n_pairs)]

    def kv_blocks(bi, g):
        lanes = slice(g * LANES, (g + 1) * LANES)
        return (kp_ref[bi, :, lanes], vp_ref[bi, :, lanes]), (kc_ref[bi, :, lanes], vc_ref[bi, :, lanes])

    scores = {}
    for bi in seqs:
        for g in range(ATT_KV_HEADS):
            (kp, _), (kc, _) = kv_blocks(bi, g)
            k_both = jnp.concatenate([kp, kc], axis=0)
            group = range(g * pairs_per_kv, (g + 1) * pairs_per_kv)
            q_stack = jnp.concatenate([q_ref[bi, :, j * LANES:(j + 1) * LANES] for j in group], axis=0)
            for p in range(2):
                s_stack = _dot_nt(q_stack, k_both * half2[p])
                for jj, j in enumerate(group):
                    scores[bi, j, p] = s_stack[jj * blk:(jj + 1) * blk, :]
    yield

    bias = [2.0 ** (-8.0 * (h + 1) / ATT_HEADS) * LOG2_E * dist - no_prev for h in range(ATT_HEADS)]
    probs, sink_terms = {}, {}
    for bi, j in units:
        for p in range(2):
            h = 2 * j + p
            s_both = scores[bi, j, p]
            s = jnp.where(from_prev, s_both[:, 0:blk], s_both[:, blk:2 * blk]) - bias[h]
            sink = sinks_ref[h] * LOG2_E
            m = jnp.maximum(jnp.max(s, axis=-1, keepdims=True), sink)
            probs[bi, j, p] = jnp.exp2(s - m)
            sink_terms[bi, j, p] = jnp.exp2(sink - m)
    yield

    outs = {}
    for bi in seqs:
        for g in range(ATT_KV_HEADS):
            (_, vp), (_, vc) = kv_blocks(bi, g)
            v_stack = jnp.concatenate([jnp.concatenate([v * half[p], half[p]], axis=1)
                                       for v in (vp, vc) for p in range(2)], axis=0)
            group = range(g * pairs_per_kv, (g + 1) * pairs_per_kv)
            p_stack = []
            for j in group:
                e = [probs[bi, j, p].astype(BF16) for p in range(2)]
                p_stack.append(jnp.concatenate([x * m for m in (prev_mask, cur_mask) for x in e], axis=1))
            pv_stack = _dot(jnp.concatenate(p_stack, axis=0), v_stack)
            for jj, j in enumerate(group):
                pv = pv_stack[jj * blk:(jj + 1) * blk, :]
                den = pv[:, LANES:2 * LANES] + jnp.where(low, sink_terms[bi, j, 0], sink_terms[bi, j, 1])
                outs[bi, j] = pv[:, 0:LANES] / den
    yield

    for bi in seqs:
        o = jnp.concatenate([outs[bi, j] for j in range(n_pairs)], axis=1)
        ms = jnp.mean(o * o, axis=-1, keepdims=True)
        o_ref[bi] = (o * lax.rsqrt(ms + EPS) * gatt_ref[...]).astype(BF16)
        kp_ref[bi] = kc_ref[bi]
        vp_ref[bi] = vc_ref[bi]


def _gla_stages(seqs, q_ref, k_ref, v_ref, la_ref, og_ref, cum_ref, bdms_ref, smask_ref,
                ggla_ref, o_ref, state_ref):
    L = GLA_CHUNK
    lane = lax.broadcasted_iota(jnp.int32, (GLA_SUB, LANES), 1)
    head_mask = [jnp.where(lane // GLA_DK == hh, 1.0, 0.0).astype(BF16) for hh in range(2)]
    causal = lax.broadcasted_iota(jnp.int32, (L, L), 0) >= lax.broadcasted_iota(jnp.int32, (L, L), 1)
    units = [(s, pair) for s in seqs for pair in range(GLA_HEADS // 2)]
    kl = lambda pair: slice(pair * LANES, (pair + 1) * LANES)
    vl = lambda pair: slice(pair * 2 * GLA_DV, (pair + 1) * 2 * GLA_DV)

    b, b_in = {}, {}
    for s in seqs:
        la = la_ref[s]
        la_hi = la.astype(BF16)
        la_lo = (la - la_hi.astype(F32)).astype(BF16)
        sums = _dot(cum_ref[...], jnp.concatenate([la_hi, la_lo], axis=0)) * LOG2_E
        b[s] = sums[0:L, :]
        b_in[s] = sums[L:2 * L, :]
    yield

    q_both, keys, q_dec, k_dec_t, b_last = {}, {}, {}, {}, {}
    for s in seqs:
        ref = b[s] - b_in[s]
        b_last[s] = b[s][L - 1:L, :]
        q = q_ref[s].astype(F32)
        k = k_ref[s].astype(F32)
        q_in = (q * jnp.exp2(b_in[s])).astype(BF16)
        q_dec[s] = (q * jnp.exp2(b[s])).astype(BF16)
        k_dec = k * jnp.exp2(b_last[s] - b[s])
        for pair in range(GLA_HEADS // 2):
            k_p, b_p, ref_p = k[:, kl(pair)], b[s][:, kl(pair)], ref[:, kl(pair)]
            expanded = []
            for g in range(N_SUB):
                top = (g + 1) * GLA_SUB
                live = (k_p[0:top, :] * jnp.exp2(ref_p[g * GLA_SUB:g * GLA_SUB + 1, :] - b_p[0:top, :])).astype(BF16)
                expanded.append(live if top == L else
                                jnp.concatenate([live, jnp.zeros((L - top, LANES), BF16)], axis=0))
            keys[s, pair] = jnp.concatenate(expanded, axis=1)
            q_p = q_in[:, kl(pair)]
            zero_group = jnp.zeros((GLA_SUB, LANES), BF16)
            q_both[s, pair] = []
            for t in range(N_SUB // 2):
                lhs = []
                for hh in range(2):
                    for side in range(2):
                        g = 2 * t + side
                        piece = q_p[g * GLA_SUB:(g + 1) * GLA_SUB, :] * head_mask[hh]
                        lhs.append(jnp.concatenate([piece, zero_group] if side == 0 else [zero_group, piece], axis=1))
                q_both[s, pair].append(jnp.concatenate(lhs, axis=0))
            k_dec_t[s, pair] = k_dec[:, kl(pair)].T.astype(BF16)
    yield

    scores = {(u, t): _dot_nt(q_both[u][t], keys[u][:, 2 * t * LANES:2 * (t + 1) * LANES])
              for u in units for t in range(N_SUB // 2)}
    yield

    outs, updates = {}, {}
    for s, pair in units:
        v_p = v_ref[s, :, vl(pair)]
        o_parts = []
        for hh in range(2):
            a = jnp.concatenate([scores[(s, pair), t][2 * hh * GLA_SUB:2 * (hh + 1) * GLA_SUB, :]
                                 for t in range(N_SUB // 2)], axis=0)
            a = jnp.where(causal, a, 0.0).astype(BF16)
            o_parts.append(_dot(a, v_p[:, hh * GLA_DV:(hh + 1) * GLA_DV]))
        state = state_ref[s, pair]
        outs[s, pair] = jnp.concatenate(o_parts, axis=1) + _dot(q_dec[s][:, kl(pair)], state.astype(BF16))
        updates[s, pair] = _dot(k_dec_t[s, pair], v_p)
    yield

    for s, pair in units:
        decay = jnp.broadcast_to(jnp.exp2(b_last[s][:, kl(pair)]), (LANES, LANES)).T
        state_ref[s, pair] = (state_ref[s, pair] * jnp.concatenate([decay, decay], axis=1)
                              + updates[s, pair] * smask_ref[...])
        o = outs[s, pair]
        ms = _dot((o * o).astype(BF16), bdms_ref[...])
        y = o * lax.rsqrt(ms + EPS) * ggla_ref[:, vl(pair)] * og_ref[s, :, vl(pair)].astype(F32)
        o_ref[s, :, vl(pair)] = y.astype(BF16)


def _run_interleaved(*stage_generators):
    pending = list(stage_generators)
    while pending:
        for stages in list(pending):
            if next(stages, "done") == "done":
                pending.remove(stages)


def _route(logits_t):
    lt = logits_t[0:ROUTER_ROWS, :]
    row = lax.broadcasted_iota(jnp.int32, lt.shape, 0)
    neg_inf = -jnp.inf
    g_log = jnp.where(row < N_GROUPS, lt, neg_inf)
    g_max = jnp.max(g_log, axis=0, keepdims=True)
    g_sel = jnp.min(jnp.where(g_log == g_max, row, LANES), axis=0, keepdims=True)
    g_sum = jnp.sum(jnp.where(row < N_GROUPS, jnp.exp(lt - g_max), 0.0), axis=0, keepdims=True)
    p_group = 1.0 / g_sum
    e_lo = ROUTER_LANE0 + EXPERTS_PER_GROUP * g_sel
    in_group = jnp.logical_and(row >= e_lo, row < e_lo + EXPERTS_PER_GROUP)
    e_log = jnp.where(in_group, lt, neg_inf)
    e_max = jnp.max(e_log, axis=0, keepdims=True)
    top1 = jnp.min(jnp.where(e_log == e_max, row, LANES), axis=0, keepdims=True)
    e_log2 = jnp.where(row == top1, neg_inf, e_log)
    e_max2 = jnp.max(e_log2, axis=0, keepdims=True)
    top2 = jnp.min(jnp.where(e_log2 == e_max2, row, LANES), axis=0, keepdims=True)
    ratio = jnp.exp(e_max2 - e_max)
    w_top1 = p_group / (1.0 + ratio)
    w_top2 = p_group * ratio / (1.0 + ratio)
    row8 = lax.broadcasted_iota(jnp.int32, (8, lt.shape[1]), 0)
    weights = jnp.where(row8 == top1 - e_lo, w_top1, 0.0) + jnp.where(row8 == top2 - e_lo, w_top2, 0.0)
    return g_sel, weights


def _outproj_stages(subs, ya_ref, yg_ref, x_ref, mod_ref, wo_ref, g2_ref, wrt_ref, brt_ref, striu_ref,
                    x1_ref, row_ref, lpos_ref, cnt_ref):
    tm = MOE_TILE
    half = x_ref.shape[1]
    per_tile = tm // half
    tile_of = lambda ref, t: jnp.concatenate([ref[t * per_tile + i] for i in range(per_tile)], axis=0)

    mix = {t: _dot(tile_of(ya_ref, t), wo_ref[0:ATT_Q, :]) + _dot(tile_of(yg_ref, t), wo_ref[ATT_Q:ATT_Q + GLA_V, :])
           for t in subs}
    yield
    h2b = {}
    for t in subs:
        parts = []
        for i in range(per_tile):
            s = t * per_tile + i
            x1 = x_ref[s] + mod_ref[s, 2:3, :] * mix[t][i * half:(i + 1) * half, :]
            x1_ref[t * tm + i * half:t * tm + (i + 1) * half, :] = x1
            ms = jnp.mean(x1 * x1, axis=-1, keepdims=True)
            h2 = (x1 * lax.rsqrt(ms + EPS) * g2_ref[...]) * (1.0 + mod_ref[s, 4:5, :]) + mod_ref[s, 3:4, :]
            parts.append(h2.astype(BF16))
        h2b[t] = jnp.concatenate(parts, axis=0)
    yield
    logits_t = {t: _dot_nt(wrt_ref[...], h2b[t]) + brt_ref[...] for t in subs}
    yield

    routed = {t: _route(logits_t[t]) for t in subs}
    row8 = lax.broadcasted_iota(jnp.int32, (8, tm), 0)
    onehot = {t: jnp.where(row8 == routed[t][0], 1.0, 0.0) for t in subs}
    yield
    before = {t: _dot(onehot[t].astype(BF16), striu_ref[...]) for t in subs}
    yield

    local_row = lax.broadcasted_iota(jnp.int32, (LOCAL_ROWS, tm), 0).astype(F32)
    pad_rows = jnp.zeros((LANES - 8, tm), F32)
    for t in subs:
        count = jnp.sum(onehot[t], axis=1, keepdims=True)
        cnt_ref[t] = jnp.broadcast_to(count, (8, LANES))
        padded = jnp.broadcast_to(jnp.floor((count + (CHUNK - 1.0)) * (1.0 / CHUNK)) * CHUNK, (8, tm))
        start = jnp.zeros((8, tm), F32)
        for shift in range(1, N_GROUPS):
            start = start + jnp.where(row8 >= shift, pltpu.roll(padded, shift, axis=0), 0.0)
        lpos = jnp.sum(onehot[t] * (before[t] + start), axis=0, keepdims=True)
        lpos_ref[t] = jnp.broadcast_to(lpos, (8, tm))
        weights = jnp.concatenate([routed[t][1], pad_rows], axis=0).T
        w_hi = weights.astype(BF16)
        w_lo = (weights - w_hi.astype(F32)).astype(BF16)
        perm = jnp.where(local_row == lpos, 1.0, 0.0).astype(BF16)
        row_ref[t * LOCAL_ROWS:(t + 1) * LOCAL_ROWS, :] = _dot(
            perm, jnp.concatenate([h2b[t], w_hi, w_lo], axis=1)).astype(BF16)


def _mix_out_kernel(sinks_ref, x_ref, mod_ref, g1_ref, wf_ref, qn_ref, kn_ref, bdq_ref, bdk_ref, wgk_ref, bgk_ref,
                    gatt_ref, cum_ref, bdms_ref, smask_ref, ggla_ref,
                    wof_ref, g2_ref, wrt_ref, brt_ref, striu_ref,
                    x1_ref, row_ref, lpos_ref, cnt_ref,
                    w_ref, wo_ref, qa_ref, ka_ref, va_ref, qg_ref, kg_ref, vg_ref, la_ref, og_ref,
                    ya_ref, yg_ref, kp_ref, vp_ref, state_ref, w_sem):
    assert x1_ref.shape == wof_ref.shape and x1_ref.dtype == wof_ref.dtype
    _round_w_in(wf_ref, w_ref, wof_ref, wo_ref, x1_ref, w_sem)
    n_seq = x_ref.shape[0]
    seqs_per_tile = MOE_TILE // x_ref.shape[1]
    halves = []
    for first in (0, n_seq // 2):
        seqs = tuple(range(first, first + n_seq // 2))
        halves.append((seqs, (first * len(PROJ_SPLITS) // n_seq,),
                       tuple(range(first // seqs_per_tile, (first + n_seq // 2) // seqs_per_tile))))

    def inproj(sub_tiles):
        return _inproj_stages(sub_tiles, x_ref, mod_ref, g1_ref, qn_ref, kn_ref, bdq_ref, bdk_ref, wgk_ref, bgk_ref,
                              qa_ref, ka_ref, va_ref, qg_ref, kg_ref, vg_ref, la_ref, og_ref, w_ref)

    def mixers(seqs):
        return (_attn_stages(seqs, sinks_ref, qa_ref, ka_ref, va_ref, gatt_ref, ya_ref, kp_ref, vp_ref),
                _gla_stages(seqs, qg_ref, kg_ref, vg_ref, la_ref, og_ref, cum_ref, bdms_ref, smask_ref,
                            ggla_ref, yg_ref, state_ref))

    def outproj(tiles):
        return _outproj_stages(tiles, ya_ref, yg_ref, x_ref, mod_ref, wo_ref, g2_ref, wrt_ref, brt_ref, striu_ref,
                               x1_ref, row_ref, lpos_ref, cnt_ref)

    (seqs_a, subs_a, tiles_a), (seqs_b, subs_b, tiles_b) = halves
    _run_interleaved(inproj(subs_a + subs_b))

    @pl.when(pl.program_id(1) == 0)
    def _():
        state_ref[...] = jnp.zeros_like(state_ref)
        kp_ref[...] = jnp.zeros_like(kp_ref)
        vp_ref[...] = jnp.zeros_like(vp_ref)

    first_half = mixers(seqs_a)
    for stages in first_half:
        next(stages)
    _run_interleaved(*first_half, *mixers(seqs_b))
    _run_interleaved(outproj(tiles_a + tiles_b))


def _mix_out(sinks, x, mod, g1, w_in_t, qn, kn, bdq, bdk, wgk, bgk, gatt, cum, bdms, smask, ggla,
             wo, g2, wr, br, striu):
    assert GLA_CHUNK == WINDOW and MOE_TILE % WINDOW == 0
    B, T, D = x.shape
    L = WINDOW
    N = B * T
    assert B * L == PROJ_TILE
    tiles_per_step = B * L // MOE_TILE
    const = lambda a: pl.BlockSpec(a.shape, lambda b, i: (0,) * a.ndim)
    tiles = lambda *shape: pl.BlockSpec((tiles_per_step,) + shape, lambda b, i: (i,) + (0,) * len(shape))
    act = lambda c, dt=BF16: pltpu.VMEM((B, L, c), dt)
    return pl.pallas_call(
        _mix_out_kernel,
        grid=(1, T // L),
        in_specs=[pl.BlockSpec(memory_space=pltpu.SMEM),
                  pl.BlockSpec((B, L, D), lambda b, i: (0, i, 0)), const(mod), const(g1),
                  pl.BlockSpec(w_in_t.shape, lambda b, i: (0, 0), pipeline_mode=pl.Buffered(1)),
                  const(qn), const(kn), const(bdq), const(bdk), const(wgk), const(bgk),
                  const(gatt), const(cum), const(bdms), const(smask), const(ggla),
                  pl.BlockSpec(memory_space=pl.ANY), const(g2), const(wr), const(br), const(striu)],
        out_specs=[pl.BlockSpec((B * L, D), lambda b, i: (i, 0)),
                   pl.BlockSpec((tiles_per_step * LOCAL_ROWS, D + 2 * LANES), lambda b, i: (i, 0)),
                   tiles(8, MOE_TILE), tiles(8, LANES)],
        out_shape=[jax.ShapeDtypeStruct((N, D), F32),
                   jax.ShapeDtypeStruct((N // MOE_TILE * LOCAL_ROWS, D + 2 * LANES), BF16),
                   jax.ShapeDtypeStruct((N // MOE_TILE, 8, MOE_TILE), F32),
                   jax.ShapeDtypeStruct((N // MOE_TILE, 8, LANES), F32)],
        scratch_shapes=[pltpu.VMEM((IN_COLS_PAD, D), BF16), pltpu.VMEM(wo.shape, BF16),
                        act(ATT_Q), act(2 * ATT_KV), act(2 * ATT_KV), act(GLA_K), act(GLA_K), act(GLA_V),
                        act(GLA_K, F32), act(GLA_V),
                        act(ATT_Q), act(GLA_V), act(2 * ATT_KV), act(2 * ATT_KV),
                        pltpu.VMEM((B, GLA_HEADS // 2, LANES, 2 * GLA_DV), F32),
                        pltpu.SemaphoreType.DMA(())],
        compiler_params=_params("arbitrary", "arbitrary"),
        name="mix_out",
    )(sinks, x, mod, g1, w_in_t, qn, kn, bdq, bdk, wgk, bgk, gatt, cum, bdms, smask, ggla,
      wo, g2, wr, br, striu)


def _chunk_copy(src_ref, src_chunk, dst_ref, dst_chunk, sem):
    return pltpu.make_async_copy(src_ref.at[src_chunk], dst_ref.at[dst_chunk], sem)


def _moe_kernel(src_ref, dst_ref, grp_ref, next_ref, nt_ref, used_ref, rows_ref, w1f_ref, w3f_ref, w2f_ref, y_ref,
                in_buf, out_buf, zero_buf, st1_ref, st3_ref, st2_ref, w1_ref, w3_ref, w2_ref,
                in_sem, out_sem, zero_sem, w_sem):
    j = pl.program_id(0)
    n_tiles = nt_ref[0]
    d_model = w2_ref.shape[2]

    stages = ((w1f_ref, st1_ref, w1_ref), (w3f_ref, st3_ref, w3_ref), (w2f_ref, st2_ref, w2_ref))

    def fetch_weights(group):
        for hbm, stage, _ in stages:
            pltpu.make_async_copy(hbm.at[group], stage, w_sem).start()

    def enter_group():
        for hbm, stage, dst in stages:
            pltpu.make_async_copy(hbm.at[0], stage, w_sem).wait()
        for hbm, stage, dst in stages:
            for k in range(EXPERTS_PER_GROUP):
                dst[k] = stage[k].astype(BF16)

        @pl.when(next_ref[j] != grp_ref[j])
        def _():
            fetch_weights(next_ref[j])

    def gather(tile, slot):
        def body(k, carry):
            _chunk_copy(rows_ref, src_ref[tile * TILE_CHUNKS + k], in_buf.at[slot], k, in_sem.at[slot]).start()
            return carry
        lax.fori_loop(0, TILE_CHUNKS, body, 0, unroll=True)

    def wait_gather(slot):
        def body(k, carry):
            _chunk_copy(rows_ref, 0, in_buf.at[slot], k, in_sem.at[slot]).wait()
            return carry
        lax.fori_loop(0, TILE_CHUNKS, body, 0, unroll=True)

    def scatter(tile, slot):
        def body(k, carry):
            _chunk_copy(out_buf.at[slot], k, y_ref, dst_ref[tile * TILE_CHUNKS + k], out_sem.at[slot]).start()
            return carry
        lax.fori_loop(0, TILE_CHUNKS, body, 0, unroll=True)

    def wait_scatter(slot):
        def body(k, carry):
            _chunk_copy(out_buf.at[slot], k, y_ref, 0, out_sem.at[slot]).wait()
            return carry
        lax.fori_loop(0, TILE_CHUNKS, body, 0, unroll=True)

    def zero_fill(wait):
        def per_tile(i, carry):
            def body(c, inner):
                copy = _chunk_copy(zero_buf, 0, y_ref, i * LOCAL_CHUNKS + c, zero_sem)
                if wait:
                    copy.wait()
                else:
                    copy.start()
                return inner
            return lax.fori_loop(used_ref[i], LOCAL_CHUNKS, body, carry)
        lax.fori_loop(0, used_ref.shape[0], per_tile, 0)

    @pl.when(j == 0)
    def _():
        fetch_weights(grp_ref[0])
        zero_buf[...] = jnp.zeros_like(zero_buf)
        scratch0 = used_ref.shape[0] * LOCAL_CHUNKS
        for wait in (False, True):
            for k in range(2 * TILE_CHUNKS):
                copy = _chunk_copy(zero_buf, 0, y_ref, scratch0 + k, zero_sem)
                copy.wait() if wait else copy.start()
        zero_fill(wait=False)
        gather(0, 0)

    @pl.when(jnp.logical_or(j == 0, grp_ref[j] != grp_ref[jnp.maximum(j - 1, 0)]))
    def _():
        enter_group()

    @pl.when(j + 1 < n_tiles)
    def _():
        gather(j + 1, (j + 1) % 2)

    @pl.when(j < n_tiles)
    def _():
        slot = j % 2
        wait_gather(slot)
        rows = in_buf[slot].reshape(EXPERT_TILE, in_buf.shape[-1])
        h = rows[:, 0:d_model]
        weights = rows[:, d_model:d_model + LANES].astype(F32) + rows[:, d_model + LANES:].astype(F32)
        experts = range(EXPERTS_PER_GROUP)
        up = [(_dot(h, w1_ref[k]), _dot(h, w3_ref[k])) for k in experts]
        hid = [(a * _sigmoid(a) * g * weights[:, k:k + 1]).astype(BF16) for k, (a, g) in zip(experts, up)]
        y = _dot(hid[0], w2_ref[0])
        for k in experts[1:]:
            y = y + _dot(hid[k], w2_ref[k])

        @pl.when(j >= 2)
        def _():
            wait_scatter(slot)

        out_buf[slot] = y.astype(BF16).reshape(TILE_CHUNKS, CHUNK, d_model)
        scatter(j, slot)

        @pl.when(j == n_tiles - 1)
        def _():
            @pl.when(j >= 1)
            def _():
                wait_scatter(1 - slot)
            wait_scatter(slot)
            zero_fill(wait=True)


def _combine_kernel(x1_ref, mod_ref, lpos_ref, y_ref, o_ref):
    tm = MOE_TILE
    half = WINDOW
    per_tile = tm // half
    tiles_per_block = o_ref.shape[0] // per_tile
    local_row = lax.broadcasted_iota(jnp.int32, (tm, LOCAL_ROWS), 1).astype(F32)
    for t in range(x1_ref.shape[0] // tm):
        lpos = jnp.broadcast_to(lpos_ref[t][0:1, :], (LANES, tm)).T[:, 0:1]
        unsort = jnp.where(local_row == lpos, 1.0, 0.0).astype(BF16)
        y = _dot(unsort, y_ref[t * LOCAL_ROWS:(t + 1) * LOCAL_ROWS, :])
        block, tile = divmod(t, tiles_per_block)
        for i in range(per_tile):
            s = tile * per_tile + i
            rows = slice(t * tm + i * half, t * tm + (i + 1) * half)
            o_ref[s, block * half:(block + 1) * half, :] = (
                x1_ref[rows, :] + mod_ref[s, 5:6, :] * y[i * half:(i + 1) * half, :])


def _moe_plan(cnt):
    n_local = cnt.shape[0]
    chunks = (cnt + CHUNK - 1) // CHUNK
    used = jnp.sum(chunks, axis=1)
    local_off = jnp.cumsum(chunks, axis=1) - chunks
    tiles_g = (jnp.sum(chunks, axis=0) + TILE_CHUNKS - 1) // TILE_CHUNKS
    tile_end = jnp.cumsum(tiles_g)
    n_tiles = tile_end[-1]
    group_start = (tile_end - tiles_g) * TILE_CHUNKS
    seg_len = chunks.T.reshape(-1)
    seg_start = (group_start[:, None] + (jnp.cumsum(chunks, axis=0) - chunks).T).reshape(-1)
    seg_src = (jnp.arange(n_local)[None, :] * LOCAL_CHUNKS + local_off.T).reshape(-1)
    max_chunks = n_local * MOE_TILE // CHUNK + n_local * N_GROUPS + N_GROUPS * TILE_CHUNKS
    max_tiles = (max_chunks + TILE_CHUNKS - 1) // TILE_CHUNKS
    c = jnp.arange(max_tiles * TILE_CHUNKS)[:, None]
    within = c - seg_start[None, :]
    hit = jnp.logical_and(within >= 0, within < seg_len[None, :])
    valid = jnp.any(hit, axis=1)
    src = jnp.sum(jnp.where(hit, seg_src[None, :] + within, 0), axis=1)
    src = jnp.where(valid, src, LOCAL_CHUNKS - 1)
    slot_k = c[:, 0] % (2 * TILE_CHUNKS)
    dst = jnp.where(valid, src, n_local * LOCAL_CHUNKS + slot_k)
    j = jnp.minimum(jnp.arange(max_tiles), n_tiles - 1)
    grp = jnp.sum(j[:, None] >= tile_end[None, :], axis=1)
    gid = jnp.arange(N_GROUPS)
    later = jnp.where(jnp.logical_and(gid[None, :] > gid[:, None], tiles_g[None, :] > 0), gid[None, :], N_GROUPS)
    next_of = jnp.min(later, axis=1)
    next_of = jnp.where(next_of == N_GROUPS, gid, next_of)
    next_grp = jnp.sum(jnp.where(grp[:, None] == gid[None, :], next_of[None, :], 0), axis=1)
    i32 = lambda a: a.astype(jnp.int32)
    return i32(src), i32(dst), i32(grp), i32(next_grp), i32(n_tiles).reshape(1), i32(used)


def _moe(plan, rows_local, w1g, w3g, w2g):
    src, dst, grp, next_grp, n_tiles, used = plan
    D = w2g.shape[3]
    n_rows, cols = rows_local.shape
    n_chunks = n_rows // CHUNK
    hbm = pl.BlockSpec(memory_space=pl.ANY)
    group_weights = [w1g, w3g, w2g]
    y = pl.pallas_call(
        _moe_kernel,
        grid_spec=pltpu.PrefetchScalarGridSpec(
            num_scalar_prefetch=6,
            grid=(grp.shape[0],),
            in_specs=[hbm, hbm, hbm, hbm],
            out_specs=hbm,
            scratch_shapes=([pltpu.VMEM((2, TILE_CHUNKS, CHUNK, cols), BF16),
                             pltpu.VMEM((2, TILE_CHUNKS, CHUNK, D), BF16),
                             pltpu.VMEM((1, CHUNK, D), BF16)]
                            + [pltpu.VMEM(w.shape[1:], F32) for w in group_weights]
                            + [pltpu.VMEM(w.shape[1:], BF16) for w in group_weights]
                            + [pltpu.SemaphoreType.DMA((2,)), pltpu.SemaphoreType.DMA((2,)),
                               pltpu.SemaphoreType.DMA(()), pltpu.SemaphoreType.DMA(())])),
        out_shape=jax.ShapeDtypeStruct((n_chunks + 2 * TILE_CHUNKS, CHUNK, D), BF16),
        compiler_params=_params("arbitrary"),
        name="moe",
    )(src, dst, grp, next_grp, n_tiles, used, rows_local.reshape(n_chunks, CHUNK, cols), w1g, w3g, w2g)
    return y.reshape((n_chunks + 2 * TILE_CHUNKS) * CHUNK, D)


def _combine(x1, mod, lpos, y_local, *, B, T):
    N, D = x1.shape
    L = COMBINE_BLOCKS * WINDOW
    subs = B * L // MOE_TILE
    return pl.pallas_call(
        _combine_kernel,
        grid=(T // L,),
        in_specs=[pl.BlockSpec((B * L, D), lambda i: (i, 0)),
                  pl.BlockSpec(mod.shape, lambda i: (0, 0, 0)),
                  pl.BlockSpec((subs, 8, MOE_TILE), lambda i: (i, 0, 0)),
                  pl.BlockSpec((subs * LOCAL_ROWS, D), lambda i: (i, 0))],
        out_specs=pl.BlockSpec((B, L, D), lambda i: (0, i, 0)),
        out_shape=jax.ShapeDtypeStruct((B, T, D), F32),
        compiler_params=_params("arbitrary"),
        name="moe_combine",
    )(x1, mod, lpos, y_local)


def _block_diag(n, blk, value, dtype):
    r = np.arange(n)[:, None] // blk
    c = np.arange(n)[None, :] // blk
    return jnp.asarray(np.where(r == c, value, 0.0), dtype)


def _gla_constants():
    L = GLA_CHUNK
    i = np.arange(L)[:, None]
    j = np.arange(L)[None, :]
    tri = j <= i
    bdtri = np.logical_and(j <= i, i // GLA_SUB == j // GLA_SUB)
    cum = np.block([[tri, tri], [bdtri, bdtri]])
    bdms = _block_diag(2 * GLA_DV, GLA_DV, 1.0 / GLA_DV, BF16)
    d = np.arange(LANES)[:, None] // GLA_DK
    e = np.arange(2 * GLA_DV)[None, :] // GLA_DV
    smask = d == e
    return jnp.asarray(cum, BF16), bdms, jnp.asarray(smask, F32)


def kernel(x, c, w_ada, b_ada, g_norm1, w_in, q_norm, k_norm, sinks, w_gk2, b_gk, g_gla_out, g_att_out,
           w_out, g_norm2, w_group, b_group, w_router, b_router, w1, w3, w2):
    B, T, D = x.shape
    N = B * T
    depth = w_ada.shape[0]
    cum, bdms, smask = _gla_constants()
    bdq = _block_diag(2 * LANES, HEAD_DIM, 1.0 / HEAD_DIM, BF16)
    bdk = _block_diag(ATT_KV, HEAD_DIM, 1.0 / HEAD_DIM, BF16)

    for l in range(depth):
        mod = _adaln_mod(c, w_ada[l], b_ada[l]).reshape(B, 6, D)

        wgk = jnp.concatenate([w_gk2[l], jnp.zeros((LANES - GLA_RANK, GLA_K), F32)], axis=0).astype(BF16)
        pad = LANES - N_GROUPS - N_EXPERTS
        wr_t = jnp.concatenate([w_group[l], w_router[l], jnp.zeros((D, pad), F32)], axis=1).T.astype(BF16)
        br_t = jnp.concatenate([b_group[l], b_router[l], jnp.zeros((pad,), F32)]).reshape(LANES, 1)
        striu = jnp.asarray(np.arange(MOE_TILE)[:, None] < np.arange(MOE_TILE)[None, :], BF16)
        x1, rows_local, lpos, cnt = _mix_out(
            sinks[l], x, mod, g_norm1[l].reshape(1, D), w_in[l].T,
            jnp.tile(q_norm[l], ATT_HEADS).reshape(1, ATT_Q), jnp.tile(k_norm[l], ATT_KV_HEADS).reshape(1, ATT_KV),
            bdq, bdk, wgk, b_gk[l].reshape(1, GLA_K),
            g_att_out[l].reshape(1, ATT_Q), cum, bdms, smask, jnp.tile(g_gla_out[l], GLA_HEADS).reshape(1, GLA_V),
            w_out[l], g_norm2[l].reshape(1, D), wr_t, br_t, striu)
        plan = _moe_plan(cnt[:, :N_GROUPS, 0].astype(jnp.int32))
        by_group = lambda w: w.reshape((N_GROUPS, EXPERTS_PER_GROUP) + w.shape[1:])
        y_local = _moe(plan, rows_local, by_group(w1[l]), by_group(w3[l]), by_group(w2[l]))
        x = _combine(x1, mod, lpos, y_local, B=B, T=T)
    return x
```

```python
import jax
import jax.numpy as jnp
import numpy as np
from jax import lax
from jax.experimental import pallas as pl
from jax.experimental.pallas import tpu as pltpu

F32 = jnp.float32
BF16 = jnp.bfloat16

EPS = 1e-6
LOG2_E = 1.4426950408889634
ATT_HEADS = 8
ATT_KV_HEADS = 2
HEAD_DIM = 64
WINDOW = 128
ATT_Q = ATT_HEADS * HEAD_DIM
ATT_KV = ATT_KV_HEADS * HEAD_DIM
GLA_HEADS = 4
GLA_DK = 64
GLA_DV = 128
GLA_RANK = 16
GLA_NORMALIZER = 16.0
GLA_K = GLA_HEADS * GLA_DK
GLA_V = GLA_HEADS * GLA_DV
N_GROUPS = 4
EXPERTS_PER_GROUP = 4
N_EXPERTS = N_GROUPS * EXPERTS_PER_GROUP

LANES = 128
MOD_COLS = 768
PROJ_SPLITS = (512, 512)
PROJ_TILE = sum(PROJ_SPLITS)
COMBINE_BLOCKS = 2
ROUTER_ROWS = 24
GLA_CHUNK = 128
GLA_SUB = 16
N_SUB = GLA_CHUNK // GLA_SUB
ROUTER_LANE0 = N_GROUPS
VMEM_LIMIT = 62 * 1024 * 1024
MOE_TILE = 256
EXPERT_TILE = 512
CHUNK = 16
TILE_CHUNKS = EXPERT_TILE // CHUNK
LOCAL_CHUNKS = (MOE_TILE + N_GROUPS * (CHUNK - 1)) // CHUNK + 2
LOCAL_ROWS = LOCAL_CHUNKS * CHUNK

_QA0, _KA0, _VA0 = 0, ATT_Q, ATT_Q + ATT_KV
_QG0 = _VA0 + ATT_KV
_KG0 = _QG0 + GLA_K
_VG0 = _KG0 + GLA_K
_OG0 = _VG0 + GLA_V
_LR0 = _OG0 + GLA_V
IN_COLS_PAD = _LR0 + LANES


def _dot(a, b):
    return jnp.dot(a, b, preferred_element_type=F32)


def _dot_nt(a, b):
    return lax.dot_general(a, b, (((1,), (1,)), ((), ())), preferred_element_type=F32)


def _sigmoid(x):
    return 1.0 / (1.0 + jnp.exp(-x))


def _params(*sem):
    return pltpu.CompilerParams(dimension_semantics=sem, vmem_limit_bytes=VMEM_LIMIT)


def _mod_kernel(c_ref, w_ref, b_ref, o_ref):
    c = c_ref[...]
    s = (c * _sigmoid(c)).astype(BF16)
    o_ref[...] = _dot(s, w_ref[...].astype(BF16)) + b_ref[...]


def _adaln_mod(c, w_ada, b_ada):
    B, D = c.shape
    n = w_ada.shape[1]
    tn = MOD_COLS
    return pl.pallas_call(
        _mod_kernel,
        grid=(n // tn,),
        in_specs=[pl.BlockSpec((B, D), lambda j: (0, 0)),
                  pl.BlockSpec((D, tn), lambda j: (0, j)),
                  pl.BlockSpec((1, tn), lambda j: (0, j))],
        out_specs=pl.BlockSpec((B, tn), lambda j: (0, j)),
        out_shape=jax.ShapeDtypeStruct((B, n), F32),
        compiler_params=_params("arbitrary"),
        name="adaln_mod",
    )(c, w_ada, b_ada.reshape(1, n))


def _round_w_in(wf_ref, w_ref, wof_ref, wo_ref, stage_ref, sem):
    @pl.when(pl.program_id(1) == 0)
    def _():
        fetch = pltpu.make_async_copy(wof_ref, stage_ref, sem)
        fetch.start()
        lr_src = _OG0
        w_ref[0:_OG0, :] = wf_ref[0:_OG0, :].astype(BF16)
        w_ref[_OG0:_LR0, :] = wf_ref[lr_src + GLA_RANK:lr_src + GLA_RANK + GLA_V, :].astype(BF16)
        w_ref[_LR0:_LR0 + GLA_RANK, :] = wf_ref[lr_src:lr_src + GLA_RANK, :].astype(BF16)
        w_ref[_LR0 + GLA_RANK:IN_COLS_PAD, :] = jnp.zeros((LANES - GLA_RANK, wf_ref.shape[1]), BF16)
        fetch.wait()
        wo_ref[...] = stage_ref[...].astype(BF16)


def _inproj_stages(subs, x_ref, mod_ref, g1_ref, qn_ref, kn_ref, bdq_ref, bdk_ref, wgk_ref, bgk_ref,
                   qa_ref, ka_ref, va_ref, qg_ref, kg_ref, vg_ref, la_ref, og_ref, w_ref):
    blk = x_ref.shape[1]
    seqs_of = [range(sum(PROJ_SPLITS[:t]) // blk, sum(PROJ_SPLITS[:t + 1]) // blk)
               for t in range(len(PROJ_SPLITS))]

    def put(ref, t, value, lanes=slice(None)):
        for i, s in enumerate(seqs_of[t]):
            ref[s, :, lanes] = value[i * blk:(i + 1) * blk, :]

    h = {}
    for t in subs:
        parts = []
        for s in seqs_of[t]:
            x = x_ref[s]
            ms = jnp.mean(x * x, axis=-1, keepdims=True)
            xn = x * lax.rsqrt(ms + EPS) * g1_ref[...]
            parts.append((xn * (1.0 + mod_ref[s, 1:2, :]) + mod_ref[s, 0:1, :]).astype(BF16))
        h[t] = jnp.concatenate(parts, axis=0)
    yield

    proj = lambda t, c0, width: _dot_nt(h[t], w_ref[c0:c0 + width, :])
    gate_of = lambda lr: _dot(lr.astype(BF16), wgk_ref[...]) + bgk_ref[...]
    qa, kv, q_ms, k_ms, qk_g, vg, og, lr, gate = ({} for _ in range(9))
    prev = None
    for t in subs:
        qa[t] = proj(t, _QA0, ATT_Q)
        if prev is not None:
            gate[prev] = gate_of(lr[prev])
        prev = t
        yield
        kv[t] = proj(t, _KA0, 2 * ATT_KV)
        q_sq = (qa[t] * qa[t]).astype(BF16)
        q_ms[t] = jnp.concatenate([_dot(q_sq[:, c:c + 2 * LANES], bdq_ref[...])
                                   for c in range(0, ATT_Q, 2 * LANES)], axis=1)
        yield
        qk_g[t] = proj(t, _QG0, 2 * GLA_K)
        k = kv[t][:, 0:ATT_KV]
        k_ms[t] = _dot((k * k).astype(BF16), bdk_ref[...])
        yield
        vg[t] = proj(t, _VG0, GLA_V)
        yield
        og[t] = proj(t, _OG0, GLA_V)
        lr[t] = proj(t, _LR0, LANES)
        yield
    gate[prev] = gate_of(lr[prev])
    yield

    for t in subs:
        low = lax.broadcasted_iota(jnp.int32, (PROJ_SPLITS[t], ATT_KV), 1) < HEAD_DIM
        qn = jnp.concatenate([qn_ref[...]] * ATT_HEADS, axis=1)
        kn = jnp.concatenate([kn_ref[...]] * ATT_KV_HEADS, axis=1)
        put(qa_ref, t, (qa[t] * lax.rsqrt(q_ms[t] + EPS) * qn * (HEAD_DIM ** -0.5 * LOG2_E)).astype(BF16))
        k = kv[t][:, 0:ATT_KV] * lax.rsqrt(k_ms[t] + EPS) * kn
        v = kv[t][:, ATT_KV:2 * ATT_KV]
        for src, dst in ((k, ka_ref), (v, va_ref)):
            swapped = pltpu.roll(src, HEAD_DIM, axis=1)
            put(dst, t, jnp.where(low, src, swapped).astype(BF16), slice(0, LANES))
            put(dst, t, jnp.where(low, swapped, src).astype(BF16), slice(LANES, 2 * LANES))
        put(qg_ref, t, (qk_g[t][:, 0:GLA_K] * (GLA_DK ** -0.5)).astype(BF16))
        put(kg_ref, t, qk_g[t][:, GLA_K:2 * GLA_K].astype(BF16))
        put(vg_ref, t, vg[t].astype(BF16))
        put(og_ref, t, (og[t] * _sigmoid(og[t])).astype(BF16))
        log_sig = jnp.minimum(gate[t], 0.0) - jnp.log(1.0 + jnp.exp(-jnp.abs(gate[t])))
        put(la_ref, t, log_sig * (1.0 / GLA_NORMALIZER))


def _attn_stages(seqs, sinks_ref, q_ref, kc_ref, vc_ref, gatt_ref, o_ref, kp_ref, vp_ref):
    blk = WINDOW
    first = pl.program_id(1) == 0
    qi = lax.broadcasted_iota(jnp.int32, (blk, blk), 0)
    cj = lax.broadcasted_iota(jnp.int32, (blk, blk), 1)
    from_prev = cj > qi
    dist = (qi - cj + jnp.where(from_prev, blk, 0)).astype(F32)
    no_prev = jnp.where(jnp.logical_and(from_prev, first), -1e30, 0.0)
    low = cj < HEAD_DIM
    half = (jnp.where(low, 1.0, 0.0).astype(BF16), jnp.where(low, 0.0, 1.0).astype(BF16))
    half2 = tuple(jnp.concatenate([m, m], axis=0) for m in half)
    prev_mask = jnp.where(from_prev, 1.0, 0.0).astype(BF16)
    cur_mask = jnp.where(from_prev, 0.0, 1.0).astype(BF16)

    n_pairs = ATT_HEADS // 2
    pairs_per_kv = n_pairs // ATT_KV_HEADS
    units = [(bi, j) for bi in seqs for j in range(n_pairs)]

    def kv_blocks(bi, g):
        lanes = slice(g * LANES, (g + 1) * LANES)
        return (kp_ref[bi, :, lanes], vp_ref[bi, :, lanes]), (kc_ref[bi, :, lanes], vc_ref[bi, :, lanes])

    scores = {}
    for bi in seqs:
        for g in range(ATT_KV_HEADS):
            (kp, _), (kc, _) = kv_blocks(bi, g)
            k_both = jnp.concatenate([kp, kc], axis=0)
            group = range(g * pairs_per_kv, (g + 1) * pairs_per_kv)
            q_stack = jnp.concatenate([q_ref[bi, :, j * LANES:(j + 1) * LANES] for j in group], axis=0)
            for p in range(2):
                s_stack = _dot_nt(q_stack, k_both * half2[p])
                for jj, j in enumerate(group):
                    scores[bi, j, p] = s_stack[jj * blk:(jj + 1) * blk, :]
    yield

    bias = [2.0 ** (-8.0 * (h + 1) / ATT_HEADS) * LOG2_E * dist - no_prev for h in range(ATT_HEADS)]
    probs, sink_terms = {}, {}
    for bi, j in units:
        for p in range(2):
            h = 2 * j + p
            s_both = scores[bi, j, p]
            s = jnp.where(from_prev, s_both[:, 0:blk], s_both[:, blk:2 * blk]) - bias[h]
            sink = sinks_ref[h] * LOG2_E
            m = jnp.maximum(jnp.max(s, axis=-1, keepdims=True), sink)
            probs[bi, j, p] = jnp.exp2(s - m)
            sink_terms[bi, j, p] = jnp.exp2(sink - m)
    yield

    outs = {}
    for bi in seqs:
        for g in range(ATT_KV_HEADS):
            (_, vp), (_, vc) = kv_blocks(bi, g)
            v_stack = jnp.concatenate([jnp.concatenate([v * half[p], half[p]], axis=1)
                                       for v in (vp, vc) for p in range(2)], axis=0)
            group = range(g * pairs_per_kv, (g + 1) * pairs_per_kv)
            p_stack = []
            for j in group:
                e = [probs[bi, j, p].astype(BF16) for p in range(2)]
                p_stack.append(jnp.concatenate([x * m for m in (prev_mask, cur_mask) for x in e], axis=1))
            pv_stack = _dot(jnp.concatenate(p_stack, axis=0), v_stack)
            for jj, j in enumerate(group):
                pv = pv_stack[jj * blk:(jj + 1) * blk, :]
                den = pv[:, LANES:2 * LANES] + jnp.where(low, sink_terms[bi, j, 0], sink_terms[bi, j, 1])
                outs[bi, j] = pv[:, 0:LANES] / den
    yield

    for bi in seqs:
        o = jnp.concatenate([outs[bi, j] for j in range(n_pairs)], axis=1)
        ms = jnp.mean(o * o, axis=-1, keepdims=True)
        o_ref[bi] = (o * lax.rsqrt(ms + EPS) * gatt_ref[...]).astype(BF16)
        kp_ref[bi] = kc_ref[bi]
        vp_ref[bi] = vc_ref[bi]


def _gla_stages(seqs, q_ref, k_ref, v_ref, la_ref, og_ref, cum_ref, bdms_ref, smask_ref,
                ggla_ref, o_ref, state_ref):
    L = GLA_CHUNK
    lane = lax.broadcasted_iota(jnp.int32, (GLA_SUB, LANES), 1)
    head_mask = [jnp.where(lane // GLA_DK == hh, 1.0, 0.0).astype(BF16) for hh in range(2)]
    causal = lax.broadcasted_iota(jnp.int32, (L, L), 0) >= lax.broadcasted_iota(jnp.int32, (L, L), 1)
    units = [(s, pair) for s in seqs for pair in range(GLA_HEADS // 2)]
    kl = lambda pair: slice(pair * LANES, (pair + 1) * LANES)
    vl = lambda pair: slice(pair * 2 * GLA_DV, (pair + 1) * 2 * GLA_DV)

    b, b_in = {}, {}
    for s in seqs:
        la = la_ref[s]
        la_hi = la.astype(BF16)
        la_lo = (la - la_hi.astype(F32)).astype(BF16)
        sums = _dot(cum_ref[...], jnp.concatenate([la_hi, la_lo], axis=0)) * LOG2_E
        b[s] = sums[0:L, :]
        b_in[s] = sums[L:2 * L, :]
    yield

    q_both, keys, q_dec, k_dec_t, b_last = {}, {}, {}, {}, {}
    for s in seqs:
        ref = b[s] - b_in[s]
        b_last[s] = b[s][L - 1:L, :]
        q = q_ref[s].astype(F32)
        k = k_ref[s].astype(F32)
        q_in = (q * jnp.exp2(b_in[s])).astype(BF16)
        q_dec[s] = (q * jnp.exp2(b[s])).astype(BF16)
        k_dec = k * jnp.exp2(b_last[s] - b[s])
        for pair in range(GLA_HEADS // 2):
            k_p, b_p, ref_p = k[:, kl(pair)], b[s][:, kl(pair)], ref[:, kl(pair)]
            expanded = []
            for g in range(N_SUB):
                top = (g + 1) * GLA_SUB
                live = (k_p[0:top, :] * jnp.exp2(ref_p[g * GLA_SUB:g * GLA_SUB + 1, :] - b_p[0:top, :])).astype(BF16)
                expanded.append(live if top == L else
                                jnp.concatenate([live, jnp.zeros((L - top, LANES), BF16)], axis=0))
            keys[s, pair] = jnp.concatenate(expanded, axis=1)
            q_p = q_in[:, kl(pair)]
            zero_group = jnp.zeros((GLA_SUB, LANES), BF16)
            q_both[s, pair] = []
            for t in range(N_SUB // 2):
                lhs = []
                for hh in range(2):
                    for side in range(2):
                        g = 2 * t + side
                        piece = q_p[g * GLA_SUB:(g + 1) * GLA_SUB, :] * head_mask[hh]
                        lhs.append(jnp.concatenate([piece, zero_group] if side == 0 else [zero_group, piece], axis=1))
                q_both[s, pair].append(jnp.concatenate(lhs, axis=0))
            k_dec_t[s, pair] = k_dec[:, kl(pair)].T.astype(BF16)
    yield

    scores = {(u, t): _dot_nt(q_both[u][t], keys[u][:, 2 * t * LANES:2 * (t + 1) * LANES])
              for u in units for t in range(N_SUB // 2)}
    yield

    outs, updates = {}, {}
    for s, pair in units:
        v_p = v_ref[s, :, vl(pair)]
        o_parts = []
        for hh in range(2):
            a = jnp.concatenate([scores[(s, pair), t][2 * hh * GLA_SUB:2 * (hh + 1) * GLA_SUB, :]
                                 for t in range(N_SUB // 2)], axis=0)
            a = jnp.where(causal, a, 0.0).astype(BF16)
            o_parts.append(_dot(a, v_p[:, hh * GLA_DV:(hh + 1) * GLA_DV]))
        state = state_ref[s, pair]
        outs[s, pair] = jnp.concatenate(o_parts, axis=1) + _dot(q_dec[s][:, kl(pair)], state.astype(BF16))
        updates[s, pair] = _dot(k_dec_t[s, pair], v_p)
    yield

    for s, pair in units:
        decay = jnp.broadcast_to(jnp.exp2(b_last[s][:, kl(pair)]), (LANES, LANES)).T
        state_ref[s, pair] = (state_ref[s, pair] * jnp.concatenate([decay, decay], axis=1)
                              + updates[s, pair] * smask_ref[...])
        o = outs[s, pair]
        ms = _dot((o * o).astype(BF16), bdms_ref[...])
        gain = jnp.concatenate([ggla_ref[...]] * 2, axis=1)
        y = o * lax.rsqrt(ms + EPS) * gain * og_ref[s, :, vl(pair)].astype(F32)
        o_ref[s, :, vl(pair)] = y.astype(BF16)


def _run_interleaved(*stage_generators):
    pending = list(stage_generators)
    while pending:
        for stages in list(pending):
            if next(stages, "done") == "done":
                pending.remove(stages)


def _route(logits_t):
    lt = logits_t[0:ROUTER_ROWS, :]
    row = lax.broadcasted_iota(jnp.int32, lt.shape, 0)
    neg_inf = -jnp.inf
    g_log = jnp.where(row < N_GROUPS, lt, neg_inf)
    g_max = jnp.max(g_log, axis=0, keepdims=True)
    g_sel = jnp.min(jnp.where(g_log == g_max, row, LANES), axis=0, keepdims=True)
    g_sum = jnp.sum(jnp.where(row < N_GROUPS, jnp.exp(lt - g_max), 0.0), axis=0, keepdims=True)
    p_group = 1.0 / g_sum
    e_lo = ROUTER_LANE0 + EXPERTS_PER_GROUP * g_sel
    in_group = jnp.logical_and(row >= e_lo, row < e_lo + EXPERTS_PER_GROUP)
    e_log = jnp.where(in_group, lt, neg_inf)
    e_max = jnp.max(e_log, axis=0, keepdims=True)
    top1 = jnp.min(jnp.where(e_log == e_max, row, LANES), axis=0, keepdims=True)
    e_log2 = jnp.where(row == top1, neg_inf, e_log)
    e_max2 = jnp.max(e_log2, axis=0, keepdims=True)
    top2 = jnp.min(jnp.where(e_log2 == e_max2, row, LANES), axis=0, keepdims=True)
    ratio = jnp.exp(e_max2 - e_max)
    w_top1 = p_group / (1.0 + ratio)
    w_top2 = p_group * ratio / (1.0 + ratio)
    row8 = lax.broadcasted_iota(jnp.int32, (8, lt.shape[1]), 0)
    weights = jnp.where(row8 == top1 - e_lo, w_top1, 0.0) + jnp.where(row8 == top2 - e_lo, w_top2, 0.0)
    return g_sel, weights


def _outproj_stages(subs, ya_ref, yg_ref, x_ref, mod_ref, wo_ref, g2_ref, wrt_ref, brt_ref, striu_ref,
                    x1_ref, row_ref, lpos_ref, cnt_ref):
    tm = MOE_TILE
    half = x_ref.shape[1]
    per_tile = tm // half
    tile_of = lambda ref, t: jnp.concatenate([ref[t * per_tile + i] for i in range(per_tile)], axis=0)

    mix = {t: _dot(tile_of(ya_ref, t), wo_ref[0:ATT_Q, :]) + _dot(tile_of(yg_ref, t), wo_ref[ATT_Q:ATT_Q + GLA_V, :])
           for t in subs}
    yield
    h2b = {}
    for t in subs:
        parts = []
        for i in range(per_tile):
            s = t * per_tile + i
            x1 = x_ref[s] + mod_ref[s, 2:3, :] * mix[t][i * half:(i + 1) * half, :]
            x1_ref[t * tm + i * half:t * tm + (i + 1) * half, :] = x1
            ms = jnp.mean(x1 * x1, axis=-1, keepdims=True)
            h2 = (x1 * lax.rsqrt(ms + EPS) * g2_ref[...]) * (1.0 + mod_ref[s, 4:5, :]) + mod_ref[s, 3:4, :]
            parts.append(h2.astype(BF16))
        h2b[t] = jnp.concatenate(parts, axis=0)
    yield
    logits_t = {t: _dot_nt(wrt_ref[...], h2b[t]) + brt_ref[...] for t in subs}
    yield

    routed = {t: _route(logits_t[t]) for t in subs}
    row8 = lax.broadcasted_iota(jnp.int32, (8, tm), 0)
    onehot = {t: jnp.where(row8 == routed[t][0], 1.0, 0.0) for t in subs}
    yield
    before = {t: _dot(onehot[t].astype(BF16), striu_ref[...]) for t in subs}
    yield

    local_row = lax.broadcasted_iota(jnp.int32, (LOCAL_ROWS, tm), 0).astype(F32)
    pad_rows = jnp.zeros((LANES - 8, tm), F32)
    for t in subs:
        count = jnp.sum(onehot[t], axis=1, keepdims=True)
        cnt_ref[t] = jnp.broadcast_to(count, (8, LANES))
        padded = jnp.broadcast_to(jnp.floor((count + (CHUNK - 1.0)) * (1.0 / CHUNK)) * CHUNK, (8, tm))
        start = jnp.zeros((8, tm), F32)
        for shift in range(1, N_GROUPS):
            start = start + jnp.where(row8 >= shift, pltpu.roll(padded, shift, axis=0), 0.0)
        lpos = jnp.sum(onehot[t] * (before[t] + start), axis=0, keepdims=True)
        lpos_ref[t] = jnp.broadcast_to(lpos, (8, tm))
        weights = jnp.concatenate([routed[t][1], pad_rows], axis=0).T
        w_hi = weights.astype(BF16)
        w_lo = (weights - w_hi.astype(F32)).astype(BF16)
        perm = jnp.where(local_row == lpos, 1.0, 0.0).astype(BF16)
        row_ref[t * LOCAL_ROWS:(t + 1) * LOCAL_ROWS, :] = _dot(
            perm, jnp.concatenate([h2b[t], w_hi, w_lo], axis=1)).astype(BF16)


def _mix_out_kernel(sinks_ref, x_ref, mod_ref, g1_ref, wf_ref, qn_ref, kn_ref, bdq_ref, bdk_ref, wgk_ref, bgk_ref,
                    gatt_ref, cum_ref, bdms_ref, smask_ref, ggla_ref,
                    wof_ref, g2_ref, wrt_ref, brt_ref, striu_ref,
                    x1_ref, row_ref, lpos_ref, cnt_ref,
                    w_ref, wo_ref, qa_ref, ka_ref, va_ref, qg_ref, kg_ref, vg_ref, la_ref, og_ref,
                    ya_ref, yg_ref, kp_ref, vp_ref, state_ref, w_sem):
    assert x1_ref.shape == wof_ref.shape and x1_ref.dtype == wof_ref.dtype
    _round_w_in(wf_ref, w_ref, wof_ref, wo_ref, x1_ref, w_sem)
    n_seq = x_ref.shape[0]
    seqs_per_tile = MOE_TILE // x_ref.shape[1]
    halves = []
    for first in (0, n_seq // 2):
        seqs = tuple(range(first, first + n_seq // 2))
        halves.append((seqs, (first * len(PROJ_SPLITS) // n_seq,),
                       tuple(range(first // seqs_per_tile, (first + n_seq // 2) // seqs_per_tile))))

    def inproj(sub_tiles):
        return _inproj_stages(sub_tiles, x_ref, mod_ref, g1_ref, qn_ref, kn_ref, bdq_ref, bdk_ref, wgk_ref, bgk_ref,
                              qa_ref, ka_ref, va_ref, qg_ref, kg_ref, vg_ref, la_ref, og_ref, w_ref)

    def mixers(seqs):
        return (_attn_stages(seqs, sinks_ref, qa_ref, ka_ref, va_ref, gatt_ref, ya_ref, kp_ref, vp_ref),
                _gla_stages(seqs, qg_ref, kg_ref, vg_ref, la_ref, og_ref, cum_ref, bdms_ref, smask_ref,
                            ggla_ref, yg_ref, state_ref))

    def outproj(tiles):
        return _outproj_stages(tiles, ya_ref, yg_ref, x_ref, mod_ref, wo_ref, g2_ref, wrt_ref, brt_ref, striu_ref,
                               x1_ref, row_ref, lpos_ref, cnt_ref)

    (seqs_a, subs_a, tiles_a), (seqs_b, subs_b, tiles_b) = halves
    _run_interleaved(inproj(subs_a + subs_b))

    @pl.when(pl.program_id(1) == 0)
    def _():
        state_ref[...] = jnp.zeros_like(state_ref)
        kp_ref[...] = jnp.zeros_like(kp_ref)
        vp_ref[...] = jnp.zeros_like(vp_ref)

    first_half = mixers(seqs_a)
    for stages in first_half:
        next(stages)
    _run_interleaved(*first_half, *mixers(seqs_b))
    _run_interleaved(outproj(tiles_a + tiles_b))


def _mix_out(sinks, x, mod, g1, w_in_t, qn, kn, bdq, bdk, wgk, bgk, gatt, cum, bdms, smask, ggla,
             wo, g2, wr, br, striu):
    assert GLA_CHUNK == WINDOW and MOE_TILE % WINDOW == 0
    B, T, D = x.shape
    L = WINDOW
    N = B * T
    assert B * L == PROJ_TILE
    tiles_per_step = B * L // MOE_TILE
    const = lambda a: pl.BlockSpec(a.shape, lambda b, i: (0,) * a.ndim)
    tiles = lambda *shape: pl.BlockSpec((tiles_per_step,) + shape, lambda b, i: (i,) + (0,) * len(shape))
    act = lambda c, dt=BF16: pltpu.VMEM((B, L, c), dt)
    return pl.pallas_call(
        _mix_out_kernel,
        grid=(1, T // L),
        in_specs=[pl.BlockSpec(memory_space=pltpu.SMEM),
                  pl.BlockSpec((B, L, D), lambda b, i: (0, i, 0)), const(mod), const(g1),
                  pl.BlockSpec(w_in_t.shape, lambda b, i: (0, 0), pipeline_mode=pl.Buffered(1)),
                  const(qn), const(kn), const(bdq), const(bdk), const(wgk), const(bgk),
                  const(gatt), const(cum), const(bdms), const(smask), const(ggla),
                  pl.BlockSpec(memory_space=pl.ANY), const(g2), const(wr), const(br), const(striu)],
        out_specs=[pl.BlockSpec((B * L, D), lambda b, i: (i, 0)),
                   pl.BlockSpec((tiles_per_step * LOCAL_ROWS, D + 2 * LANES), lambda b, i: (i, 0)),
                   tiles(8, MOE_TILE), tiles(8, LANES)],
        out_shape=[jax.ShapeDtypeStruct((N, D), F32),
                   jax.ShapeDtypeStruct((N // MOE_TILE * LOCAL_ROWS, D + 2 * LANES), BF16),
                   jax.ShapeDtypeStruct((N // MOE_TILE, 8, MOE_TILE), F32),
                   jax.ShapeDtypeStruct((N // MOE_TILE, 8, LANES), F32)],
        scratch_shapes=[pltpu.VMEM((IN_COLS_PAD, D), BF16), pltpu.VMEM(wo.shape, BF16),
                        act(ATT_Q), act(2 * ATT_KV), act(2 * ATT_KV), act(GLA_K), act(GLA_K), act(GLA_V),
                        act(GLA_K, F32), act(GLA_V),
                        act(ATT_Q), act(GLA_V), act(2 * ATT_KV), act(2 * ATT_KV),
                        pltpu.VMEM((B, GLA_HEADS // 2, LANES, 2 * GLA_DV), F32),
                        pltpu.SemaphoreType.DMA(())],
        compiler_params=_params("arbitrary", "arbitrary"),
        name="mix_out",
    )(sinks, x, mod, g1, w_in_t, qn, kn, bdq, bdk, wgk, bgk, gatt, cum, bdms, smask, ggla,
      wo, g2, wr, br, striu)


def _chunk_copy(src_ref, src_chunk, dst_ref, dst_chunk, sem):
    return pltpu.make_async_copy(src_ref.at[src_chunk], dst_ref.at[dst_chunk], sem)


def _moe_kernel(src_ref, dst_ref, grp_ref, next_ref, nt_ref, used_ref, rows_ref, w1f_ref, w3f_ref, w2f_ref, y_ref,
                in_buf, out_buf, zero_buf, st1_ref, st3_ref, st2_ref, w1_ref, w3_ref, w2_ref,
                in_sem, out_sem, zero_sem, w_sem):
    j = pl.program_id(0)
    n_tiles = nt_ref[0]
    d_model = w2_ref.shape[2]

    stages = ((w1f_ref, st1_ref, w1_ref), (w3f_ref, st3_ref, w3_ref), (w2f_ref, st2_ref, w2_ref))

    def fetch_weights(group):
        for hbm, stage, _ in stages:
            pltpu.make_async_copy(hbm.at[group], stage, w_sem).start()

    def enter_group():
        for hbm, stage, dst in stages:
            pltpu.make_async_copy(hbm.at[0], stage, w_sem).wait()
        for hbm, stage, dst in stages:
            for k in range(EXPERTS_PER_GROUP):
                dst[k] = stage[k].astype(BF16)

        @pl.when(next_ref[j] != grp_ref[j])
        def _():
            fetch_weights(next_ref[j])

    def gather(tile, slot):
        def body(k, carry):
            _chunk_copy(rows_ref, src_ref[tile * TILE_CHUNKS + k], in_buf.at[slot], k, in_sem.at[slot]).start()
            return carry
        lax.fori_loop(0, TILE_CHUNKS, body, 0, unroll=True)

    def wait_gather(slot):
        def body(k, carry):
            _chunk_copy(rows_ref, 0, in_buf.at[slot], k, in_sem.at[slot]).wait()
            return carry
        lax.fori_loop(0, TILE_CHUNKS, body, 0, unroll=True)

    def scatter(tile, slot):
        def body(k, carry):
            _chunk_copy(out_buf.at[slot], k, y_ref, dst_ref[tile * TILE_CHUNKS + k], out_sem.at[slot]).start()
            return carry
        lax.fori_loop(0, TILE_CHUNKS, body, 0, unroll=True)

    def wait_scatter(slot):
        def body(k, carry):
            _chunk_copy(out_buf.at[slot], k, y_ref, 0, out_sem.at[slot]).wait()
            return carry
        lax.fori_loop(0, TILE_CHUNKS, body, 0, unroll=True)

    def zero_fill(wait):
        def per_tile(i, carry):
            def body(c, inner):
                copy = _chunk_copy(zero_buf, 0, y_ref, i * LOCAL_CHUNKS + c, zero_sem)
                if wait:
                    copy.wait()
                else:
                    copy.start()
                return inner
            return lax.fori_loop(used_ref[i], LOCAL_CHUNKS, body, carry)
        lax.fori_loop(0, used_ref.shape[0], per_tile, 0)

    @pl.when(j == 0)
    def _():
        fetch_weights(grp_ref[0])
        zero_buf[...] = jnp.zeros_like(zero_buf)
        scratch0 = used_ref.shape[0] * LOCAL_CHUNKS
        for wait in (False, True):
            for k in range(2 * TILE_CHUNKS):
                copy = _chunk_copy(zero_buf, 0, y_ref, scratch0 + k, zero_sem)
                copy.wait() if wait else copy.start()
        zero_fill(wait=False)
        gather(0, 0)

    @pl.when(jnp.logical_or(j == 0, grp_ref[j] != grp_ref[jnp.maximum(j - 1, 0)]))
    def _():
        enter_group()

    @pl.when(j + 1 < n_tiles)
    def _():
        gather(j + 1, (j + 1) % 2)

    @pl.when(j < n_tiles)
    def _():
        slot = j % 2
        wait_gather(slot)
        rows = in_buf[slot].reshape(EXPERT_TILE, in_buf.shape[-1])
        h = rows[:, 0:d_model]
        weights = rows[:, d_model:d_model + LANES].astype(F32) + rows[:, d_model + LANES:].astype(F32)
        experts = range(EXPERTS_PER_GROUP)
        up = [(_dot(h, w1_ref[k]), _dot(h, w3_ref[k])) for k in experts]
        hid = [(a * _sigmoid(a) * g * weights[:, k:k + 1]).astype(BF16) for k, (a, g) in zip(experts, up)]
        y = _dot(hid[0], w2_ref[0])
        for k in experts[1:]:
            y = y + _dot(hid[k], w2_ref[k])

        @pl.when(j >= 2)
        def _():
            wait_scatter(slot)

        out_buf[slot] = y.astype(BF16).reshape(TILE_CHUNKS, CHUNK, d_model)
        scatter(j, slot)

        @pl.when(j == n_tiles - 1)
        def _():
            @pl.when(j >= 1)
            def _():
                wait_scatter(1 - slot)
            wait_scatter(slot)
            zero_fill(wait=True)


def _combine_kernel(x1_ref, mod_ref, lpos_ref, y_ref, o_ref):
    tm = MOE_TILE
    half = WINDOW
    per_tile = tm // half
    tiles_per_block = o_ref.shape[0] // per_tile
    local_row = lax.broadcasted_iota(jnp.int32, (tm, LOCAL_ROWS), 1).astype(F32)
    for t in range(x1_ref.shape[0] // tm):
        lpos = jnp.broadcast_to(lpos_ref[t][0:1, :], (LANES, tm)).T[:, 0:1]
        unsort = jnp.where(local_row == lpos, 1.0, 0.0).astype(BF16)
        y = _dot(unsort, y_ref[t * LOCAL_ROWS:(t + 1) * LOCAL_ROWS, :])
        block, tile = divmod(t, tiles_per_block)
        for i in range(per_tile):
            s = tile * per_tile + i
            rows = slice(t * tm + i * half, t * tm + (i + 1) * half)
            o_ref[s, block * half:(block + 1) * half, :] = (
                x1_ref[rows, :] + mod_ref[s, 5:6, :] * y[i * half:(i + 1) * half, :])


def _moe_plan(cnt):
    n_local = cnt.shape[0]
    chunks = (cnt + CHUNK - 1) // CHUNK
    used = jnp.sum(chunks, axis=1)
    local_off = jnp.cumsum(chunks, axis=1) - chunks
    tiles_g = (jnp.sum(chunks, axis=0) + TILE_CHUNKS - 1) // TILE_CHUNKS
    tile_end = jnp.cumsum(tiles_g)
    n_tiles = tile_end[-1]
    group_start = (tile_end - tiles_g) * TILE_CHUNKS
    seg_len = chunks.T.reshape(-1)
    seg_start = (group_start[:, None] + (jnp.cumsum(chunks, axis=0) - chunks).T).reshape(-1)
    seg_src = (jnp.arange(n_local)[None, :] * LOCAL_CHUNKS + local_off.T).reshape(-1)
    max_chunks = n_local * MOE_TILE // CHUNK + n_local * N_GROUPS + N_GROUPS * TILE_CHUNKS
    max_tiles = (max_chunks + TILE_CHUNKS - 1) // TILE_CHUNKS
    c = jnp.arange(max_tiles * TILE_CHUNKS)[:, None]
    within = c - seg_start[None, :]
    hit = jnp.logical_and(within >= 0, within < seg_len[None, :])
    valid = jnp.any(hit, axis=1)
    src = jnp.sum(jnp.where(hit, seg_src[None, :] + within, 0), axis=1)
    src = jnp.where(valid, src, LOCAL_CHUNKS - 1)
    slot_k = c[:, 0] % (2 * TILE_CHUNKS)
    dst = jnp.where(valid, src, n_local * LOCAL_CHUNKS + slot_k)
    j = jnp.minimum(jnp.arange(max_tiles), n_tiles - 1)
    grp = jnp.sum(j[:, None] >= tile_end[None, :], axis=1)
    gid = jnp.arange(N_GROUPS)
    later = jnp.where(jnp.logical_and(gid[None, :] > gid[:, None], tiles_g[None, :] > 0), gid[None, :], N_GROUPS)
    next_of = jnp.min(later, axis=1)
    next_of = jnp.where(next_of == N_GROUPS, gid, next_of)
    next_grp = jnp.sum(jnp.where(grp[:, None] == gid[None, :], next_of[None, :], 0), axis=1)
    i32 = lambda a: a.astype(jnp.int32)
    return i32(src), i32(dst), i32(grp), i32(next_grp), i32(n_tiles).reshape(1), i32(used)


def _moe(plan, rows_local, w1g, w3g, w2g):
    src, dst, grp, next_grp, n_tiles, used = plan
    D = w2g.shape[3]
    n_rows, cols = rows_local.shape
    n_chunks = n_rows // CHUNK
    hbm = pl.BlockSpec(memory_space=pl.ANY)
    group_weights = [w1g, w3g, w2g]
    y = pl.pallas_call(
        _moe_kernel,
        grid_spec=pltpu.PrefetchScalarGridSpec(
            num_scalar_prefetch=6,
            grid=(grp.shape[0],),
            in_specs=[hbm, hbm, hbm, hbm],
            out_specs=hbm,
            scratch_shapes=([pltpu.VMEM((2, TILE_CHUNKS, CHUNK, cols), BF16),
                             pltpu.VMEM((2, TILE_CHUNKS, CHUNK, D), BF16),
                             pltpu.VMEM((1, CHUNK, D), BF16)]
                            + [pltpu.VMEM(w.shape[1:], F32) for w in group_weights]
                            + [pltpu.VMEM(w.shape[1:], BF16) for w in group_weights]
                            + [pltpu.SemaphoreType.DMA((2,)), pltpu.SemaphoreType.DMA((2,)),
                               pltpu.SemaphoreType.DMA(()), pltpu.SemaphoreType.DMA(())])),
        out_shape=jax.ShapeDtypeStruct((n_chunks + 2 * TILE_CHUNKS, CHUNK, D), BF16),
        compiler_params=_params("arbitrary"),
        name="moe",
    )(src, dst, grp, next_grp, n_tiles, used, rows_local.reshape(n_chunks, CHUNK, cols), w1g, w3g, w2g)
    return y.reshape((n_chunks + 2 * TILE_CHUNKS) * CHUNK, D)


def _combine(x1, mod, lpos, y_local, *, B, T):
    N, D = x1.shape
    L = COMBINE_BLOCKS * WINDOW
    subs = B * L // MOE_TILE
    return pl.pallas_call(
        _combine_kernel,
        grid=(T // L,),
        in_specs=[pl.BlockSpec((B * L, D), lambda i: (i, 0)),
                  pl.BlockSpec(mod.shape, lambda i: (0, 0, 0)),
                  pl.BlockSpec((subs, 8, MOE_TILE), lambda i: (i, 0, 0)),
                  pl.BlockSpec((subs * LOCAL_ROWS, D), lambda i: (i, 0))],
        out_specs=pl.BlockSpec((B, L, D), lambda i: (0, i, 0)),
        out_shape=jax.ShapeDtypeStruct((B, T, D), F32),
        compiler_params=_params("arbitrary"),
        name="moe_combine",
    )(x1, mod, lpos, y_local)


def _block_diag(n, blk, value, dtype):
    r = np.arange(n)[:, None] // blk
    c = np.arange(n)[None, :] // blk
    return jnp.asarray(np.where(r == c, value, 0.0), dtype)


def _gla_constants():
    L = GLA_CHUNK
    i = np.arange(L)[:, None]
    j = np.arange(L)[None, :]
    tri = j <= i
    bdtri = np.logical_and(j <= i, i // GLA_SUB == j // GLA_SUB)
    cum = np.block([[tri, tri], [bdtri, bdtri]])
    bdms = _block_diag(2 * GLA_DV, GLA_DV, 1.0 / GLA_DV, BF16)
    d = np.arange(LANES)[:, None] // GLA_DK
    e = np.arange(2 * GLA_DV)[None, :] // GLA_DV
    smask = d == e
    return jnp.asarray(cum, BF16), bdms, jnp.asarray(smask, F32)


def kernel(x, c, w_ada, b_ada, g_norm1, w_in, q_norm, k_norm, sinks, w_gk2, b_gk, g_gla_out, g_att_out,
           w_out, g_norm2, w_group, b_group, w_router, b_router, w1, w3, w2):
    B, T, D = x.shape
    N = B * T
    depth = w_ada.shape[0]
    cum, bdms, smask = _gla_constants()
    bdq = _block_diag(2 * LANES, HEAD_DIM, 1.0 / HEAD_DIM, BF16)
    bdk = _block_diag(ATT_KV, HEAD_DIM, 1.0 / HEAD_DIM, BF16)

    for l in range(depth):
        mod = _adaln_mod(c, w_ada[l], b_ada[l]).reshape(B, 6, D)

        wgk = jnp.concatenate([w_gk2[l], jnp.zeros((LANES - GLA_RANK, GLA_K), F32)], axis=0).astype(BF16)
        pad = LANES - N_GROUPS - N_EXPERTS
        wr_t = jnp.concatenate([w_group[l], w_router[l], jnp.zeros((D, pad), F32)], axis=1).T.astype(BF16)
        br_t = jnp.concatenate([b_group[l], b_router[l], jnp.zeros((pad,), F32)]).reshape(LANES, 1)
        striu = jnp.asarray(np.arange(MOE_TILE)[:, None] < np.arange(MOE_TILE)[None, :], BF16)
        x1, rows_local, lpos, cnt = _mix_out(
            sinks[l], x, mod, g_norm1[l].reshape(1, D), w_in[l].T,
            q_norm[l].reshape(1, HEAD_DIM), k_norm[l].reshape(1, HEAD_DIM),
            bdq, bdk, wgk, b_gk[l].reshape(1, GLA_K),
            g_att_out[l].reshape(1, ATT_Q), cum, bdms, smask, g_gla_out[l].reshape(1, GLA_DV),
            w_out[l], g_norm2[l].reshape(1, D), wr_t, br_t, striu)
        plan = _moe_plan(cnt[:, :N_GROUPS, 0].astype(jnp.int32))
        by_group = lambda w: w.reshape((N_GROUPS, EXPERTS_PER_GROUP) + w.shape[1:])
        y_local = _moe(plan, rows_local, by_group(w1[l]), by_group(w3[l]), by_group(w2[l]))
        x = _combine(x1, mod, lpos, y_local, B=B, T=T)
    return x
```

```python
import jax
import jax.numpy as jnp
import numpy as np
from jax import lax
from jax.experimental import pallas as pl
from jax.experimental.pallas import tpu as pltpu

F32 = jnp.float32
BF16 = jnp.bfloat16

EPS = 1e-6
LOG2_E = 1.4426950408889634
ATT_HEADS = 8
ATT_KV_HEADS = 2
HEAD_DIM = 64
WINDOW = 128
ATT_Q = ATT_HEADS * HEAD_DIM
ATT_KV = ATT_KV_HEADS * HEAD_DIM
GLA_HEADS = 4
GLA_DK = 64
GLA_DV = 128
GLA_RANK = 16
GLA_NORMALIZER = 16.0
GLA_K = GLA_HEADS * GLA_DK
GLA_V = GLA_HEADS * GLA_DV
N_GROUPS = 4
EXPERTS_PER_GROUP = 4
N_EXPERTS = N_GROUPS * EXPERTS_PER_GROUP

LANES = 128
PROJ_SPLITS = (512, 512)
PROJ_TILE = sum(PROJ_SPLITS)
COMBINE_BLOCKS = 2
ROUTER_ROWS = 24
GLA_CHUNK = 128
GLA_SUB = 16
N_SUB = GLA_CHUNK // GLA_SUB
ROUTER_LANE0 = N_GROUPS
VMEM_LIMIT = 62 * 1024 * 1024
MOE_TILE = 256
EXPERT_TILE = 512
CHUNK = 16
TILE_CHUNKS = EXPERT_TILE // CHUNK
LOCAL_CHUNKS = (MOE_TILE + N_GROUPS * (CHUNK - 1)) // CHUNK + 2
LOCAL_ROWS = LOCAL_CHUNKS * CHUNK

_QA0, _KA0, _VA0 = 0, ATT_Q, ATT_Q + ATT_KV
_QG0 = _VA0 + ATT_KV
_KG0 = _QG0 + GLA_K
_VG0 = _KG0 + GLA_K
_OG0 = _VG0 + GLA_V
_LR0 = _OG0 + GLA_V
IN_COLS_PAD = _LR0 + LANES


def _dot(a, b):
    return jnp.dot(a, b, preferred_element_type=F32)


def _dot_nt(a, b):
    return lax.dot_general(a, b, (((1,), (1,)), ((), ())), preferred_element_type=F32)


def _sigmoid(x):
    return 1.0 / (1.0 + jnp.exp(-x))


def _params(*sem):
    return pltpu.CompilerParams(dimension_semantics=sem, vmem_limit_bytes=VMEM_LIMIT)


def _mod_kernel(c_ref, w_ref, b_ref, o_ref):
    c = c_ref[...]
    s = (c * _sigmoid(c)).astype(BF16)
    o_ref[0] = _dot(s, w_ref[...].astype(BF16)) + b_ref[...]


def _adaln_mod(c, w_ada, b_ada):
    B, D = c.shape
    n = w_ada.shape[1]
    tn = D
    return pl.pallas_call(
        _mod_kernel,
        grid=(n // tn,),
        in_specs=[pl.BlockSpec((B, D), lambda j: (0, 0)),
                  pl.BlockSpec((D, tn), lambda j: (0, j)),
                  pl.BlockSpec((1, tn), lambda j: (0, j))],
        out_specs=pl.BlockSpec((1, B, tn), lambda j: (j, 0, 0)),
        out_shape=jax.ShapeDtypeStruct((n // tn, B, tn), F32),
        compiler_params=_params("arbitrary"),
        name="adaln_mod",
    )(c, w_ada, b_ada.reshape(1, n))


def _round_w_in(wf_ref, w_ref, wof_ref, wo_ref, stage_ref, sem):
    @pl.when(pl.program_id(1) == 0)
    def _():
        fetch = pltpu.make_async_copy(wof_ref, stage_ref, sem)
        fetch.start()
        lr_src = _OG0
        w_ref[0:_OG0, :] = wf_ref[0:_OG0, :].astype(BF16)
        w_ref[_OG0:_LR0, :] = wf_ref[lr_src + GLA_RANK:lr_src + GLA_RANK + GLA_V, :].astype(BF16)
        w_ref[_LR0:_LR0 + GLA_RANK, :] = wf_ref[lr_src:lr_src + GLA_RANK, :].astype(BF16)
        w_ref[_LR0 + GLA_RANK:IN_COLS_PAD, :] = jnp.zeros((LANES - GLA_RANK, wf_ref.shape[1]), BF16)
        fetch.wait()
        wo_ref[...] = stage_ref[...].astype(BF16)


def _inproj_stages(subs, x_ref, mod_ref, g1_ref, qn_ref, kn_ref, bdq_ref, bdk_ref, wgk_ref, bgk_ref,
                   qa_ref, ka_ref, va_ref, qg_ref, kg_ref, vg_ref, la_ref, og_ref, w_ref):
    blk = x_ref.shape[1]
    seqs_of = [range(sum(PROJ_SPLITS[:t]) // blk, sum(PROJ_SPLITS[:t + 1]) // blk)
               for t in range(len(PROJ_SPLITS))]

    def put(ref, t, value, lanes=slice(None)):
        for i, s in enumerate(seqs_of[t]):
            ref[s, :, lanes] = value[i * blk:(i + 1) * blk, :]

    h = {}
    for t in subs:
        parts = []
        for s in seqs_of[t]:
            x = x_ref[s]
            ms = jnp.mean(x * x, axis=-1, keepdims=True)
            xn = x * lax.rsqrt(ms + EPS) * g1_ref[...]
            parts.append((xn * (1.0 + mod_ref[1, s:s + 1, :]) + mod_ref[0, s:s + 1, :]).astype(BF16))
        h[t] = jnp.concatenate(parts, axis=0)
    yield

    proj = lambda t, c0, width: _dot_nt(h[t], w_ref[c0:c0 + width, :])
    gate_of = lambda lr: _dot(lr.astype(BF16), wgk_ref[...]) + bgk_ref[...]
    qa, kv, q_ms, k_ms, qk_g, vg, og, lr, gate = ({} for _ in range(9))
    prev = None
    for t in subs:
        qa[t] = proj(t, _QA0, ATT_Q)
        if prev is not None:
            gate[prev] = gate_of(lr[prev])
        prev = t
        yield
        kv[t] = proj(t, _KA0, 2 * ATT_KV)
        q_sq = (qa[t] * qa[t]).astype(BF16)
        q_ms[t] = jnp.concatenate([_dot(q_sq[:, c:c + 2 * LANES], bdq_ref[...])
                                   for c in range(0, ATT_Q, 2 * LANES)], axis=1)
        yield
        qk_g[t] = proj(t, _QG0, 2 * GLA_K)
        k = kv[t][:, 0:ATT_KV]
        k_ms[t] = _dot((k * k).astype(BF16), bdk_ref[...])
        yield
        vg[t] = proj(t, _VG0, GLA_V)
        yield
        og[t] = proj(t, _OG0, GLA_V)
        lr[t] = proj(t, _LR0, LANES)
        yield
    gate[prev] = gate_of(lr[prev])
    yield

    for t in subs:
        low = lax.broadcasted_iota(jnp.int32, (PROJ_SPLITS[t], ATT_KV), 1) < HEAD_DIM
        qn = jnp.concatenate([qn_ref[...]] * ATT_HEADS, axis=1)
        kn = jnp.concatenate([kn_ref[...]] * ATT_KV_HEADS, axis=1)
        put(qa_ref, t, (qa[t] * lax.rsqrt(q_ms[t] + EPS) * qn * (HEAD_DIM ** -0.5 * LOG2_E)).astype(BF16))
        k = kv[t][:, 0:ATT_KV] * lax.rsqrt(k_ms[t] + EPS) * kn
        v = kv[t][:, ATT_KV:2 * ATT_KV]
        for src, dst in ((k, ka_ref), (v, va_ref)):
            swapped = pltpu.roll(src, HEAD_DIM, axis=1)
            put(dst, t, jnp.where(low, src, swapped).astype(BF16), slice(0, LANES))
            put(dst, t, jnp.where(low, swapped, src).astype(BF16), slice(LANES, 2 * LANES))
        put(qg_ref, t, (qk_g[t][:, 0:GLA_K] * (GLA_DK ** -0.5)).astype(BF16))
        put(kg_ref, t, qk_g[t][:, GLA_K:2 * GLA_K].astype(BF16))
        put(vg_ref, t, vg[t].astype(BF16))
        put(og_ref, t, (og[t] * _sigmoid(og[t])).astype(BF16))
        log_sig = jnp.minimum(gate[t], 0.0) - jnp.log(1.0 + jnp.exp(-jnp.abs(gate[t])))
        put(la_ref, t, log_sig * (1.0 / GLA_NORMALIZER))


def _attn_stages(seqs, sinks_ref, q_ref, kc_ref, vc_ref, gatt_ref, o_ref, kp_ref, vp_ref):
    blk = WINDOW
    first = pl.program_id(1) == 0
    qi = lax.broadcasted_iota(jnp.int32, (blk, blk), 0)
    cj = lax.broadcasted_iota(jnp.int32, (blk, blk), 1)
    from_prev = cj > qi
    dist = (qi - cj + jnp.where(from_prev, blk, 0)).astype(F32)
    no_prev = jnp.where(jnp.logical_and(from_prev, first), -1e30, 0.0)
    low = cj < HEAD_DIM
    half = (jnp.where(low, 1.0, 0.0).astype(BF16), jnp.where(low, 0.0, 1.0).astype(BF16))
    half2 = tuple(jnp.concatenate([m, m], axis=0) for m in half)
    prev_mask = jnp.where(from_prev, 1.0, 0.0).astype(BF16)
    cur_mask = jnp.where(from_prev, 0.0, 1.0).astype(BF16)

    n_pairs = ATT_HEADS // 2
    pairs_per_kv = n_pairs // ATT_KV_HEADS
    units = [(bi, j) for bi in seqs for j in range(n_pairs)]

    def kv_blocks(bi, g):
        lanes = slice(g * LANES, (g + 1) * LANES)
        return (kp_ref[bi, :, lanes], vp_ref[bi, :, lanes]), (kc_ref[bi, :, lanes], vc_ref[bi, :, lanes])

    scores = {}
    for bi in seqs:
        for g in range(ATT_KV_HEADS):
            (kp, _), (kc, _) = kv_blocks(bi, g)
            k_both = jnp.concatenate([kp, kc], axis=0)
            group = range(g * pairs_per_kv, (g + 1) * pairs_per_kv)
            q_stack = jnp.concatenate([q_ref[bi, :, j * LANES:(j + 1) * LANES] for j in group], axis=0)
            for p in range(2):
                s_stack = _dot_nt(q_stack, k_both * half2[p])
                for jj, j in enumerate(group):
                    scores[bi, j, p] = s_stack[jj * blk:(jj + 1) * blk, :]
    yield

    bias = [2.0 ** (-8.0 * (h + 1) / ATT_HEADS) * LOG2_E * dist - no_prev for h in range(ATT_HEADS)]
    probs, sink_terms = {}, {}
    for bi, j in units:
        for p in range(2):
            h = 2 * j + p
            s_both = scores[bi, j, p]
            s = jnp.where(from_prev, s_both[:, 0:blk], s_both[:, blk:2 * blk]) - bias[h]
            sink = sinks_ref[h] * LOG2_E
            m = jnp.maximum(jnp.max(s, axis=-1, keepdims=True), sink)
            probs[bi, j, p] = jnp.exp2(s - m)
            sink_terms[bi, j, p] = jnp.exp2(sink - m)
    yield

    outs = {}
    for bi in seqs:
        for g in range(ATT_KV_HEADS):
            (_, vp), (_, vc) = kv_blocks(bi, g)
            v_stack = jnp.concatenate([jnp.concatenate([v * half[p], half[p]], axis=1)
                                       for v in (vp, vc) for p in range(2)], axis=0)
            group = range(g * pairs_per_kv, (g + 1) * pairs_per_kv)
            p_stack = []
            for j in group:
                e = [probs[bi, j, p].astype(BF16) for p in range(2)]
                p_stack.append(jnp.concatenate([x * m for m in (prev_mask, cur_mask) for x in e], axis=1))
            pv_stack = _dot(jnp.concatenate(p_stack, axis=0), v_stack)
            for jj, j in enumerate(group):
                pv = pv_stack[jj * blk:(jj + 1) * blk, :]
                den = pv[:, LANES:2 * LANES] + jnp.where(low, sink_terms[bi, j, 0], sink_terms[bi, j, 1])
                outs[bi, j] = pv[:, 0:LANES] / den
    yield

    for bi in seqs:
        o = jnp.concatenate([outs[bi, j] for j in range(n_pairs)], axis=1)
        ms = jnp.mean(o * o, axis=-1, keepdims=True)
        o_ref[bi] = (o * lax.rsqrt(ms + EPS) * gatt_ref[...]).astype(BF16)
        kp_ref[bi] = kc_ref[bi]
        vp_ref[bi] = vc_ref[bi]


def _gla_stages(seqs, q_ref, k_ref, v_ref, la_ref, og_ref, cum_ref, bdms_ref, smask_ref,
                ggla_ref, o_ref, state_ref):
    L = GLA_CHUNK
    lane = lax.broadcasted_iota(jnp.int32, (GLA_SUB, LANES), 1)
    head_mask = [jnp.where(lane // GLA_DK == hh, 1.0, 0.0).astype(BF16) for hh in range(2)]
    causal = lax.broadcasted_iota(jnp.int32, (L, L), 0) >= lax.broadcasted_iota(jnp.int32, (L, L), 1)
    units = [(s, pair) for s in seqs for pair in range(GLA_HEADS // 2)]
    kl = lambda pair: slice(pair * LANES, (pair + 1) * LANES)
    vl = lambda pair: slice(pair * 2 * GLA_DV, (pair + 1) * 2 * GLA_DV)

    b, b_in = {}, {}
    for s in seqs:
        la = la_ref[s]
        la_hi = la.astype(BF16)
        la_lo = (la - la_hi.astype(F32)).astype(BF16)
        sums = _dot(cum_ref[...], jnp.concatenate([la_hi, la_lo], axis=0)) * LOG2_E
        b[s] = sums[0:L, :]
        b_in[s] = sums[L:2 * L, :]
    yield

    q_both, keys, q_dec, k_dec_t, b_last = {}, {}, {}, {}, {}
    for s in seqs:
        ref = b[s] - b_in[s]
        b_last[s] = b[s][L - 1:L, :]
        q = q_ref[s].astype(F32)
        k = k_ref[s].astype(F32)
        q_in = (q * jnp.exp2(b_in[s])).astype(BF16)
        q_dec[s] = (q * jnp.exp2(b[s])).astype(BF16)
        k_dec = k * jnp.exp2(b_last[s] - b[s])
        for pair in range(GLA_HEADS // 2):
            k_p, b_p, ref_p = k[:, kl(pair)], b[s][:, kl(pair)], ref[:, kl(pair)]
            expanded = []
            for g in range(N_SUB):
                top = (g + 1) * GLA_SUB
                live = (k_p[0:top, :] * jnp.exp2(ref_p[g * GLA_SUB:g * GLA_SUB + 1, :] - b_p[0:top, :])).astype(BF16)
                expanded.append(live if top == L else
                                jnp.concatenate([live, jnp.zeros((L - top, LANES), BF16)], axis=0))
            keys[s, pair] = jnp.concatenate(expanded, axis=1)
            q_p = q_in[:, kl(pair)]
            zero_group = jnp.zeros((GLA_SUB, LANES), BF16)
            q_both[s, pair] = []
            for t in range(N_SUB // 2):
                lhs = []
                for hh in range(2):
                    for side in range(2):
                        g = 2 * t + side
                        piece = q_p[g * GLA_SUB:(g + 1) * GLA_SUB, :] * head_mask[hh]
                        lhs.append(jnp.concatenate([piece, zero_group] if side == 0 else [zero_group, piece], axis=1))
                q_both[s, pair].append(jnp.concatenate(lhs, axis=0))
            k_dec_t[s, pair] = k_dec[:, kl(pair)].T.astype(BF16)
    yield

    scores = {(u, t): _dot_nt(q_both[u][t], keys[u][:, 2 * t * LANES:2 * (t + 1) * LANES])
              for u in units for t in range(N_SUB // 2)}
    yield

    outs, updates = {}, {}
    for s, pair in units:
        v_p = v_ref[s, :, vl(pair)]
        o_parts = []
        for hh in range(2):
            a = jnp.concatenate([scores[(s, pair), t][2 * hh * GLA_SUB:2 * (hh + 1) * GLA_SUB, :]
                                 for t in range(N_SUB // 2)], axis=0)
            a = jnp.where(causal, a, 0.0).astype(BF16)
            o_parts.append(_dot(a, v_p[:, hh * GLA_DV:(hh + 1) * GLA_DV]))
        state = state_ref[s, pair]
        outs[s, pair] = jnp.concatenate(o_parts, axis=1) + _dot(q_dec[s][:, kl(pair)], state.astype(BF16))
        updates[s, pair] = _dot(k_dec_t[s, pair], v_p)
    yield

    for s, pair in units:
        decay = jnp.broadcast_to(jnp.exp2(b_last[s][:, kl(pair)]), (LANES, LANES)).T
        state_ref[s, pair] = (state_ref[s, pair] * jnp.concatenate([decay, decay], axis=1)
                              + updates[s, pair] * smask_ref[...])
        o = outs[s, pair]
        ms = _dot((o * o).astype(BF16), bdms_ref[...])
        gain = jnp.concatenate([ggla_ref[...]] * 2, axis=1)
        y = o * lax.rsqrt(ms + EPS) * gain * og_ref[s, :, vl(pair)].astype(F32)
        o_ref[s, :, vl(pair)] = y.astype(BF16)


def _run_interleaved(*stage_generators):
    pending = list(stage_generators)
    while pending:
        for stages in list(pending):
            if next(stages, "done") == "done":
                pending.remove(stages)


def _route(logits_t):
    lt = logits_t[0:ROUTER_ROWS, :]
    row = lax.broadcasted_iota(jnp.int32, lt.shape, 0)
    neg_inf = -jnp.inf
    g_log = jnp.where(row < N_GROUPS, lt, neg_inf)
    g_max = jnp.max(g_log, axis=0, keepdims=True)
    g_sel = jnp.min(jnp.where(g_log == g_max, row, LANES), axis=0, keepdims=True)
    g_sum = jnp.sum(jnp.where(row < N_GROUPS, jnp.exp(lt - g_max), 0.0), axis=0, keepdims=True)
    p_group = 1.0 / g_sum
    e_lo = ROUTER_LANE0 + EXPERTS_PER_GROUP * g_sel
    in_group = jnp.logical_and(row >= e_lo, row < e_lo + EXPERTS_PER_GROUP)
    e_log = jnp.where(in_group, lt, neg_inf)
    e_max = jnp.max(e_log, axis=0, keepdims=True)
    top1 = jnp.min(jnp.where(e_log == e_max, row, LANES), axis=0, keepdims=True)
    e_log2 = jnp.where(row == top1, neg_inf, e_log)
    e_max2 = jnp.max(e_log2, axis=0, keepdims=True)
    top2 = jnp.min(jnp.where(e_log2 == e_max2, row, LANES), axis=0, keepdims=True)
    ratio = jnp.exp(e_max2 - e_max)
    w_top1 = p_group / (1.0 + ratio)
    w_top2 = p_group * ratio / (1.0 + ratio)
    row8 = lax.broadcasted_iota(jnp.int32, (8, lt.shape[1]), 0)
    weights = jnp.where(row8 == top1 - e_lo, w_top1, 0.0) + jnp.where(row8 == top2 - e_lo, w_top2, 0.0)
    return g_sel, weights


def _outproj_stages(subs, ya_ref, yg_ref, x_ref, mod_ref, wo_ref, g2_ref, wrt_ref, brt_ref, striu_ref,
                    x1_ref, row_ref, lpos_ref, cnt_ref):
    tm = MOE_TILE
    half = x_ref.shape[1]
    per_tile = tm // half
    tile_of = lambda ref, t: jnp.concatenate([ref[t * per_tile + i] for i in range(per_tile)], axis=0)

    mix = {t: _dot(tile_of(ya_ref, t), wo_ref[0:ATT_Q, :]) + _dot(tile_of(yg_ref, t), wo_ref[ATT_Q:ATT_Q + GLA_V, :])
           for t in subs}
    yield
    h2b = {}
    for t in subs:
        parts = []
        for i in range(per_tile):
            s = t * per_tile + i
            x1 = x_ref[s] + mod_ref[2, s:s + 1, :] * mix[t][i * half:(i + 1) * half, :]
            x1_ref[t * tm + i * half:t * tm + (i + 1) * half, :] = x1
            ms = jnp.mean(x1 * x1, axis=-1, keepdims=True)
            h2 = (x1 * lax.rsqrt(ms + EPS) * g2_ref[...]) * (1.0 + mod_ref[4, s:s + 1, :]) + mod_ref[3, s:s + 1, :]
            parts.append(h2.astype(BF16))
        h2b[t] = jnp.concatenate(parts, axis=0)
    yield
    logits_t = {t: _dot_nt(wrt_ref[...], h2b[t]) + brt_ref[...] for t in subs}
    yield

    routed = {t: _route(logits_t[t]) for t in subs}
    row8 = lax.broadcasted_iota(jnp.int32, (8, tm), 0)
    onehot = {t: jnp.where(row8 == routed[t][0], 1.0, 0.0) for t in subs}
    yield
    before = {t: _dot(onehot[t].astype(BF16), striu_ref[...]) for t in subs}
    yield

    local_row = lax.broadcasted_iota(jnp.int32, (LOCAL_ROWS, tm), 0).astype(F32)
    pad_rows = jnp.zeros((LANES - 8, tm), F32)
    for t in subs:
        count = jnp.sum(onehot[t], axis=1, keepdims=True)
        cnt_ref[t] = jnp.broadcast_to(count, (8, LANES))
        padded = jnp.broadcast_to(jnp.floor((count + (CHUNK - 1.0)) * (1.0 / CHUNK)) * CHUNK, (8, tm))
        start = jnp.zeros((8, tm), F32)
        for shift in range(1, N_GROUPS):
            start = start + jnp.where(row8 >= shift, pltpu.roll(padded, shift, axis=0), 0.0)
        lpos = jnp.sum(onehot[t] * (before[t] + start), axis=0, keepdims=True)
        lpos_ref[t] = jnp.broadcast_to(lpos, (8, tm))
        weights = jnp.concatenate([routed[t][1], pad_rows], axis=0).T
        w_hi = weights.astype(BF16)
        w_lo = (weights - w_hi.astype(F32)).astype(BF16)
        perm = jnp.where(local_row == lpos, 1.0, 0.0).astype(BF16)
        row_ref[t * LOCAL_ROWS:(t + 1) * LOCAL_ROWS, :] = _dot(
            perm, jnp.concatenate([h2b[t], w_hi, w_lo], axis=1)).astype(BF16)


def _mix_out_kernel(sinks_ref, x_ref, mod_ref, g1_ref, wf_ref, qn_ref, kn_ref, bdq_ref, bdk_ref, wgk_ref, bgk_ref,
                    gatt_ref, cum_ref, bdms_ref, smask_ref, ggla_ref,
                    wof_ref, g2_ref, wrt_ref, brt_ref, striu_ref,
                    x1_ref, row_ref, lpos_ref, cnt_ref,
                    w_ref, wo_ref, qa_ref, ka_ref, va_ref, qg_ref, kg_ref, vg_ref, la_ref, og_ref,
                    ya_ref, yg_ref, kp_ref, vp_ref, state_ref, w_sem):
    assert x1_ref.shape == wof_ref.shape and x1_ref.dtype == wof_ref.dtype
    _round_w_in(wf_ref, w_ref, wof_ref, wo_ref, x1_ref, w_sem)
    n_seq = x_ref.shape[0]
    seqs_per_tile = MOE_TILE // x_ref.shape[1]
    halves = []
    for first in (0, n_seq // 2):
        seqs = tuple(range(first, first + n_seq // 2))
        halves.append((seqs, (first * len(PROJ_SPLITS) // n_seq,),
                       tuple(range(first // seqs_per_tile, (first + n_seq // 2) // seqs_per_tile))))

    def inproj(sub_tiles):
        return _inproj_stages(sub_tiles, x_ref, mod_ref, g1_ref, qn_ref, kn_ref, bdq_ref, bdk_ref, wgk_ref, bgk_ref,
                              qa_ref, ka_ref, va_ref, qg_ref, kg_ref, vg_ref, la_ref, og_ref, w_ref)

    def mixers(seqs):
        return (_attn_stages(seqs, sinks_ref, qa_ref, ka_ref, va_ref, gatt_ref, ya_ref, kp_ref, vp_ref),
                _gla_stages(seqs, qg_ref, kg_ref, vg_ref, la_ref, og_ref, cum_ref, bdms_ref, smask_ref,
                            ggla_ref, yg_ref, state_ref))

    def outproj(tiles):
        return _outproj_stages(tiles, ya_ref, yg_ref, x_ref, mod_ref, wo_ref, g2_ref, wrt_ref, brt_ref, striu_ref,
                               x1_ref, row_ref, lpos_ref, cnt_ref)

    (seqs_a, subs_a, tiles_a), (seqs_b, subs_b, tiles_b) = halves
    _run_interleaved(inproj(subs_a + subs_b))

    @pl.when(pl.program_id(1) == 0)
    def _():
        state_ref[...] = jnp.zeros_like(state_ref)
        kp_ref[...] = jnp.zeros_like(kp_ref)
        vp_ref[...] = jnp.zeros_like(vp_ref)

    first_half = mixers(seqs_a)
    for stages in first_half:
        next(stages)
    _run_interleaved(*first_half, *mixers(seqs_b))
    _run_interleaved(outproj(tiles_a + tiles_b))


def _mix_out(sinks, x, mod, g1, w_in_t, qn, kn, bdq, bdk, wgk, bgk, gatt, cum, bdms, smask, ggla,
             wo, g2, wr, br, striu):
    assert GLA_CHUNK == WINDOW and MOE_TILE % WINDOW == 0
    B, T, D = x.shape
    L = WINDOW
    N = B * T
    assert B * L == PROJ_TILE
    tiles_per_step = B * L // MOE_TILE
    const = lambda a: pl.BlockSpec(a.shape, lambda b, i: (0,) * a.ndim)
    tiles = lambda *shape: pl.BlockSpec((tiles_per_step,) + shape, lambda b, i: (i,) + (0,) * len(shape))
    act = lambda c, dt=BF16: pltpu.VMEM((B, L, c), dt)
    return pl.pallas_call(
        _mix_out_kernel,
        grid=(1, T // L),
        in_specs=[pl.BlockSpec(memory_space=pltpu.SMEM),
                  pl.BlockSpec((B, L, D), lambda b, i: (0, i, 0)), const(mod), const(g1),
                  pl.BlockSpec(w_in_t.shape, lambda b, i: (0, 0), pipeline_mode=pl.Buffered(1)),
                  const(qn), const(kn), const(bdq), const(bdk), const(wgk), const(bgk),
                  const(gatt), const(cum), const(bdms), const(smask), const(ggla),
                  pl.BlockSpec(memory_space=pl.ANY), const(g2), const(wr), const(br), const(striu)],
        out_specs=[pl.BlockSpec((B * L, D), lambda b, i: (i, 0)),
                   pl.BlockSpec((tiles_per_step * LOCAL_ROWS, D + 2 * LANES), lambda b, i: (i, 0)),
                   tiles(8, MOE_TILE), tiles(8, LANES)],
        out_shape=[jax.ShapeDtypeStruct((N, D), F32),
                   jax.ShapeDtypeStruct((N // MOE_TILE * LOCAL_ROWS, D + 2 * LANES), BF16),
                   jax.ShapeDtypeStruct((N // MOE_TILE, 8, MOE_TILE), F32),
                   jax.ShapeDtypeStruct((N // MOE_TILE, 8, LANES), F32)],
        scratch_shapes=[pltpu.VMEM((IN_COLS_PAD, D), BF16), pltpu.VMEM(wo.shape, BF16),
                        act(ATT_Q), act(2 * ATT_KV), act(2 * ATT_KV), act(GLA_K), act(GLA_K), act(GLA_V),
                        act(GLA_K, F32), act(GLA_V),
                        act(ATT_Q), act(GLA_V), act(2 * ATT_KV), act(2 * ATT_KV),
                        pltpu.VMEM((B, GLA_HEADS // 2, LANES, 2 * GLA_DV), F32),
                        pltpu.SemaphoreType.DMA(())],
        compiler_params=_params("arbitrary", "arbitrary"),
        name="mix_out",
    )(sinks, x, mod, g1, w_in_t, qn, kn, bdq, bdk, wgk, bgk, gatt, cum, bdms, smask, ggla,
      wo, g2, wr, br, striu)


def _chunk_copy(src_ref, src_chunk, dst_ref, dst_chunk, sem):
    return pltpu.make_async_copy(src_ref.at[src_chunk], dst_ref.at[dst_chunk], sem)


def _moe_kernel(src_ref, dst_ref, grp_ref, next_ref, nt_ref, used_ref, rows_ref, w1f_ref, w3f_ref, w2f_ref, y_ref,
                in_buf, out_buf, zero_buf, st1_ref, st3_ref, st2_ref, w1_ref, w3_ref, w2_ref,
                in_sem, out_sem, zero_sem, w_sem):
    j = pl.program_id(0)
    n_tiles = nt_ref[0]
    d_model = w2_ref.shape[2]

    stages = ((w1f_ref, st1_ref, w1_ref), (w3f_ref, st3_ref, w3_ref), (w2f_ref, st2_ref, w2_ref))

    def fetch_weights(group):
        for hbm, stage, _ in stages:
            pltpu.make_async_copy(hbm.at[group], stage, w_sem).start()

    def enter_group():
        for hbm, stage, dst in stages:
            pltpu.make_async_copy(hbm.at[0], stage, w_sem).wait()
        for hbm, stage, dst in stages:
            for k in range(EXPERTS_PER_GROUP):
                dst[k] = stage[k].astype(BF16)

        @pl.when(next_ref[j] != grp_ref[j])
        def _():
            fetch_weights(next_ref[j])

    def gather(tile, slot):
        def body(k, carry):
            _chunk_copy(rows_ref, src_ref[tile * TILE_CHUNKS + k], in_buf.at[slot], k, in_sem.at[slot]).start()
            return carry
        lax.fori_loop(0, TILE_CHUNKS, body, 0, unroll=True)

    def wait_gather(slot):
        def body(k, carry):
            _chunk_copy(rows_ref, 0, in_buf.at[slot], k, in_sem.at[slot]).wait()
            return carry
        lax.fori_loop(0, TILE_CHUNKS, body, 0, unroll=True)

    def scatter(tile, slot):
        def body(k, carry):
            _chunk_copy(out_buf.at[slot], k, y_ref, dst_ref[tile * TILE_CHUNKS + k], out_sem.at[slot]).start()
            return carry
        lax.fori_loop(0, TILE_CHUNKS, body, 0, unroll=True)

    def wait_scatter(slot):
        def body(k, carry):
            _chunk_copy(out_buf.at[slot], k, y_ref, 0, out_sem.at[slot]).wait()
            return carry
        lax.fori_loop(0, TILE_CHUNKS, body, 0, unroll=True)

    def zero_fill(wait):
        def per_tile(i, carry):
            def body(c, inner):
                copy = _chunk_copy(zero_buf, 0, y_ref, i * LOCAL_CHUNKS + c, zero_sem)
                if wait:
                    copy.wait()
                else:
                    copy.start()
                return inner
            return lax.fori_loop(used_ref[i], LOCAL_CHUNKS, body, carry)
        lax.fori_loop(0, used_ref.shape[0], per_tile, 0)

    @pl.when(j == 0)
    def _():
        fetch_weights(grp_ref[0])
        zero_buf[...] = jnp.zeros_like(zero_buf)
        scratch0 = used_ref.shape[0] * LOCAL_CHUNKS
        for wait in (False, True):
            for k in range(2 * TILE_CHUNKS):
                copy = _chunk_copy(zero_buf, 0, y_ref, scratch0 + k, zero_sem)
                copy.wait() if wait else copy.start()
        zero_fill(wait=False)
        gather(0, 0)

    @pl.when(jnp.logical_or(j == 0, grp_ref[j] != grp_ref[jnp.maximum(j - 1, 0)]))
    def _():
        enter_group()

    @pl.when(j + 1 < n_tiles)
    def _():
        gather(j + 1, (j + 1) % 2)

    @pl.when(j < n_tiles)
    def _():
        slot = j % 2
        wait_gather(slot)
        rows = in_buf[slot].reshape(EXPERT_TILE, in_buf.shape[-1])
        h = rows[:, 0:d_model]
        weights = rows[:, d_model:d_model + LANES].astype(F32) + rows[:, d_model + LANES:].astype(F32)
        experts = range(EXPERTS_PER_GROUP)
        up = [(_dot(h, w1_ref[k]), _dot(h, w3_ref[k])) for k in experts]
        hid = [(a * _sigmoid(a) * g * weights[:, k:k + 1]).astype(BF16) for k, (a, g) in zip(experts, up)]
        y = _dot(hid[0], w2_ref[0])
        for k in experts[1:]:
            y = y + _dot(hid[k], w2_ref[k])

        @pl.when(j >= 2)
        def _():
            wait_scatter(slot)

        out_buf[slot] = y.astype(BF16).reshape(TILE_CHUNKS, CHUNK, d_model)
        scatter(j, slot)

        @pl.when(j == n_tiles - 1)
        def _():
            @pl.when(j >= 1)
            def _():
                wait_scatter(1 - slot)
            wait_scatter(slot)
            zero_fill(wait=True)


def _combine_kernel(x1_ref, mod_ref, lpos_ref, y_ref, o_ref):
    tm = MOE_TILE
    half = WINDOW
    per_tile = tm // half
    tiles_per_block = o_ref.shape[0] // per_tile
    local_row = lax.broadcasted_iota(jnp.int32, (tm, LOCAL_ROWS), 1).astype(F32)
    for t in range(x1_ref.shape[0] // tm):
        lpos = jnp.broadcast_to(lpos_ref[t][0:1, :], (LANES, tm)).T[:, 0:1]
        unsort = jnp.where(local_row == lpos, 1.0, 0.0).astype(BF16)
        y = _dot(unsort, y_ref[t * LOCAL_ROWS:(t + 1) * LOCAL_ROWS, :])
        block, tile = divmod(t, tiles_per_block)
        for i in range(per_tile):
            s = tile * per_tile + i
            rows = slice(t * tm + i * half, t * tm + (i + 1) * half)
            o_ref[s, block * half:(block + 1) * half, :] = (
                x1_ref[rows, :] + mod_ref[5, s:s + 1, :] * y[i * half:(i + 1) * half, :])


def _moe_plan(cnt):
    n_local = cnt.shape[0]
    chunks = (cnt + CHUNK - 1) // CHUNK
    used = jnp.sum(chunks, axis=1)
    local_off = jnp.cumsum(chunks, axis=1) - chunks
    tiles_g = (jnp.sum(chunks, axis=0) + TILE_CHUNKS - 1) // TILE_CHUNKS
    tile_end = jnp.cumsum(tiles_g)
    n_tiles = tile_end[-1]
    group_start = (tile_end - tiles_g) * TILE_CHUNKS
    seg_len = chunks.T.reshape(-1)
    seg_start = (group_start[:, None] + (jnp.cumsum(chunks, axis=0) - chunks).T).reshape(-1)
    seg_src = (jnp.arange(n_local)[None, :] * LOCAL_CHUNKS + local_off.T).reshape(-1)
    max_chunks = n_local * MOE_TILE // CHUNK + n_local * N_GROUPS + N_GROUPS * TILE_CHUNKS
    max_tiles = (max_chunks + TILE_CHUNKS - 1) // TILE_CHUNKS
    c = jnp.arange(max_tiles * TILE_CHUNKS)[:, None]
    within = c - seg_start[None, :]
    hit = jnp.logical_and(within >= 0, within < seg_len[None, :])
    valid = jnp.any(hit, axis=1)
    src = jnp.sum(jnp.where(hit, seg_src[None, :] + within, 0), axis=1)
    src = jnp.where(valid, src, LOCAL_CHUNKS - 1)
    slot_k = c[:, 0] % (2 * TILE_CHUNKS)
    dst = jnp.where(valid, src, n_local * LOCAL_CHUNKS + slot_k)
    j = jnp.minimum(jnp.arange(max_tiles), n_tiles - 1)
    grp = jnp.sum(j[:, None] >= tile_end[None, :], axis=1)
    gid = jnp.arange(N_GROUPS)
    later = jnp.where(jnp.logical_and(gid[None, :] > gid[:, None], tiles_g[None, :] > 0), gid[None, :], N_GROUPS)
    next_of = jnp.min(later, axis=1)
    next_of = jnp.where(next_of == N_GROUPS, gid, next_of)
    next_grp = jnp.sum(jnp.where(grp[:, None] == gid[None, :], next_of[None, :], 0), axis=1)
    i32 = lambda a: a.astype(jnp.int32)
    return i32(src), i32(dst), i32(grp), i32(next_grp), i32(n_tiles).reshape(1), i32(used)


def _moe(plan, rows_local, w1g, w3g, w2g):
    src, dst, grp, next_grp, n_tiles, used = plan
    D = w2g.shape[3]
    n_rows, cols = rows_local.shape
    n_chunks = n_rows // CHUNK
    hbm = pl.BlockSpec(memory_space=pl.ANY)
    group_weights = [w1g, w3g, w2g]
    y = pl.pallas_call(
        _moe_kernel,
        grid_spec=pltpu.PrefetchScalarGridSpec(
            num_scalar_prefetch=6,
            grid=(grp.shape[0],),
            in_specs=[hbm, hbm, hbm, hbm],
            out_specs=hbm,
            scratch_shapes=([pltpu.VMEM((2, TILE_CHUNKS, CHUNK, cols), BF16),
                             pltpu.VMEM((2, TILE_CHUNKS, CHUNK, D), BF16),
                             pltpu.VMEM((1, CHUNK, D), BF16)]
                            + [pltpu.VMEM(w.shape[1:], F32) for w in group_weights]
                            + [pltpu.VMEM(w.shape[1:], BF16) for w in group_weights]
                            + [pltpu.SemaphoreType.DMA((2,)), pltpu.SemaphoreType.DMA((2,)),
                               pltpu.SemaphoreType.DMA(()), pltpu.SemaphoreType.DMA(())])),
        out_shape=jax.ShapeDtypeStruct((n_chunks + 2 * TILE_CHUNKS, CHUNK, D), BF16),
        compiler_params=_params("arbitrary"),
        name="moe",
    )(src, dst, grp, next_grp, n_tiles, used, rows_local.reshape(n_chunks, CHUNK, cols), w1g, w3g, w2g)
    return y.reshape((n_chunks + 2 * TILE_CHUNKS) * CHUNK, D)


def _combine(x1, mod, lpos, y_local, *, B, T):
    N, D = x1.shape
    L = COMBINE_BLOCKS * WINDOW
    subs = B * L // MOE_TILE
    return pl.pallas_call(
        _combine_kernel,
        grid=(T // L,),
        in_specs=[pl.BlockSpec((B * L, D), lambda i: (i, 0)),
                  pl.BlockSpec(mod.shape, lambda i: (0, 0, 0)),
                  pl.BlockSpec((subs, 8, MOE_TILE), lambda i: (i, 0, 0)),
                  pl.BlockSpec((subs * LOCAL_ROWS, D), lambda i: (i, 0))],
        out_specs=pl.BlockSpec((B, L, D), lambda i: (0, i, 0)),
        out_shape=jax.ShapeDtypeStruct((B, T, D), F32),
        compiler_params=_params("arbitrary"),
        name="moe_combine",
    )(x1, mod, lpos, y_local)


def _block_diag(n, blk, value, dtype):
    r = np.arange(n)[:, None] // blk
    c = np.arange(n)[None, :] // blk
    return jnp.asarray(np.where(r == c, value, 0.0), dtype)


def _gla_constants():
    L = GLA_CHUNK
    i = np.arange(L)[:, None]
    j = np.arange(L)[None, :]
    tri = j <= i
    bdtri = np.logical_and(j <= i, i // GLA_SUB == j // GLA_SUB)
    cum = np.block([[tri, tri], [bdtri, bdtri]])
    bdms = _block_diag(2 * GLA_DV, GLA_DV, 1.0 / GLA_DV, BF16)
    d = np.arange(LANES)[:, None] // GLA_DK
    e = np.arange(2 * GLA_DV)[None, :] // GLA_DV
    smask = d == e
    return jnp.asarray(cum, BF16), bdms, jnp.asarray(smask, F32)


def kernel(x, c, w_ada, b_ada, g_norm1, w_in, q_norm, k_norm, sinks, w_gk2, b_gk, g_gla_out, g_att_out,
           w_out, g_norm2, w_group, b_group, w_router, b_router, w1, w3, w2):
    B, T, D = x.shape
    N = B * T
    depth = w_ada.shape[0]
    cum, bdms, smask = _gla_constants()
    bdq = _block_diag(2 * LANES, HEAD_DIM, 1.0 / HEAD_DIM, BF16)
    bdk = _block_diag(ATT_KV, HEAD_DIM, 1.0 / HEAD_DIM, BF16)

    for l in range(depth):
        mod = _adaln_mod(c, w_ada[l], b_ada[l])

        wgk = jnp.concatenate([w_gk2[l], jnp.zeros((LANES - GLA_RANK, GLA_K), F32)], axis=0).astype(BF16)
        pad = LANES - N_GROUPS - N_EXPERTS
        wr_t = jnp.concatenate([w_group[l], w_router[l], jnp.zeros((D, pad), F32)], axis=1).T.astype(BF16)
        br_t = jnp.concatenate([b_group[l], b_router[l], jnp.zeros((pad,), F32)]).reshape(LANES, 1)
        striu = jnp.asarray(np.arange(MOE_TILE)[:, None] < np.arange(MOE_TILE)[None, :], BF16)
        x1, rows_local, lpos, cnt = _mix_out(
            sinks[l], x, mod, g_norm1[l].reshape(1, D), w_in[l].T,
            q_norm[l].reshape(1, HEAD_DIM), k_norm[l].reshape(1, HEAD_DIM),
            bdq, bdk, wgk, b_gk[l].reshape(1, GLA_K),
            g_att_out[l].reshape(1, ATT_Q), cum, bdms, smask, g_gla_out[l].reshape(1, GLA_DV),
            w_out[l], g_norm2[l].reshape(1, D), wr_t, br_t, striu)
        plan = _moe_plan(cnt[:, :N_GROUPS, 0].astype(jnp.int32))
        by_group = lambda w: w.reshape((N_GROUPS, EXPERTS_PER_GROUP) + w.shape[1:])
        y_local = _moe(plan, rows_local, by_group(w1[l]), by_group(w3[l]), by_group(w2[l]))
        x = _combine(x1, mod, lpos, y_local, B=B, T=T)
    return x
```

```python
import jax
import jax.numpy as jnp
import numpy as np
from jax import lax
from jax.experimental import pallas as pl
from jax.experimental.pallas import tpu as pltpu

F32 = jnp.float32
BF16 = jnp.bfloat16

EPS = 1e-6
LOG2_E = 1.4426950408889634
ATT_HEADS = 8
ATT_KV_HEADS = 2
HEAD_DIM = 64
WINDOW = 128
ATT_Q = ATT_HEADS * HEAD_DIM
ATT_KV = ATT_KV_HEADS * HEAD_DIM
GLA_HEADS = 4
GLA_DK = 64
GLA_DV = 128
GLA_RANK = 16
GLA_NORMALIZER = 16.0
GLA_K = GLA_HEADS * GLA_DK
GLA_V = GLA_HEADS * GLA_DV
N_GROUPS = 4
EXPERTS_PER_GROUP = 4
N_EXPERTS = N_GROUPS * EXPERTS_PER_GROUP

LANES = 128
PROJ_SPLITS = (512, 512)
PROJ_TILE = sum(PROJ_SPLITS)
COMBINE_BLOCKS = 2
ROUTER_ROWS = 24
GLA_CHUNK = 128
GLA_SUB = 16
N_SUB = GLA_CHUNK // GLA_SUB
ROUTER_LANE0 = N_GROUPS
VMEM_LIMIT = 62 * 1024 * 1024
MOE_TILE = 256
EXPERT_TILE = 512
CHUNK = 16
TILE_CHUNKS = EXPERT_TILE // CHUNK
LOCAL_CHUNKS = (MOE_TILE + N_GROUPS * (CHUNK - 1)) // CHUNK + 2
LOCAL_ROWS = LOCAL_CHUNKS * CHUNK

_QA0, _KA0, _VA0 = 0, ATT_Q, ATT_Q + ATT_KV
_QG0 = _VA0 + ATT_KV
_KG0 = _QG0 + GLA_K
_VG0 = _KG0 + GLA_K
_OG0 = _VG0 + GLA_V
_LR0 = _OG0 + GLA_V
IN_COLS_PAD = _LR0 + LANES


def _dot(a, b):
    return jnp.dot(a, b, preferred_element_type=F32)


def _dot_nt(a, b):
    return lax.dot_general(a, b, (((1,), (1,)), ((), ())), preferred_element_type=F32)


def _sigmoid(x):
    return 1.0 / (1.0 + jnp.exp(-x))


def _params(*sem):
    return pltpu.CompilerParams(dimension_semantics=sem, vmem_limit_bytes=VMEM_LIMIT)


def _mod_kernel(c_ref, w_ref, b_ref, o_ref):
    c = c_ref[...]
    s = (c * _sigmoid(c)).astype(BF16)
    o_ref[0] = _dot(s, w_ref[...].astype(BF16)) + b_ref[...]


def _adaln_mod(c, w_ada, b_ada):
    B, D = c.shape
    n = w_ada.shape[1]
    tn = D
    return pl.pallas_call(
        _mod_kernel,
        grid=(n // tn,),
        in_specs=[pl.BlockSpec((B, D), lambda j: (0, 0)),
                  pl.BlockSpec((D, tn), lambda j: (0, j)),
                  pl.BlockSpec((1, tn), lambda j: (0, j))],
        out_specs=pl.BlockSpec((1, B, tn), lambda j: (j, 0, 0)),
        out_shape=jax.ShapeDtypeStruct((n // tn, B, tn), F32),
        compiler_params=_params("arbitrary"),
        name="adaln_mod",
    )(c, w_ada, b_ada.reshape(1, n))


def _round_w_in(wf_ref, w_ref, wof_ref, wo_ref, stage_ref, sem):
    @pl.when(pl.program_id(1) == 0)
    def _():
        fetch = pltpu.make_async_copy(wof_ref, stage_ref, sem)
        fetch.start()
        lr_src = _OG0
        w_ref[0:_OG0, :] = wf_ref[0:_OG0, :].astype(BF16)
        w_ref[_OG0:_LR0, :] = wf_ref[lr_src + GLA_RANK:lr_src + GLA_RANK + GLA_V, :].astype(BF16)
        w_ref[_LR0:_LR0 + GLA_RANK, :] = wf_ref[lr_src:lr_src + GLA_RANK, :].astype(BF16)
        w_ref[_LR0 + GLA_RANK:IN_COLS_PAD, :] = jnp.zeros((LANES - GLA_RANK, wf_ref.shape[1]), BF16)
        fetch.wait()
        wo_ref[...] = stage_ref[...].astype(BF16)


def _inproj_stages(subs, x_ref, mod_ref, g1_ref, qn_ref, kn_ref, bdq_ref, bdk_ref, wgk_ref, bgk_ref,
                   qa_ref, ka_ref, va_ref, qg_ref, kg_ref, vg_ref, la_ref, og_ref, w_ref):
    blk = x_ref.shape[1]
    seqs_of = [range(sum(PROJ_SPLITS[:t]) // blk, sum(PROJ_SPLITS[:t + 1]) // blk)
               for t in range(len(PROJ_SPLITS))]

    def put(ref, t, value, lanes=slice(None)):
        for i, s in enumerate(seqs_of[t]):
            ref[s, :, lanes] = value[i * blk:(i + 1) * blk, :]

    h = {}
    for t in subs:
        parts = []
        for s in seqs_of[t]:
            x = x_ref[s]
            ms = jnp.mean(x * x, axis=-1, keepdims=True)
            xn = x * lax.rsqrt(ms + EPS) * g1_ref[...]
            parts.append((xn * (1.0 + mod_ref[1, s:s + 1, :]) + mod_ref[0, s:s + 1, :]).astype(BF16))
        h[t] = jnp.concatenate(parts, axis=0)
    yield

    proj = lambda t, c0, width: _dot_nt(h[t], w_ref[c0:c0 + width, :])
    wgk = jnp.concatenate([wgk_ref[...].astype(BF16), jnp.zeros((LANES - GLA_RANK, GLA_K), BF16)], axis=0)
    gate_of = lambda lr: _dot(lr.astype(BF16), wgk) + bgk_ref[...]
    qa, kv, q_ms, k_ms, qk_g, vg, og, lr, gate = ({} for _ in range(9))
    prev = None
    for t in subs:
        qa[t] = proj(t, _QA0, ATT_Q)
        if prev is not None:
            gate[prev] = gate_of(lr[prev])
        prev = t
        yield
        kv[t] = proj(t, _KA0, 2 * ATT_KV)
        q_sq = (qa[t] * qa[t]).astype(BF16)
        q_ms[t] = jnp.concatenate([_dot(q_sq[:, c:c + 2 * LANES], bdq_ref[...])
                                   for c in range(0, ATT_Q, 2 * LANES)], axis=1)
        yield
        qk_g[t] = proj(t, _QG0, 2 * GLA_K)
        k = kv[t][:, 0:ATT_KV]
        k_ms[t] = _dot((k * k).astype(BF16), bdk_ref[...])
        yield
        vg[t] = proj(t, _VG0, GLA_V)
        yield
        og[t] = proj(t, _OG0, GLA_V)
        lr[t] = proj(t, _LR0, LANES)
        yield
    gate[prev] = gate_of(lr[prev])
    yield

    for t in subs:
        low = lax.broadcasted_iota(jnp.int32, (PROJ_SPLITS[t], ATT_KV), 1) < HEAD_DIM
        qn = jnp.concatenate([qn_ref[...]] * ATT_HEADS, axis=1)
        kn = jnp.concatenate([kn_ref[...]] * ATT_KV_HEADS, axis=1)
        put(qa_ref, t, (qa[t] * lax.rsqrt(q_ms[t] + EPS) * qn * (HEAD_DIM ** -0.5 * LOG2_E)).astype(BF16))
        k = kv[t][:, 0:ATT_KV] * lax.rsqrt(k_ms[t] + EPS) * kn
        v = kv[t][:, ATT_KV:2 * ATT_KV]
        for src, dst in ((k, ka_ref), (v, va_ref)):
            swapped = pltpu.roll(src, HEAD_DIM, axis=1)
            put(dst, t, jnp.where(low, src, swapped).astype(BF16), slice(0, LANES))
            put(dst, t, jnp.where(low, swapped, src).astype(BF16), slice(LANES, 2 * LANES))
        put(qg_ref, t, (qk_g[t][:, 0:GLA_K] * (GLA_DK ** -0.5)).astype(BF16))
        put(kg_ref, t, qk_g[t][:, GLA_K:2 * GLA_K].astype(BF16))
        put(vg_ref, t, vg[t].astype(BF16))
        put(og_ref, t, (og[t] * _sigmoid(og[t])).astype(BF16))
        log_sig = jnp.minimum(gate[t], 0.0) - jnp.log(1.0 + jnp.exp(-jnp.abs(gate[t])))
        put(la_ref, t, log_sig * (1.0 / GLA_NORMALIZER))


def _attn_stages(seqs, sinks_ref, q_ref, kc_ref, vc_ref, gatt_ref, o_ref, kp_ref, vp_ref):
    blk = WINDOW
    first = pl.program_id(1) == 0
    qi = lax.broadcasted_iota(jnp.int32, (blk, blk), 0)
    cj = lax.broadcasted_iota(jnp.int32, (blk, blk), 1)
    from_prev = cj > qi
    dist = (qi - cj + jnp.where(from_prev, blk, 0)).astype(F32)
    no_prev = jnp.where(jnp.logical_and(from_prev, first), -1e30, 0.0)
    low = cj < HEAD_DIM
    half = (jnp.where(low, 1.0, 0.0).astype(BF16), jnp.where(low, 0.0, 1.0).astype(BF16))
    half2 = tuple(jnp.concatenate([m, m], axis=0) for m in half)
    prev_mask = jnp.where(from_prev, 1.0, 0.0).astype(BF16)
    cur_mask = jnp.where(from_prev, 0.0, 1.0).astype(BF16)

    n_pairs = ATT_HEADS // 2
    pairs_per_kv = n_pairs // ATT_KV_HEADS
    units = [(bi, j) for bi in seqs for j in range(n_pairs)]

    def kv_blocks(bi, g):
        lanes = slice(g * LANES, (g + 1) * LANES)
        return (kp_ref[bi, :, lanes], vp_ref[bi, :, lanes]), (kc_ref[bi, :, lanes], vc_ref[bi, :, lanes])

    scores = {}
    for bi in seqs:
        for g in range(ATT_KV_HEADS):
            (kp, _), (kc, _) = kv_blocks(bi, g)
            k_both = jnp.concatenate([kp, kc], axis=0)
            group = range(g * pairs_per_kv, (g + 1) * pairs_per_kv)
            q_stack = jnp.concatenate([q_ref[bi, :, j * LANES:(j + 1) * LANES] for j in group], axis=0)
            for p in range(2):
                s_stack = _dot_nt(q_stack, k_both * half2[p])
                for jj, j in enumerate(group):
                    scores[bi, j, p] = s_stack[jj * blk:(jj + 1) * blk, :]
    yield

    bias = [2.0 ** (-8.0 * (h + 1) / ATT_HEADS) * LOG2_E * dist - no_prev for h in range(ATT_HEADS)]
    probs, sink_terms = {}, {}
    for bi, j in units:
        for p in range(2):
            h = 2 * j + p
            s_both = scores[bi, j, p]
            s = jnp.where(from_prev, s_both[:, 0:blk], s_both[:, blk:2 * blk]) - bias[h]
            sink = sinks_ref[h] * LOG2_E
            m = jnp.maximum(jnp.max(s, axis=-1, keepdims=True), sink)
            probs[bi, j, p] = jnp.exp2(s - m)
            sink_terms[bi, j, p] = jnp.exp2(sink - m)
    yield

    outs = {}
    for bi in seqs:
        for g in range(ATT_KV_HEADS):
            (_, vp), (_, vc) = kv_blocks(bi, g)
            v_stack = jnp.concatenate([jnp.concatenate([v * half[p], half[p]], axis=1)
                                       for v in (vp, vc) for p in range(2)], axis=0)
            group = range(g * pairs_per_kv, (g + 1) * pairs_per_kv)
            p_stack = []
            for j in group:
                e = [probs[bi, j, p].astype(BF16) for p in range(2)]
                p_stack.append(jnp.concatenate([x * m for m in (prev_mask, cur_mask) for x in e], axis=1))
            pv_stack = _dot(jnp.concatenate(p_stack, axis=0), v_stack)
            for jj, j in enumerate(group):
                pv = pv_stack[jj * blk:(jj + 1) * blk, :]
                den = pv[:, LANES:2 * LANES] + jnp.where(low, sink_terms[bi, j, 0], sink_terms[bi, j, 1])
                outs[bi, j] = pv[:, 0:LANES] / den
    yield

    for bi in seqs:
        o = jnp.concatenate([outs[bi, j] for j in range(n_pairs)], axis=1)
        ms = jnp.mean(o * o, axis=-1, keepdims=True)
        o_ref[bi] = (o * lax.rsqrt(ms + EPS) * gatt_ref[...]).astype(BF16)
        kp_ref[bi] = kc_ref[bi]
        vp_ref[bi] = vc_ref[bi]


def _gla_stages(seqs, q_ref, k_ref, v_ref, la_ref, og_ref, cum_ref, bdms_ref, smask_ref,
                ggla_ref, o_ref, state_ref):
    L = GLA_CHUNK
    lane = lax.broadcasted_iota(jnp.int32, (GLA_SUB, LANES), 1)
    head_mask = [jnp.where(lane // GLA_DK == hh, 1.0, 0.0).astype(BF16) for hh in range(2)]
    causal = lax.broadcasted_iota(jnp.int32, (L, L), 0) >= lax.broadcasted_iota(jnp.int32, (L, L), 1)
    units = [(s, pair) for s in seqs for pair in range(GLA_HEADS // 2)]
    kl = lambda pair: slice(pair * LANES, (pair + 1) * LANES)
    vl = lambda pair: slice(pair * 2 * GLA_DV, (pair + 1) * 2 * GLA_DV)

    b, b_in = {}, {}
    for s in seqs:
        la = la_ref[s]
        la_hi = la.astype(BF16)
        la_lo = (la - la_hi.astype(F32)).astype(BF16)
        sums = _dot(cum_ref[...], jnp.concatenate([la_hi, la_lo], axis=0)) * LOG2_E
        b[s] = sums[0:L, :]
        b_in[s] = sums[L:2 * L, :]
    yield

    q_both, keys, q_dec, k_dec_t, b_last = {}, {}, {}, {}, {}
    for s in seqs:
        ref = b[s] - b_in[s]
        b_last[s] = b[s][L - 1:L, :]
        q = q_ref[s].astype(F32)
        k = k_ref[s].astype(F32)
        q_in = (q * jnp.exp2(b_in[s])).astype(BF16)
        q_dec[s] = (q * jnp.exp2(b[s])).astype(BF16)
        k_dec = k * jnp.exp2(b_last[s] - b[s])
        for pair in range(GLA_HEADS // 2):
            k_p, b_p, ref_p = k[:, kl(pair)], b[s][:, kl(pair)], ref[:, kl(pair)]
            expanded = []
            for g in range(N_SUB):
                top = (g + 1) * GLA_SUB
                live = (k_p[0:top, :] * jnp.exp2(ref_p[g * GLA_SUB:g * GLA_SUB + 1, :] - b_p[0:top, :])).astype(BF16)
                expanded.append(live if top == L else
                                jnp.concatenate([live, jnp.zeros((L - top, LANES), BF16)], axis=0))
            keys[s, pair] = jnp.concatenate(expanded, axis=1)
            q_p = q_in[:, kl(pair)]
            zero_group = jnp.zeros((GLA_SUB, LANES), BF16)
            q_both[s, pair] = []
            for t in range(N_SUB // 2):
                lhs = []
                for hh in range(2):
                    for side in range(2):
                        g = 2 * t + side
                        piece = q_p[g * GLA_SUB:(g + 1) * GLA_SUB, :] * head_mask[hh]
                        lhs.append(jnp.concatenate([piece, zero_group] if side == 0 else [zero_group, piece], axis=1))
                q_both[s, pair].append(jnp.concatenate(lhs, axis=0))
            k_dec_t[s, pair] = k_dec[:, kl(pair)].T.astype(BF16)
    yield

    scores = {(u, t): _dot_nt(q_both[u][t], keys[u][:, 2 * t * LANES:2 * (t + 1) * LANES])
              for u in units for t in range(N_SUB // 2)}
    yield

    outs, updates = {}, {}
    for s, pair in units:
        v_p = v_ref[s, :, vl(pair)]
        o_parts = []
        for hh in range(2):
            a = jnp.concatenate([scores[(s, pair), t][2 * hh * GLA_SUB:2 * (hh + 1) * GLA_SUB, :]
                                 for t in range(N_SUB // 2)], axis=0)
            a = jnp.where(causal, a, 0.0).astype(BF16)
            o_parts.append(_dot(a, v_p[:, hh * GLA_DV:(hh + 1) * GLA_DV]))
        state = state_ref[s, pair]
        outs[s, pair] = jnp.concatenate(o_parts, axis=1) + _dot(q_dec[s][:, kl(pair)], state.astype(BF16))
        updates[s, pair] = _dot(k_dec_t[s, pair], v_p)
    yield

    for s, pair in units:
        decay = jnp.broadcast_to(jnp.exp2(b_last[s][:, kl(pair)]), (LANES, LANES)).T
        state_ref[s, pair] = (state_ref[s, pair] * jnp.concatenate([decay, decay], axis=1)
                              + updates[s, pair] * smask_ref[...])
        o = outs[s, pair]
        ms = _dot((o * o).astype(BF16), bdms_ref[...])
        gain = jnp.concatenate([ggla_ref[...]] * 2, axis=1)
        y = o * lax.rsqrt(ms + EPS) * gain * og_ref[s, :, vl(pair)].astype(F32)
        o_ref[s, :, vl(pair)] = y.astype(BF16)


def _run_interleaved(*stage_generators):
    pending = list(stage_generators)
    while pending:
        for stages in list(pending):
            if next(stages, "done") == "done":
                pending.remove(stages)


def _route(logits_t):
    lt = logits_t[0:ROUTER_ROWS, :]
    row = lax.broadcasted_iota(jnp.int32, lt.shape, 0)
    neg_inf = -jnp.inf
    g_log = jnp.where(row < N_GROUPS, lt, neg_inf)
    g_max = jnp.max(g_log, axis=0, keepdims=True)
    g_sel = jnp.min(jnp.where(g_log == g_max, row, LANES), axis=0, keepdims=True)
    g_sum = jnp.sum(jnp.where(row < N_GROUPS, jnp.exp(lt - g_max), 0.0), axis=0, keepdims=True)
    p_group = 1.0 / g_sum
    e_lo = ROUTER_LANE0 + EXPERTS_PER_GROUP * g_sel
    in_group = jnp.logical_and(row >= e_lo, row < e_lo + EXPERTS_PER_GROUP)
    e_log = jnp.where(in_group, lt, neg_inf)
    e_max = jnp.max(e_log, axis=0, keepdims=True)
    top1 = jnp.min(jnp.where(e_log == e_max, row, LANES), axis=0, keepdims=True)
    e_log2 = jnp.where(row == top1, neg_inf, e_log)
    e_max2 = jnp.max(e_log2, axis=0, keepdims=True)
    top2 = jnp.min(jnp.where(e_log2 == e_max2, row, LANES), axis=0, keepdims=True)
    ratio = jnp.exp(e_max2 - e_max)
    w_top1 = p_group / (1.0 + ratio)
    w_top2 = p_group * ratio / (1.0 + ratio)
    row8 = lax.broadcasted_iota(jnp.int32, (8, lt.shape[1]), 0)
    weights = jnp.where(row8 == top1 - e_lo, w_top1, 0.0) + jnp.where(row8 == top2 - e_lo, w_top2, 0.0)
    return g_sel, weights


def _outproj_stages(subs, ya_ref, yg_ref, x_ref, mod_ref, wo_ref, g2_ref, wrt_ref, brt_ref, striu_ref,
                    x1_ref, row_ref, lpos_ref, cnt_ref):
    tm = MOE_TILE
    half = x_ref.shape[1]
    per_tile = tm // half
    tile_of = lambda ref, t: jnp.concatenate([ref[t * per_tile + i] for i in range(per_tile)], axis=0)

    mix = {t: _dot(tile_of(ya_ref, t), wo_ref[0:ATT_Q, :]) + _dot(tile_of(yg_ref, t), wo_ref[ATT_Q:ATT_Q + GLA_V, :])
           for t in subs}
    yield
    h2b = {}
    for t in subs:
        parts = []
        for i in range(per_tile):
            s = t * per_tile + i
            x1 = x_ref[s] + mod_ref[2, s:s + 1, :] * mix[t][i * half:(i + 1) * half, :]
            x1_ref[t * tm + i * half:t * tm + (i + 1) * half, :] = x1
            ms = jnp.mean(x1 * x1, axis=-1, keepdims=True)
            h2 = (x1 * lax.rsqrt(ms + EPS) * g2_ref[...]) * (1.0 + mod_ref[4, s:s + 1, :]) + mod_ref[3, s:s + 1, :]
            parts.append(h2.astype(BF16))
        h2b[t] = jnp.concatenate(parts, axis=0)
    yield
    logits_t = {t: _dot_nt(wrt_ref[...], h2b[t]) + brt_ref[...] for t in subs}
    yield

    routed = {t: _route(logits_t[t]) for t in subs}
    row8 = lax.broadcasted_iota(jnp.int32, (8, tm), 0)
    onehot = {t: jnp.where(row8 == routed[t][0], 1.0, 0.0) for t in subs}
    yield
    before = {t: _dot(onehot[t].astype(BF16), striu_ref[...]) for t in subs}
    yield

    local_row = lax.broadcasted_iota(jnp.int32, (LOCAL_ROWS, tm), 0).astype(F32)
    pad_rows = jnp.zeros((LANES - 8, tm), F32)
    for t in subs:
        count = jnp.sum(onehot[t], axis=1, keepdims=True)
        cnt_ref[t] = jnp.broadcast_to(count, (8, LANES))
        padded = jnp.broadcast_to(jnp.floor((count + (CHUNK - 1.0)) * (1.0 / CHUNK)) * CHUNK, (8, tm))
        start = jnp.zeros((8, tm), F32)
        for shift in range(1, N_GROUPS):
            start = start + jnp.where(row8 >= shift, pltpu.roll(padded, shift, axis=0), 0.0)
        lpos = jnp.sum(onehot[t] * (before[t] + start), axis=0, keepdims=True)
        lpos_ref[t] = jnp.broadcast_to(lpos, (8, tm))
        weights = jnp.concatenate([routed[t][1], pad_rows], axis=0).T
        w_hi = weights.astype(BF16)
        w_lo = (weights - w_hi.astype(F32)).astype(BF16)
        perm = jnp.where(local_row == lpos, 1.0, 0.0).astype(BF16)
        row_ref[t * LOCAL_ROWS:(t + 1) * LOCAL_ROWS, :] = _dot(
            perm, jnp.concatenate([h2b[t], w_hi, w_lo], axis=1)).astype(BF16)


def _mix_out_kernel(sinks_ref, x_ref, mod_ref, g1_ref, wf_ref, qn_ref, kn_ref, bdq_ref, bdk_ref, wgk_ref, bgk_ref,
                    gatt_ref, cum_ref, bdms_ref, smask_ref, ggla_ref,
                    wof_ref, g2_ref, wrt_ref, brt_ref, striu_ref,
                    x1_ref, row_ref, lpos_ref, cnt_ref,
                    w_ref, wo_ref, qa_ref, ka_ref, va_ref, qg_ref, kg_ref, vg_ref, la_ref, og_ref,
                    ya_ref, yg_ref, kp_ref, vp_ref, state_ref, w_sem):
    assert x1_ref.shape == wof_ref.shape and x1_ref.dtype == wof_ref.dtype
    _round_w_in(wf_ref, w_ref, wof_ref, wo_ref, x1_ref, w_sem)
    n_seq = x_ref.shape[0]
    seqs_per_tile = MOE_TILE // x_ref.shape[1]
    halves = []
    for first in (0, n_seq // 2):
        seqs = tuple(range(first, first + n_seq // 2))
        halves.append((seqs, (first * len(PROJ_SPLITS) // n_seq,),
                       tuple(range(first // seqs_per_tile, (first + n_seq // 2) // seqs_per_tile))))

    def inproj(sub_tiles):
        return _inproj_stages(sub_tiles, x_ref, mod_ref, g1_ref, qn_ref, kn_ref, bdq_ref, bdk_ref, wgk_ref, bgk_ref,
                              qa_ref, ka_ref, va_ref, qg_ref, kg_ref, vg_ref, la_ref, og_ref, w_ref)

    def mixers(seqs):
        return (_attn_stages(seqs, sinks_ref, qa_ref, ka_ref, va_ref, gatt_ref, ya_ref, kp_ref, vp_ref),
                _gla_stages(seqs, qg_ref, kg_ref, vg_ref, la_ref, og_ref, cum_ref, bdms_ref, smask_ref,
                            ggla_ref, yg_ref, state_ref))

    def outproj(tiles):
        return _outproj_stages(tiles, ya_ref, yg_ref, x_ref, mod_ref, wo_ref, g2_ref, wrt_ref, brt_ref, striu_ref,
                               x1_ref, row_ref, lpos_ref, cnt_ref)

    (seqs_a, subs_a, tiles_a), (seqs_b, subs_b, tiles_b) = halves
    _run_interleaved(inproj(subs_a + subs_b))

    @pl.when(pl.program_id(1) == 0)
    def _():
        state_ref[...] = jnp.zeros_like(state_ref)
        kp_ref[...] = jnp.zeros_like(kp_ref)
        vp_ref[...] = jnp.zeros_like(vp_ref)

    first_half = mixers(seqs_a)
    for stages in first_half:
        next(stages)
    _run_interleaved(*first_half, *mixers(seqs_b))
    _run_interleaved(outproj(tiles_a + tiles_b))


def _mix_out(sinks, x, mod, g1, w_in_t, qn, kn, bdq, bdk, wgk, bgk, gatt, cum, bdms, smask, ggla,
             wo, g2, wr, br, striu):
    assert GLA_CHUNK == WINDOW and MOE_TILE % WINDOW == 0
    B, T, D = x.shape
    L = WINDOW
    N = B * T
    assert B * L == PROJ_TILE
    tiles_per_step = B * L // MOE_TILE
    const = lambda a: pl.BlockSpec(a.shape, lambda b, i: (0,) * a.ndim)
    tiles = lambda *shape: pl.BlockSpec((tiles_per_step,) + shape, lambda b, i: (i,) + (0,) * len(shape))
    act = lambda c, dt=BF16: pltpu.VMEM((B, L, c), dt)
    return pl.pallas_call(
        _mix_out_kernel,
        grid=(1, T // L),
        in_specs=[pl.BlockSpec(memory_space=pltpu.SMEM),
                  pl.BlockSpec((B, L, D), lambda b, i: (0, i, 0)), const(mod), const(g1),
                  pl.BlockSpec(w_in_t.shape, lambda b, i: (0, 0), pipeline_mode=pl.Buffered(1)),
                  const(qn), const(kn), const(bdq), const(bdk), const(wgk), const(bgk),
                  const(gatt), const(cum), const(bdms), const(smask), const(ggla),
                  pl.BlockSpec(memory_space=pl.ANY), const(g2), const(wr), const(br), const(striu)],
        out_specs=[pl.BlockSpec((B * L, D), lambda b, i: (i, 0)),
                   pl.BlockSpec((tiles_per_step * LOCAL_ROWS, D + 2 * LANES), lambda b, i: (i, 0)),
                   tiles(8, MOE_TILE), tiles(8, LANES)],
        out_shape=[jax.ShapeDtypeStruct((N, D), F32),
                   jax.ShapeDtypeStruct((N // MOE_TILE * LOCAL_ROWS, D + 2 * LANES), BF16),
                   jax.ShapeDtypeStruct((N // MOE_TILE, 8, MOE_TILE), F32),
                   jax.ShapeDtypeStruct((N // MOE_TILE, 8, LANES), F32)],
        scratch_shapes=[pltpu.VMEM((IN_COLS_PAD, D), BF16), pltpu.VMEM(wo.shape, BF16),
                        act(ATT_Q), act(2 * ATT_KV), act(2 * ATT_KV), act(GLA_K), act(GLA_K), act(GLA_V),
                        act(GLA_K, F32), act(GLA_V),
                        act(ATT_Q), act(GLA_V), act(2 * ATT_KV), act(2 * ATT_KV),
                        pltpu.VMEM((B, GLA_HEADS // 2, LANES, 2 * GLA_DV), F32),
                        pltpu.SemaphoreType.DMA(())],
        compiler_params=_params("arbitrary", "arbitrary"),
        name="mix_out",
    )(sinks, x, mod, g1, w_in_t, qn, kn, bdq, bdk, wgk, bgk, gatt, cum, bdms, smask, ggla,
      wo, g2, wr, br, striu)


def _chunk_copy(src_ref, src_chunk, dst_ref, dst_chunk, sem):
    return pltpu.make_async_copy(src_ref.at[src_chunk], dst_ref.at[dst_chunk], sem)


def _moe_kernel(src_ref, dst_ref, grp_ref, next_ref, nt_ref, used_ref, rows_ref, w1f_ref, w3f_ref, w2f_ref, y_ref,
                in_buf, out_buf, zero_buf, st1_ref, st3_ref, st2_ref, w1_ref, w3_ref, w2_ref,
                in_sem, out_sem, zero_sem, w_sem):
    j = pl.program_id(0)
    n_tiles = nt_ref[0]
    d_model = w2_ref.shape[2]

    stages = ((w1f_ref, st1_ref, w1_ref), (w3f_ref, st3_ref, w3_ref), (w2f_ref, st2_ref, w2_ref))

    def fetch_weights(group):
        for hbm, stage, _ in stages:
            pltpu.make_async_copy(hbm.at[group], stage, w_sem).start()

    def enter_group():
        for hbm, stage, dst in stages:
            pltpu.make_async_copy(hbm.at[0], stage, w_sem).wait()
        for hbm, stage, dst in stages:
            for k in range(EXPERTS_PER_GROUP):
                dst[k] = stage[k].astype(BF16)

        @pl.when(next_ref[j] != grp_ref[j])
        def _():
            fetch_weights(next_ref[j])

    def gather(tile, slot):
        def body(k, carry):
            _chunk_copy(rows_ref, src_ref[tile * TILE_CHUNKS + k], in_buf.at[slot], k, in_sem.at[slot]).start()
            return carry
        lax.fori_loop(0, TILE_CHUNKS, body, 0, unroll=True)

    def wait_gather(slot):
        def body(k, carry):
            _chunk_copy(rows_ref, 0, in_buf.at[slot], k, in_sem.at[slot]).wait()
            return carry
        lax.fori_loop(0, TILE_CHUNKS, body, 0, unroll=True)

    def scatter(tile, slot):
        def body(k, carry):
            _chunk_copy(out_buf.at[slot], k, y_ref, dst_ref[tile * TILE_CHUNKS + k], out_sem.at[slot]).start()
            return carry
        lax.fori_loop(0, TILE_CHUNKS, body, 0, unroll=True)

    def wait_scatter(slot):
        def body(k, carry):
            _chunk_copy(out_buf.at[slot], k, y_ref, 0, out_sem.at[slot]).wait()
            return carry
        lax.fori_loop(0, TILE_CHUNKS, body, 0, unroll=True)

    def zero_fill(wait):
        def per_tile(i, carry):
            def body(c, inner):
                copy = _chunk_copy(zero_buf, 0, y_ref, i * LOCAL_CHUNKS + c, zero_sem)
                if wait:
                    copy.wait()
                else:
                    copy.start()
                return inner
            return lax.fori_loop(used_ref[i], LOCAL_CHUNKS, body, carry)
        lax.fori_loop(0, used_ref.shape[0], per_tile, 0)

    @pl.when(j == 0)
    def _():
        fetch_weights(grp_ref[0])
        zero_buf[...] = jnp.zeros_like(zero_buf)
        scratch0 = used_ref.shape[0] * LOCAL_CHUNKS
        for wait in (False, True):
            for k in range(2 * TILE_CHUNKS):
                copy = _chunk_copy(zero_buf, 0, y_ref, scratch0 + k, zero_sem)
                copy.wait() if wait else copy.start()
        zero_fill(wait=False)
        gather(0, 0)

    @pl.when(jnp.logical_or(j == 0, grp_ref[j] != grp_ref[jnp.maximum(j - 1, 0)]))
    def _():
        enter_group()

    @pl.when(j + 1 < n_tiles)
    def _():
        gather(j + 1, (j + 1) % 2)

    @pl.when(j < n_tiles)
    def _():
        slot = j % 2
        wait_gather(slot)
        rows = in_buf[slot].reshape(EXPERT_TILE, in_buf.shape[-1])
        h = rows[:, 0:d_model]
        weights = rows[:, d_model:d_model + LANES].astype(F32) + rows[:, d_model + LANES:].astype(F32)
        experts = range(EXPERTS_PER_GROUP)
        up = [(_dot(h, w1_ref[k]), _dot(h, w3_ref[k])) for k in experts]
        hid = [(a * _sigmoid(a) * g * weights[:, k:k + 1]).astype(BF16) for k, (a, g) in zip(experts, up)]
        y = _dot(hid[0], w2_ref[0])
        for k in experts[1:]:
            y = y + _dot(hid[k], w2_ref[k])

        @pl.when(j >= 2)
        def _():
            wait_scatter(slot)

        out_buf[slot] = y.astype(BF16).reshape(TILE_CHUNKS, CHUNK, d_model)
        scatter(j, slot)

        @pl.when(j == n_tiles - 1)
        def _():
            @pl.when(j >= 1)
            def _():
                wait_scatter(1 - slot)
            wait_scatter(slot)
            zero_fill(wait=True)


def _combine_kernel(x1_ref, mod_ref, lpos_ref, y_ref, o_ref):
    tm = MOE_TILE
    half = WINDOW
    per_tile = tm // half
    tiles_per_block = o_ref.shape[0] // per_tile
    local_row = lax.broadcasted_iota(jnp.int32, (tm, LOCAL_ROWS), 1).astype(F32)
    for t in range(x1_ref.shape[0] // tm):
        lpos = jnp.broadcast_to(lpos_ref[t][0:1, :], (LANES, tm)).T[:, 0:1]
        unsort = jnp.where(local_row == lpos, 1.0, 0.0).astype(BF16)
        y = _dot(unsort, y_ref[t * LOCAL_ROWS:(t + 1) * LOCAL_ROWS, :])
        block, tile = divmod(t, tiles_per_block)
        for i in range(per_tile):
            s = tile * per_tile + i
            rows = slice(t * tm + i * half, t * tm + (i + 1) * half)
            o_ref[s, block * half:(block + 1) * half, :] = (
                x1_ref[rows, :] + mod_ref[5, s:s + 1, :] * y[i * half:(i + 1) * half, :])


def _moe_plan(cnt):
    n_local = cnt.shape[0]
    chunks = (cnt + CHUNK - 1) // CHUNK
    used = jnp.sum(chunks, axis=1)
    local_off = jnp.cumsum(chunks, axis=1) - chunks
    tiles_g = (jnp.sum(chunks, axis=0) + TILE_CHUNKS - 1) // TILE_CHUNKS
    tile_end = jnp.cumsum(tiles_g)
    n_tiles = tile_end[-1]
    group_start = (tile_end - tiles_g) * TILE_CHUNKS
    seg_len = chunks.T.reshape(-1)
    seg_start = (group_start[:, None] + (jnp.cumsum(chunks, axis=0) - chunks).T).reshape(-1)
    seg_src = (jnp.arange(n_local)[None, :] * LOCAL_CHUNKS + local_off.T).reshape(-1)
    max_chunks = n_local * MOE_TILE // CHUNK + n_local * N_GROUPS + N_GROUPS * TILE_CHUNKS
    max_tiles = (max_chunks + TILE_CHUNKS - 1) // TILE_CHUNKS
    c = jnp.arange(max_tiles * TILE_CHUNKS)[:, None]
    within = c - seg_start[None, :]
    hit = jnp.logical_and(within >= 0, within < seg_len[None, :])
    valid = jnp.any(hit, axis=1)
    src = jnp.sum(jnp.where(hit, seg_src[None, :] + within, 0), axis=1)
    src = jnp.where(valid, src, LOCAL_CHUNKS - 1)
    slot_k = c[:, 0] % (2 * TILE_CHUNKS)
    dst = jnp.where(valid, src, n_local * LOCAL_CHUNKS + slot_k)
    j = jnp.minimum(jnp.arange(max_tiles), n_tiles - 1)
    grp = jnp.sum(j[:, None] >= tile_end[None, :], axis=1)
    gid = jnp.arange(N_GROUPS)
    later = jnp.where(jnp.logical_and(gid[None, :] > gid[:, None], tiles_g[None, :] > 0), gid[None, :], N_GROUPS)
    next_of = jnp.min(later, axis=1)
    next_of = jnp.where(next_of == N_GROUPS, gid, next_of)
    next_grp = jnp.sum(jnp.where(grp[:, None] == gid[None, :], next_of[None, :], 0), axis=1)
    i32 = lambda a: a.astype(jnp.int32)
    return i32(src), i32(dst), i32(grp), i32(next_grp), i32(n_tiles).reshape(1), i32(used)


def _moe(plan, rows_local, w1g, w3g, w2g):
    src, dst, grp, next_grp, n_tiles, used = plan
    D = w2g.shape[3]
    n_rows, cols = rows_local.shape
    n_chunks = n_rows // CHUNK
    hbm = pl.BlockSpec(memory_space=pl.ANY)
    group_weights = [w1g, w3g, w2g]
    y = pl.pallas_call(
        _moe_kernel,
        grid_spec=pltpu.PrefetchScalarGridSpec(
            num_scalar_prefetch=6,
            grid=(grp.shape[0],),
            in_specs=[hbm, hbm, hbm, hbm],
            out_specs=hbm,
            scratch_shapes=([pltpu.VMEM((2, TILE_CHUNKS, CHUNK, cols), BF16),
                             pltpu.VMEM((2, TILE_CHUNKS, CHUNK, D), BF16),
                             pltpu.VMEM((1, CHUNK, D), BF16)]
                            + [pltpu.VMEM(w.shape[1:], F32) for w in group_weights]
                            + [pltpu.VMEM(w.shape[1:], BF16) for w in group_weights]
                            + [pltpu.SemaphoreType.DMA((2,)), pltpu.SemaphoreType.DMA((2,)),
                               pltpu.SemaphoreType.DMA(()), pltpu.SemaphoreType.DMA(())])),
        out_shape=jax.ShapeDtypeStruct((n_chunks + 2 * TILE_CHUNKS, CHUNK, D), BF16),
        compiler_params=_params("arbitrary"),
        name="moe",
    )(src, dst, grp, next_grp, n_tiles, used, rows_local.reshape(n_chunks, CHUNK, cols), w1g, w3g, w2g)
    return y.reshape((n_chunks + 2 * TILE_CHUNKS) * CHUNK, D)


def _combine(x1, mod, lpos, y_local, *, B, T):
    N, D = x1.shape
    L = COMBINE_BLOCKS * WINDOW
    subs = B * L // MOE_TILE
    return pl.pallas_call(
        _combine_kernel,
        grid=(T // L,),
        in_specs=[pl.BlockSpec((B * L, D), lambda i: (i, 0)),
                  pl.BlockSpec(mod.shape, lambda i: (0, 0, 0)),
                  pl.BlockSpec((subs, 8, MOE_TILE), lambda i: (i, 0, 0)),
                  pl.BlockSpec((subs * LOCAL_ROWS, D), lambda i: (i, 0))],
        out_specs=pl.BlockSpec((B, L, D), lambda i: (0, i, 0)),
        out_shape=jax.ShapeDtypeStruct((B, T, D), F32),
        compiler_params=_params("arbitrary"),
        name="moe_combine",
    )(x1, mod, lpos, y_local)


def _block_diag(n, blk, value, dtype):
    r = np.arange(n)[:, None] // blk
    c = np.arange(n)[None, :] // blk
    return jnp.asarray(np.where(r == c, value, 0.0), dtype)


def _gla_constants():
    L = GLA_CHUNK
    i = np.arange(L)[:, None]
    j = np.arange(L)[None, :]
    tri = j <= i
    bdtri = np.logical_and(j <= i, i // GLA_SUB == j // GLA_SUB)
    cum = np.block([[tri, tri], [bdtri, bdtri]])
    bdms = _block_diag(2 * GLA_DV, GLA_DV, 1.0 / GLA_DV, BF16)
    d = np.arange(LANES)[:, None] // GLA_DK
    e = np.arange(2 * GLA_DV)[None, :] // GLA_DV
    smask = d == e
    return jnp.asarray(cum, BF16), bdms, jnp.asarray(smask, F32)


def kernel(x, c, w_ada, b_ada, g_norm1, w_in, q_norm, k_norm, sinks, w_gk2, b_gk, g_gla_out, g_att_out,
           w_out, g_norm2, w_group, b_group, w_router, b_router, w1, w3, w2):
    B, T, D = x.shape
    N = B * T
    depth = w_ada.shape[0]
    cum, bdms, smask = _gla_constants()
    bdq = _block_diag(2 * LANES, HEAD_DIM, 1.0 / HEAD_DIM, BF16)
    bdk = _block_diag(ATT_KV, HEAD_DIM, 1.0 / HEAD_DIM, BF16)

    for l in range(depth):
        mod = _adaln_mod(c, w_ada[l], b_ada[l])

        pad = LANES - N_GROUPS - N_EXPERTS
        wr_t = jnp.concatenate([w_group[l], w_router[l], jnp.zeros((D, pad), F32)], axis=1).T.astype(BF16)
        br_t = jnp.concatenate([b_group[l], b_router[l], jnp.zeros((pad,), F32)]).reshape(LANES, 1)
        striu = jnp.asarray(np.arange(MOE_TILE)[:, None] < np.arange(MOE_TILE)[None, :], BF16)
        x1, rows_local, lpos, cnt = _mix_out(
            sinks[l], x, mod, g_norm1[l].reshape(1, D), w_in[l].T,
            q_norm[l].reshape(1, HEAD_DIM), k_norm[l].reshape(1, HEAD_DIM),
            bdq, bdk, w_gk2[l], b_gk[l].reshape(1, GLA_K),
            g_att_out[l].reshape(1, ATT_Q), cum, bdms, smask, g_gla_out[l].reshape(1, GLA_DV),
            w_out[l], g_norm2[l].reshape(1, D), wr_t, br_t, striu)
        plan = _moe_plan(cnt[:, :N_GROUPS, 0].astype(jnp.int32))
        by_group = lambda w: w.reshape((N_GROUPS, EXPERTS_PER_GROUP) + w.shape[1:])
        y_local = _moe(plan, rows_local, by_group(w1[l]), by_group(w3[l]), by_group(w2[l]))
        x = _combine(x1, mod, lpos, y_local, B=B, T=T)
    return x
```

```python
import jax
import jax.numpy as jnp
import numpy as np
from jax import lax
from jax.experimental import pallas as pl
from jax.experimental.pallas import tpu as pltpu

F32 = jnp.float32
BF16 = jnp.bfloat16

EPS = 1e-6
LOG2_E = 1.4426950408889634
ATT_HEADS = 8
ATT_KV_HEADS = 2
HEAD_DIM = 64
WINDOW = 128
ATT_Q = ATT_HEADS * HEAD_DIM
ATT_KV = ATT_KV_HEADS * HEAD_DIM
GLA_HEADS = 4
GLA_DK = 64
GLA_DV = 128
GLA_RANK = 16
GLA_NORMALIZER = 16.0
GLA_K = GLA_HEADS * GLA_DK
GLA_V = GLA_HEADS * GLA_DV
N_GROUPS = 4
EXPERTS_PER_GROUP = 4
N_EXPERTS = N_GROUPS * EXPERTS_PER_GROUP

LANES = 128
PROJ_SPLITS = (512, 512)
PROJ_TILE = sum(PROJ_SPLITS)
COMBINE_BLOCKS = 2
ROUTER_ROWS = 24
GLA_CHUNK = 128
GLA_SUB = 16
N_SUB = GLA_CHUNK // GLA_SUB
ROUTER_LANE0 = N_GROUPS
VMEM_LIMIT = 62 * 1024 * 1024
MOE_TILE = 256
EXPERT_TILE = 512
CHUNK = 16
TILE_CHUNKS = EXPERT_TILE // CHUNK
LOCAL_CHUNKS = (MOE_TILE + N_GROUPS * (CHUNK - 1)) // CHUNK + 2
LOCAL_ROWS = LOCAL_CHUNKS * CHUNK

_QA0, _KA0, _VA0 = 0, ATT_Q, ATT_Q + ATT_KV
_QG0 = _VA0 + ATT_KV
_KG0 = _QG0 + GLA_K
_VG0 = _KG0 + GLA_K
_OG0 = _VG0 + GLA_V
_LR0 = _OG0 + GLA_V
IN_COLS_PAD = _LR0 + LANES


def _dot(a, b):
    return jnp.dot(a, b, preferred_element_type=F32)


def _dot_nt(a, b):
    return lax.dot_general(a, b, (((1,), (1,)), ((), ())), preferred_element_type=F32)


def _sigmoid(x):
    return 1.0 / (1.0 + jnp.exp(-x))


def _params(*sem):
    return pltpu.CompilerParams(dimension_semantics=sem, vmem_limit_bytes=VMEM_LIMIT)


def _mod_kernel(c_ref, w_ref, b_ref, o_ref):
    c = c_ref[...]
    s = (c * _sigmoid(c)).astype(BF16)
    o_ref[0] = _dot(s, w_ref[...].astype(BF16)) + b_ref[...]


def _adaln_mod(c, w_ada, b_ada):
    B, D = c.shape
    n = w_ada.shape[1]
    tn = D
    return pl.pallas_call(
        _mod_kernel,
        grid=(n // tn,),
        in_specs=[pl.BlockSpec((B, D), lambda j: (0, 0)),
                  pl.BlockSpec((D, tn), lambda j: (0, j)),
                  pl.BlockSpec((1, tn), lambda j: (0, j))],
        out_specs=pl.BlockSpec((1, B, tn), lambda j: (j, 0, 0)),
        out_shape=jax.ShapeDtypeStruct((n // tn, B, tn), F32),
        compiler_params=_params("arbitrary"),
        name="adaln_mod",
    )(c, w_ada, b_ada.reshape(1, n))


def _round_w_in(wf_ref, w_ref, wof_ref, wo_ref, stage_ref, sem):
    @pl.when(pl.program_id(1) == 0)
    def _():
        fetch = pltpu.make_async_copy(wof_ref, stage_ref, sem)
        fetch.start()
        lr_src = _OG0
        w_ref[0:_OG0, :] = wf_ref[0:_OG0, :].astype(BF16)
        w_ref[_OG0:_LR0, :] = wf_ref[lr_src + GLA_RANK:lr_src + GLA_RANK + GLA_V, :].astype(BF16)
        w_ref[_LR0:_LR0 + GLA_RANK, :] = wf_ref[lr_src:lr_src + GLA_RANK, :].astype(BF16)
        w_ref[_LR0 + GLA_RANK:IN_COLS_PAD, :] = jnp.zeros((LANES - GLA_RANK, wf_ref.shape[1]), BF16)
        fetch.wait()
        wo_ref[...] = stage_ref[...].astype(BF16)


def _inproj_stages(subs, x_ref, mod_ref, g1_ref, qn_ref, kn_ref, bdq_ref, bdk_ref, wgk_ref, bgk_ref,
                   qa_ref, ka_ref, va_ref, qg_ref, kg_ref, vg_ref, la_ref, og_ref, w_ref):
    blk = x_ref.shape[1]
    seqs_of = [range(sum(PROJ_SPLITS[:t]) // blk, sum(PROJ_SPLITS[:t + 1]) // blk)
               for t in range(len(PROJ_SPLITS))]

    def put(ref, t, value, lanes=slice(None)):
        for i, s in enumerate(seqs_of[t]):
            ref[s, :, lanes] = value[i * blk:(i + 1) * blk, :]

    h = {}
    for t in subs:
        parts = []
        for s in seqs_of[t]:
            x = x_ref[s]
            ms = jnp.mean(x * x, axis=-1, keepdims=True)
            xn = x * lax.rsqrt(ms + EPS) * g1_ref[...]
            parts.append((xn * (1.0 + mod_ref[1, s:s + 1, :]) + mod_ref[0, s:s + 1, :]).astype(BF16))
        h[t] = jnp.concatenate(parts, axis=0)
    yield

    proj = lambda t, c0, width: _dot_nt(h[t], w_ref[c0:c0 + width, :])
    wgk = jnp.concatenate([wgk_ref[...].astype(BF16), jnp.zeros((LANES - GLA_RANK, GLA_K), BF16)], axis=0)
    gate_of = lambda lr: _dot(lr.astype(BF16), wgk) + bgk_ref[...]
    qa, kv, q_ms, k_ms, qk_g, vg, og, lr, gate = ({} for _ in range(9))
    prev = None
    for t in subs:
        qa[t] = proj(t, _QA0, ATT_Q)
        if prev is not None:
            gate[prev] = gate_of(lr[prev])
        prev = t
        yield
        kv[t] = proj(t, _KA0, 2 * ATT_KV)
        q_sq = (qa[t] * qa[t]).astype(BF16)
        q_ms[t] = jnp.concatenate([_dot(q_sq[:, c:c + 2 * LANES], bdq_ref[...])
                                   for c in range(0, ATT_Q, 2 * LANES)], axis=1)
        yield
        qk_g[t] = proj(t, _QG0, 2 * GLA_K)
        k = kv[t][:, 0:ATT_KV]
        k_ms[t] = _dot((k * k).astype(BF16), bdk_ref[...])
        yield
        vg[t] = proj(t, _VG0, GLA_V)
        yield
        og[t] = proj(t, _OG0, GLA_V)
        lr[t] = proj(t, _LR0, LANES)
        yield
    gate[prev] = gate_of(lr[prev])
    yield

    for t in subs:
        low = lax.broadcasted_iota(jnp.int32, (PROJ_SPLITS[t], ATT_KV), 1) < HEAD_DIM
        qn = jnp.concatenate([qn_ref[...]] * ATT_HEADS, axis=1)
        kn = jnp.concatenate([kn_ref[...]] * ATT_KV_HEADS, axis=1)
        put(qa_ref, t, (qa[t] * lax.rsqrt(q_ms[t] + EPS) * qn * (HEAD_DIM ** -0.5 * LOG2_E)).astype(BF16))
        k = kv[t][:, 0:ATT_KV] * lax.rsqrt(k_ms[t] + EPS) * kn
        v = kv[t][:, ATT_KV:2 * ATT_KV]
        for src, dst in ((k, ka_ref), (v, va_ref)):
            swapped = pltpu.roll(src, HEAD_DIM, axis=1)
            put(dst, t, jnp.where(low, src, swapped).astype(BF16), slice(0, LANES))
            put(dst, t, jnp.where(low, swapped, src).astype(BF16), slice(LANES, 2 * LANES))
        put(qg_ref, t, (qk_g[t][:, 0:GLA_K] * (GLA_DK ** -0.5)).astype(BF16))
        put(kg_ref, t, qk_g[t][:, GLA_K:2 * GLA_K].astype(BF16))
        put(vg_ref, t, vg[t].astype(BF16))
        put(og_ref, t, (og[t] * _sigmoid(og[t])).astype(BF16))
        log_sig = jnp.minimum(gate[t], 0.0) - jnp.log(1.0 + jnp.exp(-jnp.abs(gate[t])))
        put(la_ref, t, log_sig * (1.0 / GLA_NORMALIZER))


def _attn_stages(seqs, sinks_ref, q_ref, kc_ref, vc_ref, gatt_ref, o_ref, kp_ref, vp_ref):
    blk = WINDOW
    first = pl.program_id(1) == 0
    qi = lax.broadcasted_iota(jnp.int32, (blk, blk), 0)
    cj = lax.broadcasted_iota(jnp.int32, (blk, blk), 1)
    from_prev = cj > qi
    dist = (qi - cj + jnp.where(from_prev, blk, 0)).astype(F32)
    no_prev = jnp.where(jnp.logical_and(from_prev, first), -1e30, 0.0)
    low = cj < HEAD_DIM
    half = (jnp.where(low, 1.0, 0.0).astype(BF16), jnp.where(low, 0.0, 1.0).astype(BF16))
    half2 = tuple(jnp.concatenate([m, m], axis=0) for m in half)
    prev_mask = jnp.where(from_prev, 1.0, 0.0).astype(BF16)
    cur_mask = jnp.where(from_prev, 0.0, 1.0).astype(BF16)

    n_pairs = ATT_HEADS // 2
    pairs_per_kv = n_pairs // ATT_KV_HEADS
    units = [(bi, j) for bi in seqs for j in range(n_pairs)]

    def kv_blocks(bi, g):
        lanes = slice(g * LANES, (g + 1) * LANES)
        return (kp_ref[bi, :, lanes], vp_ref[bi, :, lanes]), (kc_ref[bi, :, lanes], vc_ref[bi, :, lanes])

    scores = {}
    for bi in seqs:
        for g in range(ATT_KV_HEADS):
            (kp, _), (kc, _) = kv_blocks(bi, g)
            k_both = jnp.concatenate([kp, kc], axis=0)
            group = range(g * pairs_per_kv, (g + 1) * pairs_per_kv)
            q_stack = jnp.concatenate([q_ref[bi, :, j * LANES:(j + 1) * LANES] for j in group], axis=0)
            for p in range(2):
                s_stack = _dot_nt(q_stack, k_both * half2[p])
                for jj, j in enumerate(group):
                    scores[bi, j, p] = s_stack[jj * blk:(jj + 1) * blk, :]
    yield

    bias = [2.0 ** (-8.0 * (h + 1) / ATT_HEADS) * LOG2_E * dist - no_prev for h in range(ATT_HEADS)]
    probs, sink_terms = {}, {}
    for bi, j in units:
        for p in range(2):
            h = 2 * j + p
            s_both = scores[bi, j, p]
            s = jnp.where(from_prev, s_both[:, 0:blk], s_both[:, blk:2 * blk]) - bias[h]
            sink = sinks_ref[h] * LOG2_E
            m = jnp.maximum(jnp.max(s, axis=-1, keepdims=True), sink)
            probs[bi, j, p] = jnp.exp2(s - m)
            sink_terms[bi, j, p] = jnp.exp2(sink - m)
    yield

    outs = {}
    for bi in seqs:
        for g in range(ATT_KV_HEADS):
            (_, vp), (_, vc) = kv_blocks(bi, g)
            v_stack = jnp.concatenate([jnp.concatenate([v * half[p], half[p]], axis=1)
                                       for v in (vp, vc) for p in range(2)], axis=0)
            group = range(g * pairs_per_kv, (g + 1) * pairs_per_kv)
            p_stack = []
            for j in group:
                e = [probs[bi, j, p].astype(BF16) for p in range(2)]
                p_stack.append(jnp.concatenate([x * m for m in (prev_mask, cur_mask) for x in e], axis=1))
            pv_stack = _dot(jnp.concatenate(p_stack, axis=0), v_stack)
            for jj, j in enumerate(group):
                pv = pv_stack[jj * blk:(jj + 1) * blk, :]
                den = pv[:, LANES:2 * LANES] + jnp.where(low, sink_terms[bi, j, 0], sink_terms[bi, j, 1])
                outs[bi, j] = pv[:, 0:LANES] / den
    yield

    for bi in seqs:
        o = jnp.concatenate([outs[bi, j] for j in range(n_pairs)], axis=1)
        ms = jnp.mean(o * o, axis=-1, keepdims=True)
        o_ref[bi] = (o * lax.rsqrt(ms + EPS) * gatt_ref[...]).astype(BF16)
        kp_ref[bi] = kc_ref[bi]
        vp_ref[bi] = vc_ref[bi]


def _gla_stages(seqs, q_ref, k_ref, v_ref, la_ref, og_ref, cum_ref, bdms_ref, smask_ref,
                ggla_ref, o_ref, state_ref):
    L = GLA_CHUNK
    lane = lax.broadcasted_iota(jnp.int32, (GLA_SUB, LANES), 1)
    head_mask = [jnp.where(lane // GLA_DK == hh, 1.0, 0.0).astype(BF16) for hh in range(2)]
    causal = lax.broadcasted_iota(jnp.int32, (L, L), 0) >= lax.broadcasted_iota(jnp.int32, (L, L), 1)
    units = [(s, pair) for s in seqs for pair in range(GLA_HEADS // 2)]
    kl = lambda pair: slice(pair * LANES, (pair + 1) * LANES)
    vl = lambda pair: slice(pair * 2 * GLA_DV, (pair + 1) * 2 * GLA_DV)

    b, b_in = {}, {}
    for s in seqs:
        la = la_ref[s]
        la_hi = la.astype(BF16)
        la_lo = (la - la_hi.astype(F32)).astype(BF16)
        sums = _dot(cum_ref[...], jnp.concatenate([la_hi, la_lo], axis=0)) * LOG2_E
        b[s] = sums[0:L, :]
        b_in[s] = sums[L:2 * L, :]
    yield

    q_both, keys, q_dec, k_dec_t, b_last = {}, {}, {}, {}, {}
    for s in seqs:
        ref = b[s] - b_in[s]
        b_last[s] = b[s][L - 1:L, :]
        q = q_ref[s].astype(F32)
        k = k_ref[s].astype(F32)
        q_in = (q * jnp.exp2(b_in[s])).astype(BF16)
        q_dec[s] = (q * jnp.exp2(b[s])).astype(BF16)
        k_dec = k * jnp.exp2(b_last[s] - b[s])
        for pair in range(GLA_HEADS // 2):
            k_p, b_p, ref_p = k[:, kl(pair)], b[s][:, kl(pair)], ref[:, kl(pair)]
            expanded = []
            for g in range(N_SUB):
                top = (g + 1) * GLA_SUB
                live = (k_p[0:top, :] * jnp.exp2(ref_p[g * GLA_SUB:g * GLA_SUB + 1, :] - b_p[0:top, :])).astype(BF16)
                expanded.append(live if top == L else
                                jnp.concatenate([live, jnp.zeros((L - top, LANES), BF16)], axis=0))
            keys[s, pair] = jnp.concatenate(expanded, axis=1)
            q_p = q_in[:, kl(pair)]
            zero_group = jnp.zeros((GLA_SUB, LANES), BF16)
            q_both[s, pair] = []
            for t in range(N_SUB // 2):
                lhs = []
                for hh in range(2):
                    for side in range(2):
                        g = 2 * t + side
                        piece = q_p[g * GLA_SUB:(g + 1) * GLA_SUB, :] * head_mask[hh]
                        lhs.append(jnp.concatenate([piece, zero_group] if side == 0 else [zero_group, piece], axis=1))
                q_both[s, pair].append(jnp.concatenate(lhs, axis=0))
            k_dec_t[s, pair] = k_dec[:, kl(pair)].T.astype(BF16)
    yield

    scores = {(u, t): _dot_nt(q_both[u][t], keys[u][:, 2 * t * LANES:2 * (t + 1) * LANES])
              for u in units for t in range(N_SUB // 2)}
    yield

    outs, updates = {}, {}
    for s, pair in units:
        v_p = v_ref[s, :, vl(pair)]
        o_parts = []
        for hh in range(2):
            a = jnp.concatenate([scores[(s, pair), t][2 * hh * GLA_SUB:2 * (hh + 1) * GLA_SUB, :]
                                 for t in range(N_SUB // 2)], axis=0)
            a = jnp.where(causal, a, 0.0).astype(BF16)
            o_parts.append(_dot(a, v_p[:, hh * GLA_DV:(hh + 1) * GLA_DV]))
        state = state_ref[s, pair]
        outs[s, pair] = jnp.concatenate(o_parts, axis=1) + _dot(q_dec[s][:, kl(pair)], state.astype(BF16))
        updates[s, pair] = _dot(k_dec_t[s, pair], v_p)
    yield

    for s, pair in units:
        decay = jnp.broadcast_to(jnp.exp2(b_last[s][:, kl(pair)]), (LANES, LANES)).T
        state_ref[s, pair] = (state_ref[s, pair] * jnp.concatenate([decay, decay], axis=1)
                              + updates[s, pair] * smask_ref[...])
        o = outs[s, pair]
        ms = _dot((o * o).astype(BF16), bdms_ref[...])
        gain = jnp.concatenate([ggla_ref[...]] * 2, axis=1)
        y = o * lax.rsqrt(ms + EPS) * gain * og_ref[s, :, vl(pair)].astype(F32)
        o_ref[s, :, vl(pair)] = y.astype(BF16)


def _run_interleaved(*stage_generators):
    pending = list(stage_generators)
    while pending:
        for stages in list(pending):
            if next(stages, "done") == "done":
                pending.remove(stages)


def _route(logits_t):
    lt = logits_t[0:ROUTER_ROWS, :]
    row = lax.broadcasted_iota(jnp.int32, lt.shape, 0)
    neg_inf = -jnp.inf
    g_log = jnp.where(row < N_GROUPS, lt, neg_inf)
    g_max = jnp.max(g_log, axis=0, keepdims=True)
    g_sel = jnp.min(jnp.where(g_log == g_max, row, LANES), axis=0, keepdims=True)
    g_sum = jnp.sum(jnp.where(row < N_GROUPS, jnp.exp(lt - g_max), 0.0), axis=0, keepdims=True)
    p_group = 1.0 / g_sum
    e_lo = ROUTER_LANE0 + EXPERTS_PER_GROUP * g_sel
    in_group = jnp.logical_and(row >= e_lo, row < e_lo + EXPERTS_PER_GROUP)
    e_log = jnp.where(in_group, lt, neg_inf)
    e_max = jnp.max(e_log, axis=0, keepdims=True)
    top1 = jnp.min(jnp.where(e_log == e_max, row, LANES), axis=0, keepdims=True)
    e_log2 = jnp.where(row == top1, neg_inf, e_log)
    e_max2 = jnp.max(e_log2, axis=0, keepdims=True)
    top2 = jnp.min(jnp.where(e_log2 == e_max2, row, LANES), axis=0, keepdims=True)
    ratio = jnp.exp(e_max2 - e_max)
    w_top1 = p_group / (1.0 + ratio)
    w_top2 = p_group * ratio / (1.0 + ratio)
    row8 = lax.broadcasted_iota(jnp.int32, (8, lt.shape[1]), 0)
    weights = jnp.where(row8 == top1 - e_lo, w_top1, 0.0) + jnp.where(row8 == top2 - e_lo, w_top2, 0.0)
    return g_sel, weights


def _outproj_stages(subs, ya_ref, yg_ref, x_ref, mod_ref, wo_ref, g2_ref, wrt_ref, brt_ref, striu_ref,
                    x1_ref, row_ref, lpos_ref, cnt_ref):
    tm = MOE_TILE
    half = x_ref.shape[1]
    per_tile = tm // half
    tile_of = lambda ref, t: jnp.concatenate([ref[t * per_tile + i] for i in range(per_tile)], axis=0)

    mix = {t: _dot(tile_of(ya_ref, t), wo_ref[0:ATT_Q, :]) + _dot(tile_of(yg_ref, t), wo_ref[ATT_Q:ATT_Q + GLA_V, :])
           for t in subs}
    yield
    h2b = {}
    for t in subs:
        parts = []
        for i in range(per_tile):
            s = t * per_tile + i
            x1 = x_ref[s] + mod_ref[2, s:s + 1, :] * mix[t][i * half:(i + 1) * half, :]
            x1_ref[t * tm + i * half:t * tm + (i + 1) * half, :] = x1
            ms = jnp.mean(x1 * x1, axis=-1, keepdims=True)
            h2 = (x1 * lax.rsqrt(ms + EPS) * g2_ref[...]) * (1.0 + mod_ref[4, s:s + 1, :]) + mod_ref[3, s:s + 1, :]
            parts.append(h2.astype(BF16))
        h2b[t] = jnp.concatenate(parts, axis=0)
    yield
    logits_t = {t: _dot_nt(wrt_ref[...], h2b[t]) + brt_ref[...] for t in subs}
    yield

    routed = {t: _route(logits_t[t]) for t in subs}
    row8 = lax.broadcasted_iota(jnp.int32, (8, tm), 0)
    onehot = {t: jnp.where(row8 == routed[t][0], 1.0, 0.0) for t in subs}
    yield
    before = {t: _dot(onehot[t].astype(BF16), striu_ref[...]) for t in subs}
    yield

    local_row = lax.broadcasted_iota(jnp.int32, (LOCAL_ROWS, tm), 0).astype(F32)
    pad_rows = jnp.zeros((LANES - 8, tm), F32)
    for t in subs:
        count = jnp.sum(onehot[t], axis=1, keepdims=True)
        cnt_ref[t] = jnp.broadcast_to(count, (8, LANES))
        padded = jnp.broadcast_to(jnp.floor((count + (CHUNK - 1.0)) * (1.0 / CHUNK)) * CHUNK, (8, tm))
        start = jnp.zeros((8, tm), F32)
        for shift in range(1, N_GROUPS):
            start = start + jnp.where(row8 >= shift, pltpu.roll(padded, shift, axis=0), 0.0)
        lpos = jnp.sum(onehot[t] * (before[t] + start), axis=0, keepdims=True)
        lpos_ref[t] = jnp.broadcast_to(lpos, (8, tm))
        weights = jnp.concatenate([routed[t][1], pad_rows], axis=0).T
        w_hi = weights.astype(BF16)
        w_lo = (weights - w_hi.astype(F32)).astype(BF16)
        perm = jnp.where(local_row == lpos, 1.0, 0.0).astype(BF16)
        row_ref[t * LOCAL_ROWS:(t + 1) * LOCAL_ROWS, :] = _dot(
            perm, jnp.concatenate([h2b[t], w_hi, w_lo], axis=1)).astype(BF16)


def _mix_out_kernel(sinks_ref, x_ref, mod_ref, g1_ref, wf_ref, qn_ref, kn_ref, bdq_ref, bdk_ref, wgk_ref, bgk_ref,
                    gatt_ref, cum_ref, bdms_ref, smask_ref, ggla_ref,
                    wof_ref, g2_ref, wrt_ref, brt_ref, striu_ref,
                    x1_ref, row_ref, lpos_ref, cnt_ref,
                    w_ref, wo_ref, qa_ref, ka_ref, va_ref, qg_ref, kg_ref, vg_ref, la_ref, og_ref,
                    ya_ref, yg_ref, kp_ref, vp_ref, state_ref, w_sem):
    assert x1_ref.shape == wof_ref.shape and x1_ref.dtype == wof_ref.dtype
    _round_w_in(wf_ref, w_ref, wof_ref, wo_ref, x1_ref, w_sem)
    n_seq = x_ref.shape[0]
    seqs_per_tile = MOE_TILE // x_ref.shape[1]
    halves = []
    for first in (0, n_seq // 2):
        seqs = tuple(range(first, first + n_seq // 2))
        halves.append((seqs, (first * len(PROJ_SPLITS) // n_seq,),
                       tuple(range(first // seqs_per_tile, (first + n_seq // 2) // seqs_per_tile))))

    def inproj(sub_tiles):
        return _inproj_stages(sub_tiles, x_ref, mod_ref, g1_ref, qn_ref, kn_ref, bdq_ref, bdk_ref, wgk_ref, bgk_ref,
                              qa_ref, ka_ref, va_ref, qg_ref, kg_ref, vg_ref, la_ref, og_ref, w_ref)

    def mixers(seqs):
        return (_attn_stages(seqs, sinks_ref, qa_ref, ka_ref, va_ref, gatt_ref, ya_ref, kp_ref, vp_ref),
                _gla_stages(seqs, qg_ref, kg_ref, vg_ref, la_ref, og_ref, cum_ref, bdms_ref, smask_ref,
                            ggla_ref, yg_ref, state_ref))

    def outproj(tiles):
        return _outproj_stages(tiles, ya_ref, yg_ref, x_ref, mod_ref, wo_ref, g2_ref, wrt_ref, brt_ref, striu_ref,
                               x1_ref, row_ref, lpos_ref, cnt_ref)

    (seqs_a, subs_a, tiles_a), (seqs_b, subs_b, tiles_b) = halves
    _run_interleaved(inproj(subs_a + subs_b))

    @pl.when(pl.program_id(1) == 0)
    def _():
        state_ref[...] = jnp.zeros_like(state_ref)
        kp_ref[...] = jnp.zeros_like(kp_ref)
        vp_ref[...] = jnp.zeros_like(vp_ref)

    first_half = mixers(seqs_a)
    for stages in first_half:
        next(stages)
    _run_interleaved(*first_half, *mixers(seqs_b))
    _run_interleaved(outproj(tiles_a + tiles_b))


def _mix_out(sinks, x, mod, g1, w_in_t, qn, kn, bdq, bdk, wgk, bgk, gatt, cum, bdms, smask, ggla,
             wo, g2, wr, br, striu):
    assert GLA_CHUNK == WINDOW and MOE_TILE % WINDOW == 0
    B, T, D = x.shape
    L = WINDOW
    N = B * T
    assert B * L == PROJ_TILE
    tiles_per_step = B * L // MOE_TILE
    const = lambda a: pl.BlockSpec(a.shape, lambda b, i: (0,) * a.ndim)
    tiles = lambda *shape: pl.BlockSpec((tiles_per_step,) + shape, lambda b, i: (i,) + (0,) * len(shape))
    act = lambda c, dt=BF16: pltpu.VMEM((B, L, c), dt)
    return pl.pallas_call(
        _mix_out_kernel,
        grid=(1, T // L),
        in_specs=[pl.BlockSpec(memory_space=pltpu.SMEM),
                  pl.BlockSpec((B, L, D), lambda b, i: (0, i, 0)), const(mod), const(g1),
                  pl.BlockSpec(w_in_t.shape, lambda b, i: (0, 0), pipeline_mode=pl.Buffered(1)),
                  const(qn), const(kn), const(bdq), const(bdk), const(wgk), const(bgk),
                  const(gatt), const(cum), const(bdms), const(smask), const(ggla),
                  pl.BlockSpec(memory_space=pl.ANY), const(g2), const(wr), const(br), const(striu)],
        out_specs=[pl.BlockSpec((B * L, D), lambda b, i: (i, 0)),
                   pl.BlockSpec((tiles_per_step * LOCAL_ROWS, D + 2 * LANES), lambda b, i: (i, 0)),
                   tiles(8, MOE_TILE), tiles(8, LANES)],
        out_shape=[jax.ShapeDtypeStruct((N, D), F32),
                   jax.ShapeDtypeStruct((N // MOE_TILE * LOCAL_ROWS, D + 2 * LANES), BF16),
                   jax.ShapeDtypeStruct((N // MOE_TILE, 8, MOE_TILE), F32),
                   jax.ShapeDtypeStruct((N // MOE_TILE, 8, LANES), F32)],
        scratch_shapes=[pltpu.VMEM((IN_COLS_PAD, D), BF16), pltpu.VMEM(wo.shape, BF16),
                        act(ATT_Q), act(2 * ATT_KV), act(2 * ATT_KV), act(GLA_K), act(GLA_K), act(GLA_V),
                        act(GLA_K, F32), act(GLA_V),
                        act(ATT_Q), act(GLA_V), act(2 * ATT_KV), act(2 * ATT_KV),
                        pltpu.VMEM((B, GLA_HEADS // 2, LANES, 2 * GLA_DV), F32),
                        pltpu.SemaphoreType.DMA(())],
        compiler_params=_params("arbitrary", "arbitrary"),
        name="mix_out",
    )(sinks, x, mod, g1, w_in_t, qn, kn, bdq, bdk, wgk, bgk, gatt, cum, bdms, smask, ggla,
      wo, g2, wr, br, striu)


def _chunk_copy(src_ref, src_chunk, dst_ref, dst_chunk, sem):
    return pltpu.make_async_copy(src_ref.at[src_chunk], dst_ref.at[dst_chunk], sem)


def _moe_kernel(src_ref, dst_ref, grp_ref, next_ref, nt_ref, used_ref, rows_ref, w1f_ref, w3f_ref, w2f_ref, y_ref,
                in_buf, out_buf, zero_buf, st1_ref, st3_ref, st2_ref, w1_ref, w3_ref, w2_ref,
                in_sem, out_sem, zero_sem, w_sem):
    j = pl.program_id(0)
    n_tiles = nt_ref[0]
    d_model = w2_ref.shape[2]

    stages = ((w1f_ref, st1_ref, w1_ref), (w3f_ref, st3_ref, w3_ref), (w2f_ref, st2_ref, w2_ref))

    def fetch_weights(group):
        for hbm, stage, _ in stages:
            pltpu.make_async_copy(hbm.at[group], stage, w_sem).start(priority=1)

    def enter_group():
        for hbm, stage, dst in stages:
            pltpu.make_async_copy(hbm.at[0], stage, w_sem).wait()
        for hbm, stage, dst in stages:
            for k in range(EXPERTS_PER_GROUP):
                dst[k] = stage[k].astype(BF16)

        @pl.when(next_ref[j] != grp_ref[j])
        def _():
            fetch_weights(next_ref[j])

    def gather(tile, slot):
        def body(k, carry):
            _chunk_copy(rows_ref, src_ref[tile * TILE_CHUNKS + k], in_buf.at[slot], k, in_sem.at[slot]).start()
            return carry
        lax.fori_loop(0, TILE_CHUNKS, body, 0, unroll=True)

    def wait_gather(slot):
        def body(k, carry):
            _chunk_copy(rows_ref, 0, in_buf.at[slot], k, in_sem.at[slot]).wait()
            return carry
        lax.fori_loop(0, TILE_CHUNKS, body, 0, unroll=True)

    def scatter(tile, slot):
        for k in range(TILE_CHUNKS):
            _chunk_copy(out_buf.at[slot], k, y_ref, dst_ref[tile * TILE_CHUNKS + k],
                        out_sem.at[slot]).start(priority=k % 2)

    def wait_scatter(slot):
        def body(k, carry):
            _chunk_copy(out_buf.at[slot], k, y_ref, 0, out_sem.at[slot]).wait()
            return carry
        lax.fori_loop(0, TILE_CHUNKS, body, 0, unroll=True)

    def zero_fill(wait):
        def per_tile(i, carry):
            def body(c, inner):
                copy = _chunk_copy(zero_buf, 0, y_ref, i * LOCAL_CHUNKS + c, zero_sem)
                if wait:
                    copy.wait()
                else:
                    copy.start()
                return inner
            return lax.fori_loop(used_ref[i], LOCAL_CHUNKS, body, carry)
        lax.fori_loop(0, used_ref.shape[0], per_tile, 0)

    @pl.when(j == 0)
    def _():
        fetch_weights(grp_ref[0])
        zero_buf[...] = jnp.zeros_like(zero_buf)
        scratch0 = used_ref.shape[0] * LOCAL_CHUNKS
        for wait in (False, True):
            for k in range(2 * TILE_CHUNKS):
                copy = _chunk_copy(zero_buf, 0, y_ref, scratch0 + k, zero_sem)
                copy.wait() if wait else copy.start()
        zero_fill(wait=False)
        gather(0, 0)

    @pl.when(jnp.logical_or(j == 0, grp_ref[j] != grp_ref[jnp.maximum(j - 1, 0)]))
    def _():
        enter_group()

    @pl.when(j + 1 < n_tiles)
    def _():
        gather(j + 1, (j + 1) % 2)

    @pl.when(j < n_tiles)
    def _():
        slot = j % 2
        wait_gather(slot)
        rows = in_buf[slot].reshape(EXPERT_TILE, in_buf.shape[-1])
        h = rows[:, 0:d_model]
        weights = rows[:, d_model:d_model + LANES].astype(F32) + rows[:, d_model + LANES:].astype(F32)
        experts = range(EXPERTS_PER_GROUP)
        up = [(_dot(h, w1_ref[k]), _dot(h, w3_ref[k])) for k in experts]
        hid = [(a * _sigmoid(a) * g * weights[:, k:k + 1]).astype(BF16) for k, (a, g) in zip(experts, up)]
        y = _dot(hid[0], w2_ref[0])
        for k in experts[1:]:
            y = y + _dot(hid[k], w2_ref[k])

        @pl.when(j >= 2)
        def _():
            wait_scatter(slot)

        out_buf[slot] = y.astype(BF16).reshape(TILE_CHUNKS, CHUNK, d_model)
        scatter(j, slot)

        @pl.when(j == n_tiles - 1)
        def _():
            @pl.when(j >= 1)
            def _():
                wait_scatter(1 - slot)
            wait_scatter(slot)
            zero_fill(wait=True)


def _combine_kernel(x1_ref, mod_ref, lpos_ref, y_ref, o_ref):
    tm = MOE_TILE
    half = WINDOW
    per_tile = tm // half
    tiles_per_block = o_ref.shape[0] // per_tile
    local_row = lax.broadcasted_iota(jnp.int32, (tm, LOCAL_ROWS), 1).astype(F32)
    for t in range(x1_ref.shape[0] // tm):
        lpos = jnp.broadcast_to(lpos_ref[t][0:1, :], (LANES, tm)).T[:, 0:1]
        unsort = jnp.where(local_row == lpos, 1.0, 0.0).astype(BF16)
        y = _dot(unsort, y_ref[t * LOCAL_ROWS:(t + 1) * LOCAL_ROWS, :])
        block, tile = divmod(t, tiles_per_block)
        for i in range(per_tile):
            s = tile * per_tile + i
            rows = slice(t * tm + i * half, t * tm + (i + 1) * half)
            o_ref[s, block * half:(block + 1) * half, :] = (
                x1_ref[rows, :] + mod_ref[5, s:s + 1, :] * y[i * half:(i + 1) * half, :])


def _moe_plan(cnt):
    n_local = cnt.shape[0]
    chunks = (cnt + CHUNK - 1) // CHUNK
    used = jnp.sum(chunks, axis=1)
    local_off = jnp.cumsum(chunks, axis=1) - chunks
    tiles_g = (jnp.sum(chunks, axis=0) + TILE_CHUNKS - 1) // TILE_CHUNKS
    tile_end = jnp.cumsum(tiles_g)
    n_tiles = tile_end[-1]
    group_start = (tile_end - tiles_g) * TILE_CHUNKS
    seg_len = chunks.T.reshape(-1)
    seg_start = (group_start[:, None] + (jnp.cumsum(chunks, axis=0) - chunks).T).reshape(-1)
    seg_src = (jnp.arange(n_local)[None, :] * LOCAL_CHUNKS + local_off.T).reshape(-1)
    max_chunks = n_local * MOE_TILE // CHUNK + n_local * N_GROUPS + N_GROUPS * TILE_CHUNKS
    max_tiles = (max_chunks + TILE_CHUNKS - 1) // TILE_CHUNKS
    c = jnp.arange(max_tiles * TILE_CHUNKS)[:, None]
    within = c - seg_start[None, :]
    hit = jnp.logical_and(within >= 0, within < seg_len[None, :])
    valid = jnp.any(hit, axis=1)
    src = jnp.sum(jnp.where(hit, seg_src[None, :] + within, 0), axis=1)
    src = jnp.where(valid, src, LOCAL_CHUNKS - 1)
    slot_k = c[:, 0] % (2 * TILE_CHUNKS)
    dst = jnp.where(valid, src, n_local * LOCAL_CHUNKS + slot_k)
    j = jnp.minimum(jnp.arange(max_tiles), n_tiles - 1)
    grp = jnp.sum(j[:, None] >= tile_end[None, :], axis=1)
    gid = jnp.arange(N_GROUPS)
    later = jnp.where(jnp.logical_and(gid[None, :] > gid[:, None], tiles_g[None, :] > 0), gid[None, :], N_GROUPS)
    next_of = jnp.min(later, axis=1)
    next_of = jnp.where(next_of == N_GROUPS, gid, next_of)
    next_grp = jnp.sum(jnp.where(grp[:, None] == gid[None, :], next_of[None, :], 0), axis=1)
    i32 = lambda a: a.astype(jnp.int32)
    return i32(src), i32(dst), i32(grp), i32(next_grp), i32(n_tiles).reshape(1), i32(used)


def _moe(plan, rows_local, w1g, w3g, w2g):
    src, dst, grp, next_grp, n_tiles, used = plan
    D = w2g.shape[3]
    n_rows, cols = rows_local.shape
    n_chunks = n_rows // CHUNK
    hbm = pl.BlockSpec(memory_space=pl.ANY)
    group_weights = [w1g, w3g, w2g]
    y = pl.pallas_call(
        _moe_kernel,
        grid_spec=pltpu.PrefetchScalarGridSpec(
            num_scalar_prefetch=6,
            grid=(grp.shape[0],),
            in_specs=[hbm, hbm, hbm, hbm],
            out_specs=hbm,
            scratch_shapes=([pltpu.VMEM((2, TILE_CHUNKS, CHUNK, cols), BF16),
                             pltpu.VMEM((2, TILE_CHUNKS, CHUNK, D), BF16),
                             pltpu.VMEM((1, CHUNK, D), BF16)]
                            + [pltpu.VMEM(w.shape[1:], F32) for w in group_weights]
                            + [pltpu.VMEM(w.shape[1:], BF16) for w in group_weights]
                            + [pltpu.SemaphoreType.DMA((2,)), pltpu.SemaphoreType.DMA((2,)),
                               pltpu.SemaphoreType.DMA(()), pltpu.SemaphoreType.DMA(())])),
        out_shape=jax.ShapeDtypeStruct((n_chunks + 2 * TILE_CHUNKS, CHUNK, D), BF16),
        compiler_params=_params("arbitrary"),
        name="moe",
    )(src, dst, grp, next_grp, n_tiles, used, rows_local.reshape(n_chunks, CHUNK, cols), w1g, w3g, w2g)
    return y.reshape((n_chunks + 2 * TILE_CHUNKS) * CHUNK, D)


def _combine(x1, mod, lpos, y_local, *, B, T):
    N, D = x1.shape
    L = COMBINE_BLOCKS * WINDOW
    subs = B * L // MOE_TILE
    return pl.pallas_call(
        _combine_kernel,
        grid=(T // L,),
        in_specs=[pl.BlockSpec((B * L, D), lambda i: (i, 0)),
                  pl.BlockSpec(mod.shape, lambda i: (0, 0, 0)),
                  pl.BlockSpec((subs, 8, MOE_TILE), lambda i: (i, 0, 0)),
                  pl.BlockSpec((subs * LOCAL_ROWS, D), lambda i: (i, 0))],
        out_specs=pl.BlockSpec((B, L, D), lambda i: (0, i, 0)),
        out_shape=jax.ShapeDtypeStruct((B, T, D), F32),
        compiler_params=_params("arbitrary"),
        name="moe_combine",
    )(x1, mod, lpos, y_local)


def _block_diag(n, blk, value, dtype):
    r = np.arange(n)[:, None] // blk
    c = np.arange(n)[None, :] // blk
    return jnp.asarray(np.where(r == c, value, 0.0), dtype)


def _gla_constants():
    L = GLA_CHUNK
    i = np.arange(L)[:, None]
    j = np.arange(L)[None, :]
    tri = j <= i
    bdtri = np.logical_and(j <= i, i // GLA_SUB == j // GLA_SUB)
    cum = np.block([[tri, tri], [bdtri, bdtri]])
    bdms = _block_diag(2 * GLA_DV, GLA_DV, 1.0 / GLA_DV, BF16)
    d = np.arange(LANES)[:, None] // GLA_DK
    e = np.arange(2 * GLA_DV)[None, :] // GLA_DV
    smask = d == e
    return jnp.asarray(cum, BF16), bdms, jnp.asarray(smask, F32)


def kernel(x, c, w_ada, b_ada, g_norm1, w_in, q_norm, k_norm, sinks, w_gk2, b_gk, g_gla_out, g_att_out,
           w_out, g_norm2, w_group, b_group, w_router, b_router, w1, w3, w2):
    B, T, D = x.shape
    N = B * T
    depth = w_ada.shape[0]
    cum, bdms, smask = _gla_constants()
    bdq = _block_diag(2 * LANES, HEAD_DIM, 1.0 / HEAD_DIM, BF16)
    bdk = _block_diag(ATT_KV, HEAD_DIM, 1.0 / HEAD_DIM, BF16)

    for l in range(depth):
        mod = _adaln_mod(c, w_ada[l], b_ada[l])

        pad = LANES - N_GROUPS - N_EXPERTS
        wr_t = jnp.concatenate([w_group[l], w_router[l], jnp.zeros((D, pad), F32)], axis=1).T.astype(BF16)
        br_t = jnp.concatenate([b_group[l], b_router[l], jnp.zeros((pad,), F32)]).reshape(LANES, 1)
        striu = jnp.asarray(np.arange(MOE_TILE)[:, None] < np.arange(MOE_TILE)[None, :], BF16)
        x1, rows_local, lpos, cnt = _mix_out(
            sinks[l], x, mod, g_norm1[l].reshape(1, D), w_in[l].T,
            q_norm[l].reshape(1, HEAD_DIM), k_norm[l].reshape(1, HEAD_DIM),
            bdq, bdk, w_gk2[l], b_gk[l].reshape(1, GLA_K),
            g_att_out[l].reshape(1, ATT_Q), cum, bdms, smask, g_gla_out[l].reshape(1, GLA_DV),
            w_out[l], g_norm2[l].reshape(1, D), wr_t, br_t, striu)
        plan = _moe_plan(cnt[:, :N_GROUPS, 0].astype(jnp.int32))
        by_group = lambda w: w.reshape((N_GROUPS, EXPERTS_PER_GROUP) + w.shape[1:])
        y_local = _moe(plan, rows_local, by_group(w1[l]), by_group(w3[l]), by_group(w2[l]))
        x = _combine(x1, mod, lpos, y_local, B=B, T=T)
    return x
```

```python
import jax
import jax.numpy as jnp
import numpy as np
from jax import lax
from jax.experimental import pallas as pl
from jax.experimental.pallas import tpu as pltpu

F32 = jnp.float32
BF16 = jnp.bfloat16

EPS = 1e-6
LOG2_E = 1.4426950408889634
ATT_HEADS = 8
ATT_KV_HEADS = 2
HEAD_DIM = 64
WINDOW = 128
ATT_Q = ATT_HEADS * HEAD_DIM
ATT_KV = ATT_KV_HEADS * HEAD_DIM
GLA_HEADS = 4
GLA_DK = 64
GLA_DV = 128
GLA_RANK = 16
GLA_NORMALIZER = 16.0
GLA_K = GLA_HEADS * GLA_DK
GLA_V = GLA_HEADS * GLA_DV
N_GROUPS = 4
EXPERTS_PER_GROUP = 4
N_EXPERTS = N_GROUPS * EXPERTS_PER_GROUP

LANES = 128
PROJ_SPLITS = (512, 512)
PROJ_TILE = sum(PROJ_SPLITS)
COMBINE_BLOCKS = 2
ROUTER_ROWS = 24
GLA_CHUNK = 128
GLA_SUB = 16
N_SUB = GLA_CHUNK // GLA_SUB
ROUTER_LANE0 = N_GROUPS
VMEM_LIMIT = 62 * 1024 * 1024
MOE_TILE = 256
EXPERT_TILE = 512
CHUNK = 16
TILE_CHUNKS = EXPERT_TILE // CHUNK
LOCAL_CHUNKS = (MOE_TILE + N_GROUPS * (CHUNK - 1)) // CHUNK + 2
LOCAL_ROWS = LOCAL_CHUNKS * CHUNK

_QA0, _KA0, _VA0 = 0, ATT_Q, ATT_Q + ATT_KV
_QG0 = _VA0 + ATT_KV
_KG0 = _QG0 + GLA_K
_VG0 = _KG0 + GLA_K
_OG0 = _VG0 + GLA_V
_LR0 = _OG0 + GLA_V
IN_COLS_PAD = _LR0 + LANES


def _dot(a, b):
    return jnp.dot(a, b, preferred_element_type=F32)


def _dot_nt(a, b):
    return lax.dot_general(a, b, (((1,), (1,)), ((), ())), preferred_element_type=F32)


def _sigmoid(x):
    return 1.0 / (1.0 + jnp.exp(-x))


def _params(*sem):
    return pltpu.CompilerParams(dimension_semantics=sem, vmem_limit_bytes=VMEM_LIMIT)


def _mod_kernel(c_ref, w_ref, b_ref, o_ref):
    c = c_ref[...]
    s = (c * _sigmoid(c)).astype(BF16)
    o_ref[0] = _dot(s, w_ref[...].astype(BF16)) + b_ref[...]


def _adaln_mod(c, w_ada, b_ada):
    B, D = c.shape
    n = w_ada.shape[1]
    tn = D
    return pl.pallas_call(
        _mod_kernel,
        grid=(n // tn,),
        in_specs=[pl.BlockSpec((B, D), lambda j: (0, 0)),
                  pl.BlockSpec((D, tn), lambda j: (0, j)),
                  pl.BlockSpec((1, tn), lambda j: (0, j))],
        out_specs=pl.BlockSpec((1, B, tn), lambda j: (j, 0, 0)),
        out_shape=jax.ShapeDtypeStruct((n // tn, B, tn), F32),
        compiler_params=_params("arbitrary"),
        name="adaln_mod",
    )(c, w_ada, b_ada.reshape(1, n))


def _round_w_in(wf_ref, w_ref, wof_ref, wo_ref, stage_ref, sem):
    @pl.when(pl.program_id(1) == 0)
    def _():
        fetch = pltpu.make_async_copy(wof_ref, stage_ref, sem)
        fetch.start()
        lr_src = _OG0
        w_ref[0:_OG0, :] = wf_ref[0:_OG0, :].astype(BF16)
        w_ref[_OG0:_LR0, :] = wf_ref[lr_src + GLA_RANK:lr_src + GLA_RANK + GLA_V, :].astype(BF16)
        w_ref[_LR0:_LR0 + GLA_RANK, :] = wf_ref[lr_src:lr_src + GLA_RANK, :].astype(BF16)
        w_ref[_LR0 + GLA_RANK:IN_COLS_PAD, :] = jnp.zeros((LANES - GLA_RANK, wf_ref.shape[1]), BF16)
        fetch.wait()
        wo_ref[...] = stage_ref[...].astype(BF16)


def _inproj_stages(subs, x_ref, mod_ref, g1_ref, qn_ref, kn_ref, bdq_ref, bdk_ref, wgk_ref, bgk_ref,
                   qa_ref, ka_ref, va_ref, qg_ref, kg_ref, vg_ref, la_ref, og_ref, w_ref):
    blk = x_ref.shape[1]
    seqs_of = [range(sum(PROJ_SPLITS[:t]) // blk, sum(PROJ_SPLITS[:t + 1]) // blk)
               for t in range(len(PROJ_SPLITS))]

    def put(ref, t, value, lanes=slice(None)):
        for i, s in enumerate(seqs_of[t]):
            ref[s, :, lanes] = value[i * blk:(i + 1) * blk, :]

    h = {}
    for t in subs:
        parts = []
        for s in seqs_of[t]:
            x = x_ref[s]
            ms = jnp.mean(x * x, axis=-1, keepdims=True)
            xn = x * lax.rsqrt(ms + EPS) * g1_ref[...]
            parts.append((xn * (1.0 + mod_ref[1, s:s + 1, :]) + mod_ref[0, s:s + 1, :]).astype(BF16))
        h[t] = jnp.concatenate(parts, axis=0)
    yield

    proj = lambda t, c0, width: _dot_nt(h[t], w_ref[c0:c0 + width, :])
    wgk = jnp.concatenate([wgk_ref[...].astype(BF16), jnp.zeros((LANES - GLA_RANK, GLA_K), BF16)], axis=0)
    gate_of = lambda lr: _dot(lr.astype(BF16), wgk) + bgk_ref[...]
    qa, kv, q_ms, k_ms, qk_g, vg, og, lr, gate = ({} for _ in range(9))
    prev = None
    for t in subs:
        qa[t] = proj(t, _QA0, ATT_Q)
        if prev is not None:
            gate[prev] = gate_of(lr[prev])
        prev = t
        yield
        kv[t] = proj(t, _KA0, 2 * ATT_KV)
        q_sq = (qa[t] * qa[t]).astype(BF16)
        q_ms[t] = jnp.concatenate([_dot(q_sq[:, c:c + 2 * LANES], bdq_ref[...])
                                   for c in range(0, ATT_Q, 2 * LANES)], axis=1)
        yield
        qk_g[t] = proj(t, _QG0, 2 * GLA_K)
        k = kv[t][:, 0:ATT_KV]
        k_ms[t] = _dot((k * k).astype(BF16), bdk_ref[...])
        yield
        vg[t] = proj(t, _VG0, GLA_V)
        yield
        og[t] = proj(t, _OG0, GLA_V)
        lr[t] = proj(t, _LR0, LANES)
        yield
    gate[prev] = gate_of(lr[prev])
    yield

    for t in subs:
        low = lax.broadcasted_iota(jnp.int32, (PROJ_SPLITS[t], ATT_KV), 1) < HEAD_DIM
        qn = jnp.concatenate([qn_ref[...]] * ATT_HEADS, axis=1)
        kn = jnp.concatenate([kn_ref[...]] * ATT_KV_HEADS, axis=1)
        put(qa_ref, t, (qa[t] * lax.rsqrt(q_ms[t] + EPS) * qn * (HEAD_DIM ** -0.5 * LOG2_E)).astype(BF16))
        k = kv[t][:, 0:ATT_KV] * lax.rsqrt(k_ms[t] + EPS) * kn
        v = kv[t][:, ATT_KV:2 * ATT_KV]
        for src, dst in ((k, ka_ref), (v, va_ref)):
            swapped = pltpu.roll(src, HEAD_DIM, axis=1)
            put(dst, t, jnp.where(low, src, swapped).astype(BF16), slice(0, LANES))
            put(dst, t, jnp.where(low, swapped, src).astype(BF16), slice(LANES, 2 * LANES))
        put(qg_ref, t, (qk_g[t][:, 0:GLA_K] * (GLA_DK ** -0.5)).astype(BF16))
        put(kg_ref, t, qk_g[t][:, GLA_K:2 * GLA_K].astype(BF16))
        put(vg_ref, t, vg[t].astype(BF16))
        put(og_ref, t, (og[t] * _sigmoid(og[t])).astype(BF16))
        log_sig = jnp.minimum(gate[t], 0.0) - jnp.log(1.0 + jnp.exp(-jnp.abs(gate[t])))
        put(la_ref, t, log_sig * (1.0 / GLA_NORMALIZER))


def _attn_stages(seqs, sinks_ref, q_ref, kc_ref, vc_ref, gatt_ref, o_ref, kp_ref, vp_ref):
    blk = WINDOW
    first = pl.program_id(1) == 0
    qi = lax.broadcasted_iota(jnp.int32, (blk, blk), 0)
    cj = lax.broadcasted_iota(jnp.int32, (blk, blk), 1)
    from_prev = cj > qi
    dist = (qi - cj + jnp.where(from_prev, blk, 0)).astype(F32)
    no_prev = jnp.where(jnp.logical_and(from_prev, first), -1e30, 0.0)
    low = cj < HEAD_DIM
    half = (jnp.where(low, 1.0, 0.0).astype(BF16), jnp.where(low, 0.0, 1.0).astype(BF16))
    half2 = tuple(jnp.concatenate([m, m], axis=0) for m in half)
    prev_mask = jnp.where(from_prev, 1.0, 0.0).astype(BF16)
    cur_mask = jnp.where(from_prev, 0.0, 1.0).astype(BF16)

    n_pairs = ATT_HEADS // 2
    pairs_per_kv = n_pairs // ATT_KV_HEADS
    units = [(bi, j) for bi in seqs for j in range(n_pairs)]

    def kv_blocks(bi, g):
        lanes = slice(g * LANES, (g + 1) * LANES)
        return (kp_ref[bi, :, lanes], vp_ref[bi, :, lanes]), (kc_ref[bi, :, lanes], vc_ref[bi, :, lanes])

    scores = {}
    for bi in seqs:
        for g in range(ATT_KV_HEADS):
            (kp, _), (kc, _) = kv_blocks(bi, g)
            k_both = jnp.concatenate([kp, kc], axis=0)
            group = range(g * pairs_per_kv, (g + 1) * pairs_per_kv)
            q_stack = jnp.concatenate([q_ref[bi, :, j * LANES:(j + 1) * LANES] for j in group], axis=0)
            for p in range(2):
                s_stack = _dot_nt(q_stack, k_both * half2[p])
                for jj, j in enumerate(group):
                    scores[bi, j, p] = s_stack[jj * blk:(jj + 1) * blk, :]
    yield

    bias = [2.0 ** (-8.0 * (h + 1) / ATT_HEADS) * LOG2_E * dist - no_prev for h in range(ATT_HEADS)]
    probs, sink_terms = {}, {}
    for bi, j in units:
        for p in range(2):
            h = 2 * j + p
            s_both = scores[bi, j, p]
            s = jnp.where(from_prev, s_both[:, 0:blk], s_both[:, blk:2 * blk]) - bias[h]
            sink = sinks_ref[h] * LOG2_E
            m = jnp.maximum(jnp.max(s, axis=-1, keepdims=True), sink)
            probs[bi, j, p] = jnp.exp2(s - m)
            sink_terms[bi, j, p] = jnp.exp2(sink - m)
    yield

    outs = {}
    for bi in seqs:
        for g in range(ATT_KV_HEADS):
            (_, vp), (_, vc) = kv_blocks(bi, g)
            v_stack = jnp.concatenate([jnp.concatenate([v * half[p], half[p]], axis=1)
                                       for v in (vp, vc) for p in range(2)], axis=0)
            group = range(g * pairs_per_kv, (g + 1) * pairs_per_kv)
            p_stack = []
            for j in group:
                e = [probs[bi, j, p].astype(BF16) for p in range(2)]
                p_stack.append(jnp.concatenate([x * m for m in (prev_mask, cur_mask) for x in e], axis=1))
            pv_stack = _dot(jnp.concatenate(p_stack, axis=0), v_stack)
            for jj, j in enumerate(group):
                pv = pv_stack[jj * blk:(jj + 1) * blk, :]
                den = pv[:, LANES:2 * LANES] + jnp.where(low, sink_terms[bi, j, 0], sink_terms[bi, j, 1])
                outs[bi, j] = pv[:, 0:LANES] / den
    yield

    for bi in seqs:
        o = jnp.concatenate([outs[bi, j] for j in range(n_pairs)], axis=1)
        ms = jnp.mean(o * o, axis=-1, keepdims=True)
        o_ref[bi] = (o * lax.rsqrt(ms + EPS) * gatt_ref[...]).astype(BF16)
        kp_ref[bi] = kc_ref[bi]
        vp_ref[bi] = vc_ref[bi]


def _gla_stages(seqs, q_ref, k_ref, v_ref, la_ref, og_ref, cum_ref, bdms_ref, smask_ref,
                ggla_ref, o_ref, state_ref):
    L = GLA_CHUNK
    lane = lax.broadcasted_iota(jnp.int32, (GLA_SUB, LANES), 1)
    head_mask = [jnp.where(lane // GLA_DK == hh, 1.0, 0.0).astype(BF16) for hh in range(2)]
    causal = lax.broadcasted_iota(jnp.int32, (L, L), 0) >= lax.broadcasted_iota(jnp.int32, (L, L), 1)
    units = [(s, pair) for s in seqs for pair in range(GLA_HEADS // 2)]
    kl = lambda pair: slice(pair * LANES, (pair + 1) * LANES)
    vl = lambda pair: slice(pair * 2 * GLA_DV, (pair + 1) * 2 * GLA_DV)

    b, b_in = {}, {}
    for s in seqs:
        la = la_ref[s]
        la_hi = la.astype(BF16)
        la_lo = (la - la_hi.astype(F32)).astype(BF16)
        sums = _dot(cum_ref[...], jnp.concatenate([la_hi, la_lo], axis=0)) * LOG2_E
        b[s] = sums[0:L, :]
        b_in[s] = sums[L:2 * L, :]
    yield

    q_both, keys, q_dec, k_dec_t, b_last = {}, {}, {}, {}, {}
    for s in seqs:
        ref = b[s] - b_in[s]
        b_last[s] = b[s][L - 1:L, :]
        q = q_ref[s].astype(F32)
        k = k_ref[s].astype(F32)
        q_in = (q * jnp.exp2(b_in[s])).astype(BF16)
        q_dec[s] = (q * jnp.exp2(b[s])).astype(BF16)
        k_dec = k * jnp.exp2(b_last[s] - b[s])
        for pair in range(GLA_HEADS // 2):
            k_p, b_p, ref_p = k[:, kl(pair)], b[s][:, kl(pair)], ref[:, kl(pair)]
            expanded = []
            for g in range(N_SUB):
                top = (g + 1) * GLA_SUB
                live = (k_p[0:top, :] * jnp.exp2(ref_p[g * GLA_SUB:g * GLA_SUB + 1, :] - b_p[0:top, :])).astype(BF16)
                expanded.append(live if top == L else
                                jnp.concatenate([live, jnp.zeros((L - top, LANES), BF16)], axis=0))
            keys[s, pair] = jnp.concatenate(expanded, axis=1)
            q_p = q_in[:, kl(pair)]
            zero_group = jnp.zeros((GLA_SUB, LANES), BF16)
            q_both[s, pair] = []
            for t in range(N_SUB // 2):
                lhs = []
                for hh in range(2):
                    for side in range(2):
                        g = 2 * t + side
                        piece = q_p[g * GLA_SUB:(g + 1) * GLA_SUB, :] * head_mask[hh]
                        lhs.append(jnp.concatenate([piece, zero_group] if side == 0 else [zero_group, piece], axis=1))
                q_both[s, pair].append(jnp.concatenate(lhs, axis=0))
            k_dec_t[s, pair] = k_dec[:, kl(pair)].T.astype(BF16)
    yield

    scores = {(u, t): _dot_nt(q_both[u][t], keys[u][:, 2 * t * LANES:2 * (t + 1) * LANES])
              for u in units for t in range(N_SUB // 2)}
    yield

    outs, updates = {}, {}
    for s, pair in units:
        v_p = v_ref[s, :, vl(pair)]
        o_parts = []
        for hh in range(2):
            a = jnp.concatenate([scores[(s, pair), t][2 * hh * GLA_SUB:2 * (hh + 1) * GLA_SUB, :]
                                 for t in range(N_SUB // 2)], axis=0)
            a = jnp.where(causal, a, 0.0).astype(BF16)
            o_parts.append(_dot(a, v_p[:, hh * GLA_DV:(hh + 1) * GLA_DV]))
        state = state_ref[s, pair]
        outs[s, pair] = jnp.concatenate(o_parts, axis=1) + _dot(q_dec[s][:, kl(pair)], state.astype(BF16))
        updates[s, pair] = _dot(k_dec_t[s, pair], v_p)
    yield

    for s, pair in units:
        decay = jnp.broadcast_to(jnp.exp2(b_last[s][:, kl(pair)]), (LANES, LANES)).T
        state_ref[s, pair] = (state_ref[s, pair] * jnp.concatenate([decay, decay], axis=1)
                              + updates[s, pair] * smask_ref[...])
        o = outs[s, pair]
        ms = _dot((o * o).astype(BF16), bdms_ref[...])
        gain = jnp.concatenate([ggla_ref[...]] * 2, axis=1)
        y = o * lax.rsqrt(ms + EPS) * gain * og_ref[s, :, vl(pair)].astype(F32)
        o_ref[s, :, vl(pair)] = y.astype(BF16)


def _run_interleaved(*stage_generators):
    pending = list(stage_generators)
    while pending:
        for stages in list(pending):
            if next(stages, "done") == "done":
                pending.remove(stages)


def _route(logits_t):
    lt = logits_t[0:ROUTER_ROWS, :]
    row = lax.broadcasted_iota(jnp.int32, lt.shape, 0)
    neg_inf = -jnp.inf
    g_log = jnp.where(row < N_GROUPS, lt, neg_inf)
    g_max = jnp.max(g_log, axis=0, keepdims=True)
    g_sel = jnp.min(jnp.where(g_log == g_max, row, LANES), axis=0, keepdims=True)
    g_sum = jnp.sum(jnp.where(row < N_GROUPS, jnp.exp(lt - g_max), 0.0), axis=0, keepdims=True)
    p_group = 1.0 / g_sum
    e_lo = ROUTER_LANE0 + EXPERTS_PER_GROUP * g_sel
    in_group = jnp.logical_and(row >= e_lo, row < e_lo + EXPERTS_PER_GROUP)
    e_log = jnp.where(in_group, lt, neg_inf)
    e_max = jnp.max(e_log, axis=0, keepdims=True)
    top1 = jnp.min(jnp.where(e_log == e_max, row, LANES), axis=0, keepdims=True)
    e_log2 = jnp.where(row == top1, neg_inf, e_log)
    e_max2 = jnp.max(e_log2, axis=0, keepdims=True)
    top2 = jnp.min(jnp.where(e_log2 == e_max2, row, LANES), axis=0, keepdims=True)
    ratio = jnp.exp(e_max2 - e_max)
    w_top1 = p_group / (1.0 + ratio)
    w_top2 = p_group * ratio / (1.0 + ratio)
    row8 = lax.broadcasted_iota(jnp.int32, (8, lt.shape[1]), 0)
    weights = jnp.where(row8 == top1 - e_lo, w_top1, 0.0) + jnp.where(row8 == top2 - e_lo, w_top2, 0.0)
    return g_sel, weights


def _outproj_stages(subs, ya_ref, yg_ref, x_ref, mod_ref, wo_ref, g2_ref, wrt_ref, brt_ref, striu_ref,
                    x1_ref, row_ref, lpos_ref, cnt_ref):
    tm = MOE_TILE
    half = x_ref.shape[1]
    per_tile = tm // half
    tile_of = lambda ref, t: jnp.concatenate([ref[t * per_tile + i] for i in range(per_tile)], axis=0)

    mix = {t: _dot(tile_of(ya_ref, t), wo_ref[0:ATT_Q, :]) + _dot(tile_of(yg_ref, t), wo_ref[ATT_Q:ATT_Q + GLA_V, :])
           for t in subs}
    yield
    h2b = {}
    for t in subs:
        parts = []
        for i in range(per_tile):
            s = t * per_tile + i
            x1 = x_ref[s] + mod_ref[2, s:s + 1, :] * mix[t][i * half:(i + 1) * half, :]
            x1_ref[t * tm + i * half:t * tm + (i + 1) * half, :] = x1
            ms = jnp.mean(x1 * x1, axis=-1, keepdims=True)
            h2 = (x1 * lax.rsqrt(ms + EPS) * g2_ref[...]) * (1.0 + mod_ref[4, s:s + 1, :]) + mod_ref[3, s:s + 1, :]
            parts.append(h2.astype(BF16))
        h2b[t] = jnp.concatenate(parts, axis=0)
    yield
    logits_t = {t: _dot_nt(wrt_ref[...], h2b[t]) + brt_ref[...] for t in subs}
    yield

    routed = {t: _route(logits_t[t]) for t in subs}
    row8 = lax.broadcasted_iota(jnp.int32, (8, tm), 0)
    onehot = {t: jnp.where(row8 == routed[t][0], 1.0, 0.0) for t in subs}
    yield
    before = {t: _dot(onehot[t].astype(BF16), striu_ref[...]) for t in subs}
    yield

    local_row = lax.broadcasted_iota(jnp.int32, (LOCAL_ROWS, tm), 0).astype(F32)
    pad_rows = jnp.zeros((LANES - 8, tm), F32)
    for t in subs:
        count = jnp.sum(onehot[t], axis=1, keepdims=True)
        cnt_ref[t] = jnp.broadcast_to(count, (8, LANES))
        padded = jnp.broadcast_to(jnp.floor((count + (CHUNK - 1.0)) * (1.0 / CHUNK)) * CHUNK, (8, tm))
        start = jnp.zeros((8, tm), F32)
        for shift in range(1, N_GROUPS):
            start = start + jnp.where(row8 >= shift, pltpu.roll(padded, shift, axis=0), 0.0)
        lpos = jnp.sum(onehot[t] * (before[t] + start), axis=0, keepdims=True)
        lpos_ref[t] = jnp.broadcast_to(lpos, (8, tm))
        weights = jnp.concatenate([routed[t][1], pad_rows], axis=0).T
        w_hi = weights.astype(BF16)
        w_lo = (weights - w_hi.astype(F32)).astype(BF16)
        perm = jnp.where(local_row == lpos, 1.0, 0.0).astype(BF16)
        row_ref[t * LOCAL_ROWS:(t + 1) * LOCAL_ROWS, :] = _dot(
            perm, jnp.concatenate([h2b[t], w_hi, w_lo], axis=1)).astype(BF16)


def _mix_out_kernel(sinks_ref, x_ref, mod_ref, g1_ref, wf_ref, qn_ref, kn_ref, bdq_ref, bdk_ref, wgk_ref, bgk_ref,
                    gatt_ref, cum_ref, bdms_ref, smask_ref, ggla_ref,
                    wof_ref, g2_ref, wrt_ref, brt_ref, striu_ref,
                    x1_ref, row_ref, lpos_ref, cnt_ref,
                    w_ref, wo_ref, qa_ref, ka_ref, va_ref, qg_ref, kg_ref, vg_ref, la_ref, og_ref,
                    ya_ref, yg_ref, kp_ref, vp_ref, state_ref, w_sem):
    assert x1_ref.shape == wof_ref.shape and x1_ref.dtype == wof_ref.dtype
    _round_w_in(wf_ref, w_ref, wof_ref, wo_ref, x1_ref, w_sem)
    n_seq = x_ref.shape[0]
    seqs_per_tile = MOE_TILE // x_ref.shape[1]
    halves = []
    for first in (0, n_seq // 2):
        seqs = tuple(range(first, first + n_seq // 2))
        halves.append((seqs, (first * len(PROJ_SPLITS) // n_seq,),
                       tuple(range(first // seqs_per_tile, (first + n_seq // 2) // seqs_per_tile))))

    def inproj(sub_tiles):
        return _inproj_stages(sub_tiles, x_ref, mod_ref, g1_ref, qn_ref, kn_ref, bdq_ref, bdk_ref, wgk_ref, bgk_ref,
                              qa_ref, ka_ref, va_ref, qg_ref, kg_ref, vg_ref, la_ref, og_ref, w_ref)

    def mixers(seqs):
        return (_attn_stages(seqs, sinks_ref, qa_ref, ka_ref, va_ref, gatt_ref, ya_ref, kp_ref, vp_ref),
                _gla_stages(seqs, qg_ref, kg_ref, vg_ref, la_ref, og_ref, cum_ref, bdms_ref, smask_ref,
                            ggla_ref, yg_ref, state_ref))

    def outproj(tiles):
        return _outproj_stages(tiles, ya_ref, yg_ref, x_ref, mod_ref, wo_ref, g2_ref, wrt_ref, brt_ref, striu_ref,
                               x1_ref, row_ref, lpos_ref, cnt_ref)

    (seqs_a, subs_a, tiles_a), (seqs_b, subs_b, tiles_b) = halves
    _run_interleaved(inproj(subs_a + subs_b))

    @pl.when(pl.program_id(1) == 0)
    def _():
        state_ref[...] = jnp.zeros_like(state_ref)
        kp_ref[...] = jnp.zeros_like(kp_ref)
        vp_ref[...] = jnp.zeros_like(vp_ref)

    first_half = mixers(seqs_a)
    for stages in first_half:
        next(stages)
    _run_interleaved(*first_half, *mixers(seqs_b))
    _run_interleaved(outproj(tiles_a + tiles_b))


def _mix_out(sinks, x, mod, g1, w_in_t, qn, kn, bdq, bdk, wgk, bgk, gatt, cum, bdms, smask, ggla,
             wo, g2, wr, br, striu):
    assert GLA_CHUNK == WINDOW and MOE_TILE % WINDOW == 0
    B, T, D = x.shape
    L = WINDOW
    N = B * T
    assert B * L == PROJ_TILE
    tiles_per_step = B * L // MOE_TILE
    const = lambda a: pl.BlockSpec(a.shape, lambda b, i: (0,) * a.ndim)
    tiles = lambda *shape: pl.BlockSpec((tiles_per_step,) + shape, lambda b, i: (i,) + (0,) * len(shape))
    act = lambda c, dt=BF16: pltpu.VMEM((B, L, c), dt)
    return pl.pallas_call(
        _mix_out_kernel,
        grid=(1, T // L),
        in_specs=[pl.BlockSpec(memory_space=pltpu.SMEM),
                  pl.BlockSpec((B, L, D), lambda b, i: (0, i, 0)), const(mod), const(g1),
                  pl.BlockSpec(w_in_t.shape, lambda b, i: (0, 0), pipeline_mode=pl.Buffered(1)),
                  const(qn), const(kn), const(bdq), const(bdk), const(wgk), const(bgk),
                  const(gatt), const(cum), const(bdms), const(smask), const(ggla),
                  pl.BlockSpec(memory_space=pl.ANY), const(g2), const(wr), const(br), const(striu)],
        out_specs=[pl.BlockSpec((B * L, D), lambda b, i: (i, 0)),
                   pl.BlockSpec((tiles_per_step * LOCAL_ROWS, D + 2 * LANES), lambda b, i: (i, 0)),
                   tiles(8, MOE_TILE), tiles(8, LANES)],
        out_shape=[jax.ShapeDtypeStruct((N, D), F32),
                   jax.ShapeDtypeStruct((N // MOE_TILE * LOCAL_ROWS, D + 2 * LANES), BF16),
                   jax.ShapeDtypeStruct((N // MOE_TILE, 8, MOE_TILE), F32),
                   jax.ShapeDtypeStruct((N // MOE_TILE, 8, LANES), F32)],
        scratch_shapes=[pltpu.VMEM((IN_COLS_PAD, D), BF16), pltpu.VMEM(wo.shape, BF16),
                        act(ATT_Q), act(2 * ATT_KV), act(2 * ATT_KV), act(GLA_K), act(GLA_K), act(GLA_V),
                        act(GLA_K, F32), act(GLA_V),
                        act(ATT_Q), act(GLA_V), act(2 * ATT_KV), act(2 * ATT_KV),
                        pltpu.VMEM((B, GLA_HEADS // 2, LANES, 2 * GLA_DV), F32),
                        pltpu.SemaphoreType.DMA(())],
        compiler_params=_params("arbitrary", "arbitrary"),
        name="mix_out",
    )(sinks, x, mod, g1, w_in_t, qn, kn, bdq, bdk, wgk, bgk, gatt, cum, bdms, smask, ggla,
      wo, g2, wr, br, striu)


def _chunk_copy(src_ref, src_chunk, dst_ref, dst_chunk, sem):
    return pltpu.make_async_copy(src_ref.at[src_chunk], dst_ref.at[dst_chunk], sem)


def _moe_kernel(src_ref, dst_ref, grp_ref, next_ref, nt_ref, used_ref, rows_ref, w1f_ref, w3f_ref, w2f_ref, y_ref,
                in_buf, out_buf, zero_buf, st1_ref, st3_ref, st2_ref, w1_ref, w3_ref, w2_ref,
                in_sem, out_sem, zero_sem, w_sem):
    j = pl.program_id(0)
    n_tiles = nt_ref[0]
    d_model = w2_ref.shape[2]

    stages = ((w1f_ref, st1_ref, w1_ref), (w3f_ref, st3_ref, w3_ref), (w2f_ref, st2_ref, w2_ref))

    def fetch_weights(group):
        for hbm, stage, _ in stages:
            pltpu.make_async_copy(hbm.at[group], stage, w_sem).start(priority=1)

    def enter_group():
        for hbm, stage, dst in stages:
            pltpu.make_async_copy(hbm.at[0], stage, w_sem).wait()
        for hbm, stage, dst in stages:
            for k in range(EXPERTS_PER_GROUP):
                dst[k] = stage[k].astype(BF16)

        @pl.when(next_ref[j] != grp_ref[j])
        def _():
            fetch_weights(next_ref[j])

    def gather(tile, slot):
        for k in range(TILE_CHUNKS):
            _chunk_copy(rows_ref, src_ref[tile * TILE_CHUNKS + k], in_buf.at[slot], k,
                        in_sem.at[slot]).start(priority=k % 2)

    def wait_gather(slot):
        def body(k, carry):
            _chunk_copy(rows_ref, 0, in_buf.at[slot], k, in_sem.at[slot]).wait()
            return carry
        lax.fori_loop(0, TILE_CHUNKS, body, 0, unroll=True)

    def scatter(tile, slot):
        for k in range(TILE_CHUNKS):
            _chunk_copy(out_buf.at[slot], k, y_ref, dst_ref[tile * TILE_CHUNKS + k],
                        out_sem.at[slot]).start(priority=k % 2)

    def wait_scatter(slot):
        def body(k, carry):
            _chunk_copy(out_buf.at[slot], k, y_ref, 0, out_sem.at[slot]).wait()
            return carry
        lax.fori_loop(0, TILE_CHUNKS, body, 0, unroll=True)

    def zero_fill(wait):
        def per_tile(i, carry):
            def body(c, inner):
                copy = _chunk_copy(zero_buf, 0, y_ref, i * LOCAL_CHUNKS + c, zero_sem)
                if wait:
                    copy.wait()
                else:
                    copy.start()
                return inner
            return lax.fori_loop(used_ref[i], LOCAL_CHUNKS, body, carry)
        lax.fori_loop(0, used_ref.shape[0], per_tile, 0)

    @pl.when(j == 0)
    def _():
        fetch_weights(grp_ref[0])
        zero_buf[...] = jnp.zeros_like(zero_buf)
        scratch0 = used_ref.shape[0] * LOCAL_CHUNKS
        for wait in (False, True):
            for k in range(2 * TILE_CHUNKS):
                copy = _chunk_copy(zero_buf, 0, y_ref, scratch0 + k, zero_sem)
                copy.wait() if wait else copy.start()
        zero_fill(wait=False)
        gather(0, 0)

    @pl.when(jnp.logical_or(j == 0, grp_ref[j] != grp_ref[jnp.maximum(j - 1, 0)]))
    def _():
        enter_group()

    @pl.when(j + 1 < n_tiles)
    def _():
        gather(j + 1, (j + 1) % 2)

    @pl.when(j < n_tiles)
    def _():
        slot = j % 2
        wait_gather(slot)
        rows = in_buf[slot].reshape(EXPERT_TILE, in_buf.shape[-1])
        h = rows[:, 0:d_model]
        weights = rows[:, d_model:d_model + LANES].astype(F32) + rows[:, d_model + LANES:].astype(F32)
        experts = range(EXPERTS_PER_GROUP)
        up = [(_dot(h, w1_ref[k]), _dot(h, w3_ref[k])) for k in experts]
        hid = [(a * _sigmoid(a) * g * weights[:, k:k + 1]).astype(BF16) for k, (a, g) in zip(experts, up)]
        y = _dot(hid[0], w2_ref[0])
        for k in experts[1:]:
            y = y + _dot(hid[k], w2_ref[k])

        @pl.when(j >= 2)
        def _():
            wait_scatter(slot)

        out_buf[slot] = y.astype(BF16).reshape(TILE_CHUNKS, CHUNK, d_model)
        scatter(j, slot)

        @pl.when(j == n_tiles - 1)
        def _():
            @pl.when(j >= 1)
            def _():
                wait_scatter(1 - slot)
            wait_scatter(slot)
            zero_fill(wait=True)


def _combine_kernel(x1_ref, mod_ref, lpos_ref, y_ref, o_ref):
    tm = MOE_TILE
    half = WINDOW
    per_tile = tm // half
    tiles_per_block = o_ref.shape[0] // per_tile
    local_row = lax.broadcasted_iota(jnp.int32, (tm, LOCAL_ROWS), 1).astype(F32)
    for t in range(x1_ref.shape[0] // tm):
        lpos = jnp.broadcast_to(lpos_ref[t][0:1, :], (LANES, tm)).T[:, 0:1]
        unsort = jnp.where(local_row == lpos, 1.0, 0.0).astype(BF16)
        y = _dot(unsort, y_ref[t * LOCAL_ROWS:(t + 1) * LOCAL_ROWS, :])
        block, tile = divmod(t, tiles_per_block)
        for i in range(per_tile):
            s = tile * per_tile + i
            rows = slice(t * tm + i * half, t * tm + (i + 1) * half)
            o_ref[s, block * half:(block + 1) * half, :] = (
                x1_ref[rows, :] + mod_ref[5, s:s + 1, :] * y[i * half:(i + 1) * half, :])


def _moe_plan(cnt):
    n_local = cnt.shape[0]
    chunks = (cnt + CHUNK - 1) // CHUNK
    used = jnp.sum(chunks, axis=1)
    local_off = jnp.cumsum(chunks, axis=1) - chunks
    tiles_g = (jnp.sum(chunks, axis=0) + TILE_CHUNKS - 1) // TILE_CHUNKS
    tile_end = jnp.cumsum(tiles_g)
    n_tiles = tile_end[-1]
    group_start = (tile_end - tiles_g) * TILE_CHUNKS
    seg_len = chunks.T.reshape(-1)
    seg_start = (group_start[:, None] + (jnp.cumsum(chunks, axis=0) - chunks).T).reshape(-1)
    seg_src = (jnp.arange(n_local)[None, :] * LOCAL_CHUNKS + local_off.T).reshape(-1)
    max_chunks = n_local * MOE_TILE // CHUNK + n_local * N_GROUPS + N_GROUPS * TILE_CHUNKS
    max_tiles = (max_chunks + TILE_CHUNKS - 1) // TILE_CHUNKS
    c = jnp.arange(max_tiles * TILE_CHUNKS)[:, None]
    within = c - seg_start[None, :]
    hit = jnp.logical_and(within >= 0, within < seg_len[None, :])
    valid = jnp.any(hit, axis=1)
    src = jnp.sum(jnp.where(hit, seg_src[None, :] + within, 0), axis=1)
    src = jnp.where(valid, src, LOCAL_CHUNKS - 1)
    slot_k = c[:, 0] % (2 * TILE_CHUNKS)
    dst = jnp.where(valid, src, n_local * LOCAL_CHUNKS + slot_k)
    j = jnp.minimum(jnp.arange(max_tiles), n_tiles - 1)
    grp = jnp.sum(j[:, None] >= tile_end[None, :], axis=1)
    gid = jnp.arange(N_GROUPS)
    later = jnp.where(jnp.logical_and(gid[None, :] > gid[:, None], tiles_g[None, :] > 0), gid[None, :], N_GROUPS)
    next_of = jnp.min(later, axis=1)
    next_of = jnp.where(next_of == N_GROUPS, gid, next_of)
    next_grp = jnp.sum(jnp.where(grp[:, None] == gid[None, :], next_of[None, :], 0), axis=1)
    i32 = lambda a: a.astype(jnp.int32)
    return i32(src), i32(dst), i32(grp), i32(next_grp), i32(n_tiles).reshape(1), i32(used)


def _moe(plan, rows_local, w1g, w3g, w2g):
    src, dst, grp, next_grp, n_tiles, used = plan
    D = w2g.shape[3]
    n_rows, cols = rows_local.shape
    n_chunks = n_rows // CHUNK
    hbm = pl.BlockSpec(memory_space=pl.ANY)
    group_weights = [w1g, w3g, w2g]
    y = pl.pallas_call(
        _moe_kernel,
        grid_spec=pltpu.PrefetchScalarGridSpec(
            num_scalar_prefetch=6,
            grid=(grp.shape[0],),
            in_specs=[hbm, hbm, hbm, hbm],
            out_specs=hbm,
            scratch_shapes=([pltpu.VMEM((2, TILE_CHUNKS, CHUNK, cols), BF16),
                             pltpu.VMEM((2, TILE_CHUNKS, CHUNK, D), BF16),
                             pltpu.VMEM((1, CHUNK, D), BF16)]
                            + [pltpu.VMEM(w.shape[1:], F32) for w in group_weights]
                            + [pltpu.VMEM(w.shape[1:], BF16) for w in group_weights]
                            + [pltpu.SemaphoreType.DMA((2,)), pltpu.SemaphoreType.DMA((2,)),
                               pltpu.SemaphoreType.DMA(()), pltpu.SemaphoreType.DMA(())])),
        out_shape=jax.ShapeDtypeStruct((n_chunks + 2 * TILE_CHUNKS, CHUNK, D), BF16),
        compiler_params=_params("arbitrary"),
        name="moe",
    )(src, dst, grp, next_grp, n_tiles, used, rows_local.reshape(n_chunks, CHUNK, cols), w1g, w3g, w2g)
    return y.reshape((n_chunks + 2 * TILE_CHUNKS) * CHUNK, D)


def _combine(x1, mod, lpos, y_local, *, B, T):
    N, D = x1.shape
    L = COMBINE_BLOCKS * WINDOW
    subs = B * L // MOE_TILE
    return pl.pallas_call(
        _combine_kernel,
        grid=(T // L,),
        in_specs=[pl.BlockSpec((B * L, D), lambda i: (i, 0)),
                  pl.BlockSpec(mod.shape, lambda i: (0, 0, 0)),
                  pl.BlockSpec((subs, 8, MOE_TILE), lambda i: (i, 0, 0)),
                  pl.BlockSpec((subs * LOCAL_ROWS, D), lambda i: (i, 0))],
        out_specs=pl.BlockSpec((B, L, D), lambda i: (0, i, 0)),
        out_shape=jax.ShapeDtypeStruct((B, T, D), F32),
        compiler_params=_params("arbitrary"),
        name="moe_combine",
    )(x1, mod, lpos, y_local)


def _block_diag(n, blk, value, dtype):
    r = np.arange(n)[:, None] // blk
    c = np.arange(n)[None, :] // blk
    return jnp.asarray(np.where(r == c, value, 0.0), dtype)


def _gla_constants():
    L = GLA_CHUNK
    i = np.arange(L)[:, None]
    j = np.arange(L)[None, :]
    tri = j <= i
    bdtri = np.logical_and(j <= i, i // GLA_SUB == j // GLA_SUB)
    cum = np.block([[tri, tri], [bdtri, bdtri]])
    bdms = _block_diag(2 * GLA_DV, GLA_DV, 1.0 / GLA_DV, BF16)
    d = np.arange(LANES)[:, None] // GLA_DK
    e = np.arange(2 * GLA_DV)[None, :] // GLA_DV
    smask = d == e
    return jnp.asarray(cum, BF16), bdms, jnp.asarray(smask, F32)


def kernel(x, c, w_ada, b_ada, g_norm1, w_in, q_norm, k_norm, sinks, w_gk2, b_gk, g_gla_out, g_att_out,
           w_out, g_norm2, w_group, b_group, w_router, b_router, w1, w3, w2):
    B, T, D = x.shape
    N = B * T
    depth = w_ada.shape[0]
    cum, bdms, smask = _gla_constants()
    bdq = _block_diag(2 * LANES, HEAD_DIM, 1.0 / HEAD_DIM, BF16)
    bdk = _block_diag(ATT_KV, HEAD_DIM, 1.0 / HEAD_DIM, BF16)

    for l in range(depth):
        mod = _adaln_mod(c, w_ada[l], b_ada[l])

        pad = LANES - N_GROUPS - N_EXPERTS
        wr_t = jnp.concatenate([w_group[l], w_router[l], jnp.zeros((D, pad), F32)], axis=1).T.astype(BF16)
        br_t = jnp.concatenate([b_group[l], b_router[l], jnp.zeros((pad,), F32)]).reshape(LANES, 1)
        striu = jnp.asarray(np.arange(MOE_TILE)[:, None] < np.arange(MOE_TILE)[None, :], BF16)
        x1, rows_local, lpos, cnt = _mix_out(
            sinks[l], x, mod, g_norm1[l].reshape(1, D), w_in[l].T,
            q_norm[l].reshape(1, HEAD_DIM), k_norm[l].reshape(1, HEAD_DIM),
            bdq, bdk, w_gk2[l], b_gk[l].reshape(1, GLA_K),
            g_att_out[l].reshape(1, ATT_Q), cum, bdms, smask, g_gla_out[l].reshape(1, GLA_DV),
            w_out[l], g_norm2[l].reshape(1, D), wr_t, br_t, striu)
        plan = _moe_plan(cnt[:, :N_GROUPS, 0].astype(jnp.int32))
        by_group = lambda w: w.reshape((N_GROUPS, EXPERTS_PER_GROUP) + w.shape[1:])
        y_local = _moe(plan, rows_local, by_group(w1[l]), by_group(w3[l]), by_group(w2[l]))
        x = _combine(x1, mod, lpos, y_local, B=B, T=T)
    return x
```
